```python
import math
import jax, jax.numpy as jnp
from jax import lax
import numpy as np

D_MODEL = 1024
BATCH = 8
SEQ = 16384
DEPTH = 2

CONV_DIM = 1024
CONV_KERNEL = 31
D_INNER = 2 * D_MODEL
HEAD_DIM = 64
N_SSM_HEADS = D_INNER // HEAD_DIM
N_GROUPS = 8
D_STATE = 128
SSM_CONV = 5
CHUNK = 128
XBC_DIM = D_INNER + 2 * N_GROUPS * D_STATE
FFN_DIM = int(math.ceil(D_MODEL * 8 / 3 / 256) * 256)
PLE_DIM = 256
N_IN = 2 * CONV_DIM + 2 * D_MODEL + D_INNER + XBC_DIM + 2 * N_SSM_HEADS
DEEPNORM_ALPHA = (2 * DEPTH) ** 0.25
DEEPNORM_BETA = (8 * DEPTH) ** -0.25
LN_EPS = 1e-5
RMS_EPS = 1e-6

kernel_name = "hybrid_conformer_ssd_encoder"


def layer_norm(x, g, b):
    xf = x.astype(jnp.float32)
    mu = jnp.mean(xf, axis=-1, keepdims=True)
    var = jnp.mean(jnp.square(xf - mu), axis=-1, keepdims=True)
    return ((xf - mu) * lax.rsqrt(var + LN_EPS) * g + b).astype(x.dtype)


def rms_norm(x, g):
    xf = x.astype(jnp.float32)
    return (xf * lax.rsqrt(jnp.mean(jnp.square(xf), axis=-1, keepdims=True) + RMS_EPS) * g).astype(x.dtype)


def depthwise_conv_centred(u, w, b):
    pad = (w.shape[0] - 1) // 2
    out = lax.conv_general_dilated(
        u, w[:, None, :].astype(u.dtype), window_strides=(1,), padding=[(pad, pad)],
        dimension_numbers=("NWC", "WIO", "NWC"), feature_group_count=u.shape[-1])
    return out + b


def ssd_chunked(x, dt, A, Bm, Cm):
    b, s, h, p = x.shape
    g, n = Bm.shape[-2:]
    r = h // g
    c, l = s // CHUNK, CHUNK
    x = x.astype(jnp.float32)
    dt = dt.astype(jnp.float32)
    X = (x * dt[..., None]).reshape(b, c, l, g, r, p)
    Ad = (dt * A).reshape(b, c, l, g, r).transpose(0, 1, 3, 4, 2)
    Bc = Bm.astype(jnp.float32).reshape(b, c, l, g, n)
    Cc = Cm.astype(jnp.float32).reshape(b, c, l, g, n)
    A_cs = jnp.cumsum(Ad, axis=-1)
    idx = jnp.arange(l)
    lower = idx[:, None] >= idx[None, :]
    seg = A_cs[..., :, None] - A_cs[..., None, :]
    Lmat = jnp.exp(jnp.where(lower, seg, -jnp.inf))
    CB = jnp.einsum("bclgn,bcsgn->bcgls", Cc, Bc)
    y_diag = jnp.einsum("bcgrls,bcsgrp->bclgrp", CB[:, :, :, None] * Lmat, X)
    decay_states = jnp.exp(A_cs[..., -1:] - A_cs).transpose(0, 1, 4, 2, 3)
    states = jnp.einsum("bclgn,bclgrp->bcgrpn", Bc, X * decay_states[..., None])
    chunk_decay = jnp.exp(A_cs[..., -1])

    def step(carry, inp):
        st, dec = inp
        return carry * dec[..., None, None] + st, carry

    h0 = jnp.zeros((b, g, r, p, n), jnp.float32)
    _, prev = lax.scan(step, h0, (jnp.moveaxis(states, 1, 0), jnp.moveaxis(chunk_decay, 1, 0)))
    prev = jnp.moveaxis(prev, 0, 1)
    decay_out = jnp.exp(A_cs).transpose(0, 1, 4, 2, 3)
    y_off = jnp.einsum("bclgn,bcgrpn->bclgrp", Cc, prev) * decay_out[..., None]
    return (y_diag + y_off).reshape(b, s, h, p)


def conformer_branch(glu_in, conv_w, conv_b, ln_g, ln_b, w_out):
    a, gt = jnp.split(glu_in, 2, axis=-1)
    u = a * jax.nn.sigmoid(gt)
    u = depthwise_conv_centred(u, conv_w, conv_b)
    u = jax.nn.silu(layer_norm(u, ln_g, ln_b))
    return u @ w_out


def mamba2_bidir_branch(z, xbc, dt_raw, conv_w, conv_b, a_log, dt_bias, d_skip, norm_g, w_out):
    b, s, _ = z.shape
    xbc = jax.nn.silu(depthwise_conv_centred(xbc, conv_w, conv_b))
    xs, Bm, Cm = jnp.split(xbc, [D_INNER, D_INNER + N_GROUPS * D_STATE], axis=-1)
    xs = xs.reshape(b, s, N_SSM_HEADS, HEAD_DIM)
    Bm = Bm.reshape(b, s, N_GROUPS, D_STATE)
    Cm = Cm.reshape(b, s, N_GROUPS, D_STATE)
    dt_raw = dt_raw.astype(jnp.float32)
    dt_f = jax.nn.softplus(dt_raw[..., :N_SSM_HEADS] + dt_bias[0])
    dt_b = jax.nn.softplus(dt_raw[..., N_SSM_HEADS:] + dt_bias[1])
    A = -jnp.exp(a_log.astype(jnp.float32))
    y_f = ssd_chunked(xs, dt_f, A[0], Bm, Cm)
    flip = lambda t: jnp.flip(t, axis=1)
    y_b = flip(ssd_chunked(flip(xs), flip(dt_b), A[1], flip(Bm), flip(Cm)))
    y = y_f + y_b + xs.astype(jnp.float32) * d_skip[:, None]
    y = y.reshape(b, s, D_INNER) * jax.nn.silu(z.astype(jnp.float32))
    yg = y.reshape(b, s, N_GROUPS, D_INNER // N_GROUPS)
    yg = yg * lax.rsqrt(jnp.mean(jnp.square(yg), axis=-1, keepdims=True) + RMS_EPS)
    y = (yg.reshape(b, s, D_INNER) * norm_g).astype(z.dtype)
    return y @ w_out


def _fwd_setup_inputs(seed: int = 0) -> dict:
    key = jax.random.key(seed)
    ks = iter(jax.random.split(key, 32))

    def nrm(shape, scale):
        return jax.random.normal(next(ks), shape, jnp.float32) * scale

    L = DEPTH
    dt0 = jnp.exp(jax.random.uniform(next(ks), (L, 2, N_SSM_HEADS)) * (math.log(0.1) - math.log(1e-3)) + math.log(1e-3))
    dt_bias = dt0 + jnp.log(-jnp.expm1(-dt0))
    a_log = jnp.log(jax.random.uniform(next(ks), (L, 2, N_SSM_HEADS), minval=1.0, maxval=16.0))
    return {
        "x": nrm((BATCH, SEQ, D_MODEL), 1.0),
        "p": nrm((DEPTH, BATCH, SEQ, PLE_DIM), 1.0),
        "w_in": nrm((L, D_MODEL, N_IN), D_MODEL ** -0.5),
        "conv_a_w": nrm((L, CONV_KERNEL, CONV_DIM), CONV_KERNEL ** -0.5),
        "conv_a_b": nrm((L, CONV_DIM), 0.02),
        "ln_a_g": 1.0 + nrm((L, CONV_DIM), 0.02),
        "ln_a_b": nrm((L, CONV_DIM), 0.02),
        "w_a_out": nrm((L, CONV_DIM, D_MODEL), CONV_DIM ** -0.5 * DEEPNORM_BETA),
        "ssm_conv_w": nrm((L, SSM_CONV, XBC_DIM), SSM_CONV ** -0.5),
        "ssm_conv_b": nrm((L, XBC_DIM), 0.02),
        "a_log": a_log,
        "dt_bias": dt_bias,
        "d_skip": 1.0 + nrm((L, N_SSM_HEADS), 0.02),
        "ssm_norm_g": 1.0 + nrm((L, D_INNER), 0.02),
        "w_b_out": nrm((L, D_INNER, D_MODEL), D_INNER ** -0.5 * DEEPNORM_BETA),
        "w_o": nrm((L, D_MODEL, D_MODEL), D_MODEL ** -0.5 * DEEPNORM_BETA),
        "ln1_g": 1.0 + nrm((L, D_MODEL), 0.02),
        "ln1_b": nrm((L, D_MODEL), 0.02),
        "w_gate_up": nrm((L, D_MODEL, 2 * FFN_DIM), D_MODEL ** -0.5),
        "w_down": nrm((L, FFN_DIM, D_MODEL), FFN_DIM ** -0.5 * DEEPNORM_BETA),
        "ln2_g": 1.0 + nrm((L, D_MODEL), 0.02),
        "ln2_b": nrm((L, D_MODEL), 0.02),
        "w_ple": nrm((L, PLE_DIM, D_MODEL), PLE_DIM ** -0.5 * DEEPNORM_BETA),
        "ple_norm_g": 1.0 + nrm((L, D_MODEL), 0.02),
        "w_ple_gate": nrm((L, D_MODEL, D_MODEL), D_MODEL ** -0.5),
    }


def _fwd_reference(x, p, w_in, conv_a_w, conv_a_b, ln_a_g, ln_a_b, w_a_out, ssm_conv_w, ssm_conv_b,
              a_log, dt_bias, d_skip, ssm_norm_g, w_b_out, w_o, ln1_g, ln1_b, w_gate_up, w_down,
              ln2_g, ln2_b, w_ple, ple_norm_g, w_ple_gate):
    cuts = [2 * CONV_DIM, 2 * CONV_DIM + 2 * D_MODEL, 2 * CONV_DIM + 2 * D_MODEL + D_INNER,
            2 * CONV_DIM + 2 * D_MODEL + D_INNER + XBC_DIM]
    for i in range(DEPTH):
        proj = x @ w_in[i]
        glu_in, gates, z, xbc, dt_raw = jnp.split(proj, cuts, axis=-1)
        gate_a, gate_b = jnp.split(gates, 2, axis=-1)
        y_a = conformer_branch(glu_in, conv_a_w[i], conv_a_b[i], ln_a_g[i], ln_a_b[i], w_a_out[i])
        y_b = mamba2_bidir_branch(z, xbc, dt_raw, ssm_conv_w[i], ssm_conv_b[i], a_log[i],
                                  dt_bias[i], d_skip[i], ssm_norm_g[i], w_b_out[i])
        merged = jax.nn.sigmoid(gate_a) * y_a + jax.nn.sigmoid(gate_b) * y_b
        h = layer_norm(DEEPNORM_ALPHA * x + merged @ w_o[i], ln1_g[i], ln1_b[i])
        g_, u_ = jnp.split(h @ w_gate_up[i], 2, axis=-1)
        h2 = layer_norm(DEEPNORM_ALPHA * h + (jax.nn.silu(g_) * u_) @ w_down[i], ln2_g[i], ln2_b[i])
        e = rms_norm(p[i] @ w_ple[i], ple_norm_g[i])
        x = h2 + e * jax.nn.sigmoid(h2 @ w_ple_gate[i])
    return x


import jax as _jax
import jax.numpy as _jnp

TWIN_FORMAT = 'train_step'
FWD_PARAMS = ['x', 'p', 'w_in', 'conv_a_w', 'conv_a_b', 'ln_a_g', 'ln_a_b', 'w_a_out', 'ssm_conv_w', 'ssm_conv_b', 'a_log', 'dt_bias', 'd_skip', 'ssm_norm_g', 'w_b_out', 'w_o', 'ln1_g', 'ln1_b', 'w_gate_up', 'w_down', 'ln2_g', 'ln2_b', 'w_ple', 'ple_norm_g', 'w_ple_gate']
TWIN_WEIGHTS = ['w_in', 'conv_a_w', 'conv_a_b', 'ln_a_g', 'ln_a_b', 'w_a_out', 'ssm_conv_w', 'ssm_conv_b', 'a_log', 'dt_bias', 'd_skip', 'ssm_norm_g', 'w_b_out', 'w_o', 'ln1_g', 'ln1_b', 'w_gate_up', 'w_down', 'ln2_g', 'ln2_b', 'w_ple', 'ple_norm_g', 'w_ple_gate']
TWIN_DIFF_INPUT = 'x'
TWIN_INPUTS = ['x', 'p', 'w_in', 'conv_a_w', 'conv_a_b', 'ln_a_g', 'ln_a_b', 'w_a_out', 'ssm_conv_w', 'ssm_conv_b', 'a_log', 'dt_bias', 'd_skip', 'ssm_norm_g', 'w_b_out', 'w_o', 'ln1_g', 'ln1_b', 'w_gate_up', 'w_down', 'ln2_g', 'ln2_b', 'w_ple', 'ple_norm_g', 'w_ple_gate', 'loss_target', 'm_w_in', 'm_conv_a_w', 'm_conv_a_b', 'm_ln_a_g', 'm_ln_a_b', 'm_w_a_out', 'm_ssm_conv_w', 'm_ssm_conv_b', 'm_a_log', 'm_dt_bias', 'm_d_skip', 'm_ssm_norm_g', 'm_w_b_out', 'm_w_o', 'm_ln1_g', 'm_ln1_b', 'm_w_gate_up', 'm_w_down', 'm_ln2_g', 'm_ln2_b', 'm_w_ple', 'm_ple_norm_g', 'm_w_ple_gate', 'v_w_in', 'v_conv_a_w', 'v_conv_a_b', 'v_ln_a_g', 'v_ln_a_b', 'v_w_a_out', 'v_ssm_conv_w', 'v_ssm_conv_b', 'v_a_log', 'v_dt_bias', 'v_d_skip', 'v_ssm_norm_g', 'v_w_b_out', 'v_w_o', 'v_ln1_g', 'v_ln1_b', 'v_w_gate_up', 'v_w_down', 'v_ln2_g', 'v_ln2_b', 'v_w_ple', 'v_ple_norm_g', 'v_w_ple_gate']
TWIN_OUTPUTS = ['loss', 'grad_x', 'grad_w_in', 'grad_conv_a_w', 'grad_conv_a_b', 'grad_ln_a_g', 'grad_ln_a_b', 'grad_w_a_out', 'grad_ssm_conv_w', 'grad_ssm_conv_b', 'grad_a_log', 'grad_dt_bias', 'grad_d_skip', 'grad_ssm_norm_g', 'grad_w_b_out', 'grad_w_o', 'grad_ln1_g', 'grad_ln1_b', 'grad_w_gate_up', 'grad_w_down', 'grad_ln2_g', 'grad_ln2_b', 'grad_w_ple', 'grad_ple_norm_g', 'grad_w_ple_gate', 'delta_w_in', 'delta_conv_a_w', 'delta_conv_a_b', 'delta_ln_a_g', 'delta_ln_a_b', 'delta_w_a_out', 'delta_ssm_conv_w', 'delta_ssm_conv_b', 'delta_a_log', 'delta_dt_bias', 'delta_d_skip', 'delta_ssm_norm_g', 'delta_w_b_out', 'delta_w_o', 'delta_ln1_g', 'delta_ln1_b', 'delta_w_gate_up', 'delta_w_down', 'delta_ln2_g', 'delta_ln2_b', 'delta_w_ple', 'delta_ple_norm_g', 'delta_w_ple_gate', 'new_m_w_in', 'new_m_conv_a_w', 'new_m_conv_a_b', 'new_m_ln_a_g', 'new_m_ln_a_b', 'new_m_w_a_out', 'new_m_ssm_conv_w', 'new_m_ssm_conv_b', 'new_m_a_log', 'new_m_dt_bias', 'new_m_d_skip', 'new_m_ssm_norm_g', 'new_m_w_b_out', 'new_m_w_o', 'new_m_ln1_g', 'new_m_ln1_b', 'new_m_w_gate_up', 'new_m_w_down', 'new_m_ln2_g', 'new_m_ln2_b', 'new_m_w_ple', 'new_m_ple_norm_g', 'new_m_w_ple_gate', 'new_v_w_in', 'new_v_conv_a_w', 'new_v_conv_a_b', 'new_v_ln_a_g', 'new_v_ln_a_b', 'new_v_w_a_out', 'new_v_ssm_conv_w', 'new_v_ssm_conv_b', 'new_v_a_log', 'new_v_dt_bias', 'new_v_d_skip', 'new_v_ssm_norm_g', 'new_v_w_b_out', 'new_v_w_o', 'new_v_ln1_g', 'new_v_ln1_b', 'new_v_w_gate_up', 'new_v_w_down', 'new_v_ln2_g', 'new_v_ln2_b', 'new_v_w_ple', 'new_v_ple_norm_g', 'new_v_w_ple_gate']
TWIN_LEAF_KINDS = {'loss': 'loss', 'grad_x': 'grad_x', 'grad_w_in': 'grad_w', 'grad_conv_a_w': 'grad_w', 'grad_conv_a_b': 'grad_w', 'grad_ln_a_g': 'grad_w', 'grad_ln_a_b': 'grad_w', 'grad_w_a_out': 'grad_w', 'grad_ssm_conv_w': 'grad_w', 'grad_ssm_conv_b': 'grad_w', 'grad_a_log': 'grad_w', 'grad_dt_bias': 'grad_w', 'grad_d_skip': 'grad_w', 'grad_ssm_norm_g': 'grad_w', 'grad_w_b_out': 'grad_w', 'grad_w_o': 'grad_w', 'grad_ln1_g': 'grad_w', 'grad_ln1_b': 'grad_w', 'grad_w_gate_up': 'grad_w', 'grad_w_down': 'grad_w', 'grad_ln2_g': 'grad_w', 'grad_ln2_b': 'grad_w', 'grad_w_ple': 'grad_w', 'grad_ple_norm_g': 'grad_w', 'grad_w_ple_gate': 'grad_w', 'delta_w_in': 'delta_w', 'delta_conv_a_w': 'delta_w', 'delta_conv_a_b': 'delta_w', 'delta_ln_a_g': 'delta_w', 'delta_ln_a_b': 'delta_w', 'delta_w_a_out': 'delta_w', 'delta_ssm_conv_w': 'delta_w', 'delta_ssm_conv_b': 'delta_w', 'delta_a_log': 'delta_w', 'delta_dt_bias': 'delta_w', 'delta_d_skip': 'delta_w', 'delta_ssm_norm_g': 'delta_w', 'delta_w_b_out': 'delta_w', 'delta_w_o': 'delta_w', 'delta_ln1_g': 'delta_w', 'delta_ln1_b': 'delta_w', 'delta_w_gate_up': 'delta_w', 'delta_w_down': 'delta_w', 'delta_ln2_g': 'delta_w', 'delta_ln2_b': 'delta_w', 'delta_w_ple': 'delta_w', 'delta_ple_norm_g': 'delta_w', 'delta_w_ple_gate': 'delta_w', 'new_m_w_in': 'new_m', 'new_m_conv_a_w': 'new_m', 'new_m_conv_a_b': 'new_m', 'new_m_ln_a_g': 'new_m', 'new_m_ln_a_b': 'new_m', 'new_m_w_a_out': 'new_m', 'new_m_ssm_conv_w': 'new_m', 'new_m_ssm_conv_b': 'new_m', 'new_m_a_log': 'new_m', 'new_m_dt_bias': 'new_m', 'new_m_d_skip': 'new_m', 'new_m_ssm_norm_g': 'new_m', 'new_m_w_b_out': 'new_m', 'new_m_w_o': 'new_m', 'new_m_ln1_g': 'new_m', 'new_m_ln1_b': 'new_m', 'new_m_w_gate_up': 'new_m', 'new_m_w_down': 'new_m', 'new_m_ln2_g': 'new_m', 'new_m_ln2_b': 'new_m', 'new_m_w_ple': 'new_m', 'new_m_ple_norm_g': 'new_m', 'new_m_w_ple_gate': 'new_m', 'new_v_w_in': 'new_v', 'new_v_conv_a_w': 'new_v', 'new_v_conv_a_b': 'new_v', 'new_v_ln_a_g': 'new_v', 'new_v_ln_a_b': 'new_v', 'new_v_w_a_out': 'new_v', 'new_v_ssm_conv_w': 'new_v', 'new_v_ssm_conv_b': 'new_v', 'new_v_a_log': 'new_v', 'new_v_dt_bias': 'new_v', 'new_v_d_skip': 'new_v', 'new_v_ssm_norm_g': 'new_v', 'new_v_w_b_out': 'new_v', 'new_v_w_o': 'new_v', 'new_v_ln1_g': 'new_v', 'new_v_ln1_b': 'new_v', 'new_v_w_gate_up': 'new_v', 'new_v_w_down': 'new_v', 'new_v_ln2_g': 'new_v', 'new_v_ln2_b': 'new_v', 'new_v_w_ple': 'new_v', 'new_v_ple_norm_g': 'new_v', 'new_v_w_ple_gate': 'new_v'}


def _forward(args):
    return _fwd_reference(*[args[k] for k in FWD_PARAMS])


def _output_shape():
    def fwd():
        inp = _fwd_setup_inputs(0)
        return _fwd_reference(*[inp[k] for k in FWD_PARAMS])
    out = _jax.eval_shape(fwd)
    return out.shape, out.dtype

N_MICROBATCH = 1
ADAM_LR = 0.001
ADAM_B1 = 0.9
ADAM_B2 = 0.999
ADAM_EPS = 1e-08
ADAM_WD = 0.01
ADAM_STEP = 10
PER_EXAMPLE_BATCH_AXIS = {'x': 0, 'p': 1, 'loss_target': 0}
SHARED_INPUTS = []
_WEIGHT_DTYPES = {'w_in': _jnp.float32, 'conv_a_w': _jnp.float32, 'conv_a_b': _jnp.float32, 'ln_a_g': _jnp.float32, 'ln_a_b': _jnp.float32, 'w_a_out': _jnp.float32, 'ssm_conv_w': _jnp.float32, 'ssm_conv_b': _jnp.float32, 'a_log': _jnp.float32, 'dt_bias': _jnp.float32, 'd_skip': _jnp.float32, 'ssm_norm_g': _jnp.float32, 'w_b_out': _jnp.float32, 'w_o': _jnp.float32, 'ln1_g': _jnp.float32, 'ln1_b': _jnp.float32, 'w_gate_up': _jnp.float32, 'w_down': _jnp.float32, 'ln2_g': _jnp.float32, 'ln2_b': _jnp.float32, 'w_ple': _jnp.float32, 'ple_norm_g': _jnp.float32, 'w_ple_gate': _jnp.float32}
MOMENT_SCALE = {'w_in': 2.037708e-02, 'conv_a_w': 3.850679e-02, 'conv_a_b': 8.615009e-01, 'ln_a_g': 3.159956e-01, 'ln_a_b': 5.118918e-01, 'w_a_out': 3.336807e-01, 'ssm_conv_w': 4.188951e-02, 'ssm_conv_b': 1.869977e-01, 'a_log': 3.146235e-01, 'dt_bias': 4.600882e-02, 'd_skip': 1.039572e-01, 'ssm_norm_g': 1.039647e-01, 'w_b_out': 3.303611e-01, 'w_o': 4.785334e-01, 'ln1_g': 3.373928e+00, 'ln1_b': 1.266763e+01, 'w_gate_up': 5.487654e-02, 'w_down': 1.820383e-01, 'ln2_g': 9.330581e+01, 'ln2_b': 1.377338e+01, 'w_ple': 5.037971e-01, 'ple_norm_g': 2.649365e+01, 'w_ple_gate': 2.793818e-01}


def _to_microbatches(a, axis):
    t = _jnp.moveaxis(a, axis, 0)
    t = t.reshape((N_MICROBATCH, t.shape[0] // N_MICROBATCH) + t.shape[1:])
    return _jnp.moveaxis(t, 1, axis + 1)


def setup_inputs(seed: int = 0) -> dict:
    inp = _fwd_setup_inputs(seed)
    key = _jax.random.fold_in(_jax.random.key(seed), 7919)
    shape, _ = _output_shape()
    out = dict(inp)
    out["loss_target"] = _jax.random.normal(_jax.random.fold_in(key, 0), shape, _jnp.float32)
    for i, name in enumerate(TWIN_WEIGHTS):
        w = inp[name].astype(_jnp.float32)
        if MOMENT_SCALE is None:
            s = _jnp.sqrt(_jnp.mean(_jnp.square(w)) + 1e-30)
        else:
            s = MOMENT_SCALE[name]
        km, kv = _jax.random.split(_jax.random.fold_in(key, i + 1))
        out[name] = w
        out["m_" + name] = s * _jax.random.normal(km, w.shape, _jnp.float32)
        out["v_" + name] = (s * s) * _jax.random.uniform(kv, w.shape, _jnp.float32, 0.5, 1.5)
    if N_MICROBATCH > 1:
        for name, axis in PER_EXAMPLE_BATCH_AXIS.items():
            out[name] = _to_microbatches(out[name], axis)
    return {'x': out['x'], 'p': out['p'], 'w_in': out['w_in'], 'conv_a_w': out['conv_a_w'], 'conv_a_b': out['conv_a_b'], 'ln_a_g': out['ln_a_g'], 'ln_a_b': out['ln_a_b'], 'w_a_out': out['w_a_out'], 'ssm_conv_w': out['ssm_conv_w'], 'ssm_conv_b': out['ssm_conv_b'], 'a_log': out['a_log'], 'dt_bias': out['dt_bias'], 'd_skip': out['d_skip'], 'ssm_norm_g': out['ssm_norm_g'], 'w_b_out': out['w_b_out'], 'w_o': out['w_o'], 'ln1_g': out['ln1_g'], 'ln1_b': out['ln1_b'], 'w_gate_up': out['w_gate_up'], 'w_down': out['w_down'], 'ln2_g': out['ln2_g'], 'ln2_b': out['ln2_b'], 'w_ple': out['w_ple'], 'ple_norm_g': out['ple_norm_g'], 'w_ple_gate': out['w_ple_gate'], 'loss_target': out['loss_target'], 'm_w_in': out['m_w_in'], 'm_conv_a_w': out['m_conv_a_w'], 'm_conv_a_b': out['m_conv_a_b'], 'm_ln_a_g': out['m_ln_a_g'], 'm_ln_a_b': out['m_ln_a_b'], 'm_w_a_out': out['m_w_a_out'], 'm_ssm_conv_w': out['m_ssm_conv_w'], 'm_ssm_conv_b': out['m_ssm_conv_b'], 'm_a_log': out['m_a_log'], 'm_dt_bias': out['m_dt_bias'], 'm_d_skip': out['m_d_skip'], 'm_ssm_norm_g': out['m_ssm_norm_g'], 'm_w_b_out': out['m_w_b_out'], 'm_w_o': out['m_w_o'], 'm_ln1_g': out['m_ln1_g'], 'm_ln1_b': out['m_ln1_b'], 'm_w_gate_up': out['m_w_gate_up'], 'm_w_down': out['m_w_down'], 'm_ln2_g': out['m_ln2_g'], 'm_ln2_b': out['m_ln2_b'], 'm_w_ple': out['m_w_ple'], 'm_ple_norm_g': out['m_ple_norm_g'], 'm_w_ple_gate': out['m_w_ple_gate'], 'v_w_in': out['v_w_in'], 'v_conv_a_w': out['v_conv_a_w'], 'v_conv_a_b': out['v_conv_a_b'], 'v_ln_a_g': out['v_ln_a_g'], 'v_ln_a_b': out['v_ln_a_b'], 'v_w_a_out': out['v_w_a_out'], 'v_ssm_conv_w': out['v_ssm_conv_w'], 'v_ssm_conv_b': out['v_ssm_conv_b'], 'v_a_log': out['v_a_log'], 'v_dt_bias': out['v_dt_bias'], 'v_d_skip': out['v_d_skip'], 'v_ssm_norm_g': out['v_ssm_norm_g'], 'v_w_b_out': out['v_w_b_out'], 'v_w_o': out['v_w_o'], 'v_ln1_g': out['v_ln1_g'], 'v_ln1_b': out['v_ln1_b'], 'v_w_gate_up': out['v_w_gate_up'], 'v_w_down': out['v_w_down'], 'v_ln2_g': out['v_ln2_g'], 'v_ln2_b': out['v_ln2_b'], 'v_w_ple': out['v_w_ple'], 'v_ple_norm_g': out['v_ple_norm_g'], 'v_w_ple_gate': out['v_w_ple_gate']}


def _loss(weights, diff, rest, loss_target):
    with _jax.named_scope("forward"):
        args = {**rest, TWIN_DIFF_INPUT: diff, **{k: w.astype(_WEIGHT_DTYPES[k]) for k, w in weights.items()}}
        y = _forward(args)
    with _jax.named_scope("loss_head"):
        err = _jnp.square(y.astype(_jnp.float32) - loss_target)
        return 0.5 * _jnp.sum(_jnp.mean(err, axis=-1)) if err.ndim else 0.5 * err


def _adamw(w, g, m, v):
    m = ADAM_B1 * m + (1.0 - ADAM_B1) * g
    v = ADAM_B2 * v + (1.0 - ADAM_B2) * _jnp.square(g)
    m_hat = m / (1.0 - ADAM_B1 ** ADAM_STEP)
    v_hat = v / (1.0 - ADAM_B2 ** ADAM_STEP)
    delta = -ADAM_LR * (m_hat / (_jnp.sqrt(v_hat) + ADAM_EPS) + ADAM_WD * w)
    return delta, m, v


def reference(x, p, w_in, conv_a_w, conv_a_b, ln_a_g, ln_a_b, w_a_out, ssm_conv_w, ssm_conv_b, a_log, dt_bias, d_skip, ssm_norm_g, w_b_out, w_o, ln1_g, ln1_b, w_gate_up, w_down, ln2_g, ln2_b, w_ple, ple_norm_g, w_ple_gate, loss_target, m_w_in, m_conv_a_w, m_conv_a_b, m_ln_a_g, m_ln_a_b, m_w_a_out, m_ssm_conv_w, m_ssm_conv_b, m_a_log, m_dt_bias, m_d_skip, m_ssm_norm_g, m_w_b_out, m_w_o, m_ln1_g, m_ln1_b, m_w_gate_up, m_w_down, m_ln2_g, m_ln2_b, m_w_ple, m_ple_norm_g, m_w_ple_gate, v_w_in, v_conv_a_w, v_conv_a_b, v_ln_a_g, v_ln_a_b, v_w_a_out, v_ssm_conv_w, v_ssm_conv_b, v_a_log, v_dt_bias, v_d_skip, v_ssm_norm_g, v_w_b_out, v_w_o, v_ln1_g, v_ln1_b, v_w_gate_up, v_w_down, v_ln2_g, v_ln2_b, v_w_ple, v_ple_norm_g, v_w_ple_gate):
    given = dict(x=x, p=p, w_in=w_in, conv_a_w=conv_a_w, conv_a_b=conv_a_b, ln_a_g=ln_a_g, ln_a_b=ln_a_b, w_a_out=w_a_out, ssm_conv_w=ssm_conv_w, ssm_conv_b=ssm_conv_b, a_log=a_log, dt_bias=dt_bias, d_skip=d_skip, ssm_norm_g=ssm_norm_g, w_b_out=w_b_out, w_o=w_o, ln1_g=ln1_g, ln1_b=ln1_b, w_gate_up=w_gate_up, w_down=w_down, ln2_g=ln2_g, ln2_b=ln2_b, w_ple=w_ple, ple_norm_g=ple_norm_g, w_ple_gate=w_ple_gate, loss_target=loss_target, m_w_in=m_w_in, m_conv_a_w=m_conv_a_w, m_conv_a_b=m_conv_a_b, m_ln_a_g=m_ln_a_g, m_ln_a_b=m_ln_a_b, m_w_a_out=m_w_a_out, m_ssm_conv_w=m_ssm_conv_w, m_ssm_conv_b=m_ssm_conv_b, m_a_log=m_a_log, m_dt_bias=m_dt_bias, m_d_skip=m_d_skip, m_ssm_norm_g=m_ssm_norm_g, m_w_b_out=m_w_b_out, m_w_o=m_w_o, m_ln1_g=m_ln1_g, m_ln1_b=m_ln1_b, m_w_gate_up=m_w_gate_up, m_w_down=m_w_down, m_ln2_g=m_ln2_g, m_ln2_b=m_ln2_b, m_w_ple=m_w_ple, m_ple_norm_g=m_ple_norm_g, m_w_ple_gate=m_w_ple_gate, v_w_in=v_w_in, v_conv_a_w=v_conv_a_w, v_conv_a_b=v_conv_a_b, v_ln_a_g=v_ln_a_g, v_ln_a_b=v_ln_a_b, v_w_a_out=v_w_a_out, v_ssm_conv_w=v_ssm_conv_w, v_ssm_conv_b=v_ssm_conv_b, v_a_log=v_a_log, v_dt_bias=v_dt_bias, v_d_skip=v_d_skip, v_ssm_norm_g=v_ssm_norm_g, v_w_b_out=v_w_b_out, v_w_o=v_w_o, v_ln1_g=v_ln1_g, v_ln1_b=v_ln1_b, v_w_gate_up=v_w_gate_up, v_w_down=v_w_down, v_ln2_g=v_ln2_g, v_ln2_b=v_ln2_b, v_w_ple=v_w_ple, v_ple_norm_g=v_ple_norm_g, v_w_ple_gate=v_w_ple_gate)
    weights = {n: given[n] for n in TWIN_WEIGHTS}
    shared = {n: given[n] for n in SHARED_INPUTS}
    per_example = {n: given[n] for n in ['x', 'p']}
    grad_fn = _jax.value_and_grad(_loss, argnums=(0, 1))

    def one_microbatch(ex, loss_target):
        ex = dict(ex)
        diff = ex.pop(TWIN_DIFF_INPUT)
        return grad_fn(weights, diff, {**shared, **ex}, loss_target)

    if N_MICROBATCH == 1:
        loss, (grad_w, grad_x) = one_microbatch(per_example, given["loss_target"])
    else:
        def body(carry, xs):
            loss_sum, grad_sum = carry
            l_k, (gw_k, gx_k) = one_microbatch(xs[0], xs[1])
            with _jax.named_scope("update"):
                return (loss_sum + l_k, _jax.tree.map(_jnp.add, grad_sum, gw_k)), gx_k

        init = (_jnp.zeros((), _jnp.float32), _jax.tree.map(_jnp.zeros_like, weights))
        (loss, grad_w), grad_x = _jax.lax.scan(body, init, (per_example, given["loss_target"]))
    with _jax.named_scope("update"):
        delta_w, new_m, new_v = {}, {}, {}
        for n in TWIN_WEIGHTS:
            delta_w[n], new_m[n], new_v[n] = _adamw(weights[n], grad_w[n], given["m_" + n], given["v_" + n])
    return (loss, grad_x, *[grad_w[n] for n in TWIN_WEIGHTS], *[delta_w[n] for n in TWIN_WEIGHTS],
            *[new_m[n] for n in TWIN_WEIGHTS], *[new_v[n] for n in TWIN_WEIGHTS])
```

```python
import math

import jax
import jax.numpy as jnp
from jax import lax
from jax.experimental import pallas as pl
from jax.experimental.pallas import tpu as pltpu

F32 = jnp.float32
BF16 = jnp.bfloat16

VMEM_LIMIT_BYTES = 56 * 1024 * 1024
LANES = 128
SUBLANES = 8

CHUNK = 128
D_STATE = 128
HEAD_DIM = 64
LN_EPS = 1e-5
RMS_EPS = 1e-6
ADAM_LR = 0.001
ADAM_B1 = 0.9
ADAM_B2 = 0.999
ADAM_EPS = 1e-08
ADAM_WD = 0.01
ADAM_STEP = 10
HALO = 16
MESH = pl.DeviceIdType.MESH


def _params(**kw):
    return pltpu.CompilerParams(vmem_limit_bytes=VMEM_LIMIT_BYTES, **kw)


def _sig(x):
    return jax.nn.sigmoid(x)


def _dsilu(x, s):
    return s * (1.0 + x * (1.0 - s))


def _ln_stats(r):
    mu = jnp.mean(r, axis=-1, keepdims=True)
    xc = r - mu
    var = jnp.mean(xc * xc, axis=-1, keepdims=True)
    rstd = lax.rsqrt(var + LN_EPS)
    return xc * rstd, rstd


def _ln_bwd(dy, xhat, rstd, g):
    dxh = dy * g
    m1 = jnp.mean(dxh, axis=-1, keepdims=True)
    m2 = jnp.mean(dxh * xhat, axis=-1, keepdims=True)
    return rstd * (dxh - m1 - xhat * m2)


def _rows8(v):
    tm, w = v.shape
    return v.reshape(tm // SUBLANES, SUBLANES, w).sum(axis=0)


def fused_mm(name, prods, extras, epi, row_outs, col_outs=(), *, M, tm, tn, nj=1, nk=1,
             passthrough=None):
    np_ = len(prods)
    ne = len(extras)
    nro = len(row_outs)
    nco = len(col_outs)
    use_acc = nk > 1

    def body(*refs):
        a_refs = [refs[2 * p] for p in range(np_)]
        w_refs = [refs[2 * p + 1] for p in range(np_)]
        pos = 2 * np_
        e_refs = refs[pos:pos + ne]
        pos += ne
        if passthrough is not None:
            pos += 1
        ro_refs = refs[pos:pos + nro]
        pos += nro
        co_refs = refs[pos:pos + nco]
        pos += nco
        acc_ref = refs[pos] if use_acc else None
        i = pl.program_id(1)
        k = pl.program_id(2)

        def prod(p):
            a = a_refs[p][...]
            if a.dtype != BF16:
                a = a.astype(BF16)
            return jnp.dot(a, w_refs[p][...], preferred_element_type=F32)

        def finish(acc):
            rows, cols = epi(acc, [r[...] for r in e_refs])
            for v, o in zip(rows, ro_refs):
                o[...] = v.astype(o.dtype)
            for v, o in zip(cols, co_refs):
                v8 = _rows8(v)

                @pl.when(i == 0)
                def _():
                    o[...] = v8

                @pl.when(i > 0)
                def _():
                    o[...] += v8

        if not use_acc:
            acc = prod(0)
            for p in range(1, np_):
                acc = acc + prod(p)
            finish(acc)
        else:
            @pl.when(k == 0)
            def _():
                acc = None
                for p in range(np_):
                    acc = prod(p) if acc is None else acc + prod(p)
                acc_ref[...] = acc

            @pl.when(k > 0)
            def _():
                acc = None
                for p in range(np_):
                    if prods[p][3]:
                        acc = prod(p) if acc is None else acc + prod(p)
                acc_ref[...] += acc

            @pl.when(k == nk - 1)
            def _():
                finish(acc_ref[...])

    in_specs = []
    args = []
    for a, w, joff, ksplit in prods:
        K = a.shape[1]
        if ksplit:
            tk = K // nk
            in_specs.append(pl.BlockSpec((tm, tk), lambda j, i, k: (i, k)))
            in_specs.append(pl.BlockSpec((tk, tn), lambda j, i, k, joff=joff: (k, j + joff)))
        else:
            in_specs.append(pl.BlockSpec((tm, K), lambda j, i, k: (i, 0)))
            in_specs.append(pl.BlockSpec((K, tn), lambda j, i, k, joff=joff: (0, j + joff)))
        args += [a, w]
    for arr, kind, width, c0 in extras:
        if kind == 'row':
            in_specs.append(pl.BlockSpec((tm, width), lambda j, i, k, c0=c0: (i, c0 + j)))
        else:
            in_specs.append(pl.BlockSpec((arr.shape[0], width), lambda j, i, k, c0=c0: (0, c0 + j)))
        args.append(arr)
    aliases = {}
    if passthrough is not None:
        arr, oidx = passthrough
        in_specs.append(pl.BlockSpec(memory_space=pl.ANY))
        aliases = {len(args): oidx}
        args.append(arr)
    out_shape = []
    out_specs = []
    for n_total, dtype, width, c0 in row_outs:
        out_shape.append(jax.ShapeDtypeStruct((M, n_total), dtype))
        out_specs.append(pl.BlockSpec((tm, width), lambda j, i, k, c0=c0: (i, c0 + j)))
    for n_total, width, c0 in col_outs:
        out_shape.append(jax.ShapeDtypeStruct((SUBLANES, n_total), F32))
        out_specs.append(pl.BlockSpec((SUBLANES, width), lambda j, i, k, c0=c0: (0, c0 + j)))
    scratch = [pltpu.VMEM((tm, tn), F32)] if use_acc else []
    return pl.pallas_call(
        body, name=name, grid=(nj, M // tm, nk), in_specs=in_specs, out_specs=out_specs,
        out_shape=out_shape, scratch_shapes=scratch, input_output_aliases=aliases,
        compiler_params=_params(dimension_semantics=("arbitrary", "arbitrary", "arbitrary")),
    )(*args)


def mm_tn(name, a, b, *, tm, tk, tn):
    M, K = a.shape
    N = b.shape[1]

    def body(a_ref, b_ref, o_ref):
        m = pl.program_id(2)
        p = lax.dot_general(a_ref[...], b_ref[...], (((0,), (0,)), ((), ())),
                            preferred_element_type=F32)

        @pl.when(m == 0)
        def _():
            o_ref[...] = p

        @pl.when(m > 0)
        def _():
            o_ref[...] += p

    return pl.pallas_call(
        body, name=name, grid=(K // tk, N // tn, M // tm),
        in_specs=[pl.BlockSpec((tm, tk), lambda kk, j, m: (m, kk)),
                  pl.BlockSpec((tm, tn), lambda kk, j, m: (m, j))],
        out_specs=pl.BlockSpec((tk, tn), lambda kk, j, m: (kk, j)),
        out_shape=jax.ShapeDtypeStruct((K, N), F32),
        compiler_params=_params(dimension_semantics=("arbitrary", "arbitrary", "arbitrary")),
    )(a, b)


def row_call(name, fn, ins, row_outs, col_outs=(), *, M, tm, nc=1):
    ni = len(ins)
    nro = len(row_outs)

    def body(*refs):
        i = pl.program_id(1)
        vals = [r[...] for r in refs[:ni]]
        rows, cols = fn(*vals)
        for v, o in zip(rows, refs[ni:ni + nro]):
            o[...] = v.astype(o.dtype)
        for v, o in zip(cols, refs[ni + nro:]):
            v8 = _rows8(v)

            @pl.when(i == 0)
            def _():
                o[...] = v8

            @pl.when(i > 0)
            def _():
                o[...] += v8

    in_specs = []
    for arr, kind, width, c0, cmul in ins:
        if kind == 'row':
            in_specs.append(pl.BlockSpec((tm, width), lambda cj, i, c0=c0, cmul=cmul: (i, c0 + cmul * cj)))
        else:
            in_specs.append(pl.BlockSpec((arr.shape[0], width), lambda cj, i, c0=c0, cmul=cmul: (0, c0 + cmul * cj)))
    out_shape = []
    out_specs = []
    for n_total, dtype, width, c0, cmul in row_outs:
        out_shape.append(jax.ShapeDtypeStruct((M, n_total), dtype))
        out_specs.append(pl.BlockSpec((tm, width), lambda cj, i, c0=c0, cmul=cmul: (i, c0 + cmul * cj)))
    for n_total, width, c0, cmul in col_outs:
        out_shape.append(jax.ShapeDtypeStruct((SUBLANES, n_total), F32))
        out_specs.append(pl.BlockSpec((SUBLANES, width), lambda cj, i, c0=c0, cmul=cmul: (0, c0 + cmul * cj)))
    return pl.pallas_call(
        body, name=name, grid=(nc, M // tm), in_specs=in_specs, out_specs=out_specs,
        out_shape=out_shape,
        compiler_params=_params(dimension_semantics=("arbitrary", "arbitrary")),
    )(*[a[0] for a in ins])


def conv_call(name, src, src_c0, w, K, epi, extras, row_outs, col_outs=(), *, M, tm, cw, nc,
              reverse, xin=None, passthrough=None):
    pad = (K - 1) // 2
    assert pad <= HALO - 1
    R = tm // HALO
    nblk = M // HALO
    n_i = M // tm
    Kp = w.shape[0]
    ne = len(extras)
    nro = len(row_outs)
    nco = len(col_outs)
    rb = 64
    cbw = min(cw, 256)

    def body(*refs):
        main_ref, prev_ref, next_ref, w_ref = refs[:4]
        pos = 4
        xin_ref = None
        if xin is not None:
            xin_ref = refs[pos]
            pos += 1
        e_refs = refs[pos:pos + ne]
        pos += ne
        if passthrough is not None:
            pos += 1
        ro_refs = refs[pos:pos + nro]
        pos += nro
        co_refs = refs[pos:pos + nco]
        pos += nco
        dw_ref = None
        if xin is not None:
            dw_ref = refs[pos]
            pos += 1
        ext_ref, conv_ref = refs[pos], refs[pos + 1]
        i = pl.program_id(1)

        ext_ref[0:HALO, :] = jnp.where(i == 0, 0.0, prev_ref[...].astype(F32))
        ext_ref[HALO:HALO + tm, :] = main_ref[...].astype(F32)
        ext_ref[HALO + tm:, :] = jnp.where(i == n_i - 1, 0.0, next_ref[...].astype(F32))
        if dw_ref is not None:
            @pl.when(i == 0)
            def _():
                dw_ref[...] = jnp.zeros_like(dw_ref)

        for c0 in range(0, cw, cbw):
            for r0 in range(0, tm, rb):
                acc = jnp.zeros((rb, cbw), F32)
                if xin_ref is not None:
                    xblk = xin_ref[r0:r0 + rb, c0:c0 + cbw].astype(F32)
                for k in range(K):
                    off = (pad - k) if reverse else (k - pad)
                    d = ext_ref[HALO + r0 + off:HALO + r0 + off + rb, c0:c0 + cbw]
                    acc = acc + d * w_ref[k:k + 1, c0:c0 + cbw]
                    if xin_ref is not None:
                        dw_ref[k, :, c0:c0 + cbw] += _rows8(xblk * d)
                conv_ref[r0:r0 + rb, c0:c0 + cbw] = acc

        rows, cols = epi(conv_ref[...], [r[...] for r in e_refs])
        for v, o in zip(rows, ro_refs):
            o[...] = v.astype(o.dtype)
        for v, o in zip(cols, co_refs):
            v8 = _rows8(v)

            @pl.when(i == 0)
            def _():
                o[...] = v8

            @pl.when(i > 0)
            def _():
                o[...] += v8

    in_specs = [
        pl.BlockSpec((tm, cw), lambda cj, i: (i, src_c0 + cj)),
        pl.BlockSpec((HALO, cw), lambda cj, i: (jnp.maximum(i * R - 1, 0), src_c0 + cj)),
        pl.BlockSpec((HALO, cw), lambda cj, i: (jnp.minimum((i + 1) * R, nblk - 1), src_c0 + cj)),
        pl.BlockSpec((Kp, cw), lambda cj, i: (0, cj)),
    ]
    args = [src, src, src, w]
    if xin is not None:
        in_specs.append(pl.BlockSpec((tm, cw), lambda cj, i, c0=xin[1]: (i, c0 + cj)))
        args.append(xin[0])
    for arr, kind, width, c0, cmul in extras:
        if kind == 'row':
            in_specs.append(pl.BlockSpec((tm, width), lambda cj, i, c0=c0, cmul=cmul: (i, c0 + cmul * cj)))
        else:
            in_specs.append(pl.BlockSpec((arr.shape[0], width), lambda cj, i, c0=c0, cmul=cmul: (0, c0 + cmul * cj)))
        args.append(arr)
    aliases = {}
    if passthrough is not None:
        in_specs.append(pl.BlockSpec(memory_space=pl.ANY))
        aliases = {len(args): passthrough[1]}
        args.append(passthrough[0])
    out_shape = []
    out_specs = []
    for n_total, dtype, width, c0, cmul in row_outs:
        out_shape.append(jax.ShapeDtypeStruct((M, n_total), dtype))
        out_specs.append(pl.BlockSpec((tm, width), lambda cj, i, c0=c0, cmul=cmul: (i, c0 + cmul * cj)))
    for n_total, width, c0, cmul in col_outs:
        out_shape.append(jax.ShapeDtypeStruct((SUBLANES, n_total), F32))
        out_specs.append(pl.BlockSpec((SUBLANES, width), lambda cj, i, c0=c0, cmul=cmul: (0, c0 + cmul * cj)))
    if xin is not None:
        out_shape.append(jax.ShapeDtypeStruct((Kp, SUBLANES, cw * nc), F32))
        out_specs.append(pl.BlockSpec((Kp, SUBLANES, cw), lambda cj, i: (0, 0, cj)))
    return pl.pallas_call(
        body, name=name, grid=(nc, n_i), in_specs=in_specs, out_specs=out_specs,
        out_shape=out_shape, input_output_aliases=aliases,
        scratch_shapes=[pltpu.VMEM((tm + 2 * HALO, cw), F32), pltpu.VMEM((tm, cw), F32)],
        compiler_params=_params(dimension_semantics=("arbitrary", "arbitrary")),
    )(*args)


def _split_dot(m_bf16, v, n_pass, dims=None):
    out = None
    rest = v
    for p in range(n_pass):
        piece = rest.astype(BF16)
        if p + 1 < n_pass:
            rest = rest - piece.astype(F32)
        if dims is None:
            t = jnp.dot(m_bf16, piece, preferred_element_type=F32)
        else:
            t = lax.dot_general(m_bf16, piece, dims, preferred_element_type=F32)
        out = t if out is None else out + t
    return out


def _split_dot_r(v, m_bf16, n_pass):
    out = None
    rest = v
    for p in range(n_pass):
        piece = rest.astype(BF16)
        if p + 1 < n_pass:
            rest = rest - piece.astype(F32)
        t = jnp.dot(piece, m_bf16, preferred_element_type=F32)
        out = t if out is None else out + t
    return out


def _softplus(x):
    return jnp.maximum(x, 0.0) + jnp.log1p(jnp.exp(-jnp.abs(x)))


NT_DIMS = (((1,), (1,)), ((), ()))
TN_DIMS = (((0,), (0,)), ((), ()))


def _ssd_common(dtraw, dtb, alog, rev, n_heads):
    L = CHUNK
    if rev:
        dtraw = pltpu.roll(dtraw, LANES - n_heads, 1)
    pre = dtraw + dtb
    dt = _softplus(pre)
    A = -jnp.exp(alog)
    a = dt * A
    ri = lax.broadcasted_iota(jnp.int32, (L, L), 0)
    ci = lax.broadcasted_iota(jnp.int32, (L, L), 1)
    tri = (ci >= ri) if rev else (ci <= ri)
    tri_b = tri.astype(BF16)
    cs = _split_dot(tri_b, a, 3)
    last = 0 if rev else L - 1
    cs_last = cs[last:last + 1, :]
    return dict(pre=pre, dt=dt, A=A, cs=cs, csT=cs.T, dtT=dt.T, tri=tri, tri_b=tri_b,
                cs_last=cs_last, dout=jnp.exp(cs), dst=jnp.exp(cs_last - cs),
                tot=jnp.exp(cs_last), last=last, ri=ri)


def ssd_fwd(name, xbc, dtraw, dtb, alog, *, S, DI, G, H, rev):
    NC = S // CHUNK
    R = H // G
    GW = R * HEAD_DIM
    N = D_STATE
    XBC = xbc.shape[1]
    P = HEAD_DIM

    def body(xbc_ref, dtraw_ref, dtb_ref, alog_ref, y_ref, st_ref, h_ref):
        c = pl.program_id(0)

        @pl.when(c == 0)
        def _():
            h_ref[...] = jnp.zeros_like(h_ref)

        q = _ssd_common(dtraw_ref[...], dtb_ref[...], alog_ref[...], rev, H)
        cs, csT, dtT, dout = q['cs'], q['csT'], q['dtT'], q['dout']
        wst = q['dst'] * q['dt']
        tot = q['tot']
        for g in range(G):
            Bg = xbc_ref[:, DI + g * N:DI + (g + 1) * N].astype(BF16)
            Cg = xbc_ref[:, DI + G * N + g * N:DI + G * N + (g + 1) * N].astype(BF16)
            CB = lax.dot_general(Cg, Bg, NT_DIMS, preferred_element_type=F32)
            Hg = h_ref[g]
            yoff = jnp.dot(Cg, Hg.astype(BF16), preferred_element_type=F32)
            ys = []
            xws = []
            tots = []
            for r in range(R):
                h = g * R + r
                seg = jnp.where(q['tri'], cs[:, h:h + 1] - csT[h:h + 1, :], -1e30)
                Gm = CB * jnp.exp(seg) * dtT[h:h + 1, :]
                xh = xbc_ref[:, h * P:(h + 1) * P]
                yd = jnp.dot(Gm.astype(BF16), xh.astype(BF16), preferred_element_type=F32)
                ys.append(yd + yoff[:, r * P:(r + 1) * P] * dout[:, h:h + 1])
                xws.append(xh * wst[:, h:h + 1])
                tots.append(jnp.broadcast_to(tot[:, h:h + 1], (1, P)))
            y_ref[:, g * GW:(g + 1) * GW] = jnp.concatenate(ys, axis=1)
            xw = jnp.concatenate(xws, axis=1).astype(BF16)
            Sg = lax.dot_general(Bg, xw, TN_DIMS, preferred_element_type=F32)
            st_ref[0, g] = Hg
            h_ref[g] = Hg * jnp.concatenate(tots, axis=1) + Sg

    cmap = (lambda c: (NC - 1 - c, 0)) if rev else (lambda c: (c, 0))
    smap = (lambda c: (NC - 1 - c, 0, 0, 0)) if rev else (lambda c: (c, 0, 0, 0))
    return pl.pallas_call(
        body, name=name, grid=(NC,),
        in_specs=[pl.BlockSpec((CHUNK, XBC), cmap), pl.BlockSpec((CHUNK, LANES), cmap),
                  pl.BlockSpec((1, LANES), lambda c: (0, 0)), pl.BlockSpec((1, LANES), lambda c: (0, 0))],
        out_specs=[pl.BlockSpec((CHUNK, DI), cmap), pl.BlockSpec((1, G, N, GW), smap)],
        out_shape=[jax.ShapeDtypeStruct((S, DI), F32), jax.ShapeDtypeStruct((NC, G, N, GW), F32)],
        scratch_shapes=[pltpu.VMEM((G, N, GW), F32)],
        compiler_params=_params(dimension_semantics=("arbitrary",)),
    )(xbc, dtraw, dtb, alog)


def ssd_bwd(name, xbc, dtraw, dy, st, dtb, alog, headsel, *, S, DI, G, H, rev):
    NC = S // CHUNK
    R = H // G
    GW = R * HEAD_DIM
    N = D_STATE
    XBC = xbc.shape[1]
    P = HEAD_DIM
    L = CHUNK

    def body(xbc_ref, dtraw_ref, dy_ref, st_ref, dtb_ref, alog_ref, sel_ref,
             dxbc_ref, ddt_ref, da_ref, dh_ref, red_ref, dcst_ref):
        c = pl.program_id(0)

        @pl.when(c == 0)
        def _():
            dh_ref[...] = jnp.zeros_like(dh_ref)
            da_ref[...] = jnp.zeros_like(da_ref)

        q = _ssd_common(dtraw_ref[...], dtb_ref[...], alog_ref[...], rev, H)
        cs, csT, dtT, dout, dst, dt, tot = q['cs'], q['csT'], q['dtT'], q['dout'], q['dst'], q['dt'], q['tot']
        wst = dst * dt
        lane = lax.broadcasted_iota(jnp.int32, (L, LANES), 1)
        dcst_ref[...] = jnp.zeros_like(dcst_ref)
        dcs_c = jnp.zeros((L, LANES), F32)
        for g in range(G):
            Bg = xbc_ref[:, DI + g * N:DI + (g + 1) * N].astype(BF16)
            Cg = xbc_ref[:, DI + G * N + g * N:DI + G * N + (g + 1) * N].astype(BF16)
            CB = lax.dot_general(Cg, Bg, NT_DIMS, preferred_element_type=F32)
            Hp = st_ref[0, g]
            Hpb = Hp.astype(BF16)
            dHg = dh_ref[g]
            dHb = dHg.astype(BF16)
            BdH = jnp.dot(Bg, dHb, preferred_element_type=F32)
            yoff = jnp.dot(Cg, Hpb, preferred_element_type=F32)
            dCB = jnp.zeros((L, L), F32)
            us, dyds, xws, tots, dxs = [], [], [], [], []
            for r in range(R):
                h = g * R + r
                Lm = jnp.exp(jnp.where(q['tri'], cs[:, h:h + 1] - csT[h:h + 1, :], -1e30))
                Gm = (CB * Lm).astype(BF16)
                xh = xbc_ref[:, h * P:(h + 1) * P]
                xhb = xh.astype(BF16)
                dyh = dy_ref[:, h * P:(h + 1) * P]
                dyhb = dyh.astype(BF16)
                u1 = lax.dot_general(Gm, dyhb, TN_DIMS, preferred_element_type=F32)
                u = u1 + BdH[:, r * P:(r + 1) * P] * dst[:, h:h + 1]
                dGx = lax.dot_general(dyhb, xhb, NT_DIMS, preferred_element_type=F32)
                T = dGx * (Lm * dtT[h:h + 1, :])
                dCB = dCB + T
                Mseg = T * CB
                dcs_c = jnp.where(lane == h, jnp.sum(Mseg, axis=1, keepdims=True), dcs_c)
                dcst_ref[h:h + 1, :] = jnp.sum(Mseg, axis=0, keepdims=True)
                us.append(u)
                dxs.append(u * dt[:, h:h + 1])
                dyds.append(dyh * dout[:, h:h + 1])
                xws.append(xh * wst[:, h:h + 1])
                tots.append(jnp.broadcast_to(tot[:, h:h + 1], (1, P)))
            ug = jnp.concatenate(us, axis=1)
            xg = xbc_ref[:, g * GW:(g + 1) * GW]
            xw = jnp.concatenate(xws, axis=1)
            dyd32 = jnp.concatenate(dyds, axis=1)
            dyd = dyd32.astype(BF16)
            red_ref[0:L, g * GW:(g + 1) * GW] = dyd32 * yoff
            red_ref[L:2 * L, g * GW:(g + 1) * GW] = xg * ug
            red_ref[2 * L:3 * L, g * GW:(g + 1) * GW] = xw * BdH
            red_ref[3 * L:4 * L, g * GW:(g + 1) * GW] = Hp * dHg
            dxbc_ref[:, g * GW:(g + 1) * GW] = jnp.concatenate(dxs, axis=1)
            dCBb = dCB.astype(BF16)
            dC = (jnp.dot(dCBb, Bg, preferred_element_type=F32)
                  + lax.dot_general(dyd, Hpb, NT_DIMS, preferred_element_type=F32))
            dB = (lax.dot_general(dCBb, Cg, TN_DIMS, preferred_element_type=F32)
                  + lax.dot_general(xw.astype(BF16), dHb, NT_DIMS, preferred_element_type=F32))
            dxbc_ref[:, DI + g * N:DI + (g + 1) * N] = dB
            dxbc_ref[:, DI + G * N + g * N:DI + G * N + (g + 1) * N] = dC
            dh_ref[g] = (dHg * jnp.concatenate(tots, axis=1)
                         + lax.dot_general(Cg, dyd, TN_DIMS, preferred_element_type=F32))
        red = _split_dot_r(red_ref[...], sel_ref[...], 2)
        p1 = red[0:L]
        p2 = red[L:2 * L]
        p3 = red[2 * L:3 * L]
        e1 = jnp.sum(p3, axis=0, keepdims=True)
        e2 = jnp.sum(red[3 * L:4 * L], axis=0, keepdims=True)
        dcs = (dcs_c - dcst_ref[...].T + p1 - p3
               + jnp.where(q['ri'] == q['last'], e1 + tot * e2, 0.0))
        da = _split_dot(q['tri_b'], dcs, 3, TN_DIMS)
        ddt = da * q['A'] + p2
        valid = lane < H
        da_ref[...] += _rows8(jnp.where(valid, da * dt, 0.0))
        ddraw = jnp.where(valid, ddt * _sig(q['pre']), 0.0)
        if rev:
            ddraw = pltpu.roll(ddraw, H, 1)
        ddt_ref[...] = ddraw

    cmap = (lambda c: (c, 0)) if rev else (lambda c: (NC - 1 - c, 0))
    smap = (lambda c: (c, 0, 0, 0)) if rev else (lambda c: (NC - 1 - c, 0, 0, 0))
    const = lambda c: (0, 0)
    return pl.pallas_call(
        body, name=name, grid=(NC,),
        in_specs=[pl.BlockSpec((CHUNK, XBC), cmap), pl.BlockSpec((CHUNK, LANES), cmap),
                  pl.BlockSpec((CHUNK, DI), cmap),
                  pl.BlockSpec((1, G, N, GW), smap),
                  pl.BlockSpec((1, LANES), const), pl.BlockSpec((1, LANES), const),
                  pl.BlockSpec((DI, LANES), const)],
        out_specs=[pl.BlockSpec((CHUNK, XBC), cmap), pl.BlockSpec((CHUNK, LANES), cmap),
                   pl.BlockSpec((SUBLANES, LANES), const)],
        out_shape=[jax.ShapeDtypeStruct((S, XBC), F32), jax.ShapeDtypeStruct((S, LANES), F32),
                   jax.ShapeDtypeStruct((SUBLANES, LANES), F32)],
        scratch_shapes=[pltpu.VMEM((G, N, GW), F32), pltpu.VMEM((4 * CHUNK, DI), F32),
                        pltpu.VMEM((LANES, CHUNK), F32)],
        compiler_params=_params(dimension_semantics=("arbitrary",)),
    )(xbc, dtraw, dy, st, dtb, alog, headsel)


ANY = pl.BlockSpec(memory_space=pl.ANY)


def chip_exchange(name, groups, gather):
    flat = [arr for grp in groups for arr in grp]
    n_in = len(flat)
    n_out = len(groups)
    n_rc = 3 * n_in

    def body(*refs):
        in_refs = refs[:n_in]
        out_refs = refs[n_in:n_in + n_out]
        send, recv, loc = refs[n_in + n_out:]
        x, y, c = lax.axis_index("x"), lax.axis_index("y"), lax.axis_index("c")
        me = 2 * x + y
        peers = [(1 - x, y), (x, 1 - y), (1 - x, 1 - y)]
        local, remote = [], []
        q = 0
        for a, grp in enumerate(groups):
            for l in range(len(grp)):
                src = in_refs[q]
                dst = out_refs[a].at[me] if gather else out_refs[a].at[me, l]
                own = src if gather else src.at[me]
                lc = pltpu.make_async_copy(own, dst, loc.at[q])
                lc.start()
                local.append(lc)
                for j, (px, py) in enumerate(peers):
                    blk = src if gather else src.at[2 * px + py]
                    rc = pltpu.make_async_remote_copy(
                        src_ref=blk, dst_ref=dst, send_sem=send.at[3 * q + j], recv_sem=recv.at[3 * q + j],
                        device_id=(px, py, c), device_id_type=MESH)
                    rc.start()
                    remote.append(rc)
                q += 1
        for lc in local:
            lc.wait()
        for rc in remote:
            rc.wait()

    out_shape = []
    for grp in groups:
        a0 = grp[0]
        if gather:
            out_shape.append(jax.ShapeDtypeStruct((4,) + a0.shape, a0.dtype))
        else:
            out_shape.append(jax.ShapeDtypeStruct((4, len(grp)) + a0.shape[1:], a0.dtype))
    return pl.pallas_call(
        body, name=name, in_specs=[ANY] * n_in, out_specs=[ANY] * n_out, out_shape=out_shape,
        scratch_shapes=[pltpu.SemaphoreType.DMA((n_rc,)), pltpu.SemaphoreType.DMA((n_rc,)),
                        pltpu.SemaphoreType.DMA((n_in,))],
    )(*flat)


def sibling_swap(name, arrs):
    n = len(arrs)

    def body(*refs):
        in_refs = refs[:n]
        out_refs = refs[n:2 * n]
        send, recv = refs[2 * n:]
        peer = (lax.axis_index("x"), lax.axis_index("y"), 1 - lax.axis_index("c"))
        rcs = []
        for a in range(n):
            rc = pltpu.make_async_remote_copy(src_ref=in_refs[a], dst_ref=out_refs[a], send_sem=send.at[a],
                                              recv_sem=recv.at[a], device_id=peer, device_id_type=MESH)
            rc.start()
            rcs.append(rc)
        for rc in rcs:
            rc.wait()

    return pl.pallas_call(
        body, name=name, in_specs=[ANY] * n, out_specs=[ANY] * n,
        out_shape=[jax.ShapeDtypeStruct(a.shape, a.dtype) for a in arrs],
        scratch_shapes=[pltpu.SemaphoreType.DMA((n,)), pltpu.SemaphoreType.DMA((n,))],
    )(*arrs)


def all8_gather(name, v):
    flips = [(fx, fy, fc) for fx in (0, 1) for fy in (0, 1) for fc in (0, 1) if (fx, fy, fc) != (0, 0, 0)]

    def body(v_ref, out_ref, send, recv, loc):
        x, y, c = lax.axis_index("x"), lax.axis_index("y"), lax.axis_index("c")
        me = 4 * x + 2 * y + c
        lc = pltpu.make_async_copy(v_ref, out_ref.at[me], loc)
        lc.start()
        rcs = []
        for k, (fx, fy, fc) in enumerate(flips):
            tgt = (x + fx - 2 * x * fx, y + fy - 2 * y * fy, c + fc - 2 * c * fc)
            rc = pltpu.make_async_remote_copy(src_ref=v_ref, dst_ref=out_ref.at[me], send_sem=send.at[k],
                                              recv_sem=recv.at[k], device_id=tgt, device_id_type=MESH)
            rc.start()
            rcs.append(rc)
        lc.wait()
        for rc in rcs:
            rc.wait()

    return pl.pallas_call(
        body, name=name, in_specs=[ANY], out_specs=ANY,
        out_shape=jax.ShapeDtypeStruct((8,) + v.shape, v.dtype),
        scratch_shapes=[pltpu.SemaphoreType.DMA((7,)), pltpu.SemaphoreType.DMA((7,)), pltpu.SemaphoreType.DMA],
    )(v)


def _pick_rows(rows, cols, target_elems=128 * 1024):
    if rows % SUBLANES != 0:
        return rows
    best = SUBLANES
    t = SUBLANES
    while t <= rows:
        if rows % t == 0 and t * cols <= target_elems:
            best = t
        t += SUBLANES
    return best


def sum_chips(name, parts):
    _, R, C = parts.shape
    tm = _pick_rows(R, C)

    def body(p_ref, o_ref):
        o_ref[...] = (p_ref[0] + p_ref[1]) + (p_ref[2] + p_ref[3])

    return pl.pallas_call(
        body, name=name, grid=(R // tm,),
        in_specs=[pl.BlockSpec((4, tm, C), lambda i: (0, i, 0))],
        out_specs=pl.BlockSpec((tm, C), lambda i: (i, 0)),
        out_shape=jax.ShapeDtypeStruct((R, C), F32),
        compiler_params=_params(dimension_semantics=("arbitrary",)),
    )(parts)


def _adamw(g, w, m, v):
    m = ADAM_B1 * m + (1.0 - ADAM_B1) * g
    v = ADAM_B2 * v + (1.0 - ADAM_B2) * (g * g)
    m_hat = m / (1.0 - ADAM_B1 ** ADAM_STEP)
    v_hat = v / (1.0 - ADAM_B2 ** ADAM_STEP)
    delta = -ADAM_LR * (m_hat / (jnp.sqrt(v_hat) + ADAM_EPS) + ADAM_WD * w)
    return delta, m, v


def adamw_shard(name, s_mine, s_sib, w, m, v):
    R, C = w.shape
    tm = _pick_rows(R, C)

    def body(a_ref, b_ref, w_ref, m_ref, v_ref, g_out, d_out, m_out, v_out):
        g = a_ref[...] + b_ref[...]
        d, mn, vn = _adamw(g, w_ref[...], m_ref[...], v_ref[...])
        g_out[...] = g
        d_out[...] = d
        m_out[...] = mn
        v_out[...] = vn

    spec = pl.BlockSpec((tm, C), lambda i: (i, 0))
    return pl.pallas_call(
        body, name=name, grid=(R // tm,), in_specs=[spec] * 5, out_specs=[spec] * 4,
        out_shape=[jax.ShapeDtypeStruct((R, C), F32)] * 4,
        compiler_params=_params(dimension_semantics=("arbitrary",)),
    )(s_mine, s_sib, w, m, v)


def adamw_small(name, parts, w, m, v):
    W = w.shape[1]

    def body(p_ref, w_ref, m_ref, v_ref, g_out, d_out, m_out, v_out):
        acc = p_ref[0]
        for k in range(1, 8):
            acc = acc + p_ref[k]
        g = jnp.sum(acc, axis=0, keepdims=True)
        d, mn, vn = _adamw(g, w_ref[...], m_ref[...], v_ref[...])
        g_out[...] = g
        d_out[...] = d
        m_out[...] = mn
        v_out[...] = vn

    return pl.pallas_call(
        body, name=name, out_shape=[jax.ShapeDtypeStruct((1, W), F32)] * 4,
        compiler_params=_params(),
    )(parts, w, m, v)


def _pad_lanes(v, width=LANES):
    return jnp.pad(v, ((0, 0), (0, width - v.shape[1])))


def _layer_fwd(cf, x, xb, pb, W, sm):
    S, D, CD, DI, XBC, F, H, G = cf['S'], cf['D'], cf['CD'], cf['DI'], cf['XBC'], cf['F'], cf['H'], cf['G']
    NM = cf['NM']
    alpha = cf['alpha']
    tm = cf['tm']
    tn_in = cf['tn_in']
    sv = {}

    ident = lambda acc, ex: ([acc], [])
    proj, = fused_mm("in_proj", [(xb, W['in_main'], 0, False)], [], ident, [(NM, F32, tn_in, 0)],
                     M=S, tm=tm, tn=tn_in, nj=NM // tn_in)
    dtraw, = fused_mm("dt_proj", [(xb, W['in_dt'], 0, False)], [], ident, [(LANES, F32, LANES, 0)],
                      M=S, tm=tm, tn=LANES)

    u, = row_call("glu", lambda a, gt: ([a * _sig(gt)], []),
                  [(proj, 'row', CD, 0, 0), (proj, 'row', CD, 1, 0)], [(CD, F32, CD, 0, 0)], M=S, tm=tm)

    def conv_a_epi(conv, ex):
        cb_, g_, b_ = ex
        ca = conv + cb_
        xhat, _ = _ln_stats(ca)
        la = xhat * g_ + b_
        return [ca, la * _sig(la)], []

    ca, sa = conv_call("conv_a", u, 0, sm['conv_a_w'], cf['KA'], conv_a_epi,
                       [(sm['conv_a_b'], 'vec', CD, 0, 0), (sm['ln_a_g'], 'vec', CD, 0, 0), (sm['ln_a_b'], 'vec', CD, 0, 0)],
                       [(CD, F32, CD, 0, 0), (CD, BF16, CD, 0, 0)], M=S, tm=cf['tmc'], cw=CD, nc=1, reverse=False)
    y_a, = fused_mm("a_out", [(sa, W['a_out'], 0, False)], [], ident, [(D, F32, D, 0)], M=S, tm=tm, tn=D)

    def conv_b_epi(conv, ex):
        cb = conv + ex[0]
        return [cb, cb * _sig(cb)], []

    xoff = (2 * CD + 2 * D + DI) // DI
    cbv, xbc = conv_call("conv_b", proj, xoff, sm['ssm_conv_w'], cf['KB'], conv_b_epi,
                         [(sm['ssm_conv_b'], 'vec', DI, 0, 1)],
                         [(XBC, F32, DI, 0, 1), (XBC, F32, DI, 0, 1)], M=S, tm=cf['tmc'], cw=DI, nc=XBC // DI,
                         reverse=False)
    y_f, st_f = ssd_fwd("ssd_fwd_f", xbc, dtraw, sm['dtb_f'], sm['alog_f'], S=S, DI=DI, G=G, H=H, rev=False)
    y_r, st_r = ssd_fwd("ssd_fwd_r", xbc, dtraw, sm['dtb_r'], sm['alog_r'], S=S, DI=DI, G=G, H=H, rev=True)

    def gate_norm(yf, yr, xs, z, dsk, ng):
        y = yf + yr + xs * dsk
        yz = y * (z * _sig(z))
        outs = []
        gw = cf['GW']
        for g in range(G):
            t = yz[:, g * gw:(g + 1) * gw]
            outs.append(t * lax.rsqrt(jnp.mean(t * t, axis=-1, keepdims=True) + RMS_EPS))
        return [jnp.concatenate(outs, axis=1) * ng], []

    zoff = (2 * CD + 2 * D) // DI
    yn, = row_call("gate_norm", gate_norm,
                   [(y_f, 'row', DI, 0, 0), (y_r, 'row', DI, 0, 0), (xbc, 'row', DI, 0, 0), (proj, 'row', DI, zoff, 0),
                    (sm['dskip_full'], 'vec', DI, 0, 0), (sm['ssm_norm_g'], 'vec', DI, 0, 0)],
                   [(DI, BF16, DI, 0, 0)], M=S, tm=cf['tmr'])
    y_b, = fused_mm("b_out", [(yn, W['b_out'], 0, False)], [], ident, [(D, F32, D, 0)], M=S, tm=tm, tn=D)

    goff = (2 * CD) // D
    merged, = row_call("merge", lambda ga, gb, ya, yb: ([_sig(ga) * ya + _sig(gb) * yb], []),
                       [(proj, 'row', D, goff, 0), (proj, 'row', D, goff + 1, 0), (y_a, 'row', D, 0, 0), (y_b, 'row', D, 0, 0)],
                       [(D, BF16, D, 0, 0)], M=S, tm=tm)

    def mix_epi(acc, ex):
        xin, g_, b_ = ex
        r1 = alpha * xin + acc
        xhat, _ = _ln_stats(r1)
        return [r1, xhat * g_ + b_], []

    r1, hb = fused_mm("o_mix", [(merged, W['o'], 0, False)],
                      [(x, 'row', D, 0), (sm['ln1_g'], 'vec', D, 0), (sm['ln1_b'], 'vec', D, 0)],
                      mix_epi, [(D, F32, D, 0), (D, BF16, D, 0)], M=S, tm=tm, tn=D)

    tnf = cf['tnf']

    g_ = fused_mm("ffn_gate", [(hb, W['gate_up'], 0, False)], [], ident, [(F, F32, tnf, 0)],
                  M=S, tm=tm, tn=tnf, nj=F // tnf)[0]
    u_ = fused_mm("ffn_up", [(hb, W['gate_up'], F // tnf, False)], [], ident, [(F, F32, tnf, 0)],
                  M=S, tm=tm, tn=tnf, nj=F // tnf)[0]
    f, = row_call("swiglu", lambda a, b: ([a * _sig(a) * b], []),
                  [(g_, 'row', tnf, 0, 1), (u_, 'row', tnf, 0, 1)], [(F, BF16, tnf, 0, 1)], M=S, tm=tm, nc=F // tnf)

    def down_epi(acc, ex):
        r1_, g1, b1, g2, b2 = ex
        xh1, _ = _ln_stats(r1_)
        r2 = alpha * (xh1 * g1 + b1) + acc
        xh2, _ = _ln_stats(r2)
        return [r2, xh2 * g2 + b2], []

    r2, h2b = fused_mm("ffn_down", [(f, W['down'], 0, False)],
                       [(r1, 'row', D, 0), (sm['ln1_g'], 'vec', D, 0), (sm['ln1_b'], 'vec', D, 0),
                        (sm['ln2_g'], 'vec', D, 0), (sm['ln2_b'], 'vec', D, 0)],
                       down_epi, [(D, F32, D, 0), (D, BF16, D, 0)], M=S, tm=tm, tn=D)

    t_, = fused_mm("ple_gate", [(h2b, W['ple_gate'], 0, False)], [], ident, [(D, F32, D, 0)], M=S, tm=tm, tn=D)
    pe, = fused_mm("ple_proj", [(pb, W['ple'], 0, False)], [], ident, [(D, F32, D, 0)], M=S, tm=tm, tn=D)

    def ple_mix(r2_, g2, b2, t, pe_, pg):
        xh2, _ = _ln_stats(r2_)
        h2 = xh2 * g2 + b2
        e = pe_ * lax.rsqrt(jnp.mean(pe_ * pe_, axis=-1, keepdims=True) + RMS_EPS) * pg
        xn = h2 + e * _sig(t)
        return [xn, xn], []

    xn, xnb = row_call("ple_mix", ple_mix,
                       [(r2, 'row', D, 0, 0), (sm['ln2_g'], 'vec', D, 0, 0), (sm['ln2_b'], 'vec', D, 0, 0),
                        (t_, 'row', D, 0, 0), (pe, 'row', D, 0, 0), (sm['ple_norm_g'], 'vec', D, 0, 0)],
                       [(D, F32, D, 0, 0), (D, BF16, D, 0, 0)], M=S, tm=tm)
    sv.update(x=x, xb=xb, pb=pb, proj=proj, dtraw=dtraw, u=u, ca=ca, sa=sa, y_a=y_a, cbv=cbv, xbc=xbc,
              y_f=y_f, y_r=y_r, st_f=st_f, st_r=st_r, yn=yn, y_b=y_b, merged=merged, r1=r1, hb=hb,
              g_=g_, u_=u_, f=f, r2=r2, h2b=h2b, t_=t_, pe=pe)
    return xn, xnb, sv


def _layer_bwd(cf, sv, W, sm, dxn=None, target=None, xn=None):
    S, D, CD, DI, XBC, F, H, G = cf['S'], cf['D'], cf['CD'], cf['DI'], cf['XBC'], cf['F'], cf['H'], cf['G']
    NM = cf['NM']
    alpha = cf['alpha']
    tm = cf['tm']
    gw = cf['GW']
    out = {}

    def ple_bwd_core(dx_, t, pe_, pg):
        s = _sig(t)
        rinv = lax.rsqrt(jnp.mean(pe_ * pe_, axis=-1, keepdims=True) + RMS_EPS)
        pn = pe_ * rinv
        e = pn * pg
        dtg = dx_ * e * (s * (1.0 - s))
        de = dx_ * s
        qv = de * pg
        dpe = rinv * (qv - pn * jnp.mean(qv * pn, axis=-1, keepdims=True))
        return dtg, dpe, de * pn

    if dxn is None:
        def head(xn_, tgt, t, pe_, pg):
            err = xn_ - tgt
            dx_ = err * (1.0 / D)
            dtg, dpe, dpg = ple_bwd_core(dx_, t, pe_, pg)
            return [dx_, dtg, dpe], [dpg, err * err]

        (dxn, dtg, dpe, dpg, lsq) = row_call(
            "loss_ple_bwd", head,
            [(xn, 'row', D, 0, 0), (target, 'row', D, 0, 0), (sv['t_'], 'row', D, 0, 0), (sv['pe'], 'row', D, 0, 0),
             (sm['ple_norm_g'], 'vec', D, 0, 0)],
            [(D, F32, D, 0, 0), (D, BF16, D, 0, 0), (D, BF16, D, 0, 0)], [(D, D, 0, 0), (D, D, 0, 0)], M=S, tm=tm)
        out['loss_sq'] = lsq
    else:
        def mid(dx_, t, pe_, pg):
            dtg, dpe, dpg = ple_bwd_core(dx_, t, pe_, pg)
            return [dtg, dpe], [dpg]

        (dtg, dpe, dpg) = row_call(
            "ple_bwd", mid,
            [(dxn, 'row', D, 0, 0), (sv['t_'], 'row', D, 0, 0), (sv['pe'], 'row', D, 0, 0),
             (sm['ple_norm_g'], 'vec', D, 0, 0)],
            [(D, BF16, D, 0, 0), (D, BF16, D, 0, 0)], [(D, D, 0, 0)], M=S, tm=tm)
    out['ple_norm_g'] = dpg

    def ln_bwd_epi(scale):
        def epi(acc, ex):
            res, r_, g_ = ex
            dh = scale * res + acc
            xhat, rstd = _ln_stats(r_)
            dr = _ln_bwd(dh, xhat, rstd, g_)
            return [dr, dr], [dh * xhat, dh]
        return epi

    dr2, dr2b, dg2, db2 = fused_mm(
        "dh2", [(dtg, W['ple_gate_T'], 0, False)],
        [(dxn, 'row', D, 0), (sv['r2'], 'row', D, 0), (sm['ln2_g'], 'vec', D, 0)],
        ln_bwd_epi(1.0), [(D, F32, D, 0), (D, BF16, D, 0)], [(D, D, 0), (D, D, 0)], M=S, tm=tm, tn=D)
    out['ln2_g'], out['ln2_b'] = dg2, db2

    tnf = cf['tnf']

    def dswiglu_epi(acc, ex):
        gg, uu = ex
        s = _sig(gg)
        return [acc * uu * _dsilu(gg, s), acc * (gg * s)], []

    dg_b, du_b = fused_mm(
        "d_down", [(dr2b, W['down_T'], 0, False)],
        [(sv['g_'], 'row', tnf, 0), (sv['u_'], 'row', tnf, 0)], dswiglu_epi,
        [(F, BF16, tnf, 0), (F, BF16, tnf, 0)], M=S, tm=tm, tn=tnf, nj=F // tnf)

    dr1, dr1b, dg1, db1 = fused_mm(
        "dh1", [(dg_b, W['gate_T'], 0, True), (du_b, W['up_T'], 0, True)],
        [(dr2, 'row', D, 0), (sv['r1'], 'row', D, 0), (sm['ln1_g'], 'vec', D, 0)],
        ln_bwd_epi(alpha), [(D, F32, D, 0), (D, BF16, D, 0)], [(D, D, 0), (D, D, 0)],
        M=S, tm=tm, tn=D, nk=cf['nk_f'])
    out['ln1_g'], out['ln1_b'] = dg1, db1

    goff = (2 * CD) // D

    def dmerge_epi(acc, ex):
        ga, gb, ya, yb = ex
        sa_, sb_ = _sig(ga), _sig(gb)
        dga = acc * ya * (sa_ * (1.0 - sa_))
        dgb = acc * yb * (sb_ * (1.0 - sb_))
        return [jnp.concatenate([dga, dgb], axis=1), acc * sa_, acc * sb_], []

    dproj, dya_b, dyb_b = fused_mm(
        "d_merge", [(dr1b, W['o_T'], 0, False)],
        [(sv['proj'], 'row', D, goff), (sv['proj'], 'row', D, goff + 1), (sv['y_a'], 'row', D, 0), (sv['y_b'], 'row', D, 0)],
        dmerge_epi, [(NM, BF16, 2 * D, (2 * CD) // (2 * D)), (D, BF16, D, 0), (D, BF16, D, 0)], M=S, tm=tm, tn=D)

    def dsa_epi(acc, ex):
        ca_, g_, b_ = ex
        xhat, rstd = _ln_stats(ca_)
        la = xhat * g_ + b_
        dla = acc * _dsilu(la, _sig(la))
        dca = _ln_bwd(dla, xhat, rstd, g_)
        return [dca], [dla * xhat, dla, dca]

    dca, dlag, dlab, dcab = fused_mm(
        "d_a_out", [(dya_b, W['a_out_T'], 0, False)],
        [(sv['ca'], 'row', CD, 0), (sm['ln_a_g'], 'vec', CD, 0), (sm['ln_a_b'], 'vec', CD, 0)],
        dsa_epi, [(CD, F32, CD, 0)], [(CD, CD, 0), (CD, CD, 0), (CD, CD, 0)], M=S, tm=tm, tn=D)
    out['ln_a_g'], out['ln_a_b'], out['conv_a_b'] = dlag, dlab, dcab

    def dglu_epi(du, ex):
        a, gt = ex
        s = _sig(gt)
        return [jnp.concatenate([du * s, du * a * (s * (1.0 - s))], axis=1)], []

    dproj, dwa = conv_call(
        "d_conv_a", dca, 0, sm['conv_a_w'], cf['KA'], dglu_epi,
        [(sv['proj'], 'row', CD, 0, 0), (sv['proj'], 'row', CD, 1, 0)],
        [(NM, BF16, 2 * CD, 0, 0)], M=S, tm=cf['tmc'], cw=CD, nc=1, reverse=True, xin=(sv['u'], 0),
        passthrough=(dproj, 0))
    out['conv_a_w'] = dwa

    zoff = (2 * CD + 2 * D) // DI

    def dgate_norm_epi(acc, ex):
        yf, yr, xs, z, dsk, ng = ex
        y = yf + yr + xs * dsk
        sz = _sig(z)
        siluz = z * sz
        yz = y * siluz
        dyzs, yhats = [], []
        for g in range(G):
            t = yz[:, g * gw:(g + 1) * gw]
            rinv = lax.rsqrt(jnp.mean(t * t, axis=-1, keepdims=True) + RMS_EPS)
            yh = t * rinv
            qv = acc[:, g * gw:(g + 1) * gw] * ng[:, g * gw:(g + 1) * gw]
            dyzs.append(rinv * (qv - yh * jnp.mean(qv * yh, axis=-1, keepdims=True)))
            yhats.append(yh)
        dyz = jnp.concatenate(dyzs, axis=1)
        yhat = jnp.concatenate(yhats, axis=1)
        dy = dyz * siluz
        dz = dyz * y * _dsilu(z, sz)
        return [dy, dz], [acc * yhat, dy * xs]

    tmr = cf['tmr']
    dy_ssd, dproj, dng, ddsk = fused_mm(
        "d_b_out", [(dyb_b, W['b_out_T'], 0, False)],
        [(sv['y_f'], 'row', DI, 0), (sv['y_r'], 'row', DI, 0), (sv['xbc'], 'row', DI, 0), (sv['proj'], 'row', DI, zoff),
         (sm['dskip_full'], 'vec', DI, 0), (sm['ssm_norm_g'], 'vec', DI, 0)],
        dgate_norm_epi, [(DI, F32, DI, 0), (NM, BF16, DI, zoff)], [(DI, DI, 0), (DI, DI, 0)],
        M=S, tm=tmr, tn=DI, passthrough=(dproj, 1))
    out['ssm_norm_g'], out['dskip_full'] = dng, ddsk

    dxbc_f, ddt_f, dA_f = ssd_bwd("ssd_bwd_f", sv['xbc'], sv['dtraw'], dy_ssd, sv['st_f'], sm['dtb_f'], sm['alog_f'],
                                  cf['headsel'], S=S, DI=DI, G=G, H=H, rev=False)
    dxbc_r, ddt_r, dA_r = ssd_bwd("ssd_bwd_r", sv['xbc'], sv['dtraw'], dy_ssd, sv['st_r'], sm['dtb_r'], sm['alog_r'],
                                  cf['headsel'], S=S, DI=DI, G=G, H=H, rev=True)
    out['dA_f'], out['dA_r'] = dA_f, dA_r

    def dxbc_sum(a, b, dy, dsk, cbv):
        cj = pl.program_id(0)
        skip = jnp.where(cj == 0, 1.0, 0.0)
        d = a + b + (dy * dsk) * skip
        dcb = d * _dsilu(cbv, _sig(cbv))
        return [dcb], [dcb]

    dcb, dcbb = row_call(
        "d_xbc", dxbc_sum,
        [(dxbc_f, 'row', DI, 0, 1), (dxbc_r, 'row', DI, 0, 1), (dy_ssd, 'row', DI, 0, 0),
         (sm['dskip_full'], 'vec', DI, 0, 0), (sv['cbv'], 'row', DI, 0, 1)],
        [(XBC, F32, DI, 0, 1)], [(XBC, DI, 0, 1)], M=S, tm=tmr, nc=XBC // DI)
    out['ssm_conv_b'] = dcbb

    xoff = (2 * CD + 2 * D + DI) // DI
    dproj, dwb = conv_call(
        "d_conv_b", dcb, 0, sm['ssm_conv_w'], cf['KB'], lambda conv, ex: ([conv], []), [],
        [(NM, BF16, DI, xoff, 1)], M=S, tm=cf['tmc'], cw=DI, nc=XBC // DI, reverse=True, xin=(sv['proj'], xoff),
        passthrough=(dproj, 0))
    out['ssm_conv_w'] = dwb

    ddtb, ddt_bias = row_call("d_dt", lambda a, b: ([a + b], [a + b]),
                              [(ddt_f, 'row', LANES, 0, 0), (ddt_r, 'row', LANES, 0, 0)],
                              [(LANES, BF16, LANES, 0, 0)], [(LANES, LANES, 0, 0)], M=S, tm=tm)
    out['dt_bias'] = ddt_bias

    dx, = fused_mm("d_x", [(dproj, W['in_main_T'], 0, True), (ddtb, W['in_dt_T'], 0, False)],
                   [(dr1, 'row', D, 0)], lambda acc, ex: ([alpha * ex[0] + acc], []),
                   [(D, F32, D, 0)], M=S, tm=tm, tn=D, nk=cf['nk_in'])

    tmw = cf['tmw']
    xb = sv['xb']
    out['w_in'] = jnp.concatenate(
        [mm_tn("dw_in", xb, dproj, tm=tmw, tk=D, tn=cf['tn_in']),
         mm_tn("dw_dt", xb, ddtb, tm=tmw, tk=D, tn=LANES)[:, :2 * H]], axis=1)
    out['w_a_out'] = mm_tn("dw_a_out", sv['sa'], dya_b, tm=tmw, tk=CD, tn=D)
    out['w_b_out'] = mm_tn("dw_b_out", sv['yn'], dyb_b, tm=tmw, tk=DI // 2, tn=D)
    out['w_o'] = mm_tn("dw_o", sv['merged'], dr1b, tm=tmw, tk=D, tn=D)
    out['w_gate_up'] = jnp.concatenate(
        [mm_tn("dw_gate", sv['hb'], dg_b, tm=tmw, tk=D, tn=tnf),
         mm_tn("dw_up", sv['hb'], du_b, tm=tmw, tk=D, tn=tnf)], axis=1)
    out['w_down'] = mm_tn("dw_down", sv['f'], dr2b, tm=tmw, tk=tnf, tn=D)
    out['w_ple'] = mm_tn("dw_ple", sv['pb'], dpe, tm=tmw, tk=sv['pb'].shape[1], tn=D)
    out['w_ple_gate'] = mm_tn("dw_ple_gate", sv['h2b'], dtg, tm=tmw, tk=D, tn=D)
    return dx, out


_WEIGHTS = ['w_in', 'conv_a_w', 'conv_a_b', 'ln_a_g', 'ln_a_b', 'w_a_out', 'ssm_conv_w', 'ssm_conv_b', 'a_log',
            'dt_bias', 'd_skip', 'ssm_norm_g', 'w_b_out', 'w_o', 'ln1_g', 'ln1_b', 'w_gate_up', 'w_down', 'ln2_g',
            'ln2_b', 'w_ple', 'ple_norm_g', 'w_ple_gate']
_COL_SHARDED = ['w_in', 'conv_a_w', 'ssm_conv_w', 'w_gate_up', 'w_ple']
_ROW_SHARDED = ['w_a_out', 'w_b_out', 'w_o', 'w_down', 'w_ple_gate']
_BIG = _COL_SHARDED + _ROW_SHARDED
_SMALL = [n for n in _WEIGHTS if n not in _BIG]
_CONV = ['conv_a_w', 'ssm_conv_w']


def _ceil_to(n, k):
    return -(-n // k) * k


def kernel(x, p, w_in, conv_a_w, conv_a_b, ln_a_g, ln_a_b, w_a_out, ssm_conv_w, ssm_conv_b, a_log, dt_bias, d_skip, ssm_norm_g, w_b_out, w_o, ln1_g, ln1_b, w_gate_up, w_down, ln2_g, ln2_b, w_ple, ple_norm_g, w_ple_gate, loss_target, m_w_in, m_conv_a_w, m_conv_a_b, m_ln_a_g, m_ln_a_b, m_w_a_out, m_ssm_conv_w, m_ssm_conv_b, m_a_log, m_dt_bias, m_d_skip, m_ssm_norm_g, m_w_b_out, m_w_o, m_ln1_g, m_ln1_b, m_w_gate_up, m_w_down, m_ln2_g, m_ln2_b, m_w_ple, m_ple_norm_g, m_w_ple_gate, v_w_in, v_conv_a_w, v_conv_a_b, v_ln_a_g, v_ln_a_b, v_w_a_out, v_ssm_conv_w, v_ssm_conv_b, v_a_log, v_dt_bias, v_d_skip, v_ssm_norm_g, v_w_b_out, v_w_o, v_ln1_g, v_ln1_b, v_w_gate_up, v_w_down, v_ln2_g, v_ln2_b, v_w_ple, v_ple_norm_g, v_w_ple_gate):
    wt = dict(w_in=w_in, conv_a_w=conv_a_w, conv_a_b=conv_a_b, ln_a_g=ln_a_g, ln_a_b=ln_a_b, w_a_out=w_a_out,
              ssm_conv_w=ssm_conv_w, ssm_conv_b=ssm_conv_b, a_log=a_log, dt_bias=dt_bias, d_skip=d_skip,
              ssm_norm_g=ssm_norm_g, w_b_out=w_b_out, w_o=w_o, ln1_g=ln1_g, ln1_b=ln1_b, w_gate_up=w_gate_up,
              w_down=w_down, ln2_g=ln2_g, ln2_b=ln2_b, w_ple=w_ple, ple_norm_g=ple_norm_g, w_ple_gate=w_ple_gate)
    mo = dict(w_in=m_w_in, conv_a_w=m_conv_a_w, conv_a_b=m_conv_a_b, ln_a_g=m_ln_a_g, ln_a_b=m_ln_a_b,
              w_a_out=m_w_a_out, ssm_conv_w=m_ssm_conv_w, ssm_conv_b=m_ssm_conv_b, a_log=m_a_log,
              dt_bias=m_dt_bias, d_skip=m_d_skip, ssm_norm_g=m_ssm_norm_g, w_b_out=m_w_b_out, w_o=m_w_o,
              ln1_g=m_ln1_g, ln1_b=m_ln1_b, w_gate_up=m_w_gate_up, w_down=m_w_down, ln2_g=m_ln2_g, ln2_b=m_ln2_b,
              w_ple=m_w_ple, ple_norm_g=m_ple_norm_g, w_ple_gate=m_w_ple_gate)
    vo = dict(w_in=v_w_in, conv_a_w=v_conv_a_w, conv_a_b=v_conv_a_b, ln_a_g=v_ln_a_g, ln_a_b=v_ln_a_b,
              w_a_out=v_w_a_out, ssm_conv_w=v_ssm_conv_w, ssm_conv_b=v_ssm_conv_b, a_log=v_a_log,
              dt_bias=v_dt_bias, d_skip=v_d_skip, ssm_norm_g=v_ssm_norm_g, w_b_out=v_w_b_out, w_o=v_w_o,
              ln1_g=v_ln1_g, ln1_b=v_ln1_b, w_gate_up=v_w_gate_up, w_down=v_w_down, ln2_g=v_ln2_g, ln2_b=v_ln2_b,
              w_ple=v_w_ple, ple_norm_g=v_ple_norm_g, w_ple_gate=v_w_ple_gate)

    L = w_in.shape[0]
    S, D = x.shape[1], x.shape[2]
    CD = conv_a_b.shape[1]
    DI = ssm_norm_g.shape[1]
    XBC = ssm_conv_b.shape[1]
    H = d_skip.shape[1]
    G = (XBC - DI) // (2 * D_STATE)
    F = w_down.shape[1] * 4
    N_IN = w_in.shape[2] * 4
    NM = N_IN - 2 * H
    KA, KB = conv_a_w.shape[1], ssm_conv_w.shape[1]
    assert DI == H * HEAD_DIM and CD == D and DI == 2 * D and XBC == 2 * DI and NM == 2 * CD + 2 * D + DI + XBC
    assert 2 * H <= LANES and S % CHUNK == 0
    tnf = F // 2
    cf = dict(S=S, D=D, CD=CD, DI=DI, XBC=XBC, F=F, H=H, G=G, NM=NM, KA=KA, KB=KB, GW=(H // G) * HEAD_DIM,
              alpha=float((2 * L) ** 0.25), tm=min(512, S), tmc=min(256, S), tmr=min(256, S), tmw=min(1024, S),
              tn_in=D, tnf=tnf, nk_f=2, nk_in=NM // DI)
    cf['headsel'] = (lax.broadcasted_iota(jnp.int32, (DI, LANES), 0) // HEAD_DIM
                     == lax.broadcasted_iota(jnp.int32, (DI, LANES), 1)).astype(BF16)

    send = [[wt[n] if n in _CONV else wt[n].astype(BF16)] for n in _BIG]
    got = chip_exchange("gather_weights", send, gather=True)
    full = {}
    for n, g in zip(_BIG, got):
        if n in _COL_SHARDED:
            full[n] = g.transpose(1, 2, 0, 3).reshape(L, g.shape[2], 4 * g.shape[3])
        else:
            full[n] = g.transpose(1, 0, 2, 3).reshape(L, 4 * g.shape[2], g.shape[3])

    def layer_weights(l):
        win = full['w_in'][l]
        in_main = win[:, :NM]
        in_dt = _pad_lanes(win[:, NM:])
        gu = full['w_gate_up'][l]
        W = dict(in_main=in_main, in_dt=in_dt, in_main_T=in_main.T, in_dt_T=in_dt.T,
                 a_out=full['w_a_out'][l], a_out_T=full['w_a_out'][l].T,
                 b_out=full['w_b_out'][l], b_out_T=full['w_b_out'][l].T,
                 o=full['w_o'][l], o_T=full['w_o'][l].T, gate_up=gu, gate_T=gu[:, :F].T, up_T=gu[:, F:].T,
                 down=full['w_down'][l], down_T=full['w_down'][l].T, ple=full['w_ple'][l],
                 ple_gate=full['w_ple_gate'][l], ple_gate_T=full['w_ple_gate'][l].T)
        row = lambda v: v.reshape(1, -1)
        sm = dict(conv_a_w=jnp.pad(full['conv_a_w'][l], ((0, _ceil_to(KA, SUBLANES) - KA), (0, 0))),
                  ssm_conv_w=jnp.pad(full['ssm_conv_w'][l], ((0, _ceil_to(KB, SUBLANES) - KB), (0, 0))),
                  conv_a_b=row(conv_a_b[l]), ln_a_g=row(ln_a_g[l]), ln_a_b=row(ln_a_b[l]),
                  ssm_conv_b=row(ssm_conv_b[l]), ssm_norm_g=row(ssm_norm_g[l]),
                  ln1_g=row(ln1_g[l]), ln1_b=row(ln1_b[l]), ln2_g=row(ln2_g[l]), ln2_b=row(ln2_b[l]),
                  ple_norm_g=row(ple_norm_g[l]),
                  dtb_f=_pad_lanes(row(dt_bias[l, 0])), dtb_r=_pad_lanes(row(dt_bias[l, 1])),
                  alog_f=_pad_lanes(row(a_log[l, 0])), alog_r=_pad_lanes(row(a_log[l, 1])),
                  dskip_full=row(jnp.repeat(d_skip[l], HEAD_DIM)))
        return W, sm

    lw = [layer_weights(l) for l in range(L)]
    xl = x[0]
    xlb = xl.astype(BF16)
    saved = []
    for l in range(L):
        xl, xlb, sv = _layer_fwd(cf, xl, xlb, p[l, 0].astype(BF16), lw[l][0], lw[l][1])
        saved.append(sv)
    grads = [None] * L
    dxl = None
    for l in reversed(range(L)):
        if l == L - 1:
            dxl, grads[l] = _layer_bwd(cf, saved[l], lw[l][0], lw[l][1], target=loss_target[0], xn=xl)
        else:
            dxl, grads[l] = _layer_bwd(cf, saved[l], lw[l][0], lw[l][1], dxn=dxl)
    loss = lax.psum(0.5 / D * jnp.sum(grads[L - 1]['loss_sq']), ("x", "y", "c"))
    grad_x = dxl[None]

    def blocks(n, l):
        g = grads[l][n]
        if n == 'conv_a_w':
            g = g.sum(axis=1)[:KA]
        elif n == 'ssm_conv_w':
            g = g.sum(axis=1)[:KB]
        if n in _COL_SHARDED:
            return g.reshape(g.shape[0], 4, g.shape[1] // 4).transpose(1, 0, 2)
        return g.reshape(4, g.shape[0] // 4, g.shape[1])

    parts = chip_exchange("scatter_grads", [[blocks(n, l) for l in range(L)] for n in _BIG], gather=False)
    chip_sums = [sum_chips("chip_sum_" + n, pr.reshape(4, L * pr.shape[2], pr.shape[3])) for n, pr in zip(_BIG, parts)]
    sib_sums = sibling_swap("core_swap", chip_sums)
    res = {}
    for n, mine, sib in zip(_BIG, chip_sums, sib_sums):
        shp = wt[n].shape
        flat = lambda a: a.reshape(shp[0] * shp[1], shp[2])
        outs = adamw_shard("adamw_" + n, mine, sib, flat(wt[n]), flat(mo[n]), flat(vo[n]))
        res[n] = [o.reshape(shp) for o in outs]

    def small_pieces(l):
        gl = grads[l]
        A = -jnp.exp(a_log[l])
        d = dict(gl)
        d['a_log'] = jnp.concatenate([gl['dA_f'][:, :H] * A[0], gl['dA_r'][:, :H] * A[1]], axis=1)
        d['dt_bias'] = gl['dt_bias'][:, :2 * H]
        d['d_skip'] = gl['dskip_full'].reshape(SUBLANES, H, HEAD_DIM).sum(axis=-1)
        return [_pad_lanes(d[n], _ceil_to(d[n].shape[1], LANES)) for n in _SMALL]

    widths = [_ceil_to(math.prod(wt[n].shape[1:]), LANES) for n in _SMALL]
    packed = jnp.concatenate([pc for l in range(L) for pc in small_pieces(l)], axis=1)
    gathered = all8_gather("gather_small", packed)

    def pack_params(src):
        return jnp.concatenate([_pad_lanes(src[n][l].reshape(1, -1), wd) for l in range(L) for n, wd in zip(_SMALL, widths)],
                               axis=1)

    small_out = adamw_small("adamw_small", gathered, pack_params(wt), pack_params(mo), pack_params(vo))
    off = 0
    per = {n: [[] for _ in range(4)] for n in _SMALL}
    for l in range(L):
        for n, wd in zip(_SMALL, widths):
            size = math.prod(wt[n].shape[1:])
            for k in range(4):
                per[n][k].append(small_out[k][0, off:off + size].reshape(wt[n].shape[1:]))
            off += wd
    for n in _SMALL:
        res[n] = [jnp.stack(per[n][k]) for k in range(4)]

    return (loss, grad_x, *[res[n][0] for n in _WEIGHTS], *[res[n][1] for n in _WEIGHTS],
            *[res[n][2] for n in _WEIGHTS], *[res[n][3] for n in _WEIGHTS])
```

```python
import math

import jax
import jax.numpy as jnp
from jax import lax
from jax.experimental import pallas as pl
from jax.experimental.pallas import tpu as pltpu

F32 = jnp.float32
BF16 = jnp.bfloat16

VMEM_LIMIT_BYTES = 56 * 1024 * 1024
LANES = 128
SUBLANES = 8

CHUNK = 128
D_STATE = 128
HEAD_DIM = 64
LN_EPS = 1e-5
RMS_EPS = 1e-6
ADAM_LR = 0.001
ADAM_B1 = 0.9
ADAM_B2 = 0.999
ADAM_EPS = 1e-08
ADAM_WD = 0.01
ADAM_STEP = 10
HALO = 16
MESH = pl.DeviceIdType.MESH


def _params(**kw):
    return pltpu.CompilerParams(vmem_limit_bytes=VMEM_LIMIT_BYTES, **kw)


def _sig(x):
    return jax.nn.sigmoid(x)


def _dsilu(x, s):
    return s * (1.0 + x * (1.0 - s))


def _ln_stats(r):
    mu = jnp.mean(r, axis=-1, keepdims=True)
    xc = r - mu
    var = jnp.mean(xc * xc, axis=-1, keepdims=True)
    rstd = lax.rsqrt(var + LN_EPS)
    return xc * rstd, rstd


def _ln_bwd(dy, xhat, rstd, g):
    dxh = dy * g
    m1 = jnp.mean(dxh, axis=-1, keepdims=True)
    m2 = jnp.mean(dxh * xhat, axis=-1, keepdims=True)
    return rstd * (dxh - m1 - xhat * m2)


def _rows8(v):
    tm, w = v.shape
    return v.reshape(tm // SUBLANES, SUBLANES, w).sum(axis=0)


def fused_mm(name, prods, extras, epi, row_outs, col_outs=(), *, M, tm, tn, nj=1, nk=1,
             passthrough=None):
    np_ = len(prods)
    ne = len(extras)
    nro = len(row_outs)
    nco = len(col_outs)
    use_acc = nk > 1

    def body(*refs):
        a_refs = [refs[2 * p] for p in range(np_)]
        w_refs = [refs[2 * p + 1] for p in range(np_)]
        pos = 2 * np_
        e_refs = refs[pos:pos + ne]
        pos += ne
        if passthrough is not None:
            pos += 1
        ro_refs = refs[pos:pos + nro]
        pos += nro
        co_refs = refs[pos:pos + nco]
        pos += nco
        acc_ref = refs[pos] if use_acc else None
        i = pl.program_id(1)
        k = pl.program_id(2)

        def prod(p):
            a = a_refs[p][...]
            if a.dtype != BF16:
                a = a.astype(BF16)
            return jnp.dot(a, w_refs[p][...], preferred_element_type=F32)

        def finish(acc):
            rows, cols = epi(acc, [r[...] for r in e_refs])
            for v, o in zip(rows, ro_refs):
                o[...] = v.astype(o.dtype)
            for v, o in zip(cols, co_refs):
                v8 = _rows8(v)

                @pl.when(i == 0)
                def _():
                    o[...] = v8

                @pl.when(i > 0)
                def _():
                    o[...] += v8

        if not use_acc:
            acc = prod(0)
            for p in range(1, np_):
                acc = acc + prod(p)
            finish(acc)
        else:
            @pl.when(k == 0)
            def _():
                acc = None
                for p in range(np_):
                    acc = prod(p) if acc is None else acc + prod(p)
                acc_ref[...] = acc

            @pl.when(k > 0)
            def _():
                acc = None
                for p in range(np_):
                    if prods[p][3]:
                        acc = prod(p) if acc is None else acc + prod(p)
                acc_ref[...] += acc

            @pl.when(k == nk - 1)
            def _():
                finish(acc_ref[...])

    in_specs = []
    args = []
    for a, w, joff, ksplit in prods:
        K = a.shape[1]
        if ksplit:
            tk = K // nk
            in_specs.append(pl.BlockSpec((tm, tk), lambda j, i, k: (i, k)))
            in_specs.append(pl.BlockSpec((tk, tn), lambda j, i, k, joff=joff: (k, j + joff)))
        else:
            in_specs.append(pl.BlockSpec((tm, K), lambda j, i, k: (i, 0)))
            in_specs.append(pl.BlockSpec((K, tn), lambda j, i, k, joff=joff: (0, j + joff)))
        args += [a, w]
    for arr, kind, width, c0 in extras:
        if kind == 'row':
            in_specs.append(pl.BlockSpec((tm, width), lambda j, i, k, c0=c0: (i, c0 + j)))
        else:
            in_specs.append(pl.BlockSpec((arr.shape[0], width), lambda j, i, k, c0=c0: (0, c0 + j)))
        args.append(arr)
    aliases = {}
    if passthrough is not None:
        arr, oidx = passthrough
        in_specs.append(pl.BlockSpec(memory_space=pl.ANY))
        aliases = {len(args): oidx}
        args.append(arr)
    out_shape = []
    out_specs = []
    for n_total, dtype, width, c0 in row_outs:
        out_shape.append(jax.ShapeDtypeStruct((M, n_total), dtype))
        out_specs.append(pl.BlockSpec((tm, width), lambda j, i, k, c0=c0: (i, c0 + j)))
    for n_total, width, c0 in col_outs:
        out_shape.append(jax.ShapeDtypeStruct((SUBLANES, n_total), F32))
        out_specs.append(pl.BlockSpec((SUBLANES, width), lambda j, i, k, c0=c0: (0, c0 + j)))
    scratch = [pltpu.VMEM((tm, tn), F32)] if use_acc else []
    return pl.pallas_call(
        body, name=name, grid=(nj, M // tm, nk), in_specs=in_specs, out_specs=out_specs,
        out_shape=out_shape, scratch_shapes=scratch, input_output_aliases=aliases,
        compiler_params=_params(dimension_semantics=("arbitrary", "arbitrary", "arbitrary")),
    )(*args)


def mm_tn(name, a, b, *, tm, tk, tn):
    M, K = a.shape
    N = b.shape[1]

    def body(a_ref, b_ref, o_ref):
        m = pl.program_id(2)
        p = lax.dot_general(a_ref[...], b_ref[...], (((0,), (0,)), ((), ())),
                            preferred_element_type=F32)

        @pl.when(m == 0)
        def _():
            o_ref[...] = p

        @pl.when(m > 0)
        def _():
            o_ref[...] += p

    return pl.pallas_call(
        body, name=name, grid=(K // tk, N // tn, M // tm),
        in_specs=[pl.BlockSpec((tm, tk), lambda kk, j, m: (m, kk)),
                  pl.BlockSpec((tm, tn), lambda kk, j, m: (m, j))],
        out_specs=pl.BlockSpec((tk, tn), lambda kk, j, m: (kk, j)),
        out_shape=jax.ShapeDtypeStruct((K, N), F32),
        compiler_params=_params(dimension_semantics=("arbitrary", "arbitrary", "arbitrary")),
    )(a, b)


def row_call(name, fn, ins, row_outs, col_outs=(), *, M, tm, nc=1):
    ni = len(ins)
    nro = len(row_outs)

    def body(*refs):
        i = pl.program_id(1)
        vals = [r[...] for r in refs[:ni]]
        rows, cols = fn(*vals)
        for v, o in zip(rows, refs[ni:ni + nro]):
            o[...] = v.astype(o.dtype)
        for v, o in zip(cols, refs[ni + nro:]):
            v8 = _rows8(v)

            @pl.when(i == 0)
            def _():
                o[...] = v8

            @pl.when(i > 0)
            def _():
                o[...] += v8

    in_specs = []
    for arr, kind, width, c0, cmul in ins:
        if kind == 'row':
            in_specs.append(pl.BlockSpec((tm, width), lambda cj, i, c0=c0, cmul=cmul: (i, c0 + cmul * cj)))
        else:
            in_specs.append(pl.BlockSpec((arr.shape[0], width), lambda cj, i, c0=c0, cmul=cmul: (0, c0 + cmul * cj)))
    out_shape = []
    out_specs = []
    for n_total, dtype, width, c0, cmul in row_outs:
        out_shape.append(jax.ShapeDtypeStruct((M, n_total), dtype))
        out_specs.append(pl.BlockSpec((tm, width), lambda cj, i, c0=c0, cmul=cmul: (i, c0 + cmul * cj)))
    for n_total, width, c0, cmul in col_outs:
        out_shape.append(jax.ShapeDtypeStruct((SUBLANES, n_total), F32))
        out_specs.append(pl.BlockSpec((SUBLANES, width), lambda cj, i, c0=c0, cmul=cmul: (0, c0 + cmul * cj)))
    return pl.pallas_call(
        body, name=name, grid=(nc, M // tm), in_specs=in_specs, out_specs=out_specs,
        out_shape=out_shape,
        compiler_params=_params(dimension_semantics=("arbitrary", "arbitrary")),
    )(*[a[0] for a in ins])


def conv_call(name, src, src_c0, w, K, epi, extras, row_outs, col_outs=(), *, M, tm, cw, nc,
              reverse, xin=None, passthrough=None):
    pad = (K - 1) // 2
    assert pad <= HALO - 1
    R = tm // HALO
    nblk = M // HALO
    n_i = M // tm
    Kp = w.shape[0]
    ne = len(extras)
    nro = len(row_outs)
    nco = len(col_outs)
    rb = 64
    cbw = min(cw, 256)
    n_copies = SUBLANES if K > SUBLANES else 1

    def body(*refs):
        main_ref, prev_ref, next_ref, w_ref = refs[:4]
        pos = 4
        xin_ref = None
        if xin is not None:
            xin_ref = refs[pos]
            pos += 1
        e_refs = refs[pos:pos + ne]
        pos += ne
        if passthrough is not None:
            pos += 1
        ro_refs = refs[pos:pos + nro]
        pos += nro
        co_refs = refs[pos:pos + nco]
        pos += nco
        dw_ref = None
        if xin is not None:
            dw_ref = refs[pos]
            pos += 1
        ext_ref, conv_ref = refs[pos], refs[pos + 1]
        i = pl.program_id(1)

        ext_ref[0, 0:HALO, :] = jnp.where(i == 0, 0.0, prev_ref[...].astype(F32))
        ext_ref[0, HALO:HALO + tm, :] = main_ref[...].astype(F32)
        ext_ref[0, HALO + tm:, :] = jnp.where(i == n_i - 1, 0.0, next_ref[...].astype(F32))
        if dw_ref is not None:
            @pl.when(i == 0)
            def _():
                dw_ref[...] = jnp.zeros_like(dw_ref)

        n_sh = tm + 2 * HALO - SUBLANES
        for c0 in range(0, cw, cbw):
            for sft in range(1, n_copies):
                ext_ref[sft, 0:n_sh, c0:c0 + cbw] = ext_ref[0, sft:sft + n_sh, c0:c0 + cbw]

        for c0 in range(0, cw, cbw):
            for r0 in range(0, tm, rb):
                acc = jnp.zeros((rb, cbw), F32)
                if xin_ref is not None:
                    xblk = xin_ref[r0:r0 + rb, c0:c0 + cbw].astype(F32)
                for k in range(K):
                    off = HALO + r0 + ((pad - k) if reverse else (k - pad))
                    sft = off % SUBLANES if n_copies > 1 else 0
                    d = ext_ref[sft, off - sft:off - sft + rb, c0:c0 + cbw]
                    acc = acc + d * w_ref[k:k + 1, c0:c0 + cbw]
                    if xin_ref is not None:
                        dw_ref[k, :, c0:c0 + cbw] += _rows8(xblk * d)
                conv_ref[r0:r0 + rb, c0:c0 + cbw] = acc

        rows, cols = epi(conv_ref[...], [r[...] for r in e_refs])
        for v, o in zip(rows, ro_refs):
            o[...] = v.astype(o.dtype)
        for v, o in zip(cols, co_refs):
            v8 = _rows8(v)

            @pl.when(i == 0)
            def _():
                o[...] = v8

            @pl.when(i > 0)
            def _():
                o[...] += v8

    in_specs = [
        pl.BlockSpec((tm, cw), lambda cj, i: (i, src_c0 + cj)),
        pl.BlockSpec((HALO, cw), lambda cj, i: (jnp.maximum(i * R - 1, 0), src_c0 + cj)),
        pl.BlockSpec((HALO, cw), lambda cj, i: (jnp.minimum((i + 1) * R, nblk - 1), src_c0 + cj)),
        pl.BlockSpec((Kp, cw), lambda cj, i: (0, cj)),
    ]
    args = [src, src, src, w]
    if xin is not None:
        in_specs.append(pl.BlockSpec((tm, cw), lambda cj, i, c0=xin[1]: (i, c0 + cj)))
        args.append(xin[0])
    for arr, kind, width, c0, cmul in extras:
        if kind == 'row':
            in_specs.append(pl.BlockSpec((tm, width), lambda cj, i, c0=c0, cmul=cmul: (i, c0 + cmul * cj)))
        else:
            in_specs.append(pl.BlockSpec((arr.shape[0], width), lambda cj, i, c0=c0, cmul=cmul: (0, c0 + cmul * cj)))
        args.append(arr)
    aliases = {}
    if passthrough is not None:
        in_specs.append(pl.BlockSpec(memory_space=pl.ANY))
        aliases = {len(args): passthrough[1]}
        args.append(passthrough[0])
    out_shape = []
    out_specs = []
    for n_total, dtype, width, c0, cmul in row_outs:
        out_shape.append(jax.ShapeDtypeStruct((M, n_total), dtype))
        out_specs.append(pl.BlockSpec((tm, width), lambda cj, i, c0=c0, cmul=cmul: (i, c0 + cmul * cj)))
    for n_total, width, c0, cmul in col_outs:
        out_shape.append(jax.ShapeDtypeStruct((SUBLANES, n_total), F32))
        out_specs.append(pl.BlockSpec((SUBLANES, width), lambda cj, i, c0=c0, cmul=cmul: (0, c0 + cmul * cj)))
    if xin is not None:
        out_shape.append(jax.ShapeDtypeStruct((Kp, SUBLANES, cw * nc), F32))
        out_specs.append(pl.BlockSpec((Kp, SUBLANES, cw), lambda cj, i: (0, 0, cj)))
    return pl.pallas_call(
        body, name=name, grid=(nc, n_i), in_specs=in_specs, out_specs=out_specs,
        out_shape=out_shape, input_output_aliases=aliases,
        scratch_shapes=[pltpu.VMEM((n_copies, tm + 2 * HALO, cw), F32), pltpu.VMEM((tm, cw), F32)],
        compiler_params=_params(dimension_semantics=("arbitrary", "arbitrary")),
    )(*args)


def _split_dot(m_bf16, v, n_pass, dims=None):
    out = None
    rest = v
    for p in range(n_pass):
        piece = rest.astype(BF16)
        if p + 1 < n_pass:
            rest = rest - piece.astype(F32)
        if dims is None:
            t = jnp.dot(m_bf16, piece, preferred_element_type=F32)
        else:
            t = lax.dot_general(m_bf16, piece, dims, preferred_element_type=F32)
        out = t if out is None else out + t
    return out


def _split_dot_r(v, m_bf16, n_pass):
    out = None
    rest = v
    for p in range(n_pass):
        piece = rest.astype(BF16)
        if p + 1 < n_pass:
            rest = rest - piece.astype(F32)
        t = jnp.dot(piece, m_bf16, preferred_element_type=F32)
        out = t if out is None else out + t
    return out


def _softplus(x):
    return jnp.maximum(x, 0.0) + jnp.log1p(jnp.exp(-jnp.abs(x)))


NT_DIMS = (((1,), (1,)), ((), ()))
TN_DIMS = (((0,), (0,)), ((), ()))


def _ssd_common(dtraw, dtbT, alogT, rev, n_heads):
    L = CHUNK
    if rev:
        dtraw = pltpu.roll(dtraw, LANES - n_heads, 1)
    preT = dtraw.T + dtbT
    dtT = _softplus(preT)
    AT = -jnp.exp(alogT)
    aT = dtT * AT
    ri = lax.broadcasted_iota(jnp.int32, (L, L), 0)
    ci = lax.broadcasted_iota(jnp.int32, (L, L), 1)
    up = (ri >= ci) if rev else (ri <= ci)
    lo = (ri <= ci) if rev else (ri >= ci)
    csT = _split_dot_r(aT, up.astype(BF16), 3)
    last = 0 if rev else L - 1
    lastB = jnp.broadcast_to(csT[:, last:last + 1], (L, L))
    return dict(preT=preT, dtT=dtT, AT=AT, csT=csT, cs=csT.T, up=up, lo=lo, ci=ci, last=last,
                doutT=jnp.exp(csT), dstT=jnp.exp(lastB - csT), totB=jnp.exp(lastB))


def ssd_fwd(name, xbc, dtraw, dtbT, alogT, *, S, DI, G, H, rev):
    NC = S // CHUNK
    R = H // G
    GW = R * HEAD_DIM
    N = D_STATE
    XBC = xbc.shape[1]
    P = HEAD_DIM

    def body(xbc_ref, dtraw_ref, dtb_ref, alog_ref, y_ref, st_ref, h_ref):
        c = pl.program_id(0)

        @pl.when(c == 0)
        def _():
            h_ref[...] = jnp.zeros_like(h_ref)

        q = _ssd_common(dtraw_ref[...], dtb_ref[...], alog_ref[...], rev, H)
        cs, csT, dtT, doutT, totB = q['cs'], q['csT'], q['dtT'], q['doutT'], q['totB']
        wstT = q['dstT'] * dtT
        for g in range(G):
            Bg = xbc_ref[:, DI + g * N:DI + (g + 1) * N].astype(BF16)
            Cg = xbc_ref[:, DI + G * N + g * N:DI + G * N + (g + 1) * N].astype(BF16)
            CBT = lax.dot_general(Bg, Cg, NT_DIMS, preferred_element_type=F32)
            HT = h_ref[g]
            yoffT = lax.dot_general(HT.astype(BF16), Cg, NT_DIMS, preferred_element_type=F32)
            xT = xbc_ref[:, g * GW:(g + 1) * GW].T
            ys, xws, tots = [], [], []
            for r in range(R):
                h = g * R + r
                segT = jnp.where(q['up'], csT[h:h + 1, :] - cs[:, h:h + 1], -1e30)
                GT = (CBT * jnp.exp(segT)).astype(BF16)
                xTh = xT[r * P:(r + 1) * P, :]
                XTh = (xTh * dtT[h:h + 1, :]).astype(BF16)
                ydT = jnp.dot(XTh, GT, preferred_element_type=F32)
                ys.append(ydT + yoffT[r * P:(r + 1) * P, :] * doutT[h:h + 1, :])
                xws.append(xTh * wstT[h:h + 1, :])
                tots.append(jnp.broadcast_to(totB[h:h + 1, :], (P, N)))
            y_ref[:, g * GW:(g + 1) * GW] = jnp.concatenate(ys, axis=0).T
            xwT = jnp.concatenate(xws, axis=0).astype(BF16)
            ST = jnp.dot(xwT, Bg, preferred_element_type=F32)
            st_ref[0, g] = HT
            h_ref[g] = HT * jnp.concatenate(tots, axis=0) + ST

    cmap = (lambda c: (NC - 1 - c, 0)) if rev else (lambda c: (c, 0))
    smap = (lambda c: (NC - 1 - c, 0, 0, 0)) if rev else (lambda c: (c, 0, 0, 0))
    const = lambda c: (0, 0)
    return pl.pallas_call(
        body, name=name, grid=(NC,),
        in_specs=[pl.BlockSpec((CHUNK, XBC), cmap), pl.BlockSpec((CHUNK, LANES), cmap),
                  pl.BlockSpec((LANES, LANES), const), pl.BlockSpec((LANES, LANES), const)],
        out_specs=[pl.BlockSpec((CHUNK, DI), cmap), pl.BlockSpec((1, G, GW, N), smap)],
        out_shape=[jax.ShapeDtypeStruct((S, DI), F32), jax.ShapeDtypeStruct((NC, G, GW, N), F32)],
        scratch_shapes=[pltpu.VMEM((G, GW, N), F32)],
        compiler_params=_params(dimension_semantics=("arbitrary",)),
    )(xbc, dtraw, dtbT, alogT)


def ssd_bwd(name, xbc, dtraw, dy, st, dtbT, alogT, *, S, DI, G, H, rev):
    NC = S // CHUNK
    R = H // G
    GW = R * HEAD_DIM
    N = D_STATE
    XBC = xbc.shape[1]
    P = HEAD_DIM
    L = CHUNK

    def body(xbc_ref, dtraw_ref, dy_ref, st_ref, dtb_ref, alog_ref,
             dxbc_ref, ddt_ref, da_ref, dh_ref, dcst_ref, p2t_ref, p3t_ref, e2t_ref):
        c = pl.program_id(0)

        @pl.when(c == 0)
        def _():
            dh_ref[...] = jnp.zeros_like(dh_ref)
            da_ref[...] = jnp.zeros_like(da_ref)
            dcst_ref[...] = jnp.zeros_like(dcst_ref)
            p2t_ref[...] = jnp.zeros_like(p2t_ref)
            p3t_ref[...] = jnp.zeros_like(p3t_ref)
            e2t_ref[...] = jnp.zeros_like(e2t_ref)

        q = _ssd_common(dtraw_ref[...], dtb_ref[...], alog_ref[...], rev, H)
        cs, csT, dtT, doutT, dstT, totB = q['cs'], q['csT'], q['dtT'], q['doutT'], q['dstT'], q['totB']
        wstT = dstT * dtT
        lane = q['ci']
        dcs_c = jnp.zeros((L, LANES), F32)
        for g in range(G):
            Bg = xbc_ref[:, DI + g * N:DI + (g + 1) * N].astype(BF16)
            Cg = xbc_ref[:, DI + G * N + g * N:DI + G * N + (g + 1) * N].astype(BF16)
            CB = lax.dot_general(Cg, Bg, NT_DIMS, preferred_element_type=F32)
            HpT = st_ref[0, g]
            HpTb = HpT.astype(BF16)
            dHT = dh_ref[g]
            dHTb = dHT.astype(BF16)
            BdHT = lax.dot_general(dHTb, Bg, NT_DIMS, preferred_element_type=F32)
            yoffT = lax.dot_general(HpTb, Cg, NT_DIMS, preferred_element_type=F32)
            xT = xbc_ref[:, g * GW:(g + 1) * GW].T
            dyT = dy_ref[:, g * GW:(g + 1) * GW].T
            dCB = jnp.zeros((L, L), F32)
            dyds, xws, tots, dxs = [], [], [], []
            for r in range(R):
                h = g * R + r
                blk = slice(r * P, (r + 1) * P)
                Lm = jnp.exp(jnp.where(q['lo'], cs[:, h:h + 1] - csT[h:h + 1, :], -1e30))
                Gm = (CB * Lm).astype(BF16)
                xTh = xT[blk, :]
                dyTh = dyT[blk, :]
                xThb = xTh.astype(BF16)
                dyThb = dyTh.astype(BF16)
                u1T = jnp.dot(dyThb, Gm, preferred_element_type=F32)
                uT = u1T + BdHT[blk, :] * dstT[h:h + 1, :]
                dGx = lax.dot_general(dyThb, xThb, TN_DIMS, preferred_element_type=F32)
                T = dGx * (Lm * dtT[h:h + 1, :])
                dCB = dCB + T
                Mseg = T * CB
                dcs_c = jnp.where(lane == h, jnp.sum(Mseg, axis=1, keepdims=True), dcs_c)
                dydTh = dyTh * doutT[h:h + 1, :]
                xwTh = xTh * wstT[h:h + 1, :]
                p3row = jnp.sum(xwTh * BdHT[blk, :], axis=0, keepdims=True)
                dcst_ref[h:h + 1, :] = (jnp.sum(dydTh * yoffT[blk, :], axis=0, keepdims=True)
                                        - jnp.sum(Mseg, axis=0, keepdims=True) - p3row)
                p2t_ref[h:h + 1, :] = jnp.sum(xTh * uT, axis=0, keepdims=True)
                p3t_ref[h:h + 1, :] = p3row
                e2t_ref[h:h + 1, :] = jnp.sum(HpT[blk, :] * dHT[blk, :], axis=0, keepdims=True)
                dxs.append(uT * dtT[h:h + 1, :])
                dyds.append(dydTh)
                xws.append(xwTh)
                tots.append(jnp.broadcast_to(totB[h:h + 1, :], (P, N)))
            dxbc_ref[:, g * GW:(g + 1) * GW] = jnp.concatenate(dxs, axis=0).T
            dydT = jnp.concatenate(dyds, axis=0).astype(BF16)
            xwT = jnp.concatenate(xws, axis=0).astype(BF16)
            dCBb = dCB.astype(BF16)
            dC = (jnp.dot(dCBb, Bg, preferred_element_type=F32)
                  + lax.dot_general(dydT, HpTb, TN_DIMS, preferred_element_type=F32))
            dB = (lax.dot_general(dCBb, Cg, TN_DIMS, preferred_element_type=F32)
                  + lax.dot_general(xwT, dHTb, TN_DIMS, preferred_element_type=F32))
            dxbc_ref[:, DI + g * N:DI + (g + 1) * N] = dB
            dxbc_ref[:, DI + G * N + g * N:DI + G * N + (g + 1) * N] = dC
            dh_ref[g] = (dHT * jnp.concatenate(tots, axis=0)
                         + jnp.dot(dydT, Cg, preferred_element_type=F32))
        e1 = jnp.sum(p3t_ref[...], axis=1, keepdims=True)
        e2 = jnp.sum(e2t_ref[...], axis=1, keepdims=True)
        dcsT = (dcst_ref[...] + dcs_c.T
                + jnp.where(lane == q['last'], e1 + totB * e2, 0.0))
        daT = _split_dot_r(dcsT, q['lo'].astype(BF16), 3)
        ddtT = daT * q['AT'] + p2t_ref[...]
        da_ref[...] += daT * dtT
        ddraw = jnp.where(lane < H, (ddtT * _sig(q['preT'])).T, 0.0)
        if rev:
            ddraw = pltpu.roll(ddraw, H, 1)
        ddt_ref[...] = ddraw

    cmap = (lambda c: (c, 0)) if rev else (lambda c: (NC - 1 - c, 0))
    smap = (lambda c: (c, 0, 0, 0)) if rev else (lambda c: (NC - 1 - c, 0, 0, 0))
    const = lambda c: (0, 0)
    sq = pltpu.VMEM((LANES, CHUNK), F32)
    return pl.pallas_call(
        body, name=name, grid=(NC,),
        in_specs=[pl.BlockSpec((CHUNK, XBC), cmap), pl.BlockSpec((CHUNK, LANES), cmap),
                  pl.BlockSpec((CHUNK, DI), cmap),
                  pl.BlockSpec((1, G, GW, N), smap),
                  pl.BlockSpec((LANES, LANES), const), pl.BlockSpec((LANES, LANES), const)],
        out_specs=[pl.BlockSpec((CHUNK, XBC), cmap), pl.BlockSpec((CHUNK, LANES), cmap),
                   pl.BlockSpec((LANES, LANES), const)],
        out_shape=[jax.ShapeDtypeStruct((S, XBC), F32), jax.ShapeDtypeStruct((S, LANES), F32),
                   jax.ShapeDtypeStruct((LANES, LANES), F32)],
        scratch_shapes=[pltpu.VMEM((G, GW, N), F32), sq, sq, sq, sq],
        compiler_params=_params(dimension_semantics=("arbitrary",)),
    )(xbc, dtraw, dy, st, dtbT, alogT)


ANY = pl.BlockSpec(memory_space=pl.ANY)


def chip_exchange(name, groups, gather):
    flat = [arr for grp in groups for arr in grp]
    n_in = len(flat)
    n_out = len(groups)
    n_rc = 3 * n_in

    def body(*refs):
        in_refs = refs[:n_in]
        out_refs = refs[n_in:n_in + n_out]
        send, recv, loc = refs[n_in + n_out:]
        x, y, c = lax.axis_index("x"), lax.axis_index("y"), lax.axis_index("c")
        me = 2 * x + y
        peers = [(1 - x, y), (x, 1 - y), (1 - x, 1 - y)]
        local, remote = [], []
        q = 0
        for a, grp in enumerate(groups):
            for l in range(len(grp)):
                src = in_refs[q]
                dst = out_refs[a].at[me] if gather else out_refs[a].at[me, l]
                own = src if gather else src.at[me]
                lc = pltpu.make_async_copy(own, dst, loc.at[q])
                lc.start()
                local.append(lc)
                for j, (px, py) in enumerate(peers):
                    blk = src if gather else src.at[2 * px + py]
                    rc = pltpu.make_async_remote_copy(
                        src_ref=blk, dst_ref=dst, send_sem=send.at[3 * q + j], recv_sem=recv.at[3 * q + j],
                        device_id=(px, py, c), device_id_type=MESH)
                    rc.start()
                    remote.append(rc)
                q += 1
        for lc in local:
            lc.wait()
        for rc in remote:
            rc.wait()

    out_shape = []
    for grp in groups:
        a0 = grp[0]
        if gather:
            out_shape.append(jax.ShapeDtypeStruct((4,) + a0.shape, a0.dtype))
        else:
            out_shape.append(jax.ShapeDtypeStruct((4, len(grp)) + a0.shape[1:], a0.dtype))
    return pl.pallas_call(
        body, name=name, in_specs=[ANY] * n_in, out_specs=[ANY] * n_out, out_shape=out_shape,
        scratch_shapes=[pltpu.SemaphoreType.DMA((n_rc,)), pltpu.SemaphoreType.DMA((n_rc,)),
                        pltpu.SemaphoreType.DMA((n_in,))],
    )(*flat)


def sibling_swap(name, arrs):
    n = len(arrs)

    def body(*refs):
        in_refs = refs[:n]
        out_refs = refs[n:2 * n]
        send, recv = refs[2 * n:]
        peer = (lax.axis_index("x"), lax.axis_index("y"), 1 - lax.axis_index("c"))
        rcs = []
        for a in range(n):
            rc = pltpu.make_async_remote_copy(src_ref=in_refs[a], dst_ref=out_refs[a], send_sem=send.at[a],
                                              recv_sem=recv.at[a], device_id=peer, device_id_type=MESH)
            rc.start()
            rcs.append(rc)
        for rc in rcs:
            rc.wait()

    return pl.pallas_call(
        body, name=name, in_specs=[ANY] * n, out_specs=[ANY] * n,
        out_shape=[jax.ShapeDtypeStruct(a.shape, a.dtype) for a in arrs],
        scratch_shapes=[pltpu.SemaphoreType.DMA((n,)), pltpu.SemaphoreType.DMA((n,))],
    )(*arrs)


def all8_gather(name, v):
    flips = [(fx, fy, fc) for fx in (0, 1) for fy in (0, 1) for fc in (0, 1) if (fx, fy, fc) != (0, 0, 0)]

    def body(v_ref, out_ref, send, recv, loc):
        x, y, c = lax.axis_index("x"), lax.axis_index("y"), lax.axis_index("c")
        me = 4 * x + 2 * y + c
        lc = pltpu.make_async_copy(v_ref, out_ref.at[me], loc)
        lc.start()
        rcs = []
        for k, (fx, fy, fc) in enumerate(flips):
            tgt = (x + fx - 2 * x * fx, y + fy - 2 * y * fy, c + fc - 2 * c * fc)
            rc = pltpu.make_async_remote_copy(src_ref=v_ref, dst_ref=out_ref.at[me], send_sem=send.at[k],
                                              recv_sem=recv.at[k], device_id=tgt, device_id_type=MESH)
            rc.start()
            rcs.append(rc)
        lc.wait()
        for rc in rcs:
            rc.wait()

    return pl.pallas_call(
        body, name=name, in_specs=[ANY], out_specs=ANY,
        out_shape=jax.ShapeDtypeStruct((8,) + v.shape, v.dtype),
        scratch_shapes=[pltpu.SemaphoreType.DMA((7,)), pltpu.SemaphoreType.DMA((7,)), pltpu.SemaphoreType.DMA],
    )(v)


def _pick_rows(rows, cols, target_elems=128 * 1024):
    if rows % SUBLANES != 0:
        return rows
    best = SUBLANES
    t = SUBLANES
    while t <= rows:
        if rows % t == 0 and t * cols <= target_elems:
            best = t
        t += SUBLANES
    return best


def sum_chips(name, parts):
    _, R, C = parts.shape
    tm = _pick_rows(R, C)

    def body(p_ref, o_ref):
        o_ref[...] = (p_ref[0] + p_ref[1]) + (p_ref[2] + p_ref[3])

    return pl.pallas_call(
        body, name=name, grid=(R // tm,),
        in_specs=[pl.BlockSpec((4, tm, C), lambda i: (0, i, 0))],
        out_specs=pl.BlockSpec((tm, C), lambda i: (i, 0)),
        out_shape=jax.ShapeDtypeStruct((R, C), F32),
        compiler_params=_params(dimension_semantics=("arbitrary",)),
    )(parts)


def _adamw(g, w, m, v):
    m = ADAM_B1 * m + (1.0 - ADAM_B1) * g
    v = ADAM_B2 * v + (1.0 - ADAM_B2) * (g * g)
    m_hat = m / (1.0 - ADAM_B1 ** ADAM_STEP)
    v_hat = v / (1.0 - ADAM_B2 ** ADAM_STEP)
    delta = -ADAM_LR * (m_hat / (jnp.sqrt(v_hat) + ADAM_EPS) + ADAM_WD * w)
    return delta, m, v


def adamw_shard(name, s_mine, s_sib, w, m, v):
    R, C = w.shape
    tm = _pick_rows(R, C)

    def body(a_ref, b_ref, w_ref, m_ref, v_ref, g_out, d_out, m_out, v_out):
        g = a_ref[...] + b_ref[...]
        d, mn, vn = _adamw(g, w_ref[...], m_ref[...], v_ref[...])
        g_out[...] = g
        d_out[...] = d
        m_out[...] = mn
        v_out[...] = vn

    spec = pl.BlockSpec((tm, C), lambda i: (i, 0))
    return pl.pallas_call(
        body, name=name, grid=(R // tm,), in_specs=[spec] * 5, out_specs=[spec] * 4,
        out_shape=[jax.ShapeDtypeStruct((R, C), F32)] * 4,
        compiler_params=_params(dimension_semantics=("arbitrary",)),
    )(s_mine, s_sib, w, m, v)


def adamw_small(name, parts, w, m, v):
    W = w.shape[1]

    def body(p_ref, w_ref, m_ref, v_ref, g_out, d_out, m_out, v_out):
        acc = p_ref[0]
        for k in range(1, 8):
            acc = acc + p_ref[k]
        g = jnp.sum(acc, axis=0, keepdims=True)
        d, mn, vn = _adamw(g, w_ref[...], m_ref[...], v_ref[...])
        g_out[...] = g
        d_out[...] = d
        m_out[...] = mn
        v_out[...] = vn

    return pl.pallas_call(
        body, name=name, out_shape=[jax.ShapeDtypeStruct((1, W), F32)] * 4,
        compiler_params=_params(),
    )(parts, w, m, v)


def _pad_lanes(v, width=LANES):
    return jnp.pad(v, ((0, 0), (0, width - v.shape[1])))


def _layer_fwd(cf, x, xb, pb, W, sm):
    S, D, CD, DI, XBC, F, H, G = cf['S'], cf['D'], cf['CD'], cf['DI'], cf['XBC'], cf['F'], cf['H'], cf['G']
    NM = cf['NM']
    alpha = cf['alpha']
    tm = cf['tm']
    tn_in = cf['tn_in']
    sv = {}

    ident = lambda acc, ex: ([acc], [])
    proj, = fused_mm("in_proj", [(xb, W['in_main'], 0, False)], [], ident, [(NM, F32, tn_in, 0)],
                     M=S, tm=tm, tn=tn_in, nj=NM // tn_in)
    dtraw, = fused_mm("dt_proj", [(xb, W['in_dt'], 0, False)], [], ident, [(LANES, F32, LANES, 0)],
                      M=S, tm=tm, tn=LANES)

    u, = row_call("glu", lambda a, gt: ([a * _sig(gt)], []),
                  [(proj, 'row', CD, 0, 0), (proj, 'row', CD, 1, 0)], [(CD, F32, CD, 0, 0)], M=S, tm=tm)

    def conv_a_epi(conv, ex):
        cb_, g_, b_ = ex
        ca = conv + cb_
        xhat, _ = _ln_stats(ca)
        la = xhat * g_ + b_
        return [ca, la * _sig(la)], []

    ca, sa = conv_call("conv_a", u, 0, sm['conv_a_w'], cf['KA'], conv_a_epi,
                       [(sm['conv_a_b'], 'vec', CD, 0, 0), (sm['ln_a_g'], 'vec', CD, 0, 0), (sm['ln_a_b'], 'vec', CD, 0, 0)],
                       [(CD, F32, CD, 0, 0), (CD, BF16, CD, 0, 0)], M=S, tm=cf['tmc'], cw=CD, nc=1, reverse=False)
    y_a, = fused_mm("a_out", [(sa, W['a_out'], 0, False)], [], ident, [(D, F32, D, 0)], M=S, tm=tm, tn=D)

    def conv_b_epi(conv, ex):
        cb = conv + ex[0]
        return [cb, cb * _sig(cb)], []

    xoff = (2 * CD + 2 * D + DI) // DI
    cbv, xbc = conv_call("conv_b", proj, xoff, sm['ssm_conv_w'], cf['KB'], conv_b_epi,
                         [(sm['ssm_conv_b'], 'vec', DI, 0, 1)],
                         [(XBC, F32, DI, 0, 1), (XBC, F32, DI, 0, 1)], M=S, tm=cf['tmc'], cw=DI, nc=XBC // DI,
                         reverse=False)
    y_f, st_f = ssd_fwd("ssd_fwd_f", xbc, dtraw, sm['dtb_f'], sm['alog_f'], S=S, DI=DI, G=G, H=H, rev=False)
    y_r, st_r = ssd_fwd("ssd_fwd_r", xbc, dtraw, sm['dtb_r'], sm['alog_r'], S=S, DI=DI, G=G, H=H, rev=True)

    def gate_norm(yf, yr, xs, z, dsk, ng):
        y = yf + yr + xs * dsk
        yz = y * (z * _sig(z))
        outs = []
        gw = cf['GW']
        for g in range(G):
            t = yz[:, g * gw:(g + 1) * gw]
            outs.append(t * lax.rsqrt(jnp.mean(t * t, axis=-1, keepdims=True) + RMS_EPS))
        return [jnp.concatenate(outs, axis=1) * ng], []

    zoff = (2 * CD + 2 * D) // DI
    yn, = row_call("gate_norm", gate_norm,
                   [(y_f, 'row', DI, 0, 0), (y_r, 'row', DI, 0, 0), (xbc, 'row', DI, 0, 0), (proj, 'row', DI, zoff, 0),
                    (sm['dskip_full'], 'vec', DI, 0, 0), (sm['ssm_norm_g'], 'vec', DI, 0, 0)],
                   [(DI, BF16, DI, 0, 0)], M=S, tm=cf['tmr'])
    y_b, = fused_mm("b_out", [(yn, W['b_out'], 0, False)], [], ident, [(D, F32, D, 0)], M=S, tm=tm, tn=D)

    goff = (2 * CD) // D
    merged, = row_call("merge", lambda ga, gb, ya, yb: ([_sig(ga) * ya + _sig(gb) * yb], []),
                       [(proj, 'row', D, goff, 0), (proj, 'row', D, goff + 1, 0), (y_a, 'row', D, 0, 0), (y_b, 'row', D, 0, 0)],
                       [(D, BF16, D, 0, 0)], M=S, tm=tm)

    def mix_epi(acc, ex):
        xin, g_, b_ = ex
        r1 = alpha * xin + acc
        xhat, _ = _ln_stats(r1)
        return [r1, xhat * g_ + b_], []

    r1, hb = fused_mm("o_mix", [(merged, W['o'], 0, False)],
                      [(x, 'row', D, 0), (sm['ln1_g'], 'vec', D, 0), (sm['ln1_b'], 'vec', D, 0)],
                      mix_epi, [(D, F32, D, 0), (D, BF16, D, 0)], M=S, tm=tm, tn=D)

    tnf = cf['tnf']

    g_ = fused_mm("ffn_gate", [(hb, W['gate_up'], 0, False)], [], ident, [(F, F32, tnf, 0)],
                  M=S, tm=tm, tn=tnf, nj=F // tnf)[0]
    u_ = fused_mm("ffn_up", [(hb, W['gate_up'], F // tnf, False)], [], ident, [(F, F32, tnf, 0)],
                  M=S, tm=tm, tn=tnf, nj=F // tnf)[0]
    f, = row_call("swiglu", lambda a, b: ([a * _sig(a) * b], []),
                  [(g_, 'row', tnf, 0, 1), (u_, 'row', tnf, 0, 1)], [(F, BF16, tnf, 0, 1)], M=S, tm=tm, nc=F // tnf)

    def down_epi(acc, ex):
        r1_, g1, b1, g2, b2 = ex
        xh1, _ = _ln_stats(r1_)
        r2 = alpha * (xh1 * g1 + b1) + acc
        xh2, _ = _ln_stats(r2)
        return [r2, xh2 * g2 + b2], []

    r2, h2b = fused_mm("ffn_down", [(f, W['down'], 0, False)],
                       [(r1, 'row', D, 0), (sm['ln1_g'], 'vec', D, 0), (sm['ln1_b'], 'vec', D, 0),
                        (sm['ln2_g'], 'vec', D, 0), (sm['ln2_b'], 'vec', D, 0)],
                       down_epi, [(D, F32, D, 0), (D, BF16, D, 0)], M=S, tm=tm, tn=D)

    t_, = fused_mm("ple_gate", [(h2b, W['ple_gate'], 0, False)], [], ident, [(D, F32, D, 0)], M=S, tm=tm, tn=D)
    pe, = fused_mm("ple_proj", [(pb, W['ple'], 0, False)], [], ident, [(D, F32, D, 0)], M=S, tm=tm, tn=D)

    def ple_mix(r2_, g2, b2, t, pe_, pg):
        xh2, _ = _ln_stats(r2_)
        h2 = xh2 * g2 + b2
        e = pe_ * lax.rsqrt(jnp.mean(pe_ * pe_, axis=-1, keepdims=True) + RMS_EPS) * pg
        xn = h2 + e * _sig(t)
        return [xn, xn], []

    xn, xnb = row_call("ple_mix", ple_mix,
                       [(r2, 'row', D, 0, 0), (sm['ln2_g'], 'vec', D, 0, 0), (sm['ln2_b'], 'vec', D, 0, 0),
                        (t_, 'row', D, 0, 0), (pe, 'row', D, 0, 0), (sm['ple_norm_g'], 'vec', D, 0, 0)],
                       [(D, F32, D, 0, 0), (D, BF16, D, 0, 0)], M=S, tm=tm)
    sv.update(x=x, xb=xb, pb=pb, proj=proj, dtraw=dtraw, u=u, ca=ca, sa=sa, y_a=y_a, cbv=cbv, xbc=xbc,
              y_f=y_f, y_r=y_r, st_f=st_f, st_r=st_r, yn=yn, y_b=y_b, merged=merged, r1=r1, hb=hb,
              g_=g_, u_=u_, f=f, r2=r2, h2b=h2b, t_=t_, pe=pe)
    return xn, xnb, sv


def _layer_bwd(cf, sv, W, sm, dxn=None, target=None, xn=None):
    S, D, CD, DI, XBC, F, H, G = cf['S'], cf['D'], cf['CD'], cf['DI'], cf['XBC'], cf['F'], cf['H'], cf['G']
    NM = cf['NM']
    alpha = cf['alpha']
    tm = cf['tm']
    gw = cf['GW']
    out = {}

    def ple_bwd_core(dx_, t, pe_, pg):
        s = _sig(t)
        rinv = lax.rsqrt(jnp.mean(pe_ * pe_, axis=-1, keepdims=True) + RMS_EPS)
        pn = pe_ * rinv
        e = pn * pg
        dtg = dx_ * e * (s * (1.0 - s))
        de = dx_ * s
        qv = de * pg
        dpe = rinv * (qv - pn * jnp.mean(qv * pn, axis=-1, keepdims=True))
        return dtg, dpe, de * pn

    if dxn is None:
        def head(xn_, tgt, t, pe_, pg):
            err = xn_ - tgt
            dx_ = err * (1.0 / D)
            dtg, dpe, dpg = ple_bwd_core(dx_, t, pe_, pg)
            return [dx_, dtg, dpe], [dpg, err * err]

        (dxn, dtg, dpe, dpg, lsq) = row_call(
            "loss_ple_bwd", head,
            [(xn, 'row', D, 0, 0), (target, 'row', D, 0, 0), (sv['t_'], 'row', D, 0, 0), (sv['pe'], 'row', D, 0, 0),
             (sm['ple_norm_g'], 'vec', D, 0, 0)],
            [(D, F32, D, 0, 0), (D, BF16, D, 0, 0), (D, BF16, D, 0, 0)], [(D, D, 0, 0), (D, D, 0, 0)], M=S, tm=tm)
        out['loss_sq'] = lsq
    else:
        def mid(dx_, t, pe_, pg):
            dtg, dpe, dpg = ple_bwd_core(dx_, t, pe_, pg)
            return [dtg, dpe], [dpg]

        (dtg, dpe, dpg) = row_call(
            "ple_bwd", mid,
            [(dxn, 'row', D, 0, 0), (sv['t_'], 'row', D, 0, 0), (sv['pe'], 'row', D, 0, 0),
             (sm['ple_norm_g'], 'vec', D, 0, 0)],
            [(D, BF16, D, 0, 0), (D, BF16, D, 0, 0)], [(D, D, 0, 0)], M=S, tm=tm)
    out['ple_norm_g'] = dpg

    def ln_bwd_epi(scale):
        def epi(acc, ex):
            res, r_, g_ = ex
            dh = scale * res + acc
            xhat, rstd = _ln_stats(r_)
            dr = _ln_bwd(dh, xhat, rstd, g_)
            return [dr, dr], [dh * xhat, dh]
        return epi

    dr2, dr2b, dg2, db2 = fused_mm(
        "dh2", [(dtg, W['ple_gate_T'], 0, False)],
        [(dxn, 'row', D, 0), (sv['r2'], 'row', D, 0), (sm['ln2_g'], 'vec', D, 0)],
        ln_bwd_epi(1.0), [(D, F32, D, 0), (D, BF16, D, 0)], [(D, D, 0), (D, D, 0)], M=S, tm=tm, tn=D)
    out['ln2_g'], out['ln2_b'] = dg2, db2

    tnf = cf['tnf']

    def dswiglu_epi(acc, ex):
        gg, uu = ex
        s = _sig(gg)
        return [acc * uu * _dsilu(gg, s), acc * (gg * s)], []

    dg_b, du_b = fused_mm(
        "d_down", [(dr2b, W['down_T'], 0, False)],
        [(sv['g_'], 'row', tnf, 0), (sv['u_'], 'row', tnf, 0)], dswiglu_epi,
        [(F, BF16, tnf, 0), (F, BF16, tnf, 0)], M=S, tm=tm, tn=tnf, nj=F // tnf)

    dr1, dr1b, dg1, db1 = fused_mm(
        "dh1", [(dg_b, W['gate_T'], 0, True), (du_b, W['up_T'], 0, True)],
        [(dr2, 'row', D, 0), (sv['r1'], 'row', D, 0), (sm['ln1_g'], 'vec', D, 0)],
        ln_bwd_epi(alpha), [(D, F32, D, 0), (D, BF16, D, 0)], [(D, D, 0), (D, D, 0)],
        M=S, tm=tm, tn=D, nk=cf['nk_f'])
    out['ln1_g'], out['ln1_b'] = dg1, db1

    goff = (2 * CD) // D

    def dmerge_epi(acc, ex):
        ga, gb, ya, yb = ex
        sa_, sb_ = _sig(ga), _sig(gb)
        dga = acc * ya * (sa_ * (1.0 - sa_))
        dgb = acc * yb * (sb_ * (1.0 - sb_))
        return [jnp.concatenate([dga, dgb], axis=1), acc * sa_, acc * sb_], []

    dproj, dya_b, dyb_b = fused_mm(
        "d_merge", [(dr1b, W['o_T'], 0, False)],
        [(sv['proj'], 'row', D, goff), (sv['proj'], 'row', D, goff + 1), (sv['y_a'], 'row', D, 0), (sv['y_b'], 'row', D, 0)],
        dmerge_epi, [(NM, BF16, 2 * D, (2 * CD) // (2 * D)), (D, BF16, D, 0), (D, BF16, D, 0)], M=S, tm=tm, tn=D)

    def dsa_epi(acc, ex):
        ca_, g_, b_ = ex
        xhat, rstd = _ln_stats(ca_)
        la = xhat * g_ + b_
        dla = acc * _dsilu(la, _sig(la))
        dca = _ln_bwd(dla, xhat, rstd, g_)
        return [dca], [dla * xhat, dla, dca]

    dca, dlag, dlab, dcab = fused_mm(
        "d_a_out", [(dya_b, W['a_out_T'], 0, False)],
        [(sv['ca'], 'row', CD, 0), (sm['ln_a_g'], 'vec', CD, 0), (sm['ln_a_b'], 'vec', CD, 0)],
        dsa_epi, [(CD, F32, CD, 0)], [(CD, CD, 0), (CD, CD, 0), (CD, CD, 0)], M=S, tm=tm, tn=D)
    out['ln_a_g'], out['ln_a_b'], out['conv_a_b'] = dlag, dlab, dcab

    def dglu_epi(du, ex):
        a, gt = ex
        s = _sig(gt)
        return [jnp.concatenate([du * s, du * a * (s * (1.0 - s))], axis=1)], []

    dproj, dwa = conv_call(
        "d_conv_a", dca, 0, sm['conv_a_w'], cf['KA'], dglu_epi,
        [(sv['proj'], 'row', CD, 0, 0), (sv['proj'], 'row', CD, 1, 0)],
        [(NM, BF16, 2 * CD, 0, 0)], M=S, tm=cf['tmc'], cw=CD, nc=1, reverse=True, xin=(sv['u'], 0),
        passthrough=(dproj, 0))
    out['conv_a_w'] = dwa

    zoff = (2 * CD + 2 * D) // DI

    def dgate_norm_epi(acc, ex):
        yf, yr, xs, z, dsk, ng = ex
        y = yf + yr + xs * dsk
        sz = _sig(z)
        siluz = z * sz
        yz = y * siluz
        dyzs, yhats = [], []
        for g in range(G):
            t = yz[:, g * gw:(g + 1) * gw]
            rinv = lax.rsqrt(jnp.mean(t * t, axis=-1, keepdims=True) + RMS_EPS)
            yh = t * rinv
            qv = acc[:, g * gw:(g + 1) * gw] * ng[:, g * gw:(g + 1) * gw]
            dyzs.append(rinv * (qv - yh * jnp.mean(qv * yh, axis=-1, keepdims=True)))
            yhats.append(yh)
        dyz = jnp.concatenate(dyzs, axis=1)
        yhat = jnp.concatenate(yhats, axis=1)
        dy = dyz * siluz
        dz = dyz * y * _dsilu(z, sz)
        return [dy, dz], [acc * yhat, dy * xs]

    tmr = cf['tmr']
    dy_ssd, dproj, dng, ddsk = fused_mm(
        "d_b_out", [(dyb_b, W['b_out_T'], 0, False)],
        [(sv['y_f'], 'row', DI, 0), (sv['y_r'], 'row', DI, 0), (sv['xbc'], 'row', DI, 0), (sv['proj'], 'row', DI, zoff),
         (sm['dskip_full'], 'vec', DI, 0), (sm['ssm_norm_g'], 'vec', DI, 0)],
        dgate_norm_epi, [(DI, F32, DI, 0), (NM, BF16, DI, zoff)], [(DI, DI, 0), (DI, DI, 0)],
        M=S, tm=tmr, tn=DI, passthrough=(dproj, 1))
    out['ssm_norm_g'], out['dskip_full'] = dng, ddsk

    dxbc_f, ddt_f, dA_f = ssd_bwd("ssd_bwd_f", sv['xbc'], sv['dtraw'], dy_ssd, sv['st_f'], sm['dtb_f'], sm['alog_f'],
                                  S=S, DI=DI, G=G, H=H, rev=False)
    dxbc_r, ddt_r, dA_r = ssd_bwd("ssd_bwd_r", sv['xbc'], sv['dtraw'], dy_ssd, sv['st_r'], sm['dtb_r'], sm['alog_r'],
                                  S=S, DI=DI, G=G, H=H, rev=True)
    out['dA_f'], out['dA_r'] = dA_f, dA_r

    def dxbc_sum(a, b, dy, dsk, cbv):
        cj = pl.program_id(0)
        skip = jnp.where(cj == 0, 1.0, 0.0)
        d = a + b + (dy * dsk) * skip
        dcb = d * _dsilu(cbv, _sig(cbv))
        return [dcb], [dcb]

    dcb, dcbb = row_call(
        "d_xbc", dxbc_sum,
        [(dxbc_f, 'row', DI, 0, 1), (dxbc_r, 'row', DI, 0, 1), (dy_ssd, 'row', DI, 0, 0),
         (sm['dskip_full'], 'vec', DI, 0, 0), (sv['cbv'], 'row', DI, 0, 1)],
        [(XBC, F32, DI, 0, 1)], [(XBC, DI, 0, 1)], M=S, tm=tmr, nc=XBC // DI)
    out['ssm_conv_b'] = dcbb

    xoff = (2 * CD + 2 * D + DI) // DI
    dproj, dwb = conv_call(
        "d_conv_b", dcb, 0, sm['ssm_conv_w'], cf['KB'], lambda conv, ex: ([conv], []), [],
        [(NM, BF16, DI, xoff, 1)], M=S, tm=cf['tmc'], cw=DI, nc=XBC // DI, reverse=True, xin=(sv['proj'], xoff),
        passthrough=(dproj, 0))
    out['ssm_conv_w'] = dwb

    ddtb, ddt_bias = row_call("d_dt", lambda a, b: ([a + b], [a + b]),
                              [(ddt_f, 'row', LANES, 0, 0), (ddt_r, 'row', LANES, 0, 0)],
                              [(LANES, BF16, LANES, 0, 0)], [(LANES, LANES, 0, 0)], M=S, tm=tm)
    out['dt_bias'] = ddt_bias

    dx, = fused_mm("d_x", [(dproj, W['in_main_T'], 0, True), (ddtb, W['in_dt_T'], 0, False)],
                   [(dr1, 'row', D, 0)], lambda acc, ex: ([alpha * ex[0] + acc], []),
                   [(D, F32, D, 0)], M=S, tm=tm, tn=D, nk=cf['nk_in'])

    tmw = cf['tmw']
    xb = sv['xb']
    out['w_in'] = jnp.concatenate(
        [mm_tn("dw_in", xb, dproj, tm=tmw, tk=D, tn=cf['tn_in']),
         mm_tn("dw_dt", xb, ddtb, tm=tmw, tk=D, tn=LANES)[:, :2 * H]], axis=1)
    out['w_a_out'] = mm_tn("dw_a_out", sv['sa'], dya_b, tm=tmw, tk=CD, tn=D)
    out['w_b_out'] = mm_tn("dw_b_out", sv['yn'], dyb_b, tm=tmw, tk=DI // 2, tn=D)
    out['w_o'] = mm_tn("dw_o", sv['merged'], dr1b, tm=tmw, tk=D, tn=D)
    out['w_gate_up'] = jnp.concatenate(
        [mm_tn("dw_gate", sv['hb'], dg_b, tm=tmw, tk=D, tn=tnf),
         mm_tn("dw_up", sv['hb'], du_b, tm=tmw, tk=D, tn=tnf)], axis=1)
    out['w_down'] = mm_tn("dw_down", sv['f'], dr2b, tm=tmw, tk=tnf, tn=D)
    out['w_ple'] = mm_tn("dw_ple", sv['pb'], dpe, tm=tmw, tk=sv['pb'].shape[1], tn=D)
    out['w_ple_gate'] = mm_tn("dw_ple_gate", sv['h2b'], dtg, tm=tmw, tk=D, tn=D)
    return dx, out


_WEIGHTS = ['w_in', 'conv_a_w', 'conv_a_b', 'ln_a_g', 'ln_a_b', 'w_a_out', 'ssm_conv_w', 'ssm_conv_b', 'a_log',
            'dt_bias', 'd_skip', 'ssm_norm_g', 'w_b_out', 'w_o', 'ln1_g', 'ln1_b', 'w_gate_up', 'w_down', 'ln2_g',
            'ln2_b', 'w_ple', 'ple_norm_g', 'w_ple_gate']
_COL_SHARDED = ['w_in', 'conv_a_w', 'ssm_conv_w', 'w_gate_up', 'w_ple']
_ROW_SHARDED = ['w_a_out', 'w_b_out', 'w_o', 'w_down', 'w_ple_gate']
_BIG = _COL_SHARDED + _ROW_SHARDED
_SMALL = [n for n in _WEIGHTS if n not in _BIG]
_CONV = ['conv_a_w', 'ssm_conv_w']


def _ceil_to(n, k):
    return -(-n // k) * k


def kernel(x, p, w_in, conv_a_w, conv_a_b, ln_a_g, ln_a_b, w_a_out, ssm_conv_w, ssm_conv_b, a_log, dt_bias, d_skip, ssm_norm_g, w_b_out, w_o, ln1_g, ln1_b, w_gate_up, w_down, ln2_g, ln2_b, w_ple, ple_norm_g, w_ple_gate, loss_target, m_w_in, m_conv_a_w, m_conv_a_b, m_ln_a_g, m_ln_a_b, m_w_a_out, m_ssm_conv_w, m_ssm_conv_b, m_a_log, m_dt_bias, m_d_skip, m_ssm_norm_g, m_w_b_out, m_w_o, m_ln1_g, m_ln1_b, m_w_gate_up, m_w_down, m_ln2_g, m_ln2_b, m_w_ple, m_ple_norm_g, m_w_ple_gate, v_w_in, v_conv_a_w, v_conv_a_b, v_ln_a_g, v_ln_a_b, v_w_a_out, v_ssm_conv_w, v_ssm_conv_b, v_a_log, v_dt_bias, v_d_skip, v_ssm_norm_g, v_w_b_out, v_w_o, v_ln1_g, v_ln1_b, v_w_gate_up, v_w_down, v_ln2_g, v_ln2_b, v_w_ple, v_ple_norm_g, v_w_ple_gate):
    wt = dict(w_in=w_in, conv_a_w=conv_a_w, conv_a_b=conv_a_b, ln_a_g=ln_a_g, ln_a_b=ln_a_b, w_a_out=w_a_out,
              ssm_conv_w=ssm_conv_w, ssm_conv_b=ssm_conv_b, a_log=a_log, dt_bias=dt_bias, d_skip=d_skip,
              ssm_norm_g=ssm_norm_g, w_b_out=w_b_out, w_o=w_o, ln1_g=ln1_g, ln1_b=ln1_b, w_gate_up=w_gate_up,
              w_down=w_down, ln2_g=ln2_g, ln2_b=ln2_b, w_ple=w_ple, ple_norm_g=ple_norm_g, w_ple_gate=w_ple_gate)
    mo = dict(w_in=m_w_in, conv_a_w=m_conv_a_w, conv_a_b=m_conv_a_b, ln_a_g=m_ln_a_g, ln_a_b=m_ln_a_b,
              w_a_out=m_w_a_out, ssm_conv_w=m_ssm_conv_w, ssm_conv_b=m_ssm_conv_b, a_log=m_a_log,
              dt_bias=m_dt_bias, d_skip=m_d_skip, ssm_norm_g=m_ssm_norm_g, w_b_out=m_w_b_out, w_o=m_w_o,
              ln1_g=m_ln1_g, ln1_b=m_ln1_b, w_gate_up=m_w_gate_up, w_down=m_w_down, ln2_g=m_ln2_g, ln2_b=m_ln2_b,
              w_ple=m_w_ple, ple_norm_g=m_ple_norm_g, w_ple_gate=m_w_ple_gate)
    vo = dict(w_in=v_w_in, conv_a_w=v_conv_a_w, conv_a_b=v_conv_a_b, ln_a_g=v_ln_a_g, ln_a_b=v_ln_a_b,
              w_a_out=v_w_a_out, ssm_conv_w=v_ssm_conv_w, ssm_conv_b=v_ssm_conv_b, a_log=v_a_log,
              dt_bias=v_dt_bias, d_skip=v_d_skip, ssm_norm_g=v_ssm_norm_g, w_b_out=v_w_b_out, w_o=v_w_o,
              ln1_g=v_ln1_g, ln1_b=v_ln1_b, w_gate_up=v_w_gate_up, w_down=v_w_down, ln2_g=v_ln2_g, ln2_b=v_ln2_b,
              w_ple=v_w_ple, ple_norm_g=v_ple_norm_g, w_ple_gate=v_w_ple_gate)

    L = w_in.shape[0]
    S, D = x.shape[1], x.shape[2]
    CD = conv_a_b.shape[1]
    DI = ssm_norm_g.shape[1]
    XBC = ssm_conv_b.shape[1]
    H = d_skip.shape[1]
    G = (XBC - DI) // (2 * D_STATE)
    F = w_down.shape[1] * 4
    N_IN = w_in.shape[2] * 4
    NM = N_IN - 2 * H
    KA, KB = conv_a_w.shape[1], ssm_conv_w.shape[1]
    assert DI == H * HEAD_DIM and CD == D and DI == 2 * D and XBC == 2 * DI and NM == 2 * CD + 2 * D + DI + XBC
    assert 2 * H <= LANES and S % CHUNK == 0
    tnf = F // 2
    cf = dict(S=S, D=D, CD=CD, DI=DI, XBC=XBC, F=F, H=H, G=G, NM=NM, KA=KA, KB=KB, GW=(H // G) * HEAD_DIM,
              alpha=float((2 * L) ** 0.25), tm=min(512, S), tmc=min(256, S), tmr=min(256, S), tmw=min(1024, S),
              tn_in=D, tnf=tnf, nk_f=2, nk_in=NM // DI)

    send = [[wt[n] if n in _CONV else wt[n].astype(BF16)] for n in _BIG]
    got = chip_exchange("gather_weights", send, gather=True)
    full = {}
    for n, g in zip(_BIG, got):
        if n in _COL_SHARDED:
            full[n] = g.transpose(1, 2, 0, 3).reshape(L, g.shape[2], 4 * g.shape[3])
        else:
            full[n] = g.transpose(1, 0, 2, 3).reshape(L, 4 * g.shape[2], g.shape[3])

    def layer_weights(l):
        win = full['w_in'][l]
        in_main = win[:, :NM]
        in_dt = _pad_lanes(win[:, NM:])
        gu = full['w_gate_up'][l]
        W = dict(in_main=in_main, in_dt=in_dt, in_main_T=in_main.T, in_dt_T=in_dt.T,
                 a_out=full['w_a_out'][l], a_out_T=full['w_a_out'][l].T,
                 b_out=full['w_b_out'][l], b_out_T=full['w_b_out'][l].T,
                 o=full['w_o'][l], o_T=full['w_o'][l].T, gate_up=gu, gate_T=gu[:, :F].T, up_T=gu[:, F:].T,
                 down=full['w_down'][l], down_T=full['w_down'][l].T, ple=full['w_ple'][l],
                 ple_gate=full['w_ple_gate'][l], ple_gate_T=full['w_ple_gate'][l].T)
        row = lambda v: v.reshape(1, -1)
        head_table = lambda v: jnp.broadcast_to(jnp.pad(v, (0, LANES - H))[:, None], (LANES, LANES))
        sm = dict(conv_a_w=jnp.pad(full['conv_a_w'][l], ((0, _ceil_to(KA, SUBLANES) - KA), (0, 0))),
                  ssm_conv_w=jnp.pad(full['ssm_conv_w'][l], ((0, _ceil_to(KB, SUBLANES) - KB), (0, 0))),
                  conv_a_b=row(conv_a_b[l]), ln_a_g=row(ln_a_g[l]), ln_a_b=row(ln_a_b[l]),
                  ssm_conv_b=row(ssm_conv_b[l]), ssm_norm_g=row(ssm_norm_g[l]),
                  ln1_g=row(ln1_g[l]), ln1_b=row(ln1_b[l]), ln2_g=row(ln2_g[l]), ln2_b=row(ln2_b[l]),
                  ple_norm_g=row(ple_norm_g[l]),
                  dtb_f=head_table(dt_bias[l, 0]), dtb_r=head_table(dt_bias[l, 1]),
                  alog_f=head_table(a_log[l, 0]), alog_r=head_table(a_log[l, 1]),
                  dskip_full=row(jnp.repeat(d_skip[l], HEAD_DIM)))
        return W, sm

    lw = [layer_weights(l) for l in range(L)]
    xl = x[0]
    xlb = xl.astype(BF16)
    saved = []
    for l in range(L):
        xl, xlb, sv = _layer_fwd(cf, xl, xlb, p[l, 0].astype(BF16), lw[l][0], lw[l][1])
        saved.append(sv)
    grads = [None] * L
    dxl = None
    for l in reversed(range(L)):
        if l == L - 1:
            dxl, grads[l] = _layer_bwd(cf, saved[l], lw[l][0], lw[l][1], target=loss_target[0], xn=xl)
        else:
            dxl, grads[l] = _layer_bwd(cf, saved[l], lw[l][0], lw[l][1], dxn=dxl)
    loss = lax.psum(0.5 / D * jnp.sum(grads[L - 1]['loss_sq']), ("x", "y", "c"))
    grad_x = dxl[None]

    def blocks(n, l):
        g = grads[l][n]
        if n == 'conv_a_w':
            g = g.sum(axis=1)[:KA]
        elif n == 'ssm_conv_w':
            g = g.sum(axis=1)[:KB]
        if n in _COL_SHARDED:
            return g.reshape(g.shape[0], 4, g.shape[1] // 4).transpose(1, 0, 2)
        return g.reshape(4, g.shape[0] // 4, g.shape[1])

    parts = chip_exchange("scatter_grads", [[blocks(n, l) for l in range(L)] for n in _BIG], gather=False)
    chip_sums = [sum_chips("chip_sum_" + n, pr.reshape(4, L * pr.shape[2], pr.shape[3])) for n, pr in zip(_BIG, parts)]
    sib_sums = sibling_swap("core_swap", chip_sums)
    res = {}
    for n, mine, sib in zip(_BIG, chip_sums, sib_sums):
        shp = wt[n].shape
        flat = lambda a: a.reshape(shp[0] * shp[1], shp[2])
        outs = adamw_shard("adamw_" + n, mine, sib, flat(wt[n]), flat(mo[n]), flat(vo[n]))
        res[n] = [o.reshape(shp) for o in outs]

    def small_pieces(l):
        gl = grads[l]
        A = -jnp.exp(a_log[l])
        d = dict(gl)
        d_alog = jnp.concatenate([gl['dA_f'].sum(axis=1)[:H] * A[0], gl['dA_r'].sum(axis=1)[:H] * A[1]])
        d['a_log'] = jnp.pad(d_alog[None], ((0, SUBLANES - 1), (0, 0)))
        d['dt_bias'] = gl['dt_bias'][:, :2 * H]
        d['d_skip'] = gl['dskip_full'].reshape(SUBLANES, H, HEAD_DIM).sum(axis=-1)
        return [_pad_lanes(d[n], _ceil_to(d[n].shape[1], LANES)) for n in _SMALL]

    widths = [_ceil_to(math.prod(wt[n].shape[1:]), LANES) for n in _SMALL]
    packed = jnp.concatenate([pc for l in range(L) for pc in small_pieces(l)], axis=1)
    gathered = all8_gather("gather_small", packed)

    def pack_params(src):
        return jnp.concatenate([_pad_lanes(src[n][l].reshape(1, -1), wd) for l in range(L) for n, wd in zip(_SMALL, widths)],
                               axis=1)

    small_out = adamw_small("adamw_small", gathered, pack_params(wt), pack_params(mo), pack_params(vo))
    off = 0
    per = {n: [[] for _ in range(4)] for n in _SMALL}
    for l in range(L):
        for n, wd in zip(_SMALL, widths):
            size = math.prod(wt[n].shape[1:])
            for k in range(4):
                per[n][k].append(small_out[k][0, off:off + size].reshape(wt[n].shape[1:]))
            off += wd
    for n in _SMALL:
        res[n] = [jnp.stack(per[n][k]) for k in range(4)]

    return (loss, grad_x, *[res[n][0] for n in _WEIGHTS], *[res[n][1] for n in _WEIGHTS],
            *[res[n][2] for n in _WEIGHTS], *[res[n][3] for n in _WEIGHTS])
```

```python
import math

import jax
import jax.numpy as jnp
from jax import lax
from jax.experimental import pallas as pl
from jax.experimental.pallas import tpu as pltpu

F32 = jnp.float32
BF16 = jnp.bfloat16

VMEM_LIMIT_BYTES = 56 * 1024 * 1024
LANES = 128
SUBLANES = 8

CHUNK = 128
D_STATE = 128
HEAD_DIM = 64
LN_EPS = 1e-5
RMS_EPS = 1e-6
ADAM_LR = 0.001
ADAM_B1 = 0.9
ADAM_B2 = 0.999
ADAM_EPS = 1e-08
ADAM_WD = 0.01
ADAM_STEP = 10
HALO = 16
MESH = pl.DeviceIdType.MESH


def _params(**kw):
    return pltpu.CompilerParams(vmem_limit_bytes=VMEM_LIMIT_BYTES, **kw)


def _sig(x):
    return jax.nn.sigmoid(x)


def _dsilu(x, s):
    return s * (1.0 + x * (1.0 - s))


def _ln_stats(r):
    mu = jnp.mean(r, axis=-1, keepdims=True)
    xc = r - mu
    var = jnp.mean(xc * xc, axis=-1, keepdims=True)
    rstd = lax.rsqrt(var + LN_EPS)
    return xc * rstd, rstd


def _ln_bwd(dy, xhat, rstd, g):
    dxh = dy * g
    m1 = jnp.mean(dxh, axis=-1, keepdims=True)
    m2 = jnp.mean(dxh * xhat, axis=-1, keepdims=True)
    return rstd * (dxh - m1 - xhat * m2)


def _f32(v):
    return v if v.dtype == F32 else v.astype(F32)


def _rows8(v):
    tm, w = v.shape
    return v.reshape(tm // SUBLANES, SUBLANES, w).sum(axis=0)


def fused_mm(name, prods, extras, epi, row_outs, col_outs=(), *, M, tm, tn, nj=1, nk=1,
             passthrough=None):
    np_ = len(prods)
    ne = len(extras)
    nro = len(row_outs)
    nco = len(col_outs)
    use_acc = nk > 1

    def body(*refs):
        a_refs = [refs[2 * p] for p in range(np_)]
        w_refs = [refs[2 * p + 1] for p in range(np_)]
        pos = 2 * np_
        e_refs = refs[pos:pos + ne]
        pos += ne
        if passthrough is not None:
            pos += 1
        ro_refs = refs[pos:pos + nro]
        pos += nro
        co_refs = refs[pos:pos + nco]
        pos += nco
        acc_ref = refs[pos] if use_acc else None
        i = pl.program_id(1)
        k = pl.program_id(2)

        def prod(p):
            a = a_refs[p][...]
            if a.dtype != BF16:
                a = a.astype(BF16)
            return jnp.dot(a, w_refs[p][...], preferred_element_type=F32)

        def finish(acc):
            rows, cols = epi(acc, [_f32(r[...]) for r in e_refs])
            for v, o in zip(rows, ro_refs):
                o[...] = v.astype(o.dtype)
            for v, o in zip(cols, co_refs):
                v8 = _rows8(v)

                @pl.when(i == 0)
                def _():
                    o[...] = v8

                @pl.when(i > 0)
                def _():
                    o[...] += v8

        if not use_acc:
            acc = prod(0)
            for p in range(1, np_):
                acc = acc + prod(p)
            finish(acc)
        else:
            @pl.when(k == 0)
            def _():
                acc = None
                for p in range(np_):
                    acc = prod(p) if acc is None else acc + prod(p)
                acc_ref[...] = acc

            @pl.when(k > 0)
            def _():
                acc = None
                for p in range(np_):
                    if prods[p][3]:
                        acc = prod(p) if acc is None else acc + prod(p)
                acc_ref[...] += acc

            @pl.when(k == nk - 1)
            def _():
                finish(acc_ref[...])

    in_specs = []
    args = []
    for a, w, joff, ksplit in prods:
        K = a.shape[1]
        if ksplit:
            tk = K // nk
            in_specs.append(pl.BlockSpec((tm, tk), lambda j, i, k: (i, k)))
            in_specs.append(pl.BlockSpec((tk, tn), lambda j, i, k, joff=joff: (k, j + joff)))
        else:
            in_specs.append(pl.BlockSpec((tm, K), lambda j, i, k: (i, 0)))
            in_specs.append(pl.BlockSpec((K, tn), lambda j, i, k, joff=joff: (0, j + joff)))
        args += [a, w]
    for arr, kind, width, c0 in extras:
        if kind == 'row':
            in_specs.append(pl.BlockSpec((tm, width), lambda j, i, k, c0=c0: (i, c0 + j)))
        else:
            in_specs.append(pl.BlockSpec((arr.shape[0], width), lambda j, i, k, c0=c0: (0, c0 + j)))
        args.append(arr)
    aliases = {}
    if passthrough is not None:
        arr, oidx = passthrough
        in_specs.append(pl.BlockSpec(memory_space=pl.ANY))
        aliases = {len(args): oidx}
        args.append(arr)
    out_shape = []
    out_specs = []
    for n_total, dtype, width, c0 in row_outs:
        out_shape.append(jax.ShapeDtypeStruct((M, n_total), dtype))
        out_specs.append(pl.BlockSpec((tm, width), lambda j, i, k, c0=c0: (i, c0 + j)))
    for n_total, width, c0 in col_outs:
        out_shape.append(jax.ShapeDtypeStruct((SUBLANES, n_total), F32))
        out_specs.append(pl.BlockSpec((SUBLANES, width), lambda j, i, k, c0=c0: (0, c0 + j)))
    scratch = [pltpu.VMEM((tm, tn), F32)] if use_acc else []
    return pl.pallas_call(
        body, name=name, grid=(nj, M // tm, nk), in_specs=in_specs, out_specs=out_specs,
        out_shape=out_shape, scratch_shapes=scratch, input_output_aliases=aliases,
        compiler_params=_params(dimension_semantics=("arbitrary", "arbitrary", "arbitrary")),
    )(*args)


def mm_tn(name, a, b, *, tm, tk, tn):
    M, K = a.shape
    N = b.shape[1]

    def body(a_ref, b_ref, o_ref):
        m = pl.program_id(2)
        p = lax.dot_general(a_ref[...], b_ref[...], (((0,), (0,)), ((), ())),
                            preferred_element_type=F32)

        @pl.when(m == 0)
        def _():
            o_ref[...] = p

        @pl.when(m > 0)
        def _():
            o_ref[...] += p

    return pl.pallas_call(
        body, name=name, grid=(K // tk, N // tn, M // tm),
        in_specs=[pl.BlockSpec((tm, tk), lambda kk, j, m: (m, kk)),
                  pl.BlockSpec((tm, tn), lambda kk, j, m: (m, j))],
        out_specs=pl.BlockSpec((tk, tn), lambda kk, j, m: (kk, j)),
        out_shape=jax.ShapeDtypeStruct((K, N), F32),
        compiler_params=_params(dimension_semantics=("arbitrary", "arbitrary", "arbitrary")),
    )(a, b)


def row_call(name, fn, ins, row_outs, col_outs=(), *, M, tm, nc=1):
    ni = len(ins)
    nro = len(row_outs)

    def body(*refs):
        i = pl.program_id(1)
        vals = [_f32(r[...]) for r in refs[:ni]]
        rows, cols = fn(*vals)
        for v, o in zip(rows, refs[ni:ni + nro]):
            o[...] = v.astype(o.dtype)
        for v, o in zip(cols, refs[ni + nro:]):
            v8 = _rows8(v)

            @pl.when(i == 0)
            def _():
                o[...] = v8

            @pl.when(i > 0)
            def _():
                o[...] += v8

    in_specs = []
    for arr, kind, width, c0, cmul in ins:
        if kind == 'row':
            in_specs.append(pl.BlockSpec((tm, width), lambda cj, i, c0=c0, cmul=cmul: (i, c0 + cmul * cj)))
        else:
            in_specs.append(pl.BlockSpec((arr.shape[0], width), lambda cj, i, c0=c0, cmul=cmul: (0, c0 + cmul * cj)))
    out_shape = []
    out_specs = []
    for n_total, dtype, width, c0, cmul in row_outs:
        out_shape.append(jax.ShapeDtypeStruct((M, n_total), dtype))
        out_specs.append(pl.BlockSpec((tm, width), lambda cj, i, c0=c0, cmul=cmul: (i, c0 + cmul * cj)))
    for n_total, width, c0, cmul in col_outs:
        out_shape.append(jax.ShapeDtypeStruct((SUBLANES, n_total), F32))
        out_specs.append(pl.BlockSpec((SUBLANES, width), lambda cj, i, c0=c0, cmul=cmul: (0, c0 + cmul * cj)))
    return pl.pallas_call(
        body, name=name, grid=(nc, M // tm), in_specs=in_specs, out_specs=out_specs,
        out_shape=out_shape,
        compiler_params=_params(dimension_semantics=("arbitrary", "arbitrary")),
    )(*[a[0] for a in ins])


def conv_call(name, src, src_c0, w, K, epi, extras, row_outs, col_outs=(), *, M, tm, cw, nc,
              reverse, xin=None, passthrough=None):
    pad = (K - 1) // 2
    assert pad <= HALO - 1
    R = tm // HALO
    nblk = M // HALO
    n_i = M // tm
    Kp = w.shape[0]
    ne = len(extras)
    nro = len(row_outs)
    nco = len(col_outs)
    rb = 64
    cbw = min(cw, 256)
    n_copies = SUBLANES if K > SUBLANES else 1

    def body(*refs):
        main_ref, prev_ref, next_ref, w_ref = refs[:4]
        pos = 4
        xin_ref = None
        if xin is not None:
            xin_ref = refs[pos]
            pos += 1
        e_refs = refs[pos:pos + ne]
        pos += ne
        if passthrough is not None:
            pos += 1
        ro_refs = refs[pos:pos + nro]
        pos += nro
        co_refs = refs[pos:pos + nco]
        pos += nco
        dw_ref = None
        if xin is not None:
            dw_ref = refs[pos]
            pos += 1
        ext_ref, conv_ref = refs[pos], refs[pos + 1]
        i = pl.program_id(1)

        ext_ref[0, 0:HALO, :] = jnp.where(i == 0, 0.0, prev_ref[...].astype(F32))
        ext_ref[0, HALO:HALO + tm, :] = main_ref[...].astype(F32)
        ext_ref[0, HALO + tm:, :] = jnp.where(i == n_i - 1, 0.0, next_ref[...].astype(F32))
        if dw_ref is not None:
            @pl.when(i == 0)
            def _():
                dw_ref[...] = jnp.zeros_like(dw_ref)

        n_sh = tm + 2 * HALO - SUBLANES
        for c0 in range(0, cw, cbw):
            for sft in range(1, n_copies):
                ext_ref[sft, 0:n_sh, c0:c0 + cbw] = ext_ref[0, sft:sft + n_sh, c0:c0 + cbw]

        for c0 in range(0, cw, cbw):
            for r0 in range(0, tm, rb):
                acc = jnp.zeros((rb, cbw), F32)
                if xin_ref is not None:
                    xblk = xin_ref[r0:r0 + rb, c0:c0 + cbw].astype(F32)
                for k in range(K):
                    off = HALO + r0 + ((pad - k) if reverse else (k - pad))
                    sft = off % SUBLANES if n_copies > 1 else 0
                    d = ext_ref[sft, off - sft:off - sft + rb, c0:c0 + cbw]
                    acc = acc + d * w_ref[k:k + 1, c0:c0 + cbw]
                    if xin_ref is not None:
                        dw_ref[k, :, c0:c0 + cbw] += _rows8(xblk * d)
                conv_ref[r0:r0 + rb, c0:c0 + cbw] = acc

        rows, cols = epi(conv_ref[...], [_f32(r[...]) for r in e_refs])
        for v, o in zip(rows, ro_refs):
            o[...] = v.astype(o.dtype)
        for v, o in zip(cols, co_refs):
            v8 = _rows8(v)

            @pl.when(i == 0)
            def _():
                o[...] = v8

            @pl.when(i > 0)
            def _():
                o[...] += v8

    in_specs = [
        pl.BlockSpec((tm, cw), lambda cj, i: (i, src_c0 + cj)),
        pl.BlockSpec((HALO, cw), lambda cj, i: (jnp.maximum(i * R - 1, 0), src_c0 + cj)),
        pl.BlockSpec((HALO, cw), lambda cj, i: (jnp.minimum((i + 1) * R, nblk - 1), src_c0 + cj)),
        pl.BlockSpec((Kp, cw), lambda cj, i: (0, cj)),
    ]
    args = [src, src, src, w]
    if xin is not None:
        in_specs.append(pl.BlockSpec((tm, cw), lambda cj, i, c0=xin[1]: (i, c0 + cj)))
        args.append(xin[0])
    for arr, kind, width, c0, cmul in extras:
        if kind == 'row':
            in_specs.append(pl.BlockSpec((tm, width), lambda cj, i, c0=c0, cmul=cmul: (i, c0 + cmul * cj)))
        else:
            in_specs.append(pl.BlockSpec((arr.shape[0], width), lambda cj, i, c0=c0, cmul=cmul: (0, c0 + cmul * cj)))
        args.append(arr)
    aliases = {}
    if passthrough is not None:
        in_specs.append(pl.BlockSpec(memory_space=pl.ANY))
        aliases = {len(args): passthrough[1]}
        args.append(passthrough[0])
    out_shape = []
    out_specs = []
    for n_total, dtype, width, c0, cmul in row_outs:
        out_shape.append(jax.ShapeDtypeStruct((M, n_total), dtype))
        out_specs.append(pl.BlockSpec((tm, width), lambda cj, i, c0=c0, cmul=cmul: (i, c0 + cmul * cj)))
    for n_total, width, c0, cmul in col_outs:
        out_shape.append(jax.ShapeDtypeStruct((SUBLANES, n_total), F32))
        out_specs.append(pl.BlockSpec((SUBLANES, width), lambda cj, i, c0=c0, cmul=cmul: (0, c0 + cmul * cj)))
    if xin is not None:
        out_shape.append(jax.ShapeDtypeStruct((Kp, SUBLANES, cw * nc), F32))
        out_specs.append(pl.BlockSpec((Kp, SUBLANES, cw), lambda cj, i: (0, 0, cj)))
    return pl.pallas_call(
        body, name=name, grid=(nc, n_i), in_specs=in_specs, out_specs=out_specs,
        out_shape=out_shape, input_output_aliases=aliases,
        scratch_shapes=[pltpu.VMEM((n_copies, tm + 2 * HALO, cw), F32), pltpu.VMEM((tm, cw), F32)],
        compiler_params=_params(dimension_semantics=("arbitrary", "arbitrary")),
    )(*args)


def _split_dot(m_bf16, v, n_pass, dims=None):
    out = None
    rest = v
    for p in range(n_pass):
        piece = rest.astype(BF16)
        if p + 1 < n_pass:
            rest = rest - piece.astype(F32)
        if dims is None:
            t = jnp.dot(m_bf16, piece, preferred_element_type=F32)
        else:
            t = lax.dot_general(m_bf16, piece, dims, preferred_element_type=F32)
        out = t if out is None else out + t
    return out


def _split_dot_r(v, m_bf16, n_pass):
    out = None
    rest = v
    for p in range(n_pass):
        piece = rest.astype(BF16)
        if p + 1 < n_pass:
            rest = rest - piece.astype(F32)
        t = jnp.dot(piece, m_bf16, preferred_element_type=F32)
        out = t if out is None else out + t
    return out


def _softplus(x):
    return jnp.maximum(x, 0.0) + jnp.log1p(jnp.exp(-jnp.abs(x)))


NT_DIMS = (((1,), (1,)), ((), ()))
TN_DIMS = (((0,), (0,)), ((), ()))


def _ssd_common(dtraw, dtbT, alogT, rev, n_heads):
    L = CHUNK
    if rev:
        dtraw = pltpu.roll(dtraw, LANES - n_heads, 1)
    preT = dtraw.T + dtbT
    dtT = _softplus(preT)
    AT = -jnp.exp(alogT)
    aT = dtT * AT
    ri = lax.broadcasted_iota(jnp.int32, (L, L), 0)
    ci = lax.broadcasted_iota(jnp.int32, (L, L), 1)
    up = (ri >= ci) if rev else (ri <= ci)
    lo = (ri <= ci) if rev else (ri >= ci)
    csT = _split_dot_r(aT, up.astype(BF16), 3)
    last = 0 if rev else L - 1
    lastB = jnp.broadcast_to(csT[:, last:last + 1], (L, L))
    return dict(preT=preT, dtT=dtT, AT=AT, csT=csT, cs=csT.T, up=up, lo=lo, ci=ci, last=last,
                doutT=jnp.exp(csT), dstT=jnp.exp(lastB - csT), totB=jnp.exp(lastB))


def ssd_fwd(name, xbc, dtraw, dtbT, alogT, *, S, DI, G, H, rev):
    NC = S // CHUNK
    R = H // G
    GW = R * HEAD_DIM
    N = D_STATE
    XBC = xbc.shape[1]
    P = HEAD_DIM

    def body(xbc_ref, dtraw_ref, dtb_ref, alog_ref, y_ref, st_ref, h_ref):
        c = pl.program_id(0)

        @pl.when(c == 0)
        def _():
            h_ref[...] = jnp.zeros_like(h_ref)

        q = _ssd_common(dtraw_ref[...], dtb_ref[...], alog_ref[...], rev, H)
        cs, csT, dtT, doutT, totB = q['cs'], q['csT'], q['dtT'], q['doutT'], q['totB']
        wstT = q['dstT'] * dtT
        for g in range(G):
            Bg = xbc_ref[:, DI + g * N:DI + (g + 1) * N].astype(BF16)
            Cg = xbc_ref[:, DI + G * N + g * N:DI + G * N + (g + 1) * N].astype(BF16)
            CBT = lax.dot_general(Bg, Cg, NT_DIMS, preferred_element_type=F32)
            HT = h_ref[g]
            yoffT = lax.dot_general(HT.astype(BF16), Cg, NT_DIMS, preferred_element_type=F32)
            xT = xbc_ref[:, g * GW:(g + 1) * GW].T
            ys, xws, tots = [], [], []
            for r in range(R):
                h = g * R + r
                segT = jnp.where(q['up'], csT[h:h + 1, :] - cs[:, h:h + 1], -1e30)
                GT = (CBT * jnp.exp(segT)).astype(BF16)
                xTh = xT[r * P:(r + 1) * P, :]
                XTh = (xTh * dtT[h:h + 1, :]).astype(BF16)
                ydT = jnp.dot(XTh, GT, preferred_element_type=F32)
                ys.append(ydT + yoffT[r * P:(r + 1) * P, :] * doutT[h:h + 1, :])
                xws.append(xTh * wstT[h:h + 1, :])
                tots.append(jnp.broadcast_to(totB[h:h + 1, :], (P, N)))
            y_ref[:, g * GW:(g + 1) * GW] = jnp.concatenate(ys, axis=0).T
            xwT = jnp.concatenate(xws, axis=0).astype(BF16)
            ST = jnp.dot(xwT, Bg, preferred_element_type=F32)
            st_ref[0, g] = HT
            h_ref[g] = HT * jnp.concatenate(tots, axis=0) + ST

    cmap = (lambda c: (NC - 1 - c, 0)) if rev else (lambda c: (c, 0))
    smap = (lambda c: (NC - 1 - c, 0, 0, 0)) if rev else (lambda c: (c, 0, 0, 0))
    const = lambda c: (0, 0)
    return pl.pallas_call(
        body, name=name, grid=(NC,),
        in_specs=[pl.BlockSpec((CHUNK, XBC), cmap), pl.BlockSpec((CHUNK, LANES), cmap),
                  pl.BlockSpec((LANES, LANES), const), pl.BlockSpec((LANES, LANES), const)],
        out_specs=[pl.BlockSpec((CHUNK, DI), cmap), pl.BlockSpec((1, G, GW, N), smap)],
        out_shape=[jax.ShapeDtypeStruct((S, DI), F32), jax.ShapeDtypeStruct((NC, G, GW, N), F32)],
        scratch_shapes=[pltpu.VMEM((G, GW, N), F32)],
        compiler_params=_params(dimension_semantics=("arbitrary",)),
    )(xbc, dtraw, dtbT, alogT)


def ssd_bwd(name, xbc, dtraw, dy, st, dtbT, alogT, *, S, DI, G, H, rev, tail=None):
    NC = S // CHUNK
    R = H // G
    GW = R * HEAD_DIM
    N = D_STATE
    XBC = xbc.shape[1]
    P = HEAD_DIM
    L = CHUNK

    def body(*refs):
        xbc_ref, dtraw_ref, dy_ref, st_ref, dtb_ref, alog_ref = refs[:6]
        if tail is None:
            dxbc_ref, ddt_ref, da_ref, dh_ref, dcst_ref, p2t_ref, p3t_ref, e2t_ref = refs[6:]
        else:
            other_ref, cbv_ref, dsk_ref = refs[6:9]
            dxbc_ref, ddt_ref, da_ref, dcol_ref, dh_ref, dcst_ref, p2t_ref, p3t_ref, e2t_ref = refs[9:]
        c = pl.program_id(0)

        @pl.when(c == 0)
        def _():
            dh_ref[...] = jnp.zeros_like(dh_ref)
            da_ref[...] = jnp.zeros_like(da_ref)
            dcst_ref[...] = jnp.zeros_like(dcst_ref)
            p2t_ref[...] = jnp.zeros_like(p2t_ref)
            p3t_ref[...] = jnp.zeros_like(p3t_ref)
            e2t_ref[...] = jnp.zeros_like(e2t_ref)

        q = _ssd_common(dtraw_ref[...], dtb_ref[...], alog_ref[...], rev, H)
        cs, csT, dtT, doutT, dstT, totB = q['cs'], q['csT'], q['dtT'], q['doutT'], q['dstT'], q['totB']
        wstT = dstT * dtT
        lane = q['ci']
        dcs_c = jnp.zeros((L, LANES), F32)
        for g in range(G):
            Bg = xbc_ref[:, DI + g * N:DI + (g + 1) * N].astype(BF16)
            Cg = xbc_ref[:, DI + G * N + g * N:DI + G * N + (g + 1) * N].astype(BF16)
            CB = lax.dot_general(Cg, Bg, NT_DIMS, preferred_element_type=F32)
            HpT = st_ref[0, g]
            HpTb = HpT.astype(BF16)
            dHT = dh_ref[g]
            dHTb = dHT.astype(BF16)
            BdHT = lax.dot_general(dHTb, Bg, NT_DIMS, preferred_element_type=F32)
            yoffT = lax.dot_general(HpTb, Cg, NT_DIMS, preferred_element_type=F32)
            xT = xbc_ref[:, g * GW:(g + 1) * GW].T
            dyT = dy_ref[:, g * GW:(g + 1) * GW].T
            dCB = jnp.zeros((L, L), F32)
            dyds, xws, tots, dxs = [], [], [], []
            for r in range(R):
                h = g * R + r
                blk = slice(r * P, (r + 1) * P)
                Lm = jnp.exp(jnp.where(q['lo'], cs[:, h:h + 1] - csT[h:h + 1, :], -1e30))
                Gm = (CB * Lm).astype(BF16)
                xTh = xT[blk, :]
                dyTh = dyT[blk, :]
                xThb = xTh.astype(BF16)
                dyThb = dyTh.astype(BF16)
                u1T = jnp.dot(dyThb, Gm, preferred_element_type=F32)
                uT = u1T + BdHT[blk, :] * dstT[h:h + 1, :]
                dGx = lax.dot_general(dyThb, xThb, TN_DIMS, preferred_element_type=F32)
                T = dGx * (Lm * dtT[h:h + 1, :])
                dCB = dCB + T
                Mseg = T * CB
                dcs_c = jnp.where(lane == h, jnp.sum(Mseg, axis=1, keepdims=True), dcs_c)
                dydTh = dyTh * doutT[h:h + 1, :]
                xwTh = xTh * wstT[h:h + 1, :]
                p3row = jnp.sum(xwTh * BdHT[blk, :], axis=0, keepdims=True)
                dcst_ref[h:h + 1, :] = (jnp.sum(dydTh * yoffT[blk, :], axis=0, keepdims=True)
                                        - jnp.sum(Mseg, axis=0, keepdims=True) - p3row)
                p2t_ref[h:h + 1, :] = jnp.sum(xTh * uT, axis=0, keepdims=True)
                p3t_ref[h:h + 1, :] = p3row
                e2t_ref[h:h + 1, :] = jnp.sum(HpT[blk, :] * dHT[blk, :], axis=0, keepdims=True)
                dxs.append(uT * dtT[h:h + 1, :])
                dyds.append(dydTh)
                xws.append(xwTh)
                tots.append(jnp.broadcast_to(totB[h:h + 1, :], (P, N)))
            dxbc_ref[:, g * GW:(g + 1) * GW] = jnp.concatenate(dxs, axis=0).T
            dydT = jnp.concatenate(dyds, axis=0).astype(BF16)
            xwT = jnp.concatenate(xws, axis=0).astype(BF16)
            dCBb = dCB.astype(BF16)
            dC = (jnp.dot(dCBb, Bg, preferred_element_type=F32)
                  + lax.dot_general(dydT, HpTb, TN_DIMS, preferred_element_type=F32))
            dB = (lax.dot_general(dCBb, Cg, TN_DIMS, preferred_element_type=F32)
                  + lax.dot_general(xwT, dHTb, TN_DIMS, preferred_element_type=F32))
            dxbc_ref[:, DI + g * N:DI + (g + 1) * N] = dB
            dxbc_ref[:, DI + G * N + g * N:DI + G * N + (g + 1) * N] = dC
            dh_ref[g] = (dHT * jnp.concatenate(tots, axis=0)
                         + jnp.dot(dydT, Cg, preferred_element_type=F32))
        e1 = jnp.sum(p3t_ref[...], axis=1, keepdims=True)
        e2 = jnp.sum(e2t_ref[...], axis=1, keepdims=True)
        dcsT = (dcst_ref[...] + dcs_c.T
                + jnp.where(lane == q['last'], e1 + totB * e2, 0.0))
        daT = _split_dot_r(dcsT, q['lo'].astype(BF16), 3)
        ddtT = daT * q['AT'] + p2t_ref[...]
        da_ref[...] += daT * dtT
        ddraw = jnp.where(lane < H, (ddtT * _sig(q['preT'])).T, 0.0)
        if rev:
            ddraw = pltpu.roll(ddraw, H, 1)
        ddt_ref[...] = ddraw
        if tail is not None:
            for c0 in range(0, XBC, DI):
                d = dxbc_ref[:, c0:c0 + DI] + other_ref[:, c0:c0 + DI]
                if c0 == 0:
                    d = d + dy_ref[...] * dsk_ref[...]
                cb = cbv_ref[:, c0:c0 + DI]
                dcb = d * _dsilu(cb, _sig(cb))
                dxbc_ref[:, c0:c0 + DI] = dcb
                part = _rows8(dcb)

                @pl.when(c == 0)
                def _():
                    dcol_ref[:, c0:c0 + DI] = part

                @pl.when(c > 0)
                def _():
                    dcol_ref[:, c0:c0 + DI] += part

    cmap = (lambda c: (c, 0)) if rev else (lambda c: (NC - 1 - c, 0))
    smap = (lambda c: (c, 0, 0, 0)) if rev else (lambda c: (NC - 1 - c, 0, 0, 0))
    const = lambda c: (0, 0)
    sq = pltpu.VMEM((LANES, CHUNK), F32)
    in_specs = [pl.BlockSpec((CHUNK, XBC), cmap), pl.BlockSpec((CHUNK, LANES), cmap),
                pl.BlockSpec((CHUNK, DI), cmap),
                pl.BlockSpec((1, G, GW, N), smap),
                pl.BlockSpec((LANES, LANES), const), pl.BlockSpec((LANES, LANES), const)]
    out_specs = [pl.BlockSpec((CHUNK, XBC), cmap), pl.BlockSpec((CHUNK, LANES), cmap),
                 pl.BlockSpec((LANES, LANES), const)]
    out_shape = [jax.ShapeDtypeStruct((S, XBC), F32), jax.ShapeDtypeStruct((S, LANES), F32),
                 jax.ShapeDtypeStruct((LANES, LANES), F32)]
    args = [xbc, dtraw, dy, st, dtbT, alogT]
    if tail is not None:
        in_specs += [pl.BlockSpec((CHUNK, XBC), cmap), pl.BlockSpec((CHUNK, XBC), cmap),
                     pl.BlockSpec((1, DI), const)]
        out_specs.append(pl.BlockSpec((SUBLANES, XBC), const))
        out_shape.append(jax.ShapeDtypeStruct((SUBLANES, XBC), F32))
        args += list(tail)
    return pl.pallas_call(
        body, name=name, grid=(NC,), in_specs=in_specs, out_specs=out_specs, out_shape=out_shape,
        scratch_shapes=[pltpu.VMEM((G, GW, N), F32), sq, sq, sq, sq],
        compiler_params=_params(dimension_semantics=("arbitrary",)),
    )(*args)


ANY = pl.BlockSpec(memory_space=pl.ANY)


def chip_exchange(name, groups, gather):
    flat = [arr for grp in groups for arr in grp]
    n_in = len(flat)
    n_out = len(groups)
    n_rc = 3 * n_in

    def body(*refs):
        in_refs = refs[:n_in]
        out_refs = refs[n_in:n_in + n_out]
        send, recv, loc = refs[n_in + n_out:]
        x, y, c = lax.axis_index("x"), lax.axis_index("y"), lax.axis_index("c")
        me = 2 * x + y
        peers = [(1 - x, y), (x, 1 - y), (1 - x, 1 - y)]
        local, remote = [], []
        q = 0
        for a, grp in enumerate(groups):
            for l in range(len(grp)):
                src = in_refs[q]
                dst = out_refs[a].at[me] if gather else out_refs[a].at[me, l]
                own = src if gather else src.at[me]
                lc = pltpu.make_async_copy(own, dst, loc.at[q])
                lc.start()
                local.append(lc)
                for j, (px, py) in enumerate(peers):
                    blk = src if gather else src.at[2 * px + py]
                    rc = pltpu.make_async_remote_copy(
                        src_ref=blk, dst_ref=dst, send_sem=send.at[3 * q + j], recv_sem=recv.at[3 * q + j],
                        device_id=(px, py, c), device_id_type=MESH)
                    rc.start()
                    remote.append(rc)
                q += 1
        for lc in local:
            lc.wait()
        for rc in remote:
            rc.wait()

    out_shape = []
    for grp in groups:
        a0 = grp[0]
        if gather:
            out_shape.append(jax.ShapeDtypeStruct((4,) + a0.shape, a0.dtype))
        else:
            out_shape.append(jax.ShapeDtypeStruct((4, len(grp)) + a0.shape[1:], a0.dtype))
    return pl.pallas_call(
        body, name=name, in_specs=[ANY] * n_in, out_specs=[ANY] * n_out, out_shape=out_shape,
        scratch_shapes=[pltpu.SemaphoreType.DMA((n_rc,)), pltpu.SemaphoreType.DMA((n_rc,)),
                        pltpu.SemaphoreType.DMA((n_in,))],
    )(*flat)


def gather_layer(name, split, whole):
    ns, nw = len(split), len(whole)
    n = ns + nw
    n_rc = 3 * (n + ns)

    def body(*refs):
        in_refs = refs[:n]
        out_refs = refs[n:2 * n]
        send, recv, loc = refs[2 * n:]
        x, y, c = lax.axis_index("x"), lax.axis_index("y"), lax.axis_index("c")
        me = 2 * x + y
        sibling = (x, y, 1 - c)
        peers = [(1 - x, y), (x, 1 - y), (1 - x, 1 - y)]

        def region(a, chip, half):
            if a >= ns:
                return out_refs[a].at[chip]
            hr = split[a].shape[0] // 2
            return out_refs[a].at[chip, pl.ds(half * hr, hr)]

        def mine(a):
            if a >= ns:
                return in_refs[a]
            hr = split[a].shape[0] // 2
            return in_refs[a].at[pl.ds(c * hr, hr)]

        local = []
        for a in range(n):
            lc = pltpu.make_async_copy(in_refs[a], out_refs[a].at[me], loc.at[a])
            lc.start()
            local.append(lc)
        sends = []
        for a in range(n):
            for j, (px, py) in enumerate(peers):
                rc = pltpu.make_async_remote_copy(
                    src_ref=mine(a), dst_ref=region(a, me, c), send_sem=send.at[3 * a + j],
                    recv_sem=recv.at[3 * a + j], device_id=(px, py, c), device_id_type=MESH)
                rc.start()
                sends.append(rc)
        for a in range(n):
            for j, (px, py) in enumerate(peers):
                chip = 2 * px + py
                landed = pltpu.make_async_remote_copy(
                    src_ref=mine(a), dst_ref=region(a, chip, c), send_sem=send.at[3 * a + j],
                    recv_sem=recv.at[3 * a + j], device_id=(px, py, c), device_id_type=MESH)
                landed.wait_recv()
                if a < ns:
                    fw = pltpu.make_async_remote_copy(
                        src_ref=region(a, chip, c), dst_ref=region(a, chip, c), send_sem=send.at[3 * n + 3 * a + j],
                        recv_sem=recv.at[3 * n + 3 * a + j], device_id=sibling, device_id_type=MESH)
                    fw.start()
                    sends.append(fw)
        for a in range(ns):
            for j, (px, py) in enumerate(peers):
                chip = 2 * px + py
                pltpu.make_async_remote_copy(
                    src_ref=region(a, chip, 1 - c), dst_ref=region(a, chip, 1 - c), send_sem=send.at[3 * n + 3 * a + j],
                    recv_sem=recv.at[3 * n + 3 * a + j], device_id=sibling, device_id_type=MESH).wait_recv()
        for rc in sends:
            rc.wait_send()
        for lc in local:
            lc.wait()

    arrs = list(split) + list(whole)
    return pl.pallas_call(
        body, name=name, in_specs=[ANY] * n, out_specs=[ANY] * n,
        out_shape=[jax.ShapeDtypeStruct((4,) + a.shape, a.dtype) for a in arrs],
        scratch_shapes=[pltpu.SemaphoreType.DMA((n_rc,)), pltpu.SemaphoreType.DMA((n_rc,)),
                        pltpu.SemaphoreType.DMA((n,))],
    )(*arrs)


def core_send_half(name, arrs):
    n = len(arrs)

    def body(*refs):
        in_refs = refs[:n]
        out_refs = refs[n:2 * n]
        send, recv = refs[2 * n:]
        c = lax.axis_index("c")
        peer = (lax.axis_index("x"), lax.axis_index("y"), 1 - c)
        rcs = []
        for a in range(n):
            hr = arrs[a].shape[1] // 2
            rc = pltpu.make_async_remote_copy(
                src_ref=in_refs[a].at[:, pl.ds((1 - c) * hr, hr)], dst_ref=out_refs[a], send_sem=send.at[a],
                recv_sem=recv.at[a], device_id=peer, device_id_type=MESH)
            rc.start()
            rcs.append(rc)
        for rc in rcs:
            rc.wait()

    return pl.pallas_call(
        body, name=name, in_specs=[ANY] * n, out_specs=[ANY] * n,
        out_shape=[jax.ShapeDtypeStruct((4, a.shape[1] // 2, a.shape[2]), a.dtype) for a in arrs],
        scratch_shapes=[pltpu.SemaphoreType.DMA((n,)), pltpu.SemaphoreType.DMA((n,))],
    )(*arrs)


def core_fill(name, arrs, layer, n_layers):
    n = len(arrs)

    def body(*refs):
        out_refs = refs[n:2 * n]
        send, recv = refs[2 * n:]
        c = lax.axis_index("c")
        peer = (lax.axis_index("x"), lax.axis_index("y"), 1 - c)
        rcs = []
        for a in range(n):
            r = arrs[a].shape[0] // n_layers
            hr = r // 2
            rows = out_refs[a].at[pl.ds(layer * r + c * hr, hr)]
            rc = pltpu.make_async_remote_copy(src_ref=rows, dst_ref=rows, send_sem=send.at[a], recv_sem=recv.at[a],
                                              device_id=peer, device_id_type=MESH)
            rc.start()
            rcs.append(rc)
        for a in range(n):
            r = arrs[a].shape[0] // n_layers
            hr = r // 2
            theirs = out_refs[a].at[pl.ds(layer * r + (1 - c) * hr, hr)]
            pltpu.make_async_remote_copy(src_ref=theirs, dst_ref=theirs, send_sem=send.at[a], recv_sem=recv.at[a],
                                         device_id=peer, device_id_type=MESH).wait_recv()
        for rc in rcs:
            rc.wait_send()

    return pl.pallas_call(
        body, name=name, in_specs=[ANY] * n, out_specs=[ANY] * n,
        out_shape=[jax.ShapeDtypeStruct(a.shape, a.dtype) for a in arrs],
        input_output_aliases={a: a for a in range(n)},
        scratch_shapes=[pltpu.SemaphoreType.DMA((n,)), pltpu.SemaphoreType.DMA((n,))],
    )(*arrs)


def sibling_swap(name, arrs):
    n = len(arrs)

    def body(*refs):
        in_refs = refs[:n]
        out_refs = refs[n:2 * n]
        send, recv = refs[2 * n:]
        peer = (lax.axis_index("x"), lax.axis_index("y"), 1 - lax.axis_index("c"))
        rcs = []
        for a in range(n):
            rc = pltpu.make_async_remote_copy(src_ref=in_refs[a], dst_ref=out_refs[a], send_sem=send.at[a],
                                              recv_sem=recv.at[a], device_id=peer, device_id_type=MESH)
            rc.start()
            rcs.append(rc)
        for rc in rcs:
            rc.wait()

    return pl.pallas_call(
        body, name=name, in_specs=[ANY] * n, out_specs=[ANY] * n,
        out_shape=[jax.ShapeDtypeStruct(a.shape, a.dtype) for a in arrs],
        scratch_shapes=[pltpu.SemaphoreType.DMA((n,)), pltpu.SemaphoreType.DMA((n,))],
    )(*arrs)


def all8_gather(name, v):
    flips = [(fx, fy, fc) for fx in (0, 1) for fy in (0, 1) for fc in (0, 1) if (fx, fy, fc) != (0, 0, 0)]

    def body(v_ref, out_ref, send, recv, loc):
        x, y, c = lax.axis_index("x"), lax.axis_index("y"), lax.axis_index("c")
        me = 4 * x + 2 * y + c
        lc = pltpu.make_async_copy(v_ref, out_ref.at[me], loc)
        lc.start()
        rcs = []
        for k, (fx, fy, fc) in enumerate(flips):
            tgt = (x + fx - 2 * x * fx, y + fy - 2 * y * fy, c + fc - 2 * c * fc)
            rc = pltpu.make_async_remote_copy(src_ref=v_ref, dst_ref=out_ref.at[me], send_sem=send.at[k],
                                              recv_sem=recv.at[k], device_id=tgt, device_id_type=MESH)
            rc.start()
            rcs.append(rc)
        lc.wait()
        for rc in rcs:
            rc.wait()

    return pl.pallas_call(
        body, name=name, in_specs=[ANY], out_specs=ANY,
        out_shape=jax.ShapeDtypeStruct((8,) + v.shape, v.dtype),
        scratch_shapes=[pltpu.SemaphoreType.DMA((7,)), pltpu.SemaphoreType.DMA((7,)), pltpu.SemaphoreType.DMA],
    )(v)


def _pick_rows(rows, cols, target_elems=128 * 1024):
    if rows % SUBLANES != 0:
        return rows
    best = SUBLANES
    t = SUBLANES
    while t <= rows:
        if rows % t == 0 and t * cols <= target_elems:
            best = t
        t += SUBLANES
    return best


def sum_chips(name, parts):
    _, R, C = parts.shape
    tm = _pick_rows(R, C)

    def body(p_ref, o_ref):
        o_ref[...] = (p_ref[0] + p_ref[1]) + (p_ref[2] + p_ref[3])

    return pl.pallas_call(
        body, name=name, grid=(R // tm,),
        in_specs=[pl.BlockSpec((4, tm, C), lambda i: (0, i, 0))],
        out_specs=pl.BlockSpec((tm, C), lambda i: (i, 0)),
        out_shape=jax.ShapeDtypeStruct((R, C), F32),
        compiler_params=_params(dimension_semantics=("arbitrary",)),
    )(parts)


def _adamw(g, w, m, v):
    m = ADAM_B1 * m + (1.0 - ADAM_B1) * g
    v = ADAM_B2 * v + (1.0 - ADAM_B2) * (g * g)
    m_hat = m / (1.0 - ADAM_B1 ** ADAM_STEP)
    v_hat = v / (1.0 - ADAM_B2 ** ADAM_STEP)
    delta = -ADAM_LR * (m_hat / (jnp.sqrt(v_hat) + ADAM_EPS) + ADAM_WD * w)
    return delta, m, v


def adamw_shard(name, s_mine, s_sib, w, m, v):
    R, C = w.shape
    tm = _pick_rows(R, C)

    def body(a_ref, b_ref, w_ref, m_ref, v_ref, g_out, d_out, m_out, v_out):
        g = a_ref[...] + b_ref[...]
        d, mn, vn = _adamw(g, w_ref[...], m_ref[...], v_ref[...])
        g_out[...] = g
        d_out[...] = d
        m_out[...] = mn
        v_out[...] = vn

    spec = pl.BlockSpec((tm, C), lambda i: (i, 0))
    return pl.pallas_call(
        body, name=name, grid=(R // tm,), in_specs=[spec] * 5, out_specs=[spec] * 4,
        out_shape=[jax.ShapeDtypeStruct((R, C), F32)] * 4,
        compiler_params=_params(dimension_semantics=("arbitrary",)),
    )(s_mine, s_sib, w, m, v)


def core_sum(name, core, g, got):
    _, r, C = g.shape
    hr = r // 2
    tm = _pick_rows(hr, 4 * C)
    nh = hr // tm

    def body(c_ref, g_ref, s_ref, o_ref):
        o_ref[...] = g_ref[...] + s_ref[...]

    return pl.pallas_call(
        body, name=name,
        grid_spec=pltpu.PrefetchScalarGridSpec(
            num_scalar_prefetch=1, grid=(nh,),
            in_specs=[pl.BlockSpec((4, tm, C), lambda i, cr: (0, cr[0] * nh + i, 0)),
                      pl.BlockSpec((4, tm, C), lambda i, cr: (0, i, 0))],
            out_specs=pl.BlockSpec((4, tm, C), lambda i, cr: (0, i, 0))),
        out_shape=jax.ShapeDtypeStruct((4, hr, C), F32),
        compiler_params=_params(dimension_semantics=("arbitrary",)),
    )(core, g, got)


def chip_sum_into(name, core, parts, layer, n_layers, into=None):
    _, hr, C = parts.shape
    r = 2 * hr
    tm = _pick_rows(hr, 4 * C)
    nh = hr // tm

    def body(c_ref, p_ref, *rest):
        o_ref = rest[-1]
        o_ref[...] = (p_ref[0] + p_ref[1]) + (p_ref[2] + p_ref[3])

    in_specs = [pl.BlockSpec((4, tm, C), lambda i, cr: (0, i, 0))]
    args = [core, parts]
    aliases = {}
    if into is not None:
        in_specs.append(pl.BlockSpec(memory_space=pl.ANY))
        args.append(into)
        aliases = {2: 0}
    return pl.pallas_call(
        body, name=name,
        grid_spec=pltpu.PrefetchScalarGridSpec(
            num_scalar_prefetch=1, grid=(nh,), in_specs=in_specs,
            out_specs=pl.BlockSpec((tm, C), lambda i, cr: ((layer * r) // tm + cr[0] * nh + i, 0))),
        out_shape=jax.ShapeDtypeStruct((n_layers * r, C), F32), input_output_aliases=aliases,
        compiler_params=_params(dimension_semantics=("arbitrary",)),
    )(*args)


def adamw_full(name, g, w, m, v):
    R, C = w.shape
    tm = _pick_rows(R, C)

    def body(g_ref, w_ref, m_ref, v_ref, d_out, m_out, v_out):
        d, mn, vn = _adamw(g_ref[...], w_ref[...], m_ref[...], v_ref[...])
        d_out[...] = d
        m_out[...] = mn
        v_out[...] = vn

    spec = pl.BlockSpec((tm, C), lambda i: (i, 0))
    return pl.pallas_call(
        body, name=name, grid=(R // tm,), in_specs=[spec] * 4, out_specs=[spec] * 3,
        out_shape=[jax.ShapeDtypeStruct((R, C), F32)] * 3,
        compiler_params=_params(dimension_semantics=("arbitrary",)),
    )(g, w, m, v)


def adamw_small(name, parts, w, m, v):
    W = w.shape[1]

    def body(p_ref, w_ref, m_ref, v_ref, g_out, d_out, m_out, v_out):
        acc = p_ref[0]
        for k in range(1, 8):
            acc = acc + p_ref[k]
        g = jnp.sum(acc, axis=0, keepdims=True)
        d, mn, vn = _adamw(g, w_ref[...], m_ref[...], v_ref[...])
        g_out[...] = g
        d_out[...] = d
        m_out[...] = mn
        v_out[...] = vn

    return pl.pallas_call(
        body, name=name, out_shape=[jax.ShapeDtypeStruct((1, W), F32)] * 4,
        compiler_params=_params(),
    )(parts, w, m, v)


def _pad_lanes(v, width=LANES):
    return jnp.pad(v, ((0, 0), (0, width - v.shape[1])))


def _layer_fwd(cf, x, xb, pb, W, sm):
    S, D, CD, DI, XBC, F, H, G = cf['S'], cf['D'], cf['CD'], cf['DI'], cf['XBC'], cf['F'], cf['H'], cf['G']
    NM = cf['NM']
    alpha = cf['alpha']
    tm = cf['tm']
    tn_in = cf['tn_in']
    sv = {}

    ident = lambda acc, ex: ([acc], [])
    proj, = fused_mm("in_proj", [(xb, W['in_main'], 0, False)], [], ident, [(NM, BF16, tn_in, 0)],
                     M=S, tm=tm, tn=tn_in, nj=NM // tn_in)
    dtraw, = fused_mm("dt_proj", [(xb, W['in_dt'], 0, False)], [], ident, [(LANES, F32, LANES, 0)],
                      M=S, tm=tm, tn=LANES)

    u, = row_call("glu", lambda a, gt: ([a * _sig(gt)], []),
                  [(proj, 'row', CD, 0, 0), (proj, 'row', CD, 1, 0)], [(CD, F32, CD, 0, 0)], M=S, tm=tm)

    def conv_a_epi(conv, ex):
        cb_, g_, b_ = ex
        ca = conv + cb_
        xhat, _ = _ln_stats(ca)
        la = xhat * g_ + b_
        return [ca, la * _sig(la)], []

    ca, sa = conv_call("conv_a", u, 0, sm['conv_a_w'], cf['KA'], conv_a_epi,
                       [(sm['conv_a_b'], 'vec', CD, 0, 0), (sm['ln_a_g'], 'vec', CD, 0, 0), (sm['ln_a_b'], 'vec', CD, 0, 0)],
                       [(CD, F32, CD, 0, 0), (CD, BF16, CD, 0, 0)], M=S, tm=cf['tmc'], cw=CD, nc=1, reverse=False)
    y_a, = fused_mm("a_out", [(sa, W['a_out'], 0, False)], [], ident, [(D, F32, D, 0)], M=S, tm=tm, tn=D)

    def conv_b_epi(conv, ex):
        cb = conv + ex[0]
        return [cb, cb * _sig(cb)], []

    xoff = (2 * CD + 2 * D + DI) // DI
    cbv, xbc = conv_call("conv_b", proj, xoff, sm['ssm_conv_w'], cf['KB'], conv_b_epi,
                         [(sm['ssm_conv_b'], 'vec', DI, 0, 1)],
                         [(XBC, F32, DI, 0, 1), (XBC, F32, DI, 0, 1)], M=S, tm=cf['tmc'], cw=DI, nc=XBC // DI,
                         reverse=False)
    y_f, st_f = ssd_fwd("ssd_fwd_f", xbc, dtraw, sm['dtb_f'], sm['alog_f'], S=S, DI=DI, G=G, H=H, rev=False)
    y_r, st_r = ssd_fwd("ssd_fwd_r", xbc, dtraw, sm['dtb_r'], sm['alog_r'], S=S, DI=DI, G=G, H=H, rev=True)

    def gate_norm(yf, yr, xs, z, dsk, ng):
        y = yf + yr + xs * dsk
        yz = y * (z * _sig(z))
        outs = []
        gw = cf['GW']
        for g in range(G):
            t = yz[:, g * gw:(g + 1) * gw]
            outs.append(t * lax.rsqrt(jnp.mean(t * t, axis=-1, keepdims=True) + RMS_EPS))
        return [jnp.concatenate(outs, axis=1) * ng], []

    zoff = (2 * CD + 2 * D) // DI
    yn, = row_call("gate_norm", gate_norm,
                   [(y_f, 'row', DI, 0, 0), (y_r, 'row', DI, 0, 0), (xbc, 'row', DI, 0, 0), (proj, 'row', DI, zoff, 0),
                    (sm['dskip_full'], 'vec', DI, 0, 0), (sm['ssm_norm_g'], 'vec', DI, 0, 0)],
                   [(DI, BF16, DI, 0, 0)], M=S, tm=cf['tmr'])
    y_b, = fused_mm("b_out", [(yn, W['b_out'], 0, False)], [], ident, [(D, F32, D, 0)], M=S, tm=tm, tn=D)

    goff = (2 * CD) // D
    merged, = row_call("merge", lambda ga, gb, ya, yb: ([_sig(ga) * ya + _sig(gb) * yb], []),
                       [(proj, 'row', D, goff, 0), (proj, 'row', D, goff + 1, 0), (y_a, 'row', D, 0, 0), (y_b, 'row', D, 0, 0)],
                       [(D, BF16, D, 0, 0)], M=S, tm=tm)

    def mix_epi(acc, ex):
        xin, g_, b_ = ex
        r1 = alpha * xin + acc
        xhat, _ = _ln_stats(r1)
        return [r1, xhat * g_ + b_], []

    r1, hb = fused_mm("o_mix", [(merged, W['o'], 0, False)],
                      [(x, 'row', D, 0), (sm['ln1_g'], 'vec', D, 0), (sm['ln1_b'], 'vec', D, 0)],
                      mix_epi, [(D, F32, D, 0), (D, BF16, D, 0)], M=S, tm=tm, tn=D)

    tnf = cf['tnf']

    g_ = fused_mm("ffn_gate", [(hb, W['gate_up'], 0, False)], [], ident, [(F, F32, tnf, 0)],
                  M=S, tm=tm, tn=tnf, nj=F // tnf)[0]
    u_ = fused_mm("ffn_up", [(hb, W['gate_up'], F // tnf, False)], [], ident, [(F, F32, tnf, 0)],
                  M=S, tm=tm, tn=tnf, nj=F // tnf)[0]
    f, = row_call("swiglu", lambda a, b: ([a * _sig(a) * b], []),
                  [(g_, 'row', tnf, 0, 1), (u_, 'row', tnf, 0, 1)], [(F, BF16, tnf, 0, 1)], M=S, tm=tm, nc=F // tnf)

    def down_epi(acc, ex):
        r1_, g1, b1, g2, b2 = ex
        xh1, _ = _ln_stats(r1_)
        r2 = alpha * (xh1 * g1 + b1) + acc
        xh2, _ = _ln_stats(r2)
        return [r2, xh2 * g2 + b2], []

    r2, h2b = fused_mm("ffn_down", [(f, W['down'], 0, False)],
                       [(r1, 'row', D, 0), (sm['ln1_g'], 'vec', D, 0), (sm['ln1_b'], 'vec', D, 0),
                        (sm['ln2_g'], 'vec', D, 0), (sm['ln2_b'], 'vec', D, 0)],
                       down_epi, [(D, F32, D, 0), (D, BF16, D, 0)], M=S, tm=tm, tn=D)

    t_, = fused_mm("ple_gate", [(h2b, W['ple_gate'], 0, False)], [], ident, [(D, F32, D, 0)], M=S, tm=tm, tn=D)
    pe, = fused_mm("ple_proj", [(pb, W['ple'], 0, False)], [], ident, [(D, F32, D, 0)], M=S, tm=tm, tn=D)

    def ple_mix(r2_, g2, b2, t, pe_, pg):
        xh2, _ = _ln_stats(r2_)
        h2 = xh2 * g2 + b2
        e = pe_ * lax.rsqrt(jnp.mean(pe_ * pe_, axis=-1, keepdims=True) + RMS_EPS) * pg
        xn = h2 + e * _sig(t)
        return [xn, xn], []

    xn, xnb = row_call("ple_mix", ple_mix,
                       [(r2, 'row', D, 0, 0), (sm['ln2_g'], 'vec', D, 0, 0), (sm['ln2_b'], 'vec', D, 0, 0),
                        (t_, 'row', D, 0, 0), (pe, 'row', D, 0, 0), (sm['ple_norm_g'], 'vec', D, 0, 0)],
                       [(D, F32, D, 0, 0), (D, BF16, D, 0, 0)], M=S, tm=tm)
    sv.update(x=x, xb=xb, pb=pb, proj=proj, dtraw=dtraw, u=u, ca=ca, sa=sa, y_a=y_a, cbv=cbv, xbc=xbc,
              y_f=y_f, y_r=y_r, st_f=st_f, st_r=st_r, yn=yn, y_b=y_b, merged=merged, r1=r1, hb=hb,
              g_=g_, u_=u_, f=f, r2=r2, h2b=h2b, t_=t_, pe=pe)
    return xn, xnb, sv


def _layer_bwd(cf, sv, W, sm, dxn=None, target=None, xn=None):
    S, D, CD, DI, XBC, F, H, G = cf['S'], cf['D'], cf['CD'], cf['DI'], cf['XBC'], cf['F'], cf['H'], cf['G']
    NM = cf['NM']
    alpha = cf['alpha']
    tm = cf['tm']
    gw = cf['GW']
    out = {}

    def ple_bwd_core(dx_, t, pe_, pg):
        s = _sig(t)
        rinv = lax.rsqrt(jnp.mean(pe_ * pe_, axis=-1, keepdims=True) + RMS_EPS)
        pn = pe_ * rinv
        e = pn * pg
        dtg = dx_ * e * (s * (1.0 - s))
        de = dx_ * s
        qv = de * pg
        dpe = rinv * (qv - pn * jnp.mean(qv * pn, axis=-1, keepdims=True))
        return dtg, dpe, de * pn

    if dxn is None:
        def head(xn_, tgt, t, pe_, pg):
            err = xn_ - tgt
            dx_ = err * (1.0 / D)
            dtg, dpe, dpg = ple_bwd_core(dx_, t, pe_, pg)
            return [dx_, dtg, dpe], [dpg, err * err]

        (dxn, dtg, dpe, dpg, lsq) = row_call(
            "loss_ple_bwd", head,
            [(xn, 'row', D, 0, 0), (target, 'row', D, 0, 0), (sv['t_'], 'row', D, 0, 0), (sv['pe'], 'row', D, 0, 0),
             (sm['ple_norm_g'], 'vec', D, 0, 0)],
            [(D, F32, D, 0, 0), (D, BF16, D, 0, 0), (D, BF16, D, 0, 0)], [(D, D, 0, 0), (D, D, 0, 0)], M=S, tm=tm)
        out['loss_sq'] = lsq
    else:
        def mid(dx_, t, pe_, pg):
            dtg, dpe, dpg = ple_bwd_core(dx_, t, pe_, pg)
            return [dtg, dpe], [dpg]

        (dtg, dpe, dpg) = row_call(
            "ple_bwd", mid,
            [(dxn, 'row', D, 0, 0), (sv['t_'], 'row', D, 0, 0), (sv['pe'], 'row', D, 0, 0),
             (sm['ple_norm_g'], 'vec', D, 0, 0)],
            [(D, BF16, D, 0, 0), (D, BF16, D, 0, 0)], [(D, D, 0, 0)], M=S, tm=tm)
    out['ple_norm_g'] = dpg

    def ln_bwd_epi(scale):
        def epi(acc, ex):
            res, r_, g_ = ex
            dh = scale * res + acc
            xhat, rstd = _ln_stats(r_)
            dr = _ln_bwd(dh, xhat, rstd, g_)
            return [dr, dr], [dh * xhat, dh]
        return epi

    dr2, dr2b, dg2, db2 = fused_mm(
        "dh2", [(dtg, W['ple_gate_T'], 0, False)],
        [(dxn, 'row', D, 0), (sv['r2'], 'row', D, 0), (sm['ln2_g'], 'vec', D, 0)],
        ln_bwd_epi(1.0), [(D, F32, D, 0), (D, BF16, D, 0)], [(D, D, 0), (D, D, 0)], M=S, tm=tm, tn=D)
    out['ln2_g'], out['ln2_b'] = dg2, db2

    tnf = cf['tnf']

    def dswiglu_epi(acc, ex):
        gg, uu = ex
        s = _sig(gg)
        return [acc * uu * _dsilu(gg, s), acc * (gg * s)], []

    dg_b, du_b = fused_mm(
        "d_down", [(dr2b, W['down_T'], 0, False)],
        [(sv['g_'], 'row', tnf, 0), (sv['u_'], 'row', tnf, 0)], dswiglu_epi,
        [(F, BF16, tnf, 0), (F, BF16, tnf, 0)], M=S, tm=tm, tn=tnf, nj=F // tnf)

    dr1, dr1b, dg1, db1 = fused_mm(
        "dh1", [(dg_b, W['gate_T'], 0, True), (du_b, W['up_T'], 0, True)],
        [(dr2, 'row', D, 0), (sv['r1'], 'row', D, 0), (sm['ln1_g'], 'vec', D, 0)],
        ln_bwd_epi(alpha), [(D, F32, D, 0), (D, BF16, D, 0)], [(D, D, 0), (D, D, 0)],
        M=S, tm=tm, tn=D, nk=cf['nk_f'])
    out['ln1_g'], out['ln1_b'] = dg1, db1

    goff = (2 * CD) // D

    def dmerge_epi(acc, ex):
        ga, gb, ya, yb = ex
        sa_, sb_ = _sig(ga), _sig(gb)
        dga = acc * ya * (sa_ * (1.0 - sa_))
        dgb = acc * yb * (sb_ * (1.0 - sb_))
        return [jnp.concatenate([dga, dgb], axis=1), acc * sa_, acc * sb_], []

    dproj, dya_b, dyb_b = fused_mm(
        "d_merge", [(dr1b, W['o_T'], 0, False)],
        [(sv['proj'], 'row', D, goff), (sv['proj'], 'row', D, goff + 1), (sv['y_a'], 'row', D, 0), (sv['y_b'], 'row', D, 0)],
        dmerge_epi, [(NM, BF16, 2 * D, (2 * CD) // (2 * D)), (D, BF16, D, 0), (D, BF16, D, 0)], M=S, tm=tm, tn=D)

    def dsa_epi(acc, ex):
        ca_, g_, b_ = ex
        xhat, rstd = _ln_stats(ca_)
        la = xhat * g_ + b_
        dla = acc * _dsilu(la, _sig(la))
        dca = _ln_bwd(dla, xhat, rstd, g_)
        return [dca], [dla * xhat, dla, dca]

    dca, dlag, dlab, dcab = fused_mm(
        "d_a_out", [(dya_b, W['a_out_T'], 0, False)],
        [(sv['ca'], 'row', CD, 0), (sm['ln_a_g'], 'vec', CD, 0), (sm['ln_a_b'], 'vec', CD, 0)],
        dsa_epi, [(CD, F32, CD, 0)], [(CD, CD, 0), (CD, CD, 0), (CD, CD, 0)], M=S, tm=tm, tn=D)
    out['ln_a_g'], out['ln_a_b'], out['conv_a_b'] = dlag, dlab, dcab

    def dglu_epi(du, ex):
        a, gt = ex
        s = _sig(gt)
        return [jnp.concatenate([du * s, du * a * (s * (1.0 - s))], axis=1)], []

    dproj, dwa = conv_call(
        "d_conv_a", dca, 0, sm['conv_a_w'], cf['KA'], dglu_epi,
        [(sv['proj'], 'row', CD, 0, 0), (sv['proj'], 'row', CD, 1, 0)],
        [(NM, BF16, 2 * CD, 0, 0)], M=S, tm=cf['tmc'], cw=CD, nc=1, reverse=True, xin=(sv['u'], 0),
        passthrough=(dproj, 0))
    out['conv_a_w'] = dwa

    zoff = (2 * CD + 2 * D) // DI

    def dgate_norm_epi(acc, ex):
        yf, yr, xs, z, dsk, ng = ex
        y = yf + yr + xs * dsk
        sz = _sig(z)
        siluz = z * sz
        yz = y * siluz
        dyzs, yhats = [], []
        for g in range(G):
            t = yz[:, g * gw:(g + 1) * gw]
            rinv = lax.rsqrt(jnp.mean(t * t, axis=-1, keepdims=True) + RMS_EPS)
            yh = t * rinv
            qv = acc[:, g * gw:(g + 1) * gw] * ng[:, g * gw:(g + 1) * gw]
            dyzs.append(rinv * (qv - yh * jnp.mean(qv * yh, axis=-1, keepdims=True)))
            yhats.append(yh)
        dyz = jnp.concatenate(dyzs, axis=1)
        yhat = jnp.concatenate(yhats, axis=1)
        dy = dyz * siluz
        dz = dyz * y * _dsilu(z, sz)
        return [dy, dz], [acc * yhat, dy * xs]

    tmr = cf['tmr']
    dy_ssd, dproj, dng, ddsk = fused_mm(
        "d_b_out", [(dyb_b, W['b_out_T'], 0, False)],
        [(sv['y_f'], 'row', DI, 0), (sv['y_r'], 'row', DI, 0), (sv['xbc'], 'row', DI, 0), (sv['proj'], 'row', DI, zoff),
         (sm['dskip_full'], 'vec', DI, 0), (sm['ssm_norm_g'], 'vec', DI, 0)],
        dgate_norm_epi, [(DI, F32, DI, 0), (NM, BF16, DI, zoff)], [(DI, DI, 0), (DI, DI, 0)],
        M=S, tm=tmr, tn=DI, passthrough=(dproj, 1))
    out['ssm_norm_g'], out['dskip_full'] = dng, ddsk

    dxbc_f, ddt_f, dA_f = ssd_bwd("ssd_bwd_f", sv['xbc'], sv['dtraw'], dy_ssd, sv['st_f'], sm['dtb_f'], sm['alog_f'],
                                  S=S, DI=DI, G=G, H=H, rev=False)
    dcb, ddt_r, dA_r, dcbb = ssd_bwd("ssd_bwd_r", sv['xbc'], sv['dtraw'], dy_ssd, sv['st_r'], sm['dtb_r'], sm['alog_r'],
                                     S=S, DI=DI, G=G, H=H, rev=True, tail=(dxbc_f, sv['cbv'], sm['dskip_full']))
    out['dA_f'], out['dA_r'] = dA_f, dA_r
    out['ssm_conv_b'] = dcbb

    xoff = (2 * CD + 2 * D + DI) // DI
    dproj, dwb = conv_call(
        "d_conv_b", dcb, 0, sm['ssm_conv_w'], cf['KB'], lambda conv, ex: ([conv], []), [],
        [(NM, BF16, DI, xoff, 1)], M=S, tm=cf['tmc'], cw=DI, nc=XBC // DI, reverse=True, xin=(sv['proj'], xoff),
        passthrough=(dproj, 0))
    out['ssm_conv_w'] = dwb

    ddtb, ddt_bias = row_call("d_dt", lambda a, b: ([a + b], [a + b]),
                              [(ddt_f, 'row', LANES, 0, 0), (ddt_r, 'row', LANES, 0, 0)],
                              [(LANES, BF16, LANES, 0, 0)], [(LANES, LANES, 0, 0)], M=S, tm=tm)
    out['dt_bias'] = ddt_bias

    dx, = fused_mm("d_x", [(dproj, W['in_main_T'], 0, True), (ddtb, W['in_dt_T'], 0, False)],
                   [(dr1, 'row', D, 0)], lambda acc, ex: ([alpha * ex[0] + acc], []),
                   [(D, F32, D, 0)], M=S, tm=tm, tn=D, nk=cf['nk_in'])

    tmw = cf['tmw']
    xb = sv['xb']
    out['w_in'] = jnp.concatenate(
        [mm_tn("dw_in", xb, dproj, tm=tmw, tk=D, tn=cf['tn_in']),
         mm_tn("dw_dt", xb, ddtb, tm=tmw, tk=D, tn=LANES)[:, :2 * H]], axis=1)
    out['w_a_out'] = mm_tn("dw_a_out", sv['sa'], dya_b, tm=tmw, tk=CD, tn=D)
    out['w_b_out'] = mm_tn("dw_b_out", sv['yn'], dyb_b, tm=tmw, tk=DI // 2, tn=D)
    out['w_o'] = mm_tn("dw_o", sv['merged'], dr1b, tm=tmw, tk=D, tn=D)
    out['w_gate_up'] = jnp.concatenate(
        [mm_tn("dw_gate", sv['hb'], dg_b, tm=tmw, tk=D, tn=tnf),
         mm_tn("dw_up", sv['hb'], du_b, tm=tmw, tk=D, tn=tnf)], axis=1)
    out['w_down'] = mm_tn("dw_down", sv['f'], dr2b, tm=tmw, tk=tnf, tn=D)
    out['w_ple'] = mm_tn("dw_ple", sv['pb'], dpe, tm=tmw, tk=sv['pb'].shape[1], tn=D)
    out['w_ple_gate'] = mm_tn("dw_ple_gate", sv['h2b'], dtg, tm=tmw, tk=D, tn=D)
    return dx, out


_WEIGHTS = ['w_in', 'conv_a_w', 'conv_a_b', 'ln_a_g', 'ln_a_b', 'w_a_out', 'ssm_conv_w', 'ssm_conv_b', 'a_log',
            'dt_bias', 'd_skip', 'ssm_norm_g', 'w_b_out', 'w_o', 'ln1_g', 'ln1_b', 'w_gate_up', 'w_down', 'ln2_g',
            'ln2_b', 'w_ple', 'ple_norm_g', 'w_ple_gate']
_COL_SHARDED = ['w_in', 'conv_a_w', 'ssm_conv_w', 'w_gate_up', 'w_ple']
_ROW_SHARDED = ['w_a_out', 'w_b_out', 'w_o', 'w_down', 'w_ple_gate']
_BIG = _COL_SHARDED + _ROW_SHARDED
_SMALL = [n for n in _WEIGHTS if n not in _BIG]
_CONV = ['conv_a_w', 'ssm_conv_w']


def _ceil_to(n, k):
    return -(-n // k) * k


def kernel(x, p, w_in, conv_a_w, conv_a_b, ln_a_g, ln_a_b, w_a_out, ssm_conv_w, ssm_conv_b, a_log, dt_bias, d_skip, ssm_norm_g, w_b_out, w_o, ln1_g, ln1_b, w_gate_up, w_down, ln2_g, ln2_b, w_ple, ple_norm_g, w_ple_gate, loss_target, m_w_in, m_conv_a_w, m_conv_a_b, m_ln_a_g, m_ln_a_b, m_w_a_out, m_ssm_conv_w, m_ssm_conv_b, m_a_log, m_dt_bias, m_d_skip, m_ssm_norm_g, m_w_b_out, m_w_o, m_ln1_g, m_ln1_b, m_w_gate_up, m_w_down, m_ln2_g, m_ln2_b, m_w_ple, m_ple_norm_g, m_w_ple_gate, v_w_in, v_conv_a_w, v_conv_a_b, v_ln_a_g, v_ln_a_b, v_w_a_out, v_ssm_conv_w, v_ssm_conv_b, v_a_log, v_dt_bias, v_d_skip, v_ssm_norm_g, v_w_b_out, v_w_o, v_ln1_g, v_ln1_b, v_w_gate_up, v_w_down, v_ln2_g, v_ln2_b, v_w_ple, v_ple_norm_g, v_w_ple_gate):
    wt = dict(w_in=w_in, conv_a_w=conv_a_w, conv_a_b=conv_a_b, ln_a_g=ln_a_g, ln_a_b=ln_a_b, w_a_out=w_a_out,
              ssm_conv_w=ssm_conv_w, ssm_conv_b=ssm_conv_b, a_log=a_log, dt_bias=dt_bias, d_skip=d_skip,
              ssm_norm_g=ssm_norm_g, w_b_out=w_b_out, w_o=w_o, ln1_g=ln1_g, ln1_b=ln1_b, w_gate_up=w_gate_up,
              w_down=w_down, ln2_g=ln2_g, ln2_b=ln2_b, w_ple=w_ple, ple_norm_g=ple_norm_g, w_ple_gate=w_ple_gate)
    mo = dict(w_in=m_w_in, conv_a_w=m_conv_a_w, conv_a_b=m_conv_a_b, ln_a_g=m_ln_a_g, ln_a_b=m_ln_a_b,
              w_a_out=m_w_a_out, ssm_conv_w=m_ssm_conv_w, ssm_conv_b=m_ssm_conv_b, a_log=m_a_log,
              dt_bias=m_dt_bias, d_skip=m_d_skip, ssm_norm_g=m_ssm_norm_g, w_b_out=m_w_b_out, w_o=m_w_o,
              ln1_g=m_ln1_g, ln1_b=m_ln1_b, w_gate_up=m_w_gate_up, w_down=m_w_down, ln2_g=m_ln2_g, ln2_b=m_ln2_b,
              w_ple=m_w_ple, ple_norm_g=m_ple_norm_g, w_ple_gate=m_w_ple_gate)
    vo = dict(w_in=v_w_in, conv_a_w=v_conv_a_w, conv_a_b=v_conv_a_b, ln_a_g=v_ln_a_g, ln_a_b=v_ln_a_b,
              w_a_out=v_w_a_out, ssm_conv_w=v_ssm_conv_w, ssm_conv_b=v_ssm_conv_b, a_log=v_a_log,
              dt_bias=v_dt_bias, d_skip=v_d_skip, ssm_norm_g=v_ssm_norm_g, w_b_out=v_w_b_out, w_o=v_w_o,
              ln1_g=v_ln1_g, ln1_b=v_ln1_b, w_gate_up=v_w_gate_up, w_down=v_w_down, ln2_g=v_ln2_g, ln2_b=v_ln2_b,
              w_ple=v_w_ple, ple_norm_g=v_ple_norm_g, w_ple_gate=v_w_ple_gate)

    L = w_in.shape[0]
    S, D = x.shape[1], x.shape[2]
    CD = conv_a_b.shape[1]
    DI = ssm_norm_g.shape[1]
    XBC = ssm_conv_b.shape[1]
    H = d_skip.shape[1]
    G = (XBC - DI) // (2 * D_STATE)
    F = w_down.shape[1] * 4
    N_IN = w_in.shape[2] * 4
    NM = N_IN - 2 * H
    KA, KB = conv_a_w.shape[1], ssm_conv_w.shape[1]
    assert DI == H * HEAD_DIM and CD == D and DI == 2 * D and XBC == 2 * DI and NM == 2 * CD + 2 * D + DI + XBC
    assert 2 * H <= LANES and S % CHUNK == 0
    tnf = F // 2
    cf = dict(S=S, D=D, CD=CD, DI=DI, XBC=XBC, F=F, H=H, G=G, NM=NM, KA=KA, KB=KB, GW=(H // G) * HEAD_DIM,
              alpha=float((2 * L) ** 0.25), tm=min(512, S), tmc=min(256, S), tmr=min(256, S), tmw=min(1024, S),
              tn_in=D, tnf=tnf, nk_f=2, nk_in=NM // DI)

    core = lax.axis_index("c").astype(jnp.int32).reshape(1)
    split_names = [n for n in _BIG if n not in _CONV]

    def layer_weights(l):
        got = gather_layer("gather_weights", [wt[n][l].astype(BF16) for n in split_names], [wt[n][l] for n in _CONV])
        full = {}
        for n, g in zip(split_names + _CONV, got):
            if n in _COL_SHARDED:
                full[n] = g.transpose(1, 0, 2).reshape(g.shape[1], 4 * g.shape[2])
            else:
                full[n] = g.reshape(4 * g.shape[1], g.shape[2])
        win = full['w_in']
        in_main = win[:, :NM]
        in_dt = _pad_lanes(win[:, NM:])
        gu = full['w_gate_up']
        W = dict(in_main=in_main, in_dt=in_dt, in_main_T=in_main.T, in_dt_T=in_dt.T,
                 a_out=full['w_a_out'], a_out_T=full['w_a_out'].T,
                 b_out=full['w_b_out'], b_out_T=full['w_b_out'].T,
                 o=full['w_o'], o_T=full['w_o'].T, gate_up=gu, gate_T=gu[:, :F].T, up_T=gu[:, F:].T,
                 down=full['w_down'], down_T=full['w_down'].T, ple=full['w_ple'],
                 ple_gate=full['w_ple_gate'], ple_gate_T=full['w_ple_gate'].T)
        row = lambda v: v.reshape(1, -1)
        head_table = lambda v: jnp.broadcast_to(jnp.pad(v, (0, LANES - H))[:, None], (LANES, LANES))
        sm = dict(conv_a_w=jnp.pad(full['conv_a_w'], ((0, _ceil_to(KA, SUBLANES) - KA), (0, 0))),
                  ssm_conv_w=jnp.pad(full['ssm_conv_w'], ((0, _ceil_to(KB, SUBLANES) - KB), (0, 0))),
                  conv_a_b=row(conv_a_b[l]), ln_a_g=row(ln_a_g[l]), ln_a_b=row(ln_a_b[l]),
                  ssm_conv_b=row(ssm_conv_b[l]), ssm_norm_g=row(ssm_norm_g[l]),
                  ln1_g=row(ln1_g[l]), ln1_b=row(ln1_b[l]), ln2_g=row(ln2_g[l]), ln2_b=row(ln2_b[l]),
                  ple_norm_g=row(ple_norm_g[l]),
                  dtb_f=head_table(dt_bias[l, 0]), dtb_r=head_table(dt_bias[l, 1]),
                  alog_f=head_table(a_log[l, 0]), alog_r=head_table(a_log[l, 1]),
                  dskip_full=row(jnp.repeat(d_skip[l], HEAD_DIM)))
        return W, sm

    def blocks(n, gl):
        g = gl[n]
        if n == 'conv_a_w':
            g = g.sum(axis=1)[:KA]
        elif n == 'ssm_conv_w':
            g = g.sum(axis=1)[:KB]
        if n in _COL_SHARDED:
            return g.reshape(g.shape[0], 4, g.shape[1] // 4).transpose(1, 0, 2)
        return g.reshape(4, g.shape[0] // 4, g.shape[1])

    def reduce_layer(l, gl, acc):
        mine = [blocks(n, gl) for n in split_names]
        theirs = core_send_half("core_send_half", mine)
        both = [core_sum("core_sum_" + n, core, b, t) for n, b, t in zip(split_names, mine, theirs)]
        parts = chip_exchange("scatter_grads", [[t] for t in both], gather=False)
        sums = [chip_sum_into("chip_sum_" + n, core, pr.reshape(4, pr.shape[2], pr.shape[3]), l, L, into=acc.get(n))
                for n, pr in zip(split_names, parts)]
        return dict(zip(split_names, core_fill("core_fill", sums, l, L)))

    lw = [layer_weights(l) for l in range(L)]
    xl = x[0]
    xlb = xl.astype(BF16)
    saved = []
    for l in range(L):
        xl, xlb, sv = _layer_fwd(cf, xl, xlb, p[l, 0].astype(BF16), lw[l][0], lw[l][1])
        saved.append(sv)
    grads = [None] * L
    dxl = None
    gsum = {}
    for l in reversed(range(L)):
        if l == L - 1:
            dxl, grads[l] = _layer_bwd(cf, saved[l], lw[l][0], lw[l][1], target=loss_target[0], xn=xl)
        else:
            dxl, grads[l] = _layer_bwd(cf, saved[l], lw[l][0], lw[l][1], dxn=dxl)
        gsum = reduce_layer(l, grads[l], gsum)
    loss = lax.psum(0.5 / D * jnp.sum(grads[L - 1]['loss_sq']), ("x", "y", "c"))
    grad_x = dxl[None]

    res = {}
    for n in split_names:
        shp = wt[n].shape
        flat = lambda a: a.reshape(shp[0] * shp[1], shp[2])
        outs = adamw_full("adamw_" + n, gsum[n], flat(wt[n]), flat(mo[n]), flat(vo[n]))
        res[n] = [o.reshape(shp) for o in [gsum[n]] + list(outs)]
    parts = chip_exchange("scatter_conv", [[blocks(n, grads[l]) for l in range(L)] for n in _CONV], gather=False)
    chip_sums = [sum_chips("chip_sum_" + n, pr.reshape(4, L * pr.shape[2], pr.shape[3])) for n, pr in zip(_CONV, parts)]
    sib_sums = sibling_swap("core_swap", chip_sums)
    for n, mine, sib in zip(_CONV, chip_sums, sib_sums):
        shp = wt[n].shape
        flat = lambda a: a.reshape(shp[0] * shp[1], shp[2])
        outs = adamw_shard("adamw_" + n, mine, sib, flat(wt[n]), flat(mo[n]), flat(vo[n]))
        res[n] = [o.reshape(shp) for o in outs]

    def small_pieces(l):
        gl = grads[l]
        A = -jnp.exp(a_log[l])
        d = dict(gl)
        d_alog = jnp.concatenate([gl['dA_f'].sum(axis=1)[:H] * A[0], gl['dA_r'].sum(axis=1)[:H] * A[1]])
        d['a_log'] = jnp.pad(d_alog[None], ((0, SUBLANES - 1), (0, 0)))
        d['dt_bias'] = gl['dt_bias'][:, :2 * H]
        d['d_skip'] = gl['dskip_full'].reshape(SUBLANES, H, HEAD_DIM).sum(axis=-1)
        return [_pad_lanes(d[n], _ceil_to(d[n].shape[1], LANES)) for n in _SMALL]

    widths = [_ceil_to(math.prod(wt[n].shape[1:]), LANES) for n in _SMALL]
    packed = jnp.concatenate([pc for l in range(L) for pc in small_pieces(l)], axis=1)
    gathered = all8_gather("gather_small", packed)

    def pack_params(src):
        return jnp.concatenate([_pad_lanes(src[n][l].reshape(1, -1), wd) for l in range(L) for n, wd in zip(_SMALL, widths)],
                               axis=1)

    small_out = adamw_small("adamw_small", gathered, pack_params(wt), pack_params(mo), pack_params(vo))
    off = 0
    per = {n: [[] for _ in range(4)] for n in _SMALL}
    for l in range(L):
        for n, wd in zip(_SMALL, widths):
            size = math.prod(wt[n].shape[1:])
            for k in range(4):
                per[n][k].append(small_out[k][0, off:off + size].reshape(wt[n].shape[1:]))
            off += wd
    for n in _SMALL:
        res[n] = [jnp.stack(per[n][k]) for k in range(4)]

    return (loss, grad_x, *[res[n][0] for n in _WEIGHTS], *[res[n][1] for n in _WEIGHTS],
            *[res[n][2] for n in _WEIGHTS], *[res[n][3] for n in _WEIGHTS])
```

```python
import math

import jax
import jax.numpy as jnp
from jax import lax
from jax.experimental import pallas as pl
from jax.experimental.pallas import tpu as pltpu

F32 = jnp.float32
BF16 = jnp.bfloat16

VMEM_LIMIT_BYTES = 56 * 1024 * 1024
LANES = 128
SUBLANES = 8

CHUNK = 128
D_STATE = 128
HEAD_DIM = 64
LN_EPS = 1e-5
RMS_EPS = 1e-6
ADAM_LR = 0.001
ADAM_B1 = 0.9
ADAM_B2 = 0.999
ADAM_EPS = 1e-08
ADAM_WD = 0.01
ADAM_STEP = 10
HALO = 16
MESH = pl.DeviceIdType.MESH


def _params(**kw):
    return pltpu.CompilerParams(vmem_limit_bytes=VMEM_LIMIT_BYTES, **kw)


def _sig(x):
    return jax.nn.sigmoid(x)


def _dsilu(x, s):
    return s * (1.0 + x * (1.0 - s))


def _ln_stats(r):
    mu = jnp.mean(r, axis=-1, keepdims=True)
    xc = r - mu
    var = jnp.mean(xc * xc, axis=-1, keepdims=True)
    rstd = lax.rsqrt(var + LN_EPS)
    return xc * rstd, rstd


def _ln_bwd(dy, xhat, rstd, g):
    dxh = dy * g
    m1 = jnp.mean(dxh, axis=-1, keepdims=True)
    m2 = jnp.mean(dxh * xhat, axis=-1, keepdims=True)
    return rstd * (dxh - m1 - xhat * m2)


def _f32(v):
    return v if v.dtype == F32 else v.astype(F32)


def _rows8(v):
    tm, w = v.shape
    return v.reshape(tm // SUBLANES, SUBLANES, w).sum(axis=0)


def fused_mm(name, prods, extras, epi, row_outs, col_outs=(), *, M, tm, tn, nj=1, nk=1,
             passthrough=None):
    np_ = len(prods)
    ne = len(extras)
    nro = len(row_outs)
    nco = len(col_outs)
    use_acc = nk > 1

    def body(*refs):
        a_refs = [refs[2 * p] for p in range(np_)]
        w_refs = [refs[2 * p + 1] for p in range(np_)]
        pos = 2 * np_
        e_refs = refs[pos:pos + ne]
        pos += ne
        if passthrough is not None:
            pos += 1
        ro_refs = refs[pos:pos + nro]
        pos += nro
        co_refs = refs[pos:pos + nco]
        pos += nco
        acc_ref = refs[pos] if use_acc else None
        i = pl.program_id(1)
        k = pl.program_id(2)

        def prod(p):
            a = a_refs[p][...]
            if a.dtype != BF16:
                a = a.astype(BF16)
            return jnp.dot(a, w_refs[p][...], preferred_element_type=F32)

        def finish(acc):
            rows, cols = epi(acc, [_f32(r[...]) for r in e_refs])
            for v, o in zip(rows, ro_refs):
                o[...] = v.astype(o.dtype)
            for v, o in zip(cols, co_refs):
                v8 = _rows8(v)

                @pl.when(i == 0)
                def _():
                    o[...] = v8

                @pl.when(i > 0)
                def _():
                    o[...] += v8

        if not use_acc:
            acc = prod(0)
            for p in range(1, np_):
                acc = acc + prod(p)
            finish(acc)
        else:
            @pl.when(k == 0)
            def _():
                acc = None
                for p in range(np_):
                    acc = prod(p) if acc is None else acc + prod(p)
                acc_ref[...] = acc

            @pl.when(k > 0)
            def _():
                acc = None
                for p in range(np_):
                    if prods[p][3]:
                        acc = prod(p) if acc is None else acc + prod(p)
                acc_ref[...] += acc

            @pl.when(k == nk - 1)
            def _():
                finish(acc_ref[...])

    in_specs = []
    args = []
    for a, w, joff, ksplit in prods:
        K = a.shape[1]
        if ksplit:
            tk = K // nk
            in_specs.append(pl.BlockSpec((tm, tk), lambda j, i, k: (i, k)))
            in_specs.append(pl.BlockSpec((tk, tn), lambda j, i, k, joff=joff: (k, j + joff)))
        else:
            in_specs.append(pl.BlockSpec((tm, K), lambda j, i, k: (i, 0)))
            in_specs.append(pl.BlockSpec((K, tn), lambda j, i, k, joff=joff: (0, j + joff)))
        args += [a, w]
    for arr, kind, width, c0 in extras:
        if kind == 'row':
            in_specs.append(pl.BlockSpec((tm, width), lambda j, i, k, c0=c0: (i, c0 + j)))
        else:
            in_specs.append(pl.BlockSpec((arr.shape[0], width), lambda j, i, k, c0=c0: (0, c0 + j)))
        args.append(arr)
    aliases = {}
    if passthrough is not None:
        arr, oidx = passthrough
        in_specs.append(pl.BlockSpec(memory_space=pl.ANY))
        aliases = {len(args): oidx}
        args.append(arr)
    out_shape = []
    out_specs = []
    for n_total, dtype, width, c0 in row_outs:
        out_shape.append(jax.ShapeDtypeStruct((M, n_total), dtype))
        out_specs.append(pl.BlockSpec((tm, width), lambda j, i, k, c0=c0: (i, c0 + j)))
    for n_total, width, c0 in col_outs:
        out_shape.append(jax.ShapeDtypeStruct((SUBLANES, n_total), F32))
        out_specs.append(pl.BlockSpec((SUBLANES, width), lambda j, i, k, c0=c0: (0, c0 + j)))
    scratch = [pltpu.VMEM((tm, tn), F32)] if use_acc else []
    return pl.pallas_call(
        body, name=name, grid=(nj, M // tm, nk), in_specs=in_specs, out_specs=out_specs,
        out_shape=out_shape, scratch_shapes=scratch, input_output_aliases=aliases,
        compiler_params=_params(dimension_semantics=("arbitrary", "arbitrary", "arbitrary")),
    )(*args)


def mm_tn(name, a, b, *, tm, tk, tn):
    M, K = a.shape
    N = b.shape[1]

    def body(a_ref, b_ref, o_ref):
        m = pl.program_id(2)
        p = lax.dot_general(a_ref[...], b_ref[...], (((0,), (0,)), ((), ())),
                            preferred_element_type=F32)

        @pl.when(m == 0)
        def _():
            o_ref[...] = p

        @pl.when(m > 0)
        def _():
            o_ref[...] += p

    return pl.pallas_call(
        body, name=name, grid=(K // tk, N // tn, M // tm),
        in_specs=[pl.BlockSpec((tm, tk), lambda kk, j, m: (m, kk)),
                  pl.BlockSpec((tm, tn), lambda kk, j, m: (m, j))],
        out_specs=pl.BlockSpec((tk, tn), lambda kk, j, m: (kk, j)),
        out_shape=jax.ShapeDtypeStruct((K, N), F32),
        compiler_params=_params(dimension_semantics=("arbitrary", "arbitrary", "arbitrary")),
    )(a, b)


def row_call(name, fn, ins, row_outs, col_outs=(), *, M, tm, nc=1):
    ni = len(ins)
    nro = len(row_outs)

    def body(*refs):
        i = pl.program_id(1)
        vals = [_f32(r[...]) for r in refs[:ni]]
        rows, cols = fn(*vals)
        for v, o in zip(rows, refs[ni:ni + nro]):
            o[...] = v.astype(o.dtype)
        for v, o in zip(cols, refs[ni + nro:]):
            v8 = _rows8(v)

            @pl.when(i == 0)
            def _():
                o[...] = v8

            @pl.when(i > 0)
            def _():
                o[...] += v8

    in_specs = []
    for arr, kind, width, c0, cmul in ins:
        if kind == 'row':
            in_specs.append(pl.BlockSpec((tm, width), lambda cj, i, c0=c0, cmul=cmul: (i, c0 + cmul * cj)))
        else:
            in_specs.append(pl.BlockSpec((arr.shape[0], width), lambda cj, i, c0=c0, cmul=cmul: (0, c0 + cmul * cj)))
    out_shape = []
    out_specs = []
    for n_total, dtype, width, c0, cmul in row_outs:
        out_shape.append(jax.ShapeDtypeStruct((M, n_total), dtype))
        out_specs.append(pl.BlockSpec((tm, width), lambda cj, i, c0=c0, cmul=cmul: (i, c0 + cmul * cj)))
    for n_total, width, c0, cmul in col_outs:
        out_shape.append(jax.ShapeDtypeStruct((SUBLANES, n_total), F32))
        out_specs.append(pl.BlockSpec((SUBLANES, width), lambda cj, i, c0=c0, cmul=cmul: (0, c0 + cmul * cj)))
    return pl.pallas_call(
        body, name=name, grid=(nc, M // tm), in_specs=in_specs, out_specs=out_specs,
        out_shape=out_shape,
        compiler_params=_params(dimension_semantics=("arbitrary", "arbitrary")),
    )(*[a[0] for a in ins])


def conv_call(name, src, src_c0, w, K, epi, extras, row_outs, col_outs=(), *, M, tm, cw, nc,
              reverse, xin=None, passthrough=None):
    pad = (K - 1) // 2
    assert pad <= HALO - 1
    R = tm // HALO
    nblk = M // HALO
    n_i = M // tm
    Kp = w.shape[0]
    ne = len(extras)
    nro = len(row_outs)
    nco = len(col_outs)
    rb = 64
    cbw = min(cw, 256)
    n_copies = SUBLANES if K > SUBLANES else 1

    def body(*refs):
        main_ref, prev_ref, next_ref, w_ref = refs[:4]
        pos = 4
        xin_ref = None
        if xin is not None:
            xin_ref = refs[pos]
            pos += 1
        e_refs = refs[pos:pos + ne]
        pos += ne
        if passthrough is not None:
            pos += 1
        ro_refs = refs[pos:pos + nro]
        pos += nro
        co_refs = refs[pos:pos + nco]
        pos += nco
        dw_ref = None
        if xin is not None:
            dw_ref = refs[pos]
            pos += 1
        ext_ref, conv_ref = refs[pos], refs[pos + 1]
        i = pl.program_id(1)

        ext_ref[0, 0:HALO, :] = jnp.where(i == 0, 0.0, prev_ref[...].astype(F32))
        ext_ref[0, HALO:HALO + tm, :] = main_ref[...].astype(F32)
        ext_ref[0, HALO + tm:, :] = jnp.where(i == n_i - 1, 0.0, next_ref[...].astype(F32))
        if dw_ref is not None:
            @pl.when(i == 0)
            def _():
                dw_ref[...] = jnp.zeros_like(dw_ref)

        n_sh = tm + 2 * HALO - SUBLANES
        for c0 in range(0, cw, cbw):
            for sft in range(1, n_copies):
                ext_ref[sft, 0:n_sh, c0:c0 + cbw] = ext_ref[0, sft:sft + n_sh, c0:c0 + cbw]

        for c0 in range(0, cw, cbw):
            for r0 in range(0, tm, rb):
                acc = jnp.zeros((rb, cbw), F32)
                if xin_ref is not None:
                    xblk = xin_ref[r0:r0 + rb, c0:c0 + cbw].astype(F32)
                for k in range(K):
                    off = HALO + r0 + ((pad - k) if reverse else (k - pad))
                    sft = off % SUBLANES if n_copies > 1 else 0
                    d = ext_ref[sft, off - sft:off - sft + rb, c0:c0 + cbw]
                    acc = acc + d * w_ref[k:k + 1, c0:c0 + cbw]
                    if xin_ref is not None:
                        dw_ref[k, :, c0:c0 + cbw] += _rows8(xblk * d)
                conv_ref[r0:r0 + rb, c0:c0 + cbw] = acc

        rows, cols = epi(conv_ref[...], [_f32(r[...]) for r in e_refs])
        for v, o in zip(rows, ro_refs):
            o[...] = v.astype(o.dtype)
        for v, o in zip(cols, co_refs):
            v8 = _rows8(v)

            @pl.when(i == 0)
            def _():
                o[...] = v8

            @pl.when(i > 0)
            def _():
                o[...] += v8

    in_specs = [
        pl.BlockSpec((tm, cw), lambda cj, i: (i, src_c0 + cj)),
        pl.BlockSpec((HALO, cw), lambda cj, i: (jnp.maximum(i * R - 1, 0), src_c0 + cj)),
        pl.BlockSpec((HALO, cw), lambda cj, i: (jnp.minimum((i + 1) * R, nblk - 1), src_c0 + cj)),
        pl.BlockSpec((Kp, cw), lambda cj, i: (0, cj)),
    ]
    args = [src, src, src, w]
    if xin is not None:
        in_specs.append(pl.BlockSpec((tm, cw), lambda cj, i, c0=xin[1]: (i, c0 + cj)))
        args.append(xin[0])
    for arr, kind, width, c0, cmul in extras:
        if kind == 'row':
            in_specs.append(pl.BlockSpec((tm, width), lambda cj, i, c0=c0, cmul=cmul: (i, c0 + cmul * cj)))
        else:
            in_specs.append(pl.BlockSpec((arr.shape[0], width), lambda cj, i, c0=c0, cmul=cmul: (0, c0 + cmul * cj)))
        args.append(arr)
    aliases = {}
    if passthrough is not None:
        in_specs.append(pl.BlockSpec(memory_space=pl.ANY))
        aliases = {len(args): passthrough[1]}
        args.append(passthrough[0])
    out_shape = []
    out_specs = []
    for n_total, dtype, width, c0, cmul in row_outs:
        out_shape.append(jax.ShapeDtypeStruct((M, n_total), dtype))
        out_specs.append(pl.BlockSpec((tm, width), lambda cj, i, c0=c0, cmul=cmul: (i, c0 + cmul * cj)))
    for n_total, width, c0, cmul in col_outs:
        out_shape.append(jax.ShapeDtypeStruct((SUBLANES, n_total), F32))
        out_specs.append(pl.BlockSpec((SUBLANES, width), lambda cj, i, c0=c0, cmul=cmul: (0, c0 + cmul * cj)))
    if xin is not None:
        out_shape.append(jax.ShapeDtypeStruct((Kp, SUBLANES, cw * nc), F32))
        out_specs.append(pl.BlockSpec((Kp, SUBLANES, cw), lambda cj, i: (0, 0, cj)))
    return pl.pallas_call(
        body, name=name, grid=(nc, n_i), in_specs=in_specs, out_specs=out_specs,
        out_shape=out_shape, input_output_aliases=aliases,
        scratch_shapes=[pltpu.VMEM((n_copies, tm + 2 * HALO, cw), F32), pltpu.VMEM((tm, cw), F32)],
        compiler_params=_params(dimension_semantics=("arbitrary", "arbitrary")),
    )(*args)


def _split_dot(m_bf16, v, n_pass, dims=None):
    out = None
    rest = v
    for p in range(n_pass):
        piece = rest.astype(BF16)
        if p + 1 < n_pass:
            rest = rest - piece.astype(F32)
        if dims is None:
            t = jnp.dot(m_bf16, piece, preferred_element_type=F32)
        else:
            t = lax.dot_general(m_bf16, piece, dims, preferred_element_type=F32)
        out = t if out is None else out + t
    return out


def _split_dot_r(v, m_bf16, n_pass):
    out = None
    rest = v
    for p in range(n_pass):
        piece = rest.astype(BF16)
        if p + 1 < n_pass:
            rest = rest - piece.astype(F32)
        t = jnp.dot(piece, m_bf16, preferred_element_type=F32)
        out = t if out is None else out + t
    return out


def _softplus(x):
    return jnp.maximum(x, 0.0) + jnp.log1p(jnp.exp(-jnp.abs(x)))


NT_DIMS = (((1,), (1,)), ((), ()))
TN_DIMS = (((0,), (0,)), ((), ()))


def _ssd_common(dtraw, dtbT, alogT, rev, n_heads):
    L = CHUNK
    if rev:
        dtraw = pltpu.roll(dtraw, LANES - n_heads, 1)
    preT = dtraw.T + dtbT
    dtT = _softplus(preT)
    AT = -jnp.exp(alogT)
    aT = dtT * AT
    ri = lax.broadcasted_iota(jnp.int32, (L, L), 0)
    ci = lax.broadcasted_iota(jnp.int32, (L, L), 1)
    up = (ri >= ci) if rev else (ri <= ci)
    lo = (ri <= ci) if rev else (ri >= ci)
    csT = _split_dot_r(aT, up.astype(BF16), 3)
    last = 0 if rev else L - 1
    lastB = jnp.broadcast_to(csT[:, last:last + 1], (L, L))
    return dict(preT=preT, dtT=dtT, AT=AT, csT=csT, cs=csT.T, up=up, lo=lo, ci=ci, last=last,
                doutT=jnp.exp(csT), dstT=jnp.exp(lastB - csT), totB=jnp.exp(lastB))


def ssd_fwd(name, xbc, dtraw, dtbT, alogT, *, S, DI, G, H, rev):
    NC = S // CHUNK
    R = H // G
    GW = R * HEAD_DIM
    N = D_STATE
    XBC = xbc.shape[1]
    P = HEAD_DIM

    def body(xbc_ref, dtraw_ref, dtb_ref, alog_ref, y_ref, st_ref, h_ref):
        c = pl.program_id(0)

        @pl.when(c == 0)
        def _():
            h_ref[...] = jnp.zeros_like(h_ref)

        q = _ssd_common(dtraw_ref[...], dtb_ref[...], alog_ref[...], rev, H)
        cs, csT, dtT, doutT, totB = q['cs'], q['csT'], q['dtT'], q['doutT'], q['totB']
        wstT = q['dstT'] * dtT
        for g in range(G):
            Bg = xbc_ref[:, DI + g * N:DI + (g + 1) * N].astype(BF16)
            Cg = xbc_ref[:, DI + G * N + g * N:DI + G * N + (g + 1) * N].astype(BF16)
            CBT = lax.dot_general(Bg, Cg, NT_DIMS, preferred_element_type=F32)
            HT = h_ref[g]
            yoffT = lax.dot_general(HT.astype(BF16), Cg, NT_DIMS, preferred_element_type=F32)
            xT = xbc_ref[:, g * GW:(g + 1) * GW].T
            hs = [g * R + r for r in range(R)]
            blks = [slice(r * P, (r + 1) * P) for r in range(R)]
            segs = [jnp.where(q['up'], csT[h:h + 1, :] - cs[:, h:h + 1], -1e30) for h in hs]
            GTs = [(CBT * jnp.exp(sg)).astype(BF16) for sg in segs]
            xThs = [xT[b, :] for b in blks]
            XThs = [(xTh * dtT[h:h + 1, :]).astype(BF16) for xTh, h in zip(xThs, hs)]
            ydTs = [jnp.dot(a, GT, preferred_element_type=F32) for a, GT in zip(XThs, GTs)]
            ys = [ydT + yoffT[b, :] * doutT[h:h + 1, :] for ydT, b, h in zip(ydTs, blks, hs)]
            xws = [xTh * wstT[h:h + 1, :] for xTh, h in zip(xThs, hs)]
            tots = [jnp.broadcast_to(totB[h:h + 1, :], (P, N)) for h in hs]
            y_ref[:, g * GW:(g + 1) * GW] = jnp.concatenate(ys, axis=0).T
            xwT = jnp.concatenate(xws, axis=0).astype(BF16)
            ST = jnp.dot(xwT, Bg, preferred_element_type=F32)
            st_ref[0, g] = HT
            h_ref[g] = HT * jnp.concatenate(tots, axis=0) + ST

    cmap = (lambda c: (NC - 1 - c, 0)) if rev else (lambda c: (c, 0))
    smap = (lambda c: (NC - 1 - c, 0, 0, 0)) if rev else (lambda c: (c, 0, 0, 0))
    const = lambda c: (0, 0)
    return pl.pallas_call(
        body, name=name, grid=(NC,),
        in_specs=[pl.BlockSpec((CHUNK, XBC), cmap), pl.BlockSpec((CHUNK, LANES), cmap),
                  pl.BlockSpec((LANES, LANES), const), pl.BlockSpec((LANES, LANES), const)],
        out_specs=[pl.BlockSpec((CHUNK, DI), cmap), pl.BlockSpec((1, G, GW, N), smap)],
        out_shape=[jax.ShapeDtypeStruct((S, DI), F32), jax.ShapeDtypeStruct((NC, G, GW, N), F32)],
        scratch_shapes=[pltpu.VMEM((G, GW, N), F32)],
        compiler_params=_params(dimension_semantics=("arbitrary",)),
    )(xbc, dtraw, dtbT, alogT)


def ssd_bwd(name, xbc, dtraw, dy, st, dtbT, alogT, *, S, DI, G, H, rev, tail=None):
    NC = S // CHUNK
    R = H // G
    GW = R * HEAD_DIM
    N = D_STATE
    XBC = xbc.shape[1]
    P = HEAD_DIM
    L = CHUNK

    def body(*refs):
        xbc_ref, dtraw_ref, dy_ref, st_ref, dtb_ref, alog_ref = refs[:6]
        if tail is None:
            dxbc_ref, ddt_ref, da_ref, dh_ref, dcst_ref, p2t_ref, p3t_ref, e2t_ref = refs[6:]
        else:
            other_ref, cbv_ref, dsk_ref = refs[6:9]
            dxbc_ref, ddt_ref, da_ref, dcol_ref, dh_ref, dcst_ref, p2t_ref, p3t_ref, e2t_ref = refs[9:]
        c = pl.program_id(0)

        @pl.when(c == 0)
        def _():
            dh_ref[...] = jnp.zeros_like(dh_ref)
            da_ref[...] = jnp.zeros_like(da_ref)
            dcst_ref[...] = jnp.zeros_like(dcst_ref)
            p2t_ref[...] = jnp.zeros_like(p2t_ref)
            p3t_ref[...] = jnp.zeros_like(p3t_ref)
            e2t_ref[...] = jnp.zeros_like(e2t_ref)

        q = _ssd_common(dtraw_ref[...], dtb_ref[...], alog_ref[...], rev, H)
        cs, csT, dtT, doutT, dstT, totB = q['cs'], q['csT'], q['dtT'], q['doutT'], q['dstT'], q['totB']
        wstT = dstT * dtT
        lane = q['ci']
        dcs_c = jnp.zeros((L, LANES), F32)
        for g in range(G):
            Bg = xbc_ref[:, DI + g * N:DI + (g + 1) * N].astype(BF16)
            Cg = xbc_ref[:, DI + G * N + g * N:DI + G * N + (g + 1) * N].astype(BF16)
            CB = lax.dot_general(Cg, Bg, NT_DIMS, preferred_element_type=F32)
            HpT = st_ref[0, g]
            HpTb = HpT.astype(BF16)
            dHT = dh_ref[g]
            dHTb = dHT.astype(BF16)
            BdHT = lax.dot_general(dHTb, Bg, NT_DIMS, preferred_element_type=F32)
            yoffT = lax.dot_general(HpTb, Cg, NT_DIMS, preferred_element_type=F32)
            xT = xbc_ref[:, g * GW:(g + 1) * GW].T
            dyT = dy_ref[:, g * GW:(g + 1) * GW].T
            hs = [g * R + r for r in range(R)]
            blks = [slice(r * P, (r + 1) * P) for r in range(R)]
            Lms = [jnp.exp(jnp.where(q['lo'], cs[:, h:h + 1] - csT[h:h + 1, :], -1e30)) for h in hs]
            xThs = [xT[b, :] for b in blks]
            dyThs = [dyT[b, :] for b in blks]
            xThbs = [v.astype(BF16) for v in xThs]
            dyThbs = [v.astype(BF16) for v in dyThs]
            dGxs = [lax.dot_general(a, b, TN_DIMS, preferred_element_type=F32) for a, b in zip(dyThbs, xThbs)]
            Gms = [(CB * Lm).astype(BF16) for Lm in Lms]
            u1Ts = [jnp.dot(a, Gm, preferred_element_type=F32) for a, Gm in zip(dyThbs, Gms)]
            Ts = [dGx * (Lm * dtT[h:h + 1, :]) for dGx, Lm, h in zip(dGxs, Lms, hs)]
            dCB = Ts[0]
            for T in Ts[1:]:
                dCB = dCB + T
            Msegs = [T * CB for T in Ts]
            for h, Mseg in zip(hs, Msegs):
                dcs_c = jnp.where(lane == h, jnp.sum(Mseg, axis=1, keepdims=True), dcs_c)
            uTs = [u1T + BdHT[b, :] * dstT[h:h + 1, :] for u1T, b, h in zip(u1Ts, blks, hs)]
            dyds = [dyTh * doutT[h:h + 1, :] for dyTh, h in zip(dyThs, hs)]
            xws = [xTh * wstT[h:h + 1, :] for xTh, h in zip(xThs, hs)]
            for r, h in enumerate(hs):
                b = blks[r]
                p3row = jnp.sum(xws[r] * BdHT[b, :], axis=0, keepdims=True)
                dcst_ref[h:h + 1, :] = (jnp.sum(dyds[r] * yoffT[b, :], axis=0, keepdims=True)
                                        - jnp.sum(Msegs[r], axis=0, keepdims=True) - p3row)
                p2t_ref[h:h + 1, :] = jnp.sum(xThs[r] * uTs[r], axis=0, keepdims=True)
                p3t_ref[h:h + 1, :] = p3row
                e2t_ref[h:h + 1, :] = jnp.sum(HpT[b, :] * dHT[b, :], axis=0, keepdims=True)
            dxs = [uT * dtT[h:h + 1, :] for uT, h in zip(uTs, hs)]
            tots = [jnp.broadcast_to(totB[h:h + 1, :], (P, N)) for h in hs]
            dxbc_ref[:, g * GW:(g + 1) * GW] = jnp.concatenate(dxs, axis=0).T
            dydT = jnp.concatenate(dyds, axis=0).astype(BF16)
            xwT = jnp.concatenate(xws, axis=0).astype(BF16)
            dCBb = dCB.astype(BF16)
            dC = (jnp.dot(dCBb, Bg, preferred_element_type=F32)
                  + lax.dot_general(dydT, HpTb, TN_DIMS, preferred_element_type=F32))
            dB = (lax.dot_general(dCBb, Cg, TN_DIMS, preferred_element_type=F32)
                  + lax.dot_general(xwT, dHTb, TN_DIMS, preferred_element_type=F32))
            dxbc_ref[:, DI + g * N:DI + (g + 1) * N] = dB
            dxbc_ref[:, DI + G * N + g * N:DI + G * N + (g + 1) * N] = dC
            dh_ref[g] = (dHT * jnp.concatenate(tots, axis=0)
                         + jnp.dot(dydT, Cg, preferred_element_type=F32))
        e1 = jnp.sum(p3t_ref[...], axis=1, keepdims=True)
        e2 = jnp.sum(e2t_ref[...], axis=1, keepdims=True)
        dcsT = (dcst_ref[...] + dcs_c.T
                + jnp.where(lane == q['last'], e1 + totB * e2, 0.0))
        daT = _split_dot_r(dcsT, q['lo'].astype(BF16), 3)
        ddtT = daT * q['AT'] + p2t_ref[...]
        da_ref[...] += daT * dtT
        ddraw = jnp.where(lane < H, (ddtT * _sig(q['preT'])).T, 0.0)
        if rev:
            ddraw = pltpu.roll(ddraw, H, 1)
        ddt_ref[...] = ddraw
        if tail is not None:
            for c0 in range(0, XBC, DI):
                d = dxbc_ref[:, c0:c0 + DI] + other_ref[:, c0:c0 + DI]
                if c0 == 0:
                    d = d + dy_ref[...] * dsk_ref[...]
                cb = cbv_ref[:, c0:c0 + DI]
                dcb = d * _dsilu(cb, _sig(cb))
                dxbc_ref[:, c0:c0 + DI] = dcb
                part = _rows8(dcb)

                @pl.when(c == 0)
                def _():
                    dcol_ref[:, c0:c0 + DI] = part

                @pl.when(c > 0)
                def _():
                    dcol_ref[:, c0:c0 + DI] += part

    cmap = (lambda c: (c, 0)) if rev else (lambda c: (NC - 1 - c, 0))
    smap = (lambda c: (c, 0, 0, 0)) if rev else (lambda c: (NC - 1 - c, 0, 0, 0))
    const = lambda c: (0, 0)
    sq = pltpu.VMEM((LANES, CHUNK), F32)
    in_specs = [pl.BlockSpec((CHUNK, XBC), cmap), pl.BlockSpec((CHUNK, LANES), cmap),
                pl.BlockSpec((CHUNK, DI), cmap),
                pl.BlockSpec((1, G, GW, N), smap),
                pl.BlockSpec((LANES, LANES), const), pl.BlockSpec((LANES, LANES), const)]
    out_specs = [pl.BlockSpec((CHUNK, XBC), cmap), pl.BlockSpec((CHUNK, LANES), cmap),
                 pl.BlockSpec((LANES, LANES), const)]
    out_shape = [jax.ShapeDtypeStruct((S, XBC), F32), jax.ShapeDtypeStruct((S, LANES), F32),
                 jax.ShapeDtypeStruct((LANES, LANES), F32)]
    args = [xbc, dtraw, dy, st, dtbT, alogT]
    if tail is not None:
        in_specs += [pl.BlockSpec((CHUNK, XBC), cmap), pl.BlockSpec((CHUNK, XBC), cmap),
                     pl.BlockSpec((1, DI), const)]
        out_specs.append(pl.BlockSpec((SUBLANES, XBC), const))
        out_shape.append(jax.ShapeDtypeStruct((SUBLANES, XBC), F32))
        args += list(tail)
    return pl.pallas_call(
        body, name=name, grid=(NC,), in_specs=in_specs, out_specs=out_specs, out_shape=out_shape,
        scratch_shapes=[pltpu.VMEM((G, GW, N), F32), sq, sq, sq, sq],
        compiler_params=_params(dimension_semantics=("arbitrary",)),
    )(*args)


ANY = pl.BlockSpec(memory_space=pl.ANY)


def chip_exchange(name, groups, gather):
    flat = [arr for grp in groups for arr in grp]
    n_in = len(flat)
    n_out = len(groups)
    n_rc = 3 * n_in

    def body(*refs):
        in_refs = refs[:n_in]
        out_refs = refs[n_in:n_in + n_out]
        send, recv, loc = refs[n_in + n_out:]
        x, y, c = lax.axis_index("x"), lax.axis_index("y"), lax.axis_index("c")
        me = 2 * x + y
        peers = [(1 - x, y), (x, 1 - y), (1 - x, 1 - y)]
        local, remote = [], []
        q = 0
        for a, grp in enumerate(groups):
            for l in range(len(grp)):
                src = in_refs[q]
                dst = out_refs[a].at[me] if gather else out_refs[a].at[me, l]
                own = src if gather else src.at[me]
                lc = pltpu.make_async_copy(own, dst, loc.at[q])
                lc.start()
                local.append(lc)
                for j, (px, py) in enumerate(peers):
                    blk = src if gather else src.at[2 * px + py]
                    rc = pltpu.make_async_remote_copy(
                        src_ref=blk, dst_ref=dst, send_sem=send.at[3 * q + j], recv_sem=recv.at[3 * q + j],
                        device_id=(px, py, c), device_id_type=MESH)
                    rc.start()
                    remote.append(rc)
                q += 1
        for lc in local:
            lc.wait()
        for rc in remote:
            rc.wait()

    out_shape = []
    for grp in groups:
        a0 = grp[0]
        if gather:
            out_shape.append(jax.ShapeDtypeStruct((4,) + a0.shape, a0.dtype))
        else:
            out_shape.append(jax.ShapeDtypeStruct((4, len(grp)) + a0.shape[1:], a0.dtype))
    return pl.pallas_call(
        body, name=name, in_specs=[ANY] * n_in, out_specs=[ANY] * n_out, out_shape=out_shape,
        scratch_shapes=[pltpu.SemaphoreType.DMA((n_rc,)), pltpu.SemaphoreType.DMA((n_rc,)),
                        pltpu.SemaphoreType.DMA((n_in,))],
    )(*flat)


def gather_layer(name, split, whole):
    ns, nw = len(split), len(whole)
    n = ns + nw
    n_rc = 3 * (n + ns)

    def body(*refs):
        in_refs = refs[:n]
        out_refs = refs[n:2 * n]
        send, recv, loc = refs[2 * n:]
        x, y, c = lax.axis_index("x"), lax.axis_index("y"), lax.axis_index("c")
        me = 2 * x + y
        sibling = (x, y, 1 - c)
        peers = [(1 - x, y), (x, 1 - y), (1 - x, 1 - y)]

        def region(a, chip, half):
            if a >= ns:
                return out_refs[a].at[chip]
            hr = split[a].shape[0] // 2
            return out_refs[a].at[chip, pl.ds(half * hr, hr)]

        def mine(a):
            if a >= ns:
                return in_refs[a]
            hr = split[a].shape[0] // 2
            return in_refs[a].at[pl.ds(c * hr, hr)]

        local = []
        for a in range(n):
            lc = pltpu.make_async_copy(in_refs[a], out_refs[a].at[me], loc.at[a])
            lc.start()
            local.append(lc)
        sends = []
        for a in range(n):
            for j, (px, py) in enumerate(peers):
                rc = pltpu.make_async_remote_copy(
                    src_ref=mine(a), dst_ref=region(a, me, c), send_sem=send.at[3 * a + j],
                    recv_sem=recv.at[3 * a + j], device_id=(px, py, c), device_id_type=MESH)
                rc.start()
                sends.append(rc)
        for a in range(n):
            for j, (px, py) in enumerate(peers):
                chip = 2 * px + py
                landed = pltpu.make_async_remote_copy(
                    src_ref=mine(a), dst_ref=region(a, chip, c), send_sem=send.at[3 * a + j],
                    recv_sem=recv.at[3 * a + j], device_id=(px, py, c), device_id_type=MESH)
                landed.wait_recv()
                if a < ns:
                    fw = pltpu.make_async_remote_copy(
                        src_ref=region(a, chip, c), dst_ref=region(a, chip, c), send_sem=send.at[3 * n + 3 * a + j],
                        recv_sem=recv.at[3 * n + 3 * a + j], device_id=sibling, device_id_type=MESH)
                    fw.start()
                    sends.append(fw)
        for a in range(ns):
            for j, (px, py) in enumerate(peers):
                chip = 2 * px + py
                pltpu.make_async_remote_copy(
                    src_ref=region(a, chip, 1 - c), dst_ref=region(a, chip, 1 - c), send_sem=send.at[3 * n + 3 * a + j],
                    recv_sem=recv.at[3 * n + 3 * a + j], device_id=sibling, device_id_type=MESH).wait_recv()
        for rc in sends:
            rc.wait_send()
        for lc in local:
            lc.wait()

    arrs = list(split) + list(whole)
    return pl.pallas_call(
        body, name=name, in_specs=[ANY] * n, out_specs=[ANY] * n,
        out_shape=[jax.ShapeDtypeStruct((4,) + a.shape, a.dtype) for a in arrs],
        scratch_shapes=[pltpu.SemaphoreType.DMA((n_rc,)), pltpu.SemaphoreType.DMA((n_rc,)),
                        pltpu.SemaphoreType.DMA((n,))],
    )(*arrs)


def core_send_half(name, arrs):
    n = len(arrs)

    def body(*refs):
        in_refs = refs[:n]
        out_refs = refs[n:2 * n]
        send, recv = refs[2 * n:]
        c = lax.axis_index("c")
        peer = (lax.axis_index("x"), lax.axis_index("y"), 1 - c)
        rcs = []
        for a in range(n):
            hr = arrs[a].shape[1] // 2
            rc = pltpu.make_async_remote_copy(
                src_ref=in_refs[a].at[:, pl.ds((1 - c) * hr, hr)], dst_ref=out_refs[a], send_sem=send.at[a],
                recv_sem=recv.at[a], device_id=peer, device_id_type=MESH)
            rc.start()
            rcs.append(rc)
        for rc in rcs:
            rc.wait()

    return pl.pallas_call(
        body, name=name, in_specs=[ANY] * n, out_specs=[ANY] * n,
        out_shape=[jax.ShapeDtypeStruct((4, a.shape[1] // 2, a.shape[2]), a.dtype) for a in arrs],
        scratch_shapes=[pltpu.SemaphoreType.DMA((n,)), pltpu.SemaphoreType.DMA((n,))],
    )(*arrs)


def core_fill(name, arrs, layer, n_layers):
    n = len(arrs)

    def body(*refs):
        out_refs = refs[n:2 * n]
        send, recv = refs[2 * n:]
        c = lax.axis_index("c")
        peer = (lax.axis_index("x"), lax.axis_index("y"), 1 - c)
        rcs = []
        for a in range(n):
            r = arrs[a].shape[0] // n_layers
            hr = r // 2
            rows = out_refs[a].at[pl.ds(layer * r + c * hr, hr)]
            rc = pltpu.make_async_remote_copy(src_ref=rows, dst_ref=rows, send_sem=send.at[a], recv_sem=recv.at[a],
                                              device_id=peer, device_id_type=MESH)
            rc.start()
            rcs.append(rc)
        for a in range(n):
            r = arrs[a].shape[0] // n_layers
            hr = r // 2
            theirs = out_refs[a].at[pl.ds(layer * r + (1 - c) * hr, hr)]
            pltpu.make_async_remote_copy(src_ref=theirs, dst_ref=theirs, send_sem=send.at[a], recv_sem=recv.at[a],
                                         device_id=peer, device_id_type=MESH).wait_recv()
        for rc in rcs:
            rc.wait_send()

    return pl.pallas_call(
        body, name=name, in_specs=[ANY] * n, out_specs=[ANY] * n,
        out_shape=[jax.ShapeDtypeStruct(a.shape, a.dtype) for a in arrs],
        input_output_aliases={a: a for a in range(n)},
        scratch_shapes=[pltpu.SemaphoreType.DMA((n,)), pltpu.SemaphoreType.DMA((n,))],
    )(*arrs)


def sibling_swap(name, arrs):
    n = len(arrs)

    def body(*refs):
        in_refs = refs[:n]
        out_refs = refs[n:2 * n]
        send, recv = refs[2 * n:]
        peer = (lax.axis_index("x"), lax.axis_index("y"), 1 - lax.axis_index("c"))
        rcs = []
        for a in range(n):
            rc = pltpu.make_async_remote_copy(src_ref=in_refs[a], dst_ref=out_refs[a], send_sem=send.at[a],
                                              recv_sem=recv.at[a], device_id=peer, device_id_type=MESH)
            rc.start()
            rcs.append(rc)
        for rc in rcs:
            rc.wait()

    return pl.pallas_call(
        body, name=name, in_specs=[ANY] * n, out_specs=[ANY] * n,
        out_shape=[jax.ShapeDtypeStruct(a.shape, a.dtype) for a in arrs],
        scratch_shapes=[pltpu.SemaphoreType.DMA((n,)), pltpu.SemaphoreType.DMA((n,))],
    )(*arrs)


def all8_gather(name, v):
    flips = [(fx, fy, fc) for fx in (0, 1) for fy in (0, 1) for fc in (0, 1) if (fx, fy, fc) != (0, 0, 0)]

    def body(v_ref, out_ref, send, recv, loc):
        x, y, c = lax.axis_index("x"), lax.axis_index("y"), lax.axis_index("c")
        me = 4 * x + 2 * y + c
        lc = pltpu.make_async_copy(v_ref, out_ref.at[me], loc)
        lc.start()
        rcs = []
        for k, (fx, fy, fc) in enumerate(flips):
            tgt = (x + fx - 2 * x * fx, y + fy - 2 * y * fy, c + fc - 2 * c * fc)
            rc = pltpu.make_async_remote_copy(src_ref=v_ref, dst_ref=out_ref.at[me], send_sem=send.at[k],
                                              recv_sem=recv.at[k], device_id=tgt, device_id_type=MESH)
            rc.start()
            rcs.append(rc)
        lc.wait()
        for rc in rcs:
            rc.wait()

    return pl.pallas_call(
        body, name=name, in_specs=[ANY], out_specs=ANY,
        out_shape=jax.ShapeDtypeStruct((8,) + v.shape, v.dtype),
        scratch_shapes=[pltpu.SemaphoreType.DMA((7,)), pltpu.SemaphoreType.DMA((7,)), pltpu.SemaphoreType.DMA],
    )(v)


def _pick_rows(rows, cols, target_elems=128 * 1024):
    if rows % SUBLANES != 0:
        return rows
    best = SUBLANES
    t = SUBLANES
    while t <= rows:
        if rows % t == 0 and t * cols <= target_elems:
            best = t
        t += SUBLANES
    return best


def sum_chips(name, parts):
    _, R, C = parts.shape
    tm = _pick_rows(R, C)

    def body(p_ref, o_ref):
        o_ref[...] = (p_ref[0] + p_ref[1]) + (p_ref[2] + p_ref[3])

    return pl.pallas_call(
        body, name=name, grid=(R // tm,),
        in_specs=[pl.BlockSpec((4, tm, C), lambda i: (0, i, 0))],
        out_specs=pl.BlockSpec((tm, C), lambda i: (i, 0)),
        out_shape=jax.ShapeDtypeStruct((R, C), F32),
        compiler_params=_params(dimension_semantics=("arbitrary",)),
    )(parts)


def _adamw(g, w, m, v):
    m = ADAM_B1 * m + (1.0 - ADAM_B1) * g
    v = ADAM_B2 * v + (1.0 - ADAM_B2) * (g * g)
    m_hat = m / (1.0 - ADAM_B1 ** ADAM_STEP)
    v_hat = v / (1.0 - ADAM_B2 ** ADAM_STEP)
    delta = -ADAM_LR * (m_hat / (jnp.sqrt(v_hat) + ADAM_EPS) + ADAM_WD * w)
    return delta, m, v


def adamw_shard(name, s_mine, s_sib, w, m, v):
    R, C = w.shape
    tm = _pick_rows(R, C)

    def body(a_ref, b_ref, w_ref, m_ref, v_ref, g_out, d_out, m_out, v_out):
        g = a_ref[...] + b_ref[...]
        d, mn, vn = _adamw(g, w_ref[...], m_ref[...], v_ref[...])
        g_out[...] = g
        d_out[...] = d
        m_out[...] = mn
        v_out[...] = vn

    spec = pl.BlockSpec((tm, C), lambda i: (i, 0))
    return pl.pallas_call(
        body, name=name, grid=(R // tm,), in_specs=[spec] * 5, out_specs=[spec] * 4,
        out_shape=[jax.ShapeDtypeStruct((R, C), F32)] * 4,
        compiler_params=_params(dimension_semantics=("arbitrary",)),
    )(s_mine, s_sib, w, m, v)


def core_sum(name, core, g, got):
    _, r, C = g.shape
    hr = r // 2
    tm = _pick_rows(hr, 4 * C)
    nh = hr // tm

    def body(c_ref, g_ref, s_ref, o_ref):
        o_ref[...] = g_ref[...] + s_ref[...]

    return pl.pallas_call(
        body, name=name,
        grid_spec=pltpu.PrefetchScalarGridSpec(
            num_scalar_prefetch=1, grid=(nh,),
            in_specs=[pl.BlockSpec((4, tm, C), lambda i, cr: (0, cr[0] * nh + i, 0)),
                      pl.BlockSpec((4, tm, C), lambda i, cr: (0, i, 0))],
            out_specs=pl.BlockSpec((4, tm, C), lambda i, cr: (0, i, 0))),
        out_shape=jax.ShapeDtypeStruct((4, hr, C), F32),
        compiler_params=_params(dimension_semantics=("arbitrary",)),
    )(core, g, got)


def chip_sum_into(name, core, parts, layer, n_layers, into=None):
    _, hr, C = parts.shape
    r = 2 * hr
    tm = _pick_rows(hr, 4 * C)
    nh = hr // tm

    def body(c_ref, p_ref, *rest):
        o_ref = rest[-1]
        o_ref[...] = (p_ref[0] + p_ref[1]) + (p_ref[2] + p_ref[3])

    in_specs = [pl.BlockSpec((4, tm, C), lambda i, cr: (0, i, 0))]
    args = [core, parts]
    aliases = {}
    if into is not None:
        in_specs.append(pl.BlockSpec(memory_space=pl.ANY))
        args.append(into)
        aliases = {2: 0}
    return pl.pallas_call(
        body, name=name,
        grid_spec=pltpu.PrefetchScalarGridSpec(
            num_scalar_prefetch=1, grid=(nh,), in_specs=in_specs,
            out_specs=pl.BlockSpec((tm, C), lambda i, cr: ((layer * r) // tm + cr[0] * nh + i, 0))),
        out_shape=jax.ShapeDtypeStruct((n_layers * r, C), F32), input_output_aliases=aliases,
        compiler_params=_params(dimension_semantics=("arbitrary",)),
    )(*args)


def adamw_full(name, g, w, m, v):
    R, C = w.shape
    tm = _pick_rows(R, C)

    def body(g_ref, w_ref, m_ref, v_ref, d_out, m_out, v_out):
        d, mn, vn = _adamw(g_ref[...], w_ref[...], m_ref[...], v_ref[...])
        d_out[...] = d
        m_out[...] = mn
        v_out[...] = vn

    spec = pl.BlockSpec((tm, C), lambda i: (i, 0))
    return pl.pallas_call(
        body, name=name, grid=(R // tm,), in_specs=[spec] * 4, out_specs=[spec] * 3,
        out_shape=[jax.ShapeDtypeStruct((R, C), F32)] * 3,
        compiler_params=_params(dimension_semantics=("arbitrary",)),
    )(g, w, m, v)


def adamw_small(name, parts, w, m, v):
    W = w.shape[1]

    def body(p_ref, w_ref, m_ref, v_ref, g_out, d_out, m_out, v_out):
        acc = p_ref[0]
        for k in range(1, 8):
            acc = acc + p_ref[k]
        g = jnp.sum(acc, axis=0, keepdims=True)
        d, mn, vn = _adamw(g, w_ref[...], m_ref[...], v_ref[...])
        g_out[...] = g
        d_out[...] = d
        m_out[...] = mn
        v_out[...] = vn

    return pl.pallas_call(
        body, name=name, out_shape=[jax.ShapeDtypeStruct((1, W), F32)] * 4,
        compiler_params=_params(),
    )(parts, w, m, v)


def _pad_lanes(v, width=LANES):
    return jnp.pad(v, ((0, 0), (0, width - v.shape[1])))


def _layer_fwd(cf, x, xb, pb, W, sm):
    S, D, CD, DI, XBC, F, H, G = cf['S'], cf['D'], cf['CD'], cf['DI'], cf['XBC'], cf['F'], cf['H'], cf['G']
    NM = cf['NM']
    alpha = cf['alpha']
    tm = cf['tm']
    tmx = cf['tmx']
    tn_in = cf['tn_in']
    sv = {}

    ident = lambda acc, ex: ([acc], [])
    proj, = fused_mm("in_proj", [(xb, W['in_main'], 0, False)], [], ident, [(NM, BF16, tn_in, 0)],
                     M=S, tm=tmx, tn=tn_in, nj=NM // tn_in)
    dtraw, = fused_mm("dt_proj", [(xb, W['in_dt'], 0, False)], [], ident, [(LANES, F32, LANES, 0)],
                      M=S, tm=tmx, tn=LANES)

    u, = row_call("glu", lambda a, gt: ([a * _sig(gt)], []),
                  [(proj, 'row', CD, 0, 0), (proj, 'row', CD, 1, 0)], [(CD, F32, CD, 0, 0)], M=S, tm=tm)

    def conv_a_epi(conv, ex):
        cb_, g_, b_ = ex
        ca = conv + cb_
        xhat, _ = _ln_stats(ca)
        la = xhat * g_ + b_
        return [ca, la * _sig(la)], []

    ca, sa = conv_call("conv_a", u, 0, sm['conv_a_w'], cf['KA'], conv_a_epi,
                       [(sm['conv_a_b'], 'vec', CD, 0, 0), (sm['ln_a_g'], 'vec', CD, 0, 0), (sm['ln_a_b'], 'vec', CD, 0, 0)],
                       [(CD, F32, CD, 0, 0), (CD, BF16, CD, 0, 0)], M=S, tm=cf['tmc'], cw=CD, nc=1, reverse=False)
    y_a, = fused_mm("a_out", [(sa, W['a_out'], 0, False)], [], ident, [(D, F32, D, 0)], M=S, tm=tmx, tn=D)

    def conv_b_epi(conv, ex):
        cb = conv + ex[0]
        return [cb, cb * _sig(cb)], []

    xoff = (2 * CD + 2 * D + DI) // DI
    cbv, xbc = conv_call("conv_b", proj, xoff, sm['ssm_conv_w'], cf['KB'], conv_b_epi,
                         [(sm['ssm_conv_b'], 'vec', DI, 0, 1)],
                         [(XBC, F32, DI, 0, 1), (XBC, F32, DI, 0, 1)], M=S, tm=cf['tmc'], cw=DI, nc=XBC // DI,
                         reverse=False)
    y_f, st_f = ssd_fwd("ssd_fwd_f", xbc, dtraw, sm['dtb_f'], sm['alog_f'], S=S, DI=DI, G=G, H=H, rev=False)
    y_r, st_r = ssd_fwd("ssd_fwd_r", xbc, dtraw, sm['dtb_r'], sm['alog_r'], S=S, DI=DI, G=G, H=H, rev=True)

    def gate_norm(yf, yr, xs, z, dsk, ng):
        y = yf + yr + xs * dsk
        yz = y * (z * _sig(z))
        outs = []
        gw = cf['GW']
        for g in range(G):
            t = yz[:, g * gw:(g + 1) * gw]
            outs.append(t * lax.rsqrt(jnp.mean(t * t, axis=-1, keepdims=True) + RMS_EPS))
        return [jnp.concatenate(outs, axis=1) * ng], []

    zoff = (2 * CD + 2 * D) // DI
    yn, = row_call("gate_norm", gate_norm,
                   [(y_f, 'row', DI, 0, 0), (y_r, 'row', DI, 0, 0), (xbc, 'row', DI, 0, 0), (proj, 'row', DI, zoff, 0),
                    (sm['dskip_full'], 'vec', DI, 0, 0), (sm['ssm_norm_g'], 'vec', DI, 0, 0)],
                   [(DI, BF16, DI, 0, 0)], M=S, tm=cf['tmr'])
    y_b, = fused_mm("b_out", [(yn, W['b_out'], 0, False)], [], ident, [(D, F32, D, 0)], M=S, tm=tmx, tn=D)

    goff = (2 * CD) // D
    merged, = row_call("merge", lambda ga, gb, ya, yb: ([_sig(ga) * ya + _sig(gb) * yb], []),
                       [(proj, 'row', D, goff, 0), (proj, 'row', D, goff + 1, 0), (y_a, 'row', D, 0, 0), (y_b, 'row', D, 0, 0)],
                       [(D, BF16, D, 0, 0)], M=S, tm=tm)

    def mix_epi(acc, ex):
        xin, g_, b_ = ex
        r1 = alpha * xin + acc
        xhat, _ = _ln_stats(r1)
        return [r1, xhat * g_ + b_], []

    r1, hb = fused_mm("o_mix", [(merged, W['o'], 0, False)],
                      [(x, 'row', D, 0), (sm['ln1_g'], 'vec', D, 0), (sm['ln1_b'], 'vec', D, 0)],
                      mix_epi, [(D, F32, D, 0), (D, BF16, D, 0)], M=S, tm=tm, tn=D)

    tnf = cf['tnf']

    g_ = fused_mm("ffn_gate", [(hb, W['gate_up'], 0, False)], [], ident, [(F, F32, tnf, 0)],
                  M=S, tm=tmx, tn=tnf, nj=F // tnf)[0]
    u_ = fused_mm("ffn_up", [(hb, W['gate_up'], F // tnf, False)], [], ident, [(F, F32, tnf, 0)],
                  M=S, tm=tmx, tn=tnf, nj=F // tnf)[0]
    f, = row_call("swiglu", lambda a, b: ([a * _sig(a) * b], []),
                  [(g_, 'row', tnf, 0, 1), (u_, 'row', tnf, 0, 1)], [(F, BF16, tnf, 0, 1)], M=S, tm=tm, nc=F // tnf)

    def down_epi(acc, ex):
        r1_, g1, b1, g2, b2 = ex
        xh1, _ = _ln_stats(r1_)
        r2 = alpha * (xh1 * g1 + b1) + acc
        xh2, _ = _ln_stats(r2)
        return [r2, xh2 * g2 + b2], []

    r2, h2b = fused_mm("ffn_down", [(f, W['down'], 0, False)],
                       [(r1, 'row', D, 0), (sm['ln1_g'], 'vec', D, 0), (sm['ln1_b'], 'vec', D, 0),
                        (sm['ln2_g'], 'vec', D, 0), (sm['ln2_b'], 'vec', D, 0)],
                       down_epi, [(D, F32, D, 0), (D, BF16, D, 0)], M=S, tm=tm, tn=D)

    t_, = fused_mm("ple_gate", [(h2b, W['ple_gate'], 0, False)], [], ident, [(D, F32, D, 0)], M=S, tm=tmx, tn=D)
    pe, = fused_mm("ple_proj", [(pb, W['ple'], 0, False)], [], ident, [(D, F32, D, 0)], M=S, tm=tmx, tn=D)

    def ple_mix(r2_, g2, b2, t, pe_, pg):
        xh2, _ = _ln_stats(r2_)
        h2 = xh2 * g2 + b2
        e = pe_ * lax.rsqrt(jnp.mean(pe_ * pe_, axis=-1, keepdims=True) + RMS_EPS) * pg
        xn = h2 + e * _sig(t)
        return [xn, xn], []

    xn, xnb = row_call("ple_mix", ple_mix,
                       [(r2, 'row', D, 0, 0), (sm['ln2_g'], 'vec', D, 0, 0), (sm['ln2_b'], 'vec', D, 0, 0),
                        (t_, 'row', D, 0, 0), (pe, 'row', D, 0, 0), (sm['ple_norm_g'], 'vec', D, 0, 0)],
                       [(D, F32, D, 0, 0), (D, BF16, D, 0, 0)], M=S, tm=tm)
    sv.update(x=x, xb=xb, pb=pb, proj=proj, dtraw=dtraw, u=u, ca=ca, sa=sa, y_a=y_a, cbv=cbv, xbc=xbc,
              y_f=y_f, y_r=y_r, st_f=st_f, st_r=st_r, yn=yn, y_b=y_b, merged=merged, r1=r1, hb=hb,
              g_=g_, u_=u_, f=f, r2=r2, h2b=h2b, t_=t_, pe=pe)
    return xn, xnb, sv


def _layer_bwd(cf, sv, W, sm, dxn=None, target=None, xn=None):
    S, D, CD, DI, XBC, F, H, G = cf['S'], cf['D'], cf['CD'], cf['DI'], cf['XBC'], cf['F'], cf['H'], cf['G']
    NM = cf['NM']
    alpha = cf['alpha']
    tm = cf['tm']
    gw = cf['GW']
    out = {}

    def ple_bwd_core(dx_, t, pe_, pg):
        s = _sig(t)
        rinv = lax.rsqrt(jnp.mean(pe_ * pe_, axis=-1, keepdims=True) + RMS_EPS)
        pn = pe_ * rinv
        e = pn * pg
        dtg = dx_ * e * (s * (1.0 - s))
        de = dx_ * s
        qv = de * pg
        dpe = rinv * (qv - pn * jnp.mean(qv * pn, axis=-1, keepdims=True))
        return dtg, dpe, de * pn

    if dxn is None:
        def head(xn_, tgt, t, pe_, pg):
            err = xn_ - tgt
            dx_ = err * (1.0 / D)
            dtg, dpe, dpg = ple_bwd_core(dx_, t, pe_, pg)
            return [dx_, dtg, dpe], [dpg, err * err]

        (dxn, dtg, dpe, dpg, lsq) = row_call(
            "loss_ple_bwd", head,
            [(xn, 'row', D, 0, 0), (target, 'row', D, 0, 0), (sv['t_'], 'row', D, 0, 0), (sv['pe'], 'row', D, 0, 0),
             (sm['ple_norm_g'], 'vec', D, 0, 0)],
            [(D, F32, D, 0, 0), (D, BF16, D, 0, 0), (D, BF16, D, 0, 0)], [(D, D, 0, 0), (D, D, 0, 0)], M=S, tm=tm)
        out['loss_sq'] = lsq
    else:
        def mid(dx_, t, pe_, pg):
            dtg, dpe, dpg = ple_bwd_core(dx_, t, pe_, pg)
            return [dtg, dpe], [dpg]

        (dtg, dpe, dpg) = row_call(
            "ple_bwd", mid,
            [(dxn, 'row', D, 0, 0), (sv['t_'], 'row', D, 0, 0), (sv['pe'], 'row', D, 0, 0),
             (sm['ple_norm_g'], 'vec', D, 0, 0)],
            [(D, BF16, D, 0, 0), (D, BF16, D, 0, 0)], [(D, D, 0, 0)], M=S, tm=tm)
    out['ple_norm_g'] = dpg

    def ln_bwd_epi(scale):
        def epi(acc, ex):
            res, r_, g_ = ex
            dh = scale * res + acc
            xhat, rstd = _ln_stats(r_)
            dr = _ln_bwd(dh, xhat, rstd, g_)
            return [dr, dr], [dh * xhat, dh]
        return epi

    dr2, dr2b, dg2, db2 = fused_mm(
        "dh2", [(dtg, W['ple_gate_T'], 0, False)],
        [(dxn, 'row', D, 0), (sv['r2'], 'row', D, 0), (sm['ln2_g'], 'vec', D, 0)],
        ln_bwd_epi(1.0), [(D, F32, D, 0), (D, BF16, D, 0)], [(D, D, 0), (D, D, 0)], M=S, tm=tm, tn=D)
    out['ln2_g'], out['ln2_b'] = dg2, db2

    tnf = cf['tnf']

    def dswiglu_epi(acc, ex):
        gg, uu = ex
        s = _sig(gg)
        return [acc * uu * _dsilu(gg, s), acc * (gg * s)], []

    dg_b, du_b = fused_mm(
        "d_down", [(dr2b, W['down_T'], 0, False)],
        [(sv['g_'], 'row', tnf, 0), (sv['u_'], 'row', tnf, 0)], dswiglu_epi,
        [(F, BF16, tnf, 0), (F, BF16, tnf, 0)], M=S, tm=tm, tn=tnf, nj=F // tnf)

    dr1, dr1b, dg1, db1 = fused_mm(
        "dh1", [(dg_b, W['gate_T'], 0, True), (du_b, W['up_T'], 0, True)],
        [(dr2, 'row', D, 0), (sv['r1'], 'row', D, 0), (sm['ln1_g'], 'vec', D, 0)],
        ln_bwd_epi(alpha), [(D, F32, D, 0), (D, BF16, D, 0)], [(D, D, 0), (D, D, 0)],
        M=S, tm=tm, tn=D, nk=cf['nk_f'])
    out['ln1_g'], out['ln1_b'] = dg1, db1

    goff = (2 * CD) // D

    def dmerge_epi(acc, ex):
        ga, gb, ya, yb = ex
        sa_, sb_ = _sig(ga), _sig(gb)
        dga = acc * ya * (sa_ * (1.0 - sa_))
        dgb = acc * yb * (sb_ * (1.0 - sb_))
        return [jnp.concatenate([dga, dgb], axis=1), acc * sa_, acc * sb_], []

    dproj, dya_b, dyb_b = fused_mm(
        "d_merge", [(dr1b, W['o_T'], 0, False)],
        [(sv['proj'], 'row', D, goff), (sv['proj'], 'row', D, goff + 1), (sv['y_a'], 'row', D, 0), (sv['y_b'], 'row', D, 0)],
        dmerge_epi, [(NM, BF16, 2 * D, (2 * CD) // (2 * D)), (D, BF16, D, 0), (D, BF16, D, 0)], M=S, tm=tm, tn=D)

    def dsa_epi(acc, ex):
        ca_, g_, b_ = ex
        xhat, rstd = _ln_stats(ca_)
        la = xhat * g_ + b_
        dla = acc * _dsilu(la, _sig(la))
        dca = _ln_bwd(dla, xhat, rstd, g_)
        return [dca], [dla * xhat, dla, dca]

    dca, dlag, dlab, dcab = fused_mm(
        "d_a_out", [(dya_b, W['a_out_T'], 0, False)],
        [(sv['ca'], 'row', CD, 0), (sm['ln_a_g'], 'vec', CD, 0), (sm['ln_a_b'], 'vec', CD, 0)],
        dsa_epi, [(CD, F32, CD, 0)], [(CD, CD, 0), (CD, CD, 0), (CD, CD, 0)], M=S, tm=tm, tn=D)
    out['ln_a_g'], out['ln_a_b'], out['conv_a_b'] = dlag, dlab, dcab

    def dglu_epi(du, ex):
        a, gt = ex
        s = _sig(gt)
        return [jnp.concatenate([du * s, du * a * (s * (1.0 - s))], axis=1)], []

    dproj, dwa = conv_call(
        "d_conv_a", dca, 0, sm['conv_a_w'], cf['KA'], dglu_epi,
        [(sv['proj'], 'row', CD, 0, 0), (sv['proj'], 'row', CD, 1, 0)],
        [(NM, BF16, 2 * CD, 0, 0)], M=S, tm=cf['tmc'], cw=CD, nc=1, reverse=True, xin=(sv['u'], 0),
        passthrough=(dproj, 0))
    out['conv_a_w'] = dwa

    zoff = (2 * CD + 2 * D) // DI

    def dgate_norm_epi(acc, ex):
        yf, yr, xs, z, dsk, ng = ex
        y = yf + yr + xs * dsk
        sz = _sig(z)
        siluz = z * sz
        yz = y * siluz
        dyzs, yhats = [], []
        for g in range(G):
            t = yz[:, g * gw:(g + 1) * gw]
            rinv = lax.rsqrt(jnp.mean(t * t, axis=-1, keepdims=True) + RMS_EPS)
            yh = t * rinv
            qv = acc[:, g * gw:(g + 1) * gw] * ng[:, g * gw:(g + 1) * gw]
            dyzs.append(rinv * (qv - yh * jnp.mean(qv * yh, axis=-1, keepdims=True)))
            yhats.append(yh)
        dyz = jnp.concatenate(dyzs, axis=1)
        yhat = jnp.concatenate(yhats, axis=1)
        dy = dyz * siluz
        dz = dyz * y * _dsilu(z, sz)
        return [dy, dz], [acc * yhat, dy * xs]

    tmr = cf['tmr']
    dy_ssd, dproj, dng, ddsk = fused_mm(
        "d_b_out", [(dyb_b, W['b_out_T'], 0, False)],
        [(sv['y_f'], 'row', DI, 0), (sv['y_r'], 'row', DI, 0), (sv['xbc'], 'row', DI, 0), (sv['proj'], 'row', DI, zoff),
         (sm['dskip_full'], 'vec', DI, 0), (sm['ssm_norm_g'], 'vec', DI, 0)],
        dgate_norm_epi, [(DI, F32, DI, 0), (NM, BF16, DI, zoff)], [(DI, DI, 0), (DI, DI, 0)],
        M=S, tm=tmr, tn=DI, passthrough=(dproj, 1))
    out['ssm_norm_g'], out['dskip_full'] = dng, ddsk

    dxbc_f, ddt_f, dA_f = ssd_bwd("ssd_bwd_f", sv['xbc'], sv['dtraw'], dy_ssd, sv['st_f'], sm['dtb_f'], sm['alog_f'],
                                  S=S, DI=DI, G=G, H=H, rev=False)
    dcb, ddt_r, dA_r, dcbb = ssd_bwd("ssd_bwd_r", sv['xbc'], sv['dtraw'], dy_ssd, sv['st_r'], sm['dtb_r'], sm['alog_r'],
                                     S=S, DI=DI, G=G, H=H, rev=True, tail=(dxbc_f, sv['cbv'], sm['dskip_full']))
    out['dA_f'], out['dA_r'] = dA_f, dA_r
    out['ssm_conv_b'] = dcbb

    xoff = (2 * CD + 2 * D + DI) // DI
    dproj, dwb = conv_call(
        "d_conv_b", dcb, 0, sm['ssm_conv_w'], cf['KB'], lambda conv, ex: ([conv], []), [],
        [(NM, BF16, DI, xoff, 1)], M=S, tm=cf['tmc'], cw=DI, nc=XBC // DI, reverse=True, xin=(sv['proj'], xoff),
        passthrough=(dproj, 0))
    out['ssm_conv_w'] = dwb

    ddtb, ddt_bias = row_call("d_dt", lambda a, b: ([a + b], [a + b]),
                              [(ddt_f, 'row', LANES, 0, 0), (ddt_r, 'row', LANES, 0, 0)],
                              [(LANES, BF16, LANES, 0, 0)], [(LANES, LANES, 0, 0)], M=S, tm=tm)
    out['dt_bias'] = ddt_bias

    dx, = fused_mm("d_x", [(dproj, W['in_main_T'], 0, True), (ddtb, W['in_dt_T'], 0, False)],
                   [(dr1, 'row', D, 0)], lambda acc, ex: ([alpha * ex[0] + acc], []),
                   [(D, F32, D, 0)], M=S, tm=tm, tn=D, nk=cf['nk_in'])

    tmw = cf['tmw']
    xb = sv['xb']
    out['w_in'] = jnp.concatenate(
        [mm_tn("dw_in", xb, dproj, tm=tmw, tk=D, tn=cf['tn_in']),
         mm_tn("dw_dt", xb, ddtb, tm=tmw, tk=D, tn=LANES)[:, :2 * H]], axis=1)
    out['w_a_out'] = mm_tn("dw_a_out", sv['sa'], dya_b, tm=tmw, tk=CD, tn=D)
    out['w_b_out'] = mm_tn("dw_b_out", sv['yn'], dyb_b, tm=tmw, tk=DI // 2, tn=D)
    out['w_o'] = mm_tn("dw_o", sv['merged'], dr1b, tm=tmw, tk=D, tn=D)
    out['w_gate_up'] = jnp.concatenate(
        [mm_tn("dw_gate", sv['hb'], dg_b, tm=tmw, tk=D, tn=tnf),
         mm_tn("dw_up", sv['hb'], du_b, tm=tmw, tk=D, tn=tnf)], axis=1)
    out['w_down'] = mm_tn("dw_down", sv['f'], dr2b, tm=tmw, tk=tnf, tn=D)
    out['w_ple'] = mm_tn("dw_ple", sv['pb'], dpe, tm=tmw, tk=sv['pb'].shape[1], tn=D)
    out['w_ple_gate'] = mm_tn("dw_ple_gate", sv['h2b'], dtg, tm=tmw, tk=D, tn=D)
    return dx, out


_WEIGHTS = ['w_in', 'conv_a_w', 'conv_a_b', 'ln_a_g', 'ln_a_b', 'w_a_out', 'ssm_conv_w', 'ssm_conv_b', 'a_log',
            'dt_bias', 'd_skip', 'ssm_norm_g', 'w_b_out', 'w_o', 'ln1_g', 'ln1_b', 'w_gate_up', 'w_down', 'ln2_g',
            'ln2_b', 'w_ple', 'ple_norm_g', 'w_ple_gate']
_COL_SHARDED = ['w_in', 'conv_a_w', 'ssm_conv_w', 'w_gate_up', 'w_ple']
_ROW_SHARDED = ['w_a_out', 'w_b_out', 'w_o', 'w_down', 'w_ple_gate']
_BIG = _COL_SHARDED + _ROW_SHARDED
_SMALL = [n for n in _WEIGHTS if n not in _BIG]
_CONV = ['conv_a_w', 'ssm_conv_w']


def _ceil_to(n, k):
    return -(-n // k) * k


def kernel(x, p, w_in, conv_a_w, conv_a_b, ln_a_g, ln_a_b, w_a_out, ssm_conv_w, ssm_conv_b, a_log, dt_bias, d_skip, ssm_norm_g, w_b_out, w_o, ln1_g, ln1_b, w_gate_up, w_down, ln2_g, ln2_b, w_ple, ple_norm_g, w_ple_gate, loss_target, m_w_in, m_conv_a_w, m_conv_a_b, m_ln_a_g, m_ln_a_b, m_w_a_out, m_ssm_conv_w, m_ssm_conv_b, m_a_log, m_dt_bias, m_d_skip, m_ssm_norm_g, m_w_b_out, m_w_o, m_ln1_g, m_ln1_b, m_w_gate_up, m_w_down, m_ln2_g, m_ln2_b, m_w_ple, m_ple_norm_g, m_w_ple_gate, v_w_in, v_conv_a_w, v_conv_a_b, v_ln_a_g, v_ln_a_b, v_w_a_out, v_ssm_conv_w, v_ssm_conv_b, v_a_log, v_dt_bias, v_d_skip, v_ssm_norm_g, v_w_b_out, v_w_o, v_ln1_g, v_ln1_b, v_w_gate_up, v_w_down, v_ln2_g, v_ln2_b, v_w_ple, v_ple_norm_g, v_w_ple_gate):
    wt = dict(w_in=w_in, conv_a_w=conv_a_w, conv_a_b=conv_a_b, ln_a_g=ln_a_g, ln_a_b=ln_a_b, w_a_out=w_a_out,
              ssm_conv_w=ssm_conv_w, ssm_conv_b=ssm_conv_b, a_log=a_log, dt_bias=dt_bias, d_skip=d_skip,
              ssm_norm_g=ssm_norm_g, w_b_out=w_b_out, w_o=w_o, ln1_g=ln1_g, ln1_b=ln1_b, w_gate_up=w_gate_up,
              w_down=w_down, ln2_g=ln2_g, ln2_b=ln2_b, w_ple=w_ple, ple_norm_g=ple_norm_g, w_ple_gate=w_ple_gate)
    mo = dict(w_in=m_w_in, conv_a_w=m_conv_a_w, conv_a_b=m_conv_a_b, ln_a_g=m_ln_a_g, ln_a_b=m_ln_a_b,
              w_a_out=m_w_a_out, ssm_conv_w=m_ssm_conv_w, ssm_conv_b=m_ssm_conv_b, a_log=m_a_log,
              dt_bias=m_dt_bias, d_skip=m_d_skip, ssm_norm_g=m_ssm_norm_g, w_b_out=m_w_b_out, w_o=m_w_o,
              ln1_g=m_ln1_g, ln1_b=m_ln1_b, w_gate_up=m_w_gate_up, w_down=m_w_down, ln2_g=m_ln2_g, ln2_b=m_ln2_b,
              w_ple=m_w_ple, ple_norm_g=m_ple_norm_g, w_ple_gate=m_w_ple_gate)
    vo = dict(w_in=v_w_in, conv_a_w=v_conv_a_w, conv_a_b=v_conv_a_b, ln_a_g=v_ln_a_g, ln_a_b=v_ln_a_b,
              w_a_out=v_w_a_out, ssm_conv_w=v_ssm_conv_w, ssm_conv_b=v_ssm_conv_b, a_log=v_a_log,
              dt_bias=v_dt_bias, d_skip=v_d_skip, ssm_norm_g=v_ssm_norm_g, w_b_out=v_w_b_out, w_o=v_w_o,
              ln1_g=v_ln1_g, ln1_b=v_ln1_b, w_gate_up=v_w_gate_up, w_down=v_w_down, ln2_g=v_ln2_g, ln2_b=v_ln2_b,
              w_ple=v_w_ple, ple_norm_g=v_ple_norm_g, w_ple_gate=v_w_ple_gate)

    L = w_in.shape[0]
    S, D = x.shape[1], x.shape[2]
    CD = conv_a_b.shape[1]
    DI = ssm_norm_g.shape[1]
    XBC = ssm_conv_b.shape[1]
    H = d_skip.shape[1]
    G = (XBC - DI) // (2 * D_STATE)
    F = w_down.shape[1] * 4
    N_IN = w_in.shape[2] * 4
    NM = N_IN - 2 * H
    KA, KB = conv_a_w.shape[1], ssm_conv_w.shape[1]
    assert DI == H * HEAD_DIM and CD == D and DI == 2 * D and XBC == 2 * DI and NM == 2 * CD + 2 * D + DI + XBC
    assert 2 * H <= LANES and S % CHUNK == 0
    tnf = F // 2
    cf = dict(S=S, D=D, CD=CD, DI=DI, XBC=XBC, F=F, H=H, G=G, NM=NM, KA=KA, KB=KB, GW=(H // G) * HEAD_DIM,
              alpha=float((2 * L) ** 0.25), tm=min(512, S), tmx=min(1024, S), tmc=min(256, S), tmr=min(256, S), tmw=min(1024, S),
              tn_in=D, tnf=tnf, nk_f=2, nk_in=NM // DI)

    core = lax.axis_index("c").astype(jnp.int32).reshape(1)
    split_names = [n for n in _BIG if n not in _CONV]

    def layer_weights(l):
        got = gather_layer("gather_weights", [wt[n][l].astype(BF16) for n in split_names], [wt[n][l] for n in _CONV])
        full = {}
        for n, g in zip(split_names + _CONV, got):
            if n in _COL_SHARDED:
                full[n] = g.transpose(1, 0, 2).reshape(g.shape[1], 4 * g.shape[2])
            else:
                full[n] = g.reshape(4 * g.shape[1], g.shape[2])
        win = full['w_in']
        in_main = win[:, :NM]
        in_dt = _pad_lanes(win[:, NM:])
        gu = full['w_gate_up']
        W = dict(in_main=in_main, in_dt=in_dt, in_main_T=in_main.T, in_dt_T=in_dt.T,
                 a_out=full['w_a_out'], a_out_T=full['w_a_out'].T,
                 b_out=full['w_b_out'], b_out_T=full['w_b_out'].T,
                 o=full['w_o'], o_T=full['w_o'].T, gate_up=gu, gate_T=gu[:, :F].T, up_T=gu[:, F:].T,
                 down=full['w_down'], down_T=full['w_down'].T, ple=full['w_ple'],
                 ple_gate=full['w_ple_gate'], ple_gate_T=full['w_ple_gate'].T)
        row = lambda v: v.reshape(1, -1)
        head_table = lambda v: jnp.broadcast_to(jnp.pad(v, (0, LANES - H))[:, None], (LANES, LANES))
        sm = dict(conv_a_w=jnp.pad(full['conv_a_w'], ((0, _ceil_to(KA, SUBLANES) - KA), (0, 0))),
                  ssm_conv_w=jnp.pad(full['ssm_conv_w'], ((0, _ceil_to(KB, SUBLANES) - KB), (0, 0))),
                  conv_a_b=row(conv_a_b[l]), ln_a_g=row(ln_a_g[l]), ln_a_b=row(ln_a_b[l]),
                  ssm_conv_b=row(ssm_conv_b[l]), ssm_norm_g=row(ssm_norm_g[l]),
                  ln1_g=row(ln1_g[l]), ln1_b=row(ln1_b[l]), ln2_g=row(ln2_g[l]), ln2_b=row(ln2_b[l]),
                  ple_norm_g=row(ple_norm_g[l]),
                  dtb_f=head_table(dt_bias[l, 0]), dtb_r=head_table(dt_bias[l, 1]),
                  alog_f=head_table(a_log[l, 0]), alog_r=head_table(a_log[l, 1]),
                  dskip_full=row(jnp.repeat(d_skip[l], HEAD_DIM)))
        return W, sm

    def blocks(n, gl):
        g = gl[n]
        if n == 'conv_a_w':
            g = g.sum(axis=1)[:KA]
        elif n == 'ssm_conv_w':
            g = g.sum(axis=1)[:KB]
        if n in _COL_SHARDED:
            return g.reshape(g.shape[0], 4, g.shape[1] // 4).transpose(1, 0, 2)
        return g.reshape(4, g.shape[0] // 4, g.shape[1])

    def reduce_layer(l, gl, acc):
        mine = [blocks(n, gl) for n in split_names]
        theirs = core_send_half("core_send_half", mine)
        both = [core_sum("core_sum_" + n, core, b, t) for n, b, t in zip(split_names, mine, theirs)]
        parts = chip_exchange("scatter_grads", [[t] for t in both], gather=False)
        sums = [chip_sum_into("chip_sum_" + n, core, pr.reshape(4, pr.shape[2], pr.shape[3]), l, L, into=acc.get(n))
                for n, pr in zip(split_names, parts)]
        return dict(zip(split_names, core_fill("core_fill", sums, l, L)))

    lw = [layer_weights(l) for l in range(L)]
    xl = x[0]
    xlb = xl.astype(BF16)
    saved = []
    for l in range(L):
        xl, xlb, sv = _layer_fwd(cf, xl, xlb, p[l, 0].astype(BF16), lw[l][0], lw[l][1])
        saved.append(sv)
    grads = [None] * L
    dxl = None
    gsum = {}
    for l in reversed(range(L)):
        if l == L - 1:
            dxl, grads[l] = _layer_bwd(cf, saved[l], lw[l][0], lw[l][1], target=loss_target[0], xn=xl)
        else:
            dxl, grads[l] = _layer_bwd(cf, saved[l], lw[l][0], lw[l][1], dxn=dxl)
        gsum = reduce_layer(l, grads[l], gsum)
    loss = lax.psum(0.5 / D * jnp.sum(grads[L - 1]['loss_sq']), ("x", "y", "c"))
    grad_x = dxl[None]

    res = {}
    for n in split_names:
        shp = wt[n].shape
        flat = lambda a: a.reshape(shp[0] * shp[1], shp[2])
        outs = adamw_full("adamw_" + n, gsum[n], flat(wt[n]), flat(mo[n]), flat(vo[n]))
        res[n] = [o.reshape(shp) for o in [gsum[n]] + list(outs)]
    parts = chip_exchange("scatter_conv", [[blocks(n, grads[l]) for l in range(L)] for n in _CONV], gather=False)
    chip_sums = [sum_chips("chip_sum_" + n, pr.reshape(4, L * pr.shape[2], pr.shape[3])) for n, pr in zip(_CONV, parts)]
    sib_sums = sibling_swap("core_swap", chip_sums)
    for n, mine, sib in zip(_CONV, chip_sums, sib_sums):
        shp = wt[n].shape
        flat = lambda a: a.reshape(shp[0] * shp[1], shp[2])
        outs = adamw_shard("adamw_" + n, mine, sib, flat(wt[n]), flat(mo[n]), flat(vo[n]))
        res[n] = [o.reshape(shp) for o in outs]

    def small_pieces(l):
        gl = grads[l]
        A = -jnp.exp(a_log[l])
        d = dict(gl)
        d_alog = jnp.concatenate([gl['dA_f'].sum(axis=1)[:H] * A[0], gl['dA_r'].sum(axis=1)[:H] * A[1]])
        d['a_log'] = jnp.pad(d_alog[None], ((0, SUBLANES - 1), (0, 0)))
        d['dt_bias'] = gl['dt_bias'][:, :2 * H]
        d['d_skip'] = gl['dskip_full'].reshape(SUBLANES, H, HEAD_DIM).sum(axis=-1)
        return [_pad_lanes(d[n], _ceil_to(d[n].shape[1], LANES)) for n in _SMALL]

    widths = [_ceil_to(math.prod(wt[n].shape[1:]), LANES) for n in _SMALL]
    packed = jnp.concatenate([pc for l in range(L) for pc in small_pieces(l)], axis=1)
    gathered = all8_gather("gather_small", packed)

    def pack_params(src):
        return jnp.concatenate([_pad_lanes(src[n][l].reshape(1, -1), wd) for l in range(L) for n, wd in zip(_SMALL, widths)],
                               axis=1)

    small_out = adamw_small("adamw_small", gathered, pack_params(wt), pack_params(mo), pack_params(vo))
    off = 0
    per = {n: [[] for _ in range(4)] for n in _SMALL}
    for l in range(L):
        for n, wd in zip(_SMALL, widths):
            size = math.prod(wt[n].shape[1:])
            for k in range(4):
                per[n][k].append(small_out[k][0, off:off + size].reshape(wt[n].shape[1:]))
            off += wd
    for n in _SMALL:
        res[n] = [jnp.stack(per[n][k]) for k in range(4)]

    return (loss, grad_x, *[res[n][0] for n in _WEIGHTS], *[res[n][1] for n in _WEIGHTS],
            *[res[n][2] for n in _WEIGHTS], *[res[n][3] for n in _WEIGHTS])
```

```python
import math

import jax
import jax.numpy as jnp
from jax import lax
from jax.experimental import pallas as pl
from jax.experimental.pallas import tpu as pltpu

F32 = jnp.float32
BF16 = jnp.bfloat16

VMEM_LIMIT_BYTES = 56 * 1024 * 1024
LANES = 128
SUBLANES = 8

CHUNK = 128
D_STATE = 128
HEAD_DIM = 64
LN_EPS = 1e-5
RMS_EPS = 1e-6
ADAM_LR = 0.001
ADAM_B1 = 0.9
ADAM_B2 = 0.999
ADAM_EPS = 1e-08
ADAM_WD = 0.01
ADAM_STEP = 10
HALO = 16
MESH = pl.DeviceIdType.MESH


def _params(**kw):
    return pltpu.CompilerParams(vmem_limit_bytes=VMEM_LIMIT_BYTES, **kw)


def _sig(x):
    return jax.nn.sigmoid(x)


def _dsilu(x, s):
    return s * (1.0 + x * (1.0 - s))


def _ln_stats(r):
    mu = jnp.mean(r, axis=-1, keepdims=True)
    xc = r - mu
    var = jnp.mean(xc * xc, axis=-1, keepdims=True)
    rstd = lax.rsqrt(var + LN_EPS)
    return xc * rstd, rstd


def _ln_bwd(dy, xhat, rstd, g):
    dxh = dy * g
    m1 = jnp.mean(dxh, axis=-1, keepdims=True)
    m2 = jnp.mean(dxh * xhat, axis=-1, keepdims=True)
    return rstd * (dxh - m1 - xhat * m2)


def _f32(v):
    return v if v.dtype == F32 else v.astype(F32)


def _rows8(v):
    tm, w = v.shape
    return v.reshape(tm // SUBLANES, SUBLANES, w).sum(axis=0)


def fused_mm(name, prods, extras, epi, row_outs, col_outs=(), *, M, tm, tn, nj=1, nk=1,
             passthrough=None):
    np_ = len(prods)
    ne = len(extras)
    nro = len(row_outs)
    nco = len(col_outs)
    use_acc = nk > 1

    def body(*refs):
        a_refs = [refs[2 * p] for p in range(np_)]
        w_refs = [refs[2 * p + 1] for p in range(np_)]
        pos = 2 * np_
        e_refs = refs[pos:pos + ne]
        pos += ne
        if passthrough is not None:
            pos += 1
        ro_refs = refs[pos:pos + nro]
        pos += nro
        co_refs = refs[pos:pos + nco]
        pos += nco
        acc_ref = refs[pos] if use_acc else None
        i = pl.program_id(1)
        k = pl.program_id(2)

        def prod(p):
            a = a_refs[p][...]
            if a.dtype != BF16:
                a = a.astype(BF16)
            return jnp.dot(a, w_refs[p][...], preferred_element_type=F32)

        def finish(acc):
            rows, cols = epi(acc, [_f32(r[...]) for r in e_refs])
            for v, o in zip(rows, ro_refs):
                o[...] = v.astype(o.dtype)
            for v, o in zip(cols, co_refs):
                v8 = _rows8(v)

                @pl.when(i == 0)
                def _():
                    o[...] = v8

                @pl.when(i > 0)
                def _():
                    o[...] += v8

        if not use_acc:
            acc = prod(0)
            for p in range(1, np_):
                acc = acc + prod(p)
            finish(acc)
        else:
            @pl.when(k == 0)
            def _():
                acc = None
                for p in range(np_):
                    acc = prod(p) if acc is None else acc + prod(p)
                acc_ref[...] = acc

            @pl.when(k > 0)
            def _():
                acc = None
                for p in range(np_):
                    if prods[p][3]:
                        acc = prod(p) if acc is None else acc + prod(p)
                acc_ref[...] += acc

            @pl.when(k == nk - 1)
            def _():
                finish(acc_ref[...])

    in_specs = []
    args = []
    for a, w, joff, ksplit in prods:
        K = a.shape[1]
        if ksplit:
            tk = K // nk
            in_specs.append(pl.BlockSpec((tm, tk), lambda j, i, k: (i, k)))
            in_specs.append(pl.BlockSpec((tk, tn), lambda j, i, k, joff=joff: (k, j + joff)))
        else:
            in_specs.append(pl.BlockSpec((tm, K), lambda j, i, k: (i, 0)))
            in_specs.append(pl.BlockSpec((K, tn), lambda j, i, k, joff=joff: (0, j + joff)))
        args += [a, w]
    for arr, kind, width, c0 in extras:
        if kind == 'row':
            in_specs.append(pl.BlockSpec((tm, width), lambda j, i, k, c0=c0: (i, c0 + j)))
        else:
            in_specs.append(pl.BlockSpec((arr.shape[0], width), lambda j, i, k, c0=c0: (0, c0 + j)))
        args.append(arr)
    aliases = {}
    if passthrough is not None:
        arr, oidx = passthrough
        in_specs.append(pl.BlockSpec(memory_space=pl.ANY))
        aliases = {len(args): oidx}
        args.append(arr)
    out_shape = []
    out_specs = []
    for n_total, dtype, width, c0 in row_outs:
        out_shape.append(jax.ShapeDtypeStruct((M, n_total), dtype))
        out_specs.append(pl.BlockSpec((tm, width), lambda j, i, k, c0=c0: (i, c0 + j)))
    for n_total, width, c0 in col_outs:
        out_shape.append(jax.ShapeDtypeStruct((SUBLANES, n_total), F32))
        out_specs.append(pl.BlockSpec((SUBLANES, width), lambda j, i, k, c0=c0: (0, c0 + j)))
    scratch = [pltpu.VMEM((tm, tn), F32)] if use_acc else []
    return pl.pallas_call(
        body, name=name, grid=(nj, M // tm, nk), in_specs=in_specs, out_specs=out_specs,
        out_shape=out_shape, scratch_shapes=scratch, input_output_aliases=aliases,
        compiler_params=_params(dimension_semantics=("arbitrary", "arbitrary", "arbitrary")),
    )(*args)


def mm_tn(name, a, b, *, tm, tk, tn):
    M, K = a.shape
    N = b.shape[1]

    def body(a_ref, b_ref, o_ref):
        m = pl.program_id(2)
        p = lax.dot_general(a_ref[...], b_ref[...], (((0,), (0,)), ((), ())),
                            preferred_element_type=F32)

        @pl.when(m == 0)
        def _():
            o_ref[...] = p

        @pl.when(m > 0)
        def _():
            o_ref[...] += p

    return pl.pallas_call(
        body, name=name, grid=(K // tk, N // tn, M // tm),
        in_specs=[pl.BlockSpec((tm, tk), lambda kk, j, m: (m, kk)),
                  pl.BlockSpec((tm, tn), lambda kk, j, m: (m, j))],
        out_specs=pl.BlockSpec((tk, tn), lambda kk, j, m: (kk, j)),
        out_shape=jax.ShapeDtypeStruct((K, N), F32),
        compiler_params=_params(dimension_semantics=("arbitrary", "arbitrary", "arbitrary")),
    )(a, b)


def row_call(name, fn, ins, row_outs, col_outs=(), *, M, tm, nc=1):
    ni = len(ins)
    nro = len(row_outs)

    def body(*refs):
        i = pl.program_id(1)
        vals = [_f32(r[...]) for r in refs[:ni]]
        rows, cols = fn(*vals)
        for v, o in zip(rows, refs[ni:ni + nro]):
            o[...] = v.astype(o.dtype)
        for v, o in zip(cols, refs[ni + nro:]):
            v8 = _rows8(v)

            @pl.when(i == 0)
            def _():
                o[...] = v8

            @pl.when(i > 0)
            def _():
                o[...] += v8

    in_specs = []
    for arr, kind, width, c0, cmul in ins:
        if kind == 'row':
            in_specs.append(pl.BlockSpec((tm, width), lambda cj, i, c0=c0, cmul=cmul: (i, c0 + cmul * cj)))
        else:
            in_specs.append(pl.BlockSpec((arr.shape[0], width), lambda cj, i, c0=c0, cmul=cmul: (0, c0 + cmul * cj)))
    out_shape = []
    out_specs = []
    for n_total, dtype, width, c0, cmul in row_outs:
        out_shape.append(jax.ShapeDtypeStruct((M, n_total), dtype))
        out_specs.append(pl.BlockSpec((tm, width), lambda cj, i, c0=c0, cmul=cmul: (i, c0 + cmul * cj)))
    for n_total, width, c0, cmul in col_outs:
        out_shape.append(jax.ShapeDtypeStruct((SUBLANES, n_total), F32))
        out_specs.append(pl.BlockSpec((SUBLANES, width), lambda cj, i, c0=c0, cmul=cmul: (0, c0 + cmul * cj)))
    return pl.pallas_call(
        body, name=name, grid=(nc, M // tm), in_specs=in_specs, out_specs=out_specs,
        out_shape=out_shape,
        compiler_params=_params(dimension_semantics=("arbitrary", "arbitrary")),
    )(*[a[0] for a in ins])


def conv_call(name, src, src_c0, w, K, epi, extras, row_outs, col_outs=(), *, M, tm, cw, nc,
              reverse, xin=None, passthrough=None):
    pad = (K - 1) // 2
    assert pad <= HALO - 1
    R = tm // HALO
    nblk = M // HALO
    n_i = M // tm
    Kp = w.shape[0]
    ne = len(extras)
    nro = len(row_outs)
    nco = len(col_outs)
    rb = 64
    cbw = min(cw, 256)
    n_copies = SUBLANES if K > SUBLANES else 1

    def body(*refs):
        main_ref, prev_ref, next_ref, w_ref = refs[:4]
        pos = 4
        xin_ref = None
        if xin is not None:
            xin_ref = refs[pos]
            pos += 1
        e_refs = refs[pos:pos + ne]
        pos += ne
        if passthrough is not None:
            pos += 1
        ro_refs = refs[pos:pos + nro]
        pos += nro
        co_refs = refs[pos:pos + nco]
        pos += nco
        dw_ref = None
        if xin is not None:
            dw_ref = refs[pos]
            pos += 1
        ext_ref, conv_ref = refs[pos], refs[pos + 1]
        i = pl.program_id(1)

        ext_ref[0, 0:HALO, :] = jnp.where(i == 0, 0.0, prev_ref[...].astype(F32))
        ext_ref[0, HALO:HALO + tm, :] = main_ref[...].astype(F32)
        ext_ref[0, HALO + tm:, :] = jnp.where(i == n_i - 1, 0.0, next_ref[...].astype(F32))
        if dw_ref is not None:
            @pl.when(i == 0)
            def _():
                dw_ref[...] = jnp.zeros_like(dw_ref)

        n_sh = tm + 2 * HALO - SUBLANES
        for c0 in range(0, cw, cbw):
            for sft in range(1, n_copies):
                ext_ref[sft, 0:n_sh, c0:c0 + cbw] = ext_ref[0, sft:sft + n_sh, c0:c0 + cbw]

        for c0 in range(0, cw, cbw):
            for r0 in range(0, tm, rb):
                acc = jnp.zeros((rb, cbw), F32)
                if xin_ref is not None:
                    xblk = xin_ref[r0:r0 + rb, c0:c0 + cbw].astype(F32)
                for k in range(K):
                    off = HALO + r0 + ((pad - k) if reverse else (k - pad))
                    sft = off % SUBLANES if n_copies > 1 else 0
                    d = ext_ref[sft, off - sft:off - sft + rb, c0:c0 + cbw]
                    acc = acc + d * w_ref[k:k + 1, c0:c0 + cbw]
                    if xin_ref is not None:
                        dw_ref[k, :, c0:c0 + cbw] += _rows8(xblk * d)
                conv_ref[r0:r0 + rb, c0:c0 + cbw] = acc

        rows, cols = epi(conv_ref[...], [_f32(r[...]) for r in e_refs])
        for v, o in zip(rows, ro_refs):
            o[...] = v.astype(o.dtype)
        for v, o in zip(cols, co_refs):
            v8 = _rows8(v)

            @pl.when(i == 0)
            def _():
                o[...] = v8

            @pl.when(i > 0)
            def _():
                o[...] += v8

    in_specs = [
        pl.BlockSpec((tm, cw), lambda cj, i: (i, src_c0 + cj)),
        pl.BlockSpec((HALO, cw), lambda cj, i: (jnp.maximum(i * R - 1, 0), src_c0 + cj)),
        pl.BlockSpec((HALO, cw), lambda cj, i: (jnp.minimum((i + 1) * R, nblk - 1), src_c0 + cj)),
        pl.BlockSpec((Kp, cw), lambda cj, i: (0, cj)),
    ]
    args = [src, src, src, w]
    if xin is not None:
        in_specs.append(pl.BlockSpec((tm, cw), lambda cj, i, c0=xin[1]: (i, c0 + cj)))
        args.append(xin[0])
    for arr, kind, width, c0, cmul in extras:
        if kind == 'row':
            in_specs.append(pl.BlockSpec((tm, width), lambda cj, i, c0=c0, cmul=cmul: (i, c0 + cmul * cj)))
        else:
            in_specs.append(pl.BlockSpec((arr.shape[0], width), lambda cj, i, c0=c0, cmul=cmul: (0, c0 + cmul * cj)))
        args.append(arr)
    aliases = {}
    if passthrough is not None:
        in_specs.append(pl.BlockSpec(memory_space=pl.ANY))
        aliases = {len(args): passthrough[1]}
        args.append(passthrough[0])
    out_shape = []
    out_specs = []
    for n_total, dtype, width, c0, cmul in row_outs:
        out_shape.append(jax.ShapeDtypeStruct((M, n_total), dtype))
        out_specs.append(pl.BlockSpec((tm, width), lambda cj, i, c0=c0, cmul=cmul: (i, c0 + cmul * cj)))
    for n_total, width, c0, cmul in col_outs:
        out_shape.append(jax.ShapeDtypeStruct((SUBLANES, n_total), F32))
        out_specs.append(pl.BlockSpec((SUBLANES, width), lambda cj, i, c0=c0, cmul=cmul: (0, c0 + cmul * cj)))
    if xin is not None:
        out_shape.append(jax.ShapeDtypeStruct((Kp, SUBLANES, cw * nc), F32))
        out_specs.append(pl.BlockSpec((Kp, SUBLANES, cw), lambda cj, i: (0, 0, cj)))
    return pl.pallas_call(
        body, name=name, grid=(nc, n_i), in_specs=in_specs, out_specs=out_specs,
        out_shape=out_shape, input_output_aliases=aliases,
        scratch_shapes=[pltpu.VMEM((n_copies, tm + 2 * HALO, cw), F32), pltpu.VMEM((tm, cw), F32)],
        compiler_params=_params(dimension_semantics=("arbitrary", "arbitrary")),
    )(*args)


def _split_dot(m_bf16, v, n_pass, dims=None):
    out = None
    rest = v
    for p in range(n_pass):
        piece = rest.astype(BF16)
        if p + 1 < n_pass:
            rest = rest - piece.astype(F32)
        if dims is None:
            t = jnp.dot(m_bf16, piece, preferred_element_type=F32)
        else:
            t = lax.dot_general(m_bf16, piece, dims, preferred_element_type=F32)
        out = t if out is None else out + t
    return out


def _split_dot_r(v, m_bf16, n_pass):
    out = None
    rest = v
    for p in range(n_pass):
        piece = rest.astype(BF16)
        if p + 1 < n_pass:
            rest = rest - piece.astype(F32)
        t = jnp.dot(piece, m_bf16, preferred_element_type=F32)
        out = t if out is None else out + t
    return out


def _softplus(x):
    return jnp.maximum(x, 0.0) + jnp.log1p(jnp.exp(-jnp.abs(x)))


NT_DIMS = (((1,), (1,)), ((), ()))
TN_DIMS = (((0,), (0,)), ((), ()))


def _ssd_common(dtraw, dtbT, alogT, rev, n_heads):
    L = CHUNK
    if rev:
        dtraw = pltpu.roll(dtraw, LANES - n_heads, 1)
    preT = dtraw.T + dtbT
    dtT = _softplus(preT)
    AT = -jnp.exp(alogT)
    aT = dtT * AT
    ri = lax.broadcasted_iota(jnp.int32, (L, L), 0)
    ci = lax.broadcasted_iota(jnp.int32, (L, L), 1)
    up = (ri >= ci) if rev else (ri <= ci)
    lo = (ri <= ci) if rev else (ri >= ci)
    csT = _split_dot_r(aT, up.astype(BF16), 3)
    last = 0 if rev else L - 1
    lastB = jnp.broadcast_to(csT[:, last:last + 1], (L, L))
    return dict(preT=preT, dtT=dtT, AT=AT, csT=csT, cs=csT.T, up=up, lo=lo, ci=ci, last=last,
                doutT=jnp.exp(csT), dstT=jnp.exp(lastB - csT), totB=jnp.exp(lastB))


def ssd_fwd(name, xbc, dtraw, dtbT, alogT, *, S, DI, G, H, rev, tail=None):
    NC = S // CHUNK
    R = H // G
    GW = R * HEAD_DIM
    N = D_STATE
    XBC = xbc.shape[1]
    P = HEAD_DIM

    def body(*refs):
        xbc_ref, dtraw_ref, dtb_ref, alog_ref = refs[:4]
        if tail is None:
            y_ref, st_ref, h_ref = refs[4:]
        else:
            yo_ref, z_ref, dsk_ref, ng_ref = refs[4:8]
            y_ref, st_ref, yn_ref, h_ref = refs[8:]
        c = pl.program_id(0)

        @pl.when(c == 0)
        def _():
            h_ref[...] = jnp.zeros_like(h_ref)

        q = _ssd_common(dtraw_ref[...], dtb_ref[...], alog_ref[...], rev, H)
        cs, csT, dtT, doutT, totB = q['cs'], q['csT'], q['dtT'], q['doutT'], q['totB']
        wstT = q['dstT'] * dtT
        for g in range(G):
            Bg = xbc_ref[:, DI + g * N:DI + (g + 1) * N].astype(BF16)
            Cg = xbc_ref[:, DI + G * N + g * N:DI + G * N + (g + 1) * N].astype(BF16)
            CBT = lax.dot_general(Bg, Cg, NT_DIMS, preferred_element_type=F32)
            HT = h_ref[g]
            yoffT = lax.dot_general(HT.astype(BF16), Cg, NT_DIMS, preferred_element_type=F32)
            xT = xbc_ref[:, g * GW:(g + 1) * GW].T
            hs = [g * R + r for r in range(R)]
            blks = [slice(r * P, (r + 1) * P) for r in range(R)]
            segs = [jnp.where(q['up'], csT[h:h + 1, :] - cs[:, h:h + 1], -1e30) for h in hs]
            GTs = [(CBT * jnp.exp(sg)).astype(BF16) for sg in segs]
            xThs = [xT[b, :] for b in blks]
            XThs = [(xTh * dtT[h:h + 1, :]).astype(BF16) for xTh, h in zip(xThs, hs)]
            ydTs = [jnp.dot(a, GT, preferred_element_type=F32) for a, GT in zip(XThs, GTs)]
            ys = [ydT + yoffT[b, :] * doutT[h:h + 1, :] for ydT, b, h in zip(ydTs, blks, hs)]
            xws = [xTh * wstT[h:h + 1, :] for xTh, h in zip(xThs, hs)]
            tots = [jnp.broadcast_to(totB[h:h + 1, :], (P, N)) for h in hs]
            y_ref[:, g * GW:(g + 1) * GW] = jnp.concatenate(ys, axis=0).T
            xwT = jnp.concatenate(xws, axis=0).astype(BF16)
            ST = jnp.dot(xwT, Bg, preferred_element_type=F32)
            st_ref[0, g] = HT
            h_ref[g] = HT * jnp.concatenate(tots, axis=0) + ST
        if tail is not None:
            y = y_ref[...] + yo_ref[...]
            y_ref[...] = y
            z = _f32(z_ref[...])
            yz = (y + xbc_ref[:, 0:DI] * dsk_ref[...]) * (z * _sig(z))
            for g in range(G):
                t = yz[:, g * GW:(g + 1) * GW]
                tn = t * lax.rsqrt(jnp.mean(t * t, axis=-1, keepdims=True) + RMS_EPS)
                yn_ref[:, g * GW:(g + 1) * GW] = (tn * ng_ref[:, g * GW:(g + 1) * GW]).astype(BF16)

    cidx = (lambda c: NC - 1 - c) if rev else (lambda c: c)
    cmap = lambda c: (cidx(c), 0)
    smap = lambda c: (cidx(c), 0, 0, 0)
    const = lambda c: (0, 0)
    in_specs = [pl.BlockSpec((CHUNK, XBC), cmap), pl.BlockSpec((CHUNK, LANES), cmap),
                pl.BlockSpec((LANES, LANES), const), pl.BlockSpec((LANES, LANES), const)]
    out_specs = [pl.BlockSpec((CHUNK, DI), cmap), pl.BlockSpec((1, G, GW, N), smap)]
    out_shape = [jax.ShapeDtypeStruct((S, DI), F32), jax.ShapeDtypeStruct((NC, G, GW, N), F32)]
    args = [xbc, dtraw, dtbT, alogT]
    if tail is not None:
        y_other, (z_arr, z_blk), dsk, ng = tail
        in_specs += [pl.BlockSpec((CHUNK, DI), cmap), pl.BlockSpec((CHUNK, DI), lambda c: (cidx(c), z_blk)),
                     pl.BlockSpec((1, DI), const), pl.BlockSpec((1, DI), const)]
        out_specs.append(pl.BlockSpec((CHUNK, DI), cmap))
        out_shape.append(jax.ShapeDtypeStruct((S, DI), BF16))
        args += [y_other, z_arr, dsk, ng]
    return pl.pallas_call(
        body, name=name, grid=(NC,), in_specs=in_specs, out_specs=out_specs, out_shape=out_shape,
        scratch_shapes=[pltpu.VMEM((G, GW, N), F32)],
        compiler_params=_params(dimension_semantics=("arbitrary",)),
    )(*args)


def ssd_bwd(name, xbc, dtraw, dy, st, dtbT, alogT, *, S, DI, G, H, rev, tail=None):
    NC = S // CHUNK
    R = H // G
    GW = R * HEAD_DIM
    N = D_STATE
    XBC = xbc.shape[1]
    P = HEAD_DIM
    L = CHUNK

    def body(*refs):
        xbc_ref, dtraw_ref, dy_ref, st_ref, dtb_ref, alog_ref = refs[:6]
        if tail is None:
            dxbc_ref, ddt_ref, da_ref, dh_ref, dcst_ref, p2t_ref, p3t_ref, e2t_ref = refs[6:]
        else:
            other_ref, cbv_ref, dsk_ref = refs[6:9]
            dxbc_ref, ddt_ref, da_ref, dcol_ref, dh_ref, dcst_ref, p2t_ref, p3t_ref, e2t_ref = refs[9:]
        c = pl.program_id(0)

        @pl.when(c == 0)
        def _():
            dh_ref[...] = jnp.zeros_like(dh_ref)
            da_ref[...] = jnp.zeros_like(da_ref)
            dcst_ref[...] = jnp.zeros_like(dcst_ref)
            p2t_ref[...] = jnp.zeros_like(p2t_ref)
            p3t_ref[...] = jnp.zeros_like(p3t_ref)
            e2t_ref[...] = jnp.zeros_like(e2t_ref)

        q = _ssd_common(dtraw_ref[...], dtb_ref[...], alog_ref[...], rev, H)
        cs, csT, dtT, doutT, dstT, totB = q['cs'], q['csT'], q['dtT'], q['doutT'], q['dstT'], q['totB']
        wstT = dstT * dtT
        lane = q['ci']
        dcs_c = jnp.zeros((L, LANES), F32)
        for g in range(G):
            Bg = xbc_ref[:, DI + g * N:DI + (g + 1) * N].astype(BF16)
            Cg = xbc_ref[:, DI + G * N + g * N:DI + G * N + (g + 1) * N].astype(BF16)
            CB = lax.dot_general(Cg, Bg, NT_DIMS, preferred_element_type=F32)
            HpT = st_ref[0, g]
            HpTb = HpT.astype(BF16)
            dHT = dh_ref[g]
            dHTb = dHT.astype(BF16)
            BdHT = lax.dot_general(dHTb, Bg, NT_DIMS, preferred_element_type=F32)
            yoffT = lax.dot_general(HpTb, Cg, NT_DIMS, preferred_element_type=F32)
            xT = xbc_ref[:, g * GW:(g + 1) * GW].T
            dyT = dy_ref[:, g * GW:(g + 1) * GW].T
            hs = [g * R + r for r in range(R)]
            blks = [slice(r * P, (r + 1) * P) for r in range(R)]
            Lms = [jnp.exp(jnp.where(q['lo'], cs[:, h:h + 1] - csT[h:h + 1, :], -1e30)) for h in hs]
            xThs = [xT[b, :] for b in blks]
            dyThs = [dyT[b, :] for b in blks]
            xThbs = [v.astype(BF16) for v in xThs]
            dyThbs = [v.astype(BF16) for v in dyThs]
            dGxs = [lax.dot_general(a, b, TN_DIMS, preferred_element_type=F32) for a, b in zip(dyThbs, xThbs)]
            Gms = [(CB * Lm).astype(BF16) for Lm in Lms]
            u1Ts = [jnp.dot(a, Gm, preferred_element_type=F32) for a, Gm in zip(dyThbs, Gms)]
            Ts = [dGx * (Lm * dtT[h:h + 1, :]) for dGx, Lm, h in zip(dGxs, Lms, hs)]
            dCB = Ts[0]
            for T in Ts[1:]:
                dCB = dCB + T
            Msegs = [T * CB for T in Ts]
            for h, Mseg in zip(hs, Msegs):
                dcs_c = jnp.where(lane == h, jnp.sum(Mseg, axis=1, keepdims=True), dcs_c)
            uTs = [u1T + BdHT[b, :] * dstT[h:h + 1, :] for u1T, b, h in zip(u1Ts, blks, hs)]
            dyds = [dyTh * doutT[h:h + 1, :] for dyTh, h in zip(dyThs, hs)]
            xws = [xTh * wstT[h:h + 1, :] for xTh, h in zip(xThs, hs)]
            for r, h in enumerate(hs):
                b = blks[r]
                p3row = jnp.sum(xws[r] * BdHT[b, :], axis=0, keepdims=True)
                dcst_ref[h:h + 1, :] = (jnp.sum(dyds[r] * yoffT[b, :], axis=0, keepdims=True)
                                        - jnp.sum(Msegs[r], axis=0, keepdims=True) - p3row)
                p2t_ref[h:h + 1, :] = jnp.sum(xThs[r] * uTs[r], axis=0, keepdims=True)
                p3t_ref[h:h + 1, :] = p3row
                e2t_ref[h:h + 1, :] = jnp.sum(HpT[b, :] * dHT[b, :], axis=0, keepdims=True)
            dxs = [uT * dtT[h:h + 1, :] for uT, h in zip(uTs, hs)]
            tots = [jnp.broadcast_to(totB[h:h + 1, :], (P, N)) for h in hs]
            dxbc_ref[:, g * GW:(g + 1) * GW] = jnp.concatenate(dxs, axis=0).T
            dydT = jnp.concatenate(dyds, axis=0).astype(BF16)
            xwT = jnp.concatenate(xws, axis=0).astype(BF16)
            dCBb = dCB.astype(BF16)
            dC = (jnp.dot(dCBb, Bg, preferred_element_type=F32)
                  + lax.dot_general(dydT, HpTb, TN_DIMS, preferred_element_type=F32))
            dB = (lax.dot_general(dCBb, Cg, TN_DIMS, preferred_element_type=F32)
                  + lax.dot_general(xwT, dHTb, TN_DIMS, preferred_element_type=F32))
            dxbc_ref[:, DI + g * N:DI + (g + 1) * N] = dB
            dxbc_ref[:, DI + G * N + g * N:DI + G * N + (g + 1) * N] = dC
            dh_ref[g] = (dHT * jnp.concatenate(tots, axis=0)
                         + jnp.dot(dydT, Cg, preferred_element_type=F32))
        e1 = jnp.sum(p3t_ref[...], axis=1, keepdims=True)
        e2 = jnp.sum(e2t_ref[...], axis=1, keepdims=True)
        dcsT = (dcst_ref[...] + dcs_c.T
                + jnp.where(lane == q['last'], e1 + totB * e2, 0.0))
        daT = _split_dot_r(dcsT, q['lo'].astype(BF16), 3)
        ddtT = daT * q['AT'] + p2t_ref[...]
        da_ref[...] += daT * dtT
        ddraw = jnp.where(lane < H, (ddtT * _sig(q['preT'])).T, 0.0)
        if rev:
            ddraw = pltpu.roll(ddraw, H, 1)
        ddt_ref[...] = ddraw
        if tail is not None:
            for c0 in range(0, XBC, DI):
                d = dxbc_ref[:, c0:c0 + DI] + other_ref[:, c0:c0 + DI]
                if c0 == 0:
                    d = d + dy_ref[...] * dsk_ref[...]
                cb = cbv_ref[:, c0:c0 + DI]
                dcb = d * _dsilu(cb, _sig(cb))
                dxbc_ref[:, c0:c0 + DI] = dcb
                part = _rows8(dcb)

                @pl.when(c == 0)
                def _():
                    dcol_ref[:, c0:c0 + DI] = part

                @pl.when(c > 0)
                def _():
                    dcol_ref[:, c0:c0 + DI] += part

    cmap = (lambda c: (c, 0)) if rev else (lambda c: (NC - 1 - c, 0))
    smap = (lambda c: (c, 0, 0, 0)) if rev else (lambda c: (NC - 1 - c, 0, 0, 0))
    const = lambda c: (0, 0)
    sq = pltpu.VMEM((LANES, CHUNK), F32)
    in_specs = [pl.BlockSpec((CHUNK, XBC), cmap), pl.BlockSpec((CHUNK, LANES), cmap),
                pl.BlockSpec((CHUNK, DI), cmap),
                pl.BlockSpec((1, G, GW, N), smap),
                pl.BlockSpec((LANES, LANES), const), pl.BlockSpec((LANES, LANES), const)]
    out_specs = [pl.BlockSpec((CHUNK, XBC), cmap), pl.BlockSpec((CHUNK, LANES), cmap),
                 pl.BlockSpec((LANES, LANES), const)]
    out_shape = [jax.ShapeDtypeStruct((S, XBC), F32), jax.ShapeDtypeStruct((S, LANES), F32),
                 jax.ShapeDtypeStruct((LANES, LANES), F32)]
    args = [xbc, dtraw, dy, st, dtbT, alogT]
    if tail is not None:
        in_specs += [pl.BlockSpec((CHUNK, XBC), cmap), pl.BlockSpec((CHUNK, XBC), cmap),
                     pl.BlockSpec((1, DI), const)]
        out_specs.append(pl.BlockSpec((SUBLANES, XBC), const))
        out_shape.append(jax.ShapeDtypeStruct((SUBLANES, XBC), F32))
        args += list(tail)
    return pl.pallas_call(
        body, name=name, grid=(NC,), in_specs=in_specs, out_specs=out_specs, out_shape=out_shape,
        scratch_shapes=[pltpu.VMEM((G, GW, N), F32), sq, sq, sq, sq],
        compiler_params=_params(dimension_semantics=("arbitrary",)),
    )(*args)


ANY = pl.BlockSpec(memory_space=pl.ANY)


def chip_exchange(name, groups, gather):
    flat = [arr for grp in groups for arr in grp]
    n_in = len(flat)
    n_out = len(groups)
    n_rc = 3 * n_in

    def body(*refs):
        in_refs = refs[:n_in]
        out_refs = refs[n_in:n_in + n_out]
        send, recv, loc = refs[n_in + n_out:]
        x, y, c = lax.axis_index("x"), lax.axis_index("y"), lax.axis_index("c")
        me = 2 * x + y
        peers = [(1 - x, y), (x, 1 - y), (1 - x, 1 - y)]
        local, remote = [], []
        q = 0
        for a, grp in enumerate(groups):
            for l in range(len(grp)):
                src = in_refs[q]
                dst = out_refs[a].at[me] if gather else out_refs[a].at[me, l]
                own = src if gather else src.at[me]
                lc = pltpu.make_async_copy(own, dst, loc.at[q])
                lc.start()
                local.append(lc)
                for j, (px, py) in enumerate(peers):
                    blk = src if gather else src.at[2 * px + py]
                    rc = pltpu.make_async_remote_copy(
                        src_ref=blk, dst_ref=dst, send_sem=send.at[3 * q + j], recv_sem=recv.at[3 * q + j],
                        device_id=(px, py, c), device_id_type=MESH)
                    rc.start()
                    remote.append(rc)
                q += 1
        for lc in local:
            lc.wait()
        for rc in remote:
            rc.wait()

    out_shape = []
    for grp in groups:
        a0 = grp[0]
        if gather:
            out_shape.append(jax.ShapeDtypeStruct((4,) + a0.shape, a0.dtype))
        else:
            out_shape.append(jax.ShapeDtypeStruct((4, len(grp)) + a0.shape[1:], a0.dtype))
    return pl.pallas_call(
        body, name=name, in_specs=[ANY] * n_in, out_specs=[ANY] * n_out, out_shape=out_shape,
        scratch_shapes=[pltpu.SemaphoreType.DMA((n_rc,)), pltpu.SemaphoreType.DMA((n_rc,)),
                        pltpu.SemaphoreType.DMA((n_in,))],
    )(*flat)


def gather_layer(name, split, whole):
    ns, nw = len(split), len(whole)
    n = ns + nw
    n_rc = 3 * (n + ns)

    def body(*refs):
        in_refs = refs[:n]
        out_refs = refs[n:2 * n]
        send, recv, loc = refs[2 * n:]
        x, y, c = lax.axis_index("x"), lax.axis_index("y"), lax.axis_index("c")
        me = 2 * x + y
        sibling = (x, y, 1 - c)
        peers = [(1 - x, y), (x, 1 - y), (1 - x, 1 - y)]

        def region(a, chip, half):
            if a >= ns:
                return out_refs[a].at[chip]
            hr = split[a].shape[0] // 2
            return out_refs[a].at[chip, pl.ds(half * hr, hr)]

        def mine(a):
            if a >= ns:
                return in_refs[a]
            hr = split[a].shape[0] // 2
            return in_refs[a].at[pl.ds(c * hr, hr)]

        local = []
        for a in range(n):
            lc = pltpu.make_async_copy(in_refs[a], out_refs[a].at[me], loc.at[a])
            lc.start()
            local.append(lc)
        sends = []
        for a in range(n):
            for j, (px, py) in enumerate(peers):
                rc = pltpu.make_async_remote_copy(
                    src_ref=mine(a), dst_ref=region(a, me, c), send_sem=send.at[3 * a + j],
                    recv_sem=recv.at[3 * a + j], device_id=(px, py, c), device_id_type=MESH)
                rc.start()
                sends.append(rc)
        for a in range(n):
            for j, (px, py) in enumerate(peers):
                chip = 2 * px + py
                landed = pltpu.make_async_remote_copy(
                    src_ref=mine(a), dst_ref=region(a, chip, c), send_sem=send.at[3 * a + j],
                    recv_sem=recv.at[3 * a + j], device_id=(px, py, c), device_id_type=MESH)
                landed.wait_recv()
                if a < ns:
                    fw = pltpu.make_async_remote_copy(
                        src_ref=region(a, chip, c), dst_ref=region(a, chip, c), send_sem=send.at[3 * n + 3 * a + j],
                        recv_sem=recv.at[3 * n + 3 * a + j], device_id=sibling, device_id_type=MESH)
                    fw.start()
                    sends.append(fw)
        for a in range(ns):
            for j, (px, py) in enumerate(peers):
                chip = 2 * px + py
                pltpu.make_async_remote_copy(
                    src_ref=region(a, chip, 1 - c), dst_ref=region(a, chip, 1 - c), send_sem=send.at[3 * n + 3 * a + j],
                    recv_sem=recv.at[3 * n + 3 * a + j], device_id=sibling, device_id_type=MESH).wait_recv()
        for rc in sends:
            rc.wait_send()
        for lc in local:
            lc.wait()

    arrs = list(split) + list(whole)
    return pl.pallas_call(
        body, name=name, in_specs=[ANY] * n, out_specs=[ANY] * n,
        out_shape=[jax.ShapeDtypeStruct((4,) + a.shape, a.dtype) for a in arrs],
        scratch_shapes=[pltpu.SemaphoreType.DMA((n_rc,)), pltpu.SemaphoreType.DMA((n_rc,)),
                        pltpu.SemaphoreType.DMA((n,))],
    )(*arrs)


def core_send_half(name, arrs):
    n = len(arrs)

    def body(*refs):
        in_refs = refs[:n]
        out_refs = refs[n:2 * n]
        send, recv = refs[2 * n:]
        c = lax.axis_index("c")
        peer = (lax.axis_index("x"), lax.axis_index("y"), 1 - c)
        rcs = []
        for a in range(n):
            hr = arrs[a].shape[1] // 2
            rc = pltpu.make_async_remote_copy(
                src_ref=in_refs[a].at[:, pl.ds((1 - c) * hr, hr)], dst_ref=out_refs[a], send_sem=send.at[a],
                recv_sem=recv.at[a], device_id=peer, device_id_type=MESH)
            rc.start()
            rcs.append(rc)
        for rc in rcs:
            rc.wait()

    return pl.pallas_call(
        body, name=name, in_specs=[ANY] * n, out_specs=[ANY] * n,
        out_shape=[jax.ShapeDtypeStruct((4, a.shape[1] // 2, a.shape[2]), a.dtype) for a in arrs],
        scratch_shapes=[pltpu.SemaphoreType.DMA((n,)), pltpu.SemaphoreType.DMA((n,))],
    )(*arrs)


def core_fill(name, arrs, layer, n_layers):
    n = len(arrs)

    def body(*refs):
        out_refs = refs[n:2 * n]
        send, recv = refs[2 * n:]
        c = lax.axis_index("c")
        peer = (lax.axis_index("x"), lax.axis_index("y"), 1 - c)
        rcs = []
        for a in range(n):
            r = arrs[a].shape[0] // n_layers
            hr = r // 2
            rows = out_refs[a].at[pl.ds(layer * r + c * hr, hr)]
            rc = pltpu.make_async_remote_copy(src_ref=rows, dst_ref=rows, send_sem=send.at[a], recv_sem=recv.at[a],
                                              device_id=peer, device_id_type=MESH)
            rc.start()
            rcs.append(rc)
        for a in range(n):
            r = arrs[a].shape[0] // n_layers
            hr = r // 2
            theirs = out_refs[a].at[pl.ds(layer * r + (1 - c) * hr, hr)]
            pltpu.make_async_remote_copy(src_ref=theirs, dst_ref=theirs, send_sem=send.at[a], recv_sem=recv.at[a],
                                         device_id=peer, device_id_type=MESH).wait_recv()
        for rc in rcs:
            rc.wait_send()

    return pl.pallas_call(
        body, name=name, in_specs=[ANY] * n, out_specs=[ANY] * n,
        out_shape=[jax.ShapeDtypeStruct(a.shape, a.dtype) for a in arrs],
        input_output_aliases={a: a for a in range(n)},
        scratch_shapes=[pltpu.SemaphoreType.DMA((n,)), pltpu.SemaphoreType.DMA((n,))],
    )(*arrs)


def sibling_swap(name, arrs):
    n = len(arrs)

    def body(*refs):
        in_refs = refs[:n]
        out_refs = refs[n:2 * n]
        send, recv = refs[2 * n:]
        peer = (lax.axis_index("x"), lax.axis_index("y"), 1 - lax.axis_index("c"))
        rcs = []
        for a in range(n):
            rc = pltpu.make_async_remote_copy(src_ref=in_refs[a], dst_ref=out_refs[a], send_sem=send.at[a],
                                              recv_sem=recv.at[a], device_id=peer, device_id_type=MESH)
            rc.start()
            rcs.append(rc)
        for rc in rcs:
            rc.wait()

    return pl.pallas_call(
        body, name=name, in_specs=[ANY] * n, out_specs=[ANY] * n,
        out_shape=[jax.ShapeDtypeStruct(a.shape, a.dtype) for a in arrs],
        scratch_shapes=[pltpu.SemaphoreType.DMA((n,)), pltpu.SemaphoreType.DMA((n,))],
    )(*arrs)


def all8_gather(name, v):
    flips = [(fx, fy, fc) for fx in (0, 1) for fy in (0, 1) for fc in (0, 1) if (fx, fy, fc) != (0, 0, 0)]

    def body(v_ref, out_ref, send, recv, loc):
        x, y, c = lax.axis_index("x"), lax.axis_index("y"), lax.axis_index("c")
        me = 4 * x + 2 * y + c
        lc = pltpu.make_async_copy(v_ref, out_ref.at[me], loc)
        lc.start()
        rcs = []
        for k, (fx, fy, fc) in enumerate(flips):
            tgt = (x + fx - 2 * x * fx, y + fy - 2 * y * fy, c + fc - 2 * c * fc)
            rc = pltpu.make_async_remote_copy(src_ref=v_ref, dst_ref=out_ref.at[me], send_sem=send.at[k],
                                              recv_sem=recv.at[k], device_id=tgt, device_id_type=MESH)
            rc.start()
            rcs.append(rc)
        lc.wait()
        for rc in rcs:
            rc.wait()

    return pl.pallas_call(
        body, name=name, in_specs=[ANY], out_specs=ANY,
        out_shape=jax.ShapeDtypeStruct((8,) + v.shape, v.dtype),
        scratch_shapes=[pltpu.SemaphoreType.DMA((7,)), pltpu.SemaphoreType.DMA((7,)), pltpu.SemaphoreType.DMA],
    )(v)


def _pick_rows(rows, cols, target_elems=128 * 1024):
    if rows % SUBLANES != 0:
        return rows
    best = SUBLANES
    t = SUBLANES
    while t <= rows:
        if rows % t == 0 and t * cols <= target_elems:
            best = t
        t += SUBLANES
    return best


def sum_chips(name, parts):
    _, R, C = parts.shape
    tm = _pick_rows(R, C)

    def body(p_ref, o_ref):
        o_ref[...] = (p_ref[0] + p_ref[1]) + (p_ref[2] + p_ref[3])

    return pl.pallas_call(
        body, name=name, grid=(R // tm,),
        in_specs=[pl.BlockSpec((4, tm, C), lambda i: (0, i, 0))],
        out_specs=pl.BlockSpec((tm, C), lambda i: (i, 0)),
        out_shape=jax.ShapeDtypeStruct((R, C), F32),
        compiler_params=_params(dimension_semantics=("arbitrary",)),
    )(parts)


def _adamw(g, w, m, v):
    m = ADAM_B1 * m + (1.0 - ADAM_B1) * g
    v = ADAM_B2 * v + (1.0 - ADAM_B2) * (g * g)
    m_hat = m / (1.0 - ADAM_B1 ** ADAM_STEP)
    v_hat = v / (1.0 - ADAM_B2 ** ADAM_STEP)
    delta = -ADAM_LR * (m_hat / (jnp.sqrt(v_hat) + ADAM_EPS) + ADAM_WD * w)
    return delta, m, v


def adamw_shard(name, s_mine, s_sib, w, m, v):
    R, C = w.shape
    tm = _pick_rows(R, C)

    def body(a_ref, b_ref, w_ref, m_ref, v_ref, g_out, d_out, m_out, v_out):
        g = a_ref[...] + b_ref[...]
        d, mn, vn = _adamw(g, w_ref[...], m_ref[...], v_ref[...])
        g_out[...] = g
        d_out[...] = d
        m_out[...] = mn
        v_out[...] = vn

    spec = pl.BlockSpec((tm, C), lambda i: (i, 0))
    return pl.pallas_call(
        body, name=name, grid=(R // tm,), in_specs=[spec] * 5, out_specs=[spec] * 4,
        out_shape=[jax.ShapeDtypeStruct((R, C), F32)] * 4,
        compiler_params=_params(dimension_semantics=("arbitrary",)),
    )(s_mine, s_sib, w, m, v)


def core_sum(name, core, g, got):
    _, r, C = g.shape
    hr = r // 2
    tm = _pick_rows(hr, 4 * C)
    nh = hr // tm

    def body(c_ref, g_ref, s_ref, o_ref):
        o_ref[...] = g_ref[...] + s_ref[...]

    return pl.pallas_call(
        body, name=name,
        grid_spec=pltpu.PrefetchScalarGridSpec(
            num_scalar_prefetch=1, grid=(nh,),
            in_specs=[pl.BlockSpec((4, tm, C), lambda i, cr: (0, cr[0] * nh + i, 0)),
                      pl.BlockSpec((4, tm, C), lambda i, cr: (0, i, 0))],
            out_specs=pl.BlockSpec((4, tm, C), lambda i, cr: (0, i, 0))),
        out_shape=jax.ShapeDtypeStruct((4, hr, C), F32),
        compiler_params=_params(dimension_semantics=("arbitrary",)),
    )(core, g, got)


def chip_sum_into(name, core, parts, layer, n_layers, into=None):
    _, hr, C = parts.shape
    r = 2 * hr
    tm = _pick_rows(hr, 4 * C)
    nh = hr // tm

    def body(c_ref, p_ref, *rest):
        o_ref = rest[-1]
        o_ref[...] = (p_ref[0] + p_ref[1]) + (p_ref[2] + p_ref[3])

    in_specs = [pl.BlockSpec((4, tm, C), lambda i, cr: (0, i, 0))]
    args = [core, parts]
    aliases = {}
    if into is not None:
        in_specs.append(pl.BlockSpec(memory_space=pl.ANY))
        args.append(into)
        aliases = {2: 0}
    return pl.pallas_call(
        body, name=name,
        grid_spec=pltpu.PrefetchScalarGridSpec(
            num_scalar_prefetch=1, grid=(nh,), in_specs=in_specs,
            out_specs=pl.BlockSpec((tm, C), lambda i, cr: ((layer * r) // tm + cr[0] * nh + i, 0))),
        out_shape=jax.ShapeDtypeStruct((n_layers * r, C), F32), input_output_aliases=aliases,
        compiler_params=_params(dimension_semantics=("arbitrary",)),
    )(*args)


def adamw_full(name, g, w, m, v):
    R, C = w.shape
    tm = _pick_rows(R, C)

    def body(g_ref, w_ref, m_ref, v_ref, d_out, m_out, v_out):
        d, mn, vn = _adamw(g_ref[...], w_ref[...], m_ref[...], v_ref[...])
        d_out[...] = d
        m_out[...] = mn
        v_out[...] = vn

    spec = pl.BlockSpec((tm, C), lambda i: (i, 0))
    return pl.pallas_call(
        body, name=name, grid=(R // tm,), in_specs=[spec] * 4, out_specs=[spec] * 3,
        out_shape=[jax.ShapeDtypeStruct((R, C), F32)] * 3,
        compiler_params=_params(dimension_semantics=("arbitrary",)),
    )(g, w, m, v)


def adamw_small(name, parts, w, m, v):
    W = w.shape[1]

    def body(p_ref, w_ref, m_ref, v_ref, g_out, d_out, m_out, v_out):
        acc = p_ref[0]
        for k in range(1, 8):
            acc = acc + p_ref[k]
        g = jnp.sum(acc, axis=0, keepdims=True)
        d, mn, vn = _adamw(g, w_ref[...], m_ref[...], v_ref[...])
        g_out[...] = g
        d_out[...] = d
        m_out[...] = mn
        v_out[...] = vn

    return pl.pallas_call(
        body, name=name, out_shape=[jax.ShapeDtypeStruct((1, W), F32)] * 4,
        compiler_params=_params(),
    )(parts, w, m, v)


def _pad_lanes(v, width=LANES):
    return jnp.pad(v, ((0, 0), (0, width - v.shape[1])))


def _layer_fwd(cf, x, xb, pb, W, sm):
    S, D, CD, DI, XBC, F, H, G = cf['S'], cf['D'], cf['CD'], cf['DI'], cf['XBC'], cf['F'], cf['H'], cf['G']
    NM = cf['NM']
    alpha = cf['alpha']
    tm = cf['tm']
    tmx = cf['tmx']
    tn_in = cf['tn_in']
    sv = {}

    ident = lambda acc, ex: ([acc], [])
    proj, = fused_mm("in_proj", [(xb, W['in_main'], 0, False)], [], ident, [(NM, BF16, tn_in, 0)],
                     M=S, tm=tmx, tn=tn_in, nj=NM // tn_in)
    dtraw, = fused_mm("dt_proj", [(xb, W['in_dt'], 0, False)], [], ident, [(LANES, F32, LANES, 0)],
                      M=S, tm=tmx, tn=LANES)

    u, = row_call("glu", lambda a, gt: ([a * _sig(gt)], []),
                  [(proj, 'row', CD, 0, 0), (proj, 'row', CD, 1, 0)], [(CD, F32, CD, 0, 0)], M=S, tm=tm)

    def conv_a_epi(conv, ex):
        cb_, g_, b_ = ex
        ca = conv + cb_
        xhat, _ = _ln_stats(ca)
        la = xhat * g_ + b_
        return [ca, la * _sig(la)], []

    ca, sa = conv_call("conv_a", u, 0, sm['conv_a_w'], cf['KA'], conv_a_epi,
                       [(sm['conv_a_b'], 'vec', CD, 0, 0), (sm['ln_a_g'], 'vec', CD, 0, 0), (sm['ln_a_b'], 'vec', CD, 0, 0)],
                       [(CD, F32, CD, 0, 0), (CD, BF16, CD, 0, 0)], M=S, tm=cf['tmc'], cw=CD, nc=1, reverse=False)
    y_a, = fused_mm("a_out", [(sa, W['a_out'], 0, False)], [], ident, [(D, F32, D, 0)], M=S, tm=tmx, tn=D)

    def conv_b_epi(conv, ex):
        cb = conv + ex[0]
        return [cb, cb * _sig(cb)], []

    xoff = (2 * CD + 2 * D + DI) // DI
    cbv, xbc = conv_call("conv_b", proj, xoff, sm['ssm_conv_w'], cf['KB'], conv_b_epi,
                         [(sm['ssm_conv_b'], 'vec', DI, 0, 1)],
                         [(XBC, F32, DI, 0, 1), (XBC, F32, DI, 0, 1)], M=S, tm=cf['tmc'], cw=DI, nc=XBC // DI,
                         reverse=False)
    y_f, st_f = ssd_fwd("ssd_fwd_f", xbc, dtraw, sm['dtb_f'], sm['alog_f'], S=S, DI=DI, G=G, H=H, rev=False)
    zoff = (2 * CD + 2 * D) // DI
    ysum, st_r, yn = ssd_fwd("ssd_fwd_r", xbc, dtraw, sm['dtb_r'], sm['alog_r'], S=S, DI=DI, G=G, H=H, rev=True,
                             tail=(y_f, (proj, zoff), sm['dskip_full'], sm['ssm_norm_g']))
    goff = (2 * CD) // D

    def merge_epi(acc, ex):
        ga, gb, ya = ex
        return [acc, _sig(ga) * ya + _sig(gb) * acc], []

    y_b, merged = fused_mm("b_out", [(yn, W['b_out'], 0, False)],
                           [(proj, 'row', D, goff), (proj, 'row', D, goff + 1), (y_a, 'row', D, 0)],
                           merge_epi, [(D, F32, D, 0), (D, BF16, D, 0)], M=S, tm=tm, tn=D)

    def mix_epi(acc, ex):
        xin, g_, b_ = ex
        r1 = alpha * xin + acc
        xhat, _ = _ln_stats(r1)
        return [r1, xhat * g_ + b_], []

    r1, hb = fused_mm("o_mix", [(merged, W['o'], 0, False)],
                      [(x, 'row', D, 0), (sm['ln1_g'], 'vec', D, 0), (sm['ln1_b'], 'vec', D, 0)],
                      mix_epi, [(D, F32, D, 0), (D, BF16, D, 0)], M=S, tm=tm, tn=D)

    tnf = cf['tnf']

    g32, g_ = fused_mm("ffn_gate", [(hb, W['gate_up'], 0, False)], [], lambda acc, ex: ([acc, acc], []),
                       [(F, F32, tnf, 0), (F, BF16, tnf, 0)], M=S, tm=tmx, tn=tnf, nj=F // tnf)
    u_, f = fused_mm("ffn_up", [(hb, W['gate_up'], F // tnf, False)], [(g32, 'row', tnf, 0)],
                     lambda acc, ex: ([acc, ex[0] * _sig(ex[0]) * acc], []),
                     [(F, BF16, tnf, 0), (F, BF16, tnf, 0)], M=S, tm=tmx, tn=tnf, nj=F // tnf)

    def down_epi(acc, ex):
        r1_, g1, b1, g2, b2 = ex
        xh1, _ = _ln_stats(r1_)
        r2 = alpha * (xh1 * g1 + b1) + acc
        xh2, _ = _ln_stats(r2)
        return [r2, xh2 * g2 + b2], []

    r2, h2b = fused_mm("ffn_down", [(f, W['down'], 0, False)],
                       [(r1, 'row', D, 0), (sm['ln1_g'], 'vec', D, 0), (sm['ln1_b'], 'vec', D, 0),
                        (sm['ln2_g'], 'vec', D, 0), (sm['ln2_b'], 'vec', D, 0)],
                       down_epi, [(D, F32, D, 0), (D, BF16, D, 0)], M=S, tm=tm, tn=D)

    pe, = fused_mm("ple_proj", [(pb, W['ple'], 0, False)], [], ident, [(D, F32, D, 0)], M=S, tm=tmx, tn=D)

    def ple_epi(acc, ex):
        r2_, g2, b2, pe_, pg = ex
        xh2, _ = _ln_stats(r2_)
        h2 = xh2 * g2 + b2
        e = pe_ * lax.rsqrt(jnp.mean(pe_ * pe_, axis=-1, keepdims=True) + RMS_EPS) * pg
        xn = h2 + e * _sig(acc)
        return [acc, xn, xn], []

    t_, xn, xnb = fused_mm("ple_gate", [(h2b, W['ple_gate'], 0, False)],
                           [(r2, 'row', D, 0), (sm['ln2_g'], 'vec', D, 0), (sm['ln2_b'], 'vec', D, 0),
                            (pe, 'row', D, 0), (sm['ple_norm_g'], 'vec', D, 0)],
                           ple_epi, [(D, F32, D, 0), (D, F32, D, 0), (D, BF16, D, 0)], M=S, tm=tm, tn=D)
    sv.update(x=x, xb=xb, pb=pb, proj=proj, dtraw=dtraw, u=u, ca=ca, sa=sa, y_a=y_a, cbv=cbv, xbc=xbc,
              ysum=ysum, st_f=st_f, st_r=st_r, yn=yn, y_b=y_b, merged=merged, r1=r1, hb=hb,
              g_=g_, u_=u_, f=f, r2=r2, h2b=h2b, t_=t_, pe=pe)
    return xn, xnb, sv


def _layer_bwd(cf, sv, W, sm, dxn=None, target=None, xn=None):
    S, D, CD, DI, XBC, F, H, G = cf['S'], cf['D'], cf['CD'], cf['DI'], cf['XBC'], cf['F'], cf['H'], cf['G']
    NM = cf['NM']
    alpha = cf['alpha']
    tm = cf['tm']
    gw = cf['GW']
    out = {}

    def ple_bwd_core(dx_, t, pe_, pg):
        s = _sig(t)
        rinv = lax.rsqrt(jnp.mean(pe_ * pe_, axis=-1, keepdims=True) + RMS_EPS)
        pn = pe_ * rinv
        e = pn * pg
        dtg = dx_ * e * (s * (1.0 - s))
        de = dx_ * s
        qv = de * pg
        dpe = rinv * (qv - pn * jnp.mean(qv * pn, axis=-1, keepdims=True))
        return dtg, dpe, de * pn

    if dxn is None:
        def head(xn_, tgt, t, pe_, pg):
            err = xn_ - tgt
            dx_ = err * (1.0 / D)
            dtg, dpe, dpg = ple_bwd_core(dx_, t, pe_, pg)
            return [dx_, dtg, dpe], [dpg, err * err]

        (dxn, dtg, dpe, dpg, lsq) = row_call(
            "loss_ple_bwd", head,
            [(xn, 'row', D, 0, 0), (target, 'row', D, 0, 0), (sv['t_'], 'row', D, 0, 0), (sv['pe'], 'row', D, 0, 0),
             (sm['ple_norm_g'], 'vec', D, 0, 0)],
            [(D, F32, D, 0, 0), (D, BF16, D, 0, 0), (D, BF16, D, 0, 0)], [(D, D, 0, 0), (D, D, 0, 0)], M=S, tm=tm)
        out['loss_sq'] = lsq
    else:
        def mid(dx_, t, pe_, pg):
            dtg, dpe, dpg = ple_bwd_core(dx_, t, pe_, pg)
            return [dtg, dpe], [dpg]

        (dtg, dpe, dpg) = row_call(
            "ple_bwd", mid,
            [(dxn, 'row', D, 0, 0), (sv['t_'], 'row', D, 0, 0), (sv['pe'], 'row', D, 0, 0),
             (sm['ple_norm_g'], 'vec', D, 0, 0)],
            [(D, BF16, D, 0, 0), (D, BF16, D, 0, 0)], [(D, D, 0, 0)], M=S, tm=tm)
    out['ple_norm_g'] = dpg

    def ln_bwd_epi(scale):
        def epi(acc, ex):
            res, r_, g_ = ex
            dh = scale * res + acc
            xhat, rstd = _ln_stats(r_)
            dr = _ln_bwd(dh, xhat, rstd, g_)
            return [dr, dr], [dh * xhat, dh]
        return epi

    dr2, dr2b, dg2, db2 = fused_mm(
        "dh2", [(dtg, W['ple_gate_T'], 0, False)],
        [(dxn, 'row', D, 0), (sv['r2'], 'row', D, 0), (sm['ln2_g'], 'vec', D, 0)],
        ln_bwd_epi(1.0), [(D, F32, D, 0), (D, BF16, D, 0)], [(D, D, 0), (D, D, 0)], M=S, tm=tm, tn=D)
    out['ln2_g'], out['ln2_b'] = dg2, db2

    tnf = cf['tnf']

    def dswiglu_epi(acc, ex):
        gg, uu = ex
        s = _sig(gg)
        return [acc * uu * _dsilu(gg, s), acc * (gg * s)], []

    dg_b, du_b = fused_mm(
        "d_down", [(dr2b, W['down_T'], 0, False)],
        [(sv['g_'], 'row', tnf, 0), (sv['u_'], 'row', tnf, 0)], dswiglu_epi,
        [(F, BF16, tnf, 0), (F, BF16, tnf, 0)], M=S, tm=tm, tn=tnf, nj=F // tnf)

    dr1, dr1b, dg1, db1 = fused_mm(
        "dh1", [(dg_b, W['gate_T'], 0, True), (du_b, W['up_T'], 0, True)],
        [(dr2, 'row', D, 0), (sv['r1'], 'row', D, 0), (sm['ln1_g'], 'vec', D, 0)],
        ln_bwd_epi(alpha), [(D, F32, D, 0), (D, BF16, D, 0)], [(D, D, 0), (D, D, 0)],
        M=S, tm=tm, tn=D, nk=cf['nk_f'])
    out['ln1_g'], out['ln1_b'] = dg1, db1

    goff = (2 * CD) // D

    def dmerge_epi(acc, ex):
        ga, gb, ya, yb = ex
        sa_, sb_ = _sig(ga), _sig(gb)
        dga = acc * ya * (sa_ * (1.0 - sa_))
        dgb = acc * yb * (sb_ * (1.0 - sb_))
        return [jnp.concatenate([dga, dgb], axis=1), acc * sa_, acc * sb_], []

    dproj, dya_b, dyb_b = fused_mm(
        "d_merge", [(dr1b, W['o_T'], 0, False)],
        [(sv['proj'], 'row', D, goff), (sv['proj'], 'row', D, goff + 1), (sv['y_a'], 'row', D, 0), (sv['y_b'], 'row', D, 0)],
        dmerge_epi, [(NM, BF16, 2 * D, (2 * CD) // (2 * D)), (D, BF16, D, 0), (D, BF16, D, 0)], M=S, tm=tm, tn=D)

    def dsa_epi(acc, ex):
        ca_, g_, b_ = ex
        xhat, rstd = _ln_stats(ca_)
        la = xhat * g_ + b_
        dla = acc * _dsilu(la, _sig(la))
        dca = _ln_bwd(dla, xhat, rstd, g_)
        return [dca], [dla * xhat, dla, dca]

    dca, dlag, dlab, dcab = fused_mm(
        "d_a_out", [(dya_b, W['a_out_T'], 0, False)],
        [(sv['ca'], 'row', CD, 0), (sm['ln_a_g'], 'vec', CD, 0), (sm['ln_a_b'], 'vec', CD, 0)],
        dsa_epi, [(CD, F32, CD, 0)], [(CD, CD, 0), (CD, CD, 0), (CD, CD, 0)], M=S, tm=tm, tn=D)
    out['ln_a_g'], out['ln_a_b'], out['conv_a_b'] = dlag, dlab, dcab

    def dglu_epi(du, ex):
        a, gt = ex
        s = _sig(gt)
        return [jnp.concatenate([du * s, du * a * (s * (1.0 - s))], axis=1)], []

    dproj, dwa = conv_call(
        "d_conv_a", dca, 0, sm['conv_a_w'], cf['KA'], dglu_epi,
        [(sv['proj'], 'row', CD, 0, 0), (sv['proj'], 'row', CD, 1, 0)],
        [(NM, BF16, 2 * CD, 0, 0)], M=S, tm=cf['tmc'], cw=CD, nc=1, reverse=True, xin=(sv['u'], 0),
        passthrough=(dproj, 0))
    out['conv_a_w'] = dwa

    zoff = (2 * CD + 2 * D) // DI

    def dgate_norm_epi(acc, ex):
        ysum_, xs, z, dsk, ng = ex
        y = ysum_ + xs * dsk
        sz = _sig(z)
        siluz = z * sz
        yz = y * siluz
        dyzs, yhats = [], []
        for g in range(G):
            t = yz[:, g * gw:(g + 1) * gw]
            rinv = lax.rsqrt(jnp.mean(t * t, axis=-1, keepdims=True) + RMS_EPS)
            yh = t * rinv
            qv = acc[:, g * gw:(g + 1) * gw] * ng[:, g * gw:(g + 1) * gw]
            dyzs.append(rinv * (qv - yh * jnp.mean(qv * yh, axis=-1, keepdims=True)))
            yhats.append(yh)
        dyz = jnp.concatenate(dyzs, axis=1)
        yhat = jnp.concatenate(yhats, axis=1)
        dy = dyz * siluz
        dz = dyz * y * _dsilu(z, sz)
        return [dy, dz], [acc * yhat, dy * xs]

    tmr = cf['tmr']
    dy_ssd, dproj, dng, ddsk = fused_mm(
        "d_b_out", [(dyb_b, W['b_out_T'], 0, False)],
        [(sv['ysum'], 'row', DI, 0), (sv['xbc'], 'row', DI, 0), (sv['proj'], 'row', DI, zoff),
         (sm['dskip_full'], 'vec', DI, 0), (sm['ssm_norm_g'], 'vec', DI, 0)],
        dgate_norm_epi, [(DI, F32, DI, 0), (NM, BF16, DI, zoff)], [(DI, DI, 0), (DI, DI, 0)],
        M=S, tm=tmr, tn=DI, passthrough=(dproj, 1))
    out['ssm_norm_g'], out['dskip_full'] = dng, ddsk

    dxbc_f, ddt_f, dA_f = ssd_bwd("ssd_bwd_f", sv['xbc'], sv['dtraw'], dy_ssd, sv['st_f'], sm['dtb_f'], sm['alog_f'],
                                  S=S, DI=DI, G=G, H=H, rev=False)
    dcb, ddt_r, dA_r, dcbb = ssd_bwd("ssd_bwd_r", sv['xbc'], sv['dtraw'], dy_ssd, sv['st_r'], sm['dtb_r'], sm['alog_r'],
                                     S=S, DI=DI, G=G, H=H, rev=True, tail=(dxbc_f, sv['cbv'], sm['dskip_full']))
    out['dA_f'], out['dA_r'] = dA_f, dA_r
    out['ssm_conv_b'] = dcbb

    xoff = (2 * CD + 2 * D + DI) // DI
    dproj, dwb = conv_call(
        "d_conv_b", dcb, 0, sm['ssm_conv_w'], cf['KB'], lambda conv, ex: ([conv], []), [],
        [(NM, BF16, DI, xoff, 1)], M=S, tm=cf['tmc'], cw=DI, nc=XBC // DI, reverse=True, xin=(sv['proj'], xoff),
        passthrough=(dproj, 0))
    out['ssm_conv_w'] = dwb

    ddtb, ddt_bias = row_call("d_dt", lambda a, b: ([a + b], [a + b]),
                              [(ddt_f, 'row', LANES, 0, 0), (ddt_r, 'row', LANES, 0, 0)],
                              [(LANES, BF16, LANES, 0, 0)], [(LANES, LANES, 0, 0)], M=S, tm=tm)
    out['dt_bias'] = ddt_bias

    dx, = fused_mm("d_x", [(dproj, W['in_main_T'], 0, True), (ddtb, W['in_dt_T'], 0, False)],
                   [(dr1, 'row', D, 0)], lambda acc, ex: ([alpha * ex[0] + acc], []),
                   [(D, F32, D, 0)], M=S, tm=tm, tn=D, nk=cf['nk_in'])

    tmw = cf['tmw']
    xb = sv['xb']
    out['w_in'] = jnp.concatenate(
        [mm_tn("dw_in", xb, dproj, tm=tmw, tk=D, tn=cf['tn_in']),
         mm_tn("dw_dt", xb, ddtb, tm=tmw, tk=D, tn=LANES)[:, :2 * H]], axis=1)
    out['w_a_out'] = mm_tn("dw_a_out", sv['sa'], dya_b, tm=tmw, tk=CD, tn=D)
    out['w_b_out'] = mm_tn("dw_b_out", sv['yn'], dyb_b, tm=tmw, tk=DI // 2, tn=D)
    out['w_o'] = mm_tn("dw_o", sv['merged'], dr1b, tm=tmw, tk=D, tn=D)
    out['w_gate_up'] = jnp.concatenate(
        [mm_tn("dw_gate", sv['hb'], dg_b, tm=tmw, tk=D, tn=tnf),
         mm_tn("dw_up", sv['hb'], du_b, tm=tmw, tk=D, tn=tnf)], axis=1)
    out['w_down'] = mm_tn("dw_down", sv['f'], dr2b, tm=tmw, tk=tnf, tn=D)
    out['w_ple'] = mm_tn("dw_ple", sv['pb'], dpe, tm=tmw, tk=sv['pb'].shape[1], tn=D)
    out['w_ple_gate'] = mm_tn("dw_ple_gate", sv['h2b'], dtg, tm=tmw, tk=D, tn=D)
    return dx, out


_WEIGHTS = ['w_in', 'conv_a_w', 'conv_a_b', 'ln_a_g', 'ln_a_b', 'w_a_out', 'ssm_conv_w', 'ssm_conv_b', 'a_log',
            'dt_bias', 'd_skip', 'ssm_norm_g', 'w_b_out', 'w_o', 'ln1_g', 'ln1_b', 'w_gate_up', 'w_down', 'ln2_g',
            'ln2_b', 'w_ple', 'ple_norm_g', 'w_ple_gate']
_COL_SHARDED = ['w_in', 'conv_a_w', 'ssm_conv_w', 'w_gate_up', 'w_ple']
_ROW_SHARDED = ['w_a_out', 'w_b_out', 'w_o', 'w_down', 'w_ple_gate']
_BIG = _COL_SHARDED + _ROW_SHARDED
_SMALL = [n for n in _WEIGHTS if n not in _BIG]
_CONV = ['conv_a_w', 'ssm_conv_w']


def _ceil_to(n, k):
    return -(-n // k) * k


def kernel(x, p, w_in, conv_a_w, conv_a_b, ln_a_g, ln_a_b, w_a_out, ssm_conv_w, ssm_conv_b, a_log, dt_bias, d_skip, ssm_norm_g, w_b_out, w_o, ln1_g, ln1_b, w_gate_up, w_down, ln2_g, ln2_b, w_ple, ple_norm_g, w_ple_gate, loss_target, m_w_in, m_conv_a_w, m_conv_a_b, m_ln_a_g, m_ln_a_b, m_w_a_out, m_ssm_conv_w, m_ssm_conv_b, m_a_log, m_dt_bias, m_d_skip, m_ssm_norm_g, m_w_b_out, m_w_o, m_ln1_g, m_ln1_b, m_w_gate_up, m_w_down, m_ln2_g, m_ln2_b, m_w_ple, m_ple_norm_g, m_w_ple_gate, v_w_in, v_conv_a_w, v_conv_a_b, v_ln_a_g, v_ln_a_b, v_w_a_out, v_ssm_conv_w, v_ssm_conv_b, v_a_log, v_dt_bias, v_d_skip, v_ssm_norm_g, v_w_b_out, v_w_o, v_ln1_g, v_ln1_b, v_w_gate_up, v_w_down, v_ln2_g, v_ln2_b, v_w_ple, v_ple_norm_g, v_w_ple_gate):
    wt = dict(w_in=w_in, conv_a_w=conv_a_w, conv_a_b=conv_a_b, ln_a_g=ln_a_g, ln_a_b=ln_a_b, w_a_out=w_a_out,
              ssm_conv_w=ssm_conv_w, ssm_conv_b=ssm_conv_b, a_log=a_log, dt_bias=dt_bias, d_skip=d_skip,
              ssm_norm_g=ssm_norm_g, w_b_out=w_b_out, w_o=w_o, ln1_g=ln1_g, ln1_b=ln1_b, w_gate_up=w_gate_up,
              w_down=w_down, ln2_g=ln2_g, ln2_b=ln2_b, w_ple=w_ple, ple_norm_g=ple_norm_g, w_ple_gate=w_ple_gate)
    mo = dict(w_in=m_w_in, conv_a_w=m_conv_a_w, conv_a_b=m_conv_a_b, ln_a_g=m_ln_a_g, ln_a_b=m_ln_a_b,
              w_a_out=m_w_a_out, ssm_conv_w=m_ssm_conv_w, ssm_conv_b=m_ssm_conv_b, a_log=m_a_log,
              dt_bias=m_dt_bias, d_skip=m_d_skip, ssm_norm_g=m_ssm_norm_g, w_b_out=m_w_b_out, w_o=m_w_o,
              ln1_g=m_ln1_g, ln1_b=m_ln1_b, w_gate_up=m_w_gate_up, w_down=m_w_down, ln2_g=m_ln2_g, ln2_b=m_ln2_b,
              w_ple=m_w_ple, ple_norm_g=m_ple_norm_g, w_ple_gate=m_w_ple_gate)
    vo = dict(w_in=v_w_in, conv_a_w=v_conv_a_w, conv_a_b=v_conv_a_b, ln_a_g=v_ln_a_g, ln_a_b=v_ln_a_b,
              w_a_out=v_w_a_out, ssm_conv_w=v_ssm_conv_w, ssm_conv_b=v_ssm_conv_b, a_log=v_a_log,
              dt_bias=v_dt_bias, d_skip=v_d_skip, ssm_norm_g=v_ssm_norm_g, w_b_out=v_w_b_out, w_o=v_w_o,
              ln1_g=v_ln1_g, ln1_b=v_ln1_b, w_gate_up=v_w_gate_up, w_down=v_w_down, ln2_g=v_ln2_g, ln2_b=v_ln2_b,
              w_ple=v_w_ple, ple_norm_g=v_ple_norm_g, w_ple_gate=v_w_ple_gate)

    L = w_in.shape[0]
    S, D = x.shape[1], x.shape[2]
    CD = conv_a_b.shape[1]
    DI = ssm_norm_g.shape[1]
    XBC = ssm_conv_b.shape[1]
    H = d_skip.shape[1]
    G = (XBC - DI) // (2 * D_STATE)
    F = w_down.shape[1] * 4
    N_IN = w_in.shape[2] * 4
    NM = N_IN - 2 * H
    KA, KB = conv_a_w.shape[1], ssm_conv_w.shape[1]
    assert DI == H * HEAD_DIM and CD == D and DI == 2 * D and XBC == 2 * DI and NM == 2 * CD + 2 * D + DI + XBC
    assert 2 * H <= LANES and S % CHUNK == 0
    tnf = F // 2
    cf = dict(S=S, D=D, CD=CD, DI=DI, XBC=XBC, F=F, H=H, G=G, NM=NM, KA=KA, KB=KB, GW=(H // G) * HEAD_DIM,
              alpha=float((2 * L) ** 0.25), tm=min(512, S), tmx=min(1024, S), tmc=min(256, S), tmr=min(256, S), tmw=min(1024, S),
              tn_in=D, tnf=tnf, nk_f=2, nk_in=NM // DI)

    core = lax.axis_index("c").astype(jnp.int32).reshape(1)
    split_names = [n for n in _BIG if n not in _CONV]

    def layer_weights(l):
        got = gather_layer("gather_weights", [wt[n][l].astype(BF16) for n in split_names], [wt[n][l] for n in _CONV])
        full = {}
        for n, g in zip(split_names + _CONV, got):
            if n in _COL_SHARDED:
                full[n] = g.transpose(1, 0, 2).reshape(g.shape[1], 4 * g.shape[2])
            else:
                full[n] = g.reshape(4 * g.shape[1], g.shape[2])
        win = full['w_in']
        in_main = win[:, :NM]
        in_dt = _pad_lanes(win[:, NM:])
        gu = full['w_gate_up']
        W = dict(in_main=in_main, in_dt=in_dt, in_main_T=in_main.T, in_dt_T=in_dt.T,
                 a_out=full['w_a_out'], a_out_T=full['w_a_out'].T,
                 b_out=full['w_b_out'], b_out_T=full['w_b_out'].T,
                 o=full['w_o'], o_T=full['w_o'].T, gate_up=gu, gate_T=gu[:, :F].T, up_T=gu[:, F:].T,
                 down=full['w_down'], down_T=full['w_down'].T, ple=full['w_ple'],
                 ple_gate=full['w_ple_gate'], ple_gate_T=full['w_ple_gate'].T)
        row = lambda v: v.reshape(1, -1)
        head_table = lambda v: jnp.broadcast_to(jnp.pad(v, (0, LANES - H))[:, None], (LANES, LANES))
        sm = dict(conv_a_w=jnp.pad(full['conv_a_w'], ((0, _ceil_to(KA, SUBLANES) - KA), (0, 0))),
                  ssm_conv_w=jnp.pad(full['ssm_conv_w'], ((0, _ceil_to(KB, SUBLANES) - KB), (0, 0))),
                  conv_a_b=row(conv_a_b[l]), ln_a_g=row(ln_a_g[l]), ln_a_b=row(ln_a_b[l]),
                  ssm_conv_b=row(ssm_conv_b[l]), ssm_norm_g=row(ssm_norm_g[l]),
                  ln1_g=row(ln1_g[l]), ln1_b=row(ln1_b[l]), ln2_g=row(ln2_g[l]), ln2_b=row(ln2_b[l]),
                  ple_norm_g=row(ple_norm_g[l]),
                  dtb_f=head_table(dt_bias[l, 0]), dtb_r=head_table(dt_bias[l, 1]),
                  alog_f=head_table(a_log[l, 0]), alog_r=head_table(a_log[l, 1]),
                  dskip_full=row(jnp.repeat(d_skip[l], HEAD_DIM)))
        return W, sm

    def blocks(n, gl):
        g = gl[n]
        if n == 'conv_a_w':
            g = g.sum(axis=1)[:KA]
        elif n == 'ssm_conv_w':
            g = g.sum(axis=1)[:KB]
        if n in _COL_SHARDED:
            return g.reshape(g.shape[0], 4, g.shape[1] // 4).transpose(1, 0, 2)
        return g.reshape(4, g.shape[0] // 4, g.shape[1])

    def reduce_layer(l, gl, acc):
        mine = [blocks(n, gl) for n in split_names]
        theirs = core_send_half("core_send_half", mine)
        both = [core_sum("core_sum_" + n, core, b, t) for n, b, t in zip(split_names, mine, theirs)]
        parts = chip_exchange("scatter_grads", [[t] for t in both], gather=False)
        sums = [chip_sum_into("chip_sum_" + n, core, pr.reshape(4, pr.shape[2], pr.shape[3]), l, L, into=acc.get(n))
                for n, pr in zip(split_names, parts)]
        return dict(zip(split_names, core_fill("core_fill", sums, l, L)))

    lw = [layer_weights(l) for l in range(L)]
    xl = x[0]
    xlb = xl.astype(BF16)
    saved = []
    for l in range(L):
        xl, xlb, sv = _layer_fwd(cf, xl, xlb, p[l, 0].astype(BF16), lw[l][0], lw[l][1])
        saved.append(sv)
    grads = [None] * L
    dxl = None
    gsum = {}
    for l in reversed(range(L)):
        if l == L - 1:
            dxl, grads[l] = _layer_bwd(cf, saved[l], lw[l][0], lw[l][1], target=loss_target[0], xn=xl)
        else:
            dxl, grads[l] = _layer_bwd(cf, saved[l], lw[l][0], lw[l][1], dxn=dxl)
        gsum = reduce_layer(l, grads[l], gsum)
    loss = lax.psum(0.5 / D * jnp.sum(grads[L - 1]['loss_sq']), ("x", "y", "c"))
    grad_x = dxl[None]

    res = {}
    for n in split_names:
        shp = wt[n].shape
        flat = lambda a: a.reshape(shp[0] * shp[1], shp[2])
        outs = adamw_full("adamw_" + n, gsum[n], flat(wt[n]), flat(mo[n]), flat(vo[n]))
        res[n] = [o.reshape(shp) for o in [gsum[n]] + list(outs)]
    parts = chip_exchange("scatter_conv", [[blocks(n, grads[l]) for l in range(L)] for n in _CONV], gather=False)
    chip_sums = [sum_chips("chip_sum_" + n, pr.reshape(4, L * pr.shape[2], pr.shape[3])) for n, pr in zip(_CONV, parts)]
    sib_sums = sibling_swap("core_swap", chip_sums)
    for n, mine, sib in zip(_CONV, chip_sums, sib_sums):
        shp = wt[n].shape
        flat = lambda a: a.reshape(shp[0] * shp[1], shp[2])
        outs = adamw_shard("adamw_" + n, mine, sib, flat(wt[n]), flat(mo[n]), flat(vo[n]))
        res[n] = [o.reshape(shp) for o in outs]

    def small_pieces(l):
        gl = grads[l]
        A = -jnp.exp(a_log[l])
        d = dict(gl)
        d_alog = jnp.concatenate([gl['dA_f'].sum(axis=1)[:H] * A[0], gl['dA_r'].sum(axis=1)[:H] * A[1]])
        d['a_log'] = jnp.pad(d_alog[None], ((0, SUBLANES - 1), (0, 0)))
        d['dt_bias'] = gl['dt_bias'][:, :2 * H]
        d['d_skip'] = gl['dskip_full'].reshape(SUBLANES, H, HEAD_DIM).sum(axis=-1)
        return [_pad_lanes(d[n], _ceil_to(d[n].shape[1], LANES)) for n in _SMALL]

    widths = [_ceil_to(math.prod(wt[n].shape[1:]), LANES) for n in _SMALL]
    packed = jnp.concatenate([pc for l in range(L) for pc in small_pieces(l)], axis=1)
    gathered = all8_gather("gather_small", packed)

    def pack_params(src):
        return jnp.concatenate([_pad_lanes(src[n][l].reshape(1, -1), wd) for l in range(L) for n, wd in zip(_SMALL, widths)],
                               axis=1)

    small_out = adamw_small("adamw_small", gathered, pack_params(wt), pack_params(mo), pack_params(vo))
    off = 0
    per = {n: [[] for _ in range(4)] for n in _SMALL}
    for l in range(L):
        for n, wd in zip(_SMALL, widths):
            size = math.prod(wt[n].shape[1:])
            for k in range(4):
                per[n][k].append(small_out[k][0, off:off + size].reshape(wt[n].shape[1:]))
            off += wd
    for n in _SMALL:
        res[n] = [jnp.stack(per[n][k]) for k in range(4)]

    return (loss, grad_x, *[res[n][0] for n in _WEIGHTS], *[res[n][1] for n in _WEIGHTS],
            *[res[n][2] for n in _WEIGHTS], *[res[n][3] for n in _WEIGHTS])
```

```python
import math

import jax
import jax.numpy as jnp
from jax import lax
from jax.experimental import pallas as pl
from jax.experimental.pallas import tpu as pltpu

F32 = jnp.float32
BF16 = jnp.bfloat16

VMEM_LIMIT_BYTES = 56 * 1024 * 1024
LANES = 128
SUBLANES = 8

CHUNK = 128
D_STATE = 128
HEAD_DIM = 64
LN_EPS = 1e-5
RMS_EPS = 1e-6
ADAM_LR = 0.001
ADAM_B1 = 0.9
ADAM_B2 = 0.999
ADAM_EPS = 1e-08
ADAM_WD = 0.01
ADAM_STEP = 10
HALO = 16
MESH = pl.DeviceIdType.MESH


def _params(**kw):
    return pltpu.CompilerParams(vmem_limit_bytes=VMEM_LIMIT_BYTES, **kw)


def _sig(x):
    return jax.nn.sigmoid(x)


def _dsilu(x, s):
    return s * (1.0 + x * (1.0 - s))


def _ln_stats(r):
    mu = jnp.mean(r, axis=-1, keepdims=True)
    xc = r - mu
    var = jnp.mean(xc * xc, axis=-1, keepdims=True)
    rstd = lax.rsqrt(var + LN_EPS)
    return xc * rstd, rstd


def _ln_bwd(dy, xhat, rstd, g):
    dxh = dy * g
    m1 = jnp.mean(dxh, axis=-1, keepdims=True)
    m2 = jnp.mean(dxh * xhat, axis=-1, keepdims=True)
    return rstd * (dxh - m1 - xhat * m2)


def _f32(v):
    return v if v.dtype == F32 else v.astype(F32)


def _rows8(v):
    tm, w = v.shape
    return v.reshape(tm // SUBLANES, SUBLANES, w).sum(axis=0)


def fused_mm(name, prods, extras, epi, row_outs, col_outs=(), *, M, tm, tn, nj=1, nk=1,
             passthrough=None):
    np_ = len(prods)
    ne = len(extras)
    nro = len(row_outs)
    nco = len(col_outs)
    use_acc = nk > 1

    def body(*refs):
        a_refs = [refs[2 * p] for p in range(np_)]
        w_refs = [refs[2 * p + 1] for p in range(np_)]
        pos = 2 * np_
        e_refs = refs[pos:pos + ne]
        pos += ne
        if passthrough is not None:
            pos += 1
        ro_refs = refs[pos:pos + nro]
        pos += nro
        co_refs = refs[pos:pos + nco]
        pos += nco
        acc_ref = refs[pos] if use_acc else None
        i = pl.program_id(1)
        k = pl.program_id(2)

        def prod(p):
            a = a_refs[p][...]
            if a.dtype != BF16:
                a = a.astype(BF16)
            return jnp.dot(a, w_refs[p][...], preferred_element_type=F32)

        def finish(acc):
            rows, cols = epi(acc, [_f32(r[...]) for r in e_refs])
            for v, o in zip(rows, ro_refs):
                o[...] = v.astype(o.dtype)
            for v, o in zip(cols, co_refs):
                v8 = _rows8(v)

                @pl.when(i == 0)
                def _():
                    o[...] = v8

                @pl.when(i > 0)
                def _():
                    o[...] += v8

        if not use_acc:
            acc = prod(0)
            for p in range(1, np_):
                acc = acc + prod(p)
            finish(acc)
        else:
            @pl.when(k == 0)
            def _():
                acc = None
                for p in range(np_):
                    acc = prod(p) if acc is None else acc + prod(p)
                acc_ref[...] = acc

            @pl.when(k > 0)
            def _():
                acc = None
                for p in range(np_):
                    if prods[p][3]:
                        acc = prod(p) if acc is None else acc + prod(p)
                acc_ref[...] += acc

            @pl.when(k == nk - 1)
            def _():
                finish(acc_ref[...])

    in_specs = []
    args = []
    for a, w, joff, ksplit in prods:
        K = a.shape[1]
        if ksplit:
            tk = K // nk
            in_specs.append(pl.BlockSpec((tm, tk), lambda j, i, k: (i, k)))
            in_specs.append(pl.BlockSpec((tk, tn), lambda j, i, k, joff=joff: (k, j + joff)))
        else:
            in_specs.append(pl.BlockSpec((tm, K), lambda j, i, k: (i, 0)))
            in_specs.append(pl.BlockSpec((K, tn), lambda j, i, k, joff=joff: (0, j + joff)))
        args += [a, w]
    for arr, kind, width, c0 in extras:
        if kind == 'row':
            in_specs.append(pl.BlockSpec((tm, width), lambda j, i, k, c0=c0: (i, c0 + j)))
        else:
            in_specs.append(pl.BlockSpec((arr.shape[0], width), lambda j, i, k, c0=c0: (0, c0 + j)))
        args.append(arr)
    aliases = {}
    if passthrough is not None:
        arr, oidx = passthrough
        in_specs.append(pl.BlockSpec(memory_space=pl.ANY))
        aliases = {len(args): oidx}
        args.append(arr)
    out_shape = []
    out_specs = []
    for n_total, dtype, width, c0 in row_outs:
        out_shape.append(jax.ShapeDtypeStruct((M, n_total), dtype))
        out_specs.append(pl.BlockSpec((tm, width), lambda j, i, k, c0=c0: (i, c0 + j)))
    for n_total, width, c0 in col_outs:
        out_shape.append(jax.ShapeDtypeStruct((SUBLANES, n_total), F32))
        out_specs.append(pl.BlockSpec((SUBLANES, width), lambda j, i, k, c0=c0: (0, c0 + j)))
    scratch = [pltpu.VMEM((tm, tn), F32)] if use_acc else []
    return pl.pallas_call(
        body, name=name, grid=(nj, M // tm, nk), in_specs=in_specs, out_specs=out_specs,
        out_shape=out_shape, scratch_shapes=scratch, input_output_aliases=aliases,
        compiler_params=_params(dimension_semantics=("arbitrary", "arbitrary", "arbitrary")),
    )(*args)


def mm_tn(name, a, b, *, tm, tk, tn):
    M, K = a.shape
    N = b.shape[1]

    def body(a_ref, b_ref, o_ref):
        m = pl.program_id(2)
        p = lax.dot_general(a_ref[...], b_ref[...], (((0,), (0,)), ((), ())),
                            preferred_element_type=F32)

        @pl.when(m == 0)
        def _():
            o_ref[...] = p

        @pl.when(m > 0)
        def _():
            o_ref[...] += p

    return pl.pallas_call(
        body, name=name, grid=(K // tk, N // tn, M // tm),
        in_specs=[pl.BlockSpec((tm, tk), lambda kk, j, m: (m, kk)),
                  pl.BlockSpec((tm, tn), lambda kk, j, m: (m, j))],
        out_specs=pl.BlockSpec((tk, tn), lambda kk, j, m: (kk, j)),
        out_shape=jax.ShapeDtypeStruct((K, N), F32),
        compiler_params=_params(dimension_semantics=("arbitrary", "arbitrary", "arbitrary")),
    )(a, b)


def row_call(name, fn, ins, row_outs, col_outs=(), *, M, tm, nc=1):
    ni = len(ins)
    nro = len(row_outs)

    def body(*refs):
        i = pl.program_id(1)
        vals = [_f32(r[...]) for r in refs[:ni]]
        rows, cols = fn(*vals)
        for v, o in zip(rows, refs[ni:ni + nro]):
            o[...] = v.astype(o.dtype)
        for v, o in zip(cols, refs[ni + nro:]):
            v8 = _rows8(v)

            @pl.when(i == 0)
            def _():
                o[...] = v8

            @pl.when(i > 0)
            def _():
                o[...] += v8

    in_specs = []
    for arr, kind, width, c0, cmul in ins:
        if kind == 'row':
            in_specs.append(pl.BlockSpec((tm, width), lambda cj, i, c0=c0, cmul=cmul: (i, c0 + cmul * cj)))
        else:
            in_specs.append(pl.BlockSpec((arr.shape[0], width), lambda cj, i, c0=c0, cmul=cmul: (0, c0 + cmul * cj)))
    out_shape = []
    out_specs = []
    for n_total, dtype, width, c0, cmul in row_outs:
        out_shape.append(jax.ShapeDtypeStruct((M, n_total), dtype))
        out_specs.append(pl.BlockSpec((tm, width), lambda cj, i, c0=c0, cmul=cmul: (i, c0 + cmul * cj)))
    for n_total, width, c0, cmul in col_outs:
        out_shape.append(jax.ShapeDtypeStruct((SUBLANES, n_total), F32))
        out_specs.append(pl.BlockSpec((SUBLANES, width), lambda cj, i, c0=c0, cmul=cmul: (0, c0 + cmul * cj)))
    return pl.pallas_call(
        body, name=name, grid=(nc, M // tm), in_specs=in_specs, out_specs=out_specs,
        out_shape=out_shape,
        compiler_params=_params(dimension_semantics=("arbitrary", "arbitrary")),
    )(*[a[0] for a in ins])


def conv_call(name, src, src_c0, w, K, epi, extras, row_outs, col_outs=(), *, M, tm, cw, nc,
              reverse, xin=None, passthrough=None):
    pad = (K - 1) // 2
    assert pad <= HALO - 1
    R = tm // HALO
    nblk = M // HALO
    n_i = M // tm
    Kp = w.shape[0]
    ne = len(extras)
    nro = len(row_outs)
    nco = len(col_outs)
    rb = 64
    cbw = min(cw, 256)
    n_copies = SUBLANES if K > SUBLANES else 1

    def body(*refs):
        main_ref, prev_ref, next_ref, w_ref = refs[:4]
        pos = 4
        xin_ref = None
        if xin is not None:
            xin_ref = refs[pos]
            pos += 1
        e_refs = refs[pos:pos + ne]
        pos += ne
        if passthrough is not None:
            pos += 1
        ro_refs = refs[pos:pos + nro]
        pos += nro
        co_refs = refs[pos:pos + nco]
        pos += nco
        dw_ref = None
        if xin is not None:
            dw_ref = refs[pos]
            pos += 1
        ext_ref, conv_ref = refs[pos], refs[pos + 1]
        i = pl.program_id(1)

        ext_ref[0, 0:HALO, :] = jnp.where(i == 0, 0.0, prev_ref[...].astype(F32))
        ext_ref[0, HALO:HALO + tm, :] = main_ref[...].astype(F32)
        ext_ref[0, HALO + tm:, :] = jnp.where(i == n_i - 1, 0.0, next_ref[...].astype(F32))
        if dw_ref is not None:
            @pl.when(i == 0)
            def _():
                dw_ref[...] = jnp.zeros_like(dw_ref)

        n_sh = tm + 2 * HALO - SUBLANES
        for c0 in range(0, cw, cbw):
            for sft in range(1, n_copies):
                ext_ref[sft, 0:n_sh, c0:c0 + cbw] = ext_ref[0, sft:sft + n_sh, c0:c0 + cbw]

        for c0 in range(0, cw, cbw):
            for r0 in range(0, tm, rb):
                acc = jnp.zeros((rb, cbw), F32)
                if xin_ref is not None:
                    xblk = xin_ref[r0:r0 + rb, c0:c0 + cbw].astype(F32)
                for k in range(K):
                    off = HALO + r0 + ((pad - k) if reverse else (k - pad))
                    sft = off % SUBLANES if n_copies > 1 else 0
                    d = ext_ref[sft, off - sft:off - sft + rb, c0:c0 + cbw]
                    acc = acc + d * w_ref[k:k + 1, c0:c0 + cbw]
                    if xin_ref is not None:
                        dw_ref[k, :, c0:c0 + cbw] += _rows8(xblk * d)
                conv_ref[r0:r0 + rb, c0:c0 + cbw] = acc

        rows, cols = epi(conv_ref[...], [_f32(r[...]) for r in e_refs])
        for v, o in zip(rows, ro_refs):
            o[...] = v.astype(o.dtype)
        for v, o in zip(cols, co_refs):
            v8 = _rows8(v)

            @pl.when(i == 0)
            def _():
                o[...] = v8

            @pl.when(i > 0)
            def _():
                o[...] += v8

    in_specs = [
        pl.BlockSpec((tm, cw), lambda cj, i: (i, src_c0 + cj)),
        pl.BlockSpec((HALO, cw), lambda cj, i: (jnp.maximum(i * R - 1, 0), src_c0 + cj)),
        pl.BlockSpec((HALO, cw), lambda cj, i: (jnp.minimum((i + 1) * R, nblk - 1), src_c0 + cj)),
        pl.BlockSpec((Kp, cw), lambda cj, i: (0, cj)),
    ]
    args = [src, src, src, w]
    if xin is not None:
        in_specs.append(pl.BlockSpec((tm, cw), lambda cj, i, c0=xin[1]: (i, c0 + cj)))
        args.append(xin[0])
    for arr, kind, width, c0, cmul in extras:
        if kind == 'row':
            in_specs.append(pl.BlockSpec((tm, width), lambda cj, i, c0=c0, cmul=cmul: (i, c0 + cmul * cj)))
        else:
            in_specs.append(pl.BlockSpec((arr.shape[0], width), lambda cj, i, c0=c0, cmul=cmul: (0, c0 + cmul * cj)))
        args.append(arr)
    aliases = {}
    if passthrough is not None:
        in_specs.append(pl.BlockSpec(memory_space=pl.ANY))
        aliases = {len(args): passthrough[1]}
        args.append(passthrough[0])
    out_shape = []
    out_specs = []
    for n_total, dtype, width, c0, cmul in row_outs:
        out_shape.append(jax.ShapeDtypeStruct((M, n_total), dtype))
        out_specs.append(pl.BlockSpec((tm, width), lambda cj, i, c0=c0, cmul=cmul: (i, c0 + cmul * cj)))
    for n_total, width, c0, cmul in col_outs:
        out_shape.append(jax.ShapeDtypeStruct((SUBLANES, n_total), F32))
        out_specs.append(pl.BlockSpec((SUBLANES, width), lambda cj, i, c0=c0, cmul=cmul: (0, c0 + cmul * cj)))
    if xin is not None:
        out_shape.append(jax.ShapeDtypeStruct((Kp, SUBLANES, cw * nc), F32))
        out_specs.append(pl.BlockSpec((Kp, SUBLANES, cw), lambda cj, i: (0, 0, cj)))
    return pl.pallas_call(
        body, name=name, grid=(nc, n_i), in_specs=in_specs, out_specs=out_specs,
        out_shape=out_shape, input_output_aliases=aliases,
        scratch_shapes=[pltpu.VMEM((n_copies, tm + 2 * HALO, cw), F32), pltpu.VMEM((tm, cw), F32)],
        compiler_params=_params(dimension_semantics=("arbitrary", "arbitrary")),
    )(*args)


def _split_dot(m_bf16, v, n_pass, dims=None):
    out = None
    rest = v
    for p in range(n_pass):
        piece = rest.astype(BF16)
        if p + 1 < n_pass:
            rest = rest - piece.astype(F32)
        if dims is None:
            t = jnp.dot(m_bf16, piece, preferred_element_type=F32)
        else:
            t = lax.dot_general(m_bf16, piece, dims, preferred_element_type=F32)
        out = t if out is None else out + t
    return out


def _split_dot_r(v, m_bf16, n_pass):
    out = None
    rest = v
    for p in range(n_pass):
        piece = rest.astype(BF16)
        if p + 1 < n_pass:
            rest = rest - piece.astype(F32)
        t = jnp.dot(piece, m_bf16, preferred_element_type=F32)
        out = t if out is None else out + t
    return out


def _softplus(x):
    return jnp.maximum(x, 0.0) + jnp.log1p(jnp.exp(-jnp.abs(x)))


NT_DIMS = (((1,), (1,)), ((), ()))
TN_DIMS = (((0,), (0,)), ((), ()))


def _ssd_common(dtraw, dtbT, alogT, rev, n_heads):
    L = CHUNK
    if rev:
        dtraw = pltpu.roll(dtraw, LANES - n_heads, 1)
    preT = dtraw.T + dtbT
    dtT = _softplus(preT)
    AT = -jnp.exp(alogT)
    aT = dtT * AT
    ri = lax.broadcasted_iota(jnp.int32, (L, L), 0)
    ci = lax.broadcasted_iota(jnp.int32, (L, L), 1)
    up = (ri >= ci) if rev else (ri <= ci)
    lo = (ri <= ci) if rev else (ri >= ci)
    csT = _split_dot_r(aT, up.astype(BF16), 3)
    last = 0 if rev else L - 1
    lastB = jnp.broadcast_to(csT[:, last:last + 1], (L, L))
    return dict(preT=preT, dtT=dtT, AT=AT, csT=csT, cs=csT.T, up=up, lo=lo, ci=ci, last=last,
                doutT=jnp.exp(csT), dstT=jnp.exp(lastB - csT), totB=jnp.exp(lastB))


def ssd_fwd(name, xbc, dtraw, dtbT, alogT, *, S, DI, G, H, rev, tail=None):
    NC = S // CHUNK
    R = H // G
    GW = R * HEAD_DIM
    N = D_STATE
    XBC = xbc.shape[1]
    P = HEAD_DIM

    def body(*refs):
        xbc_ref, dtraw_ref, dtb_ref, alog_ref = refs[:4]
        if tail is None:
            y_ref, st_ref, h_ref = refs[4:]
        else:
            yo_ref, z_ref, dsk_ref, ng_ref = refs[4:8]
            y_ref, st_ref, yn_ref, h_ref = refs[8:]
        c = pl.program_id(0)

        @pl.when(c == 0)
        def _():
            h_ref[...] = jnp.zeros_like(h_ref)

        q = _ssd_common(dtraw_ref[...], dtb_ref[...], alog_ref[...], rev, H)
        cs, csT, dtT, doutT, totB = q['cs'], q['csT'], q['dtT'], q['doutT'], q['totB']
        wstT = q['dstT'] * dtT
        for g in range(G):
            Bg = xbc_ref[:, DI + g * N:DI + (g + 1) * N].astype(BF16)
            Cg = xbc_ref[:, DI + G * N + g * N:DI + G * N + (g + 1) * N].astype(BF16)
            CBT = lax.dot_general(Bg, Cg, NT_DIMS, preferred_element_type=F32)
            HT = h_ref[g]
            yoffT = lax.dot_general(HT.astype(BF16), Cg, NT_DIMS, preferred_element_type=F32)
            xT = xbc_ref[:, g * GW:(g + 1) * GW].T
            hs = [g * R + r for r in range(R)]
            blks = [slice(r * P, (r + 1) * P) for r in range(R)]
            segs = [jnp.where(q['up'], csT[h:h + 1, :] - cs[:, h:h + 1], -1e30) for h in hs]
            GTs = [(CBT * jnp.exp(sg)).astype(BF16) for sg in segs]
            xThs = [xT[b, :] for b in blks]
            XThs = [(xTh * dtT[h:h + 1, :]).astype(BF16) for xTh, h in zip(xThs, hs)]
            ydTs = [jnp.dot(a, GT, preferred_element_type=F32) for a, GT in zip(XThs, GTs)]
            ys = [ydT + yoffT[b, :] * doutT[h:h + 1, :] for ydT, b, h in zip(ydTs, blks, hs)]
            xws = [xTh * wstT[h:h + 1, :] for xTh, h in zip(xThs, hs)]
            tots = [jnp.broadcast_to(totB[h:h + 1, :], (P, N)) for h in hs]
            y_ref[:, g * GW:(g + 1) * GW] = jnp.concatenate(ys, axis=0).T
            xwT = jnp.concatenate(xws, axis=0).astype(BF16)
            ST = jnp.dot(xwT, Bg, preferred_element_type=F32)
            st_ref[0, g] = HT
            h_ref[g] = HT * jnp.concatenate(tots, axis=0) + ST
        if tail is not None:
            y = y_ref[...] + yo_ref[...]
            y_ref[...] = y
            z = _f32(z_ref[...])
            yz = (y + xbc_ref[:, 0:DI] * dsk_ref[...]) * (z * _sig(z))
            for g in range(G):
                t = yz[:, g * GW:(g + 1) * GW]
                tn = t * lax.rsqrt(jnp.mean(t * t, axis=-1, keepdims=True) + RMS_EPS)
                yn_ref[:, g * GW:(g + 1) * GW] = (tn * ng_ref[:, g * GW:(g + 1) * GW]).astype(BF16)

    cidx = (lambda c: NC - 1 - c) if rev else (lambda c: c)
    cmap = lambda c: (cidx(c), 0)
    smap = lambda c: (cidx(c), 0, 0, 0)
    const = lambda c: (0, 0)
    in_specs = [pl.BlockSpec((CHUNK, XBC), cmap), pl.BlockSpec((CHUNK, LANES), cmap),
                pl.BlockSpec((LANES, LANES), const), pl.BlockSpec((LANES, LANES), const)]
    out_specs = [pl.BlockSpec((CHUNK, DI), cmap), pl.BlockSpec((1, G, GW, N), smap)]
    out_shape = [jax.ShapeDtypeStruct((S, DI), F32), jax.ShapeDtypeStruct((NC, G, GW, N), F32)]
    args = [xbc, dtraw, dtbT, alogT]
    if tail is not None:
        y_other, (z_arr, z_blk), dsk, ng = tail
        in_specs += [pl.BlockSpec((CHUNK, DI), cmap), pl.BlockSpec((CHUNK, DI), lambda c: (cidx(c), z_blk)),
                     pl.BlockSpec((1, DI), const), pl.BlockSpec((1, DI), const)]
        out_specs.append(pl.BlockSpec((CHUNK, DI), cmap))
        out_shape.append(jax.ShapeDtypeStruct((S, DI), BF16))
        args += [y_other, z_arr, dsk, ng]
    return pl.pallas_call(
        body, name=name, grid=(NC,), in_specs=in_specs, out_specs=out_specs, out_shape=out_shape,
        scratch_shapes=[pltpu.VMEM((G, GW, N), F32)],
        compiler_params=_params(dimension_semantics=("arbitrary",)),
    )(*args)


def ssd_bwd(name, xbc, dtraw, dy, st, dtbT, alogT, *, S, DI, G, H, rev, tail=None):
    NC = S // CHUNK
    R = H // G
    GW = R * HEAD_DIM
    N = D_STATE
    XBC = xbc.shape[1]
    P = HEAD_DIM
    L = CHUNK

    def body(*refs):
        xbc_ref, dtraw_ref, dy_ref, st_ref, dtb_ref, alog_ref = refs[:6]
        if tail is None:
            dxbc_ref, ddt_ref, da_ref, dh_ref, dcst_ref, p2t_ref, p3t_ref, e2t_ref = refs[6:]
        else:
            other_ref, cbv_ref, dsk_ref = refs[6:9]
            dxbc_ref, ddt_ref, da_ref, dcol_ref, dh_ref, dcst_ref, p2t_ref, p3t_ref, e2t_ref = refs[9:]
        c = pl.program_id(0)

        @pl.when(c == 0)
        def _():
            dh_ref[...] = jnp.zeros_like(dh_ref)
            da_ref[...] = jnp.zeros_like(da_ref)
            dcst_ref[...] = jnp.zeros_like(dcst_ref)
            p2t_ref[...] = jnp.zeros_like(p2t_ref)
            p3t_ref[...] = jnp.zeros_like(p3t_ref)
            e2t_ref[...] = jnp.zeros_like(e2t_ref)

        q = _ssd_common(dtraw_ref[...], dtb_ref[...], alog_ref[...], rev, H)
        cs, csT, dtT, doutT, dstT, totB = q['cs'], q['csT'], q['dtT'], q['doutT'], q['dstT'], q['totB']
        wstT = dstT * dtT
        lane = q['ci']
        dcs_c = jnp.zeros((L, LANES), F32)
        for g in range(G):
            Bg = xbc_ref[:, DI + g * N:DI + (g + 1) * N].astype(BF16)
            Cg = xbc_ref[:, DI + G * N + g * N:DI + G * N + (g + 1) * N].astype(BF16)
            CB = lax.dot_general(Cg, Bg, NT_DIMS, preferred_element_type=F32)
            HpT = st_ref[0, g]
            HpTb = HpT.astype(BF16)
            dHT = dh_ref[g]
            dHTb = dHT.astype(BF16)
            BdHT = lax.dot_general(dHTb, Bg, NT_DIMS, preferred_element_type=F32)
            yoffT = lax.dot_general(HpTb, Cg, NT_DIMS, preferred_element_type=F32)
            xT = xbc_ref[:, g * GW:(g + 1) * GW].T
            dyT = dy_ref[:, g * GW:(g + 1) * GW].T
            hs = [g * R + r for r in range(R)]
            blks = [slice(r * P, (r + 1) * P) for r in range(R)]
            Lms = [jnp.exp(jnp.where(q['lo'], cs[:, h:h + 1] - csT[h:h + 1, :], -1e30)) for h in hs]
            xThs = [xT[b, :] for b in blks]
            dyThs = [dyT[b, :] for b in blks]
            xThbs = [v.astype(BF16) for v in xThs]
            dyThbs = [v.astype(BF16) for v in dyThs]
            dGxs = [lax.dot_general(a, b, TN_DIMS, preferred_element_type=F32) for a, b in zip(dyThbs, xThbs)]
            Gms = [(CB * Lm).astype(BF16) for Lm in Lms]
            u1Ts = [jnp.dot(a, Gm, preferred_element_type=F32) for a, Gm in zip(dyThbs, Gms)]
            Ts = [dGx * (Lm * dtT[h:h + 1, :]) for dGx, Lm, h in zip(dGxs, Lms, hs)]
            dCB = Ts[0]
            for T in Ts[1:]:
                dCB = dCB + T
            Msegs = [T * CB for T in Ts]
            for h, Mseg in zip(hs, Msegs):
                dcs_c = jnp.where(lane == h, jnp.sum(Mseg, axis=1, keepdims=True), dcs_c)
            uTs = [u1T + BdHT[b, :] * dstT[h:h + 1, :] for u1T, b, h in zip(u1Ts, blks, hs)]
            dyds = [dyTh * doutT[h:h + 1, :] for dyTh, h in zip(dyThs, hs)]
            xws = [xTh * wstT[h:h + 1, :] for xTh, h in zip(xThs, hs)]
            for r, h in enumerate(hs):
                b = blks[r]
                p3row = jnp.sum(xws[r] * BdHT[b, :], axis=0, keepdims=True)
                dcst_ref[h:h + 1, :] = (jnp.sum(dyds[r] * yoffT[b, :], axis=0, keepdims=True)
                                        - jnp.sum(Msegs[r], axis=0, keepdims=True) - p3row)
                p2t_ref[h:h + 1, :] = jnp.sum(xThs[r] * uTs[r], axis=0, keepdims=True)
                p3t_ref[h:h + 1, :] = p3row
                e2t_ref[h:h + 1, :] = jnp.sum(HpT[b, :] * dHT[b, :], axis=0, keepdims=True)
            dxs = [uT * dtT[h:h + 1, :] for uT, h in zip(uTs, hs)]
            tots = [jnp.broadcast_to(totB[h:h + 1, :], (P, N)) for h in hs]
            dxbc_ref[:, g * GW:(g + 1) * GW] = jnp.concatenate(dxs, axis=0).T
            dydT = jnp.concatenate(dyds, axis=0).astype(BF16)
            xwT = jnp.concatenate(xws, axis=0).astype(BF16)
            dCBb = dCB.astype(BF16)
            dC = (jnp.dot(dCBb, Bg, preferred_element_type=F32)
                  + lax.dot_general(dydT, HpTb, TN_DIMS, preferred_element_type=F32))
            dB = (lax.dot_general(dCBb, Cg, TN_DIMS, preferred_element_type=F32)
                  + lax.dot_general(xwT, dHTb, TN_DIMS, preferred_element_type=F32))
            dxbc_ref[:, DI + g * N:DI + (g + 1) * N] = dB
            dxbc_ref[:, DI + G * N + g * N:DI + G * N + (g + 1) * N] = dC
            dh_ref[g] = (dHT * jnp.concatenate(tots, axis=0)
                         + jnp.dot(dydT, Cg, preferred_element_type=F32))
        e1 = jnp.sum(p3t_ref[...], axis=1, keepdims=True)
        e2 = jnp.sum(e2t_ref[...], axis=1, keepdims=True)
        dcsT = (dcst_ref[...] + dcs_c.T
                + jnp.where(lane == q['last'], e1 + totB * e2, 0.0))
        daT = _split_dot_r(dcsT, q['lo'].astype(BF16), 3)
        ddtT = daT * q['AT'] + p2t_ref[...]
        da_ref[...] += daT * dtT
        ddraw = jnp.where(lane < H, (ddtT * _sig(q['preT'])).T, 0.0)
        if rev:
            ddraw = pltpu.roll(ddraw, H, 1)
        ddt_ref[...] = ddraw
        if tail is not None:
            for c0 in range(0, XBC, DI):
                d = dxbc_ref[:, c0:c0 + DI] + other_ref[:, c0:c0 + DI]
                if c0 == 0:
                    d = d + dy_ref[...] * dsk_ref[...]
                cb = cbv_ref[:, c0:c0 + DI]
                dcb = d * _dsilu(cb, _sig(cb))
                dxbc_ref[:, c0:c0 + DI] = dcb
                part = _rows8(dcb)

                @pl.when(c == 0)
                def _():
                    dcol_ref[:, c0:c0 + DI] = part

                @pl.when(c > 0)
                def _():
                    dcol_ref[:, c0:c0 + DI] += part

    cmap = (lambda c: (c, 0)) if rev else (lambda c: (NC - 1 - c, 0))
    smap = (lambda c: (c, 0, 0, 0)) if rev else (lambda c: (NC - 1 - c, 0, 0, 0))
    const = lambda c: (0, 0)
    sq = pltpu.VMEM((LANES, CHUNK), F32)
    in_specs = [pl.BlockSpec((CHUNK, XBC), cmap), pl.BlockSpec((CHUNK, LANES), cmap),
                pl.BlockSpec((CHUNK, DI), cmap),
                pl.BlockSpec((1, G, GW, N), smap),
                pl.BlockSpec((LANES, LANES), const), pl.BlockSpec((LANES, LANES), const)]
    out_specs = [pl.BlockSpec((CHUNK, XBC), cmap), pl.BlockSpec((CHUNK, LANES), cmap),
                 pl.BlockSpec((LANES, LANES), const)]
    out_shape = [jax.ShapeDtypeStruct((S, XBC), F32), jax.ShapeDtypeStruct((S, LANES), F32),
                 jax.ShapeDtypeStruct((LANES, LANES), F32)]
    args = [xbc, dtraw, dy, st, dtbT, alogT]
    if tail is not None:
        in_specs += [pl.BlockSpec((CHUNK, XBC), cmap), pl.BlockSpec((CHUNK, XBC), cmap),
                     pl.BlockSpec((1, DI), const)]
        out_specs.append(pl.BlockSpec((SUBLANES, XBC), const))
        out_shape.append(jax.ShapeDtypeStruct((SUBLANES, XBC), F32))
        args += list(tail)
    return pl.pallas_call(
        body, name=name, grid=(NC,), in_specs=in_specs, out_specs=out_specs, out_shape=out_shape,
        scratch_shapes=[pltpu.VMEM((G, GW, N), F32), sq, sq, sq, sq],
        compiler_params=_params(dimension_semantics=("arbitrary",)),
    )(*args)


ANY = pl.BlockSpec(memory_space=pl.ANY)


def chip_exchange(name, groups, gather):
    flat = [arr for grp in groups for arr in grp]
    n_in = len(flat)
    n_out = len(groups)
    n_rc = 3 * n_in

    def body(*refs):
        in_refs = refs[:n_in]
        out_refs = refs[n_in:n_in + n_out]
        send, recv, loc = refs[n_in + n_out:]
        x, y, c = lax.axis_index("x"), lax.axis_index("y"), lax.axis_index("c")
        me = 2 * x + y
        peers = [(1 - x, y), (x, 1 - y), (1 - x, 1 - y)]
        local, remote = [], []
        q = 0
        for a, grp in enumerate(groups):
            for l in range(len(grp)):
                src = in_refs[q]
                dst = out_refs[a].at[me] if gather else out_refs[a].at[me, l]
                own = src if gather else src.at[me]
                lc = pltpu.make_async_copy(own, dst, loc.at[q])
                lc.start()
                local.append(lc)
                for j, (px, py) in enumerate(peers):
                    blk = src if gather else src.at[2 * px + py]
                    rc = pltpu.make_async_remote_copy(
                        src_ref=blk, dst_ref=dst, send_sem=send.at[3 * q + j], recv_sem=recv.at[3 * q + j],
                        device_id=(px, py, c), device_id_type=MESH)
                    rc.start()
                    remote.append(rc)
                q += 1
        for lc in local:
            lc.wait()
        for rc in remote:
            rc.wait()

    out_shape = []
    for grp in groups:
        a0 = grp[0]
        if gather:
            out_shape.append(jax.ShapeDtypeStruct((4,) + a0.shape, a0.dtype))
        else:
            out_shape.append(jax.ShapeDtypeStruct((4, len(grp)) + a0.shape[1:], a0.dtype))
    return pl.pallas_call(
        body, name=name, in_specs=[ANY] * n_in, out_specs=[ANY] * n_out, out_shape=out_shape,
        scratch_shapes=[pltpu.SemaphoreType.DMA((n_rc,)), pltpu.SemaphoreType.DMA((n_rc,)),
                        pltpu.SemaphoreType.DMA((n_in,))],
    )(*flat)


def gather_layer(name, split, whole):
    ns, nw = len(split), len(whole)
    n = ns + nw
    n_rc = 3 * (n + ns)

    def body(*refs):
        in_refs = refs[:n]
        out_refs = refs[n:2 * n]
        send, recv, loc = refs[2 * n:]
        x, y, c = lax.axis_index("x"), lax.axis_index("y"), lax.axis_index("c")
        me = 2 * x + y
        sibling = (x, y, 1 - c)
        peers = [(1 - x, y), (x, 1 - y), (1 - x, 1 - y)]

        def region(a, chip, half):
            if a >= ns:
                return out_refs[a].at[chip]
            hr = split[a].shape[0] // 2
            return out_refs[a].at[chip, pl.ds(half * hr, hr)]

        def mine(a):
            if a >= ns:
                return in_refs[a]
            hr = split[a].shape[0] // 2
            return in_refs[a].at[pl.ds(c * hr, hr)]

        local = []
        for a in range(n):
            lc = pltpu.make_async_copy(in_refs[a], out_refs[a].at[me], loc.at[a])
            lc.start()
            local.append(lc)
        sends = []
        for a in range(n):
            for j, (px, py) in enumerate(peers):
                rc = pltpu.make_async_remote_copy(
                    src_ref=mine(a), dst_ref=region(a, me, c), send_sem=send.at[3 * a + j],
                    recv_sem=recv.at[3 * a + j], device_id=(px, py, c), device_id_type=MESH)
                rc.start()
                sends.append(rc)
        for a in range(n):
            for j, (px, py) in enumerate(peers):
                chip = 2 * px + py
                landed = pltpu.make_async_remote_copy(
                    src_ref=mine(a), dst_ref=region(a, chip, c), send_sem=send.at[3 * a + j],
                    recv_sem=recv.at[3 * a + j], device_id=(px, py, c), device_id_type=MESH)
                landed.wait_recv()
                if a < ns:
                    fw = pltpu.make_async_remote_copy(
                        src_ref=region(a, chip, c), dst_ref=region(a, chip, c), send_sem=send.at[3 * n + 3 * a + j],
                        recv_sem=recv.at[3 * n + 3 * a + j], device_id=sibling, device_id_type=MESH)
                    fw.start()
                    sends.append(fw)
        for a in range(ns):
            for j, (px, py) in enumerate(peers):
                chip = 2 * px + py
                pltpu.make_async_remote_copy(
                    src_ref=region(a, chip, 1 - c), dst_ref=region(a, chip, 1 - c), send_sem=send.at[3 * n + 3 * a + j],
                    recv_sem=recv.at[3 * n + 3 * a + j], device_id=sibling, device_id_type=MESH).wait_recv()
        for rc in sends:
            rc.wait_send()
        for lc in local:
            lc.wait()

    arrs = list(split) + list(whole)
    return pl.pallas_call(
        body, name=name, in_specs=[ANY] * n, out_specs=[ANY] * n,
        out_shape=[jax.ShapeDtypeStruct((4,) + a.shape, a.dtype) for a in arrs],
        scratch_shapes=[pltpu.SemaphoreType.DMA((n_rc,)), pltpu.SemaphoreType.DMA((n_rc,)),
                        pltpu.SemaphoreType.DMA((n,))],
    )(*arrs)


def core_send_half(name, arrs):
    n = len(arrs)

    def body(*refs):
        in_refs = refs[:n]
        out_refs = refs[n:2 * n]
        send, recv = refs[2 * n:]
        c = lax.axis_index("c")
        peer = (lax.axis_index("x"), lax.axis_index("y"), 1 - c)
        rcs = []
        for a in range(n):
            hr = arrs[a].shape[1] // 2
            rc = pltpu.make_async_remote_copy(
                src_ref=in_refs[a].at[:, pl.ds((1 - c) * hr, hr)], dst_ref=out_refs[a], send_sem=send.at[a],
                recv_sem=recv.at[a], device_id=peer, device_id_type=MESH)
            rc.start()
            rcs.append(rc)
        for rc in rcs:
            rc.wait()

    return pl.pallas_call(
        body, name=name, in_specs=[ANY] * n, out_specs=[ANY] * n,
        out_shape=[jax.ShapeDtypeStruct((4, a.shape[1] // 2, a.shape[2]), a.dtype) for a in arrs],
        scratch_shapes=[pltpu.SemaphoreType.DMA((n,)), pltpu.SemaphoreType.DMA((n,))],
    )(*arrs)


def core_fill(name, arrs, layer, n_layers):
    n = len(arrs)

    def body(*refs):
        out_refs = refs[n:2 * n]
        send, recv = refs[2 * n:]
        c = lax.axis_index("c")
        peer = (lax.axis_index("x"), lax.axis_index("y"), 1 - c)
        rcs = []
        for a in range(n):
            r = arrs[a].shape[0] // n_layers
            hr = r // 2
            rows = out_refs[a].at[pl.ds(layer * r + c * hr, hr)]
            rc = pltpu.make_async_remote_copy(src_ref=rows, dst_ref=rows, send_sem=send.at[a], recv_sem=recv.at[a],
                                              device_id=peer, device_id_type=MESH)
            rc.start()
            rcs.append(rc)
        for a in range(n):
            r = arrs[a].shape[0] // n_layers
            hr = r // 2
            theirs = out_refs[a].at[pl.ds(layer * r + (1 - c) * hr, hr)]
            pltpu.make_async_remote_copy(src_ref=theirs, dst_ref=theirs, send_sem=send.at[a], recv_sem=recv.at[a],
                                         device_id=peer, device_id_type=MESH).wait_recv()
        for rc in rcs:
            rc.wait_send()

    return pl.pallas_call(
        body, name=name, in_specs=[ANY] * n, out_specs=[ANY] * n,
        out_shape=[jax.ShapeDtypeStruct(a.shape, a.dtype) for a in arrs],
        input_output_aliases={a: a for a in range(n)},
        scratch_shapes=[pltpu.SemaphoreType.DMA((n,)), pltpu.SemaphoreType.DMA((n,))],
    )(*arrs)


def sibling_swap(name, arrs):
    n = len(arrs)

    def body(*refs):
        in_refs = refs[:n]
        out_refs = refs[n:2 * n]
        send, recv = refs[2 * n:]
        peer = (lax.axis_index("x"), lax.axis_index("y"), 1 - lax.axis_index("c"))
        rcs = []
        for a in range(n):
            rc = pltpu.make_async_remote_copy(src_ref=in_refs[a], dst_ref=out_refs[a], send_sem=send.at[a],
                                              recv_sem=recv.at[a], device_id=peer, device_id_type=MESH)
            rc.start()
            rcs.append(rc)
        for rc in rcs:
            rc.wait()

    return pl.pallas_call(
        body, name=name, in_specs=[ANY] * n, out_specs=[ANY] * n,
        out_shape=[jax.ShapeDtypeStruct(a.shape, a.dtype) for a in arrs],
        scratch_shapes=[pltpu.SemaphoreType.DMA((n,)), pltpu.SemaphoreType.DMA((n,))],
    )(*arrs)


def all8_gather(name, v):
    flips = [(fx, fy, fc) for fx in (0, 1) for fy in (0, 1) for fc in (0, 1) if (fx, fy, fc) != (0, 0, 0)]

    def body(v_ref, out_ref, send, recv, loc):
        x, y, c = lax.axis_index("x"), lax.axis_index("y"), lax.axis_index("c")
        me = 4 * x + 2 * y + c
        lc = pltpu.make_async_copy(v_ref, out_ref.at[me], loc)
        lc.start()
        rcs = []
        for k, (fx, fy, fc) in enumerate(flips):
            tgt = (x + fx - 2 * x * fx, y + fy - 2 * y * fy, c + fc - 2 * c * fc)
            rc = pltpu.make_async_remote_copy(src_ref=v_ref, dst_ref=out_ref.at[me], send_sem=send.at[k],
                                              recv_sem=recv.at[k], device_id=tgt, device_id_type=MESH)
            rc.start()
            rcs.append(rc)
        lc.wait()
        for rc in rcs:
            rc.wait()

    return pl.pallas_call(
        body, name=name, in_specs=[ANY], out_specs=ANY,
        out_shape=jax.ShapeDtypeStruct((8,) + v.shape, v.dtype),
        scratch_shapes=[pltpu.SemaphoreType.DMA((7,)), pltpu.SemaphoreType.DMA((7,)), pltpu.SemaphoreType.DMA],
    )(v)


def _pick_rows(rows, cols, target_elems=128 * 1024, mult=SUBLANES):
    if rows % mult != 0:
        return rows
    best = mult
    t = mult
    while t <= rows:
        if rows % t == 0 and t * cols <= target_elems:
            best = t
        t += mult
    return best


def sum_chips(name, parts):
    _, R, C = parts.shape
    tm = _pick_rows(R, C)

    def body(p_ref, o_ref):
        o_ref[...] = (p_ref[0] + p_ref[1]) + (p_ref[2] + p_ref[3])

    return pl.pallas_call(
        body, name=name, grid=(R // tm,),
        in_specs=[pl.BlockSpec((4, tm, C), lambda i: (0, i, 0))],
        out_specs=pl.BlockSpec((tm, C), lambda i: (i, 0)),
        out_shape=jax.ShapeDtypeStruct((R, C), F32),
        compiler_params=_params(dimension_semantics=("arbitrary",)),
    )(parts)


def _adamw(g, w, m, v):
    m = ADAM_B1 * m + (1.0 - ADAM_B1) * g
    v = ADAM_B2 * v + (1.0 - ADAM_B2) * (g * g)
    m_hat = m / (1.0 - ADAM_B1 ** ADAM_STEP)
    v_hat = v / (1.0 - ADAM_B2 ** ADAM_STEP)
    delta = -ADAM_LR * (m_hat / (jnp.sqrt(v_hat) + ADAM_EPS) + ADAM_WD * w)
    return delta, m, v


def adamw_shard(name, s_mine, s_sib, w, m, v):
    R, C = w.shape
    tm = _pick_rows(R, C)

    def body(a_ref, b_ref, w_ref, m_ref, v_ref, g_out, d_out, m_out, v_out):
        g = a_ref[...] + b_ref[...]
        d, mn, vn = _adamw(g, w_ref[...], m_ref[...], v_ref[...])
        g_out[...] = g
        d_out[...] = d
        m_out[...] = mn
        v_out[...] = vn

    spec = pl.BlockSpec((tm, C), lambda i: (i, 0))
    return pl.pallas_call(
        body, name=name, grid=(R // tm,), in_specs=[spec] * 5, out_specs=[spec] * 4,
        out_shape=[jax.ShapeDtypeStruct((R, C), F32)] * 4,
        compiler_params=_params(dimension_semantics=("arbitrary",)),
    )(s_mine, s_sib, w, m, v)


def core_sum(name, core, g, got):
    _, r, C = g.shape
    hr = r // 2
    tm = _pick_rows(hr, 4 * C, 256 * 1024, 2 * SUBLANES)
    nh = hr // tm

    def body(c_ref, g_ref, s_ref, o_ref):
        o_ref[...] = (g_ref[...] + s_ref[...]).astype(BF16)

    return pl.pallas_call(
        body, name=name,
        grid_spec=pltpu.PrefetchScalarGridSpec(
            num_scalar_prefetch=1, grid=(nh,),
            in_specs=[pl.BlockSpec((4, tm, C), lambda i, cr: (0, cr[0] * nh + i, 0)),
                      pl.BlockSpec((4, tm, C), lambda i, cr: (0, i, 0))],
            out_specs=pl.BlockSpec((4, tm, C), lambda i, cr: (0, i, 0))),
        out_shape=jax.ShapeDtypeStruct((4, hr, C), BF16),
        compiler_params=_params(dimension_semantics=("arbitrary",)),
    )(core, g, got)


def chip_sum_into(name, core, parts, layer, n_layers, into=None):
    _, hr, C = parts.shape
    r = 2 * hr
    tm = _pick_rows(hr, 4 * C, 256 * 1024, 2 * SUBLANES)
    nh = hr // tm

    def body(c_ref, p_ref, *rest):
        o_ref = rest[-1]
        o_ref[...] = (_f32(p_ref[0]) + _f32(p_ref[1])) + (_f32(p_ref[2]) + _f32(p_ref[3]))

    in_specs = [pl.BlockSpec((4, tm, C), lambda i, cr: (0, i, 0))]
    args = [core, parts]
    aliases = {}
    if into is not None:
        in_specs.append(pl.BlockSpec(memory_space=pl.ANY))
        args.append(into)
        aliases = {2: 0}
    return pl.pallas_call(
        body, name=name,
        grid_spec=pltpu.PrefetchScalarGridSpec(
            num_scalar_prefetch=1, grid=(nh,), in_specs=in_specs,
            out_specs=pl.BlockSpec((tm, C), lambda i, cr: ((layer * r) // tm + cr[0] * nh + i, 0))),
        out_shape=jax.ShapeDtypeStruct((n_layers * r, C), F32), input_output_aliases=aliases,
        compiler_params=_params(dimension_semantics=("arbitrary",)),
    )(*args)


def adamw_full(name, g, w, m, v):
    R, C = w.shape
    tm = _pick_rows(R, C)

    def body(g_ref, w_ref, m_ref, v_ref, d_out, m_out, v_out):
        d, mn, vn = _adamw(g_ref[...], w_ref[...], m_ref[...], v_ref[...])
        d_out[...] = d
        m_out[...] = mn
        v_out[...] = vn

    spec = pl.BlockSpec((tm, C), lambda i: (i, 0))
    return pl.pallas_call(
        body, name=name, grid=(R // tm,), in_specs=[spec] * 4, out_specs=[spec] * 3,
        out_shape=[jax.ShapeDtypeStruct((R, C), F32)] * 3,
        compiler_params=_params(dimension_semantics=("arbitrary",)),
    )(g, w, m, v)


def adamw_small(name, parts, w, m, v):
    W = w.shape[1]

    def body(p_ref, w_ref, m_ref, v_ref, g_out, d_out, m_out, v_out):
        acc = p_ref[0]
        for k in range(1, 8):
            acc = acc + p_ref[k]
        g = jnp.sum(acc, axis=0, keepdims=True)
        d, mn, vn = _adamw(g, w_ref[...], m_ref[...], v_ref[...])
        g_out[...] = g
        d_out[...] = d
        m_out[...] = mn
        v_out[...] = vn

    return pl.pallas_call(
        body, name=name, out_shape=[jax.ShapeDtypeStruct((1, W), F32)] * 4,
        compiler_params=_params(),
    )(parts, w, m, v)


def _pad_lanes(v, width=LANES):
    return jnp.pad(v, ((0, 0), (0, width - v.shape[1])))


def _layer_fwd(cf, x, xb, pb, W, sm):
    S, D, CD, DI, XBC, F, H, G = cf['S'], cf['D'], cf['CD'], cf['DI'], cf['XBC'], cf['F'], cf['H'], cf['G']
    NM = cf['NM']
    alpha = cf['alpha']
    tm = cf['tm']
    tmx = cf['tmx']
    tn_in = cf['tn_in']
    sv = {}

    ident = lambda acc, ex: ([acc], [])
    proj, = fused_mm("in_proj", [(xb, W['in_main'], 0, False)], [], ident, [(NM, BF16, tn_in, 0)],
                     M=S, tm=tmx, tn=tn_in, nj=NM // tn_in)
    dtraw, = fused_mm("dt_proj", [(xb, W['in_dt'], 0, False)], [], ident, [(LANES, F32, LANES, 0)],
                      M=S, tm=tmx, tn=LANES)

    u, = row_call("glu", lambda a, gt: ([a * _sig(gt)], []),
                  [(proj, 'row', CD, 0, 0), (proj, 'row', CD, 1, 0)], [(CD, F32, CD, 0, 0)], M=S, tm=tm)

    def conv_a_epi(conv, ex):
        cb_, g_, b_ = ex
        ca = conv + cb_
        xhat, _ = _ln_stats(ca)
        la = xhat * g_ + b_
        return [ca, la * _sig(la)], []

    ca, sa = conv_call("conv_a", u, 0, sm['conv_a_w'], cf['KA'], conv_a_epi,
                       [(sm['conv_a_b'], 'vec', CD, 0, 0), (sm['ln_a_g'], 'vec', CD, 0, 0), (sm['ln_a_b'], 'vec', CD, 0, 0)],
                       [(CD, F32, CD, 0, 0), (CD, BF16, CD, 0, 0)], M=S, tm=cf['tmc'], cw=CD, nc=1, reverse=False)
    y_a, = fused_mm("a_out", [(sa, W['a_out'], 0, False)], [], ident, [(D, F32, D, 0)], M=S, tm=tmx, tn=D)

    def conv_b_epi(conv, ex):
        cb = conv + ex[0]
        return [cb, cb * _sig(cb)], []

    xoff = (2 * CD + 2 * D + DI) // DI
    cbv, xbc = conv_call("conv_b", proj, xoff, sm['ssm_conv_w'], cf['KB'], conv_b_epi,
                         [(sm['ssm_conv_b'], 'vec', DI, 0, 1)],
                         [(XBC, F32, DI, 0, 1), (XBC, F32, DI, 0, 1)], M=S, tm=cf['tmc'], cw=DI, nc=XBC // DI,
                         reverse=False)
    y_f, st_f = ssd_fwd("ssd_fwd_f", xbc, dtraw, sm['dtb_f'], sm['alog_f'], S=S, DI=DI, G=G, H=H, rev=False)
    zoff = (2 * CD + 2 * D) // DI
    ysum, st_r, yn = ssd_fwd("ssd_fwd_r", xbc, dtraw, sm['dtb_r'], sm['alog_r'], S=S, DI=DI, G=G, H=H, rev=True,
                             tail=(y_f, (proj, zoff), sm['dskip_full'], sm['ssm_norm_g']))
    goff = (2 * CD) // D

    def merge_epi(acc, ex):
        ga, gb, ya = ex
        return [acc, _sig(ga) * ya + _sig(gb) * acc], []

    y_b, merged = fused_mm("b_out", [(yn, W['b_out'], 0, False)],
                           [(proj, 'row', D, goff), (proj, 'row', D, goff + 1), (y_a, 'row', D, 0)],
                           merge_epi, [(D, F32, D, 0), (D, BF16, D, 0)], M=S, tm=tm, tn=D)

    def mix_epi(acc, ex):
        xin, g_, b_ = ex
        r1 = alpha * xin + acc
        xhat, _ = _ln_stats(r1)
        return [r1, xhat * g_ + b_], []

    r1, hb = fused_mm("o_mix", [(merged, W['o'], 0, False)],
                      [(x, 'row', D, 0), (sm['ln1_g'], 'vec', D, 0), (sm['ln1_b'], 'vec', D, 0)],
                      mix_epi, [(D, F32, D, 0), (D, BF16, D, 0)], M=S, tm=tm, tn=D)

    tnf = cf['tnf']

    g32, g_ = fused_mm("ffn_gate", [(hb, W['gate_up'], 0, False)], [], lambda acc, ex: ([acc, acc], []),
                       [(F, F32, tnf, 0), (F, BF16, tnf, 0)], M=S, tm=tmx, tn=tnf, nj=F // tnf)
    u_, f = fused_mm("ffn_up", [(hb, W['gate_up'], F // tnf, False)], [(g32, 'row', tnf, 0)],
                     lambda acc, ex: ([acc, ex[0] * _sig(ex[0]) * acc], []),
                     [(F, BF16, tnf, 0), (F, BF16, tnf, 0)], M=S, tm=tmx, tn=tnf, nj=F // tnf)

    def down_epi(acc, ex):
        r1_, g1, b1, g2, b2 = ex
        xh1, _ = _ln_stats(r1_)
        r2 = alpha * (xh1 * g1 + b1) + acc
        xh2, _ = _ln_stats(r2)
        return [r2, xh2 * g2 + b2], []

    r2, h2b = fused_mm("ffn_down", [(f, W['down'], 0, False)],
                       [(r1, 'row', D, 0), (sm['ln1_g'], 'vec', D, 0), (sm['ln1_b'], 'vec', D, 0),
                        (sm['ln2_g'], 'vec', D, 0), (sm['ln2_b'], 'vec', D, 0)],
                       down_epi, [(D, F32, D, 0), (D, BF16, D, 0)], M=S, tm=tm, tn=D)

    pe, = fused_mm("ple_proj", [(pb, W['ple'], 0, False)], [], ident, [(D, F32, D, 0)], M=S, tm=tmx, tn=D)

    def ple_epi(acc, ex):
        r2_, g2, b2, pe_, pg = ex
        xh2, _ = _ln_stats(r2_)
        h2 = xh2 * g2 + b2
        e = pe_ * lax.rsqrt(jnp.mean(pe_ * pe_, axis=-1, keepdims=True) + RMS_EPS) * pg
        xn = h2 + e * _sig(acc)
        return [acc, xn, xn], []

    t_, xn, xnb = fused_mm("ple_gate", [(h2b, W['ple_gate'], 0, False)],
                           [(r2, 'row', D, 0), (sm['ln2_g'], 'vec', D, 0), (sm['ln2_b'], 'vec', D, 0),
                            (pe, 'row', D, 0), (sm['ple_norm_g'], 'vec', D, 0)],
                           ple_epi, [(D, F32, D, 0), (D, F32, D, 0), (D, BF16, D, 0)], M=S, tm=tm, tn=D)
    sv.update(x=x, xb=xb, pb=pb, proj=proj, dtraw=dtraw, u=u, ca=ca, sa=sa, y_a=y_a, cbv=cbv, xbc=xbc,
              ysum=ysum, st_f=st_f, st_r=st_r, yn=yn, y_b=y_b, merged=merged, r1=r1, hb=hb,
              g_=g_, u_=u_, f=f, r2=r2, h2b=h2b, t_=t_, pe=pe)
    return xn, xnb, sv


def _layer_bwd(cf, sv, W, sm, dxn=None, target=None, xn=None):
    S, D, CD, DI, XBC, F, H, G = cf['S'], cf['D'], cf['CD'], cf['DI'], cf['XBC'], cf['F'], cf['H'], cf['G']
    NM = cf['NM']
    alpha = cf['alpha']
    tm = cf['tm']
    gw = cf['GW']
    out = {}

    def ple_bwd_core(dx_, t, pe_, pg):
        s = _sig(t)
        rinv = lax.rsqrt(jnp.mean(pe_ * pe_, axis=-1, keepdims=True) + RMS_EPS)
        pn = pe_ * rinv
        e = pn * pg
        dtg = dx_ * e * (s * (1.0 - s))
        de = dx_ * s
        qv = de * pg
        dpe = rinv * (qv - pn * jnp.mean(qv * pn, axis=-1, keepdims=True))
        return dtg, dpe, de * pn

    if dxn is None:
        def head(xn_, tgt, t, pe_, pg):
            err = xn_ - tgt
            dx_ = err * (1.0 / D)
            dtg, dpe, dpg = ple_bwd_core(dx_, t, pe_, pg)
            return [dx_, dtg, dpe], [dpg, err * err]

        (dxn, dtg, dpe, dpg, lsq) = row_call(
            "loss_ple_bwd", head,
            [(xn, 'row', D, 0, 0), (target, 'row', D, 0, 0), (sv['t_'], 'row', D, 0, 0), (sv['pe'], 'row', D, 0, 0),
             (sm['ple_norm_g'], 'vec', D, 0, 0)],
            [(D, F32, D, 0, 0), (D, BF16, D, 0, 0), (D, BF16, D, 0, 0)], [(D, D, 0, 0), (D, D, 0, 0)], M=S, tm=tm)
        out['loss_sq'] = lsq
    else:
        def mid(dx_, t, pe_, pg):
            dtg, dpe, dpg = ple_bwd_core(dx_, t, pe_, pg)
            return [dtg, dpe], [dpg]

        (dtg, dpe, dpg) = row_call(
            "ple_bwd", mid,
            [(dxn, 'row', D, 0, 0), (sv['t_'], 'row', D, 0, 0), (sv['pe'], 'row', D, 0, 0),
             (sm['ple_norm_g'], 'vec', D, 0, 0)],
            [(D, BF16, D, 0, 0), (D, BF16, D, 0, 0)], [(D, D, 0, 0)], M=S, tm=tm)
    out['ple_norm_g'] = dpg

    def ln_bwd_epi(scale):
        def epi(acc, ex):
            res, r_, g_ = ex
            dh = scale * res + acc
            xhat, rstd = _ln_stats(r_)
            dr = _ln_bwd(dh, xhat, rstd, g_)
            return [dr, dr], [dh * xhat, dh]
        return epi

    dr2, dr2b, dg2, db2 = fused_mm(
        "dh2", [(dtg, W['ple_gate_T'], 0, False)],
        [(dxn, 'row', D, 0), (sv['r2'], 'row', D, 0), (sm['ln2_g'], 'vec', D, 0)],
        ln_bwd_epi(1.0), [(D, F32, D, 0), (D, BF16, D, 0)], [(D, D, 0), (D, D, 0)], M=S, tm=tm, tn=D)
    out['ln2_g'], out['ln2_b'] = dg2, db2

    tnf = cf['tnf']

    def dswiglu_epi(acc, ex):
        gg, uu = ex
        s = _sig(gg)
        return [acc * uu * _dsilu(gg, s), acc * (gg * s)], []

    dg_b, du_b = fused_mm(
        "d_down", [(dr2b, W['down_T'], 0, False)],
        [(sv['g_'], 'row', tnf, 0), (sv['u_'], 'row', tnf, 0)], dswiglu_epi,
        [(F, BF16, tnf, 0), (F, BF16, tnf, 0)], M=S, tm=tm, tn=tnf, nj=F // tnf)

    dr1, dr1b, dg1, db1 = fused_mm(
        "dh1", [(dg_b, W['gate_T'], 0, True), (du_b, W['up_T'], 0, True)],
        [(dr2, 'row', D, 0), (sv['r1'], 'row', D, 0), (sm['ln1_g'], 'vec', D, 0)],
        ln_bwd_epi(alpha), [(D, F32, D, 0), (D, BF16, D, 0)], [(D, D, 0), (D, D, 0)],
        M=S, tm=tm, tn=D, nk=cf['nk_f'])
    out['ln1_g'], out['ln1_b'] = dg1, db1

    goff = (2 * CD) // D

    def dmerge_epi(acc, ex):
        ga, gb, ya, yb = ex
        sa_, sb_ = _sig(ga), _sig(gb)
        dga = acc * ya * (sa_ * (1.0 - sa_))
        dgb = acc * yb * (sb_ * (1.0 - sb_))
        return [jnp.concatenate([dga, dgb], axis=1), acc * sa_, acc * sb_], []

    dproj, dya_b, dyb_b = fused_mm(
        "d_merge", [(dr1b, W['o_T'], 0, False)],
        [(sv['proj'], 'row', D, goff), (sv['proj'], 'row', D, goff + 1), (sv['y_a'], 'row', D, 0), (sv['y_b'], 'row', D, 0)],
        dmerge_epi, [(NM, BF16, 2 * D, (2 * CD) // (2 * D)), (D, BF16, D, 0), (D, BF16, D, 0)], M=S, tm=tm, tn=D)

    def dsa_epi(acc, ex):
        ca_, g_, b_ = ex
        xhat, rstd = _ln_stats(ca_)
        la = xhat * g_ + b_
        dla = acc * _dsilu(la, _sig(la))
        dca = _ln_bwd(dla, xhat, rstd, g_)
        return [dca], [dla * xhat, dla, dca]

    dca, dlag, dlab, dcab = fused_mm(
        "d_a_out", [(dya_b, W['a_out_T'], 0, False)],
        [(sv['ca'], 'row', CD, 0), (sm['ln_a_g'], 'vec', CD, 0), (sm['ln_a_b'], 'vec', CD, 0)],
        dsa_epi, [(CD, F32, CD, 0)], [(CD, CD, 0), (CD, CD, 0), (CD, CD, 0)], M=S, tm=tm, tn=D)
    out['ln_a_g'], out['ln_a_b'], out['conv_a_b'] = dlag, dlab, dcab

    def dglu_epi(du, ex):
        a, gt = ex
        s = _sig(gt)
        return [jnp.concatenate([du * s, du * a * (s * (1.0 - s))], axis=1)], []

    dproj, dwa = conv_call(
        "d_conv_a", dca, 0, sm['conv_a_w'], cf['KA'], dglu_epi,
        [(sv['proj'], 'row', CD, 0, 0), (sv['proj'], 'row', CD, 1, 0)],
        [(NM, BF16, 2 * CD, 0, 0)], M=S, tm=cf['tmc'], cw=CD, nc=1, reverse=True, xin=(sv['u'], 0),
        passthrough=(dproj, 0))
    out['conv_a_w'] = dwa

    zoff = (2 * CD + 2 * D) // DI

    def dgate_norm_epi(acc, ex):
        ysum_, xs, z, dsk, ng = ex
        y = ysum_ + xs * dsk
        sz = _sig(z)
        siluz = z * sz
        yz = y * siluz
        dyzs, yhats = [], []
        for g in range(G):
            t = yz[:, g * gw:(g + 1) * gw]
            rinv = lax.rsqrt(jnp.mean(t * t, axis=-1, keepdims=True) + RMS_EPS)
            yh = t * rinv
            qv = acc[:, g * gw:(g + 1) * gw] * ng[:, g * gw:(g + 1) * gw]
            dyzs.append(rinv * (qv - yh * jnp.mean(qv * yh, axis=-1, keepdims=True)))
            yhats.append(yh)
        dyz = jnp.concatenate(dyzs, axis=1)
        yhat = jnp.concatenate(yhats, axis=1)
        dy = dyz * siluz
        dz = dyz * y * _dsilu(z, sz)
        return [dy, dz], [acc * yhat, dy * xs]

    tmr = cf['tmr']
    dy_ssd, dproj, dng, ddsk = fused_mm(
        "d_b_out", [(dyb_b, W['b_out_T'], 0, False)],
        [(sv['ysum'], 'row', DI, 0), (sv['xbc'], 'row', DI, 0), (sv['proj'], 'row', DI, zoff),
         (sm['dskip_full'], 'vec', DI, 0), (sm['ssm_norm_g'], 'vec', DI, 0)],
        dgate_norm_epi, [(DI, F32, DI, 0), (NM, BF16, DI, zoff)], [(DI, DI, 0), (DI, DI, 0)],
        M=S, tm=tmr, tn=DI, passthrough=(dproj, 1))
    out['ssm_norm_g'], out['dskip_full'] = dng, ddsk

    dxbc_f, ddt_f, dA_f = ssd_bwd("ssd_bwd_f", sv['xbc'], sv['dtraw'], dy_ssd, sv['st_f'], sm['dtb_f'], sm['alog_f'],
                                  S=S, DI=DI, G=G, H=H, rev=False)
    dcb, ddt_r, dA_r, dcbb = ssd_bwd("ssd_bwd_r", sv['xbc'], sv['dtraw'], dy_ssd, sv['st_r'], sm['dtb_r'], sm['alog_r'],
                                     S=S, DI=DI, G=G, H=H, rev=True, tail=(dxbc_f, sv['cbv'], sm['dskip_full']))
    out['dA_f'], out['dA_r'] = dA_f, dA_r
    out['ssm_conv_b'] = dcbb

    xoff = (2 * CD + 2 * D + DI) // DI
    dproj, dwb = conv_call(
        "d_conv_b", dcb, 0, sm['ssm_conv_w'], cf['KB'], lambda conv, ex: ([conv], []), [],
        [(NM, BF16, DI, xoff, 1)], M=S, tm=cf['tmc'], cw=DI, nc=XBC // DI, reverse=True, xin=(sv['proj'], xoff),
        passthrough=(dproj, 0))
    out['ssm_conv_w'] = dwb

    ddtb, ddt_bias = row_call("d_dt", lambda a, b: ([a + b], [a + b]),
                              [(ddt_f, 'row', LANES, 0, 0), (ddt_r, 'row', LANES, 0, 0)],
                              [(LANES, BF16, LANES, 0, 0)], [(LANES, LANES, 0, 0)], M=S, tm=tm)
    out['dt_bias'] = ddt_bias

    dx, = fused_mm("d_x", [(dproj, W['in_main_T'], 0, True), (ddtb, W['in_dt_T'], 0, False)],
                   [(dr1, 'row', D, 0)], lambda acc, ex: ([alpha * ex[0] + acc], []),
                   [(D, F32, D, 0)], M=S, tm=cf['tmx'], tn=D, nk=cf['nk_in'])

    tmw = cf['tmw']
    xb = sv['xb']
    out['w_in'] = jnp.concatenate(
        [mm_tn("dw_in", xb, dproj, tm=tmw, tk=D, tn=cf['tn_in']),
         mm_tn("dw_dt", xb, ddtb, tm=tmw, tk=D, tn=LANES)[:, :2 * H]], axis=1)
    out['w_a_out'] = mm_tn("dw_a_out", sv['sa'], dya_b, tm=tmw, tk=CD, tn=D)
    out['w_b_out'] = mm_tn("dw_b_out", sv['yn'], dyb_b, tm=tmw, tk=DI // 2, tn=D)
    out['w_o'] = mm_tn("dw_o", sv['merged'], dr1b, tm=tmw, tk=D, tn=D)
    out['w_gate_up'] = jnp.concatenate(
        [mm_tn("dw_gate", sv['hb'], dg_b, tm=tmw, tk=D, tn=tnf),
         mm_tn("dw_up", sv['hb'], du_b, tm=tmw, tk=D, tn=tnf)], axis=1)
    out['w_down'] = mm_tn("dw_down", sv['f'], dr2b, tm=tmw, tk=tnf, tn=D)
    out['w_ple'] = mm_tn("dw_ple", sv['pb'], dpe, tm=tmw, tk=sv['pb'].shape[1], tn=D)
    out['w_ple_gate'] = mm_tn("dw_ple_gate", sv['h2b'], dtg, tm=tmw, tk=D, tn=D)
    return dx, out


_WEIGHTS = ['w_in', 'conv_a_w', 'conv_a_b', 'ln_a_g', 'ln_a_b', 'w_a_out', 'ssm_conv_w', 'ssm_conv_b', 'a_log',
            'dt_bias', 'd_skip', 'ssm_norm_g', 'w_b_out', 'w_o', 'ln1_g', 'ln1_b', 'w_gate_up', 'w_down', 'ln2_g',
            'ln2_b', 'w_ple', 'ple_norm_g', 'w_ple_gate']
_COL_SHARDED = ['w_in', 'conv_a_w', 'ssm_conv_w', 'w_gate_up', 'w_ple']
_ROW_SHARDED = ['w_a_out', 'w_b_out', 'w_o', 'w_down', 'w_ple_gate']
_BIG = _COL_SHARDED + _ROW_SHARDED
_SMALL = [n for n in _WEIGHTS if n not in _BIG]
_CONV = ['conv_a_w', 'ssm_conv_w']


def _ceil_to(n, k):
    return -(-n // k) * k


def kernel(x, p, w_in, conv_a_w, conv_a_b, ln_a_g, ln_a_b, w_a_out, ssm_conv_w, ssm_conv_b, a_log, dt_bias, d_skip, ssm_norm_g, w_b_out, w_o, ln1_g, ln1_b, w_gate_up, w_down, ln2_g, ln2_b, w_ple, ple_norm_g, w_ple_gate, loss_target, m_w_in, m_conv_a_w, m_conv_a_b, m_ln_a_g, m_ln_a_b, m_w_a_out, m_ssm_conv_w, m_ssm_conv_b, m_a_log, m_dt_bias, m_d_skip, m_ssm_norm_g, m_w_b_out, m_w_o, m_ln1_g, m_ln1_b, m_w_gate_up, m_w_down, m_ln2_g, m_ln2_b, m_w_ple, m_ple_norm_g, m_w_ple_gate, v_w_in, v_conv_a_w, v_conv_a_b, v_ln_a_g, v_ln_a_b, v_w_a_out, v_ssm_conv_w, v_ssm_conv_b, v_a_log, v_dt_bias, v_d_skip, v_ssm_norm_g, v_w_b_out, v_w_o, v_ln1_g, v_ln1_b, v_w_gate_up, v_w_down, v_ln2_g, v_ln2_b, v_w_ple, v_ple_norm_g, v_w_ple_gate):
    wt = dict(w_in=w_in, conv_a_w=conv_a_w, conv_a_b=conv_a_b, ln_a_g=ln_a_g, ln_a_b=ln_a_b, w_a_out=w_a_out,
              ssm_conv_w=ssm_conv_w, ssm_conv_b=ssm_conv_b, a_log=a_log, dt_bias=dt_bias, d_skip=d_skip,
              ssm_norm_g=ssm_norm_g, w_b_out=w_b_out, w_o=w_o, ln1_g=ln1_g, ln1_b=ln1_b, w_gate_up=w_gate_up,
              w_down=w_down, ln2_g=ln2_g, ln2_b=ln2_b, w_ple=w_ple, ple_norm_g=ple_norm_g, w_ple_gate=w_ple_gate)
    mo = dict(w_in=m_w_in, conv_a_w=m_conv_a_w, conv_a_b=m_conv_a_b, ln_a_g=m_ln_a_g, ln_a_b=m_ln_a_b,
              w_a_out=m_w_a_out, ssm_conv_w=m_ssm_conv_w, ssm_conv_b=m_ssm_conv_b, a_log=m_a_log,
              dt_bias=m_dt_bias, d_skip=m_d_skip, ssm_norm_g=m_ssm_norm_g, w_b_out=m_w_b_out, w_o=m_w_o,
              ln1_g=m_ln1_g, ln1_b=m_ln1_b, w_gate_up=m_w_gate_up, w_down=m_w_down, ln2_g=m_ln2_g, ln2_b=m_ln2_b,
              w_ple=m_w_ple, ple_norm_g=m_ple_norm_g, w_ple_gate=m_w_ple_gate)
    vo = dict(w_in=v_w_in, conv_a_w=v_conv_a_w, conv_a_b=v_conv_a_b, ln_a_g=v_ln_a_g, ln_a_b=v_ln_a_b,
              w_a_out=v_w_a_out, ssm_conv_w=v_ssm_conv_w, ssm_conv_b=v_ssm_conv_b, a_log=v_a_log,
              dt_bias=v_dt_bias, d_skip=v_d_skip, ssm_norm_g=v_ssm_norm_g, w_b_out=v_w_b_out, w_o=v_w_o,
              ln1_g=v_ln1_g, ln1_b=v_ln1_b, w_gate_up=v_w_gate_up, w_down=v_w_down, ln2_g=v_ln2_g, ln2_b=v_ln2_b,
              w_ple=v_w_ple, ple_norm_g=v_ple_norm_g, w_ple_gate=v_w_ple_gate)

    L = w_in.shape[0]
    S, D = x.shape[1], x.shape[2]
    CD = conv_a_b.shape[1]
    DI = ssm_norm_g.shape[1]
    XBC = ssm_conv_b.shape[1]
    H = d_skip.shape[1]
    G = (XBC - DI) // (2 * D_STATE)
    F = w_down.shape[1] * 4
    N_IN = w_in.shape[2] * 4
    NM = N_IN - 2 * H
    KA, KB = conv_a_w.shape[1], ssm_conv_w.shape[1]
    assert DI == H * HEAD_DIM and CD == D and DI == 2 * D and XBC == 2 * DI and NM == 2 * CD + 2 * D + DI + XBC
    assert 2 * H <= LANES and S % CHUNK == 0
    tnf = F // 2
    cf = dict(S=S, D=D, CD=CD, DI=DI, XBC=XBC, F=F, H=H, G=G, NM=NM, KA=KA, KB=KB, GW=(H // G) * HEAD_DIM,
              alpha=float((2 * L) ** 0.25), tm=min(512, S), tmx=min(1024, S), tmc=min(256, S), tmr=min(256, S), tmw=min(1024, S),
              tn_in=D, tnf=tnf, nk_f=2, nk_in=NM // DI)

    core = lax.axis_index("c").astype(jnp.int32).reshape(1)
    split_names = [n for n in _BIG if n not in _CONV]

    def layer_weights(l):
        got = gather_layer("gather_weights", [wt[n][l].astype(BF16) for n in split_names], [wt[n][l] for n in _CONV])
        full = {}
        for n, g in zip(split_names + _CONV, got):
            if n in _COL_SHARDED:
                full[n] = g.transpose(1, 0, 2).reshape(g.shape[1], 4 * g.shape[2])
            else:
                full[n] = g.reshape(4 * g.shape[1], g.shape[2])
        win = full['w_in']
        in_main = win[:, :NM]
        in_dt = _pad_lanes(win[:, NM:])
        gu = full['w_gate_up']
        W = dict(in_main=in_main, in_dt=in_dt, in_main_T=in_main.T, in_dt_T=in_dt.T,
                 a_out=full['w_a_out'], a_out_T=full['w_a_out'].T,
                 b_out=full['w_b_out'], b_out_T=full['w_b_out'].T,
                 o=full['w_o'], o_T=full['w_o'].T, gate_up=gu, gate_T=gu[:, :F].T, up_T=gu[:, F:].T,
                 down=full['w_down'], down_T=full['w_down'].T, ple=full['w_ple'],
                 ple_gate=full['w_ple_gate'], ple_gate_T=full['w_ple_gate'].T)
        row = lambda v: v.reshape(1, -1)
        head_table = lambda v: jnp.broadcast_to(jnp.pad(v, (0, LANES - H))[:, None], (LANES, LANES))
        sm = dict(conv_a_w=jnp.pad(full['conv_a_w'], ((0, _ceil_to(KA, SUBLANES) - KA), (0, 0))),
                  ssm_conv_w=jnp.pad(full['ssm_conv_w'], ((0, _ceil_to(KB, SUBLANES) - KB), (0, 0))),
                  conv_a_b=row(conv_a_b[l]), ln_a_g=row(ln_a_g[l]), ln_a_b=row(ln_a_b[l]),
                  ssm_conv_b=row(ssm_conv_b[l]), ssm_norm_g=row(ssm_norm_g[l]),
                  ln1_g=row(ln1_g[l]), ln1_b=row(ln1_b[l]), ln2_g=row(ln2_g[l]), ln2_b=row(ln2_b[l]),
                  ple_norm_g=row(ple_norm_g[l]),
                  dtb_f=head_table(dt_bias[l, 0]), dtb_r=head_table(dt_bias[l, 1]),
                  alog_f=head_table(a_log[l, 0]), alog_r=head_table(a_log[l, 1]),
                  dskip_full=row(jnp.repeat(d_skip[l], HEAD_DIM)))
        return W, sm

    def blocks(n, gl):
        g = gl[n]
        if n == 'conv_a_w':
            g = g.sum(axis=1)[:KA]
        elif n == 'ssm_conv_w':
            g = g.sum(axis=1)[:KB]
        if n in _COL_SHARDED:
            return g.reshape(g.shape[0], 4, g.shape[1] // 4).transpose(1, 0, 2)
        return g.reshape(4, g.shape[0] // 4, g.shape[1])

    def reduce_layer(l, gl, acc):
        mine = [blocks(n, gl) for n in split_names]
        theirs = core_send_half("core_send_half", mine)
        both = [core_sum("core_sum_" + n, core, b, t) for n, b, t in zip(split_names, mine, theirs)]
        parts = chip_exchange("scatter_grads", [[t] for t in both], gather=False)
        sums = [chip_sum_into("chip_sum_" + n, core, pr.reshape(4, pr.shape[2], pr.shape[3]), l, L, into=acc.get(n))
                for n, pr in zip(split_names, parts)]
        return dict(zip(split_names, core_fill("core_fill", sums, l, L)))

    lw = [layer_weights(l) for l in range(L)]
    xl = x[0]
    xlb = xl.astype(BF16)
    saved = []
    for l in range(L):
        xl, xlb, sv = _layer_fwd(cf, xl, xlb, p[l, 0].astype(BF16), lw[l][0], lw[l][1])
        saved.append(sv)
    grads = [None] * L
    dxl = None
    gsum = {}
    for l in reversed(range(L)):
        if l == L - 1:
            dxl, grads[l] = _layer_bwd(cf, saved[l], lw[l][0], lw[l][1], target=loss_target[0], xn=xl)
        else:
            dxl, grads[l] = _layer_bwd(cf, saved[l], lw[l][0], lw[l][1], dxn=dxl)
        gsum = reduce_layer(l, grads[l], gsum)
    loss = lax.psum(0.5 / D * jnp.sum(grads[L - 1]['loss_sq']), ("x", "y", "c"))
    grad_x = dxl[None]

    res = {}
    for n in split_names:
        shp = wt[n].shape
        flat = lambda a: a.reshape(shp[0] * shp[1], shp[2])
        outs = adamw_full("adamw_" + n, gsum[n], flat(wt[n]), flat(mo[n]), flat(vo[n]))
        res[n] = [o.reshape(shp) for o in [gsum[n]] + list(outs)]
    parts = chip_exchange("scatter_conv", [[blocks(n, grads[l]) for l in range(L)] for n in _CONV], gather=False)
    chip_sums = [sum_chips("chip_sum_" + n, pr.reshape(4, L * pr.shape[2], pr.shape[3])) for n, pr in zip(_CONV, parts)]
    sib_sums = sibling_swap("core_swap", chip_sums)
    for n, mine, sib in zip(_CONV, chip_sums, sib_sums):
        shp = wt[n].shape
        flat = lambda a: a.reshape(shp[0] * shp[1], shp[2])
        outs = adamw_shard("adamw_" + n, mine, sib, flat(wt[n]), flat(mo[n]), flat(vo[n]))
        res[n] = [o.reshape(shp) for o in outs]

    def small_pieces(l):
        gl = grads[l]
        A = -jnp.exp(a_log[l])
        d = dict(gl)
        d_alog = jnp.concatenate([gl['dA_f'].sum(axis=1)[:H] * A[0], gl['dA_r'].sum(axis=1)[:H] * A[1]])
        d['a_log'] = jnp.pad(d_alog[None], ((0, SUBLANES - 1), (0, 0)))
        d['dt_bias'] = gl['dt_bias'][:, :2 * H]
        d['d_skip'] = gl['dskip_full'].reshape(SUBLANES, H, HEAD_DIM).sum(axis=-1)
        return [_pad_lanes(d[n], _ceil_to(d[n].shape[1], LANES)) for n in _SMALL]

    widths = [_ceil_to(math.prod(wt[n].shape[1:]), LANES) for n in _SMALL]
    packed = jnp.concatenate([pc for l in range(L) for pc in small_pieces(l)], axis=1)
    gathered = all8_gather("gather_small", packed)

    def pack_params(src):
        return jnp.concatenate([_pad_lanes(src[n][l].reshape(1, -1), wd) for l in range(L) for n, wd in zip(_SMALL, widths)],
                               axis=1)

    small_out = adamw_small("adamw_small", gathered, pack_params(wt), pack_params(mo), pack_params(vo))
    off = 0
    per = {n: [[] for _ in range(4)] for n in _SMALL}
    for l in range(L):
        for n, wd in zip(_SMALL, widths):
            size = math.prod(wt[n].shape[1:])
            for k in range(4):
                per[n][k].append(small_out[k][0, off:off + size].reshape(wt[n].shape[1:]))
            off += wd
    for n in _SMALL:
        res[n] = [jnp.stack(per[n][k]) for k in range(4)]

    return (loss, grad_x, *[res[n][0] for n in _WEIGHTS], *[res[n][1] for n in _WEIGHTS],
            *[res[n][2] for n in _WEIGHTS], *[res[n][3] for n in _WEIGHTS])
```

```python
import math

import jax
import jax.numpy as jnp
from jax import lax
from jax.experimental import pallas as pl
from jax.experimental.pallas import tpu as pltpu

F32 = jnp.float32
BF16 = jnp.bfloat16

VMEM_LIMIT_BYTES = 56 * 1024 * 1024
LANES = 128
SUBLANES = 8

CHUNK = 128
D_STATE = 128
HEAD_DIM = 64
LN_EPS = 1e-5
RMS_EPS = 1e-6
ADAM_LR = 0.001
ADAM_B1 = 0.9
ADAM_B2 = 0.999
ADAM_EPS = 1e-08
ADAM_WD = 0.01
ADAM_STEP = 10
HALO = 16
MESH = pl.DeviceIdType.MESH


def _params(**kw):
    return pltpu.CompilerParams(vmem_limit_bytes=VMEM_LIMIT_BYTES, **kw)


def _sig(x):
    return jax.nn.sigmoid(x)


def _dsilu(x, s):
    return s * (1.0 + x * (1.0 - s))


def _ln_stats(r):
    mu = jnp.mean(r, axis=-1, keepdims=True)
    xc = r - mu
    var = jnp.mean(xc * xc, axis=-1, keepdims=True)
    rstd = lax.rsqrt(var + LN_EPS)
    return xc * rstd, rstd


def _ln_bwd(dy, xhat, rstd, g):
    dxh = dy * g
    m1 = jnp.mean(dxh, axis=-1, keepdims=True)
    m2 = jnp.mean(dxh * xhat, axis=-1, keepdims=True)
    return rstd * (dxh - m1 - xhat * m2)


def _f32(v):
    return v if v.dtype == F32 else v.astype(F32)


def _rows8(v):
    tm, w = v.shape
    return v.reshape(tm // SUBLANES, SUBLANES, w).sum(axis=0)


def fused_mm(name, prods, extras, epi, row_outs, col_outs=(), *, M, tm, tn, nj=1, nk=1,
             passthrough=None, t_outs=()):
    np_ = len(prods)
    ne = len(extras)
    nro = len(row_outs)
    nco = len(col_outs)
    use_acc = nk > 1

    def body(*refs):
        a_refs = [refs[2 * p] for p in range(np_)]
        w_refs = [refs[2 * p + 1] for p in range(np_)]
        pos = 2 * np_
        e_refs = refs[pos:pos + ne]
        pos += ne
        if passthrough is not None:
            pos += 1
        ro_refs = refs[pos:pos + nro]
        pos += nro
        co_refs = refs[pos:pos + nco]
        pos += nco
        to_refs = refs[pos:pos + len(t_outs)]
        pos += len(t_outs)
        acc_ref = refs[pos] if use_acc else None
        i = pl.program_id(1)
        k = pl.program_id(2)

        def prod(p):
            a = a_refs[p][...]
            if a.dtype != BF16:
                a = a.astype(BF16)
            return jnp.dot(a, w_refs[p][...], preferred_element_type=F32)

        def finish(acc):
            res = epi(acc, [_f32(r[...]) for r in e_refs])
            rows, cols = res[0], res[1]
            for v, o in zip(rows, ro_refs):
                o[...] = v.astype(o.dtype)
            for v, o in zip(res[2] if len(res) > 2 else (), to_refs):
                o[...] = v.T.astype(o.dtype)
            for v, o in zip(cols, co_refs):
                v8 = _rows8(v)

                @pl.when(i == 0)
                def _():
                    o[...] = v8

                @pl.when(i > 0)
                def _():
                    o[...] += v8

        if not use_acc:
            acc = prod(0)
            for p in range(1, np_):
                acc = acc + prod(p)
            finish(acc)
        else:
            @pl.when(k == 0)
            def _():
                acc = None
                for p in range(np_):
                    acc = prod(p) if acc is None else acc + prod(p)
                acc_ref[...] = acc

            @pl.when(k > 0)
            def _():
                acc = None
                for p in range(np_):
                    if prods[p][3]:
                        acc = prod(p) if acc is None else acc + prod(p)
                acc_ref[...] += acc

            @pl.when(k == nk - 1)
            def _():
                finish(acc_ref[...])

    in_specs = []
    args = []
    for a, w, joff, ksplit in prods:
        K = a.shape[1]
        if ksplit:
            tk = K // nk
            in_specs.append(pl.BlockSpec((tm, tk), lambda j, i, k: (i, k)))
            in_specs.append(pl.BlockSpec((tk, tn), lambda j, i, k, joff=joff: (k, j + joff)))
        else:
            in_specs.append(pl.BlockSpec((tm, K), lambda j, i, k: (i, 0)))
            in_specs.append(pl.BlockSpec((K, tn), lambda j, i, k, joff=joff: (0, j + joff)))
        args += [a, w]
    for arr, kind, width, c0 in extras:
        if kind == 'row':
            in_specs.append(pl.BlockSpec((tm, width), lambda j, i, k, c0=c0: (i, c0 + j)))
        else:
            in_specs.append(pl.BlockSpec((arr.shape[0], width), lambda j, i, k, c0=c0: (0, c0 + j)))
        args.append(arr)
    aliases = {}
    if passthrough is not None:
        arr, oidx = passthrough
        in_specs.append(pl.BlockSpec(memory_space=pl.ANY))
        aliases = {len(args): oidx}
        args.append(arr)
    out_shape = []
    out_specs = []
    for n_total, dtype, width, c0 in row_outs:
        out_shape.append(jax.ShapeDtypeStruct((M, n_total), dtype))
        out_specs.append(pl.BlockSpec((tm, width), lambda j, i, k, c0=c0: (i, c0 + j)))
    for n_total, width, c0 in col_outs:
        out_shape.append(jax.ShapeDtypeStruct((SUBLANES, n_total), F32))
        out_specs.append(pl.BlockSpec((SUBLANES, width), lambda j, i, k, c0=c0: (0, c0 + j)))
    for n_total, dtype, width, c0 in t_outs:
        out_shape.append(jax.ShapeDtypeStruct((n_total, M), dtype))
        out_specs.append(pl.BlockSpec((width, tm), lambda j, i, k, c0=c0: (c0 + j, i)))
    scratch = [pltpu.VMEM((tm, tn), F32)] if use_acc else []
    return pl.pallas_call(
        body, name=name, grid=(nj, M // tm, nk), in_specs=in_specs, out_specs=out_specs,
        out_shape=out_shape, scratch_shapes=scratch, input_output_aliases=aliases,
        compiler_params=_params(dimension_semantics=("arbitrary", "arbitrary", "arbitrary")),
    )(*args)


def mm_tn(name, a, b, *, tm, tk, tn):
    M, K = a.shape
    N = b.shape[1]

    def body(a_ref, b_ref, o_ref):
        m = pl.program_id(2)
        p = lax.dot_general(a_ref[...], b_ref[...], (((0,), (0,)), ((), ())),
                            preferred_element_type=F32)

        @pl.when(m == 0)
        def _():
            o_ref[...] = p

        @pl.when(m > 0)
        def _():
            o_ref[...] += p

    return pl.pallas_call(
        body, name=name, grid=(K // tk, N // tn, M // tm),
        in_specs=[pl.BlockSpec((tm, tk), lambda kk, j, m: (m, kk)),
                  pl.BlockSpec((tm, tn), lambda kk, j, m: (m, j))],
        out_specs=pl.BlockSpec((tk, tn), lambda kk, j, m: (kk, j)),
        out_shape=jax.ShapeDtypeStruct((K, N), F32),
        compiler_params=_params(dimension_semantics=("arbitrary", "arbitrary", "arbitrary")),
    )(a, b)


def row_call(name, fn, ins, row_outs, col_outs=(), *, M, tm, nc=1):
    ni = len(ins)
    nro = len(row_outs)

    def body(*refs):
        i = pl.program_id(1)
        vals = [_f32(r[...]) for r in refs[:ni]]
        rows, cols = fn(*vals)
        for v, o in zip(rows, refs[ni:ni + nro]):
            o[...] = v.astype(o.dtype)
        for v, o in zip(cols, refs[ni + nro:]):
            v8 = _rows8(v)

            @pl.when(i == 0)
            def _():
                o[...] = v8

            @pl.when(i > 0)
            def _():
                o[...] += v8

    in_specs = []
    for arr, kind, width, c0, cmul in ins:
        if kind == 'row':
            in_specs.append(pl.BlockSpec((tm, width), lambda cj, i, c0=c0, cmul=cmul: (i, c0 + cmul * cj)))
        else:
            in_specs.append(pl.BlockSpec((arr.shape[0], width), lambda cj, i, c0=c0, cmul=cmul: (0, c0 + cmul * cj)))
    out_shape = []
    out_specs = []
    for n_total, dtype, width, c0, cmul in row_outs:
        out_shape.append(jax.ShapeDtypeStruct((M, n_total), dtype))
        out_specs.append(pl.BlockSpec((tm, width), lambda cj, i, c0=c0, cmul=cmul: (i, c0 + cmul * cj)))
    for n_total, width, c0, cmul in col_outs:
        out_shape.append(jax.ShapeDtypeStruct((SUBLANES, n_total), F32))
        out_specs.append(pl.BlockSpec((SUBLANES, width), lambda cj, i, c0=c0, cmul=cmul: (0, c0 + cmul * cj)))
    return pl.pallas_call(
        body, name=name, grid=(nc, M // tm), in_specs=in_specs, out_specs=out_specs,
        out_shape=out_shape,
        compiler_params=_params(dimension_semantics=("arbitrary", "arbitrary")),
    )(*[a[0] for a in ins])


def conv_call(name, src, src_c0, w, K, epi, extras, row_outs, col_outs=(), *, M, tm, cw, nc,
              reverse, xin=None, passthrough=None, t_outs=(), w_c0=0):
    pad = (K - 1) // 2
    assert pad <= HALO - 1
    R = tm // HALO
    nblk = M // HALO
    n_i = M // tm
    Kp = w.shape[0]
    ne = len(extras)
    nro = len(row_outs)
    nco = len(col_outs)
    rb = 64
    cbw = min(cw, 256)
    n_copies = SUBLANES if K > SUBLANES else 1

    def body(*refs):
        main_ref, prev_ref, next_ref, w_ref = refs[:4]
        pos = 4
        xin_ref = None
        if xin is not None:
            xin_ref = refs[pos]
            pos += 1
        e_refs = refs[pos:pos + ne]
        pos += ne
        if passthrough is not None:
            pos += 1
        ro_refs = refs[pos:pos + nro]
        pos += nro
        co_refs = refs[pos:pos + nco]
        pos += nco
        to_refs = refs[pos:pos + len(t_outs)]
        pos += len(t_outs)
        dw_ref = None
        if xin is not None:
            dw_ref = refs[pos]
            pos += 1
        ext_ref, conv_ref = refs[pos], refs[pos + 1]
        i = pl.program_id(1)

        ext_ref[0, 0:HALO, :] = jnp.where(i == 0, 0.0, prev_ref[...].astype(F32))
        ext_ref[0, HALO:HALO + tm, :] = main_ref[...].astype(F32)
        ext_ref[0, HALO + tm:, :] = jnp.where(i == n_i - 1, 0.0, next_ref[...].astype(F32))
        if dw_ref is not None:
            @pl.when(i == 0)
            def _():
                dw_ref[...] = jnp.zeros_like(dw_ref)

        n_sh = tm + 2 * HALO - SUBLANES
        for c0 in range(0, cw, cbw):
            for sft in range(1, n_copies):
                ext_ref[sft, 0:n_sh, c0:c0 + cbw] = ext_ref[0, sft:sft + n_sh, c0:c0 + cbw]

        for c0 in range(0, cw, cbw):
            for r0 in range(0, tm, rb):
                acc = jnp.zeros((rb, cbw), F32)
                if xin_ref is not None:
                    xblk = xin_ref[r0:r0 + rb, c0:c0 + cbw].astype(F32)
                for k in range(K):
                    off = HALO + r0 + ((pad - k) if reverse else (k - pad))
                    sft = off % SUBLANES if n_copies > 1 else 0
                    d = ext_ref[sft, off - sft:off - sft + rb, c0:c0 + cbw]
                    acc = acc + d * w_ref[k:k + 1, c0:c0 + cbw]
                    if xin_ref is not None:
                        dw_ref[k, :, c0:c0 + cbw] += _rows8(xblk * d)
                conv_ref[r0:r0 + rb, c0:c0 + cbw] = acc

        res = epi(conv_ref[...], [_f32(r[...]) for r in e_refs])
        rows, cols = res[0], res[1]
        for v, o in zip(rows, ro_refs):
            o[...] = v.astype(o.dtype)
        for v, o in zip(res[2] if len(res) > 2 else (), to_refs):
            o[...] = v.T.astype(o.dtype)
        for v, o in zip(cols, co_refs):
            v8 = _rows8(v)

            @pl.when(i == 0)
            def _():
                o[...] = v8

            @pl.when(i > 0)
            def _():
                o[...] += v8

    in_specs = [
        pl.BlockSpec((tm, cw), lambda cj, i: (i, src_c0 + cj)),
        pl.BlockSpec((HALO, cw), lambda cj, i: (jnp.maximum(i * R - 1, 0), src_c0 + cj)),
        pl.BlockSpec((HALO, cw), lambda cj, i: (jnp.minimum((i + 1) * R, nblk - 1), src_c0 + cj)),
        pl.BlockSpec((Kp, cw), lambda cj, i: (0, w_c0 + cj)),
    ]
    args = [src, src, src, w]
    if xin is not None:
        in_specs.append(pl.BlockSpec((tm, cw), lambda cj, i, c0=xin[1]: (i, c0 + cj)))
        args.append(xin[0])
    for arr, kind, width, c0, cmul in extras:
        if kind == 'row':
            in_specs.append(pl.BlockSpec((tm, width), lambda cj, i, c0=c0, cmul=cmul: (i, c0 + cmul * cj)))
        else:
            in_specs.append(pl.BlockSpec((arr.shape[0], width), lambda cj, i, c0=c0, cmul=cmul: (0, c0 + cmul * cj)))
        args.append(arr)
    aliases = {}
    if passthrough is not None:
        in_specs.append(pl.BlockSpec(memory_space=pl.ANY))
        aliases = {len(args): passthrough[1]}
        args.append(passthrough[0])
    out_shape = []
    out_specs = []
    for n_total, dtype, width, c0, cmul in row_outs:
        out_shape.append(jax.ShapeDtypeStruct((M, n_total), dtype))
        out_specs.append(pl.BlockSpec((tm, width), lambda cj, i, c0=c0, cmul=cmul: (i, c0 + cmul * cj)))
    for n_total, width, c0, cmul in col_outs:
        out_shape.append(jax.ShapeDtypeStruct((SUBLANES, n_total), F32))
        out_specs.append(pl.BlockSpec((SUBLANES, width), lambda cj, i, c0=c0, cmul=cmul: (0, c0 + cmul * cj)))
    for n_total, dtype, width, c0, cmul in t_outs:
        out_shape.append(jax.ShapeDtypeStruct((n_total, M), dtype))
        out_specs.append(pl.BlockSpec((width, tm), lambda cj, i, c0=c0, cmul=cmul: (c0 + cmul * cj, i)))
    if xin is not None:
        out_shape.append(jax.ShapeDtypeStruct((Kp, SUBLANES, cw * nc), F32))
        out_specs.append(pl.BlockSpec((Kp, SUBLANES, cw), lambda cj, i: (0, 0, cj)))
    return pl.pallas_call(
        body, name=name, grid=(nc, n_i), in_specs=in_specs, out_specs=out_specs,
        out_shape=out_shape, input_output_aliases=aliases,
        scratch_shapes=[pltpu.VMEM((n_copies, tm + 2 * HALO, cw), F32), pltpu.VMEM((tm, cw), F32)],
        compiler_params=_params(dimension_semantics=("arbitrary", "arbitrary")),
    )(*args)


def _split_dot(m_bf16, v, n_pass, dims=None):
    out = None
    rest = v
    for p in range(n_pass):
        piece = rest.astype(BF16)
        if p + 1 < n_pass:
            rest = rest - piece.astype(F32)
        if dims is None:
            t = jnp.dot(m_bf16, piece, preferred_element_type=F32)
        else:
            t = lax.dot_general(m_bf16, piece, dims, preferred_element_type=F32)
        out = t if out is None else out + t
    return out


def _split_dot_r(v, m_bf16, n_pass):
    out = None
    rest = v
    for p in range(n_pass):
        piece = rest.astype(BF16)
        if p + 1 < n_pass:
            rest = rest - piece.astype(F32)
        t = jnp.dot(piece, m_bf16, preferred_element_type=F32)
        out = t if out is None else out + t
    return out


def _softplus(x):
    return jnp.maximum(x, 0.0) + jnp.log1p(jnp.exp(-jnp.abs(x)))


NT_DIMS = (((1,), (1,)), ((), ()))
TN_DIMS = (((0,), (0,)), ((), ()))


def _ssd_common(dtraw, dtbT, alogT, rev, n_heads):
    L = CHUNK
    if rev:
        dtraw = pltpu.roll(dtraw, LANES - n_heads, 1)
    preT = dtraw.T + dtbT
    dtT = _softplus(preT)
    AT = -jnp.exp(alogT)
    aT = dtT * AT
    ri = lax.broadcasted_iota(jnp.int32, (L, L), 0)
    ci = lax.broadcasted_iota(jnp.int32, (L, L), 1)
    up = (ri >= ci) if rev else (ri <= ci)
    lo = (ri <= ci) if rev else (ri >= ci)
    csT = _split_dot_r(aT, up.astype(BF16), 3)
    last = 0 if rev else L - 1
    lastB = jnp.broadcast_to(csT[:, last:last + 1], (L, L))
    return dict(preT=preT, dtT=dtT, AT=AT, csT=csT, cs=csT.T, up=up, lo=lo, ci=ci, last=last,
                doutT=jnp.exp(csT), dstT=jnp.exp(lastB - csT), totB=jnp.exp(lastB))


def ssd_fwd(name, xsT, bc, dtraw, dtbT, alogT, *, S, DI, G, H, rev, tail=None):
    NC = S // CHUNK
    R = H // G
    GW = R * HEAD_DIM
    N = D_STATE
    P = HEAD_DIM

    def body(*refs):
        xsT_ref, bc_ref, dtraw_ref, dtb_ref, alog_ref = refs[:5]
        if tail is None:
            y_ref, st_ref, h_ref = refs[5:]
        else:
            yo_ref, z_ref, xs_ref, dsk_ref, ng_ref = refs[5:10]
            y_ref, st_ref, yn_ref, h_ref = refs[10:]
        c = pl.program_id(0)

        @pl.when(c == 0)
        def _():
            h_ref[...] = jnp.zeros_like(h_ref)

        q = _ssd_common(dtraw_ref[...], dtb_ref[...], alog_ref[...], rev, H)
        cs, csT, dtT, doutT, totB = q['cs'], q['csT'], q['dtT'], q['doutT'], q['totB']
        wstT = q['dstT'] * dtT
        for g in range(G):
            Bg = bc_ref[:, g * N:(g + 1) * N].astype(BF16)
            Cg = bc_ref[:, G * N + g * N:G * N + (g + 1) * N].astype(BF16)
            CBT = lax.dot_general(Bg, Cg, NT_DIMS, preferred_element_type=F32)
            HT = h_ref[g]
            yoffT = lax.dot_general(HT.astype(BF16), Cg, NT_DIMS, preferred_element_type=F32)
            xT = xsT_ref[g * GW:(g + 1) * GW, :]
            hs = [g * R + r for r in range(R)]
            blks = [slice(r * P, (r + 1) * P) for r in range(R)]
            segs = [jnp.where(q['up'], csT[h:h + 1, :] - cs[:, h:h + 1], -1e30) for h in hs]
            GTs = [(CBT * jnp.exp(sg)).astype(BF16) for sg in segs]
            xThs = [xT[b, :] for b in blks]
            XThs = [(xTh * dtT[h:h + 1, :]).astype(BF16) for xTh, h in zip(xThs, hs)]
            ydTs = [jnp.dot(a, GT, preferred_element_type=F32) for a, GT in zip(XThs, GTs)]
            ys = [ydT + yoffT[b, :] * doutT[h:h + 1, :] for ydT, b, h in zip(ydTs, blks, hs)]
            xws = [xTh * wstT[h:h + 1, :] for xTh, h in zip(xThs, hs)]
            tots = [jnp.broadcast_to(totB[h:h + 1, :], (P, N)) for h in hs]
            y_ref[:, g * GW:(g + 1) * GW] = jnp.concatenate(ys, axis=0).T
            xwT = jnp.concatenate(xws, axis=0).astype(BF16)
            ST = jnp.dot(xwT, Bg, preferred_element_type=F32)
            st_ref[0, g] = HT
            h_ref[g] = HT * jnp.concatenate(tots, axis=0) + ST
        if tail is not None:
            y = y_ref[...] + yo_ref[...]
            y_ref[...] = y
            z = _f32(z_ref[...])
            yz = (y + xs_ref[...] * dsk_ref[...]) * (z * _sig(z))
            for g in range(G):
                t = yz[:, g * GW:(g + 1) * GW]
                tn = t * lax.rsqrt(jnp.mean(t * t, axis=-1, keepdims=True) + RMS_EPS)
                yn_ref[:, g * GW:(g + 1) * GW] = (tn * ng_ref[:, g * GW:(g + 1) * GW]).astype(BF16)

    cidx = (lambda c: NC - 1 - c) if rev else (lambda c: c)
    cmap = lambda c: (cidx(c), 0)
    smap = lambda c: (cidx(c), 0, 0, 0)
    const = lambda c: (0, 0)
    tmap = lambda c: (0, cidx(c))
    in_specs = [pl.BlockSpec((DI, CHUNK), tmap), pl.BlockSpec((CHUNK, 2 * G * N), cmap), pl.BlockSpec((CHUNK, LANES), cmap),
                pl.BlockSpec((LANES, LANES), const), pl.BlockSpec((LANES, LANES), const)]
    out_specs = [pl.BlockSpec((CHUNK, DI), cmap), pl.BlockSpec((1, G, GW, N), smap)]
    out_shape = [jax.ShapeDtypeStruct((S, DI), F32), jax.ShapeDtypeStruct((NC, G, GW, N), F32)]
    args = [xsT, bc, dtraw, dtbT, alogT]
    if tail is not None:
        y_other, (z_arr, z_blk), xs_row, dsk, ng = tail
        in_specs += [pl.BlockSpec((CHUNK, DI), cmap), pl.BlockSpec((CHUNK, DI), lambda c: (cidx(c), z_blk)),
                     pl.BlockSpec((CHUNK, DI), cmap), pl.BlockSpec((1, DI), const), pl.BlockSpec((1, DI), const)]
        out_specs.append(pl.BlockSpec((CHUNK, DI), cmap))
        out_shape.append(jax.ShapeDtypeStruct((S, DI), BF16))
        args += [y_other, z_arr, xs_row, dsk, ng]
    return pl.pallas_call(
        body, name=name, grid=(NC,), in_specs=in_specs, out_specs=out_specs, out_shape=out_shape,
        scratch_shapes=[pltpu.VMEM((G, GW, N), F32)],
        compiler_params=_params(dimension_semantics=("arbitrary",)),
    )(*args)


def ssd_bwd(name, xsT, bc, dtraw, dyT, st, dtbT, alogT, *, S, DI, G, H, rev, tail=None):
    NC = S // CHUNK
    R = H // G
    GW = R * HEAD_DIM
    N = D_STATE
    XBC = DI + 2 * G * N
    P = HEAD_DIM
    L = CHUNK

    def body(*refs):
        xsT_ref, bc_ref, dtraw_ref, dyT_ref, st_ref, dtb_ref, alog_ref = refs[:7]
        if tail is None:
            dxbc_ref, ddt_ref, da_ref, dh_ref, dcst_ref, p2t_ref, p3t_ref, e2t_ref = refs[7:]
        else:
            other_ref, cbx_ref, cbbc_ref, dskT_ref = refs[7:11]
            dxbc_ref, ddt_ref, da_ref, dcol_ref, dh_ref, dcst_ref, p2t_ref, p3t_ref, e2t_ref = refs[11:]
        c = pl.program_id(0)

        @pl.when(c == 0)
        def _():
            dh_ref[...] = jnp.zeros_like(dh_ref)
            da_ref[...] = jnp.zeros_like(da_ref)
            dcst_ref[...] = jnp.zeros_like(dcst_ref)
            p2t_ref[...] = jnp.zeros_like(p2t_ref)
            p3t_ref[...] = jnp.zeros_like(p3t_ref)
            e2t_ref[...] = jnp.zeros_like(e2t_ref)

        q = _ssd_common(dtraw_ref[...], dtb_ref[...], alog_ref[...], rev, H)
        cs, csT, dtT, doutT, dstT, totB = q['cs'], q['csT'], q['dtT'], q['doutT'], q['dstT'], q['totB']
        wstT = dstT * dtT
        lane = q['ci']
        dcs_c = jnp.zeros((L, LANES), F32)
        for g in range(G):
            Bg = bc_ref[:, g * N:(g + 1) * N].astype(BF16)
            Cg = bc_ref[:, G * N + g * N:G * N + (g + 1) * N].astype(BF16)
            CB = lax.dot_general(Cg, Bg, NT_DIMS, preferred_element_type=F32)
            HpT = st_ref[0, g]
            HpTb = HpT.astype(BF16)
            dHT = dh_ref[g]
            dHTb = dHT.astype(BF16)
            BdHT = lax.dot_general(dHTb, Bg, NT_DIMS, preferred_element_type=F32)
            yoffT = lax.dot_general(HpTb, Cg, NT_DIMS, preferred_element_type=F32)
            xT = xsT_ref[g * GW:(g + 1) * GW, :]
            dyT = dyT_ref[g * GW:(g + 1) * GW, :]
            hs = [g * R + r for r in range(R)]
            blks = [slice(r * P, (r + 1) * P) for r in range(R)]
            Lms = [jnp.exp(jnp.where(q['lo'], cs[:, h:h + 1] - csT[h:h + 1, :], -1e30)) for h in hs]
            xThs = [xT[b, :] for b in blks]
            dyThs = [dyT[b, :] for b in blks]
            xThbs = [v.astype(BF16) for v in xThs]
            dyThbs = [v.astype(BF16) for v in dyThs]
            dGxs = [lax.dot_general(a, b, TN_DIMS, preferred_element_type=F32) for a, b in zip(dyThbs, xThbs)]
            Gms = [(CB * Lm).astype(BF16) for Lm in Lms]
            u1Ts = [jnp.dot(a, Gm, preferred_element_type=F32) for a, Gm in zip(dyThbs, Gms)]
            Ts = [dGx * (Lm * dtT[h:h + 1, :]) for dGx, Lm, h in zip(dGxs, Lms, hs)]
            dCB = Ts[0]
            for T in Ts[1:]:
                dCB = dCB + T
            Msegs = [T * CB for T in Ts]
            for h, Mseg in zip(hs, Msegs):
                dcs_c = jnp.where(lane == h, jnp.sum(Mseg, axis=1, keepdims=True), dcs_c)
            uTs = [u1T + BdHT[b, :] * dstT[h:h + 1, :] for u1T, b, h in zip(u1Ts, blks, hs)]
            dyds = [dyTh * doutT[h:h + 1, :] for dyTh, h in zip(dyThs, hs)]
            xws = [xTh * wstT[h:h + 1, :] for xTh, h in zip(xThs, hs)]
            for r, h in enumerate(hs):
                b = blks[r]
                p3row = jnp.sum(xws[r] * BdHT[b, :], axis=0, keepdims=True)
                dcst_ref[h:h + 1, :] = (jnp.sum(dyds[r] * yoffT[b, :], axis=0, keepdims=True)
                                        - jnp.sum(Msegs[r], axis=0, keepdims=True) - p3row)
                p2t_ref[h:h + 1, :] = jnp.sum(xThs[r] * uTs[r], axis=0, keepdims=True)
                p3t_ref[h:h + 1, :] = p3row
                e2t_ref[h:h + 1, :] = jnp.sum(HpT[b, :] * dHT[b, :], axis=0, keepdims=True)
            dxs = [uT * dtT[h:h + 1, :] for uT, h in zip(uTs, hs)]
            if tail is not None:
                dxs = [d + dyTh * dskT_ref[g * GW + r * P:g * GW + (r + 1) * P, :]
                       for r, (d, dyTh) in enumerate(zip(dxs, dyThs))]
            tots = [jnp.broadcast_to(totB[h:h + 1, :], (P, N)) for h in hs]
            dxbc_ref[:, g * GW:(g + 1) * GW] = jnp.concatenate(dxs, axis=0).T
            dydT = jnp.concatenate(dyds, axis=0).astype(BF16)
            xwT = jnp.concatenate(xws, axis=0).astype(BF16)
            dCBb = dCB.astype(BF16)
            dC = (jnp.dot(dCBb, Bg, preferred_element_type=F32)
                  + lax.dot_general(dydT, HpTb, TN_DIMS, preferred_element_type=F32))
            dB = (lax.dot_general(dCBb, Cg, TN_DIMS, preferred_element_type=F32)
                  + lax.dot_general(xwT, dHTb, TN_DIMS, preferred_element_type=F32))
            dxbc_ref[:, DI + g * N:DI + (g + 1) * N] = dB
            dxbc_ref[:, DI + G * N + g * N:DI + G * N + (g + 1) * N] = dC
            dh_ref[g] = (dHT * jnp.concatenate(tots, axis=0)
                         + jnp.dot(dydT, Cg, preferred_element_type=F32))
        e1 = jnp.sum(p3t_ref[...], axis=1, keepdims=True)
        e2 = jnp.sum(e2t_ref[...], axis=1, keepdims=True)
        dcsT = (dcst_ref[...] + dcs_c.T
                + jnp.where(lane == q['last'], e1 + totB * e2, 0.0))
        daT = _split_dot_r(dcsT, q['lo'].astype(BF16), 3)
        ddtT = daT * q['AT'] + p2t_ref[...]
        da_ref[...] += daT * dtT
        ddraw = jnp.where(lane < H, (ddtT * _sig(q['preT'])).T, 0.0)
        if rev:
            ddraw = pltpu.roll(ddraw, H, 1)
        ddt_ref[...] = ddraw
        if tail is not None:
            for c0, cb_ref in ((0, cbx_ref), (DI, cbbc_ref)):
                d = dxbc_ref[:, c0:c0 + DI] + other_ref[:, c0:c0 + DI]
                cb = cb_ref[...]
                dcb = d * _dsilu(cb, _sig(cb))
                dxbc_ref[:, c0:c0 + DI] = dcb
                part = _rows8(dcb)

                @pl.when(c == 0)
                def _():
                    dcol_ref[:, c0:c0 + DI] = part

                @pl.when(c > 0)
                def _():
                    dcol_ref[:, c0:c0 + DI] += part

    cmap = (lambda c: (c, 0)) if rev else (lambda c: (NC - 1 - c, 0))
    smap = (lambda c: (c, 0, 0, 0)) if rev else (lambda c: (NC - 1 - c, 0, 0, 0))
    const = lambda c: (0, 0)
    sq = pltpu.VMEM((LANES, CHUNK), F32)
    cix = (lambda c: c) if rev else (lambda c: NC - 1 - c)
    tmap = lambda c: (0, cix(c))
    in_specs = [pl.BlockSpec((DI, CHUNK), tmap), pl.BlockSpec((CHUNK, 2 * G * N), cmap), pl.BlockSpec((CHUNK, LANES), cmap),
                pl.BlockSpec((DI, CHUNK), tmap),
                pl.BlockSpec((1, G, GW, N), smap),
                pl.BlockSpec((LANES, LANES), const), pl.BlockSpec((LANES, LANES), const)]
    out_specs = [pl.BlockSpec((CHUNK, XBC), cmap), pl.BlockSpec((CHUNK, LANES), cmap),
                 pl.BlockSpec((LANES, LANES), const)]
    out_shape = [jax.ShapeDtypeStruct((S, XBC), F32), jax.ShapeDtypeStruct((S, LANES), F32),
                 jax.ShapeDtypeStruct((LANES, LANES), F32)]
    args = [xsT, bc, dtraw, dyT, st, dtbT, alogT]
    if tail is not None:
        in_specs += [pl.BlockSpec((CHUNK, XBC), cmap), pl.BlockSpec((CHUNK, DI), cmap),
                     pl.BlockSpec((CHUNK, 2 * G * N), cmap), pl.BlockSpec((DI, LANES), const)]
        out_specs.append(pl.BlockSpec((SUBLANES, XBC), const))
        out_shape.append(jax.ShapeDtypeStruct((SUBLANES, XBC), F32))
        args += list(tail)
    return pl.pallas_call(
        body, name=name, grid=(NC,), in_specs=in_specs, out_specs=out_specs, out_shape=out_shape,
        scratch_shapes=[pltpu.VMEM((G, GW, N), F32), sq, sq, sq, sq],
        compiler_params=_params(dimension_semantics=("arbitrary",)),
    )(*args)


ANY = pl.BlockSpec(memory_space=pl.ANY)


def chip_exchange(name, groups, gather):
    flat = [arr for grp in groups for arr in grp]
    n_in = len(flat)
    n_out = len(groups)
    n_rc = 3 * n_in

    def body(*refs):
        in_refs = refs[:n_in]
        out_refs = refs[n_in:n_in + n_out]
        send, recv, loc = refs[n_in + n_out:]
        x, y, c = lax.axis_index("x"), lax.axis_index("y"), lax.axis_index("c")
        me = 2 * x + y
        peers = [(1 - x, y), (x, 1 - y), (1 - x, 1 - y)]
        local, remote = [], []
        q = 0
        for a, grp in enumerate(groups):
            for l in range(len(grp)):
                src = in_refs[q]
                dst = out_refs[a].at[me] if gather else out_refs[a].at[me, l]
                own = src if gather else src.at[me]
                lc = pltpu.make_async_copy(own, dst, loc.at[q])
                lc.start()
                local.append(lc)
                for j, (px, py) in enumerate(peers):
                    blk = src if gather else src.at[2 * px + py]
                    rc = pltpu.make_async_remote_copy(
                        src_ref=blk, dst_ref=dst, send_sem=send.at[3 * q + j], recv_sem=recv.at[3 * q + j],
                        device_id=(px, py, c), device_id_type=MESH)
                    rc.start()
                    remote.append(rc)
                q += 1
        for lc in local:
            lc.wait()
        for rc in remote:
            rc.wait()

    out_shape = []
    for grp in groups:
        a0 = grp[0]
        if gather:
            out_shape.append(jax.ShapeDtypeStruct((4,) + a0.shape, a0.dtype))
        else:
            out_shape.append(jax.ShapeDtypeStruct((4, len(grp)) + a0.shape[1:], a0.dtype))
    return pl.pallas_call(
        body, name=name, in_specs=[ANY] * n_in, out_specs=[ANY] * n_out, out_shape=out_shape,
        scratch_shapes=[pltpu.SemaphoreType.DMA((n_rc,)), pltpu.SemaphoreType.DMA((n_rc,)),
                        pltpu.SemaphoreType.DMA((n_in,))],
    )(*flat)


def gather_layer(name, split, whole):
    ns, nw = len(split), len(whole)
    n = ns + nw
    n_rc = 3 * (n + ns)

    def body(*refs):
        in_refs = refs[:n]
        out_refs = refs[n:2 * n]
        send, recv, loc = refs[2 * n:]
        x, y, c = lax.axis_index("x"), lax.axis_index("y"), lax.axis_index("c")
        me = 2 * x + y
        sibling = (x, y, 1 - c)
        peers = [(1 - x, y), (x, 1 - y), (1 - x, 1 - y)]

        def region(a, chip, half):
            if a >= ns:
                return out_refs[a].at[chip]
            hr = split[a].shape[0] // 2
            return out_refs[a].at[chip, pl.ds(half * hr, hr)]

        def mine(a):
            if a >= ns:
                return in_refs[a]
            hr = split[a].shape[0] // 2
            return in_refs[a].at[pl.ds(c * hr, hr)]

        local = []
        for a in range(n):
            lc = pltpu.make_async_copy(in_refs[a], out_refs[a].at[me], loc.at[a])
            lc.start()
            local.append(lc)
        sends = []
        for a in range(n):
            for j, (px, py) in enumerate(peers):
                rc = pltpu.make_async_remote_copy(
                    src_ref=mine(a), dst_ref=region(a, me, c), send_sem=send.at[3 * a + j],
                    recv_sem=recv.at[3 * a + j], device_id=(px, py, c), device_id_type=MESH)
                rc.start()
                sends.append(rc)
        for a in range(n):
            for j, (px, py) in enumerate(peers):
                chip = 2 * px + py
                landed = pltpu.make_async_remote_copy(
                    src_ref=mine(a), dst_ref=region(a, chip, c), send_sem=send.at[3 * a + j],
                    recv_sem=recv.at[3 * a + j], device_id=(px, py, c), device_id_type=MESH)
                landed.wait_recv()
                if a < ns:
                    fw = pltpu.make_async_remote_copy(
                        src_ref=region(a, chip, c), dst_ref=region(a, chip, c), send_sem=send.at[3 * n + 3 * a + j],
                        recv_sem=recv.at[3 * n + 3 * a + j], device_id=sibling, device_id_type=MESH)
                    fw.start()
                    sends.append(fw)
        for a in range(ns):
            for j, (px, py) in enumerate(peers):
                chip = 2 * px + py
                pltpu.make_async_remote_copy(
                    src_ref=region(a, chip, 1 - c), dst_ref=region(a, chip, 1 - c), send_sem=send.at[3 * n + 3 * a + j],
                    recv_sem=recv.at[3 * n + 3 * a + j], device_id=sibling, device_id_type=MESH).wait_recv()
        for rc in sends:
            rc.wait_send()
        for lc in local:
            lc.wait()

    arrs = list(split) + list(whole)
    return pl.pallas_call(
        body, name=name, in_specs=[ANY] * n, out_specs=[ANY] * n,
        out_shape=[jax.ShapeDtypeStruct((4,) + a.shape, a.dtype) for a in arrs],
        scratch_shapes=[pltpu.SemaphoreType.DMA((n_rc,)), pltpu.SemaphoreType.DMA((n_rc,)),
                        pltpu.SemaphoreType.DMA((n,))],
    )(*arrs)


def core_send_half(name, arrs):
    n = len(arrs)

    def body(*refs):
        in_refs = refs[:n]
        out_refs = refs[n:2 * n]
        send, recv = refs[2 * n:]
        c = lax.axis_index("c")
        peer = (lax.axis_index("x"), lax.axis_index("y"), 1 - c)
        rcs = []
        for a in range(n):
            hr = arrs[a].shape[1] // 2
            rc = pltpu.make_async_remote_copy(
                src_ref=in_refs[a].at[:, pl.ds((1 - c) * hr, hr)], dst_ref=out_refs[a], send_sem=send.at[a],
                recv_sem=recv.at[a], device_id=peer, device_id_type=MESH)
            rc.start()
            rcs.append(rc)
        for rc in rcs:
            rc.wait()

    return pl.pallas_call(
        body, name=name, in_specs=[ANY] * n, out_specs=[ANY] * n,
        out_shape=[jax.ShapeDtypeStruct((4, a.shape[1] // 2, a.shape[2]), a.dtype) for a in arrs],
        scratch_shapes=[pltpu.SemaphoreType.DMA((n,)), pltpu.SemaphoreType.DMA((n,))],
    )(*arrs)


def core_fill(name, arrs, layer, n_layers):
    n = len(arrs)

    def body(*refs):
        out_refs = refs[n:2 * n]
        send, recv = refs[2 * n:]
        c = lax.axis_index("c")
        peer = (lax.axis_index("x"), lax.axis_index("y"), 1 - c)
        rcs = []
        for a in range(n):
            r = arrs[a].shape[0] // n_layers
            hr = r // 2
            rows = out_refs[a].at[pl.ds(layer * r + c * hr, hr)]
            rc = pltpu.make_async_remote_copy(src_ref=rows, dst_ref=rows, send_sem=send.at[a], recv_sem=recv.at[a],
                                              device_id=peer, device_id_type=MESH)
            rc.start()
            rcs.append(rc)
        for a in range(n):
            r = arrs[a].shape[0] // n_layers
            hr = r // 2
            theirs = out_refs[a].at[pl.ds(layer * r + (1 - c) * hr, hr)]
            pltpu.make_async_remote_copy(src_ref=theirs, dst_ref=theirs, send_sem=send.at[a], recv_sem=recv.at[a],
                                         device_id=peer, device_id_type=MESH).wait_recv()
        for rc in rcs:
            rc.wait_send()

    return pl.pallas_call(
        body, name=name, in_specs=[ANY] * n, out_specs=[ANY] * n,
        out_shape=[jax.ShapeDtypeStruct(a.shape, a.dtype) for a in arrs],
        input_output_aliases={a: a for a in range(n)},
        scratch_shapes=[pltpu.SemaphoreType.DMA((n,)), pltpu.SemaphoreType.DMA((n,))],
    )(*arrs)


def sibling_swap(name, arrs):
    n = len(arrs)

    def body(*refs):
        in_refs = refs[:n]
        out_refs = refs[n:2 * n]
        send, recv = refs[2 * n:]
        peer = (lax.axis_index("x"), lax.axis_index("y"), 1 - lax.axis_index("c"))
        rcs = []
        for a in range(n):
            rc = pltpu.make_async_remote_copy(src_ref=in_refs[a], dst_ref=out_refs[a], send_sem=send.at[a],
                                              recv_sem=recv.at[a], device_id=peer, device_id_type=MESH)
            rc.start()
            rcs.append(rc)
        for rc in rcs:
            rc.wait()

    return pl.pallas_call(
        body, name=name, in_specs=[ANY] * n, out_specs=[ANY] * n,
        out_shape=[jax.ShapeDtypeStruct(a.shape, a.dtype) for a in arrs],
        scratch_shapes=[pltpu.SemaphoreType.DMA((n,)), pltpu.SemaphoreType.DMA((n,))],
    )(*arrs)


def all8_gather(name, v):
    flips = [(fx, fy, fc) for fx in (0, 1) for fy in (0, 1) for fc in (0, 1) if (fx, fy, fc) != (0, 0, 0)]

    def body(v_ref, out_ref, send, recv, loc):
        x, y, c = lax.axis_index("x"), lax.axis_index("y"), lax.axis_index("c")
        me = 4 * x + 2 * y + c
        lc = pltpu.make_async_copy(v_ref, out_ref.at[me], loc)
        lc.start()
        rcs = []
        for k, (fx, fy, fc) in enumerate(flips):
            tgt = (x + fx - 2 * x * fx, y + fy - 2 * y * fy, c + fc - 2 * c * fc)
            rc = pltpu.make_async_remote_copy(src_ref=v_ref, dst_ref=out_ref.at[me], send_sem=send.at[k],
                                              recv_sem=recv.at[k], device_id=tgt, device_id_type=MESH)
            rc.start()
            rcs.append(rc)
        lc.wait()
        for rc in rcs:
            rc.wait()

    return pl.pallas_call(
        body, name=name, in_specs=[ANY], out_specs=ANY,
        out_shape=jax.ShapeDtypeStruct((8,) + v.shape, v.dtype),
        scratch_shapes=[pltpu.SemaphoreType.DMA((7,)), pltpu.SemaphoreType.DMA((7,)), pltpu.SemaphoreType.DMA],
    )(v)


def _pick_rows(rows, cols, target_elems=128 * 1024, mult=SUBLANES):
    if rows % mult != 0:
        return rows
    best = mult
    t = mult
    while t <= rows:
        if rows % t == 0 and t * cols <= target_elems:
            best = t
        t += mult
    return best


def sum_chips(name, parts):
    _, R, C = parts.shape
    tm = _pick_rows(R, C)

    def body(p_ref, o_ref):
        o_ref[...] = (p_ref[0] + p_ref[1]) + (p_ref[2] + p_ref[3])

    return pl.pallas_call(
        body, name=name, grid=(R // tm,),
        in_specs=[pl.BlockSpec((4, tm, C), lambda i: (0, i, 0))],
        out_specs=pl.BlockSpec((tm, C), lambda i: (i, 0)),
        out_shape=jax.ShapeDtypeStruct((R, C), F32),
        compiler_params=_params(dimension_semantics=("arbitrary",)),
    )(parts)


def _adamw(g, w, m, v):
    m = ADAM_B1 * m + (1.0 - ADAM_B1) * g
    v = ADAM_B2 * v + (1.0 - ADAM_B2) * (g * g)
    m_hat = m / (1.0 - ADAM_B1 ** ADAM_STEP)
    v_hat = v / (1.0 - ADAM_B2 ** ADAM_STEP)
    delta = -ADAM_LR * (m_hat / (jnp.sqrt(v_hat) + ADAM_EPS) + ADAM_WD * w)
    return delta, m, v


def adamw_shard(name, s_mine, s_sib, w, m, v):
    R, C = w.shape
    tm = _pick_rows(R, C)

    def body(a_ref, b_ref, w_ref, m_ref, v_ref, g_out, d_out, m_out, v_out):
        g = a_ref[...] + b_ref[...]
        d, mn, vn = _adamw(g, w_ref[...], m_ref[...], v_ref[...])
        g_out[...] = g
        d_out[...] = d
        m_out[...] = mn
        v_out[...] = vn

    spec = pl.BlockSpec((tm, C), lambda i: (i, 0))
    return pl.pallas_call(
        body, name=name, grid=(R // tm,), in_specs=[spec] * 5, out_specs=[spec] * 4,
        out_shape=[jax.ShapeDtypeStruct((R, C), F32)] * 4,
        compiler_params=_params(dimension_semantics=("arbitrary",)),
    )(s_mine, s_sib, w, m, v)


def core_sum(name, core, g, got):
    _, r, C = g.shape
    hr = r // 2
    tm = _pick_rows(hr, 4 * C, 256 * 1024, 2 * SUBLANES)
    nh = hr // tm

    def body(c_ref, g_ref, s_ref, o_ref):
        o_ref[...] = (g_ref[...] + s_ref[...]).astype(BF16)

    return pl.pallas_call(
        body, name=name,
        grid_spec=pltpu.PrefetchScalarGridSpec(
            num_scalar_prefetch=1, grid=(nh,),
            in_specs=[pl.BlockSpec((4, tm, C), lambda i, cr: (0, cr[0] * nh + i, 0)),
                      pl.BlockSpec((4, tm, C), lambda i, cr: (0, i, 0))],
            out_specs=pl.BlockSpec((4, tm, C), lambda i, cr: (0, i, 0))),
        out_shape=jax.ShapeDtypeStruct((4, hr, C), BF16),
        compiler_params=_params(dimension_semantics=("arbitrary",)),
    )(core, g, got)


def chip_sum_into(name, core, parts, layer, n_layers, into=None):
    _, hr, C = parts.shape
    r = 2 * hr
    tm = _pick_rows(hr, 4 * C, 256 * 1024, 2 * SUBLANES)
    nh = hr // tm

    def body(c_ref, p_ref, *rest):
        o_ref = rest[-1]
        o_ref[...] = (_f32(p_ref[0]) + _f32(p_ref[1])) + (_f32(p_ref[2]) + _f32(p_ref[3]))

    in_specs = [pl.BlockSpec((4, tm, C), lambda i, cr: (0, i, 0))]
    args = [core, parts]
    aliases = {}
    if into is not None:
        in_specs.append(pl.BlockSpec(memory_space=pl.ANY))
        args.append(into)
        aliases = {2: 0}
    return pl.pallas_call(
        body, name=name,
        grid_spec=pltpu.PrefetchScalarGridSpec(
            num_scalar_prefetch=1, grid=(nh,), in_specs=in_specs,
            out_specs=pl.BlockSpec((tm, C), lambda i, cr: ((layer * r) // tm + cr[0] * nh + i, 0))),
        out_shape=jax.ShapeDtypeStruct((n_layers * r, C), F32), input_output_aliases=aliases,
        compiler_params=_params(dimension_semantics=("arbitrary",)),
    )(*args)


def adamw_full(name, g, w, m, v):
    R, C = w.shape
    tm = _pick_rows(R, C)

    def body(g_ref, w_ref, m_ref, v_ref, d_out, m_out, v_out):
        d, mn, vn = _adamw(g_ref[...], w_ref[...], m_ref[...], v_ref[...])
        d_out[...] = d
        m_out[...] = mn
        v_out[...] = vn

    spec = pl.BlockSpec((tm, C), lambda i: (i, 0))
    return pl.pallas_call(
        body, name=name, grid=(R // tm,), in_specs=[spec] * 4, out_specs=[spec] * 3,
        out_shape=[jax.ShapeDtypeStruct((R, C), F32)] * 3,
        compiler_params=_params(dimension_semantics=("arbitrary",)),
    )(g, w, m, v)


def adamw_small(name, parts, w, m, v):
    W = w.shape[1]

    def body(p_ref, w_ref, m_ref, v_ref, g_out, d_out, m_out, v_out):
        acc = p_ref[0]
        for k in range(1, 8):
            acc = acc + p_ref[k]
        g = jnp.sum(acc, axis=0, keepdims=True)
        d, mn, vn = _adamw(g, w_ref[...], m_ref[...], v_ref[...])
        g_out[...] = g
        d_out[...] = d
        m_out[...] = mn
        v_out[...] = vn

    return pl.pallas_call(
        body, name=name, out_shape=[jax.ShapeDtypeStruct((1, W), F32)] * 4,
        compiler_params=_params(),
    )(parts, w, m, v)


def _pad_lanes(v, width=LANES):
    return jnp.pad(v, ((0, 0), (0, width - v.shape[1])))


def _layer_fwd(cf, x, xb, pb, W, sm):
    S, D, CD, DI, XBC, F, H, G = cf['S'], cf['D'], cf['CD'], cf['DI'], cf['XBC'], cf['F'], cf['H'], cf['G']
    NM = cf['NM']
    alpha = cf['alpha']
    tm = cf['tm']
    tmx = cf['tmx']
    tn_in = cf['tn_in']
    sv = {}

    ident = lambda acc, ex: ([acc], [])
    proj, = fused_mm("in_proj", [(xb, W['in_main'], 0, False)], [], ident, [(NM, BF16, tn_in, 0)],
                     M=S, tm=tmx, tn=tn_in, nj=NM // tn_in)
    dtraw, = fused_mm("dt_proj", [(xb, W['in_dt'], 0, False)], [], ident, [(LANES, F32, LANES, 0)],
                      M=S, tm=tmx, tn=LANES)

    u, = row_call("glu", lambda a, gt: ([a * _sig(gt)], []),
                  [(proj, 'row', CD, 0, 0), (proj, 'row', CD, 1, 0)], [(CD, F32, CD, 0, 0)], M=S, tm=tm)

    def conv_a_epi(conv, ex):
        cb_, g_, b_ = ex
        ca = conv + cb_
        xhat, _ = _ln_stats(ca)
        la = xhat * g_ + b_
        return [ca, la * _sig(la)], []

    ca, sa = conv_call("conv_a", u, 0, sm['conv_a_w'], cf['KA'], conv_a_epi,
                       [(sm['conv_a_b'], 'vec', CD, 0, 0), (sm['ln_a_g'], 'vec', CD, 0, 0), (sm['ln_a_b'], 'vec', CD, 0, 0)],
                       [(CD, F32, CD, 0, 0), (CD, BF16, CD, 0, 0)], M=S, tm=cf['tmc'], cw=CD, nc=1, reverse=False)
    y_a, = fused_mm("a_out", [(sa, W['a_out'], 0, False)], [], ident, [(D, F32, D, 0)], M=S, tm=tmx, tn=D)

    def conv_x_epi(conv, ex):
        cb = conv + ex[0]
        act = cb * _sig(cb)
        return [cb, act], [], [act]

    def conv_bc_epi(conv, ex):
        cb = conv + ex[0]
        return [cb, cb * _sig(cb)], []

    xoff = (2 * CD + 2 * D + DI) // DI
    cbv_x, xs, xsT = conv_call("conv_b_x", proj, xoff, sm['ssm_conv_w'], cf['KB'], conv_x_epi,
                               [(sm['ssm_conv_b'], 'vec', DI, 0, 0)],
                               [(DI, F32, DI, 0, 0), (DI, F32, DI, 0, 0)], M=S, tm=cf['tmc'], cw=DI, nc=1,
                               reverse=False, t_outs=[(DI, F32, DI, 0, 0)])
    cbv_bc, bc = conv_call("conv_b_bc", proj, xoff + 1, sm['ssm_conv_w'], cf['KB'], conv_bc_epi,
                           [(sm['ssm_conv_b'], 'vec', DI, 1, 0)],
                           [(DI, F32, DI, 0, 0), (DI, F32, DI, 0, 0)], M=S, tm=cf['tmc'], cw=DI, nc=1,
                           reverse=False, w_c0=1)
    y_f, st_f = ssd_fwd("ssd_fwd_f", xsT, bc, dtraw, sm['dtb_f'], sm['alog_f'], S=S, DI=DI, G=G, H=H, rev=False)
    zoff = (2 * CD + 2 * D) // DI
    ysum, st_r, yn = ssd_fwd("ssd_fwd_r", xsT, bc, dtraw, sm['dtb_r'], sm['alog_r'], S=S, DI=DI, G=G, H=H, rev=True,
                             tail=(y_f, (proj, zoff), xs, sm['dskip_full'], sm['ssm_norm_g']))
    goff = (2 * CD) // D

    def merge_epi(acc, ex):
        ga, gb, ya = ex
        return [acc, _sig(ga) * ya + _sig(gb) * acc], []

    y_b, merged = fused_mm("b_out", [(yn, W['b_out'], 0, False)],
                           [(proj, 'row', D, goff), (proj, 'row', D, goff + 1), (y_a, 'row', D, 0)],
                           merge_epi, [(D, F32, D, 0), (D, BF16, D, 0)], M=S, tm=tm, tn=D)

    def mix_epi(acc, ex):
        xin, g_, b_ = ex
        r1 = alpha * xin + acc
        xhat, _ = _ln_stats(r1)
        return [r1, xhat * g_ + b_], []

    r1, hb = fused_mm("o_mix", [(merged, W['o'], 0, False)],
                      [(x, 'row', D, 0), (sm['ln1_g'], 'vec', D, 0), (sm['ln1_b'], 'vec', D, 0)],
                      mix_epi, [(D, F32, D, 0), (D, BF16, D, 0)], M=S, tm=tm, tn=D)

    tnf = cf['tnf']

    g32, g_ = fused_mm("ffn_gate", [(hb, W['gate_up'], 0, False)], [], lambda acc, ex: ([acc, acc], []),
                       [(F, F32, tnf, 0), (F, BF16, tnf, 0)], M=S, tm=tmx, tn=tnf, nj=F // tnf)
    u_, f = fused_mm("ffn_up", [(hb, W['gate_up'], F // tnf, False)], [(g32, 'row', tnf, 0)],
                     lambda acc, ex: ([acc, ex[0] * _sig(ex[0]) * acc], []),
                     [(F, BF16, tnf, 0), (F, BF16, tnf, 0)], M=S, tm=tmx, tn=tnf, nj=F // tnf)

    def down_epi(acc, ex):
        r1_, g1, b1, g2, b2 = ex
        xh1, _ = _ln_stats(r1_)
        r2 = alpha * (xh1 * g1 + b1) + acc
        xh2, _ = _ln_stats(r2)
        return [r2, xh2 * g2 + b2], []

    r2, h2b = fused_mm("ffn_down", [(f, W['down'], 0, False)],
                       [(r1, 'row', D, 0), (sm['ln1_g'], 'vec', D, 0), (sm['ln1_b'], 'vec', D, 0),
                        (sm['ln2_g'], 'vec', D, 0), (sm['ln2_b'], 'vec', D, 0)],
                       down_epi, [(D, F32, D, 0), (D, BF16, D, 0)], M=S, tm=tm, tn=D)

    pe, = fused_mm("ple_proj", [(pb, W['ple'], 0, False)], [], ident, [(D, F32, D, 0)], M=S, tm=tmx, tn=D)

    def ple_epi(acc, ex):
        r2_, g2, b2, pe_, pg = ex
        xh2, _ = _ln_stats(r2_)
        h2 = xh2 * g2 + b2
        e = pe_ * lax.rsqrt(jnp.mean(pe_ * pe_, axis=-1, keepdims=True) + RMS_EPS) * pg
        xn = h2 + e * _sig(acc)
        return [acc, xn, xn], []

    t_, xn, xnb = fused_mm("ple_gate", [(h2b, W['ple_gate'], 0, False)],
                           [(r2, 'row', D, 0), (sm['ln2_g'], 'vec', D, 0), (sm['ln2_b'], 'vec', D, 0),
                            (pe, 'row', D, 0), (sm['ple_norm_g'], 'vec', D, 0)],
                           ple_epi, [(D, F32, D, 0), (D, F32, D, 0), (D, BF16, D, 0)], M=S, tm=tm, tn=D)
    sv.update(x=x, xb=xb, pb=pb, proj=proj, dtraw=dtraw, u=u, ca=ca, sa=sa, y_a=y_a, cbv_x=cbv_x, cbv_bc=cbv_bc,
              xs=xs, xsT=xsT, bc=bc,
              ysum=ysum, st_f=st_f, st_r=st_r, yn=yn, y_b=y_b, merged=merged, r1=r1, hb=hb,
              g_=g_, u_=u_, f=f, r2=r2, h2b=h2b, t_=t_, pe=pe)
    return xn, xnb, sv


def _layer_bwd(cf, sv, W, sm, dxn=None, target=None, xn=None):
    S, D, CD, DI, XBC, F, H, G = cf['S'], cf['D'], cf['CD'], cf['DI'], cf['XBC'], cf['F'], cf['H'], cf['G']
    NM = cf['NM']
    alpha = cf['alpha']
    tm = cf['tm']
    gw = cf['GW']
    out = {}

    def ple_bwd_core(dx_, t, pe_, pg):
        s = _sig(t)
        rinv = lax.rsqrt(jnp.mean(pe_ * pe_, axis=-1, keepdims=True) + RMS_EPS)
        pn = pe_ * rinv
        e = pn * pg
        dtg = dx_ * e * (s * (1.0 - s))
        de = dx_ * s
        qv = de * pg
        dpe = rinv * (qv - pn * jnp.mean(qv * pn, axis=-1, keepdims=True))
        return dtg, dpe, de * pn

    if dxn is None:
        def head(xn_, tgt, t, pe_, pg):
            err = xn_ - tgt
            dx_ = err * (1.0 / D)
            dtg, dpe, dpg = ple_bwd_core(dx_, t, pe_, pg)
            return [dx_, dtg, dpe], [dpg, err * err]

        (dxn, dtg, dpe, dpg, lsq) = row_call(
            "loss_ple_bwd", head,
            [(xn, 'row', D, 0, 0), (target, 'row', D, 0, 0), (sv['t_'], 'row', D, 0, 0), (sv['pe'], 'row', D, 0, 0),
             (sm['ple_norm_g'], 'vec', D, 0, 0)],
            [(D, F32, D, 0, 0), (D, BF16, D, 0, 0), (D, BF16, D, 0, 0)], [(D, D, 0, 0), (D, D, 0, 0)], M=S, tm=tm)
        out['loss_sq'] = lsq
    else:
        def mid(dx_, t, pe_, pg):
            dtg, dpe, dpg = ple_bwd_core(dx_, t, pe_, pg)
            return [dtg, dpe], [dpg]

        (dtg, dpe, dpg) = row_call(
            "ple_bwd", mid,
            [(dxn, 'row', D, 0, 0), (sv['t_'], 'row', D, 0, 0), (sv['pe'], 'row', D, 0, 0),
             (sm['ple_norm_g'], 'vec', D, 0, 0)],
            [(D, BF16, D, 0, 0), (D, BF16, D, 0, 0)], [(D, D, 0, 0)], M=S, tm=tm)
    out['ple_norm_g'] = dpg

    def ln_bwd_epi(scale):
        def epi(acc, ex):
            res, r_, g_ = ex
            dh = scale * res + acc
            xhat, rstd = _ln_stats(r_)
            dr = _ln_bwd(dh, xhat, rstd, g_)
            return [dr, dr], [dh * xhat, dh]
        return epi

    dr2, dr2b, dg2, db2 = fused_mm(
        "dh2", [(dtg, W['ple_gate_T'], 0, False)],
        [(dxn, 'row', D, 0), (sv['r2'], 'row', D, 0), (sm['ln2_g'], 'vec', D, 0)],
        ln_bwd_epi(1.0), [(D, F32, D, 0), (D, BF16, D, 0)], [(D, D, 0), (D, D, 0)], M=S, tm=tm, tn=D)
    out['ln2_g'], out['ln2_b'] = dg2, db2

    tnf = cf['tnf']

    def dswiglu_epi(acc, ex):
        gg, uu = ex
        s = _sig(gg)
        return [acc * uu * _dsilu(gg, s), acc * (gg * s)], []

    dg_b, du_b = fused_mm(
        "d_down", [(dr2b, W['down_T'], 0, False)],
        [(sv['g_'], 'row', tnf, 0), (sv['u_'], 'row', tnf, 0)], dswiglu_epi,
        [(F, BF16, tnf, 0), (F, BF16, tnf, 0)], M=S, tm=tm, tn=tnf, nj=F // tnf)

    dr1, dr1b, dg1, db1 = fused_mm(
        "dh1", [(dg_b, W['gate_T'], 0, True), (du_b, W['up_T'], 0, True)],
        [(dr2, 'row', D, 0), (sv['r1'], 'row', D, 0), (sm['ln1_g'], 'vec', D, 0)],
        ln_bwd_epi(alpha), [(D, F32, D, 0), (D, BF16, D, 0)], [(D, D, 0), (D, D, 0)],
        M=S, tm=tm, tn=D, nk=cf['nk_f'])
    out['ln1_g'], out['ln1_b'] = dg1, db1

    goff = (2 * CD) // D

    def dmerge_epi(acc, ex):
        ga, gb, ya, yb = ex
        sa_, sb_ = _sig(ga), _sig(gb)
        dga = acc * ya * (sa_ * (1.0 - sa_))
        dgb = acc * yb * (sb_ * (1.0 - sb_))
        return [jnp.concatenate([dga, dgb], axis=1), acc * sa_, acc * sb_], []

    dproj, dya_b, dyb_b = fused_mm(
        "d_merge", [(dr1b, W['o_T'], 0, False)],
        [(sv['proj'], 'row', D, goff), (sv['proj'], 'row', D, goff + 1), (sv['y_a'], 'row', D, 0), (sv['y_b'], 'row', D, 0)],
        dmerge_epi, [(NM, BF16, 2 * D, (2 * CD) // (2 * D)), (D, BF16, D, 0), (D, BF16, D, 0)], M=S, tm=tm, tn=D)

    def dsa_epi(acc, ex):
        ca_, g_, b_ = ex
        xhat, rstd = _ln_stats(ca_)
        la = xhat * g_ + b_
        dla = acc * _dsilu(la, _sig(la))
        dca = _ln_bwd(dla, xhat, rstd, g_)
        return [dca], [dla * xhat, dla, dca]

    dca, dlag, dlab, dcab = fused_mm(
        "d_a_out", [(dya_b, W['a_out_T'], 0, False)],
        [(sv['ca'], 'row', CD, 0), (sm['ln_a_g'], 'vec', CD, 0), (sm['ln_a_b'], 'vec', CD, 0)],
        dsa_epi, [(CD, F32, CD, 0)], [(CD, CD, 0), (CD, CD, 0), (CD, CD, 0)], M=S, tm=tm, tn=D)
    out['ln_a_g'], out['ln_a_b'], out['conv_a_b'] = dlag, dlab, dcab

    def dglu_epi(du, ex):
        a, gt = ex
        s = _sig(gt)
        return [jnp.concatenate([du * s, du * a * (s * (1.0 - s))], axis=1)], []

    dproj, dwa = conv_call(
        "d_conv_a", dca, 0, sm['conv_a_w'], cf['KA'], dglu_epi,
        [(sv['proj'], 'row', CD, 0, 0), (sv['proj'], 'row', CD, 1, 0)],
        [(NM, BF16, 2 * CD, 0, 0)], M=S, tm=cf['tmc'], cw=CD, nc=1, reverse=True, xin=(sv['u'], 0),
        passthrough=(dproj, 0))
    out['conv_a_w'] = dwa

    zoff = (2 * CD + 2 * D) // DI

    def dgate_norm_epi(acc, ex):
        ysum_, xs, z, dsk, ng = ex
        y = ysum_ + xs * dsk
        sz = _sig(z)
        siluz = z * sz
        yz = y * siluz
        dyzs, yhats = [], []
        for g in range(G):
            t = yz[:, g * gw:(g + 1) * gw]
            rinv = lax.rsqrt(jnp.mean(t * t, axis=-1, keepdims=True) + RMS_EPS)
            yh = t * rinv
            qv = acc[:, g * gw:(g + 1) * gw] * ng[:, g * gw:(g + 1) * gw]
            dyzs.append(rinv * (qv - yh * jnp.mean(qv * yh, axis=-1, keepdims=True)))
            yhats.append(yh)
        dyz = jnp.concatenate(dyzs, axis=1)
        yhat = jnp.concatenate(yhats, axis=1)
        dy = dyz * siluz
        dz = dyz * y * _dsilu(z, sz)
        return [dz], [acc * yhat, dy * xs], [dy]

    tmr = cf['tmr']
    dproj, dng, ddsk, dyT = fused_mm(
        "d_b_out", [(dyb_b, W['b_out_T'], 0, False)],
        [(sv['ysum'], 'row', DI, 0), (sv['xs'], 'row', DI, 0), (sv['proj'], 'row', DI, zoff),
         (sm['dskip_full'], 'vec', DI, 0), (sm['ssm_norm_g'], 'vec', DI, 0)],
        dgate_norm_epi, [(NM, BF16, DI, zoff)], [(DI, DI, 0), (DI, DI, 0)],
        M=S, tm=tmr, tn=DI, passthrough=(dproj, 0), t_outs=[(DI, F32, DI, 0)])
    out['ssm_norm_g'], out['dskip_full'] = dng, ddsk

    dxbc_f, ddt_f, dA_f = ssd_bwd("ssd_bwd_f", sv['xsT'], sv['bc'], sv['dtraw'], dyT, sv['st_f'], sm['dtb_f'],
                                  sm['alog_f'], S=S, DI=DI, G=G, H=H, rev=False)
    dcb, ddt_r, dA_r, dcbb = ssd_bwd("ssd_bwd_r", sv['xsT'], sv['bc'], sv['dtraw'], dyT, sv['st_r'], sm['dtb_r'],
                                     sm['alog_r'], S=S, DI=DI, G=G, H=H, rev=True,
                                     tail=(dxbc_f, sv['cbv_x'], sv['cbv_bc'], sm['dskipT']))
    out['dA_f'], out['dA_r'] = dA_f, dA_r
    out['ssm_conv_b'] = dcbb

    xoff = (2 * CD + 2 * D + DI) // DI
    dproj, dwb = conv_call(
        "d_conv_b", dcb, 0, sm['ssm_conv_w'], cf['KB'], lambda conv, ex: ([conv], []), [],
        [(NM, BF16, DI, xoff, 1)], M=S, tm=cf['tmc'], cw=DI, nc=XBC // DI, reverse=True, xin=(sv['proj'], xoff),
        passthrough=(dproj, 0))
    out['ssm_conv_w'] = dwb

    ddtb, ddt_bias = row_call("d_dt", lambda a, b: ([a + b], [a + b]),
                              [(ddt_f, 'row', LANES, 0, 0), (ddt_r, 'row', LANES, 0, 0)],
                              [(LANES, BF16, LANES, 0, 0)], [(LANES, LANES, 0, 0)], M=S, tm=tm)
    out['dt_bias'] = ddt_bias

    dx, = fused_mm("d_x", [(dproj, W['in_main_T'], 0, True), (ddtb, W['in_dt_T'], 0, False)],
                   [(dr1, 'row', D, 0)], lambda acc, ex: ([alpha * ex[0] + acc], []),
                   [(D, F32, D, 0)], M=S, tm=cf['tmx'], tn=D, nk=cf['nk_in'])

    tmw = cf['tmw']
    xb = sv['xb']
    out['w_in'] = jnp.concatenate(
        [mm_tn("dw_in", xb, dproj, tm=tmw, tk=D, tn=cf['tn_in']),
         mm_tn("dw_dt", xb, ddtb, tm=tmw, tk=D, tn=LANES)[:, :2 * H]], axis=1)
    out['w_a_out'] = mm_tn("dw_a_out", sv['sa'], dya_b, tm=tmw, tk=CD, tn=D)
    out['w_b_out'] = mm_tn("dw_b_out", sv['yn'], dyb_b, tm=tmw, tk=DI // 2, tn=D)
    out['w_o'] = mm_tn("dw_o", sv['merged'], dr1b, tm=tmw, tk=D, tn=D)
    out['w_gate_up'] = jnp.concatenate(
        [mm_tn("dw_gate", sv['hb'], dg_b, tm=tmw, tk=D, tn=tnf),
         mm_tn("dw_up", sv['hb'], du_b, tm=tmw, tk=D, tn=tnf)], axis=1)
    out['w_down'] = mm_tn("dw_down", sv['f'], dr2b, tm=tmw, tk=tnf, tn=D)
    out['w_ple'] = mm_tn("dw_ple", sv['pb'], dpe, tm=tmw, tk=sv['pb'].shape[1], tn=D)
    out['w_ple_gate'] = mm_tn("dw_ple_gate", sv['h2b'], dtg, tm=tmw, tk=D, tn=D)
    return dx, out


_WEIGHTS = ['w_in', 'conv_a_w', 'conv_a_b', 'ln_a_g', 'ln_a_b', 'w_a_out', 'ssm_conv_w', 'ssm_conv_b', 'a_log',
            'dt_bias', 'd_skip', 'ssm_norm_g', 'w_b_out', 'w_o', 'ln1_g', 'ln1_b', 'w_gate_up', 'w_down', 'ln2_g',
            'ln2_b', 'w_ple', 'ple_norm_g', 'w_ple_gate']
_COL_SHARDED = ['w_in', 'conv_a_w', 'ssm_conv_w', 'w_gate_up', 'w_ple']
_ROW_SHARDED = ['w_a_out', 'w_b_out', 'w_o', 'w_down', 'w_ple_gate']
_BIG = _COL_SHARDED + _ROW_SHARDED
_SMALL = [n for n in _WEIGHTS if n not in _BIG]
_CONV = ['conv_a_w', 'ssm_conv_w']


def _ceil_to(n, k):
    return -(-n // k) * k


def kernel(x, p, w_in, conv_a_w, conv_a_b, ln_a_g, ln_a_b, w_a_out, ssm_conv_w, ssm_conv_b, a_log, dt_bias, d_skip, ssm_norm_g, w_b_out, w_o, ln1_g, ln1_b, w_gate_up, w_down, ln2_g, ln2_b, w_ple, ple_norm_g, w_ple_gate, loss_target, m_w_in, m_conv_a_w, m_conv_a_b, m_ln_a_g, m_ln_a_b, m_w_a_out, m_ssm_conv_w, m_ssm_conv_b, m_a_log, m_dt_bias, m_d_skip, m_ssm_norm_g, m_w_b_out, m_w_o, m_ln1_g, m_ln1_b, m_w_gate_up, m_w_down, m_ln2_g, m_ln2_b, m_w_ple, m_ple_norm_g, m_w_ple_gate, v_w_in, v_conv_a_w, v_conv_a_b, v_ln_a_g, v_ln_a_b, v_w_a_out, v_ssm_conv_w, v_ssm_conv_b, v_a_log, v_dt_bias, v_d_skip, v_ssm_norm_g, v_w_b_out, v_w_o, v_ln1_g, v_ln1_b, v_w_gate_up, v_w_down, v_ln2_g, v_ln2_b, v_w_ple, v_ple_norm_g, v_w_ple_gate):
    wt = dict(w_in=w_in, conv_a_w=conv_a_w, conv_a_b=conv_a_b, ln_a_g=ln_a_g, ln_a_b=ln_a_b, w_a_out=w_a_out,
              ssm_conv_w=ssm_conv_w, ssm_conv_b=ssm_conv_b, a_log=a_log, dt_bias=dt_bias, d_skip=d_skip,
              ssm_norm_g=ssm_norm_g, w_b_out=w_b_out, w_o=w_o, ln1_g=ln1_g, ln1_b=ln1_b, w_gate_up=w_gate_up,
              w_down=w_down, ln2_g=ln2_g, ln2_b=ln2_b, w_ple=w_ple, ple_norm_g=ple_norm_g, w_ple_gate=w_ple_gate)
    mo = dict(w_in=m_w_in, conv_a_w=m_conv_a_w, conv_a_b=m_conv_a_b, ln_a_g=m_ln_a_g, ln_a_b=m_ln_a_b,
              w_a_out=m_w_a_out, ssm_conv_w=m_ssm_conv_w, ssm_conv_b=m_ssm_conv_b, a_log=m_a_log,
              dt_bias=m_dt_bias, d_skip=m_d_skip, ssm_norm_g=m_ssm_norm_g, w_b_out=m_w_b_out, w_o=m_w_o,
              ln1_g=m_ln1_g, ln1_b=m_ln1_b, w_gate_up=m_w_gate_up, w_down=m_w_down, ln2_g=m_ln2_g, ln2_b=m_ln2_b,
              w_ple=m_w_ple, ple_norm_g=m_ple_norm_g, w_ple_gate=m_w_ple_gate)
    vo = dict(w_in=v_w_in, conv_a_w=v_conv_a_w, conv_a_b=v_conv_a_b, ln_a_g=v_ln_a_g, ln_a_b=v_ln_a_b,
              w_a_out=v_w_a_out, ssm_conv_w=v_ssm_conv_w, ssm_conv_b=v_ssm_conv_b, a_log=v_a_log,
              dt_bias=v_dt_bias, d_skip=v_d_skip, ssm_norm_g=v_ssm_norm_g, w_b_out=v_w_b_out, w_o=v_w_o,
              ln1_g=v_ln1_g, ln1_b=v_ln1_b, w_gate_up=v_w_gate_up, w_down=v_w_down, ln2_g=v_ln2_g, ln2_b=v_ln2_b,
              w_ple=v_w_ple, ple_norm_g=v_ple_norm_g, w_ple_gate=v_w_ple_gate)

    L = w_in.shape[0]
    S, D = x.shape[1], x.shape[2]
    CD = conv_a_b.shape[1]
    DI = ssm_norm_g.shape[1]
    XBC = ssm_conv_b.shape[1]
    H = d_skip.shape[1]
    G = (XBC - DI) // (2 * D_STATE)
    F = w_down.shape[1] * 4
    N_IN = w_in.shape[2] * 4
    NM = N_IN - 2 * H
    KA, KB = conv_a_w.shape[1], ssm_conv_w.shape[1]
    assert DI == H * HEAD_DIM and CD == D and DI == 2 * D and XBC == 2 * DI and NM == 2 * CD + 2 * D + DI + XBC
    assert 2 * H <= LANES and S % CHUNK == 0
    tnf = F // 2
    cf = dict(S=S, D=D, CD=CD, DI=DI, XBC=XBC, F=F, H=H, G=G, NM=NM, KA=KA, KB=KB, GW=(H // G) * HEAD_DIM,
              alpha=float((2 * L) ** 0.25), tm=min(512, S), tmx=min(1024, S), tmc=min(256, S), tmr=min(256, S), tmw=min(1024, S),
              tn_in=D, tnf=tnf, nk_f=2, nk_in=NM // DI)

    core = lax.axis_index("c").astype(jnp.int32).reshape(1)
    split_names = [n for n in _BIG if n not in _CONV]

    def layer_weights(l):
        got = gather_layer("gather_weights", [wt[n][l].astype(BF16) for n in split_names], [wt[n][l] for n in _CONV])
        full = {}
        for n, g in zip(split_names + _CONV, got):
            if n in _COL_SHARDED:
                full[n] = g.transpose(1, 0, 2).reshape(g.shape[1], 4 * g.shape[2])
            else:
                full[n] = g.reshape(4 * g.shape[1], g.shape[2])
        win = full['w_in']
        in_main = win[:, :NM]
        in_dt = _pad_lanes(win[:, NM:])
        gu = full['w_gate_up']
        W = dict(in_main=in_main, in_dt=in_dt, in_main_T=in_main.T, in_dt_T=in_dt.T,
                 a_out=full['w_a_out'], a_out_T=full['w_a_out'].T,
                 b_out=full['w_b_out'], b_out_T=full['w_b_out'].T,
                 o=full['w_o'], o_T=full['w_o'].T, gate_up=gu, gate_T=gu[:, :F].T, up_T=gu[:, F:].T,
                 down=full['w_down'], down_T=full['w_down'].T, ple=full['w_ple'],
                 ple_gate=full['w_ple_gate'], ple_gate_T=full['w_ple_gate'].T)
        row = lambda v: v.reshape(1, -1)
        head_table = lambda v: jnp.broadcast_to(jnp.pad(v, (0, LANES - H))[:, None], (LANES, LANES))
        sm = dict(conv_a_w=jnp.pad(full['conv_a_w'], ((0, _ceil_to(KA, SUBLANES) - KA), (0, 0))),
                  ssm_conv_w=jnp.pad(full['ssm_conv_w'], ((0, _ceil_to(KB, SUBLANES) - KB), (0, 0))),
                  conv_a_b=row(conv_a_b[l]), ln_a_g=row(ln_a_g[l]), ln_a_b=row(ln_a_b[l]),
                  ssm_conv_b=row(ssm_conv_b[l]), ssm_norm_g=row(ssm_norm_g[l]),
                  ln1_g=row(ln1_g[l]), ln1_b=row(ln1_b[l]), ln2_g=row(ln2_g[l]), ln2_b=row(ln2_b[l]),
                  ple_norm_g=row(ple_norm_g[l]),
                  dtb_f=head_table(dt_bias[l, 0]), dtb_r=head_table(dt_bias[l, 1]),
                  alog_f=head_table(a_log[l, 0]), alog_r=head_table(a_log[l, 1]),
                  dskip_full=row(jnp.repeat(d_skip[l], HEAD_DIM)),
                  dskipT=jnp.broadcast_to(jnp.repeat(d_skip[l], HEAD_DIM)[:, None], (DI, LANES)))
        return W, sm

    def blocks(n, gl):
        g = gl[n]
        if n == 'conv_a_w':
            g = g.sum(axis=1)[:KA]
        elif n == 'ssm_conv_w':
            g = g.sum(axis=1)[:KB]
        if n in _COL_SHARDED:
            return g.reshape(g.shape[0], 4, g.shape[1] // 4).transpose(1, 0, 2)
        return g.reshape(4, g.shape[0] // 4, g.shape[1])

    def reduce_layer(l, gl, acc):
        mine = [blocks(n, gl) for n in split_names]
        theirs = core_send_half("core_send_half", mine)
        both = [core_sum("core_sum_" + n, core, b, t) for n, b, t in zip(split_names, mine, theirs)]
        parts = chip_exchange("scatter_grads", [[t] for t in both], gather=False)
        sums = [chip_sum_into("chip_sum_" + n, core, pr.reshape(4, pr.shape[2], pr.shape[3]), l, L, into=acc.get(n))
                for n, pr in zip(split_names, parts)]
        return dict(zip(split_names, core_fill("core_fill", sums, l, L)))

    lw = [layer_weights(l) for l in range(L)]
    xl = x[0]
    xlb = xl.astype(BF16)
    saved = []
    for l in range(L):
        xl, xlb, sv = _layer_fwd(cf, xl, xlb, p[l, 0].astype(BF16), lw[l][0], lw[l][1])
        saved.append(sv)
    grads = [None] * L
    dxl = None
    gsum = {}
    for l in reversed(range(L)):
        if l == L - 1:
            dxl, grads[l] = _layer_bwd(cf, saved[l], lw[l][0], lw[l][1], target=loss_target[0], xn=xl)
        else:
            dxl, grads[l] = _layer_bwd(cf, saved[l], lw[l][0], lw[l][1], dxn=dxl)
        gsum = reduce_layer(l, grads[l], gsum)
    loss = lax.psum(0.5 / D * jnp.sum(grads[L - 1]['loss_sq']), ("x", "y", "c"))
    grad_x = dxl[None]

    res = {}
    for n in split_names:
        shp = wt[n].shape
        flat = lambda a: a.reshape(shp[0] * shp[1], shp[2])
        outs = adamw_full("adamw_" + n, gsum[n], flat(wt[n]), flat(mo[n]), flat(vo[n]))
        res[n] = [o.reshape(shp) for o in [gsum[n]] + list(outs)]
    parts = chip_exchange("scatter_conv", [[blocks(n, grads[l]) for l in range(L)] for n in _CONV], gather=False)
    chip_sums = [sum_chips("chip_sum_" + n, pr.reshape(4, L * pr.shape[2], pr.shape[3])) for n, pr in zip(_CONV, parts)]
    sib_sums = sibling_swap("core_swap", chip_sums)
    for n, mine, sib in zip(_CONV, chip_sums, sib_sums):
        shp = wt[n].shape
        flat = lambda a: a.reshape(shp[0] * shp[1], shp[2])
        outs = adamw_shard("adamw_" + n, mine, sib, flat(wt[n]), flat(mo[n]), flat(vo[n]))
        res[n] = [o.reshape(shp) for o in outs]

    def small_pieces(l):
        gl = grads[l]
        A = -jnp.exp(a_log[l])
        d = dict(gl)
        d_alog = jnp.concatenate([gl['dA_f'].sum(axis=1)[:H] * A[0], gl['dA_r'].sum(axis=1)[:H] * A[1]])
        d['a_log'] = jnp.pad(d_alog[None], ((0, SUBLANES - 1), (0, 0)))
        d['dt_bias'] = gl['dt_bias'][:, :2 * H]
        d['d_skip'] = gl['dskip_full'].reshape(SUBLANES, H, HEAD_DIM).sum(axis=-1)
        return [_pad_lanes(d[n], _ceil_to(d[n].shape[1], LANES)) for n in _SMALL]

    widths = [_ceil_to(math.prod(wt[n].shape[1:]), LANES) for n in _SMALL]
    packed = jnp.concatenate([pc for l in range(L) for pc in small_pieces(l)], axis=1)
    gathered = all8_gather("gather_small", packed)

    def pack_params(src):
        return jnp.concatenate([_pad_lanes(src[n][l].reshape(1, -1), wd) for l in range(L) for n, wd in zip(_SMALL, widths)],
                               axis=1)

    small_out = adamw_small("adamw_small", gathered, pack_params(wt), pack_params(mo), pack_params(vo))
    off = 0
    per = {n: [[] for _ in range(4)] for n in _SMALL}
    for l in range(L):
        for n, wd in zip(_SMALL, widths):
            size = math.prod(wt[n].shape[1:])
            for k in range(4):
                per[n][k].append(small_out[k][0, off:off + size].reshape(wt[n].shape[1:]))
            off += wd
    for n in _SMALL:
        res[n] = [jnp.stack(per[n][k]) for k in range(4)]

    return (loss, grad_x, *[res[n][0] for n in _WEIGHTS], *[res[n][1] for n in _WEIGHTS],
            *[res[n][2] for n in _WEIGHTS], *[res[n][3] for n in _WEIGHTS])
```

```python
import math

import jax
import jax.numpy as jnp
from jax import lax
from jax.experimental import pallas as pl
from jax.experimental.pallas import tpu as pltpu

F32 = jnp.float32
BF16 = jnp.bfloat16

VMEM_LIMIT_BYTES = 56 * 1024 * 1024
LANES = 128
SUBLANES = 8

CHUNK = 128
D_STATE = 128
HEAD_DIM = 64
LN_EPS = 1e-5
RMS_EPS = 1e-6
ADAM_LR = 0.001
ADAM_B1 = 0.9
ADAM_B2 = 0.999
ADAM_EPS = 1e-08
ADAM_WD = 0.01
ADAM_STEP = 10
HALO = 16
MESH = pl.DeviceIdType.MESH


def _params(**kw):
    return pltpu.CompilerParams(vmem_limit_bytes=VMEM_LIMIT_BYTES, **kw)


def _sig(x):
    return jax.nn.sigmoid(x)


def _dsilu(x, s):
    return s * (1.0 + x * (1.0 - s))


def _ln_stats(r):
    mu = jnp.mean(r, axis=-1, keepdims=True)
    xc = r - mu
    var = jnp.mean(xc * xc, axis=-1, keepdims=True)
    rstd = lax.rsqrt(var + LN_EPS)
    return xc * rstd, rstd


def _ln_bwd(dy, xhat, rstd, g):
    dxh = dy * g
    m1 = jnp.mean(dxh, axis=-1, keepdims=True)
    m2 = jnp.mean(dxh * xhat, axis=-1, keepdims=True)
    return rstd * (dxh - m1 - xhat * m2)


def _f32(v):
    return v if v.dtype == F32 else v.astype(F32)


def _rows8(v):
    tm, w = v.shape
    return v.reshape(tm // SUBLANES, SUBLANES, w).sum(axis=0)


def fused_mm(name, prods, extras, epi, row_outs, col_outs=(), *, M, tm, tn, nj=1, nk=1,
             passthrough=None, t_outs=()):
    np_ = len(prods)
    ne = len(extras)
    nro = len(row_outs)
    nco = len(col_outs)
    use_acc = nk > 1

    def body(*refs):
        a_refs = [refs[2 * p] for p in range(np_)]
        w_refs = [refs[2 * p + 1] for p in range(np_)]
        pos = 2 * np_
        e_refs = refs[pos:pos + ne]
        pos += ne
        if passthrough is not None:
            pos += 1
        ro_refs = refs[pos:pos + nro]
        pos += nro
        co_refs = refs[pos:pos + nco]
        pos += nco
        to_refs = refs[pos:pos + len(t_outs)]
        pos += len(t_outs)
        acc_ref = refs[pos] if use_acc else None
        i = pl.program_id(1)
        k = pl.program_id(2)

        def prod(p):
            a = a_refs[p][...]
            if a.dtype != BF16:
                a = a.astype(BF16)
            return jnp.dot(a, w_refs[p][...], preferred_element_type=F32)

        def finish(acc):
            res = epi(acc, [_f32(r[...]) for r in e_refs])
            rows, cols = res[0], res[1]
            for v, o in zip(rows, ro_refs):
                o[...] = v.astype(o.dtype)
            for v, o in zip(res[2] if len(res) > 2 else (), to_refs):
                o[...] = v.T.astype(o.dtype)
            for v, o in zip(cols, co_refs):
                v8 = _rows8(v)

                @pl.when(i == 0)
                def _():
                    o[...] = v8

                @pl.when(i > 0)
                def _():
                    o[...] += v8

        if not use_acc:
            acc = prod(0)
            for p in range(1, np_):
                acc = acc + prod(p)
            finish(acc)
        else:
            @pl.when(k == 0)
            def _():
                acc = None
                for p in range(np_):
                    acc = prod(p) if acc is None else acc + prod(p)
                acc_ref[...] = acc

            @pl.when(k > 0)
            def _():
                acc = None
                for p in range(np_):
                    if prods[p][3]:
                        acc = prod(p) if acc is None else acc + prod(p)
                acc_ref[...] += acc

            @pl.when(k == nk - 1)
            def _():
                finish(acc_ref[...])

    in_specs = []
    args = []
    for a, w, joff, ksplit in prods:
        K = a.shape[1]
        if ksplit:
            tk = K // nk
            in_specs.append(pl.BlockSpec((tm, tk), lambda j, i, k: (i, k)))
            in_specs.append(pl.BlockSpec((tk, tn), lambda j, i, k, joff=joff: (k, j + joff)))
        else:
            in_specs.append(pl.BlockSpec((tm, K), lambda j, i, k: (i, 0)))
            in_specs.append(pl.BlockSpec((K, tn), lambda j, i, k, joff=joff: (0, j + joff)))
        args += [a, w]
    for arr, kind, width, c0 in extras:
        if kind == 'row':
            in_specs.append(pl.BlockSpec((tm, width), lambda j, i, k, c0=c0: (i, c0 + j)))
        else:
            in_specs.append(pl.BlockSpec((arr.shape[0], width), lambda j, i, k, c0=c0: (0, c0 + j)))
        args.append(arr)
    aliases = {}
    if passthrough is not None:
        arr, oidx = passthrough
        in_specs.append(pl.BlockSpec(memory_space=pl.ANY))
        aliases = {len(args): oidx}
        args.append(arr)
    out_shape = []
    out_specs = []
    for n_total, dtype, width, c0 in row_outs:
        out_shape.append(jax.ShapeDtypeStruct((M, n_total), dtype))
        out_specs.append(pl.BlockSpec((tm, width), lambda j, i, k, c0=c0: (i, c0 + j)))
    for n_total, width, c0 in col_outs:
        out_shape.append(jax.ShapeDtypeStruct((SUBLANES, n_total), F32))
        out_specs.append(pl.BlockSpec((SUBLANES, width), lambda j, i, k, c0=c0: (0, c0 + j)))
    for n_total, dtype, width, c0 in t_outs:
        out_shape.append(jax.ShapeDtypeStruct((n_total, M), dtype))
        out_specs.append(pl.BlockSpec((width, tm), lambda j, i, k, c0=c0: (c0 + j, i)))
    scratch = [pltpu.VMEM((tm, tn), F32)] if use_acc else []
    return pl.pallas_call(
        body, name=name, grid=(nj, M // tm, nk), in_specs=in_specs, out_specs=out_specs,
        out_shape=out_shape, scratch_shapes=scratch, input_output_aliases=aliases,
        compiler_params=_params(dimension_semantics=("arbitrary", "arbitrary", "arbitrary")),
    )(*args)


def mm_tn(name, a, b, *, tm, tk, tn):
    M, K = a.shape
    N = b.shape[1]

    def body(a_ref, b_ref, o_ref):
        m = pl.program_id(2)
        p = lax.dot_general(a_ref[...], b_ref[...], (((0,), (0,)), ((), ())),
                            preferred_element_type=F32)

        @pl.when(m == 0)
        def _():
            o_ref[...] = p

        @pl.when(m > 0)
        def _():
            o_ref[...] += p

    return pl.pallas_call(
        body, name=name, grid=(K // tk, N // tn, M // tm),
        in_specs=[pl.BlockSpec((tm, tk), lambda kk, j, m: (m, kk)),
                  pl.BlockSpec((tm, tn), lambda kk, j, m: (m, j))],
        out_specs=pl.BlockSpec((tk, tn), lambda kk, j, m: (kk, j)),
        out_shape=jax.ShapeDtypeStruct((K, N), F32),
        compiler_params=_params(dimension_semantics=("arbitrary", "arbitrary", "arbitrary")),
    )(a, b)


def row_call(name, fn, ins, row_outs, col_outs=(), *, M, tm, nc=1):
    ni = len(ins)
    nro = len(row_outs)

    def body(*refs):
        i = pl.program_id(1)
        vals = [_f32(r[...]) for r in refs[:ni]]
        rows, cols = fn(*vals)
        for v, o in zip(rows, refs[ni:ni + nro]):
            o[...] = v.astype(o.dtype)
        for v, o in zip(cols, refs[ni + nro:]):
            v8 = _rows8(v)

            @pl.when(i == 0)
            def _():
                o[...] = v8

            @pl.when(i > 0)
            def _():
                o[...] += v8

    in_specs = []
    for arr, kind, width, c0, cmul in ins:
        if kind == 'row':
            in_specs.append(pl.BlockSpec((tm, width), lambda cj, i, c0=c0, cmul=cmul: (i, c0 + cmul * cj)))
        else:
            in_specs.append(pl.BlockSpec((arr.shape[0], width), lambda cj, i, c0=c0, cmul=cmul: (0, c0 + cmul * cj)))
    out_shape = []
    out_specs = []
    for n_total, dtype, width, c0, cmul in row_outs:
        out_shape.append(jax.ShapeDtypeStruct((M, n_total), dtype))
        out_specs.append(pl.BlockSpec((tm, width), lambda cj, i, c0=c0, cmul=cmul: (i, c0 + cmul * cj)))
    for n_total, width, c0, cmul in col_outs:
        out_shape.append(jax.ShapeDtypeStruct((SUBLANES, n_total), F32))
        out_specs.append(pl.BlockSpec((SUBLANES, width), lambda cj, i, c0=c0, cmul=cmul: (0, c0 + cmul * cj)))
    return pl.pallas_call(
        body, name=name, grid=(nc, M // tm), in_specs=in_specs, out_specs=out_specs,
        out_shape=out_shape,
        compiler_params=_params(dimension_semantics=("arbitrary", "arbitrary")),
    )(*[a[0] for a in ins])


def conv_call(name, src, src_c0, w, K, epi, extras, row_outs, col_outs=(), *, M, tm, cw, nc,
              reverse, xin=None, passthrough=None, t_outs=(), w_c0=0):
    pad = (K - 1) // 2
    assert pad <= HALO - 1
    R = tm // HALO
    nblk = M // HALO
    n_i = M // tm
    Kp = w.shape[0]
    ne = len(extras)
    nro = len(row_outs)
    nco = len(col_outs)
    rb = 64
    cbw = min(cw, 256)
    n_copies = SUBLANES if K > SUBLANES else 1

    def body(*refs):
        main_ref, prev_ref, next_ref, w_ref = refs[:4]
        pos = 4
        xin_ref = None
        if xin is not None:
            xin_ref = refs[pos]
            pos += 1
        e_refs = refs[pos:pos + ne]
        pos += ne
        if passthrough is not None:
            pos += 1
        ro_refs = refs[pos:pos + nro]
        pos += nro
        co_refs = refs[pos:pos + nco]
        pos += nco
        to_refs = refs[pos:pos + len(t_outs)]
        pos += len(t_outs)
        dw_ref = None
        if xin is not None:
            dw_ref = refs[pos]
            pos += 1
        ext_ref, conv_ref = refs[pos], refs[pos + 1]
        i = pl.program_id(1)

        ext_ref[0, 0:HALO, :] = jnp.where(i == 0, 0.0, prev_ref[...].astype(F32))
        ext_ref[0, HALO:HALO + tm, :] = main_ref[...].astype(F32)
        ext_ref[0, HALO + tm:, :] = jnp.where(i == n_i - 1, 0.0, next_ref[...].astype(F32))
        if dw_ref is not None:
            @pl.when(i == 0)
            def _():
                dw_ref[...] = jnp.zeros_like(dw_ref)

        n_sh = tm + 2 * HALO - SUBLANES
        for c0 in range(0, cw, cbw):
            for sft in range(1, n_copies):
                ext_ref[sft, 0:n_sh, c0:c0 + cbw] = ext_ref[0, sft:sft + n_sh, c0:c0 + cbw]

        for c0 in range(0, cw, cbw):
            for r0 in range(0, tm, rb):
                acc = jnp.zeros((rb, cbw), F32)
                if xin_ref is not None:
                    xblk = xin_ref[r0:r0 + rb, c0:c0 + cbw].astype(F32)
                for k in range(K):
                    off = HALO + r0 + ((pad - k) if reverse else (k - pad))
                    sft = off % SUBLANES if n_copies > 1 else 0
                    d = ext_ref[sft, off - sft:off - sft + rb, c0:c0 + cbw]
                    acc = acc + d * w_ref[k:k + 1, c0:c0 + cbw]
                    if xin_ref is not None:
                        dw_ref[k, :, c0:c0 + cbw] += _rows8(xblk * d)
                conv_ref[r0:r0 + rb, c0:c0 + cbw] = acc

        res = epi(conv_ref[...], [_f32(r[...]) for r in e_refs])
        rows, cols = res[0], res[1]
        for v, o in zip(rows, ro_refs):
            o[...] = v.astype(o.dtype)
        for v, o in zip(res[2] if len(res) > 2 else (), to_refs):
            o[...] = v.T.astype(o.dtype)
        for v, o in zip(cols, co_refs):
            v8 = _rows8(v)

            @pl.when(i == 0)
            def _():
                o[...] = v8

            @pl.when(i > 0)
            def _():
                o[...] += v8

    in_specs = [
        pl.BlockSpec((tm, cw), lambda cj, i: (i, src_c0 + cj)),
        pl.BlockSpec((HALO, cw), lambda cj, i: (jnp.maximum(i * R - 1, 0), src_c0 + cj)),
        pl.BlockSpec((HALO, cw), lambda cj, i: (jnp.minimum((i + 1) * R, nblk - 1), src_c0 + cj)),
        pl.BlockSpec((Kp, cw), lambda cj, i: (0, w_c0 + cj)),
    ]
    args = [src, src, src, w]
    if xin is not None:
        in_specs.append(pl.BlockSpec((tm, cw), lambda cj, i, c0=xin[1]: (i, c0 + cj)))
        args.append(xin[0])
    for arr, kind, width, c0, cmul in extras:
        if kind == 'row':
            in_specs.append(pl.BlockSpec((tm, width), lambda cj, i, c0=c0, cmul=cmul: (i, c0 + cmul * cj)))
        else:
            in_specs.append(pl.BlockSpec((arr.shape[0], width), lambda cj, i, c0=c0, cmul=cmul: (0, c0 + cmul * cj)))
        args.append(arr)
    aliases = {}
    if passthrough is not None:
        in_specs.append(pl.BlockSpec(memory_space=pl.ANY))
        aliases = {len(args): passthrough[1]}
        args.append(passthrough[0])
    out_shape = []
    out_specs = []
    for n_total, dtype, width, c0, cmul in row_outs:
        out_shape.append(jax.ShapeDtypeStruct((M, n_total), dtype))
        out_specs.append(pl.BlockSpec((tm, width), lambda cj, i, c0=c0, cmul=cmul: (i, c0 + cmul * cj)))
    for n_total, width, c0, cmul in col_outs:
        out_shape.append(jax.ShapeDtypeStruct((SUBLANES, n_total), F32))
        out_specs.append(pl.BlockSpec((SUBLANES, width), lambda cj, i, c0=c0, cmul=cmul: (0, c0 + cmul * cj)))
    for n_total, dtype, width, c0, cmul in t_outs:
        out_shape.append(jax.ShapeDtypeStruct((n_total, M), dtype))
        out_specs.append(pl.BlockSpec((width, tm), lambda cj, i, c0=c0, cmul=cmul: (c0 + cmul * cj, i)))
    if xin is not None:
        out_shape.append(jax.ShapeDtypeStruct((Kp, SUBLANES, cw * nc), F32))
        out_specs.append(pl.BlockSpec((Kp, SUBLANES, cw), lambda cj, i: (0, 0, cj)))
    return pl.pallas_call(
        body, name=name, grid=(nc, n_i), in_specs=in_specs, out_specs=out_specs,
        out_shape=out_shape, input_output_aliases=aliases,
        scratch_shapes=[pltpu.VMEM((n_copies, tm + 2 * HALO, cw), F32), pltpu.VMEM((tm, cw), F32)],
        compiler_params=_params(dimension_semantics=("arbitrary", "arbitrary")),
    )(*args)


def _split_dot(m_bf16, v, n_pass, dims=None):
    out = None
    rest = v
    for p in range(n_pass):
        piece = rest.astype(BF16)
        if p + 1 < n_pass:
            rest = rest - piece.astype(F32)
        if dims is None:
            t = jnp.dot(m_bf16, piece, preferred_element_type=F32)
        else:
            t = lax.dot_general(m_bf16, piece, dims, preferred_element_type=F32)
        out = t if out is None else out + t
    return out


def _split_dot_r(v, m_bf16, n_pass):
    out = None
    rest = v
    for p in range(n_pass):
        piece = rest.astype(BF16)
        if p + 1 < n_pass:
            rest = rest - piece.astype(F32)
        t = jnp.dot(piece, m_bf16, preferred_element_type=F32)
        out = t if out is None else out + t
    return out


def _softplus(x):
    return jnp.maximum(x, 0.0) + jnp.log1p(jnp.exp(-jnp.abs(x)))


NT_DIMS = (((1,), (1,)), ((), ()))
TN_DIMS = (((0,), (0,)), ((), ()))


def _ssd_common(dtraw, dtbT, alogT, rev, n_heads):
    L = CHUNK
    if rev:
        dtraw = pltpu.roll(dtraw, LANES - n_heads, 1)
    preT = dtraw.T + dtbT
    dtT = _softplus(preT)
    AT = -jnp.exp(alogT)
    aT = dtT * AT
    ri = lax.broadcasted_iota(jnp.int32, (L, L), 0)
    ci = lax.broadcasted_iota(jnp.int32, (L, L), 1)
    up = (ri >= ci) if rev else (ri <= ci)
    lo = (ri <= ci) if rev else (ri >= ci)
    csT = _split_dot_r(aT, up.astype(BF16), 3)
    last = 0 if rev else L - 1
    lastB = jnp.broadcast_to(csT[:, last:last + 1], (L, L))
    return dict(preT=preT, dtT=dtT, AT=AT, csT=csT, cs=csT.T, up=up, lo=lo, ci=ci, last=last,
                doutT=jnp.exp(csT), dstT=jnp.exp(lastB - csT), totB=jnp.exp(lastB))


def ssd_fwd(name, xsT, bc, dtraw, dtbT, alogT, *, S, DI, G, H, rev, tail=None):
    NC = S // CHUNK
    R = H // G
    GW = R * HEAD_DIM
    N = D_STATE
    P = HEAD_DIM

    def body(*refs):
        xsT_ref, bc_ref, dtraw_ref, dtb_ref, alog_ref = refs[:5]
        if tail is None:
            y_ref, st_ref, h_ref = refs[5:]
        else:
            yo_ref, z_ref, xs_ref, dsk_ref, ng_ref = refs[5:10]
            y_ref, st_ref, yn_ref, h_ref = refs[10:]
        c = pl.program_id(0)

        @pl.when(c == 0)
        def _():
            h_ref[...] = jnp.zeros_like(h_ref)

        q = _ssd_common(dtraw_ref[...], dtb_ref[...], alog_ref[...], rev, H)
        cs, csT, dtT, doutT, totB = q['cs'], q['csT'], q['dtT'], q['doutT'], q['totB']
        wstT = q['dstT'] * dtT
        for g in range(G):
            Bg = bc_ref[:, g * N:(g + 1) * N].astype(BF16)
            Cg = bc_ref[:, G * N + g * N:G * N + (g + 1) * N].astype(BF16)
            CBT = lax.dot_general(Bg, Cg, NT_DIMS, preferred_element_type=F32)
            HT = h_ref[g]
            yoffT = lax.dot_general(HT.astype(BF16), Cg, NT_DIMS, preferred_element_type=F32)
            xT = xsT_ref[g * GW:(g + 1) * GW, :]
            hs = [g * R + r for r in range(R)]
            blks = [slice(r * P, (r + 1) * P) for r in range(R)]
            segs = [jnp.where(q['up'], csT[h:h + 1, :] - cs[:, h:h + 1], -1e30) for h in hs]
            GTs = [(CBT * jnp.exp(sg)).astype(BF16) for sg in segs]
            xThs = [xT[b, :] for b in blks]
            XThs = [(xTh * dtT[h:h + 1, :]).astype(BF16) for xTh, h in zip(xThs, hs)]
            ydTs = [jnp.dot(a, GT, preferred_element_type=F32) for a, GT in zip(XThs, GTs)]
            ys = [ydT + yoffT[b, :] * doutT[h:h + 1, :] for ydT, b, h in zip(ydTs, blks, hs)]
            xws = [xTh * wstT[h:h + 1, :] for xTh, h in zip(xThs, hs)]
            tots = [jnp.broadcast_to(totB[h:h + 1, :], (P, N)) for h in hs]
            y_ref[:, g * GW:(g + 1) * GW] = jnp.concatenate(ys, axis=0).T
            xwT = jnp.concatenate(xws, axis=0).astype(BF16)
            ST = jnp.dot(xwT, Bg, preferred_element_type=F32)
            st_ref[0, g] = HT
            h_ref[g] = HT * jnp.concatenate(tots, axis=0) + ST
        if tail is not None:
            y = y_ref[...] + yo_ref[...]
            y_ref[...] = y
            z = _f32(z_ref[...])
            yz = (y + xs_ref[...] * dsk_ref[...]) * (z * _sig(z))
            for g in range(G):
                t = yz[:, g * GW:(g + 1) * GW]
                tn = t * lax.rsqrt(jnp.mean(t * t, axis=-1, keepdims=True) + RMS_EPS)
                yn_ref[:, g * GW:(g + 1) * GW] = (tn * ng_ref[:, g * GW:(g + 1) * GW]).astype(BF16)

    cidx = (lambda c: NC - 1 - c) if rev else (lambda c: c)
    cmap = lambda c: (cidx(c), 0)
    smap = lambda c: (cidx(c), 0, 0, 0)
    const = lambda c: (0, 0)
    tmap = lambda c: (0, cidx(c))
    in_specs = [pl.BlockSpec((DI, CHUNK), tmap), pl.BlockSpec((CHUNK, 2 * G * N), cmap), pl.BlockSpec((CHUNK, LANES), cmap),
                pl.BlockSpec((LANES, LANES), const), pl.BlockSpec((LANES, LANES), const)]
    out_specs = [pl.BlockSpec((CHUNK, DI), cmap), pl.BlockSpec((1, G, GW, N), smap)]
    out_shape = [jax.ShapeDtypeStruct((S, DI), F32), jax.ShapeDtypeStruct((NC, G, GW, N), F32)]
    args = [xsT, bc, dtraw, dtbT, alogT]
    if tail is not None:
        y_other, (z_arr, z_blk), xs_row, dsk, ng = tail
        in_specs += [pl.BlockSpec((CHUNK, DI), cmap), pl.BlockSpec((CHUNK, DI), lambda c: (cidx(c), z_blk)),
                     pl.BlockSpec((CHUNK, DI), cmap), pl.BlockSpec((1, DI), const), pl.BlockSpec((1, DI), const)]
        out_specs.append(pl.BlockSpec((CHUNK, DI), cmap))
        out_shape.append(jax.ShapeDtypeStruct((S, DI), BF16))
        args += [y_other, z_arr, xs_row, dsk, ng]
    return pl.pallas_call(
        body, name=name, grid=(NC,), in_specs=in_specs, out_specs=out_specs, out_shape=out_shape,
        scratch_shapes=[pltpu.VMEM((G, GW, N), F32)],
        compiler_params=_params(dimension_semantics=("arbitrary",)),
    )(*args)


def ssd_bwd(name, xsT, bc, dtraw, dyT, st, dtbT, alogT, *, S, DI, G, H, rev, tail=None):
    NC = S // CHUNK
    R = H // G
    GW = R * HEAD_DIM
    N = D_STATE
    XBC = DI + 2 * G * N
    P = HEAD_DIM
    L = CHUNK

    def body(*refs):
        xsT_ref, bc_ref, dtraw_ref, dyT_ref, st_ref, dtb_ref, alog_ref = refs[:7]
        if tail is None:
            dxbc_ref, ddt_ref, da_ref, dh_ref, dcst_ref, p2t_ref, p3t_ref, e2t_ref = refs[7:]
        else:
            other_ref, cbx_ref, cbbc_ref, dskT_ref = refs[7:11]
            dxbc_ref, ddt_ref, da_ref, dcol_ref, dh_ref, dcst_ref, p2t_ref, p3t_ref, e2t_ref = refs[11:]
        c = pl.program_id(0)

        @pl.when(c == 0)
        def _():
            dh_ref[...] = jnp.zeros_like(dh_ref)
            da_ref[...] = jnp.zeros_like(da_ref)
            dcst_ref[...] = jnp.zeros_like(dcst_ref)
            p2t_ref[...] = jnp.zeros_like(p2t_ref)
            p3t_ref[...] = jnp.zeros_like(p3t_ref)
            e2t_ref[...] = jnp.zeros_like(e2t_ref)

        q = _ssd_common(dtraw_ref[...], dtb_ref[...], alog_ref[...], rev, H)
        cs, csT, dtT, doutT, dstT, totB = q['cs'], q['csT'], q['dtT'], q['doutT'], q['dstT'], q['totB']
        wstT = dstT * dtT
        lane = q['ci']
        for g in range(G):
            Bg = bc_ref[:, g * N:(g + 1) * N].astype(BF16)
            Cg = bc_ref[:, G * N + g * N:G * N + (g + 1) * N].astype(BF16)
            CB = lax.dot_general(Cg, Bg, NT_DIMS, preferred_element_type=F32)
            HpT = st_ref[0, g]
            HpTb = HpT.astype(BF16)
            dHT = dh_ref[g]
            dHTb = dHT.astype(BF16)
            BdHT = lax.dot_general(dHTb, Bg, NT_DIMS, preferred_element_type=F32)
            yoffT = lax.dot_general(HpTb, Cg, NT_DIMS, preferred_element_type=F32)
            xT = xsT_ref[g * GW:(g + 1) * GW, :]
            dyT = dyT_ref[g * GW:(g + 1) * GW, :]
            hs = [g * R + r for r in range(R)]
            blks = [slice(r * P, (r + 1) * P) for r in range(R)]
            Lms = [jnp.exp(jnp.where(q['lo'], cs[:, h:h + 1] - csT[h:h + 1, :], -1e30)) for h in hs]
            xThs = [xT[b, :] for b in blks]
            dyThs = [dyT[b, :] for b in blks]
            xThbs = [v.astype(BF16) for v in xThs]
            dyThbs = [v.astype(BF16) for v in dyThs]
            dGxs = [lax.dot_general(a, b, TN_DIMS, preferred_element_type=F32) for a, b in zip(dyThbs, xThbs)]
            Gms = [(CB * Lm).astype(BF16) for Lm in Lms]
            XThbs = [(xTh * dtT[h:h + 1, :]).astype(BF16) for xTh, h in zip(xThs, hs)]
            u1Ts = [jnp.dot(a, Gm, preferred_element_type=F32) for a, Gm in zip(dyThbs, Gms)]
            ydTs = [lax.dot_general(a, Gm, NT_DIMS, preferred_element_type=F32) for a, Gm in zip(XThbs, Gms)]
            Ts = [dGx * (Lm * dtT[h:h + 1, :]) for dGx, Lm, h in zip(dGxs, Lms, hs)]
            dCB = Ts[0]
            for T in Ts[1:]:
                dCB = dCB + T
            uTs = [u1T + BdHT[b, :] * dstT[h:h + 1, :] for u1T, b, h in zip(u1Ts, blks, hs)]
            dyds = [dyTh * doutT[h:h + 1, :] for dyTh, h in zip(dyThs, hs)]
            xws = [xTh * wstT[h:h + 1, :] for xTh, h in zip(xThs, hs)]
            for r, h in enumerate(hs):
                b = blks[r]
                p3row = jnp.sum(xws[r] * BdHT[b, :], axis=0, keepdims=True)
                seg_row = jnp.sum(_f32(dyThbs[r]) * ydTs[r], axis=0, keepdims=True)
                seg_col = jnp.sum(_f32(XThbs[r]) * u1Ts[r], axis=0, keepdims=True)
                dcst_ref[h:h + 1, :] = (jnp.sum(dyds[r] * yoffT[b, :], axis=0, keepdims=True)
                                        + seg_row - seg_col - p3row)
                p2t_ref[h:h + 1, :] = jnp.sum(xThs[r] * uTs[r], axis=0, keepdims=True)
                p3t_ref[h:h + 1, :] = p3row
                e2t_ref[h:h + 1, :] = jnp.sum(HpT[b, :] * dHT[b, :], axis=0, keepdims=True)
            dxs = [uT * dtT[h:h + 1, :] for uT, h in zip(uTs, hs)]
            if tail is not None:
                dxs = [d + dyTh * dskT_ref[g * GW + r * P:g * GW + (r + 1) * P, :]
                       for r, (d, dyTh) in enumerate(zip(dxs, dyThs))]
            tots = [jnp.broadcast_to(totB[h:h + 1, :], (P, N)) for h in hs]
            dxbc_ref[:, g * GW:(g + 1) * GW] = jnp.concatenate(dxs, axis=0).T
            dydT = jnp.concatenate(dyds, axis=0).astype(BF16)
            xwT = jnp.concatenate(xws, axis=0).astype(BF16)
            dCBb = dCB.astype(BF16)
            dC = (jnp.dot(dCBb, Bg, preferred_element_type=F32)
                  + lax.dot_general(dydT, HpTb, TN_DIMS, preferred_element_type=F32))
            dB = (lax.dot_general(dCBb, Cg, TN_DIMS, preferred_element_type=F32)
                  + lax.dot_general(xwT, dHTb, TN_DIMS, preferred_element_type=F32))
            dxbc_ref[:, DI + g * N:DI + (g + 1) * N] = dB
            dxbc_ref[:, DI + G * N + g * N:DI + G * N + (g + 1) * N] = dC
            dh_ref[g] = (dHT * jnp.concatenate(tots, axis=0)
                         + jnp.dot(dydT, Cg, preferred_element_type=F32))
        e1 = jnp.sum(p3t_ref[...], axis=1, keepdims=True)
        e2 = jnp.sum(e2t_ref[...], axis=1, keepdims=True)
        dcsT = dcst_ref[...] + jnp.where(lane == q['last'], e1 + totB * e2, 0.0)
        daT = _split_dot_r(dcsT, q['lo'].astype(BF16), 3)
        ddtT = daT * q['AT'] + p2t_ref[...]
        da_ref[...] += daT * dtT
        ddraw = jnp.where(lane < H, (ddtT * _sig(q['preT'])).T, 0.0)
        if rev:
            ddraw = pltpu.roll(ddraw, H, 1)
        ddt_ref[...] = ddraw
        if tail is not None:
            for c0, cb_ref in ((0, cbx_ref), (DI, cbbc_ref)):
                d = dxbc_ref[:, c0:c0 + DI] + other_ref[:, c0:c0 + DI]
                cb = cb_ref[...]
                dcb = d * _dsilu(cb, _sig(cb))
                dxbc_ref[:, c0:c0 + DI] = dcb
                part = _rows8(dcb)

                @pl.when(c == 0)
                def _():
                    dcol_ref[:, c0:c0 + DI] = part

                @pl.when(c > 0)
                def _():
                    dcol_ref[:, c0:c0 + DI] += part

    cmap = (lambda c: (c, 0)) if rev else (lambda c: (NC - 1 - c, 0))
    smap = (lambda c: (c, 0, 0, 0)) if rev else (lambda c: (NC - 1 - c, 0, 0, 0))
    const = lambda c: (0, 0)
    sq = pltpu.VMEM((LANES, CHUNK), F32)
    cix = (lambda c: c) if rev else (lambda c: NC - 1 - c)
    tmap = lambda c: (0, cix(c))
    in_specs = [pl.BlockSpec((DI, CHUNK), tmap), pl.BlockSpec((CHUNK, 2 * G * N), cmap), pl.BlockSpec((CHUNK, LANES), cmap),
                pl.BlockSpec((DI, CHUNK), tmap),
                pl.BlockSpec((1, G, GW, N), smap),
                pl.BlockSpec((LANES, LANES), const), pl.BlockSpec((LANES, LANES), const)]
    out_specs = [pl.BlockSpec((CHUNK, XBC), cmap), pl.BlockSpec((CHUNK, LANES), cmap),
                 pl.BlockSpec((LANES, LANES), const)]
    out_shape = [jax.ShapeDtypeStruct((S, XBC), F32), jax.ShapeDtypeStruct((S, LANES), F32),
                 jax.ShapeDtypeStruct((LANES, LANES), F32)]
    args = [xsT, bc, dtraw, dyT, st, dtbT, alogT]
    if tail is not None:
        in_specs += [pl.BlockSpec((CHUNK, XBC), cmap), pl.BlockSpec((CHUNK, DI), cmap),
                     pl.BlockSpec((CHUNK, 2 * G * N), cmap), pl.BlockSpec((DI, LANES), const)]
        out_specs.append(pl.BlockSpec((SUBLANES, XBC), const))
        out_shape.append(jax.ShapeDtypeStruct((SUBLANES, XBC), F32))
        args += list(tail)
    return pl.pallas_call(
        body, name=name, grid=(NC,), in_specs=in_specs, out_specs=out_specs, out_shape=out_shape,
        scratch_shapes=[pltpu.VMEM((G, GW, N), F32), sq, sq, sq, sq],
        compiler_params=_params(dimension_semantics=("arbitrary",)),
    )(*args)


ANY = pl.BlockSpec(memory_space=pl.ANY)


def chip_exchange(name, groups, gather):
    flat = [arr for grp in groups for arr in grp]
    n_in = len(flat)
    n_out = len(groups)
    n_rc = 3 * n_in

    def body(*refs):
        in_refs = refs[:n_in]
        out_refs = refs[n_in:n_in + n_out]
        send, recv, loc = refs[n_in + n_out:]
        x, y, c = lax.axis_index("x"), lax.axis_index("y"), lax.axis_index("c")
        me = 2 * x + y
        peers = [(1 - x, y), (x, 1 - y), (1 - x, 1 - y)]
        local, remote = [], []
        q = 0
        for a, grp in enumerate(groups):
            for l in range(len(grp)):
                src = in_refs[q]
                dst = out_refs[a].at[me] if gather else out_refs[a].at[me, l]
                own = src if gather else src.at[me]
                lc = pltpu.make_async_copy(own, dst, loc.at[q])
                lc.start()
                local.append(lc)
                for j, (px, py) in enumerate(peers):
                    blk = src if gather else src.at[2 * px + py]
                    rc = pltpu.make_async_remote_copy(
                        src_ref=blk, dst_ref=dst, send_sem=send.at[3 * q + j], recv_sem=recv.at[3 * q + j],
                        device_id=(px, py, c), device_id_type=MESH)
                    rc.start()
                    remote.append(rc)
                q += 1
        for lc in local:
            lc.wait()
        for rc in remote:
            rc.wait()

    out_shape = []
    for grp in groups:
        a0 = grp[0]
        if gather:
            out_shape.append(jax.ShapeDtypeStruct((4,) + a0.shape, a0.dtype))
        else:
            out_shape.append(jax.ShapeDtypeStruct((4, len(grp)) + a0.shape[1:], a0.dtype))
    return pl.pallas_call(
        body, name=name, in_specs=[ANY] * n_in, out_specs=[ANY] * n_out, out_shape=out_shape,
        scratch_shapes=[pltpu.SemaphoreType.DMA((n_rc,)), pltpu.SemaphoreType.DMA((n_rc,)),
                        pltpu.SemaphoreType.DMA((n_in,))],
    )(*flat)


def gather_layer(name, split, whole):
    ns, nw = len(split), len(whole)
    n = ns + nw
    n_rc = 3 * (n + ns)

    def body(*refs):
        in_refs = refs[:n]
        out_refs = refs[n:2 * n]
        send, recv, loc = refs[2 * n:]
        x, y, c = lax.axis_index("x"), lax.axis_index("y"), lax.axis_index("c")
        me = 2 * x + y
        sibling = (x, y, 1 - c)
        peers = [(1 - x, y), (x, 1 - y), (1 - x, 1 - y)]

        def region(a, chip, half):
            if a >= ns:
                return out_refs[a].at[chip]
            hr = split[a].shape[0] // 2
            return out_refs[a].at[chip, pl.ds(half * hr, hr)]

        def mine(a):
            if a >= ns:
                return in_refs[a]
            hr = split[a].shape[0] // 2
            return in_refs[a].at[pl.ds(c * hr, hr)]

        local = []
        for a in range(n):
            lc = pltpu.make_async_copy(in_refs[a], out_refs[a].at[me], loc.at[a])
            lc.start()
            local.append(lc)
        sends = []
        for a in range(n):
            for j, (px, py) in enumerate(peers):
                rc = pltpu.make_async_remote_copy(
                    src_ref=mine(a), dst_ref=region(a, me, c), send_sem=send.at[3 * a + j],
                    recv_sem=recv.at[3 * a + j], device_id=(px, py, c), device_id_type=MESH)
                rc.start()
                sends.append(rc)
        for a in range(n):
            for j, (px, py) in enumerate(peers):
                chip = 2 * px + py
                landed = pltpu.make_async_remote_copy(
                    src_ref=mine(a), dst_ref=region(a, chip, c), send_sem=send.at[3 * a + j],
                    recv_sem=recv.at[3 * a + j], device_id=(px, py, c), device_id_type=MESH)
                landed.wait_recv()
                if a < ns:
                    fw = pltpu.make_async_remote_copy(
                        src_ref=region(a, chip, c), dst_ref=region(a, chip, c), send_sem=send.at[3 * n + 3 * a + j],
                        recv_sem=recv.at[3 * n + 3 * a + j], device_id=sibling, device_id_type=MESH)
                    fw.start()
                    sends.append(fw)
        for a in range(ns):
            for j, (px, py) in enumerate(peers):
                chip = 2 * px + py
                pltpu.make_async_remote_copy(
                    src_ref=region(a, chip, 1 - c), dst_ref=region(a, chip, 1 - c), send_sem=send.at[3 * n + 3 * a + j],
                    recv_sem=recv.at[3 * n + 3 * a + j], device_id=sibling, device_id_type=MESH).wait_recv()
        for rc in sends:
            rc.wait_send()
        for lc in local:
            lc.wait()

    arrs = list(split) + list(whole)
    return pl.pallas_call(
        body, name=name, in_specs=[ANY] * n, out_specs=[ANY] * n,
        out_shape=[jax.ShapeDtypeStruct((4,) + a.shape, a.dtype) for a in arrs],
        scratch_shapes=[pltpu.SemaphoreType.DMA((n_rc,)), pltpu.SemaphoreType.DMA((n_rc,)),
                        pltpu.SemaphoreType.DMA((n,))],
    )(*arrs)


def core_send_half(name, arrs):
    n = len(arrs)

    def body(*refs):
        in_refs = refs[:n]
        out_refs = refs[n:2 * n]
        send, recv = refs[2 * n:]
        c = lax.axis_index("c")
        peer = (lax.axis_index("x"), lax.axis_index("y"), 1 - c)
        rcs = []
        for a in range(n):
            hr = arrs[a].shape[1] // 2
            rc = pltpu.make_async_remote_copy(
                src_ref=in_refs[a].at[:, pl.ds((1 - c) * hr, hr)], dst_ref=out_refs[a], send_sem=send.at[a],
                recv_sem=recv.at[a], device_id=peer, device_id_type=MESH)
            rc.start()
            rcs.append(rc)
        for rc in rcs:
            rc.wait()

    return pl.pallas_call(
        body, name=name, in_specs=[ANY] * n, out_specs=[ANY] * n,
        out_shape=[jax.ShapeDtypeStruct((4, a.shape[1] // 2, a.shape[2]), a.dtype) for a in arrs],
        scratch_shapes=[pltpu.SemaphoreType.DMA((n,)), pltpu.SemaphoreType.DMA((n,))],
    )(*arrs)


def core_fill(name, arrs, layer, n_layers):
    n = len(arrs)

    def body(*refs):
        out_refs = refs[n:2 * n]
        send, recv = refs[2 * n:]
        c = lax.axis_index("c")
        peer = (lax.axis_index("x"), lax.axis_index("y"), 1 - c)
        rcs = []
        for a in range(n):
            r = arrs[a].shape[0] // n_layers
            hr = r // 2
            rows = out_refs[a].at[pl.ds(layer * r + c * hr, hr)]
            rc = pltpu.make_async_remote_copy(src_ref=rows, dst_ref=rows, send_sem=send.at[a], recv_sem=recv.at[a],
                                              device_id=peer, device_id_type=MESH)
            rc.start()
            rcs.append(rc)
        for a in range(n):
            r = arrs[a].shape[0] // n_layers
            hr = r // 2
            theirs = out_refs[a].at[pl.ds(layer * r + (1 - c) * hr, hr)]
            pltpu.make_async_remote_copy(src_ref=theirs, dst_ref=theirs, send_sem=send.at[a], recv_sem=recv.at[a],
                                         device_id=peer, device_id_type=MESH).wait_recv()
        for rc in rcs:
            rc.wait_send()

    return pl.pallas_call(
        body, name=name, in_specs=[ANY] * n, out_specs=[ANY] * n,
        out_shape=[jax.ShapeDtypeStruct(a.shape, a.dtype) for a in arrs],
        input_output_aliases={a: a for a in range(n)},
        scratch_shapes=[pltpu.SemaphoreType.DMA((n,)), pltpu.SemaphoreType.DMA((n,))],
    )(*arrs)


def sibling_swap(name, arrs):
    n = len(arrs)

    def body(*refs):
        in_refs = refs[:n]
        out_refs = refs[n:2 * n]
        send, recv = refs[2 * n:]
        peer = (lax.axis_index("x"), lax.axis_index("y"), 1 - lax.axis_index("c"))
        rcs = []
        for a in range(n):
            rc = pltpu.make_async_remote_copy(src_ref=in_refs[a], dst_ref=out_refs[a], send_sem=send.at[a],
                                              recv_sem=recv.at[a], device_id=peer, device_id_type=MESH)
            rc.start()
            rcs.append(rc)
        for rc in rcs:
            rc.wait()

    return pl.pallas_call(
        body, name=name, in_specs=[ANY] * n, out_specs=[ANY] * n,
        out_shape=[jax.ShapeDtypeStruct(a.shape, a.dtype) for a in arrs],
        scratch_shapes=[pltpu.SemaphoreType.DMA((n,)), pltpu.SemaphoreType.DMA((n,))],
    )(*arrs)


def all8_gather(name, v):
    flips = [(fx, fy, fc) for fx in (0, 1) for fy in (0, 1) for fc in (0, 1) if (fx, fy, fc) != (0, 0, 0)]

    def body(v_ref, out_ref, send, recv, loc):
        x, y, c = lax.axis_index("x"), lax.axis_index("y"), lax.axis_index("c")
        me = 4 * x + 2 * y + c
        lc = pltpu.make_async_copy(v_ref, out_ref.at[me], loc)
        lc.start()
        rcs = []
        for k, (fx, fy, fc) in enumerate(flips):
            tgt = (x + fx - 2 * x * fx, y + fy - 2 * y * fy, c + fc - 2 * c * fc)
            rc = pltpu.make_async_remote_copy(src_ref=v_ref, dst_ref=out_ref.at[me], send_sem=send.at[k],
                                              recv_sem=recv.at[k], device_id=tgt, device_id_type=MESH)
            rc.start()
            rcs.append(rc)
        lc.wait()
        for rc in rcs:
            rc.wait()

    return pl.pallas_call(
        body, name=name, in_specs=[ANY], out_specs=ANY,
        out_shape=jax.ShapeDtypeStruct((8,) + v.shape, v.dtype),
        scratch_shapes=[pltpu.SemaphoreType.DMA((7,)), pltpu.SemaphoreType.DMA((7,)), pltpu.SemaphoreType.DMA],
    )(v)


def _pick_rows(rows, cols, target_elems=128 * 1024, mult=SUBLANES):
    if rows % mult != 0:
        return rows
    best = mult
    t = mult
    while t <= rows:
        if rows % t == 0 and t * cols <= target_elems:
            best = t
        t += mult
    return best


def sum_chips(name, parts):
    _, R, C = parts.shape
    tm = _pick_rows(R, C)

    def body(p_ref, o_ref):
        o_ref[...] = (p_ref[0] + p_ref[1]) + (p_ref[2] + p_ref[3])

    return pl.pallas_call(
        body, name=name, grid=(R // tm,),
        in_specs=[pl.BlockSpec((4, tm, C), lambda i: (0, i, 0))],
        out_specs=pl.BlockSpec((tm, C), lambda i: (i, 0)),
        out_shape=jax.ShapeDtypeStruct((R, C), F32),
        compiler_params=_params(dimension_semantics=("arbitrary",)),
    )(parts)


def _adamw(g, w, m, v):
    m = ADAM_B1 * m + (1.0 - ADAM_B1) * g
    v = ADAM_B2 * v + (1.0 - ADAM_B2) * (g * g)
    m_hat = m / (1.0 - ADAM_B1 ** ADAM_STEP)
    v_hat = v / (1.0 - ADAM_B2 ** ADAM_STEP)
    delta = -ADAM_LR * (m_hat / (jnp.sqrt(v_hat) + ADAM_EPS) + ADAM_WD * w)
    return delta, m, v


def adamw_shard(name, s_mine, s_sib, w, m, v):
    R, C = w.shape
    tm = _pick_rows(R, C)

    def body(a_ref, b_ref, w_ref, m_ref, v_ref, g_out, d_out, m_out, v_out):
        g = a_ref[...] + b_ref[...]
        d, mn, vn = _adamw(g, w_ref[...], m_ref[...], v_ref[...])
        g_out[...] = g
        d_out[...] = d
        m_out[...] = mn
        v_out[...] = vn

    spec = pl.BlockSpec((tm, C), lambda i: (i, 0))
    return pl.pallas_call(
        body, name=name, grid=(R // tm,), in_specs=[spec] * 5, out_specs=[spec] * 4,
        out_shape=[jax.ShapeDtypeStruct((R, C), F32)] * 4,
        compiler_params=_params(dimension_semantics=("arbitrary",)),
    )(s_mine, s_sib, w, m, v)


def core_sum(name, core, g, got):
    _, r, C = g.shape
    hr = r // 2
    tm = _pick_rows(hr, 4 * C, 256 * 1024, 2 * SUBLANES)
    nh = hr // tm

    def body(c_ref, g_ref, s_ref, o_ref):
        o_ref[...] = (g_ref[...] + s_ref[...]).astype(BF16)

    return pl.pallas_call(
        body, name=name,
        grid_spec=pltpu.PrefetchScalarGridSpec(
            num_scalar_prefetch=1, grid=(nh,),
            in_specs=[pl.BlockSpec((4, tm, C), lambda i, cr: (0, cr[0] * nh + i, 0)),
                      pl.BlockSpec((4, tm, C), lambda i, cr: (0, i, 0))],
            out_specs=pl.BlockSpec((4, tm, C), lambda i, cr: (0, i, 0))),
        out_shape=jax.ShapeDtypeStruct((4, hr, C), BF16),
        compiler_params=_params(dimension_semantics=("arbitrary",)),
    )(core, g, got)


def chip_sum_into(name, core, parts, layer, n_layers, into=None):
    _, hr, C = parts.shape
    r = 2 * hr
    tm = _pick_rows(hr, 4 * C, 256 * 1024, 2 * SUBLANES)
    nh = hr // tm

    def body(c_ref, p_ref, *rest):
        o_ref = rest[-1]
        o_ref[...] = (_f32(p_ref[0]) + _f32(p_ref[1])) + (_f32(p_ref[2]) + _f32(p_ref[3]))

    in_specs = [pl.BlockSpec((4, tm, C), lambda i, cr: (0, i, 0))]
    args = [core, parts]
    aliases = {}
    if into is not None:
        in_specs.append(pl.BlockSpec(memory_space=pl.ANY))
        args.append(into)
        aliases = {2: 0}
    return pl.pallas_call(
        body, name=name,
        grid_spec=pltpu.PrefetchScalarGridSpec(
            num_scalar_prefetch=1, grid=(nh,), in_specs=in_specs,
            out_specs=pl.BlockSpec((tm, C), lambda i, cr: ((layer * r) // tm + cr[0] * nh + i, 0))),
        out_shape=jax.ShapeDtypeStruct((n_layers * r, C), F32), input_output_aliases=aliases,
        compiler_params=_params(dimension_semantics=("arbitrary",)),
    )(*args)


def adamw_full(name, g, w, m, v):
    R, C = w.shape
    tm = _pick_rows(R, C)

    def body(g_ref, w_ref, m_ref, v_ref, d_out, m_out, v_out):
        d, mn, vn = _adamw(g_ref[...], w_ref[...], m_ref[...], v_ref[...])
        d_out[...] = d
        m_out[...] = mn
        v_out[...] = vn

    spec = pl.BlockSpec((tm, C), lambda i: (i, 0))
    return pl.pallas_call(
        body, name=name, grid=(R // tm,), in_specs=[spec] * 4, out_specs=[spec] * 3,
        out_shape=[jax.ShapeDtypeStruct((R, C), F32)] * 3,
        compiler_params=_params(dimension_semantics=("arbitrary",)),
    )(g, w, m, v)


def adamw_small(name, parts, w, m, v):
    W = w.shape[1]

    def body(p_ref, w_ref, m_ref, v_ref, g_out, d_out, m_out, v_out):
        acc = p_ref[0]
        for k in range(1, 8):
            acc = acc + p_ref[k]
        g = jnp.sum(acc, axis=0, keepdims=True)
        d, mn, vn = _adamw(g, w_ref[...], m_ref[...], v_ref[...])
        g_out[...] = g
        d_out[...] = d
        m_out[...] = mn
        v_out[...] = vn

    return pl.pallas_call(
        body, name=name, out_shape=[jax.ShapeDtypeStruct((1, W), F32)] * 4,
        compiler_params=_params(),
    )(parts, w, m, v)


def _pad_lanes(v, width=LANES):
    return jnp.pad(v, ((0, 0), (0, width - v.shape[1])))


def _layer_fwd(cf, x, xb, pb, W, sm):
    S, D, CD, DI, XBC, F, H, G = cf['S'], cf['D'], cf['CD'], cf['DI'], cf['XBC'], cf['F'], cf['H'], cf['G']
    NM = cf['NM']
    alpha = cf['alpha']
    tm = cf['tm']
    tmx = cf['tmx']
    tn_in = cf['tn_in']
    sv = {}

    ident = lambda acc, ex: ([acc], [])
    proj, = fused_mm("in_proj", [(xb, W['in_main'], 0, False)], [], ident, [(NM, BF16, tn_in, 0)],
                     M=S, tm=tmx, tn=tn_in, nj=NM // tn_in)
    dtraw, = fused_mm("dt_proj", [(xb, W['in_dt'], 0, False)], [], ident, [(LANES, F32, LANES, 0)],
                      M=S, tm=tmx, tn=LANES)

    u, = row_call("glu", lambda a, gt: ([a * _sig(gt)], []),
                  [(proj, 'row', CD, 0, 0), (proj, 'row', CD, 1, 0)], [(CD, F32, CD, 0, 0)], M=S, tm=tm)

    def conv_a_epi(conv, ex):
        cb_, g_, b_ = ex
        ca = conv + cb_
        xhat, _ = _ln_stats(ca)
        la = xhat * g_ + b_
        return [ca, la * _sig(la)], []

    ca, sa = conv_call("conv_a", u, 0, sm['conv_a_w'], cf['KA'], conv_a_epi,
                       [(sm['conv_a_b'], 'vec', CD, 0, 0), (sm['ln_a_g'], 'vec', CD, 0, 0), (sm['ln_a_b'], 'vec', CD, 0, 0)],
                       [(CD, F32, CD, 0, 0), (CD, BF16, CD, 0, 0)], M=S, tm=cf['tmc'], cw=CD, nc=1, reverse=False)
    y_a, = fused_mm("a_out", [(sa, W['a_out'], 0, False)], [], ident, [(D, F32, D, 0)], M=S, tm=tmx, tn=D)

    def conv_x_epi(conv, ex):
        cb = conv + ex[0]
        act = cb * _sig(cb)
        return [cb, act], [], [act]

    def conv_bc_epi(conv, ex):
        cb = conv + ex[0]
        return [cb, cb * _sig(cb)], []

    xoff = (2 * CD + 2 * D + DI) // DI
    cbv_x, xs, xsT = conv_call("conv_b_x", proj, xoff, sm['ssm_conv_w'], cf['KB'], conv_x_epi,
                               [(sm['ssm_conv_b'], 'vec', DI, 0, 0)],
                               [(DI, F32, DI, 0, 0), (DI, F32, DI, 0, 0)], M=S, tm=cf['tmc'], cw=DI, nc=1,
                               reverse=False, t_outs=[(DI, F32, DI, 0, 0)])
    cbv_bc, bc = conv_call("conv_b_bc", proj, xoff + 1, sm['ssm_conv_w'], cf['KB'], conv_bc_epi,
                           [(sm['ssm_conv_b'], 'vec', DI, 1, 0)],
                           [(DI, F32, DI, 0, 0), (DI, F32, DI, 0, 0)], M=S, tm=cf['tmc'], cw=DI, nc=1,
                           reverse=False, w_c0=1)
    y_f, st_f = ssd_fwd("ssd_fwd_f", xsT, bc, dtraw, sm['dtb_f'], sm['alog_f'], S=S, DI=DI, G=G, H=H, rev=False)
    zoff = (2 * CD + 2 * D) // DI
    ysum, st_r, yn = ssd_fwd("ssd_fwd_r", xsT, bc, dtraw, sm['dtb_r'], sm['alog_r'], S=S, DI=DI, G=G, H=H, rev=True,
                             tail=(y_f, (proj, zoff), xs, sm['dskip_full'], sm['ssm_norm_g']))
    goff = (2 * CD) // D

    def merge_epi(acc, ex):
        ga, gb, ya = ex
        return [acc, _sig(ga) * ya + _sig(gb) * acc], []

    y_b, merged = fused_mm("b_out", [(yn, W['b_out'], 0, False)],
                           [(proj, 'row', D, goff), (proj, 'row', D, goff + 1), (y_a, 'row', D, 0)],
                           merge_epi, [(D, F32, D, 0), (D, BF16, D, 0)], M=S, tm=tm, tn=D)

    def mix_epi(acc, ex):
        xin, g_, b_ = ex
        r1 = alpha * xin + acc
        xhat, _ = _ln_stats(r1)
        return [r1, xhat * g_ + b_], []

    r1, hb = fused_mm("o_mix", [(merged, W['o'], 0, False)],
                      [(x, 'row', D, 0), (sm['ln1_g'], 'vec', D, 0), (sm['ln1_b'], 'vec', D, 0)],
                      mix_epi, [(D, F32, D, 0), (D, BF16, D, 0)], M=S, tm=tm, tn=D)

    tnf = cf['tnf']

    g32, g_ = fused_mm("ffn_gate", [(hb, W['gate_up'], 0, False)], [], lambda acc, ex: ([acc, acc], []),
                       [(F, F32, tnf, 0), (F, BF16, tnf, 0)], M=S, tm=tmx, tn=tnf, nj=F // tnf)
    u_, f = fused_mm("ffn_up", [(hb, W['gate_up'], F // tnf, False)], [(g32, 'row', tnf, 0)],
                     lambda acc, ex: ([acc, ex[0] * _sig(ex[0]) * acc], []),
                     [(F, BF16, tnf, 0), (F, BF16, tnf, 0)], M=S, tm=tmx, tn=tnf, nj=F // tnf)

    def down_epi(acc, ex):
        r1_, g1, b1, g2, b2 = ex
        xh1, _ = _ln_stats(r1_)
        r2 = alpha * (xh1 * g1 + b1) + acc
        xh2, _ = _ln_stats(r2)
        return [r2, xh2 * g2 + b2], []

    r2, h2b = fused_mm("ffn_down", [(f, W['down'], 0, False)],
                       [(r1, 'row', D, 0), (sm['ln1_g'], 'vec', D, 0), (sm['ln1_b'], 'vec', D, 0),
                        (sm['ln2_g'], 'vec', D, 0), (sm['ln2_b'], 'vec', D, 0)],
                       down_epi, [(D, F32, D, 0), (D, BF16, D, 0)], M=S, tm=tm, tn=D)

    pe, = fused_mm("ple_proj", [(pb, W['ple'], 0, False)], [], ident, [(D, F32, D, 0)], M=S, tm=tmx, tn=D)

    def ple_epi(acc, ex):
        r2_, g2, b2, pe_, pg = ex
        xh2, _ = _ln_stats(r2_)
        h2 = xh2 * g2 + b2
        e = pe_ * lax.rsqrt(jnp.mean(pe_ * pe_, axis=-1, keepdims=True) + RMS_EPS) * pg
        xn = h2 + e * _sig(acc)
        return [acc, xn, xn], []

    t_, xn, xnb = fused_mm("ple_gate", [(h2b, W['ple_gate'], 0, False)],
                           [(r2, 'row', D, 0), (sm['ln2_g'], 'vec', D, 0), (sm['ln2_b'], 'vec', D, 0),
                            (pe, 'row', D, 0), (sm['ple_norm_g'], 'vec', D, 0)],
                           ple_epi, [(D, F32, D, 0), (D, F32, D, 0), (D, BF16, D, 0)], M=S, tm=tm, tn=D)
    sv.update(x=x, xb=xb, pb=pb, proj=proj, dtraw=dtraw, u=u, ca=ca, sa=sa, y_a=y_a, cbv_x=cbv_x, cbv_bc=cbv_bc,
              xs=xs, xsT=xsT, bc=bc,
              ysum=ysum, st_f=st_f, st_r=st_r, yn=yn, y_b=y_b, merged=merged, r1=r1, hb=hb,
              g_=g_, u_=u_, f=f, r2=r2, h2b=h2b, t_=t_, pe=pe)
    return xn, xnb, sv


def _layer_bwd(cf, sv, W, sm, dxn=None, target=None, xn=None):
    S, D, CD, DI, XBC, F, H, G = cf['S'], cf['D'], cf['CD'], cf['DI'], cf['XBC'], cf['F'], cf['H'], cf['G']
    NM = cf['NM']
    alpha = cf['alpha']
    tm = cf['tm']
    gw = cf['GW']
    out = {}

    def ple_bwd_core(dx_, t, pe_, pg):
        s = _sig(t)
        rinv = lax.rsqrt(jnp.mean(pe_ * pe_, axis=-1, keepdims=True) + RMS_EPS)
        pn = pe_ * rinv
        e = pn * pg
        dtg = dx_ * e * (s * (1.0 - s))
        de = dx_ * s
        qv = de * pg
        dpe = rinv * (qv - pn * jnp.mean(qv * pn, axis=-1, keepdims=True))
        return dtg, dpe, de * pn

    if dxn is None:
        def head(xn_, tgt, t, pe_, pg):
            err = xn_ - tgt
            dx_ = err * (1.0 / D)
            dtg, dpe, dpg = ple_bwd_core(dx_, t, pe_, pg)
            return [dx_, dtg, dpe], [dpg, err * err]

        (dxn, dtg, dpe, dpg, lsq) = row_call(
            "loss_ple_bwd", head,
            [(xn, 'row', D, 0, 0), (target, 'row', D, 0, 0), (sv['t_'], 'row', D, 0, 0), (sv['pe'], 'row', D, 0, 0),
             (sm['ple_norm_g'], 'vec', D, 0, 0)],
            [(D, F32, D, 0, 0), (D, BF16, D, 0, 0), (D, BF16, D, 0, 0)], [(D, D, 0, 0), (D, D, 0, 0)], M=S, tm=tm)
        out['loss_sq'] = lsq
    else:
        def mid(dx_, t, pe_, pg):
            dtg, dpe, dpg = ple_bwd_core(dx_, t, pe_, pg)
            return [dtg, dpe], [dpg]

        (dtg, dpe, dpg) = row_call(
            "ple_bwd", mid,
            [(dxn, 'row', D, 0, 0), (sv['t_'], 'row', D, 0, 0), (sv['pe'], 'row', D, 0, 0),
             (sm['ple_norm_g'], 'vec', D, 0, 0)],
            [(D, BF16, D, 0, 0), (D, BF16, D, 0, 0)], [(D, D, 0, 0)], M=S, tm=tm)
    out['ple_norm_g'] = dpg

    def ln_bwd_epi(scale):
        def epi(acc, ex):
            res, r_, g_ = ex
            dh = scale * res + acc
            xhat, rstd = _ln_stats(r_)
            dr = _ln_bwd(dh, xhat, rstd, g_)
            return [dr, dr], [dh * xhat, dh]
        return epi

    dr2, dr2b, dg2, db2 = fused_mm(
        "dh2", [(dtg, W['ple_gate_T'], 0, False)],
        [(dxn, 'row', D, 0), (sv['r2'], 'row', D, 0), (sm['ln2_g'], 'vec', D, 0)],
        ln_bwd_epi(1.0), [(D, F32, D, 0), (D, BF16, D, 0)], [(D, D, 0), (D, D, 0)], M=S, tm=tm, tn=D)
    out['ln2_g'], out['ln2_b'] = dg2, db2

    tnf = cf['tnf']

    def dswiglu_epi(acc, ex):
        gg, uu = ex
        s = _sig(gg)
        return [acc * uu * _dsilu(gg, s), acc * (gg * s)], []

    dg_b, du_b = fused_mm(
        "d_down", [(dr2b, W['down_T'], 0, False)],
        [(sv['g_'], 'row', tnf, 0), (sv['u_'], 'row', tnf, 0)], dswiglu_epi,
        [(F, BF16, tnf, 0), (F, BF16, tnf, 0)], M=S, tm=tm, tn=tnf, nj=F // tnf)

    dr1, dr1b, dg1, db1 = fused_mm(
        "dh1", [(dg_b, W['gate_T'], 0, True), (du_b, W['up_T'], 0, True)],
        [(dr2, 'row', D, 0), (sv['r1'], 'row', D, 0), (sm['ln1_g'], 'vec', D, 0)],
        ln_bwd_epi(alpha), [(D, F32, D, 0), (D, BF16, D, 0)], [(D, D, 0), (D, D, 0)],
        M=S, tm=tm, tn=D, nk=cf['nk_f'])
    out['ln1_g'], out['ln1_b'] = dg1, db1

    goff = (2 * CD) // D

    def dmerge_epi(acc, ex):
        ga, gb, ya, yb = ex
        sa_, sb_ = _sig(ga), _sig(gb)
        dga = acc * ya * (sa_ * (1.0 - sa_))
        dgb = acc * yb * (sb_ * (1.0 - sb_))
        return [jnp.concatenate([dga, dgb], axis=1), acc * sa_, acc * sb_], []

    dproj, dya_b, dyb_b = fused_mm(
        "d_merge", [(dr1b, W['o_T'], 0, False)],
        [(sv['proj'], 'row', D, goff), (sv['proj'], 'row', D, goff + 1), (sv['y_a'], 'row', D, 0), (sv['y_b'], 'row', D, 0)],
        dmerge_epi, [(NM, BF16, 2 * D, (2 * CD) // (2 * D)), (D, BF16, D, 0), (D, BF16, D, 0)], M=S, tm=tm, tn=D)

    def dsa_epi(acc, ex):
        ca_, g_, b_ = ex
        xhat, rstd = _ln_stats(ca_)
        la = xhat * g_ + b_
        dla = acc * _dsilu(la, _sig(la))
        dca = _ln_bwd(dla, xhat, rstd, g_)
        return [dca], [dla * xhat, dla, dca]

    dca, dlag, dlab, dcab = fused_mm(
        "d_a_out", [(dya_b, W['a_out_T'], 0, False)],
        [(sv['ca'], 'row', CD, 0), (sm['ln_a_g'], 'vec', CD, 0), (sm['ln_a_b'], 'vec', CD, 0)],
        dsa_epi, [(CD, F32, CD, 0)], [(CD, CD, 0), (CD, CD, 0), (CD, CD, 0)], M=S, tm=tm, tn=D)
    out['ln_a_g'], out['ln_a_b'], out['conv_a_b'] = dlag, dlab, dcab

    def dglu_epi(du, ex):
        a, gt = ex
        s = _sig(gt)
        return [jnp.concatenate([du * s, du * a * (s * (1.0 - s))], axis=1)], []

    dproj, dwa = conv_call(
        "d_conv_a", dca, 0, sm['conv_a_w'], cf['KA'], dglu_epi,
        [(sv['proj'], 'row', CD, 0, 0), (sv['proj'], 'row', CD, 1, 0)],
        [(NM, BF16, 2 * CD, 0, 0)], M=S, tm=cf['tmc'], cw=CD, nc=1, reverse=True, xin=(sv['u'], 0),
        passthrough=(dproj, 0))
    out['conv_a_w'] = dwa

    zoff = (2 * CD + 2 * D) // DI

    def dgate_norm_epi(acc, ex):
        ysum_, xs, z, dsk, ng = ex
        y = ysum_ + xs * dsk
        sz = _sig(z)
        siluz = z * sz
        yz = y * siluz
        dyzs, yhats = [], []
        for g in range(G):
            t = yz[:, g * gw:(g + 1) * gw]
            rinv = lax.rsqrt(jnp.mean(t * t, axis=-1, keepdims=True) + RMS_EPS)
            yh = t * rinv
            qv = acc[:, g * gw:(g + 1) * gw] * ng[:, g * gw:(g + 1) * gw]
            dyzs.append(rinv * (qv - yh * jnp.mean(qv * yh, axis=-1, keepdims=True)))
            yhats.append(yh)
        dyz = jnp.concatenate(dyzs, axis=1)
        yhat = jnp.concatenate(yhats, axis=1)
        dy = dyz * siluz
        dz = dyz * y * _dsilu(z, sz)
        return [dz], [acc * yhat, dy * xs], [dy]

    tmr = cf['tmr']
    dproj, dng, ddsk, dyT = fused_mm(
        "d_b_out", [(dyb_b, W['b_out_T'], 0, False)],
        [(sv['ysum'], 'row', DI, 0), (sv['xs'], 'row', DI, 0), (sv['proj'], 'row', DI, zoff),
         (sm['dskip_full'], 'vec', DI, 0), (sm['ssm_norm_g'], 'vec', DI, 0)],
        dgate_norm_epi, [(NM, BF16, DI, zoff)], [(DI, DI, 0), (DI, DI, 0)],
        M=S, tm=tmr, tn=DI, passthrough=(dproj, 0), t_outs=[(DI, F32, DI, 0)])
    out['ssm_norm_g'], out['dskip_full'] = dng, ddsk

    dxbc_f, ddt_f, dA_f = ssd_bwd("ssd_bwd_f", sv['xsT'], sv['bc'], sv['dtraw'], dyT, sv['st_f'], sm['dtb_f'],
                                  sm['alog_f'], S=S, DI=DI, G=G, H=H, rev=False)
    dcb, ddt_r, dA_r, dcbb = ssd_bwd("ssd_bwd_r", sv['xsT'], sv['bc'], sv['dtraw'], dyT, sv['st_r'], sm['dtb_r'],
                                     sm['alog_r'], S=S, DI=DI, G=G, H=H, rev=True,
                                     tail=(dxbc_f, sv['cbv_x'], sv['cbv_bc'], sm['dskipT']))
    out['dA_f'], out['dA_r'] = dA_f, dA_r
    out['ssm_conv_b'] = dcbb

    xoff = (2 * CD + 2 * D + DI) // DI
    dproj, dwb = conv_call(
        "d_conv_b", dcb, 0, sm['ssm_conv_w'], cf['KB'], lambda conv, ex: ([conv], []), [],
        [(NM, BF16, DI, xoff, 1)], M=S, tm=cf['tmc'], cw=DI, nc=XBC // DI, reverse=True, xin=(sv['proj'], xoff),
        passthrough=(dproj, 0))
    out['ssm_conv_w'] = dwb

    ddtb, ddt_bias = row_call("d_dt", lambda a, b: ([a + b], [a + b]),
                              [(ddt_f, 'row', LANES, 0, 0), (ddt_r, 'row', LANES, 0, 0)],
                              [(LANES, BF16, LANES, 0, 0)], [(LANES, LANES, 0, 0)], M=S, tm=tm)
    out['dt_bias'] = ddt_bias

    dx, = fused_mm("d_x", [(dproj, W['in_main_T'], 0, True), (ddtb, W['in_dt_T'], 0, False)],
                   [(dr1, 'row', D, 0)], lambda acc, ex: ([alpha * ex[0] + acc], []),
                   [(D, F32, D, 0)], M=S, tm=cf['tmx'], tn=D, nk=cf['nk_in'])

    tmw = cf['tmw']
    xb = sv['xb']
    out['w_in'] = jnp.concatenate(
        [mm_tn("dw_in", xb, dproj, tm=tmw, tk=D, tn=cf['tn_in']),
         mm_tn("dw_dt", xb, ddtb, tm=tmw, tk=D, tn=LANES)[:, :2 * H]], axis=1)
    out['w_a_out'] = mm_tn("dw_a_out", sv['sa'], dya_b, tm=tmw, tk=CD, tn=D)
    out['w_b_out'] = mm_tn("dw_b_out", sv['yn'], dyb_b, tm=tmw, tk=DI // 2, tn=D)
    out['w_o'] = mm_tn("dw_o", sv['merged'], dr1b, tm=tmw, tk=D, tn=D)
    out['w_gate_up'] = jnp.concatenate(
        [mm_tn("dw_gate", sv['hb'], dg_b, tm=tmw, tk=D, tn=tnf),
         mm_tn("dw_up", sv['hb'], du_b, tm=tmw, tk=D, tn=tnf)], axis=1)
    out['w_down'] = mm_tn("dw_down", sv['f'], dr2b, tm=tmw, tk=tnf, tn=D)
    out['w_ple'] = mm_tn("dw_ple", sv['pb'], dpe, tm=tmw, tk=sv['pb'].shape[1], tn=D)
    out['w_ple_gate'] = mm_tn("dw_ple_gate", sv['h2b'], dtg, tm=tmw, tk=D, tn=D)
    return dx, out


_WEIGHTS = ['w_in', 'conv_a_w', 'conv_a_b', 'ln_a_g', 'ln_a_b', 'w_a_out', 'ssm_conv_w', 'ssm_conv_b', 'a_log',
            'dt_bias', 'd_skip', 'ssm_norm_g', 'w_b_out', 'w_o', 'ln1_g', 'ln1_b', 'w_gate_up', 'w_down', 'ln2_g',
            'ln2_b', 'w_ple', 'ple_norm_g', 'w_ple_gate']
_COL_SHARDED = ['w_in', 'conv_a_w', 'ssm_conv_w', 'w_gate_up', 'w_ple']
_ROW_SHARDED = ['w_a_out', 'w_b_out', 'w_o', 'w_down', 'w_ple_gate']
_BIG = _COL_SHARDED + _ROW_SHARDED
_SMALL = [n for n in _WEIGHTS if n not in _BIG]
_CONV = ['conv_a_w', 'ssm_conv_w']


def _ceil_to(n, k):
    return -(-n // k) * k


def kernel(x, p, w_in, conv_a_w, conv_a_b, ln_a_g, ln_a_b, w_a_out, ssm_conv_w, ssm_conv_b, a_log, dt_bias, d_skip, ssm_norm_g, w_b_out, w_o, ln1_g, ln1_b, w_gate_up, w_down, ln2_g, ln2_b, w_ple, ple_norm_g, w_ple_gate, loss_target, m_w_in, m_conv_a_w, m_conv_a_b, m_ln_a_g, m_ln_a_b, m_w_a_out, m_ssm_conv_w, m_ssm_conv_b, m_a_log, m_dt_bias, m_d_skip, m_ssm_norm_g, m_w_b_out, m_w_o, m_ln1_g, m_ln1_b, m_w_gate_up, m_w_down, m_ln2_g, m_ln2_b, m_w_ple, m_ple_norm_g, m_w_ple_gate, v_w_in, v_conv_a_w, v_conv_a_b, v_ln_a_g, v_ln_a_b, v_w_a_out, v_ssm_conv_w, v_ssm_conv_b, v_a_log, v_dt_bias, v_d_skip, v_ssm_norm_g, v_w_b_out, v_w_o, v_ln1_g, v_ln1_b, v_w_gate_up, v_w_down, v_ln2_g, v_ln2_b, v_w_ple, v_ple_norm_g, v_w_ple_gate):
    wt = dict(w_in=w_in, conv_a_w=conv_a_w, conv_a_b=conv_a_b, ln_a_g=ln_a_g, ln_a_b=ln_a_b, w_a_out=w_a_out,
              ssm_conv_w=ssm_conv_w, ssm_conv_b=ssm_conv_b, a_log=a_log, dt_bias=dt_bias, d_skip=d_skip,
              ssm_norm_g=ssm_norm_g, w_b_out=w_b_out, w_o=w_o, ln1_g=ln1_g, ln1_b=ln1_b, w_gate_up=w_gate_up,
              w_down=w_down, ln2_g=ln2_g, ln2_b=ln2_b, w_ple=w_ple, ple_norm_g=ple_norm_g, w_ple_gate=w_ple_gate)
    mo = dict(w_in=m_w_in, conv_a_w=m_conv_a_w, conv_a_b=m_conv_a_b, ln_a_g=m_ln_a_g, ln_a_b=m_ln_a_b,
              w_a_out=m_w_a_out, ssm_conv_w=m_ssm_conv_w, ssm_conv_b=m_ssm_conv_b, a_log=m_a_log,
              dt_bias=m_dt_bias, d_skip=m_d_skip, ssm_norm_g=m_ssm_norm_g, w_b_out=m_w_b_out, w_o=m_w_o,
              ln1_g=m_ln1_g, ln1_b=m_ln1_b, w_gate_up=m_w_gate_up, w_down=m_w_down, ln2_g=m_ln2_g, ln2_b=m_ln2_b,
              w_ple=m_w_ple, ple_norm_g=m_ple_norm_g, w_ple_gate=m_w_ple_gate)
    vo = dict(w_in=v_w_in, conv_a_w=v_conv_a_w, conv_a_b=v_conv_a_b, ln_a_g=v_ln_a_g, ln_a_b=v_ln_a_b,
              w_a_out=v_w_a_out, ssm_conv_w=v_ssm_conv_w, ssm_conv_b=v_ssm_conv_b, a_log=v_a_log,
              dt_bias=v_dt_bias, d_skip=v_d_skip, ssm_norm_g=v_ssm_norm_g, w_b_out=v_w_b_out, w_o=v_w_o,
              ln1_g=v_ln1_g, ln1_b=v_ln1_b, w_gate_up=v_w_gate_up, w_down=v_w_down, ln2_g=v_ln2_g, ln2_b=v_ln2_b,
              w_ple=v_w_ple, ple_norm_g=v_ple_norm_g, w_ple_gate=v_w_ple_gate)

    L = w_in.shape[0]
    S, D = x.shape[1], x.shape[2]
    CD = conv_a_b.shape[1]
    DI = ssm_norm_g.shape[1]
    XBC = ssm_conv_b.shape[1]
    H = d_skip.shape[1]
    G = (XBC - DI) // (2 * D_STATE)
    F = w_down.shape[1] * 4
    N_IN = w_in.shape[2] * 4
    NM = N_IN - 2 * H
    KA, KB = conv_a_w.shape[1], ssm_conv_w.shape[1]
    assert DI == H * HEAD_DIM and CD == D and DI == 2 * D and XBC == 2 * DI and NM == 2 * CD + 2 * D + DI + XBC
    assert 2 * H <= LANES and S % CHUNK == 0
    tnf = F // 2
    cf = dict(S=S, D=D, CD=CD, DI=DI, XBC=XBC, F=F, H=H, G=G, NM=NM, KA=KA, KB=KB, GW=(H // G) * HEAD_DIM,
              alpha=float((2 * L) ** 0.25), tm=min(512, S), tmx=min(1024, S), tmc=min(256, S), tmr=min(256, S), tmw=min(1024, S),
              tn_in=D, tnf=tnf, nk_f=2, nk_in=NM // DI)

    core = lax.axis_index("c").astype(jnp.int32).reshape(1)
    split_names = [n for n in _BIG if n not in _CONV]

    def layer_weights(l):
        got = gather_layer("gather_weights", [wt[n][l].astype(BF16) for n in split_names], [wt[n][l] for n in _CONV])
        full = {}
        for n, g in zip(split_names + _CONV, got):
            if n in _COL_SHARDED:
                full[n] = g.transpose(1, 0, 2).reshape(g.shape[1], 4 * g.shape[2])
            else:
                full[n] = g.reshape(4 * g.shape[1], g.shape[2])
        win = full['w_in']
        in_main = win[:, :NM]
        in_dt = _pad_lanes(win[:, NM:])
        gu = full['w_gate_up']
        W = dict(in_main=in_main, in_dt=in_dt, in_main_T=in_main.T, in_dt_T=in_dt.T,
                 a_out=full['w_a_out'], a_out_T=full['w_a_out'].T,
                 b_out=full['w_b_out'], b_out_T=full['w_b_out'].T,
                 o=full['w_o'], o_T=full['w_o'].T, gate_up=gu, gate_T=gu[:, :F].T, up_T=gu[:, F:].T,
                 down=full['w_down'], down_T=full['w_down'].T, ple=full['w_ple'],
                 ple_gate=full['w_ple_gate'], ple_gate_T=full['w_ple_gate'].T)
        row = lambda v: v.reshape(1, -1)
        head_table = lambda v: jnp.broadcast_to(jnp.pad(v, (0, LANES - H))[:, None], (LANES, LANES))
        sm = dict(conv_a_w=jnp.pad(full['conv_a_w'], ((0, _ceil_to(KA, SUBLANES) - KA), (0, 0))),
                  ssm_conv_w=jnp.pad(full['ssm_conv_w'], ((0, _ceil_to(KB, SUBLANES) - KB), (0, 0))),
                  conv_a_b=row(conv_a_b[l]), ln_a_g=row(ln_a_g[l]), ln_a_b=row(ln_a_b[l]),
                  ssm_conv_b=row(ssm_conv_b[l]), ssm_norm_g=row(ssm_norm_g[l]),
                  ln1_g=row(ln1_g[l]), ln1_b=row(ln1_b[l]), ln2_g=row(ln2_g[l]), ln2_b=row(ln2_b[l]),
                  ple_norm_g=row(ple_norm_g[l]),
                  dtb_f=head_table(dt_bias[l, 0]), dtb_r=head_table(dt_bias[l, 1]),
                  alog_f=head_table(a_log[l, 0]), alog_r=head_table(a_log[l, 1]),
                  dskip_full=row(jnp.repeat(d_skip[l], HEAD_DIM)),
                  dskipT=jnp.broadcast_to(jnp.repeat(d_skip[l], HEAD_DIM)[:, None], (DI, LANES)))
        return W, sm

    def blocks(n, gl):
        g = gl[n]
        if n == 'conv_a_w':
            g = g.sum(axis=1)[:KA]
        elif n == 'ssm_conv_w':
            g = g.sum(axis=1)[:KB]
        if n in _COL_SHARDED:
            return g.reshape(g.shape[0], 4, g.shape[1] // 4).transpose(1, 0, 2)
        return g.reshape(4, g.shape[0] // 4, g.shape[1])

    def reduce_layer(l, gl, acc):
        mine = [blocks(n, gl) for n in split_names]
        theirs = core_send_half("core_send_half", mine)
        both = [core_sum("core_sum_" + n, core, b, t) for n, b, t in zip(split_names, mine, theirs)]
        parts = chip_exchange("scatter_grads", [[t] for t in both], gather=False)
        sums = [chip_sum_into("chip_sum_" + n, core, pr.reshape(4, pr.shape[2], pr.shape[3]), l, L, into=acc.get(n))
                for n, pr in zip(split_names, parts)]
        return dict(zip(split_names, core_fill("core_fill", sums, l, L)))

    lw = [layer_weights(l) for l in range(L)]
    xl = x[0]
    xlb = xl.astype(BF16)
    saved = []
    for l in range(L):
        xl, xlb, sv = _layer_fwd(cf, xl, xlb, p[l, 0].astype(BF16), lw[l][0], lw[l][1])
        saved.append(sv)
    grads = [None] * L
    dxl = None
    gsum = {}
    for l in reversed(range(L)):
        if l == L - 1:
            dxl, grads[l] = _layer_bwd(cf, saved[l], lw[l][0], lw[l][1], target=loss_target[0], xn=xl)
        else:
            dxl, grads[l] = _layer_bwd(cf, saved[l], lw[l][0], lw[l][1], dxn=dxl)
        gsum = reduce_layer(l, grads[l], gsum)
    loss = lax.psum(0.5 / D * jnp.sum(grads[L - 1]['loss_sq']), ("x", "y", "c"))
    grad_x = dxl[None]

    res = {}
    for n in split_names:
        shp = wt[n].shape
        flat = lambda a: a.reshape(shp[0] * shp[1], shp[2])
        outs = adamw_full("adamw_" + n, gsum[n], flat(wt[n]), flat(mo[n]), flat(vo[n]))
        res[n] = [o.reshape(shp) for o in [gsum[n]] + list(outs)]
    parts = chip_exchange("scatter_conv", [[blocks(n, grads[l]) for l in range(L)] for n in _CONV], gather=False)
    chip_sums = [sum_chips("chip_sum_" + n, pr.reshape(4, L * pr.shape[2], pr.shape[3])) for n, pr in zip(_CONV, parts)]
    sib_sums = sibling_swap("core_swap", chip_sums)
    for n, mine, sib in zip(_CONV, chip_sums, sib_sums):
        shp = wt[n].shape
        flat = lambda a: a.reshape(shp[0] * shp[1], shp[2])
        outs = adamw_shard("adamw_" + n, mine, sib, flat(wt[n]), flat(mo[n]), flat(vo[n]))
        res[n] = [o.reshape(shp) for o in outs]

    def small_pieces(l):
        gl = grads[l]
        A = -jnp.exp(a_log[l])
        d = dict(gl)
        d_alog = jnp.concatenate([gl['dA_f'].sum(axis=1)[:H] * A[0], gl['dA_r'].sum(axis=1)[:H] * A[1]])
        d['a_log'] = jnp.pad(d_alog[None], ((0, SUBLANES - 1), (0, 0)))
        d['dt_bias'] = gl['dt_bias'][:, :2 * H]
        d['d_skip'] = gl['dskip_full'].reshape(SUBLANES, H, HEAD_DIM).sum(axis=-1)
        return [_pad_lanes(d[n], _ceil_to(d[n].shape[1], LANES)) for n in _SMALL]

    widths = [_ceil_to(math.prod(wt[n].shape[1:]), LANES) for n in _SMALL]
    packed = jnp.concatenate([pc for l in range(L) for pc in small_pieces(l)], axis=1)
    gathered = all8_gather("gather_small", packed)

    def pack_params(src):
        return jnp.concatenate([_pad_lanes(src[n][l].reshape(1, -1), wd) for l in range(L) for n, wd in zip(_SMALL, widths)],
                               axis=1)

    small_out = adamw_small("adamw_small", gathered, pack_params(wt), pack_params(mo), pack_params(vo))
    off = 0
    per = {n: [[] for _ in range(4)] for n in _SMALL}
    for l in range(L):
        for n, wd in zip(_SMALL, widths):
            size = math.prod(wt[n].shape[1:])
            for k in range(4):
                per[n][k].append(small_out[k][0, off:off + size].reshape(wt[n].shape[1:]))
            off += wd
    for n in _SMALL:
        res[n] = [jnp.stack(per[n][k]) for k in range(4)]

    return (loss, grad_x, *[res[n][0] for n in _WEIGHTS], *[res[n][1] for n in _WEIGHTS],
            *[res[n][2] for n in _WEIGHTS], *[res[n][3] for n in _WEIGHTS])
```

```python
import math

import jax
import jax.numpy as jnp
from jax import lax
from jax.experimental import pallas as pl
from jax.experimental.pallas import tpu as pltpu

F32 = jnp.float32
BF16 = jnp.bfloat16

VMEM_LIMIT_BYTES = 56 * 1024 * 1024
LANES = 128
SUBLANES = 8

CHUNK = 128
D_STATE = 128
HEAD_DIM = 64
LN_EPS = 1e-5
RMS_EPS = 1e-6
ADAM_LR = 0.001
ADAM_B1 = 0.9
ADAM_B2 = 0.999
ADAM_EPS = 1e-08
ADAM_WD = 0.01
ADAM_STEP = 10
HALO = 16
MESH = pl.DeviceIdType.MESH


def _params(**kw):
    return pltpu.CompilerParams(vmem_limit_bytes=VMEM_LIMIT_BYTES, **kw)


def _sig(x):
    return jax.nn.sigmoid(x)


def _dsilu(x, s):
    return s * (1.0 + x * (1.0 - s))


def _ln_stats(r):
    mu = jnp.mean(r, axis=-1, keepdims=True)
    xc = r - mu
    var = jnp.mean(xc * xc, axis=-1, keepdims=True)
    rstd = lax.rsqrt(var + LN_EPS)
    return xc * rstd, rstd


def _ln_bwd(dy, xhat, rstd, g):
    dxh = dy * g
    m1 = jnp.mean(dxh, axis=-1, keepdims=True)
    m2 = jnp.mean(dxh * xhat, axis=-1, keepdims=True)
    return rstd * (dxh - m1 - xhat * m2)


def _f32(v):
    return v if v.dtype == F32 else v.astype(F32)


def _rows8(v):
    tm, w = v.shape
    return v.reshape(tm // SUBLANES, SUBLANES, w).sum(axis=0)


def fused_mm(name, prods, extras, epi, row_outs, col_outs=(), *, M, tm, tn, nj=1, nk=1,
             passthrough=None, t_outs=()):
    np_ = len(prods)
    ne = len(extras)
    nro = len(row_outs)
    nco = len(col_outs)
    use_acc = nk > 1

    def body(*refs):
        a_refs = [refs[2 * p] for p in range(np_)]
        w_refs = [refs[2 * p + 1] for p in range(np_)]
        pos = 2 * np_
        e_refs = refs[pos:pos + ne]
        pos += ne
        if passthrough is not None:
            pos += 1
        ro_refs = refs[pos:pos + nro]
        pos += nro
        co_refs = refs[pos:pos + nco]
        pos += nco
        to_refs = refs[pos:pos + len(t_outs)]
        pos += len(t_outs)
        acc_ref = refs[pos] if use_acc else None
        i = pl.program_id(1)
        k = pl.program_id(2)

        def prod(p):
            a = a_refs[p][...]
            if a.dtype != BF16:
                a = a.astype(BF16)
            return jnp.dot(a, w_refs[p][...], preferred_element_type=F32)

        def finish(acc):
            res = epi(acc, [_f32(r[...]) for r in e_refs])
            rows, cols = res[0], res[1]
            for v, o in zip(rows, ro_refs):
                o[...] = v.astype(o.dtype)
            for v, o in zip(res[2] if len(res) > 2 else (), to_refs):
                o[...] = v.T.astype(o.dtype)
            for v, o in zip(cols, co_refs):
                v8 = _rows8(v)

                @pl.when(i == 0)
                def _():
                    o[...] = v8

                @pl.when(i > 0)
                def _():
                    o[...] += v8

        if not use_acc:
            acc = prod(0)
            for p in range(1, np_):
                acc = acc + prod(p)
            finish(acc)
        else:
            @pl.when(k == 0)
            def _():
                acc = None
                for p in range(np_):
                    acc = prod(p) if acc is None else acc + prod(p)
                acc_ref[...] = acc

            @pl.when(k > 0)
            def _():
                acc = None
                for p in range(np_):
                    if prods[p][3]:
                        acc = prod(p) if acc is None else acc + prod(p)
                acc_ref[...] += acc

            @pl.when(k == nk - 1)
            def _():
                finish(acc_ref[...])

    in_specs = []
    args = []
    for a, w, joff, ksplit in prods:
        K = a.shape[1]
        if ksplit:
            tk = K // nk
            in_specs.append(pl.BlockSpec((tm, tk), lambda j, i, k: (i, k)))
            in_specs.append(pl.BlockSpec((tk, tn), lambda j, i, k, joff=joff: (k, j + joff)))
        else:
            in_specs.append(pl.BlockSpec((tm, K), lambda j, i, k: (i, 0)))
            in_specs.append(pl.BlockSpec((K, tn), lambda j, i, k, joff=joff: (0, j + joff)))
        args += [a, w]
    for arr, kind, width, c0 in extras:
        if kind == 'row':
            in_specs.append(pl.BlockSpec((tm, width), lambda j, i, k, c0=c0: (i, c0 + j)))
        else:
            in_specs.append(pl.BlockSpec((arr.shape[0], width), lambda j, i, k, c0=c0: (0, c0 + j)))
        args.append(arr)
    aliases = {}
    if passthrough is not None:
        arr, oidx = passthrough
        in_specs.append(pl.BlockSpec(memory_space=pl.ANY))
        aliases = {len(args): oidx}
        args.append(arr)
    out_shape = []
    out_specs = []
    for n_total, dtype, width, c0 in row_outs:
        out_shape.append(jax.ShapeDtypeStruct((M, n_total), dtype))
        out_specs.append(pl.BlockSpec((tm, width), lambda j, i, k, c0=c0: (i, c0 + j)))
    for n_total, width, c0 in col_outs:
        out_shape.append(jax.ShapeDtypeStruct((SUBLANES, n_total), F32))
        out_specs.append(pl.BlockSpec((SUBLANES, width), lambda j, i, k, c0=c0: (0, c0 + j)))
    for n_total, dtype, width, c0 in t_outs:
        out_shape.append(jax.ShapeDtypeStruct((n_total, M), dtype))
        out_specs.append(pl.BlockSpec((width, tm), lambda j, i, k, c0=c0: (c0 + j, i)))
    scratch = [pltpu.VMEM((tm, tn), F32)] if use_acc else []
    return pl.pallas_call(
        body, name=name, grid=(nj, M // tm, nk), in_specs=in_specs, out_specs=out_specs,
        out_shape=out_shape, scratch_shapes=scratch, input_output_aliases=aliases,
        compiler_params=_params(dimension_semantics=("arbitrary", "arbitrary", "arbitrary")),
    )(*args)


def mm_tn(name, a, b, *, tm, tk, tn):
    M, K = a.shape
    N = b.shape[1]

    def body(a_ref, b_ref, o_ref):
        m = pl.program_id(2)
        p = lax.dot_general(a_ref[...], b_ref[...], (((0,), (0,)), ((), ())),
                            preferred_element_type=F32)

        @pl.when(m == 0)
        def _():
            o_ref[...] = p

        @pl.when(m > 0)
        def _():
            o_ref[...] += p

    return pl.pallas_call(
        body, name=name, grid=(K // tk, N // tn, M // tm),
        in_specs=[pl.BlockSpec((tm, tk), lambda kk, j, m: (m, kk)),
                  pl.BlockSpec((tm, tn), lambda kk, j, m: (m, j))],
        out_specs=pl.BlockSpec((tk, tn), lambda kk, j, m: (kk, j)),
        out_shape=jax.ShapeDtypeStruct((K, N), F32),
        compiler_params=_params(dimension_semantics=("arbitrary", "arbitrary", "arbitrary")),
    )(a, b)


def row_call(name, fn, ins, row_outs, col_outs=(), *, M, tm, nc=1):
    ni = len(ins)
    nro = len(row_outs)

    def body(*refs):
        i = pl.program_id(1)
        vals = [_f32(r[...]) for r in refs[:ni]]
        rows, cols = fn(*vals)
        for v, o in zip(rows, refs[ni:ni + nro]):
            o[...] = v.astype(o.dtype)
        for v, o in zip(cols, refs[ni + nro:]):
            v8 = _rows8(v)

            @pl.when(i == 0)
            def _():
                o[...] = v8

            @pl.when(i > 0)
            def _():
                o[...] += v8

    in_specs = []
    for arr, kind, width, c0, cmul in ins:
        if kind == 'row':
            in_specs.append(pl.BlockSpec((tm, width), lambda cj, i, c0=c0, cmul=cmul: (i, c0 + cmul * cj)))
        else:
            in_specs.append(pl.BlockSpec((arr.shape[0], width), lambda cj, i, c0=c0, cmul=cmul: (0, c0 + cmul * cj)))
    out_shape = []
    out_specs = []
    for n_total, dtype, width, c0, cmul in row_outs:
        out_shape.append(jax.ShapeDtypeStruct((M, n_total), dtype))
        out_specs.append(pl.BlockSpec((tm, width), lambda cj, i, c0=c0, cmul=cmul: (i, c0 + cmul * cj)))
    for n_total, width, c0, cmul in col_outs:
        out_shape.append(jax.ShapeDtypeStruct((SUBLANES, n_total), F32))
        out_specs.append(pl.BlockSpec((SUBLANES, width), lambda cj, i, c0=c0, cmul=cmul: (0, c0 + cmul * cj)))
    return pl.pallas_call(
        body, name=name, grid=(nc, M // tm), in_specs=in_specs, out_specs=out_specs,
        out_shape=out_shape,
        compiler_params=_params(dimension_semantics=("arbitrary", "arbitrary")),
    )(*[a[0] for a in ins])


def conv_call(name, src, src_c0, w, K, epi, extras, row_outs, col_outs=(), *, M, tm, cw, nc,
              reverse, xin=None, passthrough=None, t_outs=(), w_c0=0):
    pad = (K - 1) // 2
    assert pad <= HALO - 1
    R = tm // HALO
    nblk = M // HALO
    n_i = M // tm
    Kp = w.shape[0]
    ne = len(extras)
    nro = len(row_outs)
    nco = len(col_outs)
    rb = 64
    cbw = min(cw, 256)
    n_copies = SUBLANES if K > SUBLANES else 1

    def body(*refs):
        main_ref, prev_ref, next_ref, w_ref = refs[:4]
        pos = 4
        xin_ref = None
        if xin is not None:
            xin_ref = refs[pos]
            pos += 1
        e_refs = refs[pos:pos + ne]
        pos += ne
        if passthrough is not None:
            pos += 1
        ro_refs = refs[pos:pos + nro]
        pos += nro
        co_refs = refs[pos:pos + nco]
        pos += nco
        to_refs = refs[pos:pos + len(t_outs)]
        pos += len(t_outs)
        dw_ref = None
        if xin is not None:
            dw_ref = refs[pos]
            pos += 1
        ext_ref, conv_ref = refs[pos], refs[pos + 1]
        i = pl.program_id(1)

        ext_ref[0, 0:HALO, :] = jnp.where(i == 0, 0.0, prev_ref[...].astype(F32))
        ext_ref[0, HALO:HALO + tm, :] = main_ref[...].astype(F32)
        ext_ref[0, HALO + tm:, :] = jnp.where(i == n_i - 1, 0.0, next_ref[...].astype(F32))
        if dw_ref is not None:
            @pl.when(i == 0)
            def _():
                dw_ref[...] = jnp.zeros_like(dw_ref)

        n_sh = tm + 2 * HALO - SUBLANES
        for c0 in range(0, cw, cbw):
            for sft in range(1, n_copies):
                ext_ref[sft, 0:n_sh, c0:c0 + cbw] = ext_ref[0, sft:sft + n_sh, c0:c0 + cbw]

        for c0 in range(0, cw, cbw):
            for r0 in range(0, tm, rb):
                acc = jnp.zeros((rb, cbw), F32)
                if xin_ref is not None:
                    xblk = xin_ref[r0:r0 + rb, c0:c0 + cbw].astype(F32)
                for k in range(K):
                    off = HALO + r0 + ((pad - k) if reverse else (k - pad))
                    sft = off % SUBLANES if n_copies > 1 else 0
                    d = ext_ref[sft, off - sft:off - sft + rb, c0:c0 + cbw]
                    acc = acc + d * w_ref[k:k + 1, c0:c0 + cbw]
                    if xin_ref is not None:
                        dw_ref[k, :, c0:c0 + cbw] += _rows8(xblk * d)
                conv_ref[r0:r0 + rb, c0:c0 + cbw] = acc

        res = epi(conv_ref[...], [_f32(r[...]) for r in e_refs])
        rows, cols = res[0], res[1]
        for v, o in zip(rows, ro_refs):
            o[...] = v.astype(o.dtype)
        for v, o in zip(res[2] if len(res) > 2 else (), to_refs):
            o[...] = v.T.astype(o.dtype)
        for v, o in zip(cols, co_refs):
            v8 = _rows8(v)

            @pl.when(i == 0)
            def _():
                o[...] = v8

            @pl.when(i > 0)
            def _():
                o[...] += v8

    in_specs = [
        pl.BlockSpec((tm, cw), lambda cj, i: (i, src_c0 + cj)),
        pl.BlockSpec((HALO, cw), lambda cj, i: (jnp.maximum(i * R - 1, 0), src_c0 + cj)),
        pl.BlockSpec((HALO, cw), lambda cj, i: (jnp.minimum((i + 1) * R, nblk - 1), src_c0 + cj)),
        pl.BlockSpec((Kp, cw), lambda cj, i: (0, w_c0 + cj)),
    ]
    args = [src, src, src, w]
    if xin is not None:
        in_specs.append(pl.BlockSpec((tm, cw), lambda cj, i, c0=xin[1]: (i, c0 + cj)))
        args.append(xin[0])
    for arr, kind, width, c0, cmul in extras:
        if kind == 'row':
            in_specs.append(pl.BlockSpec((tm, width), lambda cj, i, c0=c0, cmul=cmul: (i, c0 + cmul * cj)))
        else:
            in_specs.append(pl.BlockSpec((arr.shape[0], width), lambda cj, i, c0=c0, cmul=cmul: (0, c0 + cmul * cj)))
        args.append(arr)
    aliases = {}
    if passthrough is not None:
        in_specs.append(pl.BlockSpec(memory_space=pl.ANY))
        aliases = {len(args): passthrough[1]}
        args.append(passthrough[0])
    out_shape = []
    out_specs = []
    for n_total, dtype, width, c0, cmul in row_outs:
        out_shape.append(jax.ShapeDtypeStruct((M, n_total), dtype))
        out_specs.append(pl.BlockSpec((tm, width), lambda cj, i, c0=c0, cmul=cmul: (i, c0 + cmul * cj)))
    for n_total, width, c0, cmul in col_outs:
        out_shape.append(jax.ShapeDtypeStruct((SUBLANES, n_total), F32))
        out_specs.append(pl.BlockSpec((SUBLANES, width), lambda cj, i, c0=c0, cmul=cmul: (0, c0 + cmul * cj)))
    for n_total, dtype, width, c0, cmul in t_outs:
        out_shape.append(jax.ShapeDtypeStruct((n_total, M), dtype))
        out_specs.append(pl.BlockSpec((width, tm), lambda cj, i, c0=c0, cmul=cmul: (c0 + cmul * cj, i)))
    if xin is not None:
        out_shape.append(jax.ShapeDtypeStruct((Kp, SUBLANES, cw * nc), F32))
        out_specs.append(pl.BlockSpec((Kp, SUBLANES, cw), lambda cj, i: (0, 0, cj)))
    return pl.pallas_call(
        body, name=name, grid=(nc, n_i), in_specs=in_specs, out_specs=out_specs,
        out_shape=out_shape, input_output_aliases=aliases,
        scratch_shapes=[pltpu.VMEM((n_copies, tm + 2 * HALO, cw), F32), pltpu.VMEM((tm, cw), F32)],
        compiler_params=_params(dimension_semantics=("arbitrary", "arbitrary")),
    )(*args)


def _split_dot(m_bf16, v, n_pass, dims=None):
    out = None
    rest = v
    for p in range(n_pass):
        piece = rest.astype(BF16)
        if p + 1 < n_pass:
            rest = rest - piece.astype(F32)
        if dims is None:
            t = jnp.dot(m_bf16, piece, preferred_element_type=F32)
        else:
            t = lax.dot_general(m_bf16, piece, dims, preferred_element_type=F32)
        out = t if out is None else out + t
    return out


def _split_dot_r(v, m_bf16, n_pass):
    out = None
    rest = v
    for p in range(n_pass):
        piece = rest.astype(BF16)
        if p + 1 < n_pass:
            rest = rest - piece.astype(F32)
        t = jnp.dot(piece, m_bf16, preferred_element_type=F32)
        out = t if out is None else out + t
    return out


def _softplus(x):
    return jnp.maximum(x, 0.0) + jnp.log1p(jnp.exp(-jnp.abs(x)))


NT_DIMS = (((1,), (1,)), ((), ()))
TN_DIMS = (((0,), (0,)), ((), ()))


def _ssd_common(dtraw, dtbT, alogT, rev, n_heads):
    L = CHUNK
    if rev:
        dtraw = pltpu.roll(dtraw, LANES - n_heads, 1)
    preT = dtraw.T + dtbT
    dtT = _softplus(preT)
    AT = -jnp.exp(alogT)
    aT = dtT * AT
    ri = lax.broadcasted_iota(jnp.int32, (L, L), 0)
    ci = lax.broadcasted_iota(jnp.int32, (L, L), 1)
    up = (ri >= ci) if rev else (ri <= ci)
    lo = (ri <= ci) if rev else (ri >= ci)
    csT = _split_dot_r(aT, up.astype(BF16), 3)
    last = 0 if rev else L - 1
    lastB = jnp.broadcast_to(csT[:, last:last + 1], (L, L))
    return dict(preT=preT, dtT=dtT, AT=AT, csT=csT, cs=csT.T, up=up, lo=lo, ci=ci, last=last,
                doutT=jnp.exp(csT), dstT=jnp.exp(lastB - csT), totB=jnp.exp(lastB))


def ssd_fwd(name, xsT, bc, dtraw, dtbT, alogT, *, S, DI, G, H, rev, tail=None):
    NC = S // CHUNK
    R = H // G
    GW = R * HEAD_DIM
    N = D_STATE
    P = HEAD_DIM

    def body(*refs):
        xsT_ref, bc_ref, dtraw_ref, dtb_ref, alog_ref = refs[:5]
        if tail is None:
            y_ref, st_ref, h_ref = refs[5:]
        else:
            yo_ref, z_ref, xs_ref, dsk_ref, ng_ref = refs[5:10]
            y_ref, st_ref, yn_ref, h_ref = refs[10:]
        c = pl.program_id(0)

        @pl.when(c == 0)
        def _():
            h_ref[...] = jnp.zeros_like(h_ref)

        q = _ssd_common(dtraw_ref[...], dtb_ref[...], alog_ref[...], rev, H)
        cs, csT, dtT, doutT, totB = q['cs'], q['csT'], q['dtT'], q['doutT'], q['totB']
        wstT = q['dstT'] * dtT
        for g in range(G):
            Bg = bc_ref[:, g * N:(g + 1) * N].astype(BF16)
            Cg = bc_ref[:, G * N + g * N:G * N + (g + 1) * N].astype(BF16)
            CBT = lax.dot_general(Bg, Cg, NT_DIMS, preferred_element_type=F32)
            HT = h_ref[g]
            yoffT = lax.dot_general(HT.astype(BF16), Cg, NT_DIMS, preferred_element_type=F32)
            xT = xsT_ref[g * GW:(g + 1) * GW, :]
            hs = [g * R + r for r in range(R)]
            blks = [slice(r * P, (r + 1) * P) for r in range(R)]
            segs = [jnp.where(q['up'], csT[h:h + 1, :] - cs[:, h:h + 1], -1e30) for h in hs]
            GTs = [(CBT * jnp.exp(sg)).astype(BF16) for sg in segs]
            xThs = [xT[b, :] for b in blks]
            XThs = [(xTh * dtT[h:h + 1, :]).astype(BF16) for xTh, h in zip(xThs, hs)]
            ydTs = [jnp.dot(a, GT, preferred_element_type=F32) for a, GT in zip(XThs, GTs)]
            ys = [ydT + yoffT[b, :] * doutT[h:h + 1, :] for ydT, b, h in zip(ydTs, blks, hs)]
            xws = [xTh * wstT[h:h + 1, :] for xTh, h in zip(xThs, hs)]
            tots = [jnp.broadcast_to(totB[h:h + 1, :], (P, N)) for h in hs]
            y_ref[:, g * GW:(g + 1) * GW] = jnp.concatenate(ys, axis=0).T
            xwT = jnp.concatenate(xws, axis=0).astype(BF16)
            ST = jnp.dot(xwT, Bg, preferred_element_type=F32)
            st_ref[0, g] = HT
            h_ref[g] = HT * jnp.concatenate(tots, axis=0) + ST
        if tail is not None:
            y = y_ref[...] + yo_ref[...]
            y_ref[...] = y
            z = _f32(z_ref[...])
            yz = (y + xs_ref[...] * dsk_ref[...]) * (z * _sig(z))
            for g in range(G):
                t = yz[:, g * GW:(g + 1) * GW]
                tn = t * lax.rsqrt(jnp.mean(t * t, axis=-1, keepdims=True) + RMS_EPS)
                yn_ref[:, g * GW:(g + 1) * GW] = (tn * ng_ref[:, g * GW:(g + 1) * GW]).astype(BF16)

    cidx = (lambda c: NC - 1 - c) if rev else (lambda c: c)
    cmap = lambda c: (cidx(c), 0)
    smap = lambda c: (cidx(c), 0, 0, 0)
    const = lambda c: (0, 0)
    tmap = lambda c: (0, cidx(c))
    in_specs = [pl.BlockSpec((DI, CHUNK), tmap), pl.BlockSpec((CHUNK, 2 * G * N), cmap), pl.BlockSpec((CHUNK, LANES), cmap),
                pl.BlockSpec((LANES, LANES), const), pl.BlockSpec((LANES, LANES), const)]
    out_specs = [pl.BlockSpec((CHUNK, DI), cmap), pl.BlockSpec((1, G, GW, N), smap)]
    out_shape = [jax.ShapeDtypeStruct((S, DI), F32), jax.ShapeDtypeStruct((NC, G, GW, N), F32)]
    args = [xsT, bc, dtraw, dtbT, alogT]
    if tail is not None:
        y_other, (z_arr, z_blk), xs_row, dsk, ng = tail
        in_specs += [pl.BlockSpec((CHUNK, DI), cmap), pl.BlockSpec((CHUNK, DI), lambda c: (cidx(c), z_blk)),
                     pl.BlockSpec((CHUNK, DI), cmap), pl.BlockSpec((1, DI), const), pl.BlockSpec((1, DI), const)]
        out_specs.append(pl.BlockSpec((CHUNK, DI), cmap))
        out_shape.append(jax.ShapeDtypeStruct((S, DI), BF16))
        args += [y_other, z_arr, xs_row, dsk, ng]
    return pl.pallas_call(
        body, name=name, grid=(NC,), in_specs=in_specs, out_specs=out_specs, out_shape=out_shape,
        scratch_shapes=[pltpu.VMEM((G, GW, N), F32)],
        compiler_params=_params(dimension_semantics=("arbitrary",)),
    )(*args)


def ssd_bwd(name, xsT, bc, dtraw, dyT, st, dtbT, alogT, *, S, DI, G, H, rev, tail=None):
    NC = S // CHUNK
    R = H // G
    GW = R * HEAD_DIM
    N = D_STATE
    XBC = DI + 2 * G * N
    P = HEAD_DIM
    L = CHUNK

    def body(*refs):
        xsT_ref, bc_ref, dtraw_ref, dyT_ref, st_ref, dtb_ref, alog_ref = refs[:7]
        if tail is None:
            dxbc_ref, ddt_ref, da_ref, dh_ref, dcst_ref, p2t_ref, p3t_ref, e2t_ref = refs[7:]
        else:
            other_ref, cbx_ref, cbbc_ref, dskT_ref = refs[7:11]
            dxbc_ref, ddt_ref, da_ref, dcol_ref, dh_ref, dcst_ref, p2t_ref, p3t_ref, e2t_ref = refs[11:]
        c = pl.program_id(0)

        @pl.when(c == 0)
        def _():
            dh_ref[...] = jnp.zeros_like(dh_ref)
            da_ref[...] = jnp.zeros_like(da_ref)
            dcst_ref[...] = jnp.zeros_like(dcst_ref)
            p2t_ref[...] = jnp.zeros_like(p2t_ref)
            p3t_ref[...] = jnp.zeros_like(p3t_ref)
            e2t_ref[...] = jnp.zeros_like(e2t_ref)

        q = _ssd_common(dtraw_ref[...], dtb_ref[...], alog_ref[...], rev, H)
        cs, csT, dtT, doutT, dstT, totB = q['cs'], q['csT'], q['dtT'], q['doutT'], q['dstT'], q['totB']
        wstT = dstT * dtT
        lane = q['ci']
        for g in range(G):
            Bg = bc_ref[:, g * N:(g + 1) * N].astype(BF16)
            Cg = bc_ref[:, G * N + g * N:G * N + (g + 1) * N].astype(BF16)
            CB = lax.dot_general(Cg, Bg, NT_DIMS, preferred_element_type=F32)
            HpT = st_ref[0, g]
            HpTb = HpT.astype(BF16)
            dHT = dh_ref[g]
            dHTb = dHT.astype(BF16)
            BdHT = lax.dot_general(dHTb, Bg, NT_DIMS, preferred_element_type=F32)
            yoffT = lax.dot_general(HpTb, Cg, NT_DIMS, preferred_element_type=F32)
            xT = xsT_ref[g * GW:(g + 1) * GW, :]
            dyT = dyT_ref[g * GW:(g + 1) * GW, :]
            hs = [g * R + r for r in range(R)]
            blks = [slice(r * P, (r + 1) * P) for r in range(R)]
            Lms = [jnp.exp(jnp.where(q['lo'], cs[:, h:h + 1] - csT[h:h + 1, :], -1e30)) for h in hs]
            xThs = [xT[b, :] for b in blks]
            dyThs = [dyT[b, :] for b in blks]
            xThbs = [v.astype(BF16) for v in xThs]
            dyThbs = [v.astype(BF16) for v in dyThs]
            dGxs = [lax.dot_general(a, b, TN_DIMS, preferred_element_type=F32) for a, b in zip(dyThbs, xThbs)]
            Gms = [(CB * Lm).astype(BF16) for Lm in Lms]
            XThbs = [(xTh * dtT[h:h + 1, :]).astype(BF16) for xTh, h in zip(xThs, hs)]
            u1Ts = [jnp.dot(a, Gm, preferred_element_type=F32) for a, Gm in zip(dyThbs, Gms)]
            ydTs = [lax.dot_general(a, Gm, NT_DIMS, preferred_element_type=F32) for a, Gm in zip(XThbs, Gms)]
            Ts = [dGx * (Lm * dtT[h:h + 1, :]) for dGx, Lm, h in zip(dGxs, Lms, hs)]
            dCB = Ts[0]
            for T in Ts[1:]:
                dCB = dCB + T
            uTs = [u1T + BdHT[b, :] * dstT[h:h + 1, :] for u1T, b, h in zip(u1Ts, blks, hs)]
            dyds = [dyTh * doutT[h:h + 1, :] for dyTh, h in zip(dyThs, hs)]
            xws = [xTh * wstT[h:h + 1, :] for xTh, h in zip(xThs, hs)]
            for r, h in enumerate(hs):
                b = blks[r]
                p3row = jnp.sum(xws[r] * BdHT[b, :], axis=0, keepdims=True)
                seg_row = jnp.sum(_f32(dyThbs[r]) * ydTs[r], axis=0, keepdims=True)
                seg_col = jnp.sum(_f32(XThbs[r]) * u1Ts[r], axis=0, keepdims=True)
                dcst_ref[h:h + 1, :] = (jnp.sum(dyds[r] * yoffT[b, :], axis=0, keepdims=True)
                                        + seg_row - seg_col - p3row)
                p2t_ref[h:h + 1, :] = jnp.sum(xThs[r] * uTs[r], axis=0, keepdims=True)
                p3t_ref[h:h + 1, :] = p3row
                e2t_ref[h:h + 1, :] = jnp.sum(HpT[b, :] * dHT[b, :], axis=0, keepdims=True)
            dxs = [uT * dtT[h:h + 1, :] for uT, h in zip(uTs, hs)]
            if tail is not None:
                dxs = [d + dyTh * dskT_ref[g * GW + r * P:g * GW + (r + 1) * P, :]
                       for r, (d, dyTh) in enumerate(zip(dxs, dyThs))]
            tots = [jnp.broadcast_to(totB[h:h + 1, :], (P, N)) for h in hs]
            dxbc_ref[:, g * GW:(g + 1) * GW] = jnp.concatenate(dxs, axis=0).T
            dydT = jnp.concatenate(dyds, axis=0).astype(BF16)
            xwT = jnp.concatenate(xws, axis=0).astype(BF16)
            dCBb = dCB.astype(BF16)
            dC = (jnp.dot(dCBb, Bg, preferred_element_type=F32)
                  + lax.dot_general(dydT, HpTb, TN_DIMS, preferred_element_type=F32))
            dB = (lax.dot_general(dCBb, Cg, TN_DIMS, preferred_element_type=F32)
                  + lax.dot_general(xwT, dHTb, TN_DIMS, preferred_element_type=F32))
            dxbc_ref[:, DI + g * N:DI + (g + 1) * N] = dB
            dxbc_ref[:, DI + G * N + g * N:DI + G * N + (g + 1) * N] = dC
            dh_ref[g] = (dHT * jnp.concatenate(tots, axis=0)
                         + jnp.dot(dydT, Cg, preferred_element_type=F32))
        e1 = jnp.sum(p3t_ref[...], axis=1, keepdims=True)
        e2 = jnp.sum(e2t_ref[...], axis=1, keepdims=True)
        dcsT = dcst_ref[...] + jnp.where(lane == q['last'], e1 + totB * e2, 0.0)
        daT = _split_dot_r(dcsT, q['lo'].astype(BF16), 3)
        ddtT = daT * q['AT'] + p2t_ref[...]
        da_ref[...] += daT * dtT
        ddraw = jnp.where(lane < H, (ddtT * _sig(q['preT'])).T, 0.0)
        if rev:
            ddraw = pltpu.roll(ddraw, H, 1)
        ddt_ref[...] = ddraw
        if tail is not None:
            for c0, cb_ref in ((0, cbx_ref), (DI, cbbc_ref)):
                d = dxbc_ref[:, c0:c0 + DI] + other_ref[:, c0:c0 + DI]
                cb = cb_ref[...]
                dcb = d * _dsilu(cb, _sig(cb))
                dxbc_ref[:, c0:c0 + DI] = dcb
                part = _rows8(dcb)

                @pl.when(c == 0)
                def _():
                    dcol_ref[:, c0:c0 + DI] = part

                @pl.when(c > 0)
                def _():
                    dcol_ref[:, c0:c0 + DI] += part

    cmap = (lambda c: (c, 0)) if rev else (lambda c: (NC - 1 - c, 0))
    smap = (lambda c: (c, 0, 0, 0)) if rev else (lambda c: (NC - 1 - c, 0, 0, 0))
    const = lambda c: (0, 0)
    sq = pltpu.VMEM((LANES, CHUNK), F32)
    cix = (lambda c: c) if rev else (lambda c: NC - 1 - c)
    tmap = lambda c: (0, cix(c))
    in_specs = [pl.BlockSpec((DI, CHUNK), tmap), pl.BlockSpec((CHUNK, 2 * G * N), cmap), pl.BlockSpec((CHUNK, LANES), cmap),
                pl.BlockSpec((DI, CHUNK), tmap),
                pl.BlockSpec((1, G, GW, N), smap),
                pl.BlockSpec((LANES, LANES), const), pl.BlockSpec((LANES, LANES), const)]
    out_specs = [pl.BlockSpec((CHUNK, XBC), cmap), pl.BlockSpec((CHUNK, LANES), cmap),
                 pl.BlockSpec((LANES, LANES), const)]
    out_shape = [jax.ShapeDtypeStruct((S, XBC), F32), jax.ShapeDtypeStruct((S, LANES), F32),
                 jax.ShapeDtypeStruct((LANES, LANES), F32)]
    args = [xsT, bc, dtraw, dyT, st, dtbT, alogT]
    if tail is not None:
        in_specs += [pl.BlockSpec((CHUNK, XBC), cmap), pl.BlockSpec((CHUNK, DI), cmap),
                     pl.BlockSpec((CHUNK, 2 * G * N), cmap), pl.BlockSpec((DI, LANES), const)]
        out_specs.append(pl.BlockSpec((SUBLANES, XBC), const))
        out_shape.append(jax.ShapeDtypeStruct((SUBLANES, XBC), F32))
        args += list(tail)
    return pl.pallas_call(
        body, name=name, grid=(NC,), in_specs=in_specs, out_specs=out_specs, out_shape=out_shape,
        scratch_shapes=[pltpu.VMEM((G, GW, N), F32), sq, sq, sq, sq],
        compiler_params=_params(dimension_semantics=("arbitrary",)),
    )(*args)


ANY = pl.BlockSpec(memory_space=pl.ANY)


def chip_exchange(name, groups, gather):
    flat = [arr for grp in groups for arr in grp]
    n_in = len(flat)
    n_out = len(groups)
    n_rc = 3 * n_in

    def body(*refs):
        in_refs = refs[:n_in]
        out_refs = refs[n_in:n_in + n_out]
        send, recv, loc = refs[n_in + n_out:]
        x, y, c = lax.axis_index("x"), lax.axis_index("y"), lax.axis_index("c")
        me = 2 * x + y
        peers = [(1 - x, y), (x, 1 - y), (1 - x, 1 - y)]
        local, remote = [], []
        q = 0
        for a, grp in enumerate(groups):
            for l in range(len(grp)):
                src = in_refs[q]
                dst = out_refs[a].at[me] if gather else out_refs[a].at[me, l]
                own = src if gather else src.at[me]
                lc = pltpu.make_async_copy(own, dst, loc.at[q])
                lc.start()
                local.append(lc)
                for j, (px, py) in enumerate(peers):
                    blk = src if gather else src.at[2 * px + py]
                    rc = pltpu.make_async_remote_copy(
                        src_ref=blk, dst_ref=dst, send_sem=send.at[3 * q + j], recv_sem=recv.at[3 * q + j],
                        device_id=(px, py, c), device_id_type=MESH)
                    rc.start()
                    remote.append(rc)
                q += 1
        for lc in local:
            lc.wait()
        for rc in remote:
            rc.wait()

    out_shape = []
    for grp in groups:
        a0 = grp[0]
        if gather:
            out_shape.append(jax.ShapeDtypeStruct((4,) + a0.shape, a0.dtype))
        else:
            out_shape.append(jax.ShapeDtypeStruct((4, len(grp)) + a0.shape[1:], a0.dtype))
    return pl.pallas_call(
        body, name=name, in_specs=[ANY] * n_in, out_specs=[ANY] * n_out, out_shape=out_shape,
        scratch_shapes=[pltpu.SemaphoreType.DMA((n_rc,)), pltpu.SemaphoreType.DMA((n_rc,)),
                        pltpu.SemaphoreType.DMA((n_in,))],
    )(*flat)


def gather_layer(name, split, whole):
    ns, nw = len(split), len(whole)
    n = ns + nw
    n_rc = 3 * (n + ns)

    def body(*refs):
        in_refs = refs[:n]
        out_refs = refs[n:2 * n]
        send, recv, loc = refs[2 * n:]
        x, y, c = lax.axis_index("x"), lax.axis_index("y"), lax.axis_index("c")
        me = 2 * x + y
        sibling = (x, y, 1 - c)
        peers = [(1 - x, y), (x, 1 - y), (1 - x, 1 - y)]

        def region(a, chip, half):
            if a >= ns:
                return out_refs[a].at[chip]
            hr = split[a].shape[0] // 2
            return out_refs[a].at[chip, pl.ds(half * hr, hr)]

        def mine(a):
            if a >= ns:
                return in_refs[a]
            hr = split[a].shape[0] // 2
            return in_refs[a].at[pl.ds(c * hr, hr)]

        local = []
        for a in range(n):
            lc = pltpu.make_async_copy(in_refs[a], out_refs[a].at[me], loc.at[a])
            lc.start()
            local.append(lc)
        sends = []
        for a in range(n):
            for j, (px, py) in enumerate(peers):
                rc = pltpu.make_async_remote_copy(
                    src_ref=mine(a), dst_ref=region(a, me, c), send_sem=send.at[3 * a + j],
                    recv_sem=recv.at[3 * a + j], device_id=(px, py, c), device_id_type=MESH)
                rc.start()
                sends.append(rc)
        for a in range(n):
            for j, (px, py) in enumerate(peers):
                chip = 2 * px + py
                landed = pltpu.make_async_remote_copy(
                    src_ref=mine(a), dst_ref=region(a, chip, c), send_sem=send.at[3 * a + j],
                    recv_sem=recv.at[3 * a + j], device_id=(px, py, c), device_id_type=MESH)
                landed.wait_recv()
                if a < ns:
                    fw = pltpu.make_async_remote_copy(
                        src_ref=region(a, chip, c), dst_ref=region(a, chip, c), send_sem=send.at[3 * n + 3 * a + j],
                        recv_sem=recv.at[3 * n + 3 * a + j], device_id=sibling, device_id_type=MESH)
                    fw.start()
                    sends.append(fw)
        for a in range(ns):
            for j, (px, py) in enumerate(peers):
                chip = 2 * px + py
                pltpu.make_async_remote_copy(
                    src_ref=region(a, chip, 1 - c), dst_ref=region(a, chip, 1 - c), send_sem=send.at[3 * n + 3 * a + j],
                    recv_sem=recv.at[3 * n + 3 * a + j], device_id=sibling, device_id_type=MESH).wait_recv()
        for rc in sends:
            rc.wait_send()
        for lc in local:
            lc.wait()

    arrs = list(split) + list(whole)
    return pl.pallas_call(
        body, name=name, in_specs=[ANY] * n, out_specs=[ANY] * n,
        out_shape=[jax.ShapeDtypeStruct((4,) + a.shape, a.dtype) for a in arrs],
        scratch_shapes=[pltpu.SemaphoreType.DMA((n_rc,)), pltpu.SemaphoreType.DMA((n_rc,)),
                        pltpu.SemaphoreType.DMA((n,))],
    )(*arrs)


HBM_SPEC = pl.BlockSpec(memory_space=pltpu.HBM)
SEM_SPEC = pl.BlockSpec(memory_space=pltpu.SEMAPHORE)
IN_FLIGHT = pltpu.SideEffectType.DATAFLOW_SIDE_EFFECTING


def _chip_leg(kind, a_ref, l_ref, shape, c, me, chip):
    if kind == 'gather':
        hr = shape[0] // 2
        rows = pl.ds(c * hr, hr)
        return a_ref.at[rows], l_ref.at[me, rows], l_ref.at[chip, rows]
    return a_ref.at[chip], l_ref.at[me], l_ref.at[chip]


def chip_legs_start(name, kind, arrs, lands):
    n = len(arrs)

    def body(*refs):
        a_refs = refs[:n]
        l_refs = refs[n:2 * n]
        send, recv = refs[2 * n], refs[2 * n + 1]
        token = refs[-1]
        x, y, c = lax.axis_index("x"), lax.axis_index("y"), lax.axis_index("c")
        me = 2 * x + y
        for a in range(n):
            for j, (px, py) in enumerate([(1 - x, y), (x, 1 - y), (1 - x, 1 - y)]):
                src, dst, _ = _chip_leg(kind, a_refs[a], l_refs[a], arrs[a].shape, c, me, 2 * px + py)
                pltpu.make_async_remote_copy(src_ref=src, dst_ref=dst, send_sem=send.at[3 * a + j],
                                             recv_sem=recv.at[3 * a + j], device_id=(px, py, c),
                                             device_id_type=MESH).start()
        token[...] = jnp.zeros_like(token)

    both = list(arrs) + list(lands)
    outs = pl.pallas_call(
        body, name=name,
        out_shape=(pltpu.SemaphoreType.DMA((3 * n,)), pltpu.SemaphoreType.DMA((3 * n,)),
                   *[pltpu.HBM(a.shape, a.dtype) for a in both], jax.ShapeDtypeStruct((SUBLANES, LANES), F32)),
        in_specs=[HBM_SPEC] * (2 * n),
        out_specs=(SEM_SPEC, SEM_SPEC, *[HBM_SPEC] * (2 * n), pl.BlockSpec(memory_space=pltpu.VMEM)),
        input_output_aliases={i: 2 + i for i in range(2 * n)},
        compiler_params=pltpu.CompilerParams(has_side_effects=IN_FLIGHT),
    )(*[pltpu.with_memory_space_constraint(a, pltpu.HBM) for a in both])
    return outs[0], outs[1], list(outs[2:2 + n]), list(outs[2 + n:2 + 2 * n]), outs[-1]


def chip_legs_wait(name, kind, send, recv, arrs, lands, after):
    n = len(arrs)

    def body(*refs):
        a_refs = refs[:n]
        l_refs = refs[n:2 * n]
        send_, recv_ = refs[2 * n], refs[2 * n + 1]
        x, y, c = lax.axis_index("x"), lax.axis_index("y"), lax.axis_index("c")
        me = 2 * x + y
        legs = []
        for a in range(n):
            for j, (px, py) in enumerate([(1 - x, y), (x, 1 - y), (1 - x, 1 - y)]):
                src, dst, landing = _chip_leg(kind, a_refs[a], l_refs[a], arrs[a].shape, c, me, 2 * px + py)
                legs.append(pltpu.make_async_remote_copy(src_ref=src, dst_ref=landing, send_sem=send_.at[3 * a + j],
                                                         recv_sem=recv_.at[3 * a + j], device_id=(px, py, c),
                                                         device_id_type=MESH))
        for leg in legs:
            leg.wait_send()
        for leg in legs:
            leg.wait_recv()

    both = list(arrs) + list(lands)
    outs = pl.pallas_call(
        body, name=name, out_shape=tuple(pltpu.HBM(a.shape, a.dtype) for a in both),
        in_specs=[HBM_SPEC] * (2 * n) + [SEM_SPEC, SEM_SPEC, ANY], out_specs=tuple([HBM_SPEC] * (2 * n)),
        input_output_aliases={i: i for i in range(2 * n)},
        compiler_params=pltpu.CompilerParams(has_side_effects=IN_FLIGHT),
    )(*both, send, recv, after)
    return list(outs[n:])


def gather_finish(name, split, landed):
    n = len(split)

    def body(*refs):
        in_refs = refs[:n]
        out_refs = refs[2 * n:3 * n]
        send, recv, loc = refs[3 * n:]
        x, y, c = lax.axis_index("x"), lax.axis_index("y"), lax.axis_index("c")
        me = 2 * x + y
        sibling = (x, y, 1 - c)
        chips = [2 * (1 - x) + y, 2 * x + (1 - y), 2 * (1 - x) + (1 - y)]

        def region(a, chip, half):
            hr = split[a].shape[0] // 2
            return out_refs[a].at[chip, pl.ds(half * hr, hr)]

        local, sends = [], []
        for a in range(n):
            lc = pltpu.make_async_copy(in_refs[a], out_refs[a].at[me], loc.at[a])
            lc.start()
            local.append(lc)
            for j, chip in enumerate(chips):
                fw = pltpu.make_async_remote_copy(
                    src_ref=region(a, chip, c), dst_ref=region(a, chip, c), send_sem=send.at[3 * a + j],
                    recv_sem=recv.at[3 * a + j], device_id=sibling, device_id_type=MESH)
                fw.start()
                sends.append(fw)
        for a in range(n):
            for j, chip in enumerate(chips):
                pltpu.make_async_remote_copy(
                    src_ref=region(a, chip, 1 - c), dst_ref=region(a, chip, 1 - c), send_sem=send.at[3 * a + j],
                    recv_sem=recv.at[3 * a + j], device_id=sibling, device_id_type=MESH).wait_recv()
        for fw in sends:
            fw.wait_send()
        for lc in local:
            lc.wait()

    return pl.pallas_call(
        body, name=name, in_specs=[ANY] * (2 * n), out_specs=[ANY] * n,
        out_shape=[jax.ShapeDtypeStruct(a.shape, a.dtype) for a in landed],
        input_output_aliases={n + a: a for a in range(n)},
        scratch_shapes=[pltpu.SemaphoreType.DMA((3 * n,)), pltpu.SemaphoreType.DMA((3 * n,)),
                        pltpu.SemaphoreType.DMA((n,))],
    )(*split, *landed)


def place_own(name, arrs, landed):
    n = len(arrs)

    def body(*refs):
        in_refs = refs[:n]
        out_refs = refs[2 * n:3 * n]
        loc = refs[3 * n]
        me = 2 * lax.axis_index("x") + lax.axis_index("y")
        cps = []
        for a in range(n):
            cp = pltpu.make_async_copy(in_refs[a].at[me], out_refs[a].at[me], loc.at[a])
            cp.start()
            cps.append(cp)
        for cp in cps:
            cp.wait()

    return pl.pallas_call(
        body, name=name, in_specs=[ANY] * (2 * n), out_specs=[ANY] * n,
        out_shape=[jax.ShapeDtypeStruct(a.shape, a.dtype) for a in landed],
        input_output_aliases={n + a: a for a in range(n)},
        scratch_shapes=[pltpu.SemaphoreType.DMA((n,))],
    )(*arrs, *landed)


def core_send_half(name, arrs):
    n = len(arrs)

    def body(*refs):
        in_refs = refs[:n]
        out_refs = refs[n:2 * n]
        send, recv = refs[2 * n:]
        c = lax.axis_index("c")
        peer = (lax.axis_index("x"), lax.axis_index("y"), 1 - c)
        rcs = []
        for a in range(n):
            hr = arrs[a].shape[1] // 2
            rc = pltpu.make_async_remote_copy(
                src_ref=in_refs[a].at[:, pl.ds((1 - c) * hr, hr)], dst_ref=out_refs[a], send_sem=send.at[a],
                recv_sem=recv.at[a], device_id=peer, device_id_type=MESH)
            rc.start()
            rcs.append(rc)
        for rc in rcs:
            rc.wait()

    return pl.pallas_call(
        body, name=name, in_specs=[ANY] * n, out_specs=[ANY] * n,
        out_shape=[jax.ShapeDtypeStruct((4, a.shape[1] // 2, a.shape[2]), a.dtype) for a in arrs],
        scratch_shapes=[pltpu.SemaphoreType.DMA((n,)), pltpu.SemaphoreType.DMA((n,))],
    )(*arrs)


def core_fill(name, arrs, layer, n_layers):
    n = len(arrs)

    def body(*refs):
        out_refs = refs[n:2 * n]
        send, recv = refs[2 * n:]
        c = lax.axis_index("c")
        peer = (lax.axis_index("x"), lax.axis_index("y"), 1 - c)
        rcs = []
        for a in range(n):
            r = arrs[a].shape[0] // n_layers
            hr = r // 2
            rows = out_refs[a].at[pl.ds(layer * r + c * hr, hr)]
            rc = pltpu.make_async_remote_copy(src_ref=rows, dst_ref=rows, send_sem=send.at[a], recv_sem=recv.at[a],
                                              device_id=peer, device_id_type=MESH)
            rc.start()
            rcs.append(rc)
        for a in range(n):
            r = arrs[a].shape[0] // n_layers
            hr = r // 2
            theirs = out_refs[a].at[pl.ds(layer * r + (1 - c) * hr, hr)]
            pltpu.make_async_remote_copy(src_ref=theirs, dst_ref=theirs, send_sem=send.at[a], recv_sem=recv.at[a],
                                         device_id=peer, device_id_type=MESH).wait_recv()
        for rc in rcs:
            rc.wait_send()

    return pl.pallas_call(
        body, name=name, in_specs=[ANY] * n, out_specs=[ANY] * n,
        out_shape=[jax.ShapeDtypeStruct(a.shape, a.dtype) for a in arrs],
        input_output_aliases={a: a for a in range(n)},
        scratch_shapes=[pltpu.SemaphoreType.DMA((n,)), pltpu.SemaphoreType.DMA((n,))],
    )(*arrs)


def sibling_swap(name, arrs):
    n = len(arrs)

    def body(*refs):
        in_refs = refs[:n]
        out_refs = refs[n:2 * n]
        send, recv = refs[2 * n:]
        peer = (lax.axis_index("x"), lax.axis_index("y"), 1 - lax.axis_index("c"))
        rcs = []
        for a in range(n):
            rc = pltpu.make_async_remote_copy(src_ref=in_refs[a], dst_ref=out_refs[a], send_sem=send.at[a],
                                              recv_sem=recv.at[a], device_id=peer, device_id_type=MESH)
            rc.start()
            rcs.append(rc)
        for rc in rcs:
            rc.wait()

    return pl.pallas_call(
        body, name=name, in_specs=[ANY] * n, out_specs=[ANY] * n,
        out_shape=[jax.ShapeDtypeStruct(a.shape, a.dtype) for a in arrs],
        scratch_shapes=[pltpu.SemaphoreType.DMA((n,)), pltpu.SemaphoreType.DMA((n,))],
    )(*arrs)


def all8_gather(name, v):
    flips = [(fx, fy, fc) for fx in (0, 1) for fy in (0, 1) for fc in (0, 1) if (fx, fy, fc) != (0, 0, 0)]

    def body(v_ref, out_ref, send, recv, loc):
        x, y, c = lax.axis_index("x"), lax.axis_index("y"), lax.axis_index("c")
        me = 4 * x + 2 * y + c
        lc = pltpu.make_async_copy(v_ref, out_ref.at[me], loc)
        lc.start()
        rcs = []
        for k, (fx, fy, fc) in enumerate(flips):
            tgt = (x + fx - 2 * x * fx, y + fy - 2 * y * fy, c + fc - 2 * c * fc)
            rc = pltpu.make_async_remote_copy(src_ref=v_ref, dst_ref=out_ref.at[me], send_sem=send.at[k],
                                              recv_sem=recv.at[k], device_id=tgt, device_id_type=MESH)
            rc.start()
            rcs.append(rc)
        lc.wait()
        for rc in rcs:
            rc.wait()

    return pl.pallas_call(
        body, name=name, in_specs=[ANY], out_specs=ANY,
        out_shape=jax.ShapeDtypeStruct((8,) + v.shape, v.dtype),
        scratch_shapes=[pltpu.SemaphoreType.DMA((7,)), pltpu.SemaphoreType.DMA((7,)), pltpu.SemaphoreType.DMA],
    )(v)


def _pick_rows(rows, cols, target_elems=128 * 1024, mult=SUBLANES):
    if rows % mult != 0:
        return rows
    best = mult
    t = mult
    while t <= rows:
        if rows % t == 0 and t * cols <= target_elems:
            best = t
        t += mult
    return best


def sum_chips(name, parts):
    _, R, C = parts.shape
    tm = _pick_rows(R, C)

    def body(p_ref, o_ref):
        o_ref[...] = (p_ref[0] + p_ref[1]) + (p_ref[2] + p_ref[3])

    return pl.pallas_call(
        body, name=name, grid=(R // tm,),
        in_specs=[pl.BlockSpec((4, tm, C), lambda i: (0, i, 0))],
        out_specs=pl.BlockSpec((tm, C), lambda i: (i, 0)),
        out_shape=jax.ShapeDtypeStruct((R, C), F32),
        compiler_params=_params(dimension_semantics=("arbitrary",)),
    )(parts)


def _adamw(g, w, m, v):
    m = ADAM_B1 * m + (1.0 - ADAM_B1) * g
    v = ADAM_B2 * v + (1.0 - ADAM_B2) * (g * g)
    m_hat = m / (1.0 - ADAM_B1 ** ADAM_STEP)
    v_hat = v / (1.0 - ADAM_B2 ** ADAM_STEP)
    delta = -ADAM_LR * (m_hat / (jnp.sqrt(v_hat) + ADAM_EPS) + ADAM_WD * w)
    return delta, m, v


def adamw_shard(name, s_mine, s_sib, w, m, v):
    R, C = w.shape
    tm = _pick_rows(R, C)

    def body(a_ref, b_ref, w_ref, m_ref, v_ref, g_out, d_out, m_out, v_out):
        g = a_ref[...] + b_ref[...]
        d, mn, vn = _adamw(g, w_ref[...], m_ref[...], v_ref[...])
        g_out[...] = g
        d_out[...] = d
        m_out[...] = mn
        v_out[...] = vn

    spec = pl.BlockSpec((tm, C), lambda i: (i, 0))
    return pl.pallas_call(
        body, name=name, grid=(R // tm,), in_specs=[spec] * 5, out_specs=[spec] * 4,
        out_shape=[jax.ShapeDtypeStruct((R, C), F32)] * 4,
        compiler_params=_params(dimension_semantics=("arbitrary",)),
    )(s_mine, s_sib, w, m, v)


def core_sum(name, core, g, got):
    _, r, C = g.shape
    hr = r // 2
    tm = _pick_rows(hr, 4 * C, 256 * 1024, 2 * SUBLANES)
    nh = hr // tm

    def body(c_ref, g_ref, s_ref, o_ref):
        o_ref[...] = (g_ref[...] + s_ref[...]).astype(BF16)

    return pl.pallas_call(
        body, name=name,
        grid_spec=pltpu.PrefetchScalarGridSpec(
            num_scalar_prefetch=1, grid=(nh,),
            in_specs=[pl.BlockSpec((4, tm, C), lambda i, cr: (0, cr[0] * nh + i, 0)),
                      pl.BlockSpec((4, tm, C), lambda i, cr: (0, i, 0))],
            out_specs=pl.BlockSpec((4, tm, C), lambda i, cr: (0, i, 0))),
        out_shape=jax.ShapeDtypeStruct((4, hr, C), BF16),
        compiler_params=_params(dimension_semantics=("arbitrary",)),
    )(core, g, got)


def chip_sum_into(name, core, parts, layer, n_layers, into=None):
    _, hr, C = parts.shape
    r = 2 * hr
    tm = _pick_rows(hr, 4 * C, 256 * 1024, 2 * SUBLANES)
    nh = hr // tm

    def body(c_ref, p_ref, *rest):
        o_ref = rest[-1]
        o_ref[...] = (_f32(p_ref[0]) + _f32(p_ref[1])) + (_f32(p_ref[2]) + _f32(p_ref[3]))

    in_specs = [pl.BlockSpec((4, tm, C), lambda i, cr: (0, i, 0))]
    args = [core, parts]
    aliases = {}
    if into is not None:
        in_specs.append(pl.BlockSpec(memory_space=pl.ANY))
        args.append(into)
        aliases = {2: 0}
    return pl.pallas_call(
        body, name=name,
        grid_spec=pltpu.PrefetchScalarGridSpec(
            num_scalar_prefetch=1, grid=(nh,), in_specs=in_specs,
            out_specs=pl.BlockSpec((tm, C), lambda i, cr: ((layer * r) // tm + cr[0] * nh + i, 0))),
        out_shape=jax.ShapeDtypeStruct((n_layers * r, C), F32), input_output_aliases=aliases,
        compiler_params=_params(dimension_semantics=("arbitrary",)),
    )(*args)


def adamw_full(name, g, w, m, v):
    R, C = w.shape
    tm = _pick_rows(R, C)

    def body(g_ref, w_ref, m_ref, v_ref, d_out, m_out, v_out):
        d, mn, vn = _adamw(g_ref[...], w_ref[...], m_ref[...], v_ref[...])
        d_out[...] = d
        m_out[...] = mn
        v_out[...] = vn

    spec = pl.BlockSpec((tm, C), lambda i: (i, 0))
    return pl.pallas_call(
        body, name=name, grid=(R // tm,), in_specs=[spec] * 4, out_specs=[spec] * 3,
        out_shape=[jax.ShapeDtypeStruct((R, C), F32)] * 3,
        compiler_params=_params(dimension_semantics=("arbitrary",)),
    )(g, w, m, v)


def adamw_small(name, parts, w, m, v):
    W = w.shape[1]

    def body(p_ref, w_ref, m_ref, v_ref, g_out, d_out, m_out, v_out):
        acc = p_ref[0]
        for k in range(1, 8):
            acc = acc + p_ref[k]
        g = jnp.sum(acc, axis=0, keepdims=True)
        d, mn, vn = _adamw(g, w_ref[...], m_ref[...], v_ref[...])
        g_out[...] = g
        d_out[...] = d
        m_out[...] = mn
        v_out[...] = vn

    return pl.pallas_call(
        body, name=name, out_shape=[jax.ShapeDtypeStruct((1, W), F32)] * 4,
        compiler_params=_params(),
    )(parts, w, m, v)


def _pad_lanes(v, width=LANES):
    return jnp.pad(v, ((0, 0), (0, width - v.shape[1])))


def _layer_fwd(cf, x, xb, pb, W, sm):
    S, D, CD, DI, XBC, F, H, G = cf['S'], cf['D'], cf['CD'], cf['DI'], cf['XBC'], cf['F'], cf['H'], cf['G']
    NM = cf['NM']
    alpha = cf['alpha']
    tm = cf['tm']
    tmx = cf['tmx']
    tn_in = cf['tn_in']
    sv = {}

    ident = lambda acc, ex: ([acc], [])
    proj, = fused_mm("in_proj", [(xb, W['in_main'], 0, False)], [], ident, [(NM, BF16, tn_in, 0)],
                     M=S, tm=tmx, tn=tn_in, nj=NM // tn_in)
    dtraw, = fused_mm("dt_proj", [(xb, W['in_dt'], 0, False)], [], ident, [(LANES, F32, LANES, 0)],
                      M=S, tm=tmx, tn=LANES)

    u, = row_call("glu", lambda a, gt: ([a * _sig(gt)], []),
                  [(proj, 'row', CD, 0, 0), (proj, 'row', CD, 1, 0)], [(CD, F32, CD, 0, 0)], M=S, tm=tm)

    def conv_a_epi(conv, ex):
        cb_, g_, b_ = ex
        ca = conv + cb_
        xhat, _ = _ln_stats(ca)
        la = xhat * g_ + b_
        return [ca, la * _sig(la)], []

    ca, sa = conv_call("conv_a", u, 0, sm['conv_a_w'], cf['KA'], conv_a_epi,
                       [(sm['conv_a_b'], 'vec', CD, 0, 0), (sm['ln_a_g'], 'vec', CD, 0, 0), (sm['ln_a_b'], 'vec', CD, 0, 0)],
                       [(CD, F32, CD, 0, 0), (CD, BF16, CD, 0, 0)], M=S, tm=cf['tmc'], cw=CD, nc=1, reverse=False)
    y_a, = fused_mm("a_out", [(sa, W['a_out'], 0, False)], [], ident, [(D, F32, D, 0)], M=S, tm=tmx, tn=D)

    def conv_x_epi(conv, ex):
        cb = conv + ex[0]
        act = cb * _sig(cb)
        return [cb, act], [], [act]

    def conv_bc_epi(conv, ex):
        cb = conv + ex[0]
        return [cb, cb * _sig(cb)], []

    xoff = (2 * CD + 2 * D + DI) // DI
    cbv_x, xs, xsT = conv_call("conv_b_x", proj, xoff, sm['ssm_conv_w'], cf['KB'], conv_x_epi,
                               [(sm['ssm_conv_b'], 'vec', DI, 0, 0)],
                               [(DI, F32, DI, 0, 0), (DI, F32, DI, 0, 0)], M=S, tm=cf['tmc'], cw=DI, nc=1,
                               reverse=False, t_outs=[(DI, F32, DI, 0, 0)])
    cbv_bc, bc = conv_call("conv_b_bc", proj, xoff + 1, sm['ssm_conv_w'], cf['KB'], conv_bc_epi,
                           [(sm['ssm_conv_b'], 'vec', DI, 1, 0)],
                           [(DI, F32, DI, 0, 0), (DI, F32, DI, 0, 0)], M=S, tm=cf['tmc'], cw=DI, nc=1,
                           reverse=False, w_c0=1)
    y_f, st_f = ssd_fwd("ssd_fwd_f", xsT, bc, dtraw, sm['dtb_f'], sm['alog_f'], S=S, DI=DI, G=G, H=H, rev=False)
    zoff = (2 * CD + 2 * D) // DI
    ysum, st_r, yn = ssd_fwd("ssd_fwd_r", xsT, bc, dtraw, sm['dtb_r'], sm['alog_r'], S=S, DI=DI, G=G, H=H, rev=True,
                             tail=(y_f, (proj, zoff), xs, sm['dskip_full'], sm['ssm_norm_g']))
    goff = (2 * CD) // D

    def merge_epi(acc, ex):
        ga, gb, ya = ex
        return [acc, _sig(ga) * ya + _sig(gb) * acc], []

    y_b, merged = fused_mm("b_out", [(yn, W['b_out'], 0, False)],
                           [(proj, 'row', D, goff), (proj, 'row', D, goff + 1), (y_a, 'row', D, 0)],
                           merge_epi, [(D, F32, D, 0), (D, BF16, D, 0)], M=S, tm=tm, tn=D)

    def mix_epi(acc, ex):
        xin, g_, b_ = ex
        r1 = alpha * xin + acc
        xhat, _ = _ln_stats(r1)
        return [r1, xhat * g_ + b_], []

    r1, hb = fused_mm("o_mix", [(merged, W['o'], 0, False)],
                      [(x, 'row', D, 0), (sm['ln1_g'], 'vec', D, 0), (sm['ln1_b'], 'vec', D, 0)],
                      mix_epi, [(D, F32, D, 0), (D, BF16, D, 0)], M=S, tm=tm, tn=D)

    tnf = cf['tnf']

    g32, g_ = fused_mm("ffn_gate", [(hb, W['gate_up'], 0, False)], [], lambda acc, ex: ([acc, acc], []),
                       [(F, F32, tnf, 0), (F, BF16, tnf, 0)], M=S, tm=tmx, tn=tnf, nj=F // tnf)
    u_, f = fused_mm("ffn_up", [(hb, W['gate_up'], F // tnf, False)], [(g32, 'row', tnf, 0)],
                     lambda acc, ex: ([acc, ex[0] * _sig(ex[0]) * acc], []),
                     [(F, BF16, tnf, 0), (F, BF16, tnf, 0)], M=S, tm=tmx, tn=tnf, nj=F // tnf)

    def down_epi(acc, ex):
        r1_, g1, b1, g2, b2 = ex
        xh1, _ = _ln_stats(r1_)
        r2 = alpha * (xh1 * g1 + b1) + acc
        xh2, _ = _ln_stats(r2)
        return [r2, xh2 * g2 + b2], []

    r2, h2b = fused_mm("ffn_down", [(f, W['down'], 0, False)],
                       [(r1, 'row', D, 0), (sm['ln1_g'], 'vec', D, 0), (sm['ln1_b'], 'vec', D, 0),
                        (sm['ln2_g'], 'vec', D, 0), (sm['ln2_b'], 'vec', D, 0)],
                       down_epi, [(D, F32, D, 0), (D, BF16, D, 0)], M=S, tm=tm, tn=D)

    pe, = fused_mm("ple_proj", [(pb, W['ple'], 0, False)], [], ident, [(D, F32, D, 0)], M=S, tm=tmx, tn=D)

    def ple_epi(acc, ex):
        r2_, g2, b2, pe_, pg = ex
        xh2, _ = _ln_stats(r2_)
        h2 = xh2 * g2 + b2
        e = pe_ * lax.rsqrt(jnp.mean(pe_ * pe_, axis=-1, keepdims=True) + RMS_EPS) * pg
        xn = h2 + e * _sig(acc)
        return [acc, xn, xn], []

    t_, xn, xnb = fused_mm("ple_gate", [(h2b, W['ple_gate'], 0, False)],
                           [(r2, 'row', D, 0), (sm['ln2_g'], 'vec', D, 0), (sm['ln2_b'], 'vec', D, 0),
                            (pe, 'row', D, 0), (sm['ple_norm_g'], 'vec', D, 0)],
                           ple_epi, [(D, F32, D, 0), (D, F32, D, 0), (D, BF16, D, 0)], M=S, tm=tm, tn=D)
    sv.update(x=x, xb=xb, pb=pb, proj=proj, dtraw=dtraw, u=u, ca=ca, sa=sa, y_a=y_a, cbv_x=cbv_x, cbv_bc=cbv_bc,
              xs=xs, xsT=xsT, bc=bc,
              ysum=ysum, st_f=st_f, st_r=st_r, yn=yn, y_b=y_b, merged=merged, r1=r1, hb=hb,
              g_=g_, u_=u_, f=f, r2=r2, h2b=h2b, t_=t_, pe=pe)
    return xn, xnb, sv


def _layer_bwd(cf, sv, W, sm, dxn=None, target=None, xn=None):
    S, D, CD, DI, XBC, F, H, G = cf['S'], cf['D'], cf['CD'], cf['DI'], cf['XBC'], cf['F'], cf['H'], cf['G']
    NM = cf['NM']
    alpha = cf['alpha']
    tm = cf['tm']
    gw = cf['GW']
    out = {}

    def ple_bwd_core(dx_, t, pe_, pg):
        s = _sig(t)
        rinv = lax.rsqrt(jnp.mean(pe_ * pe_, axis=-1, keepdims=True) + RMS_EPS)
        pn = pe_ * rinv
        e = pn * pg
        dtg = dx_ * e * (s * (1.0 - s))
        de = dx_ * s
        qv = de * pg
        dpe = rinv * (qv - pn * jnp.mean(qv * pn, axis=-1, keepdims=True))
        return dtg, dpe, de * pn

    if dxn is None:
        def head(xn_, tgt, t, pe_, pg):
            err = xn_ - tgt
            dx_ = err * (1.0 / D)
            dtg, dpe, dpg = ple_bwd_core(dx_, t, pe_, pg)
            return [dx_, dtg, dpe], [dpg, err * err]

        (dxn, dtg, dpe, dpg, lsq) = row_call(
            "loss_ple_bwd", head,
            [(xn, 'row', D, 0, 0), (target, 'row', D, 0, 0), (sv['t_'], 'row', D, 0, 0), (sv['pe'], 'row', D, 0, 0),
             (sm['ple_norm_g'], 'vec', D, 0, 0)],
            [(D, F32, D, 0, 0), (D, BF16, D, 0, 0), (D, BF16, D, 0, 0)], [(D, D, 0, 0), (D, D, 0, 0)], M=S, tm=tm)
        out['loss_sq'] = lsq
    else:
        def mid(dx_, t, pe_, pg):
            dtg, dpe, dpg = ple_bwd_core(dx_, t, pe_, pg)
            return [dtg, dpe], [dpg]

        (dtg, dpe, dpg) = row_call(
            "ple_bwd", mid,
            [(dxn, 'row', D, 0, 0), (sv['t_'], 'row', D, 0, 0), (sv['pe'], 'row', D, 0, 0),
             (sm['ple_norm_g'], 'vec', D, 0, 0)],
            [(D, BF16, D, 0, 0), (D, BF16, D, 0, 0)], [(D, D, 0, 0)], M=S, tm=tm)
    out['ple_norm_g'] = dpg

    def ln_bwd_epi(scale):
        def epi(acc, ex):
            res, r_, g_ = ex
            dh = scale * res + acc
            xhat, rstd = _ln_stats(r_)
            dr = _ln_bwd(dh, xhat, rstd, g_)
            return [dr, dr], [dh * xhat, dh]
        return epi

    dr2, dr2b, dg2, db2 = fused_mm(
        "dh2", [(dtg, W['ple_gate_T'], 0, False)],
        [(dxn, 'row', D, 0), (sv['r2'], 'row', D, 0), (sm['ln2_g'], 'vec', D, 0)],
        ln_bwd_epi(1.0), [(D, F32, D, 0), (D, BF16, D, 0)], [(D, D, 0), (D, D, 0)], M=S, tm=tm, tn=D)
    out['ln2_g'], out['ln2_b'] = dg2, db2

    tnf = cf['tnf']

    def dswiglu_epi(acc, ex):
        gg, uu = ex
        s = _sig(gg)
        return [acc * uu * _dsilu(gg, s), acc * (gg * s)], []

    dg_b, du_b = fused_mm(
        "d_down", [(dr2b, W['down_T'], 0, False)],
        [(sv['g_'], 'row', tnf, 0), (sv['u_'], 'row', tnf, 0)], dswiglu_epi,
        [(F, BF16, tnf, 0), (F, BF16, tnf, 0)], M=S, tm=tm, tn=tnf, nj=F // tnf)

    dr1, dr1b, dg1, db1 = fused_mm(
        "dh1", [(dg_b, W['gate_T'], 0, True), (du_b, W['up_T'], 0, True)],
        [(dr2, 'row', D, 0), (sv['r1'], 'row', D, 0), (sm['ln1_g'], 'vec', D, 0)],
        ln_bwd_epi(alpha), [(D, F32, D, 0), (D, BF16, D, 0)], [(D, D, 0), (D, D, 0)],
        M=S, tm=tm, tn=D, nk=cf['nk_f'])
    out['ln1_g'], out['ln1_b'] = dg1, db1

    goff = (2 * CD) // D

    def dmerge_epi(acc, ex):
        ga, gb, ya, yb = ex
        sa_, sb_ = _sig(ga), _sig(gb)
        dga = acc * ya * (sa_ * (1.0 - sa_))
        dgb = acc * yb * (sb_ * (1.0 - sb_))
        return [jnp.concatenate([dga, dgb], axis=1), acc * sa_, acc * sb_], []

    dproj, dya_b, dyb_b = fused_mm(
        "d_merge", [(dr1b, W['o_T'], 0, False)],
        [(sv['proj'], 'row', D, goff), (sv['proj'], 'row', D, goff + 1), (sv['y_a'], 'row', D, 0), (sv['y_b'], 'row', D, 0)],
        dmerge_epi, [(NM, BF16, 2 * D, (2 * CD) // (2 * D)), (D, BF16, D, 0), (D, BF16, D, 0)], M=S, tm=tm, tn=D)

    def dsa_epi(acc, ex):
        ca_, g_, b_ = ex
        xhat, rstd = _ln_stats(ca_)
        la = xhat * g_ + b_
        dla = acc * _dsilu(la, _sig(la))
        dca = _ln_bwd(dla, xhat, rstd, g_)
        return [dca], [dla * xhat, dla, dca]

    dca, dlag, dlab, dcab = fused_mm(
        "d_a_out", [(dya_b, W['a_out_T'], 0, False)],
        [(sv['ca'], 'row', CD, 0), (sm['ln_a_g'], 'vec', CD, 0), (sm['ln_a_b'], 'vec', CD, 0)],
        dsa_epi, [(CD, F32, CD, 0)], [(CD, CD, 0), (CD, CD, 0), (CD, CD, 0)], M=S, tm=tm, tn=D)
    out['ln_a_g'], out['ln_a_b'], out['conv_a_b'] = dlag, dlab, dcab

    def dglu_epi(du, ex):
        a, gt = ex
        s = _sig(gt)
        return [jnp.concatenate([du * s, du * a * (s * (1.0 - s))], axis=1)], []

    dproj, dwa = conv_call(
        "d_conv_a", dca, 0, sm['conv_a_w'], cf['KA'], dglu_epi,
        [(sv['proj'], 'row', CD, 0, 0), (sv['proj'], 'row', CD, 1, 0)],
        [(NM, BF16, 2 * CD, 0, 0)], M=S, tm=cf['tmc'], cw=CD, nc=1, reverse=True, xin=(sv['u'], 0),
        passthrough=(dproj, 0))
    out['conv_a_w'] = dwa

    zoff = (2 * CD + 2 * D) // DI

    def dgate_norm_epi(acc, ex):
        ysum_, xs, z, dsk, ng = ex
        y = ysum_ + xs * dsk
        sz = _sig(z)
        siluz = z * sz
        yz = y * siluz
        dyzs, yhats = [], []
        for g in range(G):
            t = yz[:, g * gw:(g + 1) * gw]
            rinv = lax.rsqrt(jnp.mean(t * t, axis=-1, keepdims=True) + RMS_EPS)
            yh = t * rinv
            qv = acc[:, g * gw:(g + 1) * gw] * ng[:, g * gw:(g + 1) * gw]
            dyzs.append(rinv * (qv - yh * jnp.mean(qv * yh, axis=-1, keepdims=True)))
            yhats.append(yh)
        dyz = jnp.concatenate(dyzs, axis=1)
        yhat = jnp.concatenate(yhats, axis=1)
        dy = dyz * siluz
        dz = dyz * y * _dsilu(z, sz)
        return [dz], [acc * yhat, dy * xs], [dy]

    tmr = cf['tmr']
    dproj, dng, ddsk, dyT = fused_mm(
        "d_b_out", [(dyb_b, W['b_out_T'], 0, False)],
        [(sv['ysum'], 'row', DI, 0), (sv['xs'], 'row', DI, 0), (sv['proj'], 'row', DI, zoff),
         (sm['dskip_full'], 'vec', DI, 0), (sm['ssm_norm_g'], 'vec', DI, 0)],
        dgate_norm_epi, [(NM, BF16, DI, zoff)], [(DI, DI, 0), (DI, DI, 0)],
        M=S, tm=tmr, tn=DI, passthrough=(dproj, 0), t_outs=[(DI, F32, DI, 0)])
    out['ssm_norm_g'], out['dskip_full'] = dng, ddsk

    dxbc_f, ddt_f, dA_f = ssd_bwd("ssd_bwd_f", sv['xsT'], sv['bc'], sv['dtraw'], dyT, sv['st_f'], sm['dtb_f'],
                                  sm['alog_f'], S=S, DI=DI, G=G, H=H, rev=False)
    dcb, ddt_r, dA_r, dcbb = ssd_bwd("ssd_bwd_r", sv['xsT'], sv['bc'], sv['dtraw'], dyT, sv['st_r'], sm['dtb_r'],
                                     sm['alog_r'], S=S, DI=DI, G=G, H=H, rev=True,
                                     tail=(dxbc_f, sv['cbv_x'], sv['cbv_bc'], sm['dskipT']))
    out['dA_f'], out['dA_r'] = dA_f, dA_r
    out['ssm_conv_b'] = dcbb

    xoff = (2 * CD + 2 * D + DI) // DI
    dproj, dwb = conv_call(
        "d_conv_b", dcb, 0, sm['ssm_conv_w'], cf['KB'], lambda conv, ex: ([conv], []), [],
        [(NM, BF16, DI, xoff, 1)], M=S, tm=cf['tmc'], cw=DI, nc=XBC // DI, reverse=True, xin=(sv['proj'], xoff),
        passthrough=(dproj, 0))
    out['ssm_conv_w'] = dwb

    ddtb, ddt_bias = row_call("d_dt", lambda a, b: ([a + b], [a + b]),
                              [(ddt_f, 'row', LANES, 0, 0), (ddt_r, 'row', LANES, 0, 0)],
                              [(LANES, BF16, LANES, 0, 0)], [(LANES, LANES, 0, 0)], M=S, tm=tm)
    out['dt_bias'] = ddt_bias

    dx, = fused_mm("d_x", [(dproj, W['in_main_T'], 0, True), (ddtb, W['in_dt_T'], 0, False)],
                   [(dr1, 'row', D, 0)], lambda acc, ex: ([alpha * ex[0] + acc], []),
                   [(D, F32, D, 0)], M=S, tm=cf['tmx'], tn=D, nk=cf['nk_in'])

    tmw = cf['tmw']
    xb = sv['xb']
    out['w_in'] = jnp.concatenate(
        [mm_tn("dw_in", xb, dproj, tm=tmw, tk=D, tn=cf['tn_in']),
         mm_tn("dw_dt", xb, ddtb, tm=tmw, tk=D, tn=LANES)[:, :2 * H]], axis=1)
    out['w_a_out'] = mm_tn("dw_a_out", sv['sa'], dya_b, tm=tmw, tk=CD, tn=D)
    out['w_b_out'] = mm_tn("dw_b_out", sv['yn'], dyb_b, tm=tmw, tk=DI // 2, tn=D)
    out['w_o'] = mm_tn("dw_o", sv['merged'], dr1b, tm=tmw, tk=D, tn=D)
    out['w_gate_up'] = jnp.concatenate(
        [mm_tn("dw_gate", sv['hb'], dg_b, tm=tmw, tk=D, tn=tnf),
         mm_tn("dw_up", sv['hb'], du_b, tm=tmw, tk=D, tn=tnf)], axis=1)
    out['w_down'] = mm_tn("dw_down", sv['f'], dr2b, tm=tmw, tk=tnf, tn=D)
    out['w_ple'] = mm_tn("dw_ple", sv['pb'], dpe, tm=tmw, tk=sv['pb'].shape[1], tn=D)
    out['w_ple_gate'] = mm_tn("dw_ple_gate", sv['h2b'], dtg, tm=tmw, tk=D, tn=D)
    return dx, out


_WEIGHTS = ['w_in', 'conv_a_w', 'conv_a_b', 'ln_a_g', 'ln_a_b', 'w_a_out', 'ssm_conv_w', 'ssm_conv_b', 'a_log',
            'dt_bias', 'd_skip', 'ssm_norm_g', 'w_b_out', 'w_o', 'ln1_g', 'ln1_b', 'w_gate_up', 'w_down', 'ln2_g',
            'ln2_b', 'w_ple', 'ple_norm_g', 'w_ple_gate']
_COL_SHARDED = ['w_in', 'conv_a_w', 'ssm_conv_w', 'w_gate_up', 'w_ple']
_ROW_SHARDED = ['w_a_out', 'w_b_out', 'w_o', 'w_down', 'w_ple_gate']
_BIG = _COL_SHARDED + _ROW_SHARDED
_SMALL = [n for n in _WEIGHTS if n not in _BIG]
_CONV = ['conv_a_w', 'ssm_conv_w']


def _ceil_to(n, k):
    return -(-n // k) * k


def kernel(x, p, w_in, conv_a_w, conv_a_b, ln_a_g, ln_a_b, w_a_out, ssm_conv_w, ssm_conv_b, a_log, dt_bias, d_skip, ssm_norm_g, w_b_out, w_o, ln1_g, ln1_b, w_gate_up, w_down, ln2_g, ln2_b, w_ple, ple_norm_g, w_ple_gate, loss_target, m_w_in, m_conv_a_w, m_conv_a_b, m_ln_a_g, m_ln_a_b, m_w_a_out, m_ssm_conv_w, m_ssm_conv_b, m_a_log, m_dt_bias, m_d_skip, m_ssm_norm_g, m_w_b_out, m_w_o, m_ln1_g, m_ln1_b, m_w_gate_up, m_w_down, m_ln2_g, m_ln2_b, m_w_ple, m_ple_norm_g, m_w_ple_gate, v_w_in, v_conv_a_w, v_conv_a_b, v_ln_a_g, v_ln_a_b, v_w_a_out, v_ssm_conv_w, v_ssm_conv_b, v_a_log, v_dt_bias, v_d_skip, v_ssm_norm_g, v_w_b_out, v_w_o, v_ln1_g, v_ln1_b, v_w_gate_up, v_w_down, v_ln2_g, v_ln2_b, v_w_ple, v_ple_norm_g, v_w_ple_gate):
    wt = dict(w_in=w_in, conv_a_w=conv_a_w, conv_a_b=conv_a_b, ln_a_g=ln_a_g, ln_a_b=ln_a_b, w_a_out=w_a_out,
              ssm_conv_w=ssm_conv_w, ssm_conv_b=ssm_conv_b, a_log=a_log, dt_bias=dt_bias, d_skip=d_skip,
              ssm_norm_g=ssm_norm_g, w_b_out=w_b_out, w_o=w_o, ln1_g=ln1_g, ln1_b=ln1_b, w_gate_up=w_gate_up,
              w_down=w_down, ln2_g=ln2_g, ln2_b=ln2_b, w_ple=w_ple, ple_norm_g=ple_norm_g, w_ple_gate=w_ple_gate)
    mo = dict(w_in=m_w_in, conv_a_w=m_conv_a_w, conv_a_b=m_conv_a_b, ln_a_g=m_ln_a_g, ln_a_b=m_ln_a_b,
              w_a_out=m_w_a_out, ssm_conv_w=m_ssm_conv_w, ssm_conv_b=m_ssm_conv_b, a_log=m_a_log,
              dt_bias=m_dt_bias, d_skip=m_d_skip, ssm_norm_g=m_ssm_norm_g, w_b_out=m_w_b_out, w_o=m_w_o,
              ln1_g=m_ln1_g, ln1_b=m_ln1_b, w_gate_up=m_w_gate_up, w_down=m_w_down, ln2_g=m_ln2_g, ln2_b=m_ln2_b,
              w_ple=m_w_ple, ple_norm_g=m_ple_norm_g, w_ple_gate=m_w_ple_gate)
    vo = dict(w_in=v_w_in, conv_a_w=v_conv_a_w, conv_a_b=v_conv_a_b, ln_a_g=v_ln_a_g, ln_a_b=v_ln_a_b,
              w_a_out=v_w_a_out, ssm_conv_w=v_ssm_conv_w, ssm_conv_b=v_ssm_conv_b, a_log=v_a_log,
              dt_bias=v_dt_bias, d_skip=v_d_skip, ssm_norm_g=v_ssm_norm_g, w_b_out=v_w_b_out, w_o=v_w_o,
              ln1_g=v_ln1_g, ln1_b=v_ln1_b, w_gate_up=v_w_gate_up, w_down=v_w_down, ln2_g=v_ln2_g, ln2_b=v_ln2_b,
              w_ple=v_w_ple, ple_norm_g=v_ple_norm_g, w_ple_gate=v_w_ple_gate)

    L = w_in.shape[0]
    S, D = x.shape[1], x.shape[2]
    CD = conv_a_b.shape[1]
    DI = ssm_norm_g.shape[1]
    XBC = ssm_conv_b.shape[1]
    H = d_skip.shape[1]
    G = (XBC - DI) // (2 * D_STATE)
    F = w_down.shape[1] * 4
    N_IN = w_in.shape[2] * 4
    NM = N_IN - 2 * H
    KA, KB = conv_a_w.shape[1], ssm_conv_w.shape[1]
    assert DI == H * HEAD_DIM and CD == D and DI == 2 * D and XBC == 2 * DI and NM == 2 * CD + 2 * D + DI + XBC
    assert 2 * H <= LANES and S % CHUNK == 0
    tnf = F // 2
    cf = dict(S=S, D=D, CD=CD, DI=DI, XBC=XBC, F=F, H=H, G=G, NM=NM, KA=KA, KB=KB, GW=(H // G) * HEAD_DIM,
              alpha=float((2 * L) ** 0.25), tm=min(512, S), tmx=min(1024, S), tmc=min(256, S), tmr=min(256, S), tmw=min(1024, S),
              tn_in=D, tnf=tnf, nk_f=2, nk_in=NM // DI)

    core = lax.axis_index("c").astype(jnp.int32).reshape(1)
    split_names = [n for n in _BIG if n not in _CONV]

    def layer_weights(l, got):
        full = {}
        for n, g in zip(split_names + _CONV, got):
            if n in _COL_SHARDED:
                full[n] = g.transpose(1, 0, 2).reshape(g.shape[1], 4 * g.shape[2])
            else:
                full[n] = g.reshape(4 * g.shape[1], g.shape[2])
        win = full['w_in']
        in_main = win[:, :NM]
        in_dt = _pad_lanes(win[:, NM:])
        gu = full['w_gate_up']
        W = dict(in_main=in_main, in_dt=in_dt, in_main_T=in_main.T, in_dt_T=in_dt.T,
                 a_out=full['w_a_out'], a_out_T=full['w_a_out'].T,
                 b_out=full['w_b_out'], b_out_T=full['w_b_out'].T,
                 o=full['w_o'], o_T=full['w_o'].T, gate_up=gu, gate_T=gu[:, :F].T, up_T=gu[:, F:].T,
                 down=full['w_down'], down_T=full['w_down'].T, ple=full['w_ple'],
                 ple_gate=full['w_ple_gate'], ple_gate_T=full['w_ple_gate'].T)
        row = lambda v: v.reshape(1, -1)
        head_table = lambda v: jnp.broadcast_to(jnp.pad(v, (0, LANES - H))[:, None], (LANES, LANES))
        sm = dict(conv_a_w=jnp.pad(full['conv_a_w'], ((0, _ceil_to(KA, SUBLANES) - KA), (0, 0))),
                  ssm_conv_w=jnp.pad(full['ssm_conv_w'], ((0, _ceil_to(KB, SUBLANES) - KB), (0, 0))),
                  conv_a_b=row(conv_a_b[l]), ln_a_g=row(ln_a_g[l]), ln_a_b=row(ln_a_b[l]),
                  ssm_conv_b=row(ssm_conv_b[l]), ssm_norm_g=row(ssm_norm_g[l]),
                  ln1_g=row(ln1_g[l]), ln1_b=row(ln1_b[l]), ln2_g=row(ln2_g[l]), ln2_b=row(ln2_b[l]),
                  ple_norm_g=row(ple_norm_g[l]),
                  dtb_f=head_table(dt_bias[l, 0]), dtb_r=head_table(dt_bias[l, 1]),
                  alog_f=head_table(a_log[l, 0]), alog_r=head_table(a_log[l, 1]),
                  dskip_full=row(jnp.repeat(d_skip[l], HEAD_DIM)),
                  dskipT=jnp.broadcast_to(jnp.repeat(d_skip[l], HEAD_DIM)[:, None], (DI, LANES)))
        return W, sm

    def blocks(n, gl):
        g = gl[n]
        if n == 'conv_a_w':
            g = g.sum(axis=1)[:KA]
        elif n == 'ssm_conv_w':
            g = g.sum(axis=1)[:KB]
        if n in _COL_SHARDED:
            return g.reshape(g.shape[0], 4, g.shape[1] // 4).transpose(1, 0, 2)
        return g.reshape(4, g.shape[0] // 4, g.shape[1])

    def core_sums(gl):
        mine = [blocks(n, gl) for n in split_names]
        theirs = core_send_half("core_send_half", mine)
        return [core_sum("core_sum_" + n, core, b, t) for n, b, t in zip(split_names, mine, theirs)]

    def chip_sums(l, parts, acc):
        sums = [chip_sum_into("chip_sum_" + n, core, pr, l, L, into=acc.get(n)) for n, pr in zip(split_names, parts)]
        return dict(zip(split_names, core_fill("core_fill", sums, l, L)))

    def shards(l):
        return [wt[n][l].astype(BF16) for n in split_names]

    lw = [None] * L
    pending = None
    for l in range(L):
        if l < L - 1 or L == 1:
            lw[l] = layer_weights(l, gather_layer("gather_weights", shards(l), [wt[n][l] for n in _CONV]))
    xl = x[0]
    if L > 1:
        sh = shards(L - 1)
        send, recv, sh, lands, token = chip_legs_start(
            "gather_start", 'gather', sh, [lax.empty((4,) + a.shape, a.dtype) for a in sh])
        pending = (send, recv, sh, lands)
        xlb = (xl + token[0, 0]).astype(BF16)
    else:
        xlb = xl.astype(BF16)
    saved = []
    for l in range(L):
        if l == L - 1 and pending is not None:
            send, recv, sh, lands = pending
            landed = chip_legs_wait("gather_wait", 'gather', send, recv, sh, lands, xl)
            conv_got = chip_exchange("gather_conv", [[wt[n][l]] for n in _CONV], gather=True)
            lw[l] = layer_weights(l, list(gather_finish("gather_finish", sh, landed)) + list(conv_got))
        xl, xlb, sv = _layer_fwd(cf, xl, xlb, p[l, 0].astype(BF16), lw[l][0], lw[l][1])
        saved.append(sv)
    grads = [None] * L
    dxl = None
    gsum = {}
    pending = None
    for l in reversed(range(L)):
        sm_l = lw[l][1]
        if pending is not None:
            sm_l = dict(sm_l, ple_norm_g=sm_l['ple_norm_g'] + pending[4][0, 0])
        if l == L - 1:
            dxl, grads[l] = _layer_bwd(cf, saved[l], lw[l][0], sm_l, target=loss_target[0], xn=xl)
        else:
            dxl, grads[l] = _layer_bwd(cf, saved[l], lw[l][0], sm_l, dxn=dxl)
        both = core_sums(grads[l])
        if l == L - 1 and L > 1:
            send, recv, both, lands, token = chip_legs_start(
                "scatter_start", 'scatter', both, [lax.empty(a.shape, a.dtype) for a in both])
            pending = (send, recv, both, lands, token)
            continue
        if pending is not None:
            send, recv, sent, lands, _ = pending
            landed = chip_legs_wait("scatter_wait", 'scatter', send, recv, sent, lands, dxl)
            gsum = chip_sums(L - 1, place_own("place_own", sent, landed), gsum)
            pending = None
        parts = chip_exchange("scatter_grads", [[t] for t in both], gather=False)
        gsum = chip_sums(l, [pr.reshape(4, pr.shape[2], pr.shape[3]) for pr in parts], gsum)
    loss = lax.psum(0.5 / D * jnp.sum(grads[L - 1]['loss_sq']), ("x", "y", "c"))
    grad_x = dxl[None]

    res = {}
    for n in split_names:
        shp = wt[n].shape
        flat = lambda a: a.reshape(shp[0] * shp[1], shp[2])
        outs = adamw_full("adamw_" + n, gsum[n], flat(wt[n]), flat(mo[n]), flat(vo[n]))
        res[n] = [o.reshape(shp) for o in [gsum[n]] + list(outs)]
    parts = chip_exchange("scatter_conv", [[blocks(n, grads[l]) for l in range(L)] for n in _CONV], gather=False)
    chip_sums = [sum_chips("chip_sum_" + n, pr.reshape(4, L * pr.shape[2], pr.shape[3])) for n, pr in zip(_CONV, parts)]
    sib_sums = sibling_swap("core_swap", chip_sums)
    for n, mine, sib in zip(_CONV, chip_sums, sib_sums):
        shp = wt[n].shape
        flat = lambda a: a.reshape(shp[0] * shp[1], shp[2])
        outs = adamw_shard("adamw_" + n, mine, sib, flat(wt[n]), flat(mo[n]), flat(vo[n]))
        res[n] = [o.reshape(shp) for o in outs]

    def small_pieces(l):
        gl = grads[l]
        A = -jnp.exp(a_log[l])
        d = dict(gl)
        d_alog = jnp.concatenate([gl['dA_f'].sum(axis=1)[:H] * A[0], gl['dA_r'].sum(axis=1)[:H] * A[1]])
        d['a_log'] = jnp.pad(d_alog[None], ((0, SUBLANES - 1), (0, 0)))
        d['dt_bias'] = gl['dt_bias'][:, :2 * H]
        d['d_skip'] = gl['dskip_full'].reshape(SUBLANES, H, HEAD_DIM).sum(axis=-1)
        return [_pad_lanes(d[n], _ceil_to(d[n].shape[1], LANES)) for n in _SMALL]

    widths = [_ceil_to(math.prod(wt[n].shape[1:]), LANES) for n in _SMALL]
    packed = jnp.concatenate([pc for l in range(L) for pc in small_pieces(l)], axis=1)
    gathered = all8_gather("gather_small", packed)

    def pack_params(src):
        return jnp.concatenate([_pad_lanes(src[n][l].reshape(1, -1), wd) for l in range(L) for n, wd in zip(_SMALL, widths)],
                               axis=1)

    small_out = adamw_small("adamw_small", gathered, pack_params(wt), pack_params(mo), pack_params(vo))
    off = 0
    per = {n: [[] for _ in range(4)] for n in _SMALL}
    for l in range(L):
        for n, wd in zip(_SMALL, widths):
            size = math.prod(wt[n].shape[1:])
            for k in range(4):
                per[n][k].append(small_out[k][0, off:off + size].reshape(wt[n].shape[1:]))
            off += wd
    for n in _SMALL:
        res[n] = [jnp.stack(per[n][k]) for k in range(4)]

    return (loss, grad_x, *[res[n][0] for n in _WEIGHTS], *[res[n][1] for n in _WEIGHTS],
            *[res[n][2] for n in _WEIGHTS], *[res[n][3] for n in _WEIGHTS])
```

```python
import math

import jax
import jax.numpy as jnp
from jax import lax
from jax.experimental import pallas as pl
from jax.experimental.pallas import tpu as pltpu

F32 = jnp.float32
BF16 = jnp.bfloat16

VMEM_LIMIT_BYTES = 56 * 1024 * 1024
LANES = 128
SUBLANES = 8

CHUNK = 128
D_STATE = 128
HEAD_DIM = 64
LN_EPS = 1e-5
RMS_EPS = 1e-6
ADAM_LR = 0.001
ADAM_B1 = 0.9
ADAM_B2 = 0.999
ADAM_EPS = 1e-08
ADAM_WD = 0.01
ADAM_STEP = 10
HALO = 16
MESH = pl.DeviceIdType.MESH


def _params(**kw):
    return pltpu.CompilerParams(vmem_limit_bytes=VMEM_LIMIT_BYTES, **kw)


def _sig(x):
    return jax.nn.sigmoid(x)


def _dsilu(x, s):
    return s * (1.0 + x * (1.0 - s))


def _ln_stats(r):
    mu = jnp.mean(r, axis=-1, keepdims=True)
    xc = r - mu
    var = jnp.mean(xc * xc, axis=-1, keepdims=True)
    rstd = lax.rsqrt(var + LN_EPS)
    return xc * rstd, rstd


def _ln_bwd(dy, xhat, rstd, g):
    dxh = dy * g
    m1 = jnp.mean(dxh, axis=-1, keepdims=True)
    m2 = jnp.mean(dxh * xhat, axis=-1, keepdims=True)
    return rstd * (dxh - m1 - xhat * m2)


def _f32(v):
    return v if v.dtype == F32 else v.astype(F32)


def _rows8(v):
    tm, w = v.shape
    return v.reshape(tm // SUBLANES, SUBLANES, w).sum(axis=0)


def fused_mm(name, prods, extras, epi, row_outs, col_outs=(), *, M, tm, tn, nj=1, nk=1,
             passthrough=None, t_outs=()):
    np_ = len(prods)
    ne = len(extras)
    nro = len(row_outs)
    nco = len(col_outs)
    use_acc = nk > 1

    def body(*refs):
        a_refs = [refs[2 * p] for p in range(np_)]
        w_refs = [refs[2 * p + 1] for p in range(np_)]
        pos = 2 * np_
        e_refs = refs[pos:pos + ne]
        pos += ne
        if passthrough is not None:
            pos += 1
        ro_refs = refs[pos:pos + nro]
        pos += nro
        co_refs = refs[pos:pos + nco]
        pos += nco
        to_refs = refs[pos:pos + len(t_outs)]
        pos += len(t_outs)
        acc_ref = refs[pos] if use_acc else None
        i = pl.program_id(1)
        k = pl.program_id(2)

        def prod(p):
            a = a_refs[p][...]
            if a.dtype != BF16:
                a = a.astype(BF16)
            return jnp.dot(a, w_refs[p][...], preferred_element_type=F32)

        def finish(acc):
            res = epi(acc, [_f32(r[...]) for r in e_refs])
            rows, cols = res[0], res[1]
            for v, o in zip(rows, ro_refs):
                o[...] = v.astype(o.dtype)
            for v, o in zip(res[2] if len(res) > 2 else (), to_refs):
                o[...] = v.T.astype(o.dtype)
            for v, o in zip(cols, co_refs):
                v8 = _rows8(v)

                @pl.when(i == 0)
                def _():
                    o[...] = v8

                @pl.when(i > 0)
                def _():
                    o[...] += v8

        if not use_acc:
            acc = prod(0)
            for p in range(1, np_):
                acc = acc + prod(p)
            finish(acc)
        else:
            @pl.when(k == 0)
            def _():
                acc = None
                for p in range(np_):
                    acc = prod(p) if acc is None else acc + prod(p)
                acc_ref[...] = acc

            @pl.when(k > 0)
            def _():
                acc = None
                for p in range(np_):
                    if prods[p][3]:
                        acc = prod(p) if acc is None else acc + prod(p)
                acc_ref[...] += acc

            @pl.when(k == nk - 1)
            def _():
                finish(acc_ref[...])

    in_specs = []
    args = []
    for a, w, joff, ksplit in prods:
        K = a.shape[1]
        if ksplit:
            tk = K // nk
            in_specs.append(pl.BlockSpec((tm, tk), lambda j, i, k: (i, k)))
            in_specs.append(pl.BlockSpec((tk, tn), lambda j, i, k, joff=joff: (k, j + joff)))
        else:
            in_specs.append(pl.BlockSpec((tm, K), lambda j, i, k: (i, 0)))
            in_specs.append(pl.BlockSpec((K, tn), lambda j, i, k, joff=joff: (0, j + joff)))
        args += [a, w]
    for arr, kind, width, c0 in extras:
        if kind == 'row':
            in_specs.append(pl.BlockSpec((tm, width), lambda j, i, k, c0=c0: (i, c0 + j)))
        else:
            in_specs.append(pl.BlockSpec((arr.shape[0], width), lambda j, i, k, c0=c0: (0, c0 + j)))
        args.append(arr)
    aliases = {}
    if passthrough is not None:
        arr, oidx = passthrough
        in_specs.append(pl.BlockSpec(memory_space=pl.ANY))
        aliases = {len(args): oidx}
        args.append(arr)
    out_shape = []
    out_specs = []
    for n_total, dtype, width, c0 in row_outs:
        out_shape.append(jax.ShapeDtypeStruct((M, n_total), dtype))
        out_specs.append(pl.BlockSpec((tm, width), lambda j, i, k, c0=c0: (i, c0 + j)))
    for n_total, width, c0 in col_outs:
        out_shape.append(jax.ShapeDtypeStruct((SUBLANES, n_total), F32))
        out_specs.append(pl.BlockSpec((SUBLANES, width), lambda j, i, k, c0=c0: (0, c0 + j)))
    for n_total, dtype, width, c0 in t_outs:
        out_shape.append(jax.ShapeDtypeStruct((n_total, M), dtype))
        out_specs.append(pl.BlockSpec((width, tm), lambda j, i, k, c0=c0: (c0 + j, i)))
    scratch = [pltpu.VMEM((tm, tn), F32)] if use_acc else []
    return pl.pallas_call(
        body, name=name, grid=(nj, M // tm, nk), in_specs=in_specs, out_specs=out_specs,
        out_shape=out_shape, scratch_shapes=scratch, input_output_aliases=aliases,
        compiler_params=_params(dimension_semantics=("arbitrary", "arbitrary", "arbitrary")),
    )(*args)


def mm_tn(name, a, b, *, tm, tk, tn):
    M, K = a.shape
    N = b.shape[1]

    def body(a_ref, b_ref, o_ref):
        m = pl.program_id(2)
        p = lax.dot_general(a_ref[...], b_ref[...], (((0,), (0,)), ((), ())),
                            preferred_element_type=F32)

        @pl.when(m == 0)
        def _():
            o_ref[...] = p

        @pl.when(m > 0)
        def _():
            o_ref[...] += p

    return pl.pallas_call(
        body, name=name, grid=(K // tk, N // tn, M // tm),
        in_specs=[pl.BlockSpec((tm, tk), lambda kk, j, m: (m, kk)),
                  pl.BlockSpec((tm, tn), lambda kk, j, m: (m, j))],
        out_specs=pl.BlockSpec((tk, tn), lambda kk, j, m: (kk, j)),
        out_shape=jax.ShapeDtypeStruct((K, N), F32),
        compiler_params=_params(dimension_semantics=("arbitrary", "arbitrary", "arbitrary")),
    )(a, b)


def row_call(name, fn, ins, row_outs, col_outs=(), *, M, tm, nc=1):
    ni = len(ins)
    nro = len(row_outs)

    def body(*refs):
        i = pl.program_id(1)
        vals = [_f32(r[...]) for r in refs[:ni]]
        rows, cols = fn(*vals)
        for v, o in zip(rows, refs[ni:ni + nro]):
            o[...] = v.astype(o.dtype)
        for v, o in zip(cols, refs[ni + nro:]):
            v8 = _rows8(v)

            @pl.when(i == 0)
            def _():
                o[...] = v8

            @pl.when(i > 0)
            def _():
                o[...] += v8

    in_specs = []
    for arr, kind, width, c0, cmul in ins:
        if kind == 'row':
            in_specs.append(pl.BlockSpec((tm, width), lambda cj, i, c0=c0, cmul=cmul: (i, c0 + cmul * cj)))
        else:
            in_specs.append(pl.BlockSpec((arr.shape[0], width), lambda cj, i, c0=c0, cmul=cmul: (0, c0 + cmul * cj)))
    out_shape = []
    out_specs = []
    for n_total, dtype, width, c0, cmul in row_outs:
        out_shape.append(jax.ShapeDtypeStruct((M, n_total), dtype))
        out_specs.append(pl.BlockSpec((tm, width), lambda cj, i, c0=c0, cmul=cmul: (i, c0 + cmul * cj)))
    for n_total, width, c0, cmul in col_outs:
        out_shape.append(jax.ShapeDtypeStruct((SUBLANES, n_total), F32))
        out_specs.append(pl.BlockSpec((SUBLANES, width), lambda cj, i, c0=c0, cmul=cmul: (0, c0 + cmul * cj)))
    return pl.pallas_call(
        body, name=name, grid=(nc, M // tm), in_specs=in_specs, out_specs=out_specs,
        out_shape=out_shape,
        compiler_params=_params(dimension_semantics=("arbitrary", "arbitrary")),
    )(*[a[0] for a in ins])


def conv_call(name, src, src_c0, w, K, epi, extras, row_outs, col_outs=(), *, M, tm, cw, nc,
              reverse, xin=None, passthrough=None, t_outs=(), w_c0=0):
    pad = (K - 1) // 2
    assert pad <= HALO - 1
    R = tm // HALO
    nblk = M // HALO
    n_i = M // tm
    Kp = w.shape[0]
    ne = len(extras)
    nro = len(row_outs)
    nco = len(col_outs)
    rb = 64
    cbw = min(cw, 256)
    n_copies = SUBLANES if K > SUBLANES else 1

    def body(*refs):
        main_ref, prev_ref, next_ref, w_ref = refs[:4]
        pos = 4
        xin_ref = None
        if xin is not None:
            xin_ref = refs[pos]
            pos += 1
        e_refs = refs[pos:pos + ne]
        pos += ne
        if passthrough is not None:
            pos += 1
        ro_refs = refs[pos:pos + nro]
        pos += nro
        co_refs = refs[pos:pos + nco]
        pos += nco
        to_refs = refs[pos:pos + len(t_outs)]
        pos += len(t_outs)
        dw_ref = None
        if xin is not None:
            dw_ref = refs[pos]
            pos += 1
        ext_ref, conv_ref = refs[pos], refs[pos + 1]
        i = pl.program_id(1)

        ext_ref[0, 0:HALO, :] = jnp.where(i == 0, 0.0, prev_ref[...].astype(F32))
        ext_ref[0, HALO:HALO + tm, :] = main_ref[...].astype(F32)
        ext_ref[0, HALO + tm:, :] = jnp.where(i == n_i - 1, 0.0, next_ref[...].astype(F32))
        if dw_ref is not None:
            @pl.when(i == 0)
            def _():
                dw_ref[...] = jnp.zeros_like(dw_ref)

        n_sh = tm + 2 * HALO - SUBLANES
        for c0 in range(0, cw, cbw):
            for sft in range(1, n_copies):
                ext_ref[sft, 0:n_sh, c0:c0 + cbw] = ext_ref[0, sft:sft + n_sh, c0:c0 + cbw]

        for c0 in range(0, cw, cbw):
            for r0 in range(0, tm, rb):
                acc = jnp.zeros((rb, cbw), F32)
                if xin_ref is not None:
                    xblk = xin_ref[r0:r0 + rb, c0:c0 + cbw].astype(F32)
                for k in range(K):
                    off = HALO + r0 + ((pad - k) if reverse else (k - pad))
                    sft = off % SUBLANES if n_copies > 1 else 0
                    d = ext_ref[sft, off - sft:off - sft + rb, c0:c0 + cbw]
                    acc = acc + d * w_ref[k:k + 1, c0:c0 + cbw]
                    if xin_ref is not None:
                        dw_ref[k, :, c0:c0 + cbw] += _rows8(xblk * d)
                conv_ref[r0:r0 + rb, c0:c0 + cbw] = acc

        res = epi(conv_ref[...], [_f32(r[...]) for r in e_refs])
        rows, cols = res[0], res[1]
        for v, o in zip(rows, ro_refs):
            o[...] = v.astype(o.dtype)
        for v, o in zip(res[2] if len(res) > 2 else (), to_refs):
            o[...] = v.T.astype(o.dtype)
        for v, o in zip(cols, co_refs):
            v8 = _rows8(v)

            @pl.when(i == 0)
            def _():
                o[...] = v8

            @pl.when(i > 0)
            def _():
                o[...] += v8

    in_specs = [
        pl.BlockSpec((tm, cw), lambda cj, i: (i, src_c0 + cj)),
        pl.BlockSpec((HALO, cw), lambda cj, i: (jnp.maximum(i * R - 1, 0), src_c0 + cj)),
        pl.BlockSpec((HALO, cw), lambda cj, i: (jnp.minimum((i + 1) * R, nblk - 1), src_c0 + cj)),
        pl.BlockSpec((Kp, cw), lambda cj, i: (0, w_c0 + cj)),
    ]
    args = [src, src, src, w]
    if xin is not None:
        in_specs.append(pl.BlockSpec((tm, cw), lambda cj, i, c0=xin[1]: (i, c0 + cj)))
        args.append(xin[0])
    for arr, kind, width, c0, cmul in extras:
        if kind == 'row':
            in_specs.append(pl.BlockSpec((tm, width), lambda cj, i, c0=c0, cmul=cmul: (i, c0 + cmul * cj)))
        else:
            in_specs.append(pl.BlockSpec((arr.shape[0], width), lambda cj, i, c0=c0, cmul=cmul: (0, c0 + cmul * cj)))
        args.append(arr)
    aliases = {}
    if passthrough is not None:
        in_specs.append(pl.BlockSpec(memory_space=pl.ANY))
        aliases = {len(args): passthrough[1]}
        args.append(passthrough[0])
    out_shape = []
    out_specs = []
    for n_total, dtype, width, c0, cmul in row_outs:
        out_shape.append(jax.ShapeDtypeStruct((M, n_total), dtype))
        out_specs.append(pl.BlockSpec((tm, width), lambda cj, i, c0=c0, cmul=cmul: (i, c0 + cmul * cj)))
    for n_total, width, c0, cmul in col_outs:
        out_shape.append(jax.ShapeDtypeStruct((SUBLANES, n_total), F32))
        out_specs.append(pl.BlockSpec((SUBLANES, width), lambda cj, i, c0=c0, cmul=cmul: (0, c0 + cmul * cj)))
    for n_total, dtype, width, c0, cmul in t_outs:
        out_shape.append(jax.ShapeDtypeStruct((n_total, M), dtype))
        out_specs.append(pl.BlockSpec((width, tm), lambda cj, i, c0=c0, cmul=cmul: (c0 + cmul * cj, i)))
    if xin is not None:
        out_shape.append(jax.ShapeDtypeStruct((Kp, SUBLANES, cw * nc), F32))
        out_specs.append(pl.BlockSpec((Kp, SUBLANES, cw), lambda cj, i: (0, 0, cj)))
    return pl.pallas_call(
        body, name=name, grid=(nc, n_i), in_specs=in_specs, out_specs=out_specs,
        out_shape=out_shape, input_output_aliases=aliases,
        scratch_shapes=[pltpu.VMEM((n_copies, tm + 2 * HALO, cw), F32), pltpu.VMEM((tm, cw), F32)],
        compiler_params=_params(dimension_semantics=("arbitrary", "arbitrary")),
    )(*args)


def _split_dot(m_bf16, v, n_pass, dims=None):
    out = None
    rest = v
    for p in range(n_pass):
        piece = rest.astype(BF16)
        if p + 1 < n_pass:
            rest = rest - piece.astype(F32)
        if dims is None:
            t = jnp.dot(m_bf16, piece, preferred_element_type=F32)
        else:
            t = lax.dot_general(m_bf16, piece, dims, preferred_element_type=F32)
        out = t if out is None else out + t
    return out


def _split_dot_r(v, m_bf16, n_pass):
    out = None
    rest = v
    for p in range(n_pass):
        piece = rest.astype(BF16)
        if p + 1 < n_pass:
            rest = rest - piece.astype(F32)
        t = jnp.dot(piece, m_bf16, preferred_element_type=F32)
        out = t if out is None else out + t
    return out


def _softplus(x):
    return jnp.maximum(x, 0.0) + jnp.log1p(jnp.exp(-jnp.abs(x)))


NT_DIMS = (((1,), (1,)), ((), ()))
TN_DIMS = (((0,), (0,)), ((), ()))


def _ssd_common(dtraw, dtbT, alogT, rev, n_heads):
    L = CHUNK
    if rev:
        dtraw = pltpu.roll(dtraw, LANES - n_heads, 1)
    preT = dtraw.T + dtbT
    dtT = _softplus(preT)
    AT = -jnp.exp(alogT)
    aT = dtT * AT
    ri = lax.broadcasted_iota(jnp.int32, (L, L), 0)
    ci = lax.broadcasted_iota(jnp.int32, (L, L), 1)
    up = (ri >= ci) if rev else (ri <= ci)
    lo = (ri <= ci) if rev else (ri >= ci)
    csT = _split_dot_r(aT, up.astype(BF16), 3)
    last = 0 if rev else L - 1
    lastB = jnp.broadcast_to(csT[:, last:last + 1], (L, L))
    return dict(preT=preT, dtT=dtT, AT=AT, csT=csT, cs=csT.T, up=up, lo=lo, ci=ci, last=last,
                doutT=jnp.exp(csT), dstT=jnp.exp(lastB - csT), totB=jnp.exp(lastB))


def ssd_fwd(name, xsT, bc, dtraw, dtbT, alogT, *, S, DI, G, H, rev, tail=None):
    NC = S // CHUNK
    R = H // G
    GW = R * HEAD_DIM
    N = D_STATE
    P = HEAD_DIM

    def body(*refs):
        xsT_ref, bc_ref, dtraw_ref, dtb_ref, alog_ref = refs[:5]
        if tail is None:
            y_ref, st_ref, h_ref = refs[5:]
        else:
            yo_ref, z_ref, xs_ref, dsk_ref, ng_ref = refs[5:10]
            y_ref, st_ref, yn_ref, h_ref = refs[10:]
        c = pl.program_id(0)

        @pl.when(c == 0)
        def _():
            h_ref[...] = jnp.zeros_like(h_ref)

        q = _ssd_common(dtraw_ref[...], dtb_ref[...], alog_ref[...], rev, H)
        cs, csT, dtT, doutT, totB = q['cs'], q['csT'], q['dtT'], q['doutT'], q['totB']
        wstT = q['dstT'] * dtT
        for g in range(G):
            Bg = bc_ref[:, g * N:(g + 1) * N].astype(BF16)
            Cg = bc_ref[:, G * N + g * N:G * N + (g + 1) * N].astype(BF16)
            CBT = lax.dot_general(Bg, Cg, NT_DIMS, preferred_element_type=F32)
            HT = h_ref[g]
            yoffT = lax.dot_general(HT.astype(BF16), Cg, NT_DIMS, preferred_element_type=F32)
            xT = xsT_ref[g * GW:(g + 1) * GW, :]
            hs = [g * R + r for r in range(R)]
            blks = [slice(r * P, (r + 1) * P) for r in range(R)]
            segs = [jnp.where(q['up'], csT[h:h + 1, :] - cs[:, h:h + 1], -1e30) for h in hs]
            GTs = [(CBT * jnp.exp(sg)).astype(BF16) for sg in segs]
            xThs = [xT[b, :] for b in blks]
            XThs = [(xTh * dtT[h:h + 1, :]).astype(BF16) for xTh, h in zip(xThs, hs)]
            ydTs = [jnp.dot(a, GT, preferred_element_type=F32) for a, GT in zip(XThs, GTs)]
            ys = [ydT + yoffT[b, :] * doutT[h:h + 1, :] for ydT, b, h in zip(ydTs, blks, hs)]
            xws = [xTh * wstT[h:h + 1, :] for xTh, h in zip(xThs, hs)]
            tots = [jnp.broadcast_to(totB[h:h + 1, :], (P, N)) for h in hs]
            y_ref[:, g * GW:(g + 1) * GW] = jnp.concatenate(ys, axis=0).T
            xwT = jnp.concatenate(xws, axis=0).astype(BF16)
            ST = jnp.dot(xwT, Bg, preferred_element_type=F32)
            st_ref[0, g] = HT
            h_ref[g] = HT * jnp.concatenate(tots, axis=0) + ST
        if tail is not None:
            y = y_ref[...] + yo_ref[...]
            y_ref[...] = y
            z = _f32(z_ref[...])
            yz = (y + xs_ref[...] * dsk_ref[...]) * (z * _sig(z))
            for g in range(G):
                t = yz[:, g * GW:(g + 1) * GW]
                tn = t * lax.rsqrt(jnp.mean(t * t, axis=-1, keepdims=True) + RMS_EPS)
                yn_ref[:, g * GW:(g + 1) * GW] = (tn * ng_ref[:, g * GW:(g + 1) * GW]).astype(BF16)

    cidx = (lambda c: NC - 1 - c) if rev else (lambda c: c)
    cmap = lambda c: (cidx(c), 0)
    smap = lambda c: (cidx(c), 0, 0, 0)
    const = lambda c: (0, 0)
    tmap = lambda c: (0, cidx(c))
    in_specs = [pl.BlockSpec((DI, CHUNK), tmap), pl.BlockSpec((CHUNK, 2 * G * N), cmap), pl.BlockSpec((CHUNK, LANES), cmap),
                pl.BlockSpec((LANES, LANES), const), pl.BlockSpec((LANES, LANES), const)]
    out_specs = [pl.BlockSpec((CHUNK, DI), cmap), pl.BlockSpec((1, G, GW, N), smap)]
    out_shape = [jax.ShapeDtypeStruct((S, DI), F32), jax.ShapeDtypeStruct((NC, G, GW, N), F32)]
    args = [xsT, bc, dtraw, dtbT, alogT]
    if tail is not None:
        y_other, (z_arr, z_blk), xs_row, dsk, ng = tail
        in_specs += [pl.BlockSpec((CHUNK, DI), cmap), pl.BlockSpec((CHUNK, DI), lambda c: (cidx(c), z_blk)),
                     pl.BlockSpec((CHUNK, DI), cmap), pl.BlockSpec((1, DI), const), pl.BlockSpec((1, DI), const)]
        out_specs.append(pl.BlockSpec((CHUNK, DI), cmap))
        out_shape.append(jax.ShapeDtypeStruct((S, DI), BF16))
        args += [y_other, z_arr, xs_row, dsk, ng]
    return pl.pallas_call(
        body, name=name, grid=(NC,), in_specs=in_specs, out_specs=out_specs, out_shape=out_shape,
        scratch_shapes=[pltpu.VMEM((G, GW, N), F32)],
        compiler_params=_params(dimension_semantics=("arbitrary",)),
    )(*args)


def ssd_bwd(name, xsT, bc, dtraw, dyT, st, dtbT, alogT, *, S, DI, G, H, rev, tail=None):
    NC = S // CHUNK
    R = H // G
    GW = R * HEAD_DIM
    N = D_STATE
    XBC = DI + 2 * G * N
    P = HEAD_DIM
    L = CHUNK

    def body(*refs):
        xsT_ref, bc_ref, dtraw_ref, dyT_ref, st_ref, dtb_ref, alog_ref = refs[:7]
        if tail is None:
            dxbc_ref, ddt_ref, da_ref, dh_ref, dcst_ref, p2t_ref, p3t_ref, e2t_ref = refs[7:]
        else:
            other_ref, cbx_ref, cbbc_ref, dskT_ref = refs[7:11]
            dxbc_ref, ddt_ref, da_ref, dcol_ref, dh_ref, dcst_ref, p2t_ref, p3t_ref, e2t_ref = refs[11:]
        c = pl.program_id(0)

        @pl.when(c == 0)
        def _():
            dh_ref[...] = jnp.zeros_like(dh_ref)
            da_ref[...] = jnp.zeros_like(da_ref)
            dcst_ref[...] = jnp.zeros_like(dcst_ref)
            p2t_ref[...] = jnp.zeros_like(p2t_ref)
            p3t_ref[...] = jnp.zeros_like(p3t_ref)
            e2t_ref[...] = jnp.zeros_like(e2t_ref)

        q = _ssd_common(dtraw_ref[...], dtb_ref[...], alog_ref[...], rev, H)
        cs, csT, dtT, doutT, dstT, totB = q['cs'], q['csT'], q['dtT'], q['doutT'], q['dstT'], q['totB']
        wstT = dstT * dtT
        lane = q['ci']
        for g in range(G):
            Bg = bc_ref[:, g * N:(g + 1) * N].astype(BF16)
            Cg = bc_ref[:, G * N + g * N:G * N + (g + 1) * N].astype(BF16)
            CB = lax.dot_general(Cg, Bg, NT_DIMS, preferred_element_type=F32)
            HpT = st_ref[0, g]
            HpTb = HpT.astype(BF16)
            dHT = dh_ref[g]
            dHTb = dHT.astype(BF16)
            BdHT = lax.dot_general(dHTb, Bg, NT_DIMS, preferred_element_type=F32)
            yoffT = lax.dot_general(HpTb, Cg, NT_DIMS, preferred_element_type=F32)
            xT = xsT_ref[g * GW:(g + 1) * GW, :]
            dyT = dyT_ref[g * GW:(g + 1) * GW, :]
            hs = [g * R + r for r in range(R)]
            blks = [slice(r * P, (r + 1) * P) for r in range(R)]
            Lms = [jnp.exp(jnp.where(q['lo'], cs[:, h:h + 1] - csT[h:h + 1, :], -1e30)) for h in hs]
            xThs = [xT[b, :] for b in blks]
            dyThs = [dyT[b, :] for b in blks]
            xThbs = [v.astype(BF16) for v in xThs]
            dyThbs = [v.astype(BF16) for v in dyThs]
            dGxs = [lax.dot_general(a, b, TN_DIMS, preferred_element_type=F32) for a, b in zip(dyThbs, xThbs)]
            Gms = [(CB * Lm).astype(BF16) for Lm in Lms]
            XThbs = [(xTh * dtT[h:h + 1, :]).astype(BF16) for xTh, h in zip(xThs, hs)]
            u1Ts = [jnp.dot(a, Gm, preferred_element_type=F32) for a, Gm in zip(dyThbs, Gms)]
            ydTs = [lax.dot_general(a, Gm, NT_DIMS, preferred_element_type=F32) for a, Gm in zip(XThbs, Gms)]
            Ts = [dGx * (Lm * dtT[h:h + 1, :]) for dGx, Lm, h in zip(dGxs, Lms, hs)]
            dCB = Ts[0]
            for T in Ts[1:]:
                dCB = dCB + T
            uTs = [u1T + BdHT[b, :] * dstT[h:h + 1, :] for u1T, b, h in zip(u1Ts, blks, hs)]
            dyds = [dyTh * doutT[h:h + 1, :] for dyTh, h in zip(dyThs, hs)]
            xws = [xTh * wstT[h:h + 1, :] for xTh, h in zip(xThs, hs)]
            for r, h in enumerate(hs):
                b = blks[r]
                p3row = jnp.sum(xws[r] * BdHT[b, :], axis=0, keepdims=True)
                seg_row = jnp.sum(_f32(dyThbs[r]) * ydTs[r], axis=0, keepdims=True)
                seg_col = jnp.sum(_f32(XThbs[r]) * u1Ts[r], axis=0, keepdims=True)
                dcst_ref[h:h + 1, :] = (jnp.sum(dyds[r] * yoffT[b, :], axis=0, keepdims=True)
                                        + seg_row - seg_col - p3row)
                p2t_ref[h:h + 1, :] = jnp.sum(xThs[r] * uTs[r], axis=0, keepdims=True)
                p3t_ref[h:h + 1, :] = p3row
                e2t_ref[h:h + 1, :] = jnp.sum(HpT[b, :] * dHT[b, :], axis=0, keepdims=True)
            dxs = [uT * dtT[h:h + 1, :] for uT, h in zip(uTs, hs)]
            if tail is not None:
                dxs = [d + dyTh * dskT_ref[g * GW + r * P:g * GW + (r + 1) * P, :]
                       for r, (d, dyTh) in enumerate(zip(dxs, dyThs))]
            tots = [jnp.broadcast_to(totB[h:h + 1, :], (P, N)) for h in hs]
            dxbc_ref[:, g * GW:(g + 1) * GW] = jnp.concatenate(dxs, axis=0).T
            dydT = jnp.concatenate(dyds, axis=0).astype(BF16)
            xwT = jnp.concatenate(xws, axis=0).astype(BF16)
            dCBb = dCB.astype(BF16)
            dC = (jnp.dot(dCBb, Bg, preferred_element_type=F32)
                  + lax.dot_general(dydT, HpTb, TN_DIMS, preferred_element_type=F32))
            dB = (lax.dot_general(dCBb, Cg, TN_DIMS, preferred_element_type=F32)
                  + lax.dot_general(xwT, dHTb, TN_DIMS, preferred_element_type=F32))
            dxbc_ref[:, DI + g * N:DI + (g + 1) * N] = dB
            dxbc_ref[:, DI + G * N + g * N:DI + G * N + (g + 1) * N] = dC
            dh_ref[g] = (dHT * jnp.concatenate(tots, axis=0)
                         + jnp.dot(dydT, Cg, preferred_element_type=F32))
        e1 = jnp.sum(p3t_ref[...], axis=1, keepdims=True)
        e2 = jnp.sum(e2t_ref[...], axis=1, keepdims=True)
        dcsT = dcst_ref[...] + jnp.where(lane == q['last'], e1 + totB * e2, 0.0)
        daT = _split_dot_r(dcsT, q['lo'].astype(BF16), 3)
        ddtT = daT * q['AT'] + p2t_ref[...]
        da_ref[...] += daT * dtT
        ddraw = jnp.where(lane < H, (ddtT * _sig(q['preT'])).T, 0.0)
        if rev:
            ddraw = pltpu.roll(ddraw, H, 1)
        ddt_ref[...] = ddraw
        if tail is not None:
            for c0, cb_ref in ((0, cbx_ref), (DI, cbbc_ref)):
                d = dxbc_ref[:, c0:c0 + DI] + other_ref[:, c0:c0 + DI]
                cb = cb_ref[...]
                dcb = d * _dsilu(cb, _sig(cb))
                dxbc_ref[:, c0:c0 + DI] = dcb
                part = _rows8(dcb)

                @pl.when(c == 0)
                def _():
                    dcol_ref[:, c0:c0 + DI] = part

                @pl.when(c > 0)
                def _():
                    dcol_ref[:, c0:c0 + DI] += part

    cmap = (lambda c: (c, 0)) if rev else (lambda c: (NC - 1 - c, 0))
    smap = (lambda c: (c, 0, 0, 0)) if rev else (lambda c: (NC - 1 - c, 0, 0, 0))
    const = lambda c: (0, 0)
    sq = pltpu.VMEM((LANES, CHUNK), F32)
    cix = (lambda c: c) if rev else (lambda c: NC - 1 - c)
    tmap = lambda c: (0, cix(c))
    in_specs = [pl.BlockSpec((DI, CHUNK), tmap), pl.BlockSpec((CHUNK, 2 * G * N), cmap), pl.BlockSpec((CHUNK, LANES), cmap),
                pl.BlockSpec((DI, CHUNK), tmap),
                pl.BlockSpec((1, G, GW, N), smap),
                pl.BlockSpec((LANES, LANES), const), pl.BlockSpec((LANES, LANES), const)]
    out_specs = [pl.BlockSpec((CHUNK, XBC), cmap), pl.BlockSpec((CHUNK, LANES), cmap),
                 pl.BlockSpec((LANES, LANES), const)]
    out_shape = [jax.ShapeDtypeStruct((S, XBC), F32), jax.ShapeDtypeStruct((S, LANES), F32),
                 jax.ShapeDtypeStruct((LANES, LANES), F32)]
    args = [xsT, bc, dtraw, dyT, st, dtbT, alogT]
    if tail is not None:
        in_specs += [pl.BlockSpec((CHUNK, XBC), cmap), pl.BlockSpec((CHUNK, DI), cmap),
                     pl.BlockSpec((CHUNK, 2 * G * N), cmap), pl.BlockSpec((DI, LANES), const)]
        out_specs.append(pl.BlockSpec((SUBLANES, XBC), const))
        out_shape.append(jax.ShapeDtypeStruct((SUBLANES, XBC), F32))
        args += list(tail)
    return pl.pallas_call(
        body, name=name, grid=(NC,), in_specs=in_specs, out_specs=out_specs, out_shape=out_shape,
        scratch_shapes=[pltpu.VMEM((G, GW, N), F32), sq, sq, sq, sq],
        compiler_params=_params(dimension_semantics=("arbitrary",)),
    )(*args)


ANY = pl.BlockSpec(memory_space=pl.ANY)


def chip_exchange(name, groups, gather):
    flat = [arr for grp in groups for arr in grp]
    n_in = len(flat)
    n_out = len(groups)
    n_rc = 3 * n_in

    def body(*refs):
        in_refs = refs[:n_in]
        out_refs = refs[n_in:n_in + n_out]
        send, recv = refs[n_in + n_out:]
        x, y, c = lax.axis_index("x"), lax.axis_index("y"), lax.axis_index("c")
        me = 2 * x + y
        peers = [(1 - x, y), (x, 1 - y), (1 - x, 1 - y)]
        remote = []
        q = 0
        for a, grp in enumerate(groups):
            for l in range(len(grp)):
                src = in_refs[q]
                dst = out_refs[a].at[me] if gather else out_refs[a].at[me, l]
                for j, (px, py) in enumerate(peers):
                    blk = src if gather else src.at[2 * px + py]
                    rc = pltpu.make_async_remote_copy(
                        src_ref=blk, dst_ref=dst, send_sem=send.at[3 * q + j], recv_sem=recv.at[3 * q + j],
                        device_id=(px, py, c), device_id_type=MESH)
                    rc.start()
                    remote.append(rc)
                q += 1
        for rc in remote:
            rc.wait()

    out_shape = []
    for grp in groups:
        a0 = grp[0]
        if gather:
            out_shape.append(jax.ShapeDtypeStruct((4,) + a0.shape, a0.dtype))
        else:
            out_shape.append(jax.ShapeDtypeStruct((4, len(grp)) + a0.shape[1:], a0.dtype))
    outs = pl.pallas_call(
        body, name=name, in_specs=[ANY] * n_in, out_specs=[ANY] * n_out, out_shape=out_shape,
        scratch_shapes=[pltpu.SemaphoreType.DMA((n_rc,)), pltpu.SemaphoreType.DMA((n_rc,))],
    )(*flat)
    me = _chip_index()
    res = []
    for grp, o in zip(groups, outs):
        for l, src in enumerate(grp):
            o = _put_block(o, src, (me,)) if gather else _put_block(o, _take_block(src, me), (me, l))
        res.append(o)
    return res


def _chip_index():
    return 2 * lax.axis_index("x") + lax.axis_index("y")


def _take_block(arr, idx):
    return lax.dynamic_index_in_dim(arr, idx, 0, keepdims=False)


def _put_block(dst, blk, idx):
    lead = len(idx)
    return lax.dynamic_update_slice(dst, blk.reshape((1,) * lead + blk.shape), tuple(idx) + (0,) * (dst.ndim - lead))


def gather_layer(name, split, whole):
    ns, nw = len(split), len(whole)
    n = ns + nw
    n_rc = 3 * (n + ns)

    def body(*refs):
        in_refs = refs[:n]
        out_refs = refs[n:2 * n]
        send, recv = refs[2 * n:]
        x, y, c = lax.axis_index("x"), lax.axis_index("y"), lax.axis_index("c")
        me = 2 * x + y
        sibling = (x, y, 1 - c)
        peers = [(1 - x, y), (x, 1 - y), (1 - x, 1 - y)]

        def region(a, chip, half):
            if a >= ns:
                return out_refs[a].at[chip]
            hr = split[a].shape[0] // 2
            return out_refs[a].at[chip, pl.ds(half * hr, hr)]

        def mine(a):
            if a >= ns:
                return in_refs[a]
            hr = split[a].shape[0] // 2
            return in_refs[a].at[pl.ds(c * hr, hr)]

        sends = []
        for a in range(n):
            for j, (px, py) in enumerate(peers):
                rc = pltpu.make_async_remote_copy(
                    src_ref=mine(a), dst_ref=region(a, me, c), send_sem=send.at[3 * a + j],
                    recv_sem=recv.at[3 * a + j], device_id=(px, py, c), device_id_type=MESH)
                rc.start()
                sends.append(rc)
        for a in range(n):
            for j, (px, py) in enumerate(peers):
                chip = 2 * px + py
                landed = pltpu.make_async_remote_copy(
                    src_ref=mine(a), dst_ref=region(a, chip, c), send_sem=send.at[3 * a + j],
                    recv_sem=recv.at[3 * a + j], device_id=(px, py, c), device_id_type=MESH)
                landed.wait_recv()
                if a < ns:
                    fw = pltpu.make_async_remote_copy(
                        src_ref=region(a, chip, c), dst_ref=region(a, chip, c), send_sem=send.at[3 * n + 3 * a + j],
                        recv_sem=recv.at[3 * n + 3 * a + j], device_id=sibling, device_id_type=MESH)
                    fw.start()
                    sends.append(fw)
        for a in range(ns):
            for j, (px, py) in enumerate(peers):
                chip = 2 * px + py
                pltpu.make_async_remote_copy(
                    src_ref=region(a, chip, 1 - c), dst_ref=region(a, chip, 1 - c), send_sem=send.at[3 * n + 3 * a + j],
                    recv_sem=recv.at[3 * n + 3 * a + j], device_id=sibling, device_id_type=MESH).wait_recv()
        for rc in sends:
            rc.wait_send()

    arrs = list(split) + list(whole)
    outs = pl.pallas_call(
        body, name=name, in_specs=[ANY] * n, out_specs=[ANY] * n,
        out_shape=[jax.ShapeDtypeStruct((4,) + a.shape, a.dtype) for a in arrs],
        scratch_shapes=[pltpu.SemaphoreType.DMA((n_rc,)), pltpu.SemaphoreType.DMA((n_rc,))],
    )(*arrs)
    me = _chip_index()
    return [_put_block(o, a, (me,)) for o, a in zip(outs, arrs)]


HBM_SPEC = pl.BlockSpec(memory_space=pltpu.HBM)
SEM_SPEC = pl.BlockSpec(memory_space=pltpu.SEMAPHORE)
IN_FLIGHT = pltpu.SideEffectType.DATAFLOW_SIDE_EFFECTING


def _chip_leg(kind, a_ref, l_ref, shape, c, me, chip):
    if kind == 'gather':
        hr = shape[0] // 2
        rows = pl.ds(c * hr, hr)
        return a_ref.at[rows], l_ref.at[me, rows], l_ref.at[chip, rows]
    return a_ref.at[chip], l_ref.at[me], l_ref.at[chip]


def chip_legs_start(name, kind, arrs, lands):
    n = len(arrs)

    def body(*refs):
        a_refs = refs[:n]
        l_refs = refs[n:2 * n]
        send, recv = refs[2 * n], refs[2 * n + 1]
        token = refs[-1]
        x, y, c = lax.axis_index("x"), lax.axis_index("y"), lax.axis_index("c")
        me = 2 * x + y
        for a in range(n):
            for j, (px, py) in enumerate([(1 - x, y), (x, 1 - y), (1 - x, 1 - y)]):
                src, dst, _ = _chip_leg(kind, a_refs[a], l_refs[a], arrs[a].shape, c, me, 2 * px + py)
                pltpu.make_async_remote_copy(src_ref=src, dst_ref=dst, send_sem=send.at[3 * a + j],
                                             recv_sem=recv.at[3 * a + j], device_id=(px, py, c),
                                             device_id_type=MESH).start()
        token[...] = jnp.zeros_like(token)

    both = list(arrs) + list(lands)
    outs = pl.pallas_call(
        body, name=name,
        out_shape=(pltpu.SemaphoreType.DMA((3 * n,)), pltpu.SemaphoreType.DMA((3 * n,)),
                   *[pltpu.HBM(a.shape, a.dtype) for a in both], jax.ShapeDtypeStruct((SUBLANES, LANES), F32)),
        in_specs=[HBM_SPEC] * (2 * n),
        out_specs=(SEM_SPEC, SEM_SPEC, *[HBM_SPEC] * (2 * n), pl.BlockSpec(memory_space=pltpu.VMEM)),
        input_output_aliases={i: 2 + i for i in range(2 * n)},
        compiler_params=pltpu.CompilerParams(has_side_effects=IN_FLIGHT),
    )(*[pltpu.with_memory_space_constraint(a, pltpu.HBM) for a in both])
    return outs[0], outs[1], list(outs[2:2 + n]), list(outs[2 + n:2 + 2 * n]), outs[-1]


def chip_legs_wait(name, kind, send, recv, arrs, lands, after):
    n = len(arrs)

    def body(*refs):
        a_refs = refs[:n]
        l_refs = refs[n:2 * n]
        send_, recv_ = refs[2 * n], refs[2 * n + 1]
        x, y, c = lax.axis_index("x"), lax.axis_index("y"), lax.axis_index("c")
        me = 2 * x + y
        legs = []
        for a in range(n):
            for j, (px, py) in enumerate([(1 - x, y), (x, 1 - y), (1 - x, 1 - y)]):
                src, dst, landing = _chip_leg(kind, a_refs[a], l_refs[a], arrs[a].shape, c, me, 2 * px + py)
                legs.append(pltpu.make_async_remote_copy(src_ref=src, dst_ref=landing, send_sem=send_.at[3 * a + j],
                                                         recv_sem=recv_.at[3 * a + j], device_id=(px, py, c),
                                                         device_id_type=MESH))
        for leg in legs:
            leg.wait_send()
        for leg in legs:
            leg.wait_recv()

    both = list(arrs) + list(lands)
    outs = pl.pallas_call(
        body, name=name, out_shape=tuple(pltpu.HBM(a.shape, a.dtype) for a in both),
        in_specs=[HBM_SPEC] * (2 * n) + [SEM_SPEC, SEM_SPEC, ANY], out_specs=tuple([HBM_SPEC] * (2 * n)),
        input_output_aliases={i: i for i in range(2 * n)},
        compiler_params=pltpu.CompilerParams(has_side_effects=IN_FLIGHT),
    )(*both, send, recv, after)
    return list(outs[n:])


def gather_finish(name, split, landed):
    n = len(split)

    def body(*refs):
        out_refs = refs[n:2 * n]
        send, recv = refs[2 * n:]
        x, y, c = lax.axis_index("x"), lax.axis_index("y"), lax.axis_index("c")
        sibling = (x, y, 1 - c)
        chips = [2 * (1 - x) + y, 2 * x + (1 - y), 2 * (1 - x) + (1 - y)]

        def region(a, chip, half):
            hr = split[a].shape[0] // 2
            return out_refs[a].at[chip, pl.ds(half * hr, hr)]

        sends = []
        for a in range(n):
            for j, chip in enumerate(chips):
                fw = pltpu.make_async_remote_copy(
                    src_ref=region(a, chip, c), dst_ref=region(a, chip, c), send_sem=send.at[3 * a + j],
                    recv_sem=recv.at[3 * a + j], device_id=sibling, device_id_type=MESH)
                fw.start()
                sends.append(fw)
        for a in range(n):
            for j, chip in enumerate(chips):
                pltpu.make_async_remote_copy(
                    src_ref=region(a, chip, 1 - c), dst_ref=region(a, chip, 1 - c), send_sem=send.at[3 * a + j],
                    recv_sem=recv.at[3 * a + j], device_id=sibling, device_id_type=MESH).wait_recv()
        for fw in sends:
            fw.wait_send()

    outs = pl.pallas_call(
        body, name=name, in_specs=[ANY] * n, out_specs=[ANY] * n,
        out_shape=[jax.ShapeDtypeStruct(a.shape, a.dtype) for a in landed],
        input_output_aliases={a: a for a in range(n)},
        scratch_shapes=[pltpu.SemaphoreType.DMA((3 * n,)), pltpu.SemaphoreType.DMA((3 * n,))],
    )(*landed)
    me = _chip_index()
    return [_put_block(o, a, (me,)) for o, a in zip(outs, split)]


def place_own(arrs, landed):
    me = _chip_index()
    return [_put_block(l, _take_block(a, me), (me,)) for a, l in zip(arrs, landed)]


def core_send_half(name, arrs):
    n = len(arrs)

    def body(*refs):
        in_refs = refs[:n]
        out_refs = refs[n:2 * n]
        send, recv = refs[2 * n:]
        c = lax.axis_index("c")
        peer = (lax.axis_index("x"), lax.axis_index("y"), 1 - c)
        rcs = []
        for a in range(n):
            hr = arrs[a].shape[1] // 2
            rc = pltpu.make_async_remote_copy(
                src_ref=in_refs[a].at[:, pl.ds((1 - c) * hr, hr)], dst_ref=out_refs[a], send_sem=send.at[a],
                recv_sem=recv.at[a], device_id=peer, device_id_type=MESH)
            rc.start()
            rcs.append(rc)
        for rc in rcs:
            rc.wait()

    return pl.pallas_call(
        body, name=name, in_specs=[ANY] * n, out_specs=[ANY] * n,
        out_shape=[jax.ShapeDtypeStruct((4, a.shape[1] // 2, a.shape[2]), a.dtype) for a in arrs],
        scratch_shapes=[pltpu.SemaphoreType.DMA((n,)), pltpu.SemaphoreType.DMA((n,))],
    )(*arrs)


def core_fill(name, arrs, layer, n_layers):
    n = len(arrs)

    def body(*refs):
        out_refs = refs[n:2 * n]
        send, recv = refs[2 * n:]
        c = lax.axis_index("c")
        peer = (lax.axis_index("x"), lax.axis_index("y"), 1 - c)
        rcs = []
        for a in range(n):
            r = arrs[a].shape[0] // n_layers
            hr = r // 2
            rows = out_refs[a].at[pl.ds(layer * r + c * hr, hr)]
            rc = pltpu.make_async_remote_copy(src_ref=rows, dst_ref=rows, send_sem=send.at[a], recv_sem=recv.at[a],
                                              device_id=peer, device_id_type=MESH)
            rc.start()
            rcs.append(rc)
        for a in range(n):
            r = arrs[a].shape[0] // n_layers
            hr = r // 2
            theirs = out_refs[a].at[pl.ds(layer * r + (1 - c) * hr, hr)]
            pltpu.make_async_remote_copy(src_ref=theirs, dst_ref=theirs, send_sem=send.at[a], recv_sem=recv.at[a],
                                         device_id=peer, device_id_type=MESH).wait_recv()
        for rc in rcs:
            rc.wait_send()

    return pl.pallas_call(
        body, name=name, in_specs=[ANY] * n, out_specs=[ANY] * n,
        out_shape=[jax.ShapeDtypeStruct(a.shape, a.dtype) for a in arrs],
        input_output_aliases={a: a for a in range(n)},
        scratch_shapes=[pltpu.SemaphoreType.DMA((n,)), pltpu.SemaphoreType.DMA((n,))],
    )(*arrs)


def sibling_swap(name, arrs):
    n = len(arrs)

    def body(*refs):
        in_refs = refs[:n]
        out_refs = refs[n:2 * n]
        send, recv = refs[2 * n:]
        peer = (lax.axis_index("x"), lax.axis_index("y"), 1 - lax.axis_index("c"))
        rcs = []
        for a in range(n):
            rc = pltpu.make_async_remote_copy(src_ref=in_refs[a], dst_ref=out_refs[a], send_sem=send.at[a],
                                              recv_sem=recv.at[a], device_id=peer, device_id_type=MESH)
            rc.start()
            rcs.append(rc)
        for rc in rcs:
            rc.wait()

    return pl.pallas_call(
        body, name=name, in_specs=[ANY] * n, out_specs=[ANY] * n,
        out_shape=[jax.ShapeDtypeStruct(a.shape, a.dtype) for a in arrs],
        scratch_shapes=[pltpu.SemaphoreType.DMA((n,)), pltpu.SemaphoreType.DMA((n,))],
    )(*arrs)


def all8_gather(name, v):
    flips = [(fx, fy, fc) for fx in (0, 1) for fy in (0, 1) for fc in (0, 1) if (fx, fy, fc) != (0, 0, 0)]

    def body(v_ref, out_ref, send, recv, loc):
        x, y, c = lax.axis_index("x"), lax.axis_index("y"), lax.axis_index("c")
        me = 4 * x + 2 * y + c
        lc = pltpu.make_async_copy(v_ref, out_ref.at[me], loc)
        lc.start()
        rcs = []
        for k, (fx, fy, fc) in enumerate(flips):
            tgt = (x + fx - 2 * x * fx, y + fy - 2 * y * fy, c + fc - 2 * c * fc)
            rc = pltpu.make_async_remote_copy(src_ref=v_ref, dst_ref=out_ref.at[me], send_sem=send.at[k],
                                              recv_sem=recv.at[k], device_id=tgt, device_id_type=MESH)
            rc.start()
            rcs.append(rc)
        lc.wait()
        for rc in rcs:
            rc.wait()

    return pl.pallas_call(
        body, name=name, in_specs=[ANY], out_specs=ANY,
        out_shape=jax.ShapeDtypeStruct((8,) + v.shape, v.dtype),
        scratch_shapes=[pltpu.SemaphoreType.DMA((7,)), pltpu.SemaphoreType.DMA((7,)), pltpu.SemaphoreType.DMA],
    )(v)


def _pick_rows(rows, cols, target_elems=128 * 1024, mult=SUBLANES):
    if rows % mult != 0:
        return rows
    best = mult
    t = mult
    while t <= rows:
        if rows % t == 0 and t * cols <= target_elems:
            best = t
        t += mult
    return best


def sum_chips(name, parts):
    _, R, C = parts.shape
    tm = _pick_rows(R, C)

    def body(p_ref, o_ref):
        o_ref[...] = (p_ref[0] + p_ref[1]) + (p_ref[2] + p_ref[3])

    return pl.pallas_call(
        body, name=name, grid=(R // tm,),
        in_specs=[pl.BlockSpec((4, tm, C), lambda i: (0, i, 0))],
        out_specs=pl.BlockSpec((tm, C), lambda i: (i, 0)),
        out_shape=jax.ShapeDtypeStruct((R, C), F32),
        compiler_params=_params(dimension_semantics=("arbitrary",)),
    )(parts)


def _adamw(g, w, m, v):
    m = ADAM_B1 * m + (1.0 - ADAM_B1) * g
    v = ADAM_B2 * v + (1.0 - ADAM_B2) * (g * g)
    m_hat = m / (1.0 - ADAM_B1 ** ADAM_STEP)
    v_hat = v / (1.0 - ADAM_B2 ** ADAM_STEP)
    delta = -ADAM_LR * (m_hat / (jnp.sqrt(v_hat) + ADAM_EPS) + ADAM_WD * w)
    return delta, m, v


def adamw_shard(name, s_mine, s_sib, w, m, v):
    R, C = w.shape
    tm = _pick_rows(R, C)

    def body(a_ref, b_ref, w_ref, m_ref, v_ref, g_out, d_out, m_out, v_out):
        g = a_ref[...] + b_ref[...]
        d, mn, vn = _adamw(g, w_ref[...], m_ref[...], v_ref[...])
        g_out[...] = g
        d_out[...] = d
        m_out[...] = mn
        v_out[...] = vn

    spec = pl.BlockSpec((tm, C), lambda i: (i, 0))
    return pl.pallas_call(
        body, name=name, grid=(R // tm,), in_specs=[spec] * 5, out_specs=[spec] * 4,
        out_shape=[jax.ShapeDtypeStruct((R, C), F32)] * 4,
        compiler_params=_params(dimension_semantics=("arbitrary",)),
    )(s_mine, s_sib, w, m, v)


def core_sum(name, core, g, got):
    _, r, C = g.shape
    hr = r // 2
    tm = _pick_rows(hr, 4 * C, 256 * 1024, 2 * SUBLANES)
    nh = hr // tm

    def body(c_ref, g_ref, s_ref, o_ref):
        o_ref[...] = (g_ref[...] + s_ref[...]).astype(BF16)

    return pl.pallas_call(
        body, name=name,
        grid_spec=pltpu.PrefetchScalarGridSpec(
            num_scalar_prefetch=1, grid=(nh,),
            in_specs=[pl.BlockSpec((4, tm, C), lambda i, cr: (0, cr[0] * nh + i, 0)),
                      pl.BlockSpec((4, tm, C), lambda i, cr: (0, i, 0))],
            out_specs=pl.BlockSpec((4, tm, C), lambda i, cr: (0, i, 0))),
        out_shape=jax.ShapeDtypeStruct((4, hr, C), BF16),
        compiler_params=_params(dimension_semantics=("arbitrary",)),
    )(core, g, got)


def chip_sum_into(name, core, parts, layer, n_layers, into=None):
    _, hr, C = parts.shape
    r = 2 * hr
    tm = _pick_rows(hr, 4 * C, 256 * 1024, 2 * SUBLANES)
    nh = hr // tm

    def body(c_ref, p_ref, *rest):
        o_ref = rest[-1]
        o_ref[...] = (_f32(p_ref[0]) + _f32(p_ref[1])) + (_f32(p_ref[2]) + _f32(p_ref[3]))

    in_specs = [pl.BlockSpec((4, tm, C), lambda i, cr: (0, i, 0))]
    args = [core, parts]
    aliases = {}
    if into is not None:
        in_specs.append(pl.BlockSpec(memory_space=pl.ANY))
        args.append(into)
        aliases = {2: 0}
    return pl.pallas_call(
        body, name=name,
        grid_spec=pltpu.PrefetchScalarGridSpec(
            num_scalar_prefetch=1, grid=(nh,), in_specs=in_specs,
            out_specs=pl.BlockSpec((tm, C), lambda i, cr: ((layer * r) // tm + cr[0] * nh + i, 0))),
        out_shape=jax.ShapeDtypeStruct((n_layers * r, C), F32), input_output_aliases=aliases,
        compiler_params=_params(dimension_semantics=("arbitrary",)),
    )(*args)


def adamw_full(name, g, w, m, v):
    R, C = w.shape
    tm = _pick_rows(R, C)

    def body(g_ref, w_ref, m_ref, v_ref, d_out, m_out, v_out):
        d, mn, vn = _adamw(g_ref[...], w_ref[...], m_ref[...], v_ref[...])
        d_out[...] = d
        m_out[...] = mn
        v_out[...] = vn

    spec = pl.BlockSpec((tm, C), lambda i: (i, 0))
    return pl.pallas_call(
        body, name=name, grid=(R // tm,), in_specs=[spec] * 4, out_specs=[spec] * 3,
        out_shape=[jax.ShapeDtypeStruct((R, C), F32)] * 3,
        compiler_params=_params(dimension_semantics=("arbitrary",)),
    )(g, w, m, v)


def adamw_small(name, parts, w, m, v):
    W = w.shape[1]

    def body(p_ref, w_ref, m_ref, v_ref, g_out, d_out, m_out, v_out):
        acc = p_ref[0]
        for k in range(1, 8):
            acc = acc + p_ref[k]
        g = jnp.sum(acc, axis=0, keepdims=True)
        d, mn, vn = _adamw(g, w_ref[...], m_ref[...], v_ref[...])
        g_out[...] = g
        d_out[...] = d
        m_out[...] = mn
        v_out[...] = vn

    return pl.pallas_call(
        body, name=name, out_shape=[jax.ShapeDtypeStruct((1, W), F32)] * 4,
        compiler_params=_params(),
    )(parts, w, m, v)


def _pad_lanes(v, width=LANES):
    return jnp.pad(v, ((0, 0), (0, width - v.shape[1])))


def _layer_fwd(cf, x, xb, pb, W, sm):
    S, D, CD, DI, XBC, F, H, G = cf['S'], cf['D'], cf['CD'], cf['DI'], cf['XBC'], cf['F'], cf['H'], cf['G']
    NM = cf['NM']
    alpha = cf['alpha']
    tm = cf['tm']
    tmx = cf['tmx']
    tn_in = cf['tn_in']
    sv = {}

    ident = lambda acc, ex: ([acc], [])
    proj, = fused_mm("in_proj", [(xb, W['in_main'], 0, False)], [], ident, [(NM, BF16, tn_in, 0)],
                     M=S, tm=tmx, tn=tn_in, nj=NM // tn_in)
    dtraw, = fused_mm("dt_proj", [(xb, W['in_dt'], 0, False)], [], ident, [(LANES, F32, LANES, 0)],
                      M=S, tm=tmx, tn=LANES)

    u, = row_call("glu", lambda a, gt: ([a * _sig(gt)], []),
                  [(proj, 'row', CD, 0, 0), (proj, 'row', CD, 1, 0)], [(CD, F32, CD, 0, 0)], M=S, tm=tm)

    def conv_a_epi(conv, ex):
        cb_, g_, b_ = ex
        ca = conv + cb_
        xhat, _ = _ln_stats(ca)
        la = xhat * g_ + b_
        return [ca, la * _sig(la)], []

    ca, sa = conv_call("conv_a", u, 0, sm['conv_a_w'], cf['KA'], conv_a_epi,
                       [(sm['conv_a_b'], 'vec', CD, 0, 0), (sm['ln_a_g'], 'vec', CD, 0, 0), (sm['ln_a_b'], 'vec', CD, 0, 0)],
                       [(CD, F32, CD, 0, 0), (CD, BF16, CD, 0, 0)], M=S, tm=cf['tmc'], cw=CD, nc=1, reverse=False)
    y_a, = fused_mm("a_out", [(sa, W['a_out'], 0, False)], [], ident, [(D, F32, D, 0)], M=S, tm=tmx, tn=D)

    def conv_x_epi(conv, ex):
        cb = conv + ex[0]
        act = cb * _sig(cb)
        return [cb, act], [], [act]

    def conv_bc_epi(conv, ex):
        cb = conv + ex[0]
        return [cb, cb * _sig(cb)], []

    xoff = (2 * CD + 2 * D + DI) // DI
    cbv_x, xs, xsT = conv_call("conv_b_x", proj, xoff, sm['ssm_conv_w'], cf['KB'], conv_x_epi,
                               [(sm['ssm_conv_b'], 'vec', DI, 0, 0)],
                               [(DI, F32, DI, 0, 0), (DI, F32, DI, 0, 0)], M=S, tm=cf['tmc'], cw=DI, nc=1,
                               reverse=False, t_outs=[(DI, F32, DI, 0, 0)])
    cbv_bc, bc = conv_call("conv_b_bc", proj, xoff + 1, sm['ssm_conv_w'], cf['KB'], conv_bc_epi,
                           [(sm['ssm_conv_b'], 'vec', DI, 1, 0)],
                           [(DI, F32, DI, 0, 0), (DI, F32, DI, 0, 0)], M=S, tm=cf['tmc'], cw=DI, nc=1,
                           reverse=False, w_c0=1)
    y_f, st_f = ssd_fwd("ssd_fwd_f", xsT, bc, dtraw, sm['dtb_f'], sm['alog_f'], S=S, DI=DI, G=G, H=H, rev=False)
    zoff = (2 * CD + 2 * D) // DI
    ysum, st_r, yn = ssd_fwd("ssd_fwd_r", xsT, bc, dtraw, sm['dtb_r'], sm['alog_r'], S=S, DI=DI, G=G, H=H, rev=True,
                             tail=(y_f, (proj, zoff), xs, sm['dskip_full'], sm['ssm_norm_g']))
    goff = (2 * CD) // D

    def merge_epi(acc, ex):
        ga, gb, ya = ex
        return [acc, _sig(ga) * ya + _sig(gb) * acc], []

    y_b, merged = fused_mm("b_out", [(yn, W['b_out'], 0, False)],
                           [(proj, 'row', D, goff), (proj, 'row', D, goff + 1), (y_a, 'row', D, 0)],
                           merge_epi, [(D, F32, D, 0), (D, BF16, D, 0)], M=S, tm=tm, tn=D)

    def mix_epi(acc, ex):
        xin, g_, b_ = ex
        r1 = alpha * xin + acc
        xhat, _ = _ln_stats(r1)
        return [r1, xhat * g_ + b_], []

    r1, hb = fused_mm("o_mix", [(merged, W['o'], 0, False)],
                      [(x, 'row', D, 0), (sm['ln1_g'], 'vec', D, 0), (sm['ln1_b'], 'vec', D, 0)],
                      mix_epi, [(D, F32, D, 0), (D, BF16, D, 0)], M=S, tm=tm, tn=D)

    tnf = cf['tnf']

    g32, g_ = fused_mm("ffn_gate", [(hb, W['gate_up'], 0, False)], [], lambda acc, ex: ([acc, acc], []),
                       [(F, F32, tnf, 0), (F, BF16, tnf, 0)], M=S, tm=tmx, tn=tnf, nj=F // tnf)
    u_, f = fused_mm("ffn_up", [(hb, W['gate_up'], F // tnf, False)], [(g32, 'row', tnf, 0)],
                     lambda acc, ex: ([acc, ex[0] * _sig(ex[0]) * acc], []),
                     [(F, BF16, tnf, 0), (F, BF16, tnf, 0)], M=S, tm=tmx, tn=tnf, nj=F // tnf)

    def down_epi(acc, ex):
        r1_, g1, b1, g2, b2 = ex
        xh1, _ = _ln_stats(r1_)
        r2 = alpha * (xh1 * g1 + b1) + acc
        xh2, _ = _ln_stats(r2)
        return [r2, xh2 * g2 + b2], []

    r2, h2b = fused_mm("ffn_down", [(f, W['down'], 0, False)],
                       [(r1, 'row', D, 0), (sm['ln1_g'], 'vec', D, 0), (sm['ln1_b'], 'vec', D, 0),
                        (sm['ln2_g'], 'vec', D, 0), (sm['ln2_b'], 'vec', D, 0)],
                       down_epi, [(D, F32, D, 0), (D, BF16, D, 0)], M=S, tm=tm, tn=D)

    pe, = fused_mm("ple_proj", [(pb, W['ple'], 0, False)], [], ident, [(D, F32, D, 0)], M=S, tm=tmx, tn=D)

    def ple_epi(acc, ex):
        r2_, g2, b2, pe_, pg = ex
        xh2, _ = _ln_stats(r2_)
        h2 = xh2 * g2 + b2
        e = pe_ * lax.rsqrt(jnp.mean(pe_ * pe_, axis=-1, keepdims=True) + RMS_EPS) * pg
        xn = h2 + e * _sig(acc)
        return [acc, xn, xn], []

    t_, xn, xnb = fused_mm("ple_gate", [(h2b, W['ple_gate'], 0, False)],
                           [(r2, 'row', D, 0), (sm['ln2_g'], 'vec', D, 0), (sm['ln2_b'], 'vec', D, 0),
                            (pe, 'row', D, 0), (sm['ple_norm_g'], 'vec', D, 0)],
                           ple_epi, [(D, F32, D, 0), (D, F32, D, 0), (D, BF16, D, 0)], M=S, tm=tm, tn=D)
    sv.update(x=x, xb=xb, pb=pb, proj=proj, dtraw=dtraw, u=u, ca=ca, sa=sa, y_a=y_a, cbv_x=cbv_x, cbv_bc=cbv_bc,
              xs=xs, xsT=xsT, bc=bc,
              ysum=ysum, st_f=st_f, st_r=st_r, yn=yn, y_b=y_b, merged=merged, r1=r1, hb=hb,
              g_=g_, u_=u_, f=f, r2=r2, h2b=h2b, t_=t_, pe=pe)
    return xn, xnb, sv


def _layer_bwd(cf, sv, W, sm, dxn=None, target=None, xn=None):
    S, D, CD, DI, XBC, F, H, G = cf['S'], cf['D'], cf['CD'], cf['DI'], cf['XBC'], cf['F'], cf['H'], cf['G']
    NM = cf['NM']
    alpha = cf['alpha']
    tm = cf['tm']
    gw = cf['GW']
    out = {}

    def ple_bwd_core(dx_, t, pe_, pg):
        s = _sig(t)
        rinv = lax.rsqrt(jnp.mean(pe_ * pe_, axis=-1, keepdims=True) + RMS_EPS)
        pn = pe_ * rinv
        e = pn * pg
        dtg = dx_ * e * (s * (1.0 - s))
        de = dx_ * s
        qv = de * pg
        dpe = rinv * (qv - pn * jnp.mean(qv * pn, axis=-1, keepdims=True))
        return dtg, dpe, de * pn

    if dxn is None:
        def head(xn_, tgt, t, pe_, pg):
            err = xn_ - tgt
            dx_ = err * (1.0 / D)
            dtg, dpe, dpg = ple_bwd_core(dx_, t, pe_, pg)
            return [dx_, dtg, dpe], [dpg, err * err]

        (dxn, dtg, dpe, dpg, lsq) = row_call(
            "loss_ple_bwd", head,
            [(xn, 'row', D, 0, 0), (target, 'row', D, 0, 0), (sv['t_'], 'row', D, 0, 0), (sv['pe'], 'row', D, 0, 0),
             (sm['ple_norm_g'], 'vec', D, 0, 0)],
            [(D, F32, D, 0, 0), (D, BF16, D, 0, 0), (D, BF16, D, 0, 0)], [(D, D, 0, 0), (D, D, 0, 0)], M=S, tm=tm)
        out['loss_sq'] = lsq
    else:
        def mid(dx_, t, pe_, pg):
            dtg, dpe, dpg = ple_bwd_core(dx_, t, pe_, pg)
            return [dtg, dpe], [dpg]

        (dtg, dpe, dpg) = row_call(
            "ple_bwd", mid,
            [(dxn, 'row', D, 0, 0), (sv['t_'], 'row', D, 0, 0), (sv['pe'], 'row', D, 0, 0),
             (sm['ple_norm_g'], 'vec', D, 0, 0)],
            [(D, BF16, D, 0, 0), (D, BF16, D, 0, 0)], [(D, D, 0, 0)], M=S, tm=tm)
    out['ple_norm_g'] = dpg

    def ln_bwd_epi(scale):
        def epi(acc, ex):
            res, r_, g_ = ex
            dh = scale * res + acc
            xhat, rstd = _ln_stats(r_)
            dr = _ln_bwd(dh, xhat, rstd, g_)
            return [dr, dr], [dh * xhat, dh]
        return epi

    dr2, dr2b, dg2, db2 = fused_mm(
        "dh2", [(dtg, W['ple_gate_T'], 0, False)],
        [(dxn, 'row', D, 0), (sv['r2'], 'row', D, 0), (sm['ln2_g'], 'vec', D, 0)],
        ln_bwd_epi(1.0), [(D, F32, D, 0), (D, BF16, D, 0)], [(D, D, 0), (D, D, 0)], M=S, tm=tm, tn=D)
    out['ln2_g'], out['ln2_b'] = dg2, db2

    tnf = cf['tnf']

    def dswiglu_epi(acc, ex):
        gg, uu = ex
        s = _sig(gg)
        return [acc * uu * _dsilu(gg, s), acc * (gg * s)], []

    dg_b, du_b = fused_mm(
        "d_down", [(dr2b, W['down_T'], 0, False)],
        [(sv['g_'], 'row', tnf, 0), (sv['u_'], 'row', tnf, 0)], dswiglu_epi,
        [(F, BF16, tnf, 0), (F, BF16, tnf, 0)], M=S, tm=tm, tn=tnf, nj=F // tnf)

    dr1, dr1b, dg1, db1 = fused_mm(
        "dh1", [(dg_b, W['gate_T'], 0, True), (du_b, W['up_T'], 0, True)],
        [(dr2, 'row', D, 0), (sv['r1'], 'row', D, 0), (sm['ln1_g'], 'vec', D, 0)],
        ln_bwd_epi(alpha), [(D, F32, D, 0), (D, BF16, D, 0)], [(D, D, 0), (D, D, 0)],
        M=S, tm=tm, tn=D, nk=cf['nk_f'])
    out['ln1_g'], out['ln1_b'] = dg1, db1

    goff = (2 * CD) // D

    def dmerge_epi(acc, ex):
        ga, gb, ya, yb = ex
        sa_, sb_ = _sig(ga), _sig(gb)
        dga = acc * ya * (sa_ * (1.0 - sa_))
        dgb = acc * yb * (sb_ * (1.0 - sb_))
        return [jnp.concatenate([dga, dgb], axis=1), acc * sa_, acc * sb_], []

    dproj, dya_b, dyb_b = fused_mm(
        "d_merge", [(dr1b, W['o_T'], 0, False)],
        [(sv['proj'], 'row', D, goff), (sv['proj'], 'row', D, goff + 1), (sv['y_a'], 'row', D, 0), (sv['y_b'], 'row', D, 0)],
        dmerge_epi, [(NM, BF16, 2 * D, (2 * CD) // (2 * D)), (D, BF16, D, 0), (D, BF16, D, 0)], M=S, tm=tm, tn=D)

    def dsa_epi(acc, ex):
        ca_, g_, b_ = ex
        xhat, rstd = _ln_stats(ca_)
        la = xhat * g_ + b_
        dla = acc * _dsilu(la, _sig(la))
        dca = _ln_bwd(dla, xhat, rstd, g_)
        return [dca], [dla * xhat, dla, dca]

    dca, dlag, dlab, dcab = fused_mm(
        "d_a_out", [(dya_b, W['a_out_T'], 0, False)],
        [(sv['ca'], 'row', CD, 0), (sm['ln_a_g'], 'vec', CD, 0), (sm['ln_a_b'], 'vec', CD, 0)],
        dsa_epi, [(CD, F32, CD, 0)], [(CD, CD, 0), (CD, CD, 0), (CD, CD, 0)], M=S, tm=tm, tn=D)
    out['ln_a_g'], out['ln_a_b'], out['conv_a_b'] = dlag, dlab, dcab

    def dglu_epi(du, ex):
        a, gt = ex
        s = _sig(gt)
        return [jnp.concatenate([du * s, du * a * (s * (1.0 - s))], axis=1)], []

    dproj, dwa = conv_call(
        "d_conv_a", dca, 0, sm['conv_a_w'], cf['KA'], dglu_epi,
        [(sv['proj'], 'row', CD, 0, 0), (sv['proj'], 'row', CD, 1, 0)],
        [(NM, BF16, 2 * CD, 0, 0)], M=S, tm=cf['tmc'], cw=CD, nc=1, reverse=True, xin=(sv['u'], 0),
        passthrough=(dproj, 0))
    out['conv_a_w'] = dwa

    zoff = (2 * CD + 2 * D) // DI

    def dgate_norm_epi(acc, ex):
        ysum_, xs, z, dsk, ng = ex
        y = ysum_ + xs * dsk
        sz = _sig(z)
        siluz = z * sz
        yz = y * siluz
        dyzs, yhats = [], []
        for g in range(G):
            t = yz[:, g * gw:(g + 1) * gw]
            rinv = lax.rsqrt(jnp.mean(t * t, axis=-1, keepdims=True) + RMS_EPS)
            yh = t * rinv
            qv = acc[:, g * gw:(g + 1) * gw] * ng[:, g * gw:(g + 1) * gw]
            dyzs.append(rinv * (qv - yh * jnp.mean(qv * yh, axis=-1, keepdims=True)))
            yhats.append(yh)
        dyz = jnp.concatenate(dyzs, axis=1)
        yhat = jnp.concatenate(yhats, axis=1)
        dy = dyz * siluz
        dz = dyz * y * _dsilu(z, sz)
        return [dz], [acc * yhat, dy * xs], [dy]

    tmr = cf['tmr']
    dproj, dng, ddsk, dyT = fused_mm(
        "d_b_out", [(dyb_b, W['b_out_T'], 0, False)],
        [(sv['ysum'], 'row', DI, 0), (sv['xs'], 'row', DI, 0), (sv['proj'], 'row', DI, zoff),
         (sm['dskip_full'], 'vec', DI, 0), (sm['ssm_norm_g'], 'vec', DI, 0)],
        dgate_norm_epi, [(NM, BF16, DI, zoff)], [(DI, DI, 0), (DI, DI, 0)],
        M=S, tm=tmr, tn=DI, passthrough=(dproj, 0), t_outs=[(DI, F32, DI, 0)])
    out['ssm_norm_g'], out['dskip_full'] = dng, ddsk

    dxbc_f, ddt_f, dA_f = ssd_bwd("ssd_bwd_f", sv['xsT'], sv['bc'], sv['dtraw'], dyT, sv['st_f'], sm['dtb_f'],
                                  sm['alog_f'], S=S, DI=DI, G=G, H=H, rev=False)
    dcb, ddt_r, dA_r, dcbb = ssd_bwd("ssd_bwd_r", sv['xsT'], sv['bc'], sv['dtraw'], dyT, sv['st_r'], sm['dtb_r'],
                                     sm['alog_r'], S=S, DI=DI, G=G, H=H, rev=True,
                                     tail=(dxbc_f, sv['cbv_x'], sv['cbv_bc'], sm['dskipT']))
    out['dA_f'], out['dA_r'] = dA_f, dA_r
    out['ssm_conv_b'] = dcbb

    xoff = (2 * CD + 2 * D + DI) // DI
    dproj, dwb = conv_call(
        "d_conv_b", dcb, 0, sm['ssm_conv_w'], cf['KB'], lambda conv, ex: ([conv], []), [],
        [(NM, BF16, DI, xoff, 1)], M=S, tm=cf['tmc'], cw=DI, nc=XBC // DI, reverse=True, xin=(sv['proj'], xoff),
        passthrough=(dproj, 0))
    out['ssm_conv_w'] = dwb

    ddtb, ddt_bias = row_call("d_dt", lambda a, b: ([a + b], [a + b]),
                              [(ddt_f, 'row', LANES, 0, 0), (ddt_r, 'row', LANES, 0, 0)],
                              [(LANES, BF16, LANES, 0, 0)], [(LANES, LANES, 0, 0)], M=S, tm=tm)
    out['dt_bias'] = ddt_bias

    dx, = fused_mm("d_x", [(dproj, W['in_main_T'], 0, True), (ddtb, W['in_dt_T'], 0, False)],
                   [(dr1, 'row', D, 0)], lambda acc, ex: ([alpha * ex[0] + acc], []),
                   [(D, F32, D, 0)], M=S, tm=cf['tmx'], tn=D, nk=cf['nk_in'])

    tmw = cf['tmw']
    xb = sv['xb']
    out['w_in'] = jnp.concatenate(
        [mm_tn("dw_in", xb, dproj, tm=tmw, tk=D, tn=cf['tn_in']),
         mm_tn("dw_dt", xb, ddtb, tm=tmw, tk=D, tn=LANES)[:, :2 * H]], axis=1)
    out['w_a_out'] = mm_tn("dw_a_out", sv['sa'], dya_b, tm=tmw, tk=CD, tn=D)
    out['w_b_out'] = mm_tn("dw_b_out", sv['yn'], dyb_b, tm=tmw, tk=DI // 2, tn=D)
    out['w_o'] = mm_tn("dw_o", sv['merged'], dr1b, tm=tmw, tk=D, tn=D)
    out['w_gate_up'] = jnp.concatenate(
        [mm_tn("dw_gate", sv['hb'], dg_b, tm=tmw, tk=D, tn=tnf),
         mm_tn("dw_up", sv['hb'], du_b, tm=tmw, tk=D, tn=tnf)], axis=1)
    out['w_down'] = mm_tn("dw_down", sv['f'], dr2b, tm=tmw, tk=tnf, tn=D)
    out['w_ple'] = mm_tn("dw_ple", sv['pb'], dpe, tm=tmw, tk=sv['pb'].shape[1], tn=D)
    out['w_ple_gate'] = mm_tn("dw_ple_gate", sv['h2b'], dtg, tm=tmw, tk=D, tn=D)
    return dx, out


_WEIGHTS = ['w_in', 'conv_a_w', 'conv_a_b', 'ln_a_g', 'ln_a_b', 'w_a_out', 'ssm_conv_w', 'ssm_conv_b', 'a_log',
            'dt_bias', 'd_skip', 'ssm_norm_g', 'w_b_out', 'w_o', 'ln1_g', 'ln1_b', 'w_gate_up', 'w_down', 'ln2_g',
            'ln2_b', 'w_ple', 'ple_norm_g', 'w_ple_gate']
_COL_SHARDED = ['w_in', 'conv_a_w', 'ssm_conv_w', 'w_gate_up', 'w_ple']
_ROW_SHARDED = ['w_a_out', 'w_b_out', 'w_o', 'w_down', 'w_ple_gate']
_BIG = _COL_SHARDED + _ROW_SHARDED
_SMALL = [n for n in _WEIGHTS if n not in _BIG]
_CONV = ['conv_a_w', 'ssm_conv_w']


def _ceil_to(n, k):
    return -(-n // k) * k


def kernel(x, p, w_in, conv_a_w, conv_a_b, ln_a_g, ln_a_b, w_a_out, ssm_conv_w, ssm_conv_b, a_log, dt_bias, d_skip, ssm_norm_g, w_b_out, w_o, ln1_g, ln1_b, w_gate_up, w_down, ln2_g, ln2_b, w_ple, ple_norm_g, w_ple_gate, loss_target, m_w_in, m_conv_a_w, m_conv_a_b, m_ln_a_g, m_ln_a_b, m_w_a_out, m_ssm_conv_w, m_ssm_conv_b, m_a_log, m_dt_bias, m_d_skip, m_ssm_norm_g, m_w_b_out, m_w_o, m_ln1_g, m_ln1_b, m_w_gate_up, m_w_down, m_ln2_g, m_ln2_b, m_w_ple, m_ple_norm_g, m_w_ple_gate, v_w_in, v_conv_a_w, v_conv_a_b, v_ln_a_g, v_ln_a_b, v_w_a_out, v_ssm_conv_w, v_ssm_conv_b, v_a_log, v_dt_bias, v_d_skip, v_ssm_norm_g, v_w_b_out, v_w_o, v_ln1_g, v_ln1_b, v_w_gate_up, v_w_down, v_ln2_g, v_ln2_b, v_w_ple, v_ple_norm_g, v_w_ple_gate):
    wt = dict(w_in=w_in, conv_a_w=conv_a_w, conv_a_b=conv_a_b, ln_a_g=ln_a_g, ln_a_b=ln_a_b, w_a_out=w_a_out,
              ssm_conv_w=ssm_conv_w, ssm_conv_b=ssm_conv_b, a_log=a_log, dt_bias=dt_bias, d_skip=d_skip,
              ssm_norm_g=ssm_norm_g, w_b_out=w_b_out, w_o=w_o, ln1_g=ln1_g, ln1_b=ln1_b, w_gate_up=w_gate_up,
              w_down=w_down, ln2_g=ln2_g, ln2_b=ln2_b, w_ple=w_ple, ple_norm_g=ple_norm_g, w_ple_gate=w_ple_gate)
    mo = dict(w_in=m_w_in, conv_a_w=m_conv_a_w, conv_a_b=m_conv_a_b, ln_a_g=m_ln_a_g, ln_a_b=m_ln_a_b,
              w_a_out=m_w_a_out, ssm_conv_w=m_ssm_conv_w, ssm_conv_b=m_ssm_conv_b, a_log=m_a_log,
              dt_bias=m_dt_bias, d_skip=m_d_skip, ssm_norm_g=m_ssm_norm_g, w_b_out=m_w_b_out, w_o=m_w_o,
              ln1_g=m_ln1_g, ln1_b=m_ln1_b, w_gate_up=m_w_gate_up, w_down=m_w_down, ln2_g=m_ln2_g, ln2_b=m_ln2_b,
              w_ple=m_w_ple, ple_norm_g=m_ple_norm_g, w_ple_gate=m_w_ple_gate)
    vo = dict(w_in=v_w_in, conv_a_w=v_conv_a_w, conv_a_b=v_conv_a_b, ln_a_g=v_ln_a_g, ln_a_b=v_ln_a_b,
              w_a_out=v_w_a_out, ssm_conv_w=v_ssm_conv_w, ssm_conv_b=v_ssm_conv_b, a_log=v_a_log,
              dt_bias=v_dt_bias, d_skip=v_d_skip, ssm_norm_g=v_ssm_norm_g, w_b_out=v_w_b_out, w_o=v_w_o,
              ln1_g=v_ln1_g, ln1_b=v_ln1_b, w_gate_up=v_w_gate_up, w_down=v_w_down, ln2_g=v_ln2_g, ln2_b=v_ln2_b,
              w_ple=v_w_ple, ple_norm_g=v_ple_norm_g, w_ple_gate=v_w_ple_gate)

    L = w_in.shape[0]
    S, D = x.shape[1], x.shape[2]
    CD = conv_a_b.shape[1]
    DI = ssm_norm_g.shape[1]
    XBC = ssm_conv_b.shape[1]
    H = d_skip.shape[1]
    G = (XBC - DI) // (2 * D_STATE)
    F = w_down.shape[1] * 4
    N_IN = w_in.shape[2] * 4
    NM = N_IN - 2 * H
    KA, KB = conv_a_w.shape[1], ssm_conv_w.shape[1]
    assert DI == H * HEAD_DIM and CD == D and DI == 2 * D and XBC == 2 * DI and NM == 2 * CD + 2 * D + DI + XBC
    assert 2 * H <= LANES and S % CHUNK == 0
    tnf = F // 2
    cf = dict(S=S, D=D, CD=CD, DI=DI, XBC=XBC, F=F, H=H, G=G, NM=NM, KA=KA, KB=KB, GW=(H // G) * HEAD_DIM,
              alpha=float((2 * L) ** 0.25), tm=min(512, S), tmx=min(1024, S), tmc=min(256, S), tmr=min(256, S), tmw=min(1024, S),
              tn_in=D, tnf=tnf, nk_f=2, nk_in=NM // DI)

    core = lax.axis_index("c").astype(jnp.int32).reshape(1)
    split_names = [n for n in _BIG if n not in _CONV]

    def layer_weights(l, got):
        full = {}
        for n, g in zip(split_names + _CONV, got):
            if n in _COL_SHARDED:
                full[n] = g.transpose(1, 0, 2).reshape(g.shape[1], 4 * g.shape[2])
            else:
                full[n] = g.reshape(4 * g.shape[1], g.shape[2])
        win = full['w_in']
        in_main = win[:, :NM]
        in_dt = _pad_lanes(win[:, NM:])
        gu = full['w_gate_up']
        W = dict(in_main=in_main, in_dt=in_dt, in_main_T=in_main.T, in_dt_T=in_dt.T,
                 a_out=full['w_a_out'], a_out_T=full['w_a_out'].T,
                 b_out=full['w_b_out'], b_out_T=full['w_b_out'].T,
                 o=full['w_o'], o_T=full['w_o'].T, gate_up=gu, gate_T=gu[:, :F].T, up_T=gu[:, F:].T,
                 down=full['w_down'], down_T=full['w_down'].T, ple=full['w_ple'],
                 ple_gate=full['w_ple_gate'], ple_gate_T=full['w_ple_gate'].T)
        row = lambda v: v.reshape(1, -1)
        head_table = lambda v: jnp.broadcast_to(jnp.pad(v, (0, LANES - H))[:, None], (LANES, LANES))
        sm = dict(conv_a_w=jnp.pad(full['conv_a_w'], ((0, _ceil_to(KA, SUBLANES) - KA), (0, 0))),
                  ssm_conv_w=jnp.pad(full['ssm_conv_w'], ((0, _ceil_to(KB, SUBLANES) - KB), (0, 0))),
                  conv_a_b=row(conv_a_b[l]), ln_a_g=row(ln_a_g[l]), ln_a_b=row(ln_a_b[l]),
                  ssm_conv_b=row(ssm_conv_b[l]), ssm_norm_g=row(ssm_norm_g[l]),
                  ln1_g=row(ln1_g[l]), ln1_b=row(ln1_b[l]), ln2_g=row(ln2_g[l]), ln2_b=row(ln2_b[l]),
                  ple_norm_g=row(ple_norm_g[l]),
                  dtb_f=head_table(dt_bias[l, 0]), dtb_r=head_table(dt_bias[l, 1]),
                  alog_f=head_table(a_log[l, 0]), alog_r=head_table(a_log[l, 1]),
                  dskip_full=row(jnp.repeat(d_skip[l], HEAD_DIM)),
                  dskipT=jnp.broadcast_to(jnp.repeat(d_skip[l], HEAD_DIM)[:, None], (DI, LANES)))
        return W, sm

    def blocks(n, gl):
        g = gl[n]
        if n == 'conv_a_w':
            g = g.sum(axis=1)[:KA]
        elif n == 'ssm_conv_w':
            g = g.sum(axis=1)[:KB]
        if n in _COL_SHARDED:
            return g.reshape(g.shape[0], 4, g.shape[1] // 4).transpose(1, 0, 2)
        return g.reshape(4, g.shape[0] // 4, g.shape[1])

    def core_sums(gl):
        mine = [blocks(n, gl) for n in split_names]
        theirs = core_send_half("core_send_half", mine)
        return [core_sum("core_sum_" + n, core, b, t) for n, b, t in zip(split_names, mine, theirs)]

    def chip_sums(l, parts, acc):
        sums = [chip_sum_into("chip_sum_" + n, core, pr, l, L, into=acc.get(n)) for n, pr in zip(split_names, parts)]
        return dict(zip(split_names, core_fill("core_fill", sums, l, L)))

    def shards(l):
        return [wt[n][l].astype(BF16) for n in split_names]

    lw = [None] * L
    pending = None
    for l in range(L):
        if l < L - 1 or L == 1:
            lw[l] = layer_weights(l, gather_layer("gather_weights", shards(l), [wt[n][l] for n in _CONV]))
    xl = x[0]
    if L > 1:
        sh = shards(L - 1)
        send, recv, sh, lands, token = chip_legs_start(
            "gather_start", 'gather', sh, [lax.empty((4,) + a.shape, a.dtype) for a in sh])
        pending = (send, recv, sh, lands)
        xlb = (xl + token[0, 0]).astype(BF16)
    else:
        xlb = xl.astype(BF16)
    saved = []
    for l in range(L):
        if l == L - 1 and pending is not None:
            send, recv, sh, lands = pending
            landed = chip_legs_wait("gather_wait", 'gather', send, recv, sh, lands, xl)
            conv_got = chip_exchange("gather_conv", [[wt[n][l]] for n in _CONV], gather=True)
            lw[l] = layer_weights(l, list(gather_finish("gather_finish", sh, landed)) + list(conv_got))
        xl, xlb, sv = _layer_fwd(cf, xl, xlb, p[l, 0].astype(BF16), lw[l][0], lw[l][1])
        saved.append(sv)
    grads = [None] * L
    dxl = None
    gsum = {}
    pending = None
    for l in reversed(range(L)):
        sm_l = lw[l][1]
        if pending is not None:
            sm_l = dict(sm_l, ple_norm_g=sm_l['ple_norm_g'] + pending[4][0, 0])
        if l == L - 1:
            dxl, grads[l] = _layer_bwd(cf, saved[l], lw[l][0], sm_l, target=loss_target[0], xn=xl)
        else:
            dxl, grads[l] = _layer_bwd(cf, saved[l], lw[l][0], sm_l, dxn=dxl)
        both = core_sums(grads[l])
        if l == L - 1 and L > 1:
            send, recv, both, lands, token = chip_legs_start(
                "scatter_start", 'scatter', both, [lax.empty(a.shape, a.dtype) for a in both])
            pending = (send, recv, both, lands, token)
            continue
        if pending is not None:
            send, recv, sent, lands, _ = pending
            landed = chip_legs_wait("scatter_wait", 'scatter', send, recv, sent, lands, dxl)
            gsum = chip_sums(L - 1, place_own(sent, landed), gsum)
            pending = None
        parts = chip_exchange("scatter_grads", [[t] for t in both], gather=False)
        gsum = chip_sums(l, [pr.reshape(4, pr.shape[2], pr.shape[3]) for pr in parts], gsum)
    loss = lax.psum(0.5 / D * jnp.sum(grads[L - 1]['loss_sq']), ("x", "y", "c"))
    grad_x = dxl[None]

    res = {}
    for n in split_names:
        shp = wt[n].shape
        flat = lambda a: a.reshape(shp[0] * shp[1], shp[2])
        outs = adamw_full("adamw_" + n, gsum[n], flat(wt[n]), flat(mo[n]), flat(vo[n]))
        res[n] = [o.reshape(shp) for o in [gsum[n]] + list(outs)]
    parts = chip_exchange("scatter_conv", [[blocks(n, grads[l]) for l in range(L)] for n in _CONV], gather=False)
    chip_sums = [sum_chips("chip_sum_" + n, pr.reshape(4, L * pr.shape[2], pr.shape[3])) for n, pr in zip(_CONV, parts)]
    sib_sums = sibling_swap("core_swap", chip_sums)
    for n, mine, sib in zip(_CONV, chip_sums, sib_sums):
        shp = wt[n].shape
        flat = lambda a: a.reshape(shp[0] * shp[1], shp[2])
        outs = adamw_shard("adamw_" + n, mine, sib, flat(wt[n]), flat(mo[n]), flat(vo[n]))
        res[n] = [o.reshape(shp) for o in outs]

    def small_pieces(l):
        gl = grads[l]
        A = -jnp.exp(a_log[l])
        d = dict(gl)
        d_alog = jnp.concatenate([gl['dA_f'].sum(axis=1)[:H] * A[0], gl['dA_r'].sum(axis=1)[:H] * A[1]])
        d['a_log'] = jnp.pad(d_alog[None], ((0, SUBLANES - 1), (0, 0)))
        d['dt_bias'] = gl['dt_bias'][:, :2 * H]
        d['d_skip'] = gl['dskip_full'].reshape(SUBLANES, H, HEAD_DIM).sum(axis=-1)
        return [_pad_lanes(d[n], _ceil_to(d[n].shape[1], LANES)) for n in _SMALL]

    widths = [_ceil_to(math.prod(wt[n].shape[1:]), LANES) for n in _SMALL]
    packed = jnp.concatenate([pc for l in range(L) for pc in small_pieces(l)], axis=1)
    gathered = all8_gather("gather_small", packed)

    def pack_params(src):
        return jnp.concatenate([_pad_lanes(src[n][l].reshape(1, -1), wd) for l in range(L) for n, wd in zip(_SMALL, widths)],
                               axis=1)

    small_out = adamw_small("adamw_small", gathered, pack_params(wt), pack_params(mo), pack_params(vo))
    off = 0
    per = {n: [[] for _ in range(4)] for n in _SMALL}
    for l in range(L):
        for n, wd in zip(_SMALL, widths):
            size = math.prod(wt[n].shape[1:])
            for k in range(4):
                per[n][k].append(small_out[k][0, off:off + size].reshape(wt[n].shape[1:]))
            off += wd
    for n in _SMALL:
        res[n] = [jnp.stack(per[n][k]) for k in range(4)]

    return (loss, grad_x, *[res[n][0] for n in _WEIGHTS], *[res[n][1] for n in _WEIGHTS],
            *[res[n][2] for n in _WEIGHTS], *[res[n][3] for n in _WEIGHTS])
```

```python
import math

import jax
import jax.numpy as jnp
from jax import lax
from jax.experimental import pallas as pl
from jax.experimental.pallas import tpu as pltpu

F32 = jnp.float32
BF16 = jnp.bfloat16

VMEM_LIMIT_BYTES = 56 * 1024 * 1024
LANES = 128
SUBLANES = 8

CHUNK = 128
D_STATE = 128
HEAD_DIM = 64
LN_EPS = 1e-5
RMS_EPS = 1e-6
ADAM_LR = 0.001
ADAM_B1 = 0.9
ADAM_B2 = 0.999
ADAM_EPS = 1e-08
ADAM_WD = 0.01
ADAM_STEP = 10
HALO = 16
MESH = pl.DeviceIdType.MESH


def _params(**kw):
    return pltpu.CompilerParams(vmem_limit_bytes=VMEM_LIMIT_BYTES, **kw)


def _sig(x):
    return jax.nn.sigmoid(x)


def _dsilu(x, s):
    return s * (1.0 + x * (1.0 - s))


def _ln_stats(r):
    mu = jnp.mean(r, axis=-1, keepdims=True)
    xc = r - mu
    var = jnp.mean(xc * xc, axis=-1, keepdims=True)
    rstd = lax.rsqrt(var + LN_EPS)
    return xc * rstd, rstd


def _ln_bwd(dy, xhat, rstd, g):
    dxh = dy * g
    m1 = jnp.mean(dxh, axis=-1, keepdims=True)
    m2 = jnp.mean(dxh * xhat, axis=-1, keepdims=True)
    return rstd * (dxh - m1 - xhat * m2)


def _f32(v):
    return v if v.dtype == F32 else v.astype(F32)


def _rows8(v):
    tm, w = v.shape
    return v.reshape(tm // SUBLANES, SUBLANES, w).sum(axis=0)


def fused_mm(name, prods, extras, epi, row_outs, col_outs=(), *, M, tm, tn, nj=1, nk=1,
             passthrough=None, t_outs=()):
    np_ = len(prods)
    ne = len(extras)
    nro = len(row_outs)
    nco = len(col_outs)
    use_acc = nk > 1

    def body(*refs):
        a_refs = [refs[2 * p] for p in range(np_)]
        w_refs = [refs[2 * p + 1] for p in range(np_)]
        pos = 2 * np_
        e_refs = refs[pos:pos + ne]
        pos += ne
        if passthrough is not None:
            pos += 1
        ro_refs = refs[pos:pos + nro]
        pos += nro
        co_refs = refs[pos:pos + nco]
        pos += nco
        to_refs = refs[pos:pos + len(t_outs)]
        pos += len(t_outs)
        acc_ref = refs[pos] if use_acc else None
        i = pl.program_id(1)
        k = pl.program_id(2)

        def prod(p):
            a = a_refs[p][...]
            if a.dtype != BF16:
                a = a.astype(BF16)
            return jnp.dot(a, w_refs[p][...], preferred_element_type=F32)

        def finish(acc):
            res = epi(acc, [_f32(r[...]) for r in e_refs])
            rows, cols = res[0], res[1]
            for v, o in zip(rows, ro_refs):
                o[...] = v.astype(o.dtype)
            for v, o in zip(res[2] if len(res) > 2 else (), to_refs):
                o[...] = v.T.astype(o.dtype)
            for v, o in zip(cols, co_refs):
                v8 = _rows8(v)

                @pl.when(i == 0)
                def _():
                    o[...] = v8

                @pl.when(i > 0)
                def _():
                    o[...] += v8

        if not use_acc:
            acc = prod(0)
            for p in range(1, np_):
                acc = acc + prod(p)
            finish(acc)
        else:
            @pl.when(k == 0)
            def _():
                acc = None
                for p in range(np_):
                    acc = prod(p) if acc is None else acc + prod(p)
                acc_ref[...] = acc

            @pl.when(k > 0)
            def _():
                acc = None
                for p in range(np_):
                    if prods[p][3]:
                        acc = prod(p) if acc is None else acc + prod(p)
                acc_ref[...] += acc

            @pl.when(k == nk - 1)
            def _():
                finish(acc_ref[...])

    in_specs = []
    args = []
    for a, w, joff, ksplit in prods:
        K = a.shape[1]
        if ksplit:
            tk = K // nk
            in_specs.append(pl.BlockSpec((tm, tk), lambda j, i, k: (i, k)))
            in_specs.append(pl.BlockSpec((tk, tn), lambda j, i, k, joff=joff: (k, j + joff)))
        else:
            in_specs.append(pl.BlockSpec((tm, K), lambda j, i, k: (i, 0)))
            in_specs.append(pl.BlockSpec((K, tn), lambda j, i, k, joff=joff: (0, j + joff)))
        args += [a, w]
    for arr, kind, width, c0 in extras:
        if kind == 'row':
            in_specs.append(pl.BlockSpec((tm, width), lambda j, i, k, c0=c0: (i, c0 + j)))
        else:
            in_specs.append(pl.BlockSpec((arr.shape[0], width), lambda j, i, k, c0=c0: (0, c0 + j)))
        args.append(arr)
    aliases = {}
    if passthrough is not None:
        arr, oidx = passthrough
        in_specs.append(pl.BlockSpec(memory_space=pl.ANY))
        aliases = {len(args): oidx}
        args.append(arr)
    out_shape = []
    out_specs = []
    for n_total, dtype, width, c0 in row_outs:
        out_shape.append(jax.ShapeDtypeStruct((M, n_total), dtype))
        out_specs.append(pl.BlockSpec((tm, width), lambda j, i, k, c0=c0: (i, c0 + j)))
    for n_total, width, c0 in col_outs:
        out_shape.append(jax.ShapeDtypeStruct((SUBLANES, n_total), F32))
        out_specs.append(pl.BlockSpec((SUBLANES, width), lambda j, i, k, c0=c0: (0, c0 + j)))
    for n_total, dtype, width, c0 in t_outs:
        out_shape.append(jax.ShapeDtypeStruct((n_total, M), dtype))
        out_specs.append(pl.BlockSpec((width, tm), lambda j, i, k, c0=c0: (c0 + j, i)))
    scratch = [pltpu.VMEM((tm, tn), F32)] if use_acc else []
    return pl.pallas_call(
        body, name=name, grid=(nj, M // tm, nk), in_specs=in_specs, out_specs=out_specs,
        out_shape=out_shape, scratch_shapes=scratch, input_output_aliases=aliases,
        compiler_params=_params(dimension_semantics=("arbitrary", "arbitrary", "arbitrary")),
    )(*args)


def mm_tn(name, a, b, *, tm, tk, tn):
    M, K = a.shape
    N = b.shape[1]

    def body(a_ref, b_ref, o_ref):
        m = pl.program_id(2)
        p = lax.dot_general(a_ref[...], b_ref[...], (((0,), (0,)), ((), ())),
                            preferred_element_type=F32)

        @pl.when(m == 0)
        def _():
            o_ref[...] = p

        @pl.when(m > 0)
        def _():
            o_ref[...] += p

    return pl.pallas_call(
        body, name=name, grid=(K // tk, N // tn, M // tm),
        in_specs=[pl.BlockSpec((tm, tk), lambda kk, j, m: (m, kk)),
                  pl.BlockSpec((tm, tn), lambda kk, j, m: (m, j))],
        out_specs=pl.BlockSpec((tk, tn), lambda kk, j, m: (kk, j)),
        out_shape=jax.ShapeDtypeStruct((K, N), F32),
        compiler_params=_params(dimension_semantics=("arbitrary", "arbitrary", "arbitrary")),
    )(a, b)


def row_call(name, fn, ins, row_outs, col_outs=(), *, M, tm, nc=1):
    ni = len(ins)
    nro = len(row_outs)

    def body(*refs):
        i = pl.program_id(1)
        vals = [_f32(r[...]) for r in refs[:ni]]
        rows, cols = fn(*vals)
        for v, o in zip(rows, refs[ni:ni + nro]):
            o[...] = v.astype(o.dtype)
        for v, o in zip(cols, refs[ni + nro:]):
            v8 = _rows8(v)

            @pl.when(i == 0)
            def _():
                o[...] = v8

            @pl.when(i > 0)
            def _():
                o[...] += v8

    in_specs = []
    for arr, kind, width, c0, cmul in ins:
        if kind == 'row':
            in_specs.append(pl.BlockSpec((tm, width), lambda cj, i, c0=c0, cmul=cmul: (i, c0 + cmul * cj)))
        else:
            in_specs.append(pl.BlockSpec((arr.shape[0], width), lambda cj, i, c0=c0, cmul=cmul: (0, c0 + cmul * cj)))
    out_shape = []
    out_specs = []
    for n_total, dtype, width, c0, cmul in row_outs:
        out_shape.append(jax.ShapeDtypeStruct((M, n_total), dtype))
        out_specs.append(pl.BlockSpec((tm, width), lambda cj, i, c0=c0, cmul=cmul: (i, c0 + cmul * cj)))
    for n_total, width, c0, cmul in col_outs:
        out_shape.append(jax.ShapeDtypeStruct((SUBLANES, n_total), F32))
        out_specs.append(pl.BlockSpec((SUBLANES, width), lambda cj, i, c0=c0, cmul=cmul: (0, c0 + cmul * cj)))
    return pl.pallas_call(
        body, name=name, grid=(nc, M // tm), in_specs=in_specs, out_specs=out_specs,
        out_shape=out_shape,
        compiler_params=_params(dimension_semantics=("arbitrary", "arbitrary")),
    )(*[a[0] for a in ins])


def conv_call(name, src, src_c0, w, K, epi, extras, row_outs, col_outs=(), *, M, tm, cw, nc,
              reverse, xin=None, passthrough=None, t_outs=(), w_c0=0):
    pad = (K - 1) // 2
    assert pad <= HALO - 1
    R = tm // HALO
    nblk = M // HALO
    n_i = M // tm
    Kp = w.shape[0]
    ne = len(extras)
    nro = len(row_outs)
    nco = len(col_outs)
    rb = 64
    cbw = min(cw, 256)
    n_copies = SUBLANES if K > SUBLANES else 1

    def body(*refs):
        main_ref, prev_ref, next_ref, w_ref = refs[:4]
        pos = 4
        xin_ref = None
        if xin is not None:
            xin_ref = refs[pos]
            pos += 1
        e_refs = refs[pos:pos + ne]
        pos += ne
        if passthrough is not None:
            pos += 1
        ro_refs = refs[pos:pos + nro]
        pos += nro
        co_refs = refs[pos:pos + nco]
        pos += nco
        to_refs = refs[pos:pos + len(t_outs)]
        pos += len(t_outs)
        dw_ref = None
        if xin is not None:
            dw_ref = refs[pos]
            pos += 1
        ext_ref, conv_ref = refs[pos], refs[pos + 1]
        i = pl.program_id(1)

        ext_ref[0, 0:HALO, :] = jnp.where(i == 0, 0.0, prev_ref[...].astype(F32))
        ext_ref[0, HALO:HALO + tm, :] = main_ref[...].astype(F32)
        ext_ref[0, HALO + tm:, :] = jnp.where(i == n_i - 1, 0.0, next_ref[...].astype(F32))
        if dw_ref is not None:
            @pl.when(i == 0)
            def _():
                dw_ref[...] = jnp.zeros_like(dw_ref)

        n_sh = tm + 2 * HALO - SUBLANES
        for c0 in range(0, cw, cbw):
            for sft in range(1, n_copies):
                ext_ref[sft, 0:n_sh, c0:c0 + cbw] = ext_ref[0, sft:sft + n_sh, c0:c0 + cbw]

        for c0 in range(0, cw, cbw):
            for r0 in range(0, tm, rb):
                acc = jnp.zeros((rb, cbw), F32)
                if xin_ref is not None:
                    xblk = xin_ref[r0:r0 + rb, c0:c0 + cbw].astype(F32)
                for k in range(K):
                    off = HALO + r0 + ((pad - k) if reverse else (k - pad))
                    sft = off % SUBLANES if n_copies > 1 else 0
                    d = ext_ref[sft, off - sft:off - sft + rb, c0:c0 + cbw]
                    acc = acc + d * w_ref[k:k + 1, c0:c0 + cbw]
                    if xin_ref is not None:
                        dw_ref[k, :, c0:c0 + cbw] += _rows8(xblk * d)
                conv_ref[r0:r0 + rb, c0:c0 + cbw] = acc

        res = epi(conv_ref[...], [_f32(r[...]) for r in e_refs])
        rows, cols = res[0], res[1]
        for v, o in zip(rows, ro_refs):
            o[...] = v.astype(o.dtype)
        for v, o in zip(res[2] if len(res) > 2 else (), to_refs):
            o[...] = v.T.astype(o.dtype)
        for v, o in zip(cols, co_refs):
            v8 = _rows8(v)

            @pl.when(i == 0)
            def _():
                o[...] = v8

            @pl.when(i > 0)
            def _():
                o[...] += v8

    in_specs = [
        pl.BlockSpec((tm, cw), lambda cj, i: (i, src_c0 + cj)),
        pl.BlockSpec((HALO, cw), lambda cj, i: (jnp.maximum(i * R - 1, 0), src_c0 + cj)),
        pl.BlockSpec((HALO, cw), lambda cj, i: (jnp.minimum((i + 1) * R, nblk - 1), src_c0 + cj)),
        pl.BlockSpec((Kp, cw), lambda cj, i: (0, w_c0 + cj)),
    ]
    args = [src, src, src, w]
    if xin is not None:
        in_specs.append(pl.BlockSpec((tm, cw), lambda cj, i, c0=xin[1]: (i, c0 + cj)))
        args.append(xin[0])
    for arr, kind, width, c0, cmul in extras:
        if kind == 'row':
            in_specs.append(pl.BlockSpec((tm, width), lambda cj, i, c0=c0, cmul=cmul: (i, c0 + cmul * cj)))
        else:
            in_specs.append(pl.BlockSpec((arr.shape[0], width), lambda cj, i, c0=c0, cmul=cmul: (0, c0 + cmul * cj)))
        args.append(arr)
    aliases = {}
    if passthrough is not None:
        in_specs.append(pl.BlockSpec(memory_space=pl.ANY))
        aliases = {len(args): passthrough[1]}
        args.append(passthrough[0])
    out_shape = []
    out_specs = []
    for n_total, dtype, width, c0, cmul in row_outs:
        out_shape.append(jax.ShapeDtypeStruct((M, n_total), dtype))
        out_specs.append(pl.BlockSpec((tm, width), lambda cj, i, c0=c0, cmul=cmul: (i, c0 + cmul * cj)))
    for n_total, width, c0, cmul in col_outs:
        out_shape.append(jax.ShapeDtypeStruct((SUBLANES, n_total), F32))
        out_specs.append(pl.BlockSpec((SUBLANES, width), lambda cj, i, c0=c0, cmul=cmul: (0, c0 + cmul * cj)))
    for n_total, dtype, width, c0, cmul in t_outs:
        out_shape.append(jax.ShapeDtypeStruct((n_total, M), dtype))
        out_specs.append(pl.BlockSpec((width, tm), lambda cj, i, c0=c0, cmul=cmul: (c0 + cmul * cj, i)))
    if xin is not None:
        out_shape.append(jax.ShapeDtypeStruct((Kp, SUBLANES, cw * nc), F32))
        out_specs.append(pl.BlockSpec((Kp, SUBLANES, cw), lambda cj, i: (0, 0, cj)))
    return pl.pallas_call(
        body, name=name, grid=(nc, n_i), in_specs=in_specs, out_specs=out_specs,
        out_shape=out_shape, input_output_aliases=aliases,
        scratch_shapes=[pltpu.VMEM((n_copies, tm + 2 * HALO, cw), F32), pltpu.VMEM((tm, cw), F32)],
        compiler_params=_params(dimension_semantics=("arbitrary", "arbitrary")),
    )(*args)


def _split_dot(m_bf16, v, n_pass, dims=None):
    out = None
    rest = v
    for p in range(n_pass):
        piece = rest.astype(BF16)
        if p + 1 < n_pass:
            rest = rest - piece.astype(F32)
        if dims is None:
            t = jnp.dot(m_bf16, piece, preferred_element_type=F32)
        else:
            t = lax.dot_general(m_bf16, piece, dims, preferred_element_type=F32)
        out = t if out is None else out + t
    return out


def _split_dot_r(v, m_bf16, n_pass):
    out = None
    rest = v
    for p in range(n_pass):
        piece = rest.astype(BF16)
        if p + 1 < n_pass:
            rest = rest - piece.astype(F32)
        t = jnp.dot(piece, m_bf16, preferred_element_type=F32)
        out = t if out is None else out + t
    return out


def _softplus(x):
    return jnp.maximum(x, 0.0) + jnp.log1p(jnp.exp(-jnp.abs(x)))


NT_DIMS = (((1,), (1,)), ((), ()))
TN_DIMS = (((0,), (0,)), ((), ()))


def _ssd_common(dtraw, dtbT, alogT, rev, n_heads):
    L = CHUNK
    if rev:
        dtraw = pltpu.roll(dtraw, LANES - n_heads, 1)
    preT = dtraw.T + dtbT
    dtT = _softplus(preT)
    AT = -jnp.exp(alogT)
    aT = dtT * AT
    ri = lax.broadcasted_iota(jnp.int32, (L, L), 0)
    ci = lax.broadcasted_iota(jnp.int32, (L, L), 1)
    up = (ri >= ci) if rev else (ri <= ci)
    lo = (ri <= ci) if rev else (ri >= ci)
    csT = _split_dot_r(aT, up.astype(BF16), 3)
    last = 0 if rev else L - 1
    lastB = jnp.broadcast_to(csT[:, last:last + 1], (L, L))
    return dict(preT=preT, dtT=dtT, AT=AT, csT=csT, cs=csT.T, up=up, lo=lo, ci=ci, last=last,
                doutT=jnp.exp(csT), dstT=jnp.exp(lastB - csT), totB=jnp.exp(lastB))


def ssd_fwd(name, xsT, bc, dtraw, dtbT, alogT, *, S, DI, G, H, rev, tail=None):
    NC = S // CHUNK
    R = H // G
    GW = R * HEAD_DIM
    N = D_STATE
    P = HEAD_DIM

    def body(*refs):
        xsT_ref, bc_ref, dtraw_ref, dtb_ref, alog_ref = refs[:5]
        if tail is None:
            y_ref, st_ref, h_ref = refs[5:]
        else:
            yo_ref, z_ref, xs_ref, dsk_ref, ng_ref = refs[5:10]
            y_ref, st_ref, yn_ref, h_ref = refs[10:]
        c = pl.program_id(0)

        @pl.when(c == 0)
        def _():
            h_ref[...] = jnp.zeros_like(h_ref)

        q = _ssd_common(dtraw_ref[...], dtb_ref[...], alog_ref[...], rev, H)
        cs, csT, dtT, doutT, totB = q['cs'], q['csT'], q['dtT'], q['doutT'], q['totB']
        wstT = q['dstT'] * dtT
        GB = 2 if G % 2 == 0 else 1
        for g0 in range(0, G, GB):
            gs = list(range(g0, g0 + GB))
            Bgs = [bc_ref[:, g * N:(g + 1) * N].astype(BF16) for g in gs]
            Cgs = [bc_ref[:, G * N + g * N:G * N + (g + 1) * N].astype(BF16) for g in gs]
            CBTs = [lax.dot_general(b, c_, NT_DIMS, preferred_element_type=F32) for b, c_ in zip(Bgs, Cgs)]
            HTs = [h_ref[g] for g in gs]
            yoffTs = [lax.dot_general(HT.astype(BF16), c_, NT_DIMS, preferred_element_type=F32)
                      for HT, c_ in zip(HTs, Cgs)]
            xTs = [xsT_ref[g * GW:(g + 1) * GW, :] for g in gs]
            heads = [(k, r) for k in range(GB) for r in range(R)]
            hs = [gs[k] * R + r for k, r in heads]
            blks = [slice(r * P, (r + 1) * P) for _, r in heads]
            segs = [jnp.where(q['up'], csT[h:h + 1, :] - cs[:, h:h + 1], -1e30) for h in hs]
            GTs = [(CBTs[k] * jnp.exp(sg)).astype(BF16) for (k, _), sg in zip(heads, segs)]
            xThs = [xTs[k][b, :] for (k, _), b in zip(heads, blks)]
            XThs = [(xTh * dtT[h:h + 1, :]).astype(BF16) for xTh, h in zip(xThs, hs)]
            ydTs = [jnp.dot(a, GT, preferred_element_type=F32) for a, GT in zip(XThs, GTs)]
            ys = [ydT + yoffTs[k][b, :] * doutT[h:h + 1, :] for ydT, (k, _), b, h in zip(ydTs, heads, blks, hs)]
            xws = [xTh * wstT[h:h + 1, :] for xTh, h in zip(xThs, hs)]
            tots = [jnp.broadcast_to(totB[h:h + 1, :], (P, N)) for h in hs]
            for k, g in enumerate(gs):
                sel = slice(k * R, (k + 1) * R)
                y_ref[:, g * GW:(g + 1) * GW] = jnp.concatenate(ys[sel], axis=0).T
                xwT = jnp.concatenate(xws[sel], axis=0).astype(BF16)
                ST = jnp.dot(xwT, Bgs[k], preferred_element_type=F32)
                st_ref[0, g] = HTs[k]
                h_ref[g] = HTs[k] * jnp.concatenate(tots[sel], axis=0) + ST
        if tail is not None:
            y = y_ref[...] + yo_ref[...]
            y_ref[...] = y
            z = _f32(z_ref[...])
            yz = (y + xs_ref[...] * dsk_ref[...]) * (z * _sig(z))
            for g in range(G):
                t = yz[:, g * GW:(g + 1) * GW]
                tn = t * lax.rsqrt(jnp.mean(t * t, axis=-1, keepdims=True) + RMS_EPS)
                yn_ref[:, g * GW:(g + 1) * GW] = (tn * ng_ref[:, g * GW:(g + 1) * GW]).astype(BF16)

    cidx = (lambda c: NC - 1 - c) if rev else (lambda c: c)
    cmap = lambda c: (cidx(c), 0)
    smap = lambda c: (cidx(c), 0, 0, 0)
    const = lambda c: (0, 0)
    tmap = lambda c: (0, cidx(c))
    in_specs = [pl.BlockSpec((DI, CHUNK), tmap), pl.BlockSpec((CHUNK, 2 * G * N), cmap), pl.BlockSpec((CHUNK, LANES), cmap),
                pl.BlockSpec((LANES, LANES), const), pl.BlockSpec((LANES, LANES), const)]
    out_specs = [pl.BlockSpec((CHUNK, DI), cmap), pl.BlockSpec((1, G, GW, N), smap)]
    out_shape = [jax.ShapeDtypeStruct((S, DI), F32), jax.ShapeDtypeStruct((NC, G, GW, N), F32)]
    args = [xsT, bc, dtraw, dtbT, alogT]
    if tail is not None:
        y_other, (z_arr, z_blk), xs_row, dsk, ng = tail
        in_specs += [pl.BlockSpec((CHUNK, DI), cmap), pl.BlockSpec((CHUNK, DI), lambda c: (cidx(c), z_blk)),
                     pl.BlockSpec((CHUNK, DI), cmap), pl.BlockSpec((1, DI), const), pl.BlockSpec((1, DI), const)]
        out_specs.append(pl.BlockSpec((CHUNK, DI), cmap))
        out_shape.append(jax.ShapeDtypeStruct((S, DI), BF16))
        args += [y_other, z_arr, xs_row, dsk, ng]
    return pl.pallas_call(
        body, name=name, grid=(NC,), in_specs=in_specs, out_specs=out_specs, out_shape=out_shape,
        scratch_shapes=[pltpu.VMEM((G, GW, N), F32)],
        compiler_params=_params(dimension_semantics=("arbitrary",)),
    )(*args)


def ssd_bwd(name, xsT, bc, dtraw, dyT, st, dtbT, alogT, *, S, DI, G, H, rev, tail=None):
    NC = S // CHUNK
    R = H // G
    GW = R * HEAD_DIM
    N = D_STATE
    XBC = DI + 2 * G * N
    P = HEAD_DIM
    L = CHUNK

    def body(*refs):
        xsT_ref, bc_ref, dtraw_ref, dyT_ref, st_ref, dtb_ref, alog_ref = refs[:7]
        if tail is None:
            dxbc_ref, ddt_ref, da_ref, dh_ref, dcst_ref, p2t_ref, p3t_ref, e2t_ref = refs[7:]
        else:
            other_ref, cbx_ref, cbbc_ref, dskT_ref = refs[7:11]
            dxbc_ref, ddt_ref, da_ref, dcol_ref, dh_ref, dcst_ref, p2t_ref, p3t_ref, e2t_ref = refs[11:]
        c = pl.program_id(0)

        @pl.when(c == 0)
        def _():
            dh_ref[...] = jnp.zeros_like(dh_ref)
            da_ref[...] = jnp.zeros_like(da_ref)
            dcst_ref[...] = jnp.zeros_like(dcst_ref)
            p2t_ref[...] = jnp.zeros_like(p2t_ref)
            p3t_ref[...] = jnp.zeros_like(p3t_ref)
            e2t_ref[...] = jnp.zeros_like(e2t_ref)

        q = _ssd_common(dtraw_ref[...], dtb_ref[...], alog_ref[...], rev, H)
        cs, csT, dtT, doutT, dstT, totB = q['cs'], q['csT'], q['dtT'], q['doutT'], q['dstT'], q['totB']
        wstT = dstT * dtT
        lane = q['ci']
        GB = 2 if G % 2 == 0 else 1
        for g0 in range(0, G, GB):
            gs = list(range(g0, g0 + GB))
            Bgs = [bc_ref[:, g * N:(g + 1) * N].astype(BF16) for g in gs]
            Cgs = [bc_ref[:, G * N + g * N:G * N + (g + 1) * N].astype(BF16) for g in gs]
            CBs = [lax.dot_general(c_, b, NT_DIMS, preferred_element_type=F32) for b, c_ in zip(Bgs, Cgs)]
            HpTs = [st_ref[0, g] for g in gs]
            HpTbs = [v.astype(BF16) for v in HpTs]
            dHTs = [dh_ref[g] for g in gs]
            dHTbs = [v.astype(BF16) for v in dHTs]
            BdHTs = [lax.dot_general(d, b, NT_DIMS, preferred_element_type=F32) for d, b in zip(dHTbs, Bgs)]
            yoffTs = [lax.dot_general(hp, c_, NT_DIMS, preferred_element_type=F32) for hp, c_ in zip(HpTbs, Cgs)]
            xTs = [xsT_ref[g * GW:(g + 1) * GW, :] for g in gs]
            dyTs = [dyT_ref[g * GW:(g + 1) * GW, :] for g in gs]
            heads = [(k, r) for k in range(GB) for r in range(R)]
            ks = [k for k, _ in heads]
            hs = [gs[k] * R + r for k, r in heads]
            blks = [slice(r * P, (r + 1) * P) for _, r in heads]
            Lms = [jnp.exp(jnp.where(q['lo'], cs[:, h:h + 1] - csT[h:h + 1, :], -1e30)) for h in hs]
            xThs = [xTs[k][b, :] for k, b in zip(ks, blks)]
            dyThs = [dyTs[k][b, :] for k, b in zip(ks, blks)]
            xThbs = [v.astype(BF16) for v in xThs]
            dyThbs = [v.astype(BF16) for v in dyThs]
            dGxs = [lax.dot_general(a, b, TN_DIMS, preferred_element_type=F32) for a, b in zip(dyThbs, xThbs)]
            Gms = [(CBs[k] * Lm).astype(BF16) for k, Lm in zip(ks, Lms)]
            XThbs = [(xTh * dtT[h:h + 1, :]).astype(BF16) for xTh, h in zip(xThs, hs)]
            u1Ts = [jnp.dot(a, Gm, preferred_element_type=F32) for a, Gm in zip(dyThbs, Gms)]
            ydTs = [lax.dot_general(a, Gm, NT_DIMS, preferred_element_type=F32) for a, Gm in zip(XThbs, Gms)]
            Ts = [dGx * (Lm * dtT[h:h + 1, :]) for dGx, Lm, h in zip(dGxs, Lms, hs)]
            uTs = [u1T + BdHTs[k][b, :] * dstT[h:h + 1, :] for u1T, k, b, h in zip(u1Ts, ks, blks, hs)]
            dyds = [dyTh * doutT[h:h + 1, :] for dyTh, h in zip(dyThs, hs)]
            xws = [xTh * wstT[h:h + 1, :] for xTh, h in zip(xThs, hs)]
            for i, h in enumerate(hs):
                k, b = ks[i], blks[i]
                p3row = jnp.sum(xws[i] * BdHTs[k][b, :], axis=0, keepdims=True)
                seg_row = jnp.sum(_f32(dyThbs[i]) * ydTs[i], axis=0, keepdims=True)
                seg_col = jnp.sum(_f32(XThbs[i]) * u1Ts[i], axis=0, keepdims=True)
                dcst_ref[h:h + 1, :] = (jnp.sum(dyds[i] * yoffTs[k][b, :], axis=0, keepdims=True)
                                        + seg_row - seg_col - p3row)
                p2t_ref[h:h + 1, :] = jnp.sum(xThs[i] * uTs[i], axis=0, keepdims=True)
                p3t_ref[h:h + 1, :] = p3row
                e2t_ref[h:h + 1, :] = jnp.sum(HpTs[k][b, :] * dHTs[k][b, :], axis=0, keepdims=True)
            dxs = [uT * dtT[h:h + 1, :] for uT, h in zip(uTs, hs)]
            if tail is not None:
                dxs = [d + dyTh * dskT_ref[h * P:(h + 1) * P, :] for d, dyTh, h in zip(dxs, dyThs, hs)]
            tots = [jnp.broadcast_to(totB[h:h + 1, :], (P, N)) for h in hs]
            for k, g in enumerate(gs):
                sel = slice(k * R, (k + 1) * R)
                dCB = Ts[k * R]
                for T in Ts[k * R + 1:(k + 1) * R]:
                    dCB = dCB + T
                dxbc_ref[:, g * GW:(g + 1) * GW] = jnp.concatenate(dxs[sel], axis=0).T
                dydT = jnp.concatenate(dyds[sel], axis=0).astype(BF16)
                xwT = jnp.concatenate(xws[sel], axis=0).astype(BF16)
                dCBb = dCB.astype(BF16)
                dC = (jnp.dot(dCBb, Bgs[k], preferred_element_type=F32)
                      + lax.dot_general(dydT, HpTbs[k], TN_DIMS, preferred_element_type=F32))
                dB = (lax.dot_general(dCBb, Cgs[k], TN_DIMS, preferred_element_type=F32)
                      + lax.dot_general(xwT, dHTbs[k], TN_DIMS, preferred_element_type=F32))
                dxbc_ref[:, DI + g * N:DI + (g + 1) * N] = dB
                dxbc_ref[:, DI + G * N + g * N:DI + G * N + (g + 1) * N] = dC
                dh_ref[g] = (dHTs[k] * jnp.concatenate(tots[sel], axis=0)
                             + jnp.dot(dydT, Cgs[k], preferred_element_type=F32))
        e1 = jnp.sum(p3t_ref[...], axis=1, keepdims=True)
        e2 = jnp.sum(e2t_ref[...], axis=1, keepdims=True)
        dcsT = dcst_ref[...] + jnp.where(lane == q['last'], e1 + totB * e2, 0.0)
        daT = _split_dot_r(dcsT, q['lo'].astype(BF16), 3)
        ddtT = daT * q['AT'] + p2t_ref[...]
        da_ref[...] += daT * dtT
        ddraw = jnp.where(lane < H, (ddtT * _sig(q['preT'])).T, 0.0)
        if rev:
            ddraw = pltpu.roll(ddraw, H, 1)
        ddt_ref[...] = ddraw
        if tail is not None:
            for c0, cb_ref in ((0, cbx_ref), (DI, cbbc_ref)):
                d = dxbc_ref[:, c0:c0 + DI] + other_ref[:, c0:c0 + DI]
                cb = cb_ref[...]
                dcb = d * _dsilu(cb, _sig(cb))
                dxbc_ref[:, c0:c0 + DI] = dcb
                part = _rows8(dcb)

                @pl.when(c == 0)
                def _():
                    dcol_ref[:, c0:c0 + DI] = part

                @pl.when(c > 0)
                def _():
                    dcol_ref[:, c0:c0 + DI] += part

    cmap = (lambda c: (c, 0)) if rev else (lambda c: (NC - 1 - c, 0))
    smap = (lambda c: (c, 0, 0, 0)) if rev else (lambda c: (NC - 1 - c, 0, 0, 0))
    const = lambda c: (0, 0)
    sq = pltpu.VMEM((LANES, CHUNK), F32)
    cix = (lambda c: c) if rev else (lambda c: NC - 1 - c)
    tmap = lambda c: (0, cix(c))
    in_specs = [pl.BlockSpec((DI, CHUNK), tmap), pl.BlockSpec((CHUNK, 2 * G * N), cmap), pl.BlockSpec((CHUNK, LANES), cmap),
                pl.BlockSpec((DI, CHUNK), tmap),
                pl.BlockSpec((1, G, GW, N), smap),
                pl.BlockSpec((LANES, LANES), const), pl.BlockSpec((LANES, LANES), const)]
    out_specs = [pl.BlockSpec((CHUNK, XBC), cmap), pl.BlockSpec((CHUNK, LANES), cmap),
                 pl.BlockSpec((LANES, LANES), const)]
    out_shape = [jax.ShapeDtypeStruct((S, XBC), F32), jax.ShapeDtypeStruct((S, LANES), F32),
                 jax.ShapeDtypeStruct((LANES, LANES), F32)]
    args = [xsT, bc, dtraw, dyT, st, dtbT, alogT]
    if tail is not None:
        in_specs += [pl.BlockSpec((CHUNK, XBC), cmap), pl.BlockSpec((CHUNK, DI), cmap),
                     pl.BlockSpec((CHUNK, 2 * G * N), cmap), pl.BlockSpec((DI, LANES), const)]
        out_specs.append(pl.BlockSpec((SUBLANES, XBC), const))
        out_shape.append(jax.ShapeDtypeStruct((SUBLANES, XBC), F32))
        args += list(tail)
    return pl.pallas_call(
        body, name=name, grid=(NC,), in_specs=in_specs, out_specs=out_specs, out_shape=out_shape,
        scratch_shapes=[pltpu.VMEM((G, GW, N), F32), sq, sq, sq, sq],
        compiler_params=_params(dimension_semantics=("arbitrary",)),
    )(*args)


ANY = pl.BlockSpec(memory_space=pl.ANY)


def chip_exchange(name, groups, gather):
    flat = [arr for grp in groups for arr in grp]
    n_in = len(flat)
    n_out = len(groups)
    n_rc = 3 * n_in

    def body(*refs):
        in_refs = refs[:n_in]
        out_refs = refs[n_in:n_in + n_out]
        send, recv = refs[n_in + n_out:]
        x, y, c = lax.axis_index("x"), lax.axis_index("y"), lax.axis_index("c")
        me = 2 * x + y
        peers = [(1 - x, y), (x, 1 - y), (1 - x, 1 - y)]
        remote = []
        q = 0
        for a, grp in enumerate(groups):
            for l in range(len(grp)):
                src = in_refs[q]
                dst = out_refs[a].at[me] if gather else out_refs[a].at[me, l]
                for j, (px, py) in enumerate(peers):
                    blk = src if gather else src.at[2 * px + py]
                    rc = pltpu.make_async_remote_copy(
                        src_ref=blk, dst_ref=dst, send_sem=send.at[3 * q + j], recv_sem=recv.at[3 * q + j],
                        device_id=(px, py, c), device_id_type=MESH)
                    rc.start()
                    remote.append(rc)
                q += 1
        for rc in remote:
            rc.wait()

    out_shape = []
    for grp in groups:
        a0 = grp[0]
        if gather:
            out_shape.append(jax.ShapeDtypeStruct((4,) + a0.shape, a0.dtype))
        else:
            out_shape.append(jax.ShapeDtypeStruct((4, len(grp)) + a0.shape[1:], a0.dtype))
    outs = pl.pallas_call(
        body, name=name, in_specs=[ANY] * n_in, out_specs=[ANY] * n_out, out_shape=out_shape,
        scratch_shapes=[pltpu.SemaphoreType.DMA((n_rc,)), pltpu.SemaphoreType.DMA((n_rc,))],
    )(*flat)
    me = _chip_index()
    res = []
    for grp, o in zip(groups, outs):
        for l, src in enumerate(grp):
            o = _put_block(o, src, (me,)) if gather else _put_block(o, _take_block(src, me), (me, l))
        res.append(o)
    return res


def _chip_index():
    return 2 * lax.axis_index("x") + lax.axis_index("y")


def _take_block(arr, idx):
    return lax.dynamic_index_in_dim(arr, idx, 0, keepdims=False)


def _put_block(dst, blk, idx):
    lead = len(idx)
    return lax.dynamic_update_slice(dst, blk.reshape((1,) * lead + blk.shape), tuple(idx) + (0,) * (dst.ndim - lead))


def gather_layer(name, split, whole):
    ns, nw = len(split), len(whole)
    n = ns + nw
    n_rc = 3 * (n + ns)

    def body(*refs):
        in_refs = refs[:n]
        out_refs = refs[n:2 * n]
        send, recv = refs[2 * n:]
        x, y, c = lax.axis_index("x"), lax.axis_index("y"), lax.axis_index("c")
        me = 2 * x + y
        sibling = (x, y, 1 - c)
        peers = [(1 - x, y), (x, 1 - y), (1 - x, 1 - y)]

        def region(a, chip, half):
            if a >= ns:
                return out_refs[a].at[chip]
            hr = split[a].shape[0] // 2
            return out_refs[a].at[chip, pl.ds(half * hr, hr)]

        def mine(a):
            if a >= ns:
                return in_refs[a]
            hr = split[a].shape[0] // 2
            return in_refs[a].at[pl.ds(c * hr, hr)]

        sends = []
        for a in range(n):
            for j, (px, py) in enumerate(peers):
                rc = pltpu.make_async_remote_copy(
                    src_ref=mine(a), dst_ref=region(a, me, c), send_sem=send.at[3 * a + j],
                    recv_sem=recv.at[3 * a + j], device_id=(px, py, c), device_id_type=MESH)
                rc.start()
                sends.append(rc)
        for a in range(n):
            for j, (px, py) in enumerate(peers):
                chip = 2 * px + py
                landed = pltpu.make_async_remote_copy(
                    src_ref=mine(a), dst_ref=region(a, chip, c), send_sem=send.at[3 * a + j],
                    recv_sem=recv.at[3 * a + j], device_id=(px, py, c), device_id_type=MESH)
                landed.wait_recv()
                if a < ns:
                    fw = pltpu.make_async_remote_copy(
                        src_ref=region(a, chip, c), dst_ref=region(a, chip, c), send_sem=send.at[3 * n + 3 * a + j],
                        recv_sem=recv.at[3 * n + 3 * a + j], device_id=sibling, device_id_type=MESH)
                    fw.start()
                    sends.append(fw)
        for a in range(ns):
            for j, (px, py) in enumerate(peers):
                chip = 2 * px + py
                pltpu.make_async_remote_copy(
                    src_ref=region(a, chip, 1 - c), dst_ref=region(a, chip, 1 - c), send_sem=send.at[3 * n + 3 * a + j],
                    recv_sem=recv.at[3 * n + 3 * a + j], device_id=sibling, device_id_type=MESH).wait_recv()
        for rc in sends:
            rc.wait_send()

    arrs = list(split) + list(whole)
    outs = pl.pallas_call(
        body, name=name, in_specs=[ANY] * n, out_specs=[ANY] * n,
        out_shape=[jax.ShapeDtypeStruct((4,) + a.shape, a.dtype) for a in arrs],
        scratch_shapes=[pltpu.SemaphoreType.DMA((n_rc,)), pltpu.SemaphoreType.DMA((n_rc,))],
    )(*arrs)
    me = _chip_index()
    return [_put_block(o, a, (me,)) for o, a in zip(outs, arrs)]


HBM_SPEC = pl.BlockSpec(memory_space=pltpu.HBM)
SEM_SPEC = pl.BlockSpec(memory_space=pltpu.SEMAPHORE)
IN_FLIGHT = pltpu.SideEffectType.DATAFLOW_SIDE_EFFECTING


def _chip_leg(kind, a_ref, l_ref, shape, c, me, chip):
    if kind == 'gather':
        hr = shape[0] // 2
        rows = pl.ds(c * hr, hr)
        return a_ref.at[rows], l_ref.at[me, rows], l_ref.at[chip, rows]
    return a_ref.at[chip], l_ref.at[me], l_ref.at[chip]


def chip_legs_start(name, kind, arrs, lands):
    n = len(arrs)

    def body(*refs):
        a_refs = refs[:n]
        l_refs = refs[n:2 * n]
        send, recv = refs[2 * n], refs[2 * n + 1]
        token = refs[-1]
        x, y, c = lax.axis_index("x"), lax.axis_index("y"), lax.axis_index("c")
        me = 2 * x + y
        for a in range(n):
            for j, (px, py) in enumerate([(1 - x, y), (x, 1 - y), (1 - x, 1 - y)]):
                src, dst, _ = _chip_leg(kind, a_refs[a], l_refs[a], arrs[a].shape, c, me, 2 * px + py)
                pltpu.make_async_remote_copy(src_ref=src, dst_ref=dst, send_sem=send.at[3 * a + j],
                                             recv_sem=recv.at[3 * a + j], device_id=(px, py, c),
                                             device_id_type=MESH).start()
        token[...] = jnp.zeros_like(token)

    both = list(arrs) + list(lands)
    outs = pl.pallas_call(
        body, name=name,
        out_shape=(pltpu.SemaphoreType.DMA((3 * n,)), pltpu.SemaphoreType.DMA((3 * n,)),
                   *[pltpu.HBM(a.shape, a.dtype) for a in both], jax.ShapeDtypeStruct((SUBLANES, LANES), F32)),
        in_specs=[HBM_SPEC] * (2 * n),
        out_specs=(SEM_SPEC, SEM_SPEC, *[HBM_SPEC] * (2 * n), pl.BlockSpec(memory_space=pltpu.VMEM)),
        input_output_aliases={i: 2 + i for i in range(2 * n)},
        compiler_params=pltpu.CompilerParams(has_side_effects=IN_FLIGHT),
    )(*[pltpu.with_memory_space_constraint(a, pltpu.HBM) for a in both])
    return outs[0], outs[1], list(outs[2:2 + n]), list(outs[2 + n:2 + 2 * n]), outs[-1]


def chip_legs_wait(name, kind, send, recv, arrs, lands, after):
    n = len(arrs)

    def body(*refs):
        a_refs = refs[:n]
        l_refs = refs[n:2 * n]
        send_, recv_ = refs[2 * n], refs[2 * n + 1]
        x, y, c = lax.axis_index("x"), lax.axis_index("y"), lax.axis_index("c")
        me = 2 * x + y
        legs = []
        for a in range(n):
            for j, (px, py) in enumerate([(1 - x, y), (x, 1 - y), (1 - x, 1 - y)]):
                src, dst, landing = _chip_leg(kind, a_refs[a], l_refs[a], arrs[a].shape, c, me, 2 * px + py)
                legs.append(pltpu.make_async_remote_copy(src_ref=src, dst_ref=landing, send_sem=send_.at[3 * a + j],
                                                         recv_sem=recv_.at[3 * a + j], device_id=(px, py, c),
                                                         device_id_type=MESH))
        for leg in legs:
            leg.wait_send()
        for leg in legs:
            leg.wait_recv()

    both = list(arrs) + list(lands)
    outs = pl.pallas_call(
        body, name=name, out_shape=tuple(pltpu.HBM(a.shape, a.dtype) for a in both),
        in_specs=[HBM_SPEC] * (2 * n) + [SEM_SPEC, SEM_SPEC, ANY], out_specs=tuple([HBM_SPEC] * (2 * n)),
        input_output_aliases={i: i for i in range(2 * n)},
        compiler_params=pltpu.CompilerParams(has_side_effects=IN_FLIGHT),
    )(*both, send, recv, after)
    return list(outs[n:])


def gather_finish(name, split, landed):
    n = len(split)

    def body(*refs):
        out_refs = refs[n:2 * n]
        send, recv = refs[2 * n:]
        x, y, c = lax.axis_index("x"), lax.axis_index("y"), lax.axis_index("c")
        sibling = (x, y, 1 - c)
        chips = [2 * (1 - x) + y, 2 * x + (1 - y), 2 * (1 - x) + (1 - y)]

        def region(a, chip, half):
            hr = split[a].shape[0] // 2
            return out_refs[a].at[chip, pl.ds(half * hr, hr)]

        sends = []
        for a in range(n):
            for j, chip in enumerate(chips):
                fw = pltpu.make_async_remote_copy(
                    src_ref=region(a, chip, c), dst_ref=region(a, chip, c), send_sem=send.at[3 * a + j],
                    recv_sem=recv.at[3 * a + j], device_id=sibling, device_id_type=MESH)
                fw.start()
                sends.append(fw)
        for a in range(n):
            for j, chip in enumerate(chips):
                pltpu.make_async_remote_copy(
                    src_ref=region(a, chip, 1 - c), dst_ref=region(a, chip, 1 - c), send_sem=send.at[3 * a + j],
                    recv_sem=recv.at[3 * a + j], device_id=sibling, device_id_type=MESH).wait_recv()
        for fw in sends:
            fw.wait_send()

    outs = pl.pallas_call(
        body, name=name, in_specs=[ANY] * n, out_specs=[ANY] * n,
        out_shape=[jax.ShapeDtypeStruct(a.shape, a.dtype) for a in landed],
        input_output_aliases={a: a for a in range(n)},
        scratch_shapes=[pltpu.SemaphoreType.DMA((3 * n,)), pltpu.SemaphoreType.DMA((3 * n,))],
    )(*landed)
    me = _chip_index()
    return [_put_block(o, a, (me,)) for o, a in zip(outs, split)]


def place_own(arrs, landed):
    me = _chip_index()
    return [_put_block(l, _take_block(a, me), (me,)) for a, l in zip(arrs, landed)]


def core_send_half(name, arrs):
    n = len(arrs)

    def body(*refs):
        in_refs = refs[:n]
        out_refs = refs[n:2 * n]
        send, recv = refs[2 * n:]
        c = lax.axis_index("c")
        peer = (lax.axis_index("x"), lax.axis_index("y"), 1 - c)
        rcs = []
        for a in range(n):
            hr = arrs[a].shape[1] // 2
            rc = pltpu.make_async_remote_copy(
                src_ref=in_refs[a].at[:, pl.ds((1 - c) * hr, hr)], dst_ref=out_refs[a], send_sem=send.at[a],
                recv_sem=recv.at[a], device_id=peer, device_id_type=MESH)
            rc.start()
            rcs.append(rc)
        for rc in rcs:
            rc.wait()

    return pl.pallas_call(
        body, name=name, in_specs=[ANY] * n, out_specs=[ANY] * n,
        out_shape=[jax.ShapeDtypeStruct((4, a.shape[1] // 2, a.shape[2]), a.dtype) for a in arrs],
        scratch_shapes=[pltpu.SemaphoreType.DMA((n,)), pltpu.SemaphoreType.DMA((n,))],
    )(*arrs)


def core_fill(name, arrs, layer, n_layers):
    n = len(arrs)

    def body(*refs):
        out_refs = refs[n:2 * n]
        send, recv = refs[2 * n:]
        c = lax.axis_index("c")
        peer = (lax.axis_index("x"), lax.axis_index("y"), 1 - c)
        rcs = []
        for a in range(n):
            r = arrs[a].shape[0] // n_layers
            hr = r // 2
            rows = out_refs[a].at[pl.ds(layer * r + c * hr, hr)]
            rc = pltpu.make_async_remote_copy(src_ref=rows, dst_ref=rows, send_sem=send.at[a], recv_sem=recv.at[a],
                                              device_id=peer, device_id_type=MESH)
            rc.start()
            rcs.append(rc)
        for a in range(n):
            r = arrs[a].shape[0] // n_layers
            hr = r // 2
            theirs = out_refs[a].at[pl.ds(layer * r + (1 - c) * hr, hr)]
            pltpu.make_async_remote_copy(src_ref=theirs, dst_ref=theirs, send_sem=send.at[a], recv_sem=recv.at[a],
                                         device_id=peer, device_id_type=MESH).wait_recv()
        for rc in rcs:
            rc.wait_send()

    return pl.pallas_call(
        body, name=name, in_specs=[ANY] * n, out_specs=[ANY] * n,
        out_shape=[jax.ShapeDtypeStruct(a.shape, a.dtype) for a in arrs],
        input_output_aliases={a: a for a in range(n)},
        scratch_shapes=[pltpu.SemaphoreType.DMA((n,)), pltpu.SemaphoreType.DMA((n,))],
    )(*arrs)


def sibling_swap(name, arrs):
    n = len(arrs)

    def body(*refs):
        in_refs = refs[:n]
        out_refs = refs[n:2 * n]
        send, recv = refs[2 * n:]
        peer = (lax.axis_index("x"), lax.axis_index("y"), 1 - lax.axis_index("c"))
        rcs = []
        for a in range(n):
            rc = pltpu.make_async_remote_copy(src_ref=in_refs[a], dst_ref=out_refs[a], send_sem=send.at[a],
                                              recv_sem=recv.at[a], device_id=peer, device_id_type=MESH)
            rc.start()
            rcs.append(rc)
        for rc in rcs:
            rc.wait()

    return pl.pallas_call(
        body, name=name, in_specs=[ANY] * n, out_specs=[ANY] * n,
        out_shape=[jax.ShapeDtypeStruct(a.shape, a.dtype) for a in arrs],
        scratch_shapes=[pltpu.SemaphoreType.DMA((n,)), pltpu.SemaphoreType.DMA((n,))],
    )(*arrs)


def all8_gather(name, v):
    flips = [(fx, fy, fc) for fx in (0, 1) for fy in (0, 1) for fc in (0, 1) if (fx, fy, fc) != (0, 0, 0)]

    def body(v_ref, out_ref, send, recv, loc):
        x, y, c = lax.axis_index("x"), lax.axis_index("y"), lax.axis_index("c")
        me = 4 * x + 2 * y + c
        lc = pltpu.make_async_copy(v_ref, out_ref.at[me], loc)
        lc.start()
        rcs = []
        for k, (fx, fy, fc) in enumerate(flips):
            tgt = (x + fx - 2 * x * fx, y + fy - 2 * y * fy, c + fc - 2 * c * fc)
            rc = pltpu.make_async_remote_copy(src_ref=v_ref, dst_ref=out_ref.at[me], send_sem=send.at[k],
                                              recv_sem=recv.at[k], device_id=tgt, device_id_type=MESH)
            rc.start()
            rcs.append(rc)
        lc.wait()
        for rc in rcs:
            rc.wait()

    return pl.pallas_call(
        body, name=name, in_specs=[ANY], out_specs=ANY,
        out_shape=jax.ShapeDtypeStruct((8,) + v.shape, v.dtype),
        scratch_shapes=[pltpu.SemaphoreType.DMA((7,)), pltpu.SemaphoreType.DMA((7,)), pltpu.SemaphoreType.DMA],
    )(v)


def _pick_rows(rows, cols, target_elems=128 * 1024, mult=SUBLANES):
    if rows % mult != 0:
        return rows
    best = mult
    t = mult
    while t <= rows:
        if rows % t == 0 and t * cols <= target_elems:
            best = t
        t += mult
    return best


def sum_chips(name, parts):
    _, R, C = parts.shape
    tm = _pick_rows(R, C)

    def body(p_ref, o_ref):
        o_ref[...] = (p_ref[0] + p_ref[1]) + (p_ref[2] + p_ref[3])

    return pl.pallas_call(
        body, name=name, grid=(R // tm,),
        in_specs=[pl.BlockSpec((4, tm, C), lambda i: (0, i, 0))],
        out_specs=pl.BlockSpec((tm, C), lambda i: (i, 0)),
        out_shape=jax.ShapeDtypeStruct((R, C), F32),
        compiler_params=_params(dimension_semantics=("arbitrary",)),
    )(parts)


def _adamw(g, w, m, v):
    m = ADAM_B1 * m + (1.0 - ADAM_B1) * g
    v = ADAM_B2 * v + (1.0 - ADAM_B2) * (g * g)
    m_hat = m / (1.0 - ADAM_B1 ** ADAM_STEP)
    v_hat = v / (1.0 - ADAM_B2 ** ADAM_STEP)
    delta = -ADAM_LR * (m_hat / (jnp.sqrt(v_hat) + ADAM_EPS) + ADAM_WD * w)
    return delta, m, v


def adamw_shard(name, s_mine, s_sib, w, m, v):
    R, C = w.shape
    tm = _pick_rows(R, C)

    def body(a_ref, b_ref, w_ref, m_ref, v_ref, g_out, d_out, m_out, v_out):
        g = a_ref[...] + b_ref[...]
        d, mn, vn = _adamw(g, w_ref[...], m_ref[...], v_ref[...])
        g_out[...] = g
        d_out[...] = d
        m_out[...] = mn
        v_out[...] = vn

    spec = pl.BlockSpec((tm, C), lambda i: (i, 0))
    return pl.pallas_call(
        body, name=name, grid=(R // tm,), in_specs=[spec] * 5, out_specs=[spec] * 4,
        out_shape=[jax.ShapeDtypeStruct((R, C), F32)] * 4,
        compiler_params=_params(dimension_semantics=("arbitrary",)),
    )(s_mine, s_sib, w, m, v)


def core_sum(name, core, g, got):
    _, r, C = g.shape
    hr = r // 2
    tm = _pick_rows(hr, 4 * C, 256 * 1024, 2 * SUBLANES)
    nh = hr // tm

    def body(c_ref, g_ref, s_ref, o_ref):
        o_ref[...] = (g_ref[...] + s_ref[...]).astype(BF16)

    return pl.pallas_call(
        body, name=name,
        grid_spec=pltpu.PrefetchScalarGridSpec(
            num_scalar_prefetch=1, grid=(nh,),
            in_specs=[pl.BlockSpec((4, tm, C), lambda i, cr: (0, cr[0] * nh + i, 0)),
                      pl.BlockSpec((4, tm, C), lambda i, cr: (0, i, 0))],
            out_specs=pl.BlockSpec((4, tm, C), lambda i, cr: (0, i, 0))),
        out_shape=jax.ShapeDtypeStruct((4, hr, C), BF16),
        compiler_params=_params(dimension_semantics=("arbitrary",)),
    )(core, g, got)


def chip_sum_into(name, core, parts, layer, n_layers, into=None):
    _, hr, C = parts.shape
    r = 2 * hr
    tm = _pick_rows(hr, 4 * C, 256 * 1024, 2 * SUBLANES)
    nh = hr // tm

    def body(c_ref, p_ref, *rest):
        o_ref = rest[-1]
        o_ref[...] = (_f32(p_ref[0]) + _f32(p_ref[1])) + (_f32(p_ref[2]) + _f32(p_ref[3]))

    in_specs = [pl.BlockSpec((4, tm, C), lambda i, cr: (0, i, 0))]
    args = [core, parts]
    aliases = {}
    if into is not None:
        in_specs.append(pl.BlockSpec(memory_space=pl.ANY))
        args.append(into)
        aliases = {2: 0}
    return pl.pallas_call(
        body, name=name,
        grid_spec=pltpu.PrefetchScalarGridSpec(
            num_scalar_prefetch=1, grid=(nh,), in_specs=in_specs,
            out_specs=pl.BlockSpec((tm, C), lambda i, cr: ((layer * r) // tm + cr[0] * nh + i, 0))),
        out_shape=jax.ShapeDtypeStruct((n_layers * r, C), F32), input_output_aliases=aliases,
        compiler_params=_params(dimension_semantics=("arbitrary",)),
    )(*args)


def adamw_full(name, g, w, m, v):
    R, C = w.shape
    tm = _pick_rows(R, C)

    def body(g_ref, w_ref, m_ref, v_ref, d_out, m_out, v_out):
        d, mn, vn = _adamw(g_ref[...], w_ref[...], m_ref[...], v_ref[...])
        d_out[...] = d
        m_out[...] = mn
        v_out[...] = vn

    spec = pl.BlockSpec((tm, C), lambda i: (i, 0))
    return pl.pallas_call(
        body, name=name, grid=(R // tm,), in_specs=[spec] * 4, out_specs=[spec] * 3,
        out_shape=[jax.ShapeDtypeStruct((R, C), F32)] * 3,
        compiler_params=_params(dimension_semantics=("arbitrary",)),
    )(g, w, m, v)


def adamw_small(name, parts, w, m, v):
    W = w.shape[1]

    def body(p_ref, w_ref, m_ref, v_ref, g_out, d_out, m_out, v_out):
        acc = p_ref[0]
        for k in range(1, 8):
            acc = acc + p_ref[k]
        g = jnp.sum(acc, axis=0, keepdims=True)
        d, mn, vn = _adamw(g, w_ref[...], m_ref[...], v_ref[...])
        g_out[...] = g
        d_out[...] = d
        m_out[...] = mn
        v_out[...] = vn

    return pl.pallas_call(
        body, name=name, out_shape=[jax.ShapeDtypeStruct((1, W), F32)] * 4,
        compiler_params=_params(),
    )(parts, w, m, v)


def _pad_lanes(v, width=LANES):
    return jnp.pad(v, ((0, 0), (0, width - v.shape[1])))


def _layer_fwd(cf, x, xb, pb, W, sm):
    S, D, CD, DI, XBC, F, H, G = cf['S'], cf['D'], cf['CD'], cf['DI'], cf['XBC'], cf['F'], cf['H'], cf['G']
    NM = cf['NM']
    alpha = cf['alpha']
    tm = cf['tm']
    tmx = cf['tmx']
    tn_in = cf['tn_in']
    sv = {}

    ident = lambda acc, ex: ([acc], [])
    proj, = fused_mm("in_proj", [(xb, W['in_main'], 0, False)], [], ident, [(NM, BF16, tn_in, 0)],
                     M=S, tm=tmx, tn=tn_in, nj=NM // tn_in)
    dtraw, = fused_mm("dt_proj", [(xb, W['in_dt'], 0, False)], [], ident, [(LANES, F32, LANES, 0)],
                      M=S, tm=tmx, tn=LANES)

    u, = row_call("glu", lambda a, gt: ([a * _sig(gt)], []),
                  [(proj, 'row', CD, 0, 0), (proj, 'row', CD, 1, 0)], [(CD, F32, CD, 0, 0)], M=S, tm=tm)

    def conv_a_epi(conv, ex):
        cb_, g_, b_ = ex
        ca = conv + cb_
        xhat, _ = _ln_stats(ca)
        la = xhat * g_ + b_
        return [ca, la * _sig(la)], []

    ca, sa = conv_call("conv_a", u, 0, sm['conv_a_w'], cf['KA'], conv_a_epi,
                       [(sm['conv_a_b'], 'vec', CD, 0, 0), (sm['ln_a_g'], 'vec', CD, 0, 0), (sm['ln_a_b'], 'vec', CD, 0, 0)],
                       [(CD, F32, CD, 0, 0), (CD, BF16, CD, 0, 0)], M=S, tm=cf['tmc'], cw=CD, nc=1, reverse=False)
    y_a, = fused_mm("a_out", [(sa, W['a_out'], 0, False)], [], ident, [(D, F32, D, 0)], M=S, tm=tmx, tn=D)

    def conv_x_epi(conv, ex):
        cb = conv + ex[0]
        act = cb * _sig(cb)
        return [cb, act], [], [act]

    def conv_bc_epi(conv, ex):
        cb = conv + ex[0]
        return [cb, cb * _sig(cb)], []

    xoff = (2 * CD + 2 * D + DI) // DI
    cbv_x, xs, xsT = conv_call("conv_b_x", proj, xoff, sm['ssm_conv_w'], cf['KB'], conv_x_epi,
                               [(sm['ssm_conv_b'], 'vec', DI, 0, 0)],
                               [(DI, F32, DI, 0, 0), (DI, F32, DI, 0, 0)], M=S, tm=cf['tmc'], cw=DI, nc=1,
                               reverse=False, t_outs=[(DI, F32, DI, 0, 0)])
    cbv_bc, bc = conv_call("conv_b_bc", proj, xoff + 1, sm['ssm_conv_w'], cf['KB'], conv_bc_epi,
                           [(sm['ssm_conv_b'], 'vec', DI, 1, 0)],
                           [(DI, F32, DI, 0, 0), (DI, F32, DI, 0, 0)], M=S, tm=cf['tmc'], cw=DI, nc=1,
                           reverse=False, w_c0=1)
    y_f, st_f = ssd_fwd("ssd_fwd_f", xsT, bc, dtraw, sm['dtb_f'], sm['alog_f'], S=S, DI=DI, G=G, H=H, rev=False)
    zoff = (2 * CD + 2 * D) // DI
    ysum, st_r, yn = ssd_fwd("ssd_fwd_r", xsT, bc, dtraw, sm['dtb_r'], sm['alog_r'], S=S, DI=DI, G=G, H=H, rev=True,
                             tail=(y_f, (proj, zoff), xs, sm['dskip_full'], sm['ssm_norm_g']))
    goff = (2 * CD) // D

    def merge_epi(acc, ex):
        ga, gb, ya = ex
        return [acc, _sig(ga) * ya + _sig(gb) * acc], []

    y_b, merged = fused_mm("b_out", [(yn, W['b_out'], 0, False)],
                           [(proj, 'row', D, goff), (proj, 'row', D, goff + 1), (y_a, 'row', D, 0)],
                           merge_epi, [(D, F32, D, 0), (D, BF16, D, 0)], M=S, tm=tm, tn=D)

    def mix_epi(acc, ex):
        xin, g_, b_ = ex
        r1 = alpha * xin + acc
        xhat, _ = _ln_stats(r1)
        return [r1, xhat * g_ + b_], []

    r1, hb = fused_mm("o_mix", [(merged, W['o'], 0, False)],
                      [(x, 'row', D, 0), (sm['ln1_g'], 'vec', D, 0), (sm['ln1_b'], 'vec', D, 0)],
                      mix_epi, [(D, F32, D, 0), (D, BF16, D, 0)], M=S, tm=tm, tn=D)

    tnf = cf['tnf']

    g32, g_ = fused_mm("ffn_gate", [(hb, W['gate_up'], 0, False)], [], lambda acc, ex: ([acc, acc], []),
                       [(F, F32, tnf, 0), (F, BF16, tnf, 0)], M=S, tm=tmx, tn=tnf, nj=F // tnf)
    u_, f = fused_mm("ffn_up", [(hb, W['gate_up'], F // tnf, False)], [(g32, 'row', tnf, 0)],
                     lambda acc, ex: ([acc, ex[0] * _sig(ex[0]) * acc], []),
                     [(F, BF16, tnf, 0), (F, BF16, tnf, 0)], M=S, tm=tmx, tn=tnf, nj=F // tnf)

    def down_epi(acc, ex):
        r1_, g1, b1, g2, b2 = ex
        xh1, _ = _ln_stats(r1_)
        r2 = alpha * (xh1 * g1 + b1) + acc
        xh2, _ = _ln_stats(r2)
        return [r2, xh2 * g2 + b2], []

    r2, h2b = fused_mm("ffn_down", [(f, W['down'], 0, False)],
                       [(r1, 'row', D, 0), (sm['ln1_g'], 'vec', D, 0), (sm['ln1_b'], 'vec', D, 0),
                        (sm['ln2_g'], 'vec', D, 0), (sm['ln2_b'], 'vec', D, 0)],
                       down_epi, [(D, F32, D, 0), (D, BF16, D, 0)], M=S, tm=tm, tn=D)

    pe, = fused_mm("ple_proj", [(pb, W['ple'], 0, False)], [], ident, [(D, F32, D, 0)], M=S, tm=tmx, tn=D)

    def ple_epi(acc, ex):
        r2_, g2, b2, pe_, pg = ex
        xh2, _ = _ln_stats(r2_)
        h2 = xh2 * g2 + b2
        e = pe_ * lax.rsqrt(jnp.mean(pe_ * pe_, axis=-1, keepdims=True) + RMS_EPS) * pg
        xn = h2 + e * _sig(acc)
        return [acc, xn, xn], []

    t_, xn, xnb = fused_mm("ple_gate", [(h2b, W['ple_gate'], 0, False)],
                           [(r2, 'row', D, 0), (sm['ln2_g'], 'vec', D, 0), (sm['ln2_b'], 'vec', D, 0),
                            (pe, 'row', D, 0), (sm['ple_norm_g'], 'vec', D, 0)],
                           ple_epi, [(D, F32, D, 0), (D, F32, D, 0), (D, BF16, D, 0)], M=S, tm=tm, tn=D)
    sv.update(x=x, xb=xb, pb=pb, proj=proj, dtraw=dtraw, u=u, ca=ca, sa=sa, y_a=y_a, cbv_x=cbv_x, cbv_bc=cbv_bc,
              xs=xs, xsT=xsT, bc=bc,
              ysum=ysum, st_f=st_f, st_r=st_r, yn=yn, y_b=y_b, merged=merged, r1=r1, hb=hb,
              g_=g_, u_=u_, f=f, r2=r2, h2b=h2b, t_=t_, pe=pe)
    return xn, xnb, sv


def _layer_bwd(cf, sv, W, sm, dxn=None, target=None, xn=None):
    S, D, CD, DI, XBC, F, H, G = cf['S'], cf['D'], cf['CD'], cf['DI'], cf['XBC'], cf['F'], cf['H'], cf['G']
    NM = cf['NM']
    alpha = cf['alpha']
    tm = cf['tm']
    gw = cf['GW']
    out = {}

    def ple_bwd_core(dx_, t, pe_, pg):
        s = _sig(t)
        rinv = lax.rsqrt(jnp.mean(pe_ * pe_, axis=-1, keepdims=True) + RMS_EPS)
        pn = pe_ * rinv
        e = pn * pg
        dtg = dx_ * e * (s * (1.0 - s))
        de = dx_ * s
        qv = de * pg
        dpe = rinv * (qv - pn * jnp.mean(qv * pn, axis=-1, keepdims=True))
        return dtg, dpe, de * pn

    if dxn is None:
        def head(xn_, tgt, t, pe_, pg):
            err = xn_ - tgt
            dx_ = err * (1.0 / D)
            dtg, dpe, dpg = ple_bwd_core(dx_, t, pe_, pg)
            return [dx_, dtg, dpe], [dpg, err * err]

        (dxn, dtg, dpe, dpg, lsq) = row_call(
            "loss_ple_bwd", head,
            [(xn, 'row', D, 0, 0), (target, 'row', D, 0, 0), (sv['t_'], 'row', D, 0, 0), (sv['pe'], 'row', D, 0, 0),
             (sm['ple_norm_g'], 'vec', D, 0, 0)],
            [(D, F32, D, 0, 0), (D, BF16, D, 0, 0), (D, BF16, D, 0, 0)], [(D, D, 0, 0), (D, D, 0, 0)], M=S, tm=tm)
        out['loss_sq'] = lsq
    else:
        def mid(dx_, t, pe_, pg):
            dtg, dpe, dpg = ple_bwd_core(dx_, t, pe_, pg)
            return [dtg, dpe], [dpg]

        (dtg, dpe, dpg) = row_call(
            "ple_bwd", mid,
            [(dxn, 'row', D, 0, 0), (sv['t_'], 'row', D, 0, 0), (sv['pe'], 'row', D, 0, 0),
             (sm['ple_norm_g'], 'vec', D, 0, 0)],
            [(D, BF16, D, 0, 0), (D, BF16, D, 0, 0)], [(D, D, 0, 0)], M=S, tm=tm)
    out['ple_norm_g'] = dpg

    def ln_bwd_epi(scale):
        def epi(acc, ex):
            res, r_, g_ = ex
            dh = scale * res + acc
            xhat, rstd = _ln_stats(r_)
            dr = _ln_bwd(dh, xhat, rstd, g_)
            return [dr, dr], [dh * xhat, dh]
        return epi

    dr2, dr2b, dg2, db2 = fused_mm(
        "dh2", [(dtg, W['ple_gate_T'], 0, False)],
        [(dxn, 'row', D, 0), (sv['r2'], 'row', D, 0), (sm['ln2_g'], 'vec', D, 0)],
        ln_bwd_epi(1.0), [(D, F32, D, 0), (D, BF16, D, 0)], [(D, D, 0), (D, D, 0)], M=S, tm=tm, tn=D)
    out['ln2_g'], out['ln2_b'] = dg2, db2

    tnf = cf['tnf']

    def dswiglu_epi(acc, ex):
        gg, uu = ex
        s = _sig(gg)
        return [acc * uu * _dsilu(gg, s), acc * (gg * s)], []

    dg_b, du_b = fused_mm(
        "d_down", [(dr2b, W['down_T'], 0, False)],
        [(sv['g_'], 'row', tnf, 0), (sv['u_'], 'row', tnf, 0)], dswiglu_epi,
        [(F, BF16, tnf, 0), (F, BF16, tnf, 0)], M=S, tm=tm, tn=tnf, nj=F // tnf)

    dr1, dr1b, dg1, db1 = fused_mm(
        "dh1", [(dg_b, W['gate_T'], 0, True), (du_b, W['up_T'], 0, True)],
        [(dr2, 'row', D, 0), (sv['r1'], 'row', D, 0), (sm['ln1_g'], 'vec', D, 0)],
        ln_bwd_epi(alpha), [(D, F32, D, 0), (D, BF16, D, 0)], [(D, D, 0), (D, D, 0)],
        M=S, tm=tm, tn=D, nk=cf['nk_f'])
    out['ln1_g'], out['ln1_b'] = dg1, db1

    goff = (2 * CD) // D

    def dmerge_epi(acc, ex):
        ga, gb, ya, yb = ex
        sa_, sb_ = _sig(ga), _sig(gb)
        dga = acc * ya * (sa_ * (1.0 - sa_))
        dgb = acc * yb * (sb_ * (1.0 - sb_))
        return [jnp.concatenate([dga, dgb], axis=1), acc * sa_, acc * sb_], []

    dproj, dya_b, dyb_b = fused_mm(
        "d_merge", [(dr1b, W['o_T'], 0, False)],
        [(sv['proj'], 'row', D, goff), (sv['proj'], 'row', D, goff + 1), (sv['y_a'], 'row', D, 0), (sv['y_b'], 'row', D, 0)],
        dmerge_epi, [(NM, BF16, 2 * D, (2 * CD) // (2 * D)), (D, BF16, D, 0), (D, BF16, D, 0)], M=S, tm=tm, tn=D)

    def dsa_epi(acc, ex):
        ca_, g_, b_ = ex
        xhat, rstd = _ln_stats(ca_)
        la = xhat * g_ + b_
        dla = acc * _dsilu(la, _sig(la))
        dca = _ln_bwd(dla, xhat, rstd, g_)
        return [dca], [dla * xhat, dla, dca]

    dca, dlag, dlab, dcab = fused_mm(
        "d_a_out", [(dya_b, W['a_out_T'], 0, False)],
        [(sv['ca'], 'row', CD, 0), (sm['ln_a_g'], 'vec', CD, 0), (sm['ln_a_b'], 'vec', CD, 0)],
        dsa_epi, [(CD, F32, CD, 0)], [(CD, CD, 0), (CD, CD, 0), (CD, CD, 0)], M=S, tm=tm, tn=D)
    out['ln_a_g'], out['ln_a_b'], out['conv_a_b'] = dlag, dlab, dcab

    def dglu_epi(du, ex):
        a, gt = ex
        s = _sig(gt)
        return [jnp.concatenate([du * s, du * a * (s * (1.0 - s))], axis=1)], []

    dproj, dwa = conv_call(
        "d_conv_a", dca, 0, sm['conv_a_w'], cf['KA'], dglu_epi,
        [(sv['proj'], 'row', CD, 0, 0), (sv['proj'], 'row', CD, 1, 0)],
        [(NM, BF16, 2 * CD, 0, 0)], M=S, tm=cf['tmc'], cw=CD, nc=1, reverse=True, xin=(sv['u'], 0),
        passthrough=(dproj, 0))
    out['conv_a_w'] = dwa

    zoff = (2 * CD + 2 * D) // DI

    def dgate_norm_epi(acc, ex):
        ysum_, xs, z, dsk, ng = ex
        y = ysum_ + xs * dsk
        sz = _sig(z)
        siluz = z * sz
        yz = y * siluz
        dyzs, yhats = [], []
        for g in range(G):
            t = yz[:, g * gw:(g + 1) * gw]
            rinv = lax.rsqrt(jnp.mean(t * t, axis=-1, keepdims=True) + RMS_EPS)
            yh = t * rinv
            qv = acc[:, g * gw:(g + 1) * gw] * ng[:, g * gw:(g + 1) * gw]
            dyzs.append(rinv * (qv - yh * jnp.mean(qv * yh, axis=-1, keepdims=True)))
            yhats.append(yh)
        dyz = jnp.concatenate(dyzs, axis=1)
        yhat = jnp.concatenate(yhats, axis=1)
        dy = dyz * siluz
        dz = dyz * y * _dsilu(z, sz)
        return [dz], [acc * yhat, dy * xs], [dy]

    tmr = cf['tmr']
    dproj, dng, ddsk, dyT = fused_mm(
        "d_b_out", [(dyb_b, W['b_out_T'], 0, False)],
        [(sv['ysum'], 'row', DI, 0), (sv['xs'], 'row', DI, 0), (sv['proj'], 'row', DI, zoff),
         (sm['dskip_full'], 'vec', DI, 0), (sm['ssm_norm_g'], 'vec', DI, 0)],
        dgate_norm_epi, [(NM, BF16, DI, zoff)], [(DI, DI, 0), (DI, DI, 0)],
        M=S, tm=tmr, tn=DI, passthrough=(dproj, 0), t_outs=[(DI, F32, DI, 0)])
    out['ssm_norm_g'], out['dskip_full'] = dng, ddsk

    dxbc_f, ddt_f, dA_f = ssd_bwd("ssd_bwd_f", sv['xsT'], sv['bc'], sv['dtraw'], dyT, sv['st_f'], sm['dtb_f'],
                                  sm['alog_f'], S=S, DI=DI, G=G, H=H, rev=False)
    dcb, ddt_r, dA_r, dcbb = ssd_bwd("ssd_bwd_r", sv['xsT'], sv['bc'], sv['dtraw'], dyT, sv['st_r'], sm['dtb_r'],
                                     sm['alog_r'], S=S, DI=DI, G=G, H=H, rev=True,
                                     tail=(dxbc_f, sv['cbv_x'], sv['cbv_bc'], sm['dskipT']))
    out['dA_f'], out['dA_r'] = dA_f, dA_r
    out['ssm_conv_b'] = dcbb

    xoff = (2 * CD + 2 * D + DI) // DI
    dproj, dwb = conv_call(
        "d_conv_b", dcb, 0, sm['ssm_conv_w'], cf['KB'], lambda conv, ex: ([conv], []), [],
        [(NM, BF16, DI, xoff, 1)], M=S, tm=cf['tmc'], cw=DI, nc=XBC // DI, reverse=True, xin=(sv['proj'], xoff),
        passthrough=(dproj, 0))
    out['ssm_conv_w'] = dwb

    ddtb, ddt_bias = row_call("d_dt", lambda a, b: ([a + b], [a + b]),
                              [(ddt_f, 'row', LANES, 0, 0), (ddt_r, 'row', LANES, 0, 0)],
                              [(LANES, BF16, LANES, 0, 0)], [(LANES, LANES, 0, 0)], M=S, tm=tm)
    out['dt_bias'] = ddt_bias

    dx, = fused_mm("d_x", [(dproj, W['in_main_T'], 0, True), (ddtb, W['in_dt_T'], 0, False)],
                   [(dr1, 'row', D, 0)], lambda acc, ex: ([alpha * ex[0] + acc], []),
                   [(D, F32, D, 0)], M=S, tm=cf['tmx'], tn=D, nk=cf['nk_in'])

    tmw = cf['tmw']
    xb = sv['xb']
    out['w_in'] = jnp.concatenate(
        [mm_tn("dw_in", xb, dproj, tm=tmw, tk=D, tn=cf['tn_in']),
         mm_tn("dw_dt", xb, ddtb, tm=tmw, tk=D, tn=LANES)[:, :2 * H]], axis=1)
    out['w_a_out'] = mm_tn("dw_a_out", sv['sa'], dya_b, tm=tmw, tk=CD, tn=D)
    out['w_b_out'] = mm_tn("dw_b_out", sv['yn'], dyb_b, tm=tmw, tk=DI // 2, tn=D)
    out['w_o'] = mm_tn("dw_o", sv['merged'], dr1b, tm=tmw, tk=D, tn=D)
    out['w_gate_up'] = jnp.concatenate(
        [mm_tn("dw_gate", sv['hb'], dg_b, tm=tmw, tk=D, tn=tnf),
         mm_tn("dw_up", sv['hb'], du_b, tm=tmw, tk=D, tn=tnf)], axis=1)
    out['w_down'] = mm_tn("dw_down", sv['f'], dr2b, tm=tmw, tk=tnf, tn=D)
    out['w_ple'] = mm_tn("dw_ple", sv['pb'], dpe, tm=tmw, tk=sv['pb'].shape[1], tn=D)
    out['w_ple_gate'] = mm_tn("dw_ple_gate", sv['h2b'], dtg, tm=tmw, tk=D, tn=D)
    return dx, out


_WEIGHTS = ['w_in', 'conv_a_w', 'conv_a_b', 'ln_a_g', 'ln_a_b', 'w_a_out', 'ssm_conv_w', 'ssm_conv_b', 'a_log',
            'dt_bias', 'd_skip', 'ssm_norm_g', 'w_b_out', 'w_o', 'ln1_g', 'ln1_b', 'w_gate_up', 'w_down', 'ln2_g',
            'ln2_b', 'w_ple', 'ple_norm_g', 'w_ple_gate']
_COL_SHARDED = ['w_in', 'conv_a_w', 'ssm_conv_w', 'w_gate_up', 'w_ple']
_ROW_SHARDED = ['w_a_out', 'w_b_out', 'w_o', 'w_down', 'w_ple_gate']
_BIG = _COL_SHARDED + _ROW_SHARDED
_SMALL = [n for n in _WEIGHTS if n not in _BIG]
_CONV = ['conv_a_w', 'ssm_conv_w']


def _ceil_to(n, k):
    return -(-n // k) * k


def kernel(x, p, w_in, conv_a_w, conv_a_b, ln_a_g, ln_a_b, w_a_out, ssm_conv_w, ssm_conv_b, a_log, dt_bias, d_skip, ssm_norm_g, w_b_out, w_o, ln1_g, ln1_b, w_gate_up, w_down, ln2_g, ln2_b, w_ple, ple_norm_g, w_ple_gate, loss_target, m_w_in, m_conv_a_w, m_conv_a_b, m_ln_a_g, m_ln_a_b, m_w_a_out, m_ssm_conv_w, m_ssm_conv_b, m_a_log, m_dt_bias, m_d_skip, m_ssm_norm_g, m_w_b_out, m_w_o, m_ln1_g, m_ln1_b, m_w_gate_up, m_w_down, m_ln2_g, m_ln2_b, m_w_ple, m_ple_norm_g, m_w_ple_gate, v_w_in, v_conv_a_w, v_conv_a_b, v_ln_a_g, v_ln_a_b, v_w_a_out, v_ssm_conv_w, v_ssm_conv_b, v_a_log, v_dt_bias, v_d_skip, v_ssm_norm_g, v_w_b_out, v_w_o, v_ln1_g, v_ln1_b, v_w_gate_up, v_w_down, v_ln2_g, v_ln2_b, v_w_ple, v_ple_norm_g, v_w_ple_gate):
    wt = dict(w_in=w_in, conv_a_w=conv_a_w, conv_a_b=conv_a_b, ln_a_g=ln_a_g, ln_a_b=ln_a_b, w_a_out=w_a_out,
              ssm_conv_w=ssm_conv_w, ssm_conv_b=ssm_conv_b, a_log=a_log, dt_bias=dt_bias, d_skip=d_skip,
              ssm_norm_g=ssm_norm_g, w_b_out=w_b_out, w_o=w_o, ln1_g=ln1_g, ln1_b=ln1_b, w_gate_up=w_gate_up,
              w_down=w_down, ln2_g=ln2_g, ln2_b=ln2_b, w_ple=w_ple, ple_norm_g=ple_norm_g, w_ple_gate=w_ple_gate)
    mo = dict(w_in=m_w_in, conv_a_w=m_conv_a_w, conv_a_b=m_conv_a_b, ln_a_g=m_ln_a_g, ln_a_b=m_ln_a_b,
              w_a_out=m_w_a_out, ssm_conv_w=m_ssm_conv_w, ssm_conv_b=m_ssm_conv_b, a_log=m_a_log,
              dt_bias=m_dt_bias, d_skip=m_d_skip, ssm_norm_g=m_ssm_norm_g, w_b_out=m_w_b_out, w_o=m_w_o,
              ln1_g=m_ln1_g, ln1_b=m_ln1_b, w_gate_up=m_w_gate_up, w_down=m_w_down, ln2_g=m_ln2_g, ln2_b=m_ln2_b,
              w_ple=m_w_ple, ple_norm_g=m_ple_norm_g, w_ple_gate=m_w_ple_gate)
    vo = dict(w_in=v_w_in, conv_a_w=v_conv_a_w, conv_a_b=v_conv_a_b, ln_a_g=v_ln_a_g, ln_a_b=v_ln_a_b,
              w_a_out=v_w_a_out, ssm_conv_w=v_ssm_conv_w, ssm_conv_b=v_ssm_conv_b, a_log=v_a_log,
              dt_bias=v_dt_bias, d_skip=v_d_skip, ssm_norm_g=v_ssm_norm_g, w_b_out=v_w_b_out, w_o=v_w_o,
              ln1_g=v_ln1_g, ln1_b=v_ln1_b, w_gate_up=v_w_gate_up, w_down=v_w_down, ln2_g=v_ln2_g, ln2_b=v_ln2_b,
              w_ple=v_w_ple, ple_norm_g=v_ple_norm_g, w_ple_gate=v_w_ple_gate)

    L = w_in.shape[0]
    S, D = x.shape[1], x.shape[2]
    CD = conv_a_b.shape[1]
    DI = ssm_norm_g.shape[1]
    XBC = ssm_conv_b.shape[1]
    H = d_skip.shape[1]
    G = (XBC - DI) // (2 * D_STATE)
    F = w_down.shape[1] * 4
    N_IN = w_in.shape[2] * 4
    NM = N_IN - 2 * H
    KA, KB = conv_a_w.shape[1], ssm_conv_w.shape[1]
    assert DI == H * HEAD_DIM and CD == D and DI == 2 * D and XBC == 2 * DI and NM == 2 * CD + 2 * D + DI + XBC
    assert 2 * H <= LANES and S % CHUNK == 0
    tnf = F // 2
    cf = dict(S=S, D=D, CD=CD, DI=DI, XBC=XBC, F=F, H=H, G=G, NM=NM, KA=KA, KB=KB, GW=(H // G) * HEAD_DIM,
              alpha=float((2 * L) ** 0.25), tm=min(512, S), tmx=min(1024, S), tmc=min(256, S), tmr=min(256, S), tmw=min(1024, S),
              tn_in=D, tnf=tnf, nk_f=2, nk_in=NM // DI)

    core = lax.axis_index("c").astype(jnp.int32).reshape(1)
    split_names = [n for n in _BIG if n not in _CONV]

    def layer_weights(l, got):
        full = {}
        for n, g in zip(split_names + _CONV, got):
            if n in _COL_SHARDED:
                full[n] = g.transpose(1, 0, 2).reshape(g.shape[1], 4 * g.shape[2])
            else:
                full[n] = g.reshape(4 * g.shape[1], g.shape[2])
        win = full['w_in']
        in_main = win[:, :NM]
        in_dt = _pad_lanes(win[:, NM:])
        gu = full['w_gate_up']
        W = dict(in_main=in_main, in_dt=in_dt, in_main_T=in_main.T, in_dt_T=in_dt.T,
                 a_out=full['w_a_out'], a_out_T=full['w_a_out'].T,
                 b_out=full['w_b_out'], b_out_T=full['w_b_out'].T,
                 o=full['w_o'], o_T=full['w_o'].T, gate_up=gu, gate_T=gu[:, :F].T, up_T=gu[:, F:].T,
                 down=full['w_down'], down_T=full['w_down'].T, ple=full['w_ple'],
                 ple_gate=full['w_ple_gate'], ple_gate_T=full['w_ple_gate'].T)
        row = lambda v: v.reshape(1, -1)
        head_table = lambda v: jnp.broadcast_to(jnp.pad(v, (0, LANES - H))[:, None], (LANES, LANES))
        sm = dict(conv_a_w=jnp.pad(full['conv_a_w'], ((0, _ceil_to(KA, SUBLANES) - KA), (0, 0))),
                  ssm_conv_w=jnp.pad(full['ssm_conv_w'], ((0, _ceil_to(KB, SUBLANES) - KB), (0, 0))),
                  conv_a_b=row(conv_a_b[l]), ln_a_g=row(ln_a_g[l]), ln_a_b=row(ln_a_b[l]),
                  ssm_conv_b=row(ssm_conv_b[l]), ssm_norm_g=row(ssm_norm_g[l]),
                  ln1_g=row(ln1_g[l]), ln1_b=row(ln1_b[l]), ln2_g=row(ln2_g[l]), ln2_b=row(ln2_b[l]),
                  ple_norm_g=row(ple_norm_g[l]),
                  dtb_f=head_table(dt_bias[l, 0]), dtb_r=head_table(dt_bias[l, 1]),
                  alog_f=head_table(a_log[l, 0]), alog_r=head_table(a_log[l, 1]),
                  dskip_full=row(jnp.repeat(d_skip[l], HEAD_DIM)),
                  dskipT=jnp.broadcast_to(jnp.repeat(d_skip[l], HEAD_DIM)[:, None], (DI, LANES)))
        return W, sm

    def blocks(n, gl):
        g = gl[n]
        if n == 'conv_a_w':
            g = g.sum(axis=1)[:KA]
        elif n == 'ssm_conv_w':
            g = g.sum(axis=1)[:KB]
        if n in _COL_SHARDED:
            return g.reshape(g.shape[0], 4, g.shape[1] // 4).transpose(1, 0, 2)
        return g.reshape(4, g.shape[0] // 4, g.shape[1])

    def core_sums(gl):
        mine = [blocks(n, gl) for n in split_names]
        theirs = core_send_half("core_send_half", mine)
        return [core_sum("core_sum_" + n, core, b, t) for n, b, t in zip(split_names, mine, theirs)]

    def chip_sums(l, parts, acc):
        sums = [chip_sum_into("chip_sum_" + n, core, pr, l, L, into=acc.get(n)) for n, pr in zip(split_names, parts)]
        return dict(zip(split_names, core_fill("core_fill", sums, l, L)))

    def shards(l):
        return [wt[n][l].astype(BF16) for n in split_names]

    lw = [None] * L
    pending = None
    for l in range(L):
        if l < L - 1 or L == 1:
            lw[l] = layer_weights(l, gather_layer("gather_weights", shards(l), [wt[n][l] for n in _CONV]))
    xl = x[0]
    if L > 1:
        sh = shards(L - 1)
        send, recv, sh, lands, token = chip_legs_start(
            "gather_start", 'gather', sh, [lax.empty((4,) + a.shape, a.dtype) for a in sh])
        pending = (send, recv, sh, lands)
        xlb = (xl + token[0, 0]).astype(BF16)
    else:
        xlb = xl.astype(BF16)
    saved = []
    for l in range(L):
        if l == L - 1 and pending is not None:
            send, recv, sh, lands = pending
            landed = chip_legs_wait("gather_wait", 'gather', send, recv, sh, lands, xl)
            conv_got = chip_exchange("gather_conv", [[wt[n][l]] for n in _CONV], gather=True)
            lw[l] = layer_weights(l, list(gather_finish("gather_finish", sh, landed)) + list(conv_got))
        xl, xlb, sv = _layer_fwd(cf, xl, xlb, p[l, 0].astype(BF16), lw[l][0], lw[l][1])
        saved.append(sv)
    grads = [None] * L
    dxl = None
    gsum = {}
    pending = None
    for l in reversed(range(L)):
        sm_l = lw[l][1]
        if pending is not None:
            sm_l = dict(sm_l, ple_norm_g=sm_l['ple_norm_g'] + pending[4][0, 0])
        if l == L - 1:
            dxl, grads[l] = _layer_bwd(cf, saved[l], lw[l][0], sm_l, target=loss_target[0], xn=xl)
        else:
            dxl, grads[l] = _layer_bwd(cf, saved[l], lw[l][0], sm_l, dxn=dxl)
        both = core_sums(grads[l])
        if l == L - 1 and L > 1:
            send, recv, both, lands, token = chip_legs_start(
                "scatter_start", 'scatter', both, [lax.empty(a.shape, a.dtype) for a in both])
            pending = (send, recv, both, lands, token)
            continue
        if pending is not None:
            send, recv, sent, lands, _ = pending
            landed = chip_legs_wait("scatter_wait", 'scatter', send, recv, sent, lands, dxl)
            gsum = chip_sums(L - 1, place_own(sent, landed), gsum)
            pending = None
        parts = chip_exchange("scatter_grads", [[t] for t in both], gather=False)
        gsum = chip_sums(l, [pr.reshape(4, pr.shape[2], pr.shape[3]) for pr in parts], gsum)
    loss = lax.psum(0.5 / D * jnp.sum(grads[L - 1]['loss_sq']), ("x", "y", "c"))
    grad_x = dxl[None]

    res = {}
    for n in split_names:
        shp = wt[n].shape
        flat = lambda a: a.reshape(shp[0] * shp[1], shp[2])
        outs = adamw_full("adamw_" + n, gsum[n], flat(wt[n]), flat(mo[n]), flat(vo[n]))
        res[n] = [o.reshape(shp) for o in [gsum[n]] + list(outs)]
    parts = chip_exchange("scatter_conv", [[blocks(n, grads[l]) for l in range(L)] for n in _CONV], gather=False)
    chip_sums = [sum_chips("chip_sum_" + n, pr.reshape(4, L * pr.shape[2], pr.shape[3])) for n, pr in zip(_CONV, parts)]
    sib_sums = sibling_swap("core_swap", chip_sums)
    for n, mine, sib in zip(_CONV, chip_sums, sib_sums):
        shp = wt[n].shape
        flat = lambda a: a.reshape(shp[0] * shp[1], shp[2])
        outs = adamw_shard("adamw_" + n, mine, sib, flat(wt[n]), flat(mo[n]), flat(vo[n]))
        res[n] = [o.reshape(shp) for o in outs]

    def small_pieces(l):
        gl = grads[l]
        A = -jnp.exp(a_log[l])
        d = dict(gl)
        d_alog = jnp.concatenate([gl['dA_f'].sum(axis=1)[:H] * A[0], gl['dA_r'].sum(axis=1)[:H] * A[1]])
        d['a_log'] = jnp.pad(d_alog[None], ((0, SUBLANES - 1), (0, 0)))
        d['dt_bias'] = gl['dt_bias'][:, :2 * H]
        d['d_skip'] = gl['dskip_full'].reshape(SUBLANES, H, HEAD_DIM).sum(axis=-1)
        return [_pad_lanes(d[n], _ceil_to(d[n].shape[1], LANES)) for n in _SMALL]

    widths = [_ceil_to(math.prod(wt[n].shape[1:]), LANES) for n in _SMALL]
    packed = jnp.concatenate([pc for l in range(L) for pc in small_pieces(l)], axis=1)
    gathered = all8_gather("gather_small", packed)

    def pack_params(src):
        return jnp.concatenate([_pad_lanes(src[n][l].reshape(1, -1), wd) for l in range(L) for n, wd in zip(_SMALL, widths)],
                               axis=1)

    small_out = adamw_small("adamw_small", gathered, pack_params(wt), pack_params(mo), pack_params(vo))
    off = 0
    per = {n: [[] for _ in range(4)] for n in _SMALL}
    for l in range(L):
        for n, wd in zip(_SMALL, widths):
            size = math.prod(wt[n].shape[1:])
            for k in range(4):
                per[n][k].append(small_out[k][0, off:off + size].reshape(wt[n].shape[1:]))
            off += wd
    for n in _SMALL:
        res[n] = [jnp.stack(per[n][k]) for k in range(4)]

    return (loss, grad_x, *[res[n][0] for n in _WEIGHTS], *[res[n][1] for n in _WEIGHTS],
            *[res[n][2] for n in _WEIGHTS], *[res[n][3] for n in _WEIGHTS])
```

```python
import math

import jax
import jax.numpy as jnp
from jax import lax
from jax.experimental import pallas as pl
from jax.experimental.pallas import tpu as pltpu

F32 = jnp.float32
BF16 = jnp.bfloat16

VMEM_LIMIT_BYTES = 56 * 1024 * 1024
LANES = 128
SUBLANES = 8

CHUNK = 128
D_STATE = 128
HEAD_DIM = 64
LN_EPS = 1e-5
RMS_EPS = 1e-6
ADAM_LR = 0.001
ADAM_B1 = 0.9
ADAM_B2 = 0.999
ADAM_EPS = 1e-08
ADAM_WD = 0.01
ADAM_STEP = 10
HALO = 16
MESH = pl.DeviceIdType.MESH


def _params(**kw):
    return pltpu.CompilerParams(vmem_limit_bytes=VMEM_LIMIT_BYTES, **kw)


def _sig(x):
    return jax.nn.sigmoid(x)


def _dsilu(x, s):
    return s * (1.0 + x * (1.0 - s))


def _ln_stats(r):
    mu = jnp.mean(r, axis=-1, keepdims=True)
    xc = r - mu
    var = jnp.mean(xc * xc, axis=-1, keepdims=True)
    rstd = lax.rsqrt(var + LN_EPS)
    return xc * rstd, rstd


def _ln_bwd(dy, xhat, rstd, g):
    dxh = dy * g
    m1 = jnp.mean(dxh, axis=-1, keepdims=True)
    m2 = jnp.mean(dxh * xhat, axis=-1, keepdims=True)
    return rstd * (dxh - m1 - xhat * m2)


def _f32(v):
    return v if v.dtype == F32 else v.astype(F32)


def _rows8(v):
    tm, w = v.shape
    return v.reshape(tm // SUBLANES, SUBLANES, w).sum(axis=0)


def fused_mm(name, prods, extras, epi, row_outs, col_outs=(), *, M, tm, tn, nj=1, nk=1,
             passthrough=None, t_outs=()):
    np_ = len(prods)
    ne = len(extras)
    nro = len(row_outs)
    nco = len(col_outs)
    use_acc = nk > 1

    def body(*refs):
        a_refs = [refs[2 * p] for p in range(np_)]
        w_refs = [refs[2 * p + 1] for p in range(np_)]
        pos = 2 * np_
        e_refs = refs[pos:pos + ne]
        pos += ne
        if passthrough is not None:
            pos += 1
        ro_refs = refs[pos:pos + nro]
        pos += nro
        co_refs = refs[pos:pos + nco]
        pos += nco
        to_refs = refs[pos:pos + len(t_outs)]
        pos += len(t_outs)
        acc_ref = refs[pos] if use_acc else None
        i = pl.program_id(1)
        k = pl.program_id(2)

        def prod(p):
            a = a_refs[p][...]
            if a.dtype != BF16:
                a = a.astype(BF16)
            return jnp.dot(a, w_refs[p][...], preferred_element_type=F32)

        def finish(acc):
            res = epi(acc, [_f32(r[...]) for r in e_refs])
            rows, cols = res[0], res[1]
            for v, o in zip(rows, ro_refs):
                o[...] = v.astype(o.dtype)
            for v, o in zip(res[2] if len(res) > 2 else (), to_refs):
                o[...] = v.T.astype(o.dtype)
            for v, o in zip(cols, co_refs):
                v8 = _rows8(v)

                @pl.when(i == 0)
                def _():
                    o[...] = v8

                @pl.when(i > 0)
                def _():
                    o[...] += v8

        if not use_acc:
            acc = prod(0)
            for p in range(1, np_):
                acc = acc + prod(p)
            finish(acc)
        else:
            @pl.when(k == 0)
            def _():
                acc = None
                for p in range(np_):
                    acc = prod(p) if acc is None else acc + prod(p)
                acc_ref[...] = acc

            @pl.when(k > 0)
            def _():
                acc = None
                for p in range(np_):
                    if prods[p][3]:
                        acc = prod(p) if acc is None else acc + prod(p)
                acc_ref[...] += acc

            @pl.when(k == nk - 1)
            def _():
                finish(acc_ref[...])

    in_specs = []
    args = []
    for a, w, joff, ksplit in prods:
        K = a.shape[1]
        if ksplit:
            tk = K // nk
            in_specs.append(pl.BlockSpec((tm, tk), lambda j, i, k: (i, k)))
            in_specs.append(pl.BlockSpec((tk, tn), lambda j, i, k, joff=joff: (k, j + joff)))
        else:
            in_specs.append(pl.BlockSpec((tm, K), lambda j, i, k: (i, 0)))
            in_specs.append(pl.BlockSpec((K, tn), lambda j, i, k, joff=joff: (0, j + joff)))
        args += [a, w]
    for arr, kind, width, c0 in extras:
        if kind == 'row':
            in_specs.append(pl.BlockSpec((tm, width), lambda j, i, k, c0=c0: (i, c0 + j)))
        else:
            in_specs.append(pl.BlockSpec((arr.shape[0], width), lambda j, i, k, c0=c0: (0, c0 + j)))
        args.append(arr)
    aliases = {}
    if passthrough is not None:
        arr, oidx = passthrough
        in_specs.append(pl.BlockSpec(memory_space=pl.ANY))
        aliases = {len(args): oidx}
        args.append(arr)
    out_shape = []
    out_specs = []
    for n_total, dtype, width, c0 in row_outs:
        out_shape.append(jax.ShapeDtypeStruct((M, n_total), dtype))
        out_specs.append(pl.BlockSpec((tm, width), lambda j, i, k, c0=c0: (i, c0 + j)))
    for n_total, width, c0 in col_outs:
        out_shape.append(jax.ShapeDtypeStruct((SUBLANES, n_total), F32))
        out_specs.append(pl.BlockSpec((SUBLANES, width), lambda j, i, k, c0=c0: (0, c0 + j)))
    for n_total, dtype, width, c0 in t_outs:
        out_shape.append(jax.ShapeDtypeStruct((n_total, M), dtype))
        out_specs.append(pl.BlockSpec((width, tm), lambda j, i, k, c0=c0: (c0 + j, i)))
    scratch = [pltpu.VMEM((tm, tn), F32)] if use_acc else []
    return pl.pallas_call(
        body, name=name, grid=(nj, M // tm, nk), in_specs=in_specs, out_specs=out_specs,
        out_shape=out_shape, scratch_shapes=scratch, input_output_aliases=aliases,
        compiler_params=_params(dimension_semantics=("arbitrary", "arbitrary", "arbitrary")),
    )(*args)


def mm_tn(name, a, b, *, tm, tk, tn):
    M, K = a.shape
    N = b.shape[1]

    def body(a_ref, b_ref, o_ref):
        m = pl.program_id(2)
        p = lax.dot_general(a_ref[...], b_ref[...], (((0,), (0,)), ((), ())),
                            preferred_element_type=F32)

        @pl.when(m == 0)
        def _():
            o_ref[...] = p

        @pl.when(m > 0)
        def _():
            o_ref[...] += p

    return pl.pallas_call(
        body, name=name, grid=(K // tk, N // tn, M // tm),
        in_specs=[pl.BlockSpec((tm, tk), lambda kk, j, m: (m, kk)),
                  pl.BlockSpec((tm, tn), lambda kk, j, m: (m, j))],
        out_specs=pl.BlockSpec((tk, tn), lambda kk, j, m: (kk, j)),
        out_shape=jax.ShapeDtypeStruct((K, N), F32),
        compiler_params=_params(dimension_semantics=("arbitrary", "arbitrary", "arbitrary")),
    )(a, b)


def row_call(name, fn, ins, row_outs, col_outs=(), *, M, tm, nc=1):
    ni = len(ins)
    nro = len(row_outs)

    def body(*refs):
        i = pl.program_id(1)
        vals = [_f32(r[...]) for r in refs[:ni]]
        rows, cols = fn(*vals)
        for v, o in zip(rows, refs[ni:ni + nro]):
            o[...] = v.astype(o.dtype)
        for v, o in zip(cols, refs[ni + nro:]):
            v8 = _rows8(v)

            @pl.when(i == 0)
            def _():
                o[...] = v8

            @pl.when(i > 0)
            def _():
                o[...] += v8

    in_specs = []
    for arr, kind, width, c0, cmul in ins:
        if kind == 'row':
            in_specs.append(pl.BlockSpec((tm, width), lambda cj, i, c0=c0, cmul=cmul: (i, c0 + cmul * cj)))
        else:
            in_specs.append(pl.BlockSpec((arr.shape[0], width), lambda cj, i, c0=c0, cmul=cmul: (0, c0 + cmul * cj)))
    out_shape = []
    out_specs = []
    for n_total, dtype, width, c0, cmul in row_outs:
        out_shape.append(jax.ShapeDtypeStruct((M, n_total), dtype))
        out_specs.append(pl.BlockSpec((tm, width), lambda cj, i, c0=c0, cmul=cmul: (i, c0 + cmul * cj)))
    for n_total, width, c0, cmul in col_outs:
        out_shape.append(jax.ShapeDtypeStruct((SUBLANES, n_total), F32))
        out_specs.append(pl.BlockSpec((SUBLANES, width), lambda cj, i, c0=c0, cmul=cmul: (0, c0 + cmul * cj)))
    return pl.pallas_call(
        body, name=name, grid=(nc, M // tm), in_specs=in_specs, out_specs=out_specs,
        out_shape=out_shape,
        compiler_params=_params(dimension_semantics=("arbitrary", "arbitrary")),
    )(*[a[0] for a in ins])


def conv_call(name, src, src_c0, w, K, epi, extras, row_outs, col_outs=(), *, M, tm, cw, nc,
              reverse, xin=None, passthrough=None, t_outs=(), w_c0=0):
    pad = (K - 1) // 2
    assert pad <= HALO - 1
    R = tm // HALO
    nblk = M // HALO
    n_i = M // tm
    Kp = w.shape[0]
    ne = len(extras)
    nro = len(row_outs)
    nco = len(col_outs)
    rb = 64
    cbw = min(cw, 256)
    n_copies = SUBLANES if K > SUBLANES else 1

    def body(*refs):
        main_ref, prev_ref, next_ref, w_ref = refs[:4]
        pos = 4
        xin_ref = None
        if xin is not None:
            xin_ref = refs[pos]
            pos += 1
        e_refs = refs[pos:pos + ne]
        pos += ne
        if passthrough is not None:
            pos += 1
        ro_refs = refs[pos:pos + nro]
        pos += nro
        co_refs = refs[pos:pos + nco]
        pos += nco
        to_refs = refs[pos:pos + len(t_outs)]
        pos += len(t_outs)
        dw_ref = None
        if xin is not None:
            dw_ref = refs[pos]
            pos += 1
        ext_ref, conv_ref = refs[pos], refs[pos + 1]
        i = pl.program_id(1)

        ext_ref[0, 0:HALO, :] = jnp.where(i == 0, 0.0, prev_ref[...].astype(F32))
        ext_ref[0, HALO:HALO + tm, :] = main_ref[...].astype(F32)
        ext_ref[0, HALO + tm:, :] = jnp.where(i == n_i - 1, 0.0, next_ref[...].astype(F32))
        if dw_ref is not None:
            @pl.when(i == 0)
            def _():
                dw_ref[...] = jnp.zeros_like(dw_ref)

        n_sh = tm + 2 * HALO - SUBLANES
        for c0 in range(0, cw, cbw):
            for sft in range(1, n_copies):
                ext_ref[sft, 0:n_sh, c0:c0 + cbw] = ext_ref[0, sft:sft + n_sh, c0:c0 + cbw]

        for c0 in range(0, cw, cbw):
            for r0 in range(0, tm, rb):
                acc = jnp.zeros((rb, cbw), F32)
                if xin_ref is not None:
                    xblk = xin_ref[r0:r0 + rb, c0:c0 + cbw].astype(F32)
                for k in range(K):
                    off = HALO + r0 + ((pad - k) if reverse else (k - pad))
                    sft = off % SUBLANES if n_copies > 1 else 0
                    d = ext_ref[sft, off - sft:off - sft + rb, c0:c0 + cbw]
                    acc = acc + d * w_ref[k:k + 1, c0:c0 + cbw]
                    if xin_ref is not None:
                        dw_ref[k, :, c0:c0 + cbw] += _rows8(xblk * d)
                conv_ref[r0:r0 + rb, c0:c0 + cbw] = acc

        res = epi(conv_ref[...], [_f32(r[...]) for r in e_refs])
        rows, cols = res[0], res[1]
        for v, o in zip(rows, ro_refs):
            o[...] = v.astype(o.dtype)
        for v, o in zip(res[2] if len(res) > 2 else (), to_refs):
            o[...] = v.T.astype(o.dtype)
        for v, o in zip(cols, co_refs):
            v8 = _rows8(v)

            @pl.when(i == 0)
            def _():
                o[...] = v8

            @pl.when(i > 0)
            def _():
                o[...] += v8

    in_specs = [
        pl.BlockSpec((tm, cw), lambda cj, i: (i, src_c0 + cj)),
        pl.BlockSpec((HALO, cw), lambda cj, i: (jnp.maximum(i * R - 1, 0), src_c0 + cj)),
        pl.BlockSpec((HALO, cw), lambda cj, i: (jnp.minimum((i + 1) * R, nblk - 1), src_c0 + cj)),
        pl.BlockSpec((Kp, cw), lambda cj, i: (0, w_c0 + cj)),
    ]
    args = [src, src, src, w]
    if xin is not None:
        in_specs.append(pl.BlockSpec((tm, cw), lambda cj, i, c0=xin[1]: (i, c0 + cj)))
        args.append(xin[0])
    for arr, kind, width, c0, cmul in extras:
        if kind == 'row':
            in_specs.append(pl.BlockSpec((tm, width), lambda cj, i, c0=c0, cmul=cmul: (i, c0 + cmul * cj)))
        else:
            in_specs.append(pl.BlockSpec((arr.shape[0], width), lambda cj, i, c0=c0, cmul=cmul: (0, c0 + cmul * cj)))
        args.append(arr)
    aliases = {}
    if passthrough is not None:
        in_specs.append(pl.BlockSpec(memory_space=pl.ANY))
        aliases = {len(args): passthrough[1]}
        args.append(passthrough[0])
    out_shape = []
    out_specs = []
    for n_total, dtype, width, c0, cmul in row_outs:
        out_shape.append(jax.ShapeDtypeStruct((M, n_total), dtype))
        out_specs.append(pl.BlockSpec((tm, width), lambda cj, i, c0=c0, cmul=cmul: (i, c0 + cmul * cj)))
    for n_total, width, c0, cmul in col_outs:
        out_shape.append(jax.ShapeDtypeStruct((SUBLANES, n_total), F32))
        out_specs.append(pl.BlockSpec((SUBLANES, width), lambda cj, i, c0=c0, cmul=cmul: (0, c0 + cmul * cj)))
    for n_total, dtype, width, c0, cmul in t_outs:
        out_shape.append(jax.ShapeDtypeStruct((n_total, M), dtype))
        out_specs.append(pl.BlockSpec((width, tm), lambda cj, i, c0=c0, cmul=cmul: (c0 + cmul * cj, i)))
    if xin is not None:
        out_shape.append(jax.ShapeDtypeStruct((Kp, SUBLANES, cw * nc), F32))
        out_specs.append(pl.BlockSpec((Kp, SUBLANES, cw), lambda cj, i: (0, 0, cj)))
    return pl.pallas_call(
        body, name=name, grid=(nc, n_i), in_specs=in_specs, out_specs=out_specs,
        out_shape=out_shape, input_output_aliases=aliases,
        scratch_shapes=[pltpu.VMEM((n_copies, tm + 2 * HALO, cw), F32), pltpu.VMEM((tm, cw), F32)],
        compiler_params=_params(dimension_semantics=("arbitrary", "arbitrary")),
    )(*args)


def _split_dot(m_bf16, v, n_pass, dims=None):
    out = None
    rest = v
    for p in range(n_pass):
        piece = rest.astype(BF16)
        if p + 1 < n_pass:
            rest = rest - piece.astype(F32)
        if dims is None:
            t = jnp.dot(m_bf16, piece, preferred_element_type=F32)
        else:
            t = lax.dot_general(m_bf16, piece, dims, preferred_element_type=F32)
        out = t if out is None else out + t
    return out


def _split_dot_r(v, m_bf16, n_pass):
    out = None
    rest = v
    for p in range(n_pass):
        piece = rest.astype(BF16)
        if p + 1 < n_pass:
            rest = rest - piece.astype(F32)
        t = jnp.dot(piece, m_bf16, preferred_element_type=F32)
        out = t if out is None else out + t
    return out


def _softplus(x):
    return jnp.maximum(x, 0.0) + jnp.log1p(jnp.exp(-jnp.abs(x)))


NT_DIMS = (((1,), (1,)), ((), ()))
TN_DIMS = (((0,), (0,)), ((), ()))


def _ssd_common(dtraw, dtbT, alogT, rev, n_heads):
    L = CHUNK
    if rev:
        dtraw = pltpu.roll(dtraw, LANES - n_heads, 1)
    preT = dtraw.T + dtbT
    dtT = _softplus(preT)
    AT = -jnp.exp(alogT)
    aT = dtT * AT
    ri = lax.broadcasted_iota(jnp.int32, (L, L), 0)
    ci = lax.broadcasted_iota(jnp.int32, (L, L), 1)
    up = (ri >= ci) if rev else (ri <= ci)
    lo = (ri <= ci) if rev else (ri >= ci)
    csT = _split_dot_r(aT, up.astype(BF16), 3)
    last = 0 if rev else L - 1
    lastB = jnp.broadcast_to(csT[:, last:last + 1], (L, L))
    return dict(preT=preT, dtT=dtT, AT=AT, csT=csT, cs=csT.T, up=up, lo=lo, ci=ci, last=last,
                doutT=jnp.exp(csT), dstT=jnp.exp(lastB - csT), totB=jnp.exp(lastB))


def ssd_fwd(name, xsT, bc, dtraw, dtbT, alogT, *, S, DI, G, H, rev, tail=None):
    NC = S // CHUNK
    R = H // G
    GW = R * HEAD_DIM
    N = D_STATE
    P = HEAD_DIM

    def body(*refs):
        xsT_ref, bc_ref, dtraw_ref, dtb_ref, alog_ref = refs[:5]
        if tail is None:
            y_ref, st_ref, h_ref = refs[5:]
        else:
            yo_ref, z_ref, xs_ref, dsk_ref, ng_ref = refs[5:10]
            y_ref, st_ref, yn_ref, h_ref = refs[10:]
        c = pl.program_id(0)

        @pl.when(c == 0)
        def _():
            h_ref[...] = jnp.zeros_like(h_ref)

        q = _ssd_common(dtraw_ref[...], dtb_ref[...], alog_ref[...], rev, H)
        cs, csT, dtT, doutT, totB = q['cs'], q['csT'], q['dtT'], q['doutT'], q['totB']
        wstT = q['dstT'] * dtT
        GB = 2 if G % 2 == 0 else 1
        for g0 in range(0, G, GB):
            gs = list(range(g0, g0 + GB))
            Bgs = [bc_ref[:, g * N:(g + 1) * N].astype(BF16) for g in gs]
            Cgs = [bc_ref[:, G * N + g * N:G * N + (g + 1) * N].astype(BF16) for g in gs]
            CBTs = [lax.dot_general(b, c_, NT_DIMS, preferred_element_type=F32) for b, c_ in zip(Bgs, Cgs)]
            HTs = [h_ref[g] for g in gs]
            yoffTs = [lax.dot_general(HT.astype(BF16), c_, NT_DIMS, preferred_element_type=F32)
                      for HT, c_ in zip(HTs, Cgs)]
            xTs = [xsT_ref[g * GW:(g + 1) * GW, :] for g in gs]
            heads = [(k, r) for k in range(GB) for r in range(R)]
            hs = [gs[k] * R + r for k, r in heads]
            blks = [slice(r * P, (r + 1) * P) for _, r in heads]
            segs = [jnp.where(q['up'], csT[h:h + 1, :] - cs[:, h:h + 1], -1e30) for h in hs]
            GTs = [(CBTs[k] * jnp.exp(sg)).astype(BF16) for (k, _), sg in zip(heads, segs)]
            xThs = [xTs[k][b, :] for (k, _), b in zip(heads, blks)]
            XThs = [(xTh * dtT[h:h + 1, :]).astype(BF16) for xTh, h in zip(xThs, hs)]
            ydTs = [jnp.dot(a, GT, preferred_element_type=F32) for a, GT in zip(XThs, GTs)]
            ys = [ydT + yoffTs[k][b, :] * doutT[h:h + 1, :] for ydT, (k, _), b, h in zip(ydTs, heads, blks, hs)]
            xws = [xTh * wstT[h:h + 1, :] for xTh, h in zip(xThs, hs)]
            tots = [jnp.broadcast_to(totB[h:h + 1, :], (P, N)) for h in hs]
            for k, g in enumerate(gs):
                sel = slice(k * R, (k + 1) * R)
                y_ref[:, g * GW:(g + 1) * GW] = jnp.concatenate(ys[sel], axis=0).T
                xwT = jnp.concatenate(xws[sel], axis=0).astype(BF16)
                ST = jnp.dot(xwT, Bgs[k], preferred_element_type=F32)
                st_ref[0, g] = HTs[k]
                h_ref[g] = HTs[k] * jnp.concatenate(tots[sel], axis=0) + ST
        if tail is not None:
            y = y_ref[...] + yo_ref[...]
            y_ref[...] = y
            z = _f32(z_ref[...])
            yz = (y + xs_ref[...] * dsk_ref[...]) * (z * _sig(z))
            for g in range(G):
                t = yz[:, g * GW:(g + 1) * GW]
                tn = t * lax.rsqrt(jnp.mean(t * t, axis=-1, keepdims=True) + RMS_EPS)
                yn_ref[:, g * GW:(g + 1) * GW] = (tn * ng_ref[:, g * GW:(g + 1) * GW]).astype(BF16)

    cidx = (lambda c: NC - 1 - c) if rev else (lambda c: c)
    cmap = lambda c: (cidx(c), 0)
    smap = lambda c: (cidx(c), 0, 0, 0)
    const = lambda c: (0, 0)
    tmap = lambda c: (0, cidx(c))
    in_specs = [pl.BlockSpec((DI, CHUNK), tmap), pl.BlockSpec((CHUNK, 2 * G * N), cmap), pl.BlockSpec((CHUNK, LANES), cmap),
                pl.BlockSpec((LANES, LANES), const), pl.BlockSpec((LANES, LANES), const)]
    out_specs = [pl.BlockSpec((CHUNK, DI), cmap), pl.BlockSpec((1, G, GW, N), smap)]
    out_shape = [jax.ShapeDtypeStruct((S, DI), F32), jax.ShapeDtypeStruct((NC, G, GW, N), F32)]
    args = [xsT, bc, dtraw, dtbT, alogT]
    if tail is not None:
        y_other, (z_arr, z_blk), xs_row, dsk, ng = tail
        in_specs += [pl.BlockSpec((CHUNK, DI), cmap), pl.BlockSpec((CHUNK, DI), lambda c: (cidx(c), z_blk)),
                     pl.BlockSpec((CHUNK, DI), cmap), pl.BlockSpec((1, DI), const), pl.BlockSpec((1, DI), const)]
        out_specs.append(pl.BlockSpec((CHUNK, DI), cmap))
        out_shape.append(jax.ShapeDtypeStruct((S, DI), BF16))
        args += [y_other, z_arr, xs_row, dsk, ng]
    return pl.pallas_call(
        body, name=name, grid=(NC,), in_specs=in_specs, out_specs=out_specs, out_shape=out_shape,
        scratch_shapes=[pltpu.VMEM((G, GW, N), F32)],
        compiler_params=_params(dimension_semantics=("arbitrary",)),
    )(*args)


def ssd_bwd(name, xsT, bc, dtraw, dyT, st, dtbT, alogT, *, S, DI, G, H, rev, tail=None):
    NC = S // CHUNK
    R = H // G
    GW = R * HEAD_DIM
    N = D_STATE
    XBC = DI + 2 * G * N
    P = HEAD_DIM
    L = CHUNK

    def body(*refs):
        xsT_ref, bc_ref, dtraw_ref, dyT_ref, st_ref, dtb_ref, alog_ref = refs[:7]
        if tail is None:
            dxbc_ref, ddt_ref, da_ref, dh_ref, dcst_ref, p2t_ref, p3t_ref, e2t_ref = refs[7:]
        else:
            other_ref, cbx_ref, cbbc_ref, dskT_ref = refs[7:11]
            dxbc_ref, ddt_ref, da_ref, dcol_ref, dh_ref, dcst_ref, p2t_ref, p3t_ref, e2t_ref = refs[11:]
        c = pl.program_id(0)

        @pl.when(c == 0)
        def _():
            dh_ref[...] = jnp.zeros_like(dh_ref)
            da_ref[...] = jnp.zeros_like(da_ref)
            dcst_ref[...] = jnp.zeros_like(dcst_ref)
            p2t_ref[...] = jnp.zeros_like(p2t_ref)
            p3t_ref[...] = jnp.zeros_like(p3t_ref)
            e2t_ref[...] = jnp.zeros_like(e2t_ref)

        q = _ssd_common(dtraw_ref[...], dtb_ref[...], alog_ref[...], rev, H)
        cs, csT, dtT, doutT, dstT, totB = q['cs'], q['csT'], q['dtT'], q['doutT'], q['dstT'], q['totB']
        wstT = dstT * dtT
        lane = q['ci']
        GB = 2 if G % 2 == 0 else 1
        for g0 in range(0, G, GB):
            gs = list(range(g0, g0 + GB))
            Bgs = [bc_ref[:, g * N:(g + 1) * N].astype(BF16) for g in gs]
            Cgs = [bc_ref[:, G * N + g * N:G * N + (g + 1) * N].astype(BF16) for g in gs]
            CBs = [lax.dot_general(c_, b, NT_DIMS, preferred_element_type=F32) for b, c_ in zip(Bgs, Cgs)]
            HpTs = [st_ref[0, g] for g in gs]
            HpTbs = [v.astype(BF16) for v in HpTs]
            dHTs = [dh_ref[g] for g in gs]
            dHTbs = [v.astype(BF16) for v in dHTs]
            BdHTs = [lax.dot_general(d, b, NT_DIMS, preferred_element_type=F32) for d, b in zip(dHTbs, Bgs)]
            yoffTs = [lax.dot_general(hp, c_, NT_DIMS, preferred_element_type=F32) for hp, c_ in zip(HpTbs, Cgs)]
            xTs = [xsT_ref[g * GW:(g + 1) * GW, :] for g in gs]
            dyTs = [dyT_ref[g * GW:(g + 1) * GW, :] for g in gs]
            heads = [(k, r) for k in range(GB) for r in range(R)]
            ks = [k for k, _ in heads]
            hs = [gs[k] * R + r for k, r in heads]
            blks = [slice(r * P, (r + 1) * P) for _, r in heads]
            Lms = [jnp.exp(jnp.where(q['lo'], cs[:, h:h + 1] - csT[h:h + 1, :], -1e30)) for h in hs]
            xThs = [xTs[k][b, :] for k, b in zip(ks, blks)]
            dyThs = [dyTs[k][b, :] for k, b in zip(ks, blks)]
            xThbs = [v.astype(BF16) for v in xThs]
            dyThbs = [v.astype(BF16) for v in dyThs]
            dGxs = [lax.dot_general(a, b, TN_DIMS, preferred_element_type=F32) for a, b in zip(dyThbs, xThbs)]
            Gms = [(CBs[k] * Lm).astype(BF16) for k, Lm in zip(ks, Lms)]
            XThbs = [(xTh * dtT[h:h + 1, :]).astype(BF16) for xTh, h in zip(xThs, hs)]
            u1Ts = [jnp.dot(a, Gm, preferred_element_type=F32) for a, Gm in zip(dyThbs, Gms)]
            ydTs = [lax.dot_general(a, Gm, NT_DIMS, preferred_element_type=F32) for a, Gm in zip(XThbs, Gms)]
            Ts = [dGx * (Lm * dtT[h:h + 1, :]) for dGx, Lm, h in zip(dGxs, Lms, hs)]
            uTs = [u1T + BdHTs[k][b, :] * dstT[h:h + 1, :] for u1T, k, b, h in zip(u1Ts, ks, blks, hs)]
            dyds = [dyTh * doutT[h:h + 1, :] for dyTh, h in zip(dyThs, hs)]
            xws = [xTh * wstT[h:h + 1, :] for xTh, h in zip(xThs, hs)]
            for i, h in enumerate(hs):
                k, b = ks[i], blks[i]
                p3row = jnp.sum(xws[i] * BdHTs[k][b, :], axis=0, keepdims=True)
                seg_row = jnp.sum(_f32(dyThbs[i]) * ydTs[i], axis=0, keepdims=True)
                seg_col = jnp.sum(_f32(XThbs[i]) * u1Ts[i], axis=0, keepdims=True)
                dcst_ref[h:h + 1, :] = (jnp.sum(dyds[i] * yoffTs[k][b, :], axis=0, keepdims=True)
                                        + seg_row - seg_col - p3row)
                p2t_ref[h:h + 1, :] = jnp.sum(xThs[i] * uTs[i], axis=0, keepdims=True)
                p3t_ref[h:h + 1, :] = p3row
                e2t_ref[h:h + 1, :] = jnp.sum(HpTs[k][b, :] * dHTs[k][b, :], axis=0, keepdims=True)
            dxs = [uT * dtT[h:h + 1, :] for uT, h in zip(uTs, hs)]
            if tail is not None:
                dxs = [d + dyTh * dskT_ref[h * P:(h + 1) * P, :] for d, dyTh, h in zip(dxs, dyThs, hs)]
            tots = [jnp.broadcast_to(totB[h:h + 1, :], (P, N)) for h in hs]
            for k, g in enumerate(gs):
                sel = slice(k * R, (k + 1) * R)
                dCB = Ts[k * R]
                for T in Ts[k * R + 1:(k + 1) * R]:
                    dCB = dCB + T
                dxbc_ref[:, g * GW:(g + 1) * GW] = jnp.concatenate(dxs[sel], axis=0).T
                dydT = jnp.concatenate(dyds[sel], axis=0).astype(BF16)
                xwT = jnp.concatenate(xws[sel], axis=0).astype(BF16)
                dCBb = dCB.astype(BF16)
                dC = (jnp.dot(dCBb, Bgs[k], preferred_element_type=F32)
                      + lax.dot_general(dydT, HpTbs[k], TN_DIMS, preferred_element_type=F32))
                dB = (lax.dot_general(dCBb, Cgs[k], TN_DIMS, preferred_element_type=F32)
                      + lax.dot_general(xwT, dHTbs[k], TN_DIMS, preferred_element_type=F32))
                dxbc_ref[:, DI + g * N:DI + (g + 1) * N] = dB
                dxbc_ref[:, DI + G * N + g * N:DI + G * N + (g + 1) * N] = dC
                dh_ref[g] = (dHTs[k] * jnp.concatenate(tots[sel], axis=0)
                             + jnp.dot(dydT, Cgs[k], preferred_element_type=F32))
        e1 = jnp.sum(p3t_ref[...], axis=1, keepdims=True)
        e2 = jnp.sum(e2t_ref[...], axis=1, keepdims=True)
        dcsT = dcst_ref[...] + jnp.where(lane == q['last'], e1 + totB * e2, 0.0)
        daT = _split_dot_r(dcsT, q['lo'].astype(BF16), 3)
        ddtT = daT * q['AT'] + p2t_ref[...]
        da_ref[...] += daT * dtT
        ddraw = jnp.where(lane < H, (ddtT * _sig(q['preT'])).T, 0.0)
        if rev:
            ddraw = pltpu.roll(ddraw, H, 1)
        ddt_ref[...] = ddraw
        if tail is not None:
            for c0, cb_ref in ((0, cbx_ref), (DI, cbbc_ref)):
                d = dxbc_ref[:, c0:c0 + DI] + other_ref[:, c0:c0 + DI]
                cb = cb_ref[...]
                dcb = d * _dsilu(cb, _sig(cb))
                dxbc_ref[:, c0:c0 + DI] = dcb
                part = _rows8(dcb)

                @pl.when(c == 0)
                def _():
                    dcol_ref[:, c0:c0 + DI] = part

                @pl.when(c > 0)
                def _():
                    dcol_ref[:, c0:c0 + DI] += part

    cmap = (lambda c: (c, 0)) if rev else (lambda c: (NC - 1 - c, 0))
    smap = (lambda c: (c, 0, 0, 0)) if rev else (lambda c: (NC - 1 - c, 0, 0, 0))
    const = lambda c: (0, 0)
    sq = pltpu.VMEM((LANES, CHUNK), F32)
    cix = (lambda c: c) if rev else (lambda c: NC - 1 - c)
    tmap = lambda c: (0, cix(c))
    in_specs = [pl.BlockSpec((DI, CHUNK), tmap), pl.BlockSpec((CHUNK, 2 * G * N), cmap), pl.BlockSpec((CHUNK, LANES), cmap),
                pl.BlockSpec((DI, CHUNK), tmap),
                pl.BlockSpec((1, G, GW, N), smap),
                pl.BlockSpec((LANES, LANES), const), pl.BlockSpec((LANES, LANES), const)]
    out_specs = [pl.BlockSpec((CHUNK, XBC), cmap), pl.BlockSpec((CHUNK, LANES), cmap),
                 pl.BlockSpec((LANES, LANES), const)]
    out_shape = [jax.ShapeDtypeStruct((S, XBC), F32), jax.ShapeDtypeStruct((S, LANES), F32),
                 jax.ShapeDtypeStruct((LANES, LANES), F32)]
    args = [xsT, bc, dtraw, dyT, st, dtbT, alogT]
    if tail is not None:
        in_specs += [pl.BlockSpec((CHUNK, XBC), cmap), pl.BlockSpec((CHUNK, DI), cmap),
                     pl.BlockSpec((CHUNK, 2 * G * N), cmap), pl.BlockSpec((DI, LANES), const)]
        out_specs.append(pl.BlockSpec((SUBLANES, XBC), const))
        out_shape.append(jax.ShapeDtypeStruct((SUBLANES, XBC), F32))
        args += list(tail)
    return pl.pallas_call(
        body, name=name, grid=(NC,), in_specs=in_specs, out_specs=out_specs, out_shape=out_shape,
        scratch_shapes=[pltpu.VMEM((G, GW, N), F32), sq, sq, sq, sq],
        compiler_params=_params(dimension_semantics=("arbitrary",)),
    )(*args)


ANY = pl.BlockSpec(memory_space=pl.ANY)


def chip_exchange(name, groups, gather):
    flat = [arr for grp in groups for arr in grp]
    n_in = len(flat)
    n_out = len(groups)
    n_rc = 3 * n_in

    def body(*refs):
        in_refs = refs[:n_in]
        out_refs = refs[n_in:n_in + n_out]
        send, recv = refs[n_in + n_out:]
        x, y, c = lax.axis_index("x"), lax.axis_index("y"), lax.axis_index("c")
        me = 2 * x + y
        peers = [(1 - x, y), (x, 1 - y), (1 - x, 1 - y)]
        remote = []
        q = 0
        for a, grp in enumerate(groups):
            for l in range(len(grp)):
                src = in_refs[q]
                dst = out_refs[a].at[me] if gather else out_refs[a].at[me, l]
                for j, (px, py) in enumerate(peers):
                    blk = src if gather else src.at[2 * px + py]
                    rc = pltpu.make_async_remote_copy(
                        src_ref=blk, dst_ref=dst, send_sem=send.at[3 * q + j], recv_sem=recv.at[3 * q + j],
                        device_id=(px, py, c), device_id_type=MESH)
                    rc.start()
                    remote.append(rc)
                q += 1
        for rc in remote:
            rc.wait()

    out_shape = []
    for grp in groups:
        a0 = grp[0]
        if gather:
            out_shape.append(jax.ShapeDtypeStruct((4,) + a0.shape, a0.dtype))
        else:
            out_shape.append(jax.ShapeDtypeStruct((4, len(grp)) + a0.shape[1:], a0.dtype))
    outs = pl.pallas_call(
        body, name=name, in_specs=[ANY] * n_in, out_specs=[ANY] * n_out, out_shape=out_shape,
        scratch_shapes=[pltpu.SemaphoreType.DMA((n_rc,)), pltpu.SemaphoreType.DMA((n_rc,))],
    )(*flat)
    me = _chip_index()
    res = []
    for grp, o in zip(groups, outs):
        for l, src in enumerate(grp):
            o = _put_block(o, src, (me,)) if gather else _put_block(o, _take_block(src, me), (me, l))
        res.append(o)
    return res


def _chip_index():
    return 2 * lax.axis_index("x") + lax.axis_index("y")


def _take_block(arr, idx):
    return lax.dynamic_index_in_dim(arr, idx, 0, keepdims=False)


def _put_block(dst, blk, idx):
    lead = len(idx)
    return lax.dynamic_update_slice(dst, blk.reshape((1,) * lead + blk.shape), tuple(idx) + (0,) * (dst.ndim - lead))


def gather_layer(name, split, whole):
    ns, nw = len(split), len(whole)
    n = ns + nw
    n_rc = 3 * (n + ns)

    def body(*refs):
        in_refs = refs[:n]
        out_refs = refs[n:2 * n]
        send, recv = refs[2 * n:]
        x, y, c = lax.axis_index("x"), lax.axis_index("y"), lax.axis_index("c")
        me = 2 * x + y
        sibling = (x, y, 1 - c)
        peers = [(1 - x, y), (x, 1 - y), (1 - x, 1 - y)]

        def region(a, chip, half):
            if a >= ns:
                return out_refs[a].at[chip]
            hr = split[a].shape[0] // 2
            return out_refs[a].at[chip, pl.ds(half * hr, hr)]

        def mine(a):
            if a >= ns:
                return in_refs[a]
            hr = split[a].shape[0] // 2
            return in_refs[a].at[pl.ds(c * hr, hr)]

        sends = []
        for a in range(n):
            for j, (px, py) in enumerate(peers):
                rc = pltpu.make_async_remote_copy(
                    src_ref=mine(a), dst_ref=region(a, me, c), send_sem=send.at[3 * a + j],
                    recv_sem=recv.at[3 * a + j], device_id=(px, py, c), device_id_type=MESH)
                rc.start()
                sends.append(rc)
        for a in range(n):
            for j, (px, py) in enumerate(peers):
                chip = 2 * px + py
                landed = pltpu.make_async_remote_copy(
                    src_ref=mine(a), dst_ref=region(a, chip, c), send_sem=send.at[3 * a + j],
                    recv_sem=recv.at[3 * a + j], device_id=(px, py, c), device_id_type=MESH)
                landed.wait_recv()
                if a < ns:
                    fw = pltpu.make_async_remote_copy(
                        src_ref=region(a, chip, c), dst_ref=region(a, chip, c), send_sem=send.at[3 * n + 3 * a + j],
                        recv_sem=recv.at[3 * n + 3 * a + j], device_id=sibling, device_id_type=MESH)
                    fw.start()
                    sends.append(fw)
        for a in range(ns):
            for j, (px, py) in enumerate(peers):
                chip = 2 * px + py
                pltpu.make_async_remote_copy(
                    src_ref=region(a, chip, 1 - c), dst_ref=region(a, chip, 1 - c), send_sem=send.at[3 * n + 3 * a + j],
                    recv_sem=recv.at[3 * n + 3 * a + j], device_id=sibling, device_id_type=MESH).wait_recv()
        for rc in sends:
            rc.wait_send()

    arrs = list(split) + list(whole)
    outs = pl.pallas_call(
        body, name=name, in_specs=[ANY] * n, out_specs=[ANY] * n,
        out_shape=[jax.ShapeDtypeStruct((4,) + a.shape, a.dtype) for a in arrs],
        scratch_shapes=[pltpu.SemaphoreType.DMA((n_rc,)), pltpu.SemaphoreType.DMA((n_rc,))],
    )(*arrs)
    me = _chip_index()
    return [_put_block(o, a, (me,)) for o, a in zip(outs, arrs)]


HBM_SPEC = pl.BlockSpec(memory_space=pltpu.HBM)
SEM_SPEC = pl.BlockSpec(memory_space=pltpu.SEMAPHORE)
IN_FLIGHT = pltpu.SideEffectType.DATAFLOW_SIDE_EFFECTING


def _chip_leg(kind, a_ref, l_ref, shape, c, me, chip):
    if kind == 'gather':
        hr = shape[0] // 2
        rows = pl.ds(c * hr, hr)
        return a_ref.at[rows], l_ref.at[me, rows], l_ref.at[chip, rows]
    return a_ref.at[chip], l_ref.at[me], l_ref.at[chip]


def chip_legs_start(name, kind, arrs, lands):
    n = len(arrs)

    def body(*refs):
        a_refs = refs[:n]
        l_refs = refs[n:2 * n]
        send, recv = refs[2 * n], refs[2 * n + 1]
        token = refs[-1]
        x, y, c = lax.axis_index("x"), lax.axis_index("y"), lax.axis_index("c")
        me = 2 * x + y
        for a in range(n):
            for j, (px, py) in enumerate([(1 - x, y), (x, 1 - y), (1 - x, 1 - y)]):
                src, dst, _ = _chip_leg(kind, a_refs[a], l_refs[a], arrs[a].shape, c, me, 2 * px + py)
                pltpu.make_async_remote_copy(src_ref=src, dst_ref=dst, send_sem=send.at[3 * a + j],
                                             recv_sem=recv.at[3 * a + j], device_id=(px, py, c),
                                             device_id_type=MESH).start()
        token[...] = jnp.zeros_like(token)

    both = list(arrs) + list(lands)
    outs = pl.pallas_call(
        body, name=name,
        out_shape=(pltpu.SemaphoreType.DMA((3 * n,)), pltpu.SemaphoreType.DMA((3 * n,)),
                   *[pltpu.HBM(a.shape, a.dtype) for a in both], jax.ShapeDtypeStruct((SUBLANES, LANES), F32)),
        in_specs=[HBM_SPEC] * (2 * n),
        out_specs=(SEM_SPEC, SEM_SPEC, *[HBM_SPEC] * (2 * n), pl.BlockSpec(memory_space=pltpu.VMEM)),
        input_output_aliases={i: 2 + i for i in range(2 * n)},
        compiler_params=pltpu.CompilerParams(has_side_effects=IN_FLIGHT),
    )(*[pltpu.with_memory_space_constraint(a, pltpu.HBM) for a in both])
    return outs[0], outs[1], list(outs[2:2 + n]), list(outs[2 + n:2 + 2 * n]), outs[-1]


def chip_legs_wait(name, kind, send, recv, arrs, lands, after):
    n = len(arrs)

    def body(*refs):
        a_refs = refs[:n]
        l_refs = refs[n:2 * n]
        send_, recv_ = refs[2 * n], refs[2 * n + 1]
        x, y, c = lax.axis_index("x"), lax.axis_index("y"), lax.axis_index("c")
        me = 2 * x + y
        legs = []
        for a in range(n):
            for j, (px, py) in enumerate([(1 - x, y), (x, 1 - y), (1 - x, 1 - y)]):
                src, dst, landing = _chip_leg(kind, a_refs[a], l_refs[a], arrs[a].shape, c, me, 2 * px + py)
                legs.append(pltpu.make_async_remote_copy(src_ref=src, dst_ref=landing, send_sem=send_.at[3 * a + j],
                                                         recv_sem=recv_.at[3 * a + j], device_id=(px, py, c),
                                                         device_id_type=MESH))
        for leg in legs:
            leg.wait_send()
        for leg in legs:
            leg.wait_recv()

    both = list(arrs) + list(lands)
    outs = pl.pallas_call(
        body, name=name, out_shape=tuple(pltpu.HBM(a.shape, a.dtype) for a in both),
        in_specs=[HBM_SPEC] * (2 * n) + [SEM_SPEC, SEM_SPEC, ANY], out_specs=tuple([HBM_SPEC] * (2 * n)),
        input_output_aliases={i: i for i in range(2 * n)},
        compiler_params=pltpu.CompilerParams(has_side_effects=IN_FLIGHT),
    )(*both, send, recv, after)
    return list(outs[n:])


def gather_finish(name, split, landed):
    n = len(split)

    def body(*refs):
        out_refs = refs[n:2 * n]
        send, recv = refs[2 * n:]
        x, y, c = lax.axis_index("x"), lax.axis_index("y"), lax.axis_index("c")
        sibling = (x, y, 1 - c)
        chips = [2 * (1 - x) + y, 2 * x + (1 - y), 2 * (1 - x) + (1 - y)]

        def region(a, chip, half):
            hr = split[a].shape[0] // 2
            return out_refs[a].at[chip, pl.ds(half * hr, hr)]

        sends = []
        for a in range(n):
            for j, chip in enumerate(chips):
                fw = pltpu.make_async_remote_copy(
                    src_ref=region(a, chip, c), dst_ref=region(a, chip, c), send_sem=send.at[3 * a + j],
                    recv_sem=recv.at[3 * a + j], device_id=sibling, device_id_type=MESH)
                fw.start()
                sends.append(fw)
        for a in range(n):
            for j, chip in enumerate(chips):
                pltpu.make_async_remote_copy(
                    src_ref=region(a, chip, 1 - c), dst_ref=region(a, chip, 1 - c), send_sem=send.at[3 * a + j],
                    recv_sem=recv.at[3 * a + j], device_id=sibling, device_id_type=MESH).wait_recv()
        for fw in sends:
            fw.wait_send()

    outs = pl.pallas_call(
        body, name=name, in_specs=[ANY] * n, out_specs=[ANY] * n,
        out_shape=[jax.ShapeDtypeStruct(a.shape, a.dtype) for a in landed],
        input_output_aliases={a: a for a in range(n)},
        scratch_shapes=[pltpu.SemaphoreType.DMA((3 * n,)), pltpu.SemaphoreType.DMA((3 * n,))],
    )(*landed)
    me = _chip_index()
    return [_put_block(o, a, (me,)) for o, a in zip(outs, split)]


def place_own(arrs, landed):
    me = _chip_index()
    return [_put_block(l, _take_block(a, me), (me,)) for a, l in zip(arrs, landed)]


def core_send_half(name, arrs):
    n = len(arrs)

    def body(*refs):
        in_refs = refs[:n]
        out_refs = refs[n:2 * n]
        send, recv = refs[2 * n:]
        c = lax.axis_index("c")
        peer = (lax.axis_index("x"), lax.axis_index("y"), 1 - c)
        rcs = []
        for a in range(n):
            hr = arrs[a].shape[1] // 2
            rc = pltpu.make_async_remote_copy(
                src_ref=in_refs[a].at[:, pl.ds((1 - c) * hr, hr)], dst_ref=out_refs[a], send_sem=send.at[a],
                recv_sem=recv.at[a], device_id=peer, device_id_type=MESH)
            rc.start()
            rcs.append(rc)
        for rc in rcs:
            rc.wait()

    return pl.pallas_call(
        body, name=name, in_specs=[ANY] * n, out_specs=[ANY] * n,
        out_shape=[jax.ShapeDtypeStruct((4, a.shape[1] // 2, a.shape[2]), a.dtype) for a in arrs],
        scratch_shapes=[pltpu.SemaphoreType.DMA((n,)), pltpu.SemaphoreType.DMA((n,))],
    )(*arrs)


def core_fill(name, arrs, layer, n_layers):
    n = len(arrs)

    def body(*refs):
        out_refs = refs[n:2 * n]
        send, recv = refs[2 * n:]
        c = lax.axis_index("c")
        peer = (lax.axis_index("x"), lax.axis_index("y"), 1 - c)
        rcs = []
        for a in range(n):
            r = arrs[a].shape[0] // n_layers
            hr = r // 2
            rows = out_refs[a].at[pl.ds(layer * r + c * hr, hr)]
            rc = pltpu.make_async_remote_copy(src_ref=rows, dst_ref=rows, send_sem=send.at[a], recv_sem=recv.at[a],
                                              device_id=peer, device_id_type=MESH)
            rc.start()
            rcs.append(rc)
        for a in range(n):
            r = arrs[a].shape[0] // n_layers
            hr = r // 2
            theirs = out_refs[a].at[pl.ds(layer * r + (1 - c) * hr, hr)]
            pltpu.make_async_remote_copy(src_ref=theirs, dst_ref=theirs, send_sem=send.at[a], recv_sem=recv.at[a],
                                         device_id=peer, device_id_type=MESH).wait_recv()
        for rc in rcs:
            rc.wait_send()

    return pl.pallas_call(
        body, name=name, in_specs=[ANY] * n, out_specs=[ANY] * n,
        out_shape=[jax.ShapeDtypeStruct(a.shape, a.dtype) for a in arrs],
        input_output_aliases={a: a for a in range(n)},
        scratch_shapes=[pltpu.SemaphoreType.DMA((n,)), pltpu.SemaphoreType.DMA((n,))],
    )(*arrs)


def sibling_swap(name, arrs):
    n = len(arrs)

    def body(*refs):
        in_refs = refs[:n]
        out_refs = refs[n:2 * n]
        send, recv = refs[2 * n:]
        peer = (lax.axis_index("x"), lax.axis_index("y"), 1 - lax.axis_index("c"))
        rcs = []
        for a in range(n):
            rc = pltpu.make_async_remote_copy(src_ref=in_refs[a], dst_ref=out_refs[a], send_sem=send.at[a],
                                              recv_sem=recv.at[a], device_id=peer, device_id_type=MESH)
            rc.start()
            rcs.append(rc)
        for rc in rcs:
            rc.wait()

    return pl.pallas_call(
        body, name=name, in_specs=[ANY] * n, out_specs=[ANY] * n,
        out_shape=[jax.ShapeDtypeStruct(a.shape, a.dtype) for a in arrs],
        scratch_shapes=[pltpu.SemaphoreType.DMA((n,)), pltpu.SemaphoreType.DMA((n,))],
    )(*arrs)


def all8_gather(name, v):
    flips = [(fx, fy, fc) for fx in (0, 1) for fy in (0, 1) for fc in (0, 1) if (fx, fy, fc) != (0, 0, 0)]

    def body(v_ref, out_ref, send, recv, loc):
        x, y, c = lax.axis_index("x"), lax.axis_index("y"), lax.axis_index("c")
        me = 4 * x + 2 * y + c
        lc = pltpu.make_async_copy(v_ref, out_ref.at[me], loc)
        lc.start()
        rcs = []
        for k, (fx, fy, fc) in enumerate(flips):
            tgt = (x + fx - 2 * x * fx, y + fy - 2 * y * fy, c + fc - 2 * c * fc)
            rc = pltpu.make_async_remote_copy(src_ref=v_ref, dst_ref=out_ref.at[me], send_sem=send.at[k],
                                              recv_sem=recv.at[k], device_id=tgt, device_id_type=MESH)
            rc.start()
            rcs.append(rc)
        lc.wait()
        for rc in rcs:
            rc.wait()

    return pl.pallas_call(
        body, name=name, in_specs=[ANY], out_specs=ANY,
        out_shape=jax.ShapeDtypeStruct((8,) + v.shape, v.dtype),
        scratch_shapes=[pltpu.SemaphoreType.DMA((7,)), pltpu.SemaphoreType.DMA((7,)), pltpu.SemaphoreType.DMA],
    )(v)


def _pick_rows(rows, cols, target_elems=128 * 1024, mult=SUBLANES):
    if rows % mult != 0:
        return rows
    best = mult
    t = mult
    while t <= rows:
        if rows % t == 0 and t * cols <= target_elems:
            best = t
        t += mult
    return best


def sum_chips(name, parts):
    _, R, C = parts.shape
    tm = _pick_rows(R, C)

    def body(p_ref, o_ref):
        o_ref[...] = (p_ref[0] + p_ref[1]) + (p_ref[2] + p_ref[3])

    return pl.pallas_call(
        body, name=name, grid=(R // tm,),
        in_specs=[pl.BlockSpec((4, tm, C), lambda i: (0, i, 0))],
        out_specs=pl.BlockSpec((tm, C), lambda i: (i, 0)),
        out_shape=jax.ShapeDtypeStruct((R, C), F32),
        compiler_params=_params(dimension_semantics=("arbitrary",)),
    )(parts)


def _adamw(g, w, m, v):
    m = ADAM_B1 * m + (1.0 - ADAM_B1) * g
    v = ADAM_B2 * v + (1.0 - ADAM_B2) * (g * g)
    m_hat = m / (1.0 - ADAM_B1 ** ADAM_STEP)
    v_hat = v / (1.0 - ADAM_B2 ** ADAM_STEP)
    delta = -ADAM_LR * (m_hat / (jnp.sqrt(v_hat) + ADAM_EPS) + ADAM_WD * w)
    return delta, m, v


def adamw_shard(name, s_mine, s_sib, w, m, v):
    R, C = w.shape
    tm = _pick_rows(R, C)

    def body(a_ref, b_ref, w_ref, m_ref, v_ref, g_out, d_out, m_out, v_out):
        g = a_ref[...] + b_ref[...]
        d, mn, vn = _adamw(g, w_ref[...], m_ref[...], v_ref[...])
        g_out[...] = g
        d_out[...] = d
        m_out[...] = mn
        v_out[...] = vn

    spec = pl.BlockSpec((tm, C), lambda i: (i, 0))
    return pl.pallas_call(
        body, name=name, grid=(R // tm,), in_specs=[spec] * 5, out_specs=[spec] * 4,
        out_shape=[jax.ShapeDtypeStruct((R, C), F32)] * 4,
        compiler_params=_params(dimension_semantics=("arbitrary",)),
    )(s_mine, s_sib, w, m, v)


def core_sum(name, core, g, got):
    _, r, C = g.shape
    hr = r // 2
    tm = _pick_rows(hr, 4 * C, 256 * 1024, 2 * SUBLANES)
    nh = hr // tm

    def body(c_ref, g_ref, s_ref, o_ref):
        o_ref[...] = (g_ref[...] + s_ref[...]).astype(BF16)

    return pl.pallas_call(
        body, name=name,
        grid_spec=pltpu.PrefetchScalarGridSpec(
            num_scalar_prefetch=1, grid=(nh,),
            in_specs=[pl.BlockSpec((4, tm, C), lambda i, cr: (0, cr[0] * nh + i, 0)),
                      pl.BlockSpec((4, tm, C), lambda i, cr: (0, i, 0))],
            out_specs=pl.BlockSpec((4, tm, C), lambda i, cr: (0, i, 0))),
        out_shape=jax.ShapeDtypeStruct((4, hr, C), BF16),
        compiler_params=_params(dimension_semantics=("arbitrary",)),
    )(core, g, got)


def chip_sum_into(name, core, parts, layer, n_layers, into=None):
    _, hr, C = parts.shape
    r = 2 * hr
    tm = _pick_rows(hr, 4 * C, 256 * 1024, 2 * SUBLANES)
    nh = hr // tm

    def body(c_ref, p_ref, *rest):
        o_ref = rest[-1]
        o_ref[...] = (_f32(p_ref[0]) + _f32(p_ref[1])) + (_f32(p_ref[2]) + _f32(p_ref[3]))

    in_specs = [pl.BlockSpec((4, tm, C), lambda i, cr: (0, i, 0))]
    args = [core, parts]
    aliases = {}
    if into is not None:
        in_specs.append(pl.BlockSpec(memory_space=pl.ANY))
        args.append(into)
        aliases = {2: 0}
    return pl.pallas_call(
        body, name=name,
        grid_spec=pltpu.PrefetchScalarGridSpec(
            num_scalar_prefetch=1, grid=(nh,), in_specs=in_specs,
            out_specs=pl.BlockSpec((tm, C), lambda i, cr: ((layer * r) // tm + cr[0] * nh + i, 0))),
        out_shape=jax.ShapeDtypeStruct((n_layers * r, C), F32), input_output_aliases=aliases,
        compiler_params=_params(dimension_semantics=("arbitrary",)),
    )(*args)


def adamw_full(name, g, w, m, v):
    R, C = w.shape
    tm = _pick_rows(R, C)

    def body(g_ref, w_ref, m_ref, v_ref, d_out, m_out, v_out):
        d, mn, vn = _adamw(g_ref[...], w_ref[...], m_ref[...], v_ref[...])
        d_out[...] = d
        m_out[...] = mn
        v_out[...] = vn

    spec = pl.BlockSpec((tm, C), lambda i: (i, 0))
    return pl.pallas_call(
        body, name=name, grid=(R // tm,), in_specs=[spec] * 4, out_specs=[spec] * 3,
        out_shape=[jax.ShapeDtypeStruct((R, C), F32)] * 3,
        compiler_params=_params(dimension_semantics=("arbitrary",)),
    )(g, w, m, v)


def adamw_small(name, parts, w, m, v):
    W = w.shape[1]

    def body(p_ref, w_ref, m_ref, v_ref, g_out, d_out, m_out, v_out):
        acc = p_ref[0]
        for k in range(1, 8):
            acc = acc + p_ref[k]
        g = jnp.sum(acc, axis=0, keepdims=True)
        d, mn, vn = _adamw(g, w_ref[...], m_ref[...], v_ref[...])
        g_out[...] = g
        d_out[...] = d
        m_out[...] = mn
        v_out[...] = vn

    return pl.pallas_call(
        body, name=name, out_shape=[jax.ShapeDtypeStruct((1, W), F32)] * 4,
        compiler_params=_params(),
    )(parts, w, m, v)


def _pad_lanes(v, width=LANES):
    return jnp.pad(v, ((0, 0), (0, width - v.shape[1])))


def _layer_fwd(cf, x, xb, pb, W, sm):
    S, D, CD, DI, XBC, F, H, G = cf['S'], cf['D'], cf['CD'], cf['DI'], cf['XBC'], cf['F'], cf['H'], cf['G']
    NM = cf['NM']
    alpha = cf['alpha']
    tm = cf['tm']
    tmx = cf['tmx']
    tn_in = cf['tn_in']
    sv = {}

    ident = lambda acc, ex: ([acc], [])
    proj, = fused_mm("in_proj", [(xb, W['in_main'], 0, False)], [], ident, [(NM, BF16, tn_in, 0)],
                     M=S, tm=tmx, tn=tn_in, nj=NM // tn_in)
    dtraw, = fused_mm("dt_proj", [(xb, W['in_dt'], 0, False)], [], ident, [(LANES, F32, LANES, 0)],
                      M=S, tm=tmx, tn=LANES)

    u, = row_call("glu", lambda a, gt: ([a * _sig(gt)], []),
                  [(proj, 'row', CD, 0, 0), (proj, 'row', CD, 1, 0)], [(CD, F32, CD, 0, 0)], M=S, tm=tm)

    def conv_a_epi(conv, ex):
        cb_, g_, b_ = ex
        ca = conv + cb_
        xhat, _ = _ln_stats(ca)
        la = xhat * g_ + b_
        return [ca, la * _sig(la)], []

    ca, sa = conv_call("conv_a", u, 0, sm['conv_a_w'], cf['KA'], conv_a_epi,
                       [(sm['conv_a_b'], 'vec', CD, 0, 0), (sm['ln_a_g'], 'vec', CD, 0, 0), (sm['ln_a_b'], 'vec', CD, 0, 0)],
                       [(CD, F32, CD, 0, 0), (CD, BF16, CD, 0, 0)], M=S, tm=cf['tmc'], cw=CD, nc=1, reverse=False)
    y_a, = fused_mm("a_out", [(sa, W['a_out'], 0, False)], [], ident, [(D, F32, D, 0)], M=S, tm=tmx, tn=D)

    def conv_x_epi(conv, ex):
        cb = conv + ex[0]
        act = cb * _sig(cb)
        return [cb, act], [], [act]

    def conv_bc_epi(conv, ex):
        cb = conv + ex[0]
        return [cb, cb * _sig(cb)], []

    xoff = (2 * CD + 2 * D + DI) // DI
    cbv_x, xs, xsT = conv_call("conv_b_x", proj, xoff, sm['ssm_conv_w'], cf['KB'], conv_x_epi,
                               [(sm['ssm_conv_b'], 'vec', DI, 0, 0)],
                               [(DI, F32, DI, 0, 0), (DI, F32, DI, 0, 0)], M=S, tm=cf['tmc'], cw=DI, nc=1,
                               reverse=False, t_outs=[(DI, F32, DI, 0, 0)])
    cbv_bc, bc = conv_call("conv_b_bc", proj, xoff + 1, sm['ssm_conv_w'], cf['KB'], conv_bc_epi,
                           [(sm['ssm_conv_b'], 'vec', DI, 1, 0)],
                           [(DI, F32, DI, 0, 0), (DI, F32, DI, 0, 0)], M=S, tm=cf['tmc'], cw=DI, nc=1,
                           reverse=False, w_c0=1)
    y_f, st_f = ssd_fwd("ssd_fwd_f", xsT, bc, dtraw, sm['dtb_f'], sm['alog_f'], S=S, DI=DI, G=G, H=H, rev=False)
    zoff = (2 * CD + 2 * D) // DI
    ysum, st_r, yn = ssd_fwd("ssd_fwd_r", xsT, bc, dtraw, sm['dtb_r'], sm['alog_r'], S=S, DI=DI, G=G, H=H, rev=True,
                             tail=(y_f, (proj, zoff), xs, sm['dskip_full'], sm['ssm_norm_g']))
    goff = (2 * CD) // D

    def merge_epi(acc, ex):
        ga, gb, ya = ex
        return [acc, _sig(ga) * ya + _sig(gb) * acc], []

    y_b, merged = fused_mm("b_out", [(yn, W['b_out'], 0, False)],
                           [(proj, 'row', D, goff), (proj, 'row', D, goff + 1), (y_a, 'row', D, 0)],
                           merge_epi, [(D, F32, D, 0), (D, BF16, D, 0)], M=S, tm=tm, tn=D)

    def mix_epi(acc, ex):
        xin, g_, b_ = ex
        r1 = alpha * xin + acc
        xhat, _ = _ln_stats(r1)
        return [r1, xhat * g_ + b_], []

    r1, hb = fused_mm("o_mix", [(merged, W['o'], 0, False)],
                      [(x, 'row', D, 0), (sm['ln1_g'], 'vec', D, 0), (sm['ln1_b'], 'vec', D, 0)],
                      mix_epi, [(D, F32, D, 0), (D, BF16, D, 0)], M=S, tm=tm, tn=D)

    tnf = cf['tnf']

    g32, g_ = fused_mm("ffn_gate", [(hb, W['gate_up'], 0, False)], [], lambda acc, ex: ([acc, acc], []),
                       [(F, F32, tnf, 0), (F, BF16, tnf, 0)], M=S, tm=tmx, tn=tnf, nj=F // tnf)
    u_, f = fused_mm("ffn_up", [(hb, W['gate_up'], F // tnf, False)], [(g32, 'row', tnf, 0)],
                     lambda acc, ex: ([acc, ex[0] * _sig(ex[0]) * acc], []),
                     [(F, BF16, tnf, 0), (F, BF16, tnf, 0)], M=S, tm=tmx, tn=tnf, nj=F // tnf)

    def down_epi(acc, ex):
        r1_, g1, b1, g2, b2 = ex
        xh1, _ = _ln_stats(r1_)
        r2 = alpha * (xh1 * g1 + b1) + acc
        xh2, _ = _ln_stats(r2)
        return [r2, xh2 * g2 + b2], []

    r2, h2b = fused_mm("ffn_down", [(f, W['down'], 0, False)],
                       [(r1, 'row', D, 0), (sm['ln1_g'], 'vec', D, 0), (sm['ln1_b'], 'vec', D, 0),
                        (sm['ln2_g'], 'vec', D, 0), (sm['ln2_b'], 'vec', D, 0)],
                       down_epi, [(D, F32, D, 0), (D, BF16, D, 0)], M=S, tm=tm, tn=D)

    pe, = fused_mm("ple_proj", [(pb, W['ple'], 0, False)], [], ident, [(D, F32, D, 0)], M=S, tm=tmx, tn=D)

    def ple_epi(acc, ex):
        r2_, g2, b2, pe_, pg = ex
        xh2, _ = _ln_stats(r2_)
        h2 = xh2 * g2 + b2
        e = pe_ * lax.rsqrt(jnp.mean(pe_ * pe_, axis=-1, keepdims=True) + RMS_EPS) * pg
        xn = h2 + e * _sig(acc)
        return [acc, xn, xn], []

    t_, xn, xnb = fused_mm("ple_gate", [(h2b, W['ple_gate'], 0, False)],
                           [(r2, 'row', D, 0), (sm['ln2_g'], 'vec', D, 0), (sm['ln2_b'], 'vec', D, 0),
                            (pe, 'row', D, 0), (sm['ple_norm_g'], 'vec', D, 0)],
                           ple_epi, [(D, F32, D, 0), (D, F32, D, 0), (D, BF16, D, 0)], M=S, tm=tm, tn=D)
    sv.update(x=x, xb=xb, pb=pb, proj=proj, dtraw=dtraw, u=u, ca=ca, sa=sa, y_a=y_a, cbv_x=cbv_x, cbv_bc=cbv_bc,
              xs=xs, xsT=xsT, bc=bc,
              ysum=ysum, st_f=st_f, st_r=st_r, yn=yn, y_b=y_b, merged=merged, r1=r1, hb=hb,
              g_=g_, u_=u_, f=f, r2=r2, h2b=h2b, t_=t_, pe=pe)
    return xn, xnb, sv


def _layer_bwd(cf, sv, W, sm, dxn=None, target=None, xn=None):
    S, D, CD, DI, XBC, F, H, G = cf['S'], cf['D'], cf['CD'], cf['DI'], cf['XBC'], cf['F'], cf['H'], cf['G']
    NM = cf['NM']
    alpha = cf['alpha']
    tm = cf['tm']
    gw = cf['GW']
    out = {}

    def ple_bwd_core(dx_, t, pe_, pg):
        s = _sig(t)
        rinv = lax.rsqrt(jnp.mean(pe_ * pe_, axis=-1, keepdims=True) + RMS_EPS)
        pn = pe_ * rinv
        e = pn * pg
        dtg = dx_ * e * (s * (1.0 - s))
        de = dx_ * s
        qv = de * pg
        dpe = rinv * (qv - pn * jnp.mean(qv * pn, axis=-1, keepdims=True))
        return dtg, dpe, de * pn

    if dxn is None:
        def head(xn_, tgt, t, pe_, pg):
            err = xn_ - tgt
            dx_ = err * (1.0 / D)
            dtg, dpe, dpg = ple_bwd_core(dx_, t, pe_, pg)
            return [dx_, dtg, dpe], [dpg, err * err]

        (dxn, dtg, dpe, dpg, lsq) = row_call(
            "loss_ple_bwd", head,
            [(xn, 'row', D, 0, 0), (target, 'row', D, 0, 0), (sv['t_'], 'row', D, 0, 0), (sv['pe'], 'row', D, 0, 0),
             (sm['ple_norm_g'], 'vec', D, 0, 0)],
            [(D, F32, D, 0, 0), (D, BF16, D, 0, 0), (D, BF16, D, 0, 0)], [(D, D, 0, 0), (D, D, 0, 0)], M=S, tm=tm)
        out['loss_sq'] = lsq
    else:
        def mid(dx_, t, pe_, pg):
            dtg, dpe, dpg = ple_bwd_core(dx_, t, pe_, pg)
            return [dtg, dpe], [dpg]

        (dtg, dpe, dpg) = row_call(
            "ple_bwd", mid,
            [(dxn, 'row', D, 0, 0), (sv['t_'], 'row', D, 0, 0), (sv['pe'], 'row', D, 0, 0),
             (sm['ple_norm_g'], 'vec', D, 0, 0)],
            [(D, BF16, D, 0, 0), (D, BF16, D, 0, 0)], [(D, D, 0, 0)], M=S, tm=tm)
    out['ple_norm_g'] = dpg

    def ln_bwd_epi(scale):
        def epi(acc, ex):
            res, r_, g_ = ex
            dh = scale * res + acc
            xhat, rstd = _ln_stats(r_)
            dr = _ln_bwd(dh, xhat, rstd, g_)
            return [dr, dr], [dh * xhat, dh]
        return epi

    dr2, dr2b, dg2, db2 = fused_mm(
        "dh2", [(dtg, W['ple_gate_T'], 0, False)],
        [(dxn, 'row', D, 0), (sv['r2'], 'row', D, 0), (sm['ln2_g'], 'vec', D, 0)],
        ln_bwd_epi(1.0), [(D, F32, D, 0), (D, BF16, D, 0)], [(D, D, 0), (D, D, 0)], M=S, tm=tm, tn=D)
    out['ln2_g'], out['ln2_b'] = dg2, db2

    tnf = cf['tnf']

    def dswiglu_epi(acc, ex):
        gg, uu = ex
        s = _sig(gg)
        return [acc * uu * _dsilu(gg, s), acc * (gg * s)], []

    dg_b, du_b = fused_mm(
        "d_down", [(dr2b, W['down_T'], 0, False)],
        [(sv['g_'], 'row', tnf, 0), (sv['u_'], 'row', tnf, 0)], dswiglu_epi,
        [(F, BF16, tnf, 0), (F, BF16, tnf, 0)], M=S, tm=tm, tn=tnf, nj=F // tnf)

    dr1, dr1b, dg1, db1 = fused_mm(
        "dh1", [(dg_b, W['gate_T'], 0, True), (du_b, W['up_T'], 0, True)],
        [(dr2, 'row', D, 0), (sv['r1'], 'row', D, 0), (sm['ln1_g'], 'vec', D, 0)],
        ln_bwd_epi(alpha), [(D, F32, D, 0), (D, BF16, D, 0)], [(D, D, 0), (D, D, 0)],
        M=S, tm=tm, tn=D, nk=cf['nk_f'])
    out['ln1_g'], out['ln1_b'] = dg1, db1

    goff = (2 * CD) // D

    def dmerge_epi(acc, ex):
        ga, gb, ya, yb = ex
        sa_, sb_ = _sig(ga), _sig(gb)
        dga = acc * ya * (sa_ * (1.0 - sa_))
        dgb = acc * yb * (sb_ * (1.0 - sb_))
        return [jnp.concatenate([dga, dgb], axis=1), acc * sa_, acc * sb_], []

    dproj, dya_b, dyb_b = fused_mm(
        "d_merge", [(dr1b, W['o_T'], 0, False)],
        [(sv['proj'], 'row', D, goff), (sv['proj'], 'row', D, goff + 1), (sv['y_a'], 'row', D, 0), (sv['y_b'], 'row', D, 0)],
        dmerge_epi, [(NM, BF16, 2 * D, (2 * CD) // (2 * D)), (D, BF16, D, 0), (D, BF16, D, 0)], M=S, tm=tm, tn=D)

    def dsa_epi(acc, ex):
        ca_, g_, b_ = ex
        xhat, rstd = _ln_stats(ca_)
        la = xhat * g_ + b_
        dla = acc * _dsilu(la, _sig(la))
        dca = _ln_bwd(dla, xhat, rstd, g_)
        return [dca], [dla * xhat, dla, dca]

    dca, dlag, dlab, dcab = fused_mm(
        "d_a_out", [(dya_b, W['a_out_T'], 0, False)],
        [(sv['ca'], 'row', CD, 0), (sm['ln_a_g'], 'vec', CD, 0), (sm['ln_a_b'], 'vec', CD, 0)],
        dsa_epi, [(CD, F32, CD, 0)], [(CD, CD, 0), (CD, CD, 0), (CD, CD, 0)], M=S, tm=tm, tn=D)
    out['ln_a_g'], out['ln_a_b'], out['conv_a_b'] = dlag, dlab, dcab

    def dglu_epi(du, ex):
        a, gt = ex
        s = _sig(gt)
        return [jnp.concatenate([du * s, du * a * (s * (1.0 - s))], axis=1)], []

    dproj, dwa = conv_call(
        "d_conv_a", dca, 0, sm['conv_a_w'], cf['KA'], dglu_epi,
        [(sv['proj'], 'row', CD, 0, 0), (sv['proj'], 'row', CD, 1, 0)],
        [(NM, BF16, 2 * CD, 0, 0)], M=S, tm=cf['tmc'], cw=CD, nc=1, reverse=True, xin=(sv['u'], 0),
        passthrough=(dproj, 0))
    out['conv_a_w'] = dwa

    zoff = (2 * CD + 2 * D) // DI

    def dgate_norm_epi(acc, ex):
        ysum_, xs, z, dsk, ng = ex
        y = ysum_ + xs * dsk
        sz = _sig(z)
        siluz = z * sz
        yz = y * siluz
        dyzs, yhats = [], []
        for g in range(G):
            t = yz[:, g * gw:(g + 1) * gw]
            rinv = lax.rsqrt(jnp.mean(t * t, axis=-1, keepdims=True) + RMS_EPS)
            yh = t * rinv
            qv = acc[:, g * gw:(g + 1) * gw] * ng[:, g * gw:(g + 1) * gw]
            dyzs.append(rinv * (qv - yh * jnp.mean(qv * yh, axis=-1, keepdims=True)))
            yhats.append(yh)
        dyz = jnp.concatenate(dyzs, axis=1)
        yhat = jnp.concatenate(yhats, axis=1)
        dy = dyz * siluz
        dz = dyz * y * _dsilu(z, sz)
        return [dz], [acc * yhat, dy * xs], [dy]

    tmr = cf['tmr']
    dproj, dng, ddsk, dyT = fused_mm(
        "d_b_out", [(dyb_b, W['b_out_T'], 0, False)],
        [(sv['ysum'], 'row', DI, 0), (sv['xs'], 'row', DI, 0), (sv['proj'], 'row', DI, zoff),
         (sm['dskip_full'], 'vec', DI, 0), (sm['ssm_norm_g'], 'vec', DI, 0)],
        dgate_norm_epi, [(NM, BF16, DI, zoff)], [(DI, DI, 0), (DI, DI, 0)],
        M=S, tm=tmr, tn=DI, passthrough=(dproj, 0), t_outs=[(DI, F32, DI, 0)])
    out['ssm_norm_g'], out['dskip_full'] = dng, ddsk

    dxbc_f, ddt_f, dA_f = ssd_bwd("ssd_bwd_f", sv['xsT'], sv['bc'], sv['dtraw'], dyT, sv['st_f'], sm['dtb_f'],
                                  sm['alog_f'], S=S, DI=DI, G=G, H=H, rev=False)
    dcb, ddt_r, dA_r, dcbb = ssd_bwd("ssd_bwd_r", sv['xsT'], sv['bc'], sv['dtraw'], dyT, sv['st_r'], sm['dtb_r'],
                                     sm['alog_r'], S=S, DI=DI, G=G, H=H, rev=True,
                                     tail=(dxbc_f, sv['cbv_x'], sv['cbv_bc'], sm['dskipT']))
    out['dA_f'], out['dA_r'] = dA_f, dA_r
    out['ssm_conv_b'] = dcbb

    xoff = (2 * CD + 2 * D + DI) // DI
    dproj, dwb = conv_call(
        "d_conv_b", dcb, 0, sm['ssm_conv_w'], cf['KB'], lambda conv, ex: ([conv], []), [],
        [(NM, BF16, DI, xoff, 1)], M=S, tm=cf['tmc'], cw=DI, nc=XBC // DI, reverse=True, xin=(sv['proj'], xoff),
        passthrough=(dproj, 0))
    out['ssm_conv_w'] = dwb

    ddtb, ddt_bias = row_call("d_dt", lambda a, b: ([a + b], [a + b]),
                              [(ddt_f, 'row', LANES, 0, 0), (ddt_r, 'row', LANES, 0, 0)],
                              [(LANES, BF16, LANES, 0, 0)], [(LANES, LANES, 0, 0)], M=S, tm=tm)
    out['dt_bias'] = ddt_bias

    dx, = fused_mm("d_x", [(dproj, W['in_main_T'], 0, True), (ddtb, W['in_dt_T'], 0, False)],
                   [(dr1, 'row', D, 0)], lambda acc, ex: ([alpha * ex[0] + acc], []),
                   [(D, F32, D, 0)], M=S, tm=cf['tmx'], tn=D, nk=cf['nk_in'])

    tmw = cf['tmw']
    xb = sv['xb']
    out['w_in'] = jnp.concatenate(
        [mm_tn("dw_in", xb, dproj, tm=tmw, tk=D, tn=cf['tn_in']),
         mm_tn("dw_dt", xb, ddtb, tm=tmw, tk=D, tn=LANES)[:, :2 * H]], axis=1)
    out['w_a_out'] = mm_tn("dw_a_out", sv['sa'], dya_b, tm=tmw, tk=CD, tn=D)
    out['w_b_out'] = mm_tn("dw_b_out", sv['yn'], dyb_b, tm=tmw, tk=DI // 2, tn=D)
    out['w_o'] = mm_tn("dw_o", sv['merged'], dr1b, tm=tmw, tk=D, tn=D)
    out['w_gate_up'] = jnp.concatenate(
        [mm_tn("dw_gate", sv['hb'], dg_b, tm=tmw, tk=D, tn=tnf),
         mm_tn("dw_up", sv['hb'], du_b, tm=tmw, tk=D, tn=tnf)], axis=1)
    out['w_down'] = mm_tn("dw_down", sv['f'], dr2b, tm=tmw, tk=tnf, tn=D)
    out['w_ple'] = mm_tn("dw_ple", sv['pb'], dpe, tm=tmw, tk=sv['pb'].shape[1], tn=D)
    out['w_ple_gate'] = mm_tn("dw_ple_gate", sv['h2b'], dtg, tm=tmw, tk=D, tn=D)
    return dx, out


_WEIGHTS = ['w_in', 'conv_a_w', 'conv_a_b', 'ln_a_g', 'ln_a_b', 'w_a_out', 'ssm_conv_w', 'ssm_conv_b', 'a_log',
            'dt_bias', 'd_skip', 'ssm_norm_g', 'w_b_out', 'w_o', 'ln1_g', 'ln1_b', 'w_gate_up', 'w_down', 'ln2_g',
            'ln2_b', 'w_ple', 'ple_norm_g', 'w_ple_gate']
_COL_SHARDED = ['w_in', 'conv_a_w', 'ssm_conv_w', 'w_gate_up', 'w_ple']
_ROW_SHARDED = ['w_a_out', 'w_b_out', 'w_o', 'w_down', 'w_ple_gate']
_BIG = _COL_SHARDED + _ROW_SHARDED
_SMALL = [n for n in _WEIGHTS if n not in _BIG]
_CONV = ['conv_a_w', 'ssm_conv_w']


def _ceil_to(n, k):
    return -(-n // k) * k


def kernel(x, p, w_in, conv_a_w, conv_a_b, ln_a_g, ln_a_b, w_a_out, ssm_conv_w, ssm_conv_b, a_log, dt_bias, d_skip, ssm_norm_g, w_b_out, w_o, ln1_g, ln1_b, w_gate_up, w_down, ln2_g, ln2_b, w_ple, ple_norm_g, w_ple_gate, loss_target, m_w_in, m_conv_a_w, m_conv_a_b, m_ln_a_g, m_ln_a_b, m_w_a_out, m_ssm_conv_w, m_ssm_conv_b, m_a_log, m_dt_bias, m_d_skip, m_ssm_norm_g, m_w_b_out, m_w_o, m_ln1_g, m_ln1_b, m_w_gate_up, m_w_down, m_ln2_g, m_ln2_b, m_w_ple, m_ple_norm_g, m_w_ple_gate, v_w_in, v_conv_a_w, v_conv_a_b, v_ln_a_g, v_ln_a_b, v_w_a_out, v_ssm_conv_w, v_ssm_conv_b, v_a_log, v_dt_bias, v_d_skip, v_ssm_norm_g, v_w_b_out, v_w_o, v_ln1_g, v_ln1_b, v_w_gate_up, v_w_down, v_ln2_g, v_ln2_b, v_w_ple, v_ple_norm_g, v_w_ple_gate):
    wt = dict(w_in=w_in, conv_a_w=conv_a_w, conv_a_b=conv_a_b, ln_a_g=ln_a_g, ln_a_b=ln_a_b, w_a_out=w_a_out,
              ssm_conv_w=ssm_conv_w, ssm_conv_b=ssm_conv_b, a_log=a_log, dt_bias=dt_bias, d_skip=d_skip,
              ssm_norm_g=ssm_norm_g, w_b_out=w_b_out, w_o=w_o, ln1_g=ln1_g, ln1_b=ln1_b, w_gate_up=w_gate_up,
              w_down=w_down, ln2_g=ln2_g, ln2_b=ln2_b, w_ple=w_ple, ple_norm_g=ple_norm_g, w_ple_gate=w_ple_gate)
    mo = dict(w_in=m_w_in, conv_a_w=m_conv_a_w, conv_a_b=m_conv_a_b, ln_a_g=m_ln_a_g, ln_a_b=m_ln_a_b,
              w_a_out=m_w_a_out, ssm_conv_w=m_ssm_conv_w, ssm_conv_b=m_ssm_conv_b, a_log=m_a_log,
              dt_bias=m_dt_bias, d_skip=m_d_skip, ssm_norm_g=m_ssm_norm_g, w_b_out=m_w_b_out, w_o=m_w_o,
              ln1_g=m_ln1_g, ln1_b=m_ln1_b, w_gate_up=m_w_gate_up, w_down=m_w_down, ln2_g=m_ln2_g, ln2_b=m_ln2_b,
              w_ple=m_w_ple, ple_norm_g=m_ple_norm_g, w_ple_gate=m_w_ple_gate)
    vo = dict(w_in=v_w_in, conv_a_w=v_conv_a_w, conv_a_b=v_conv_a_b, ln_a_g=v_ln_a_g, ln_a_b=v_ln_a_b,
              w_a_out=v_w_a_out, ssm_conv_w=v_ssm_conv_w, ssm_conv_b=v_ssm_conv_b, a_log=v_a_log,
              dt_bias=v_dt_bias, d_skip=v_d_skip, ssm_norm_g=v_ssm_norm_g, w_b_out=v_w_b_out, w_o=v_w_o,
              ln1_g=v_ln1_g, ln1_b=v_ln1_b, w_gate_up=v_w_gate_up, w_down=v_w_down, ln2_g=v_ln2_g, ln2_b=v_ln2_b,
              w_ple=v_w_ple, ple_norm_g=v_ple_norm_g, w_ple_gate=v_w_ple_gate)

    L = w_in.shape[0]
    S, D = x.shape[1], x.shape[2]
    CD = conv_a_b.shape[1]
    DI = ssm_norm_g.shape[1]
    XBC = ssm_conv_b.shape[1]
    H = d_skip.shape[1]
    G = (XBC - DI) // (2 * D_STATE)
    F = w_down.shape[1] * 4
    N_IN = w_in.shape[2] * 4
    NM = N_IN - 2 * H
    KA, KB = conv_a_w.shape[1], ssm_conv_w.shape[1]
    assert DI == H * HEAD_DIM and CD == D and DI == 2 * D and XBC == 2 * DI and NM == 2 * CD + 2 * D + DI + XBC
    assert 2 * H <= LANES and S % CHUNK == 0
    tnf = F // 2
    cf = dict(S=S, D=D, CD=CD, DI=DI, XBC=XBC, F=F, H=H, G=G, NM=NM, KA=KA, KB=KB, GW=(H // G) * HEAD_DIM,
              alpha=float((2 * L) ** 0.25), tm=min(512, S), tmx=min(1024, S), tmc=min(256, S), tmr=min(256, S), tmw=min(1024, S),
              tn_in=D, tnf=tnf, nk_f=1, nk_in=NM // DI)

    core = lax.axis_index("c").astype(jnp.int32).reshape(1)
    split_names = [n for n in _BIG if n not in _CONV]

    def layer_weights(l, got):
        full = {}
        for n, g in zip(split_names + _CONV, got):
            if n in _COL_SHARDED:
                full[n] = g.transpose(1, 0, 2).reshape(g.shape[1], 4 * g.shape[2])
            else:
                full[n] = g.reshape(4 * g.shape[1], g.shape[2])
        win = full['w_in']
        in_main = win[:, :NM]
        in_dt = _pad_lanes(win[:, NM:])
        gu = full['w_gate_up']
        W = dict(in_main=in_main, in_dt=in_dt, in_main_T=in_main.T, in_dt_T=in_dt.T,
                 a_out=full['w_a_out'], a_out_T=full['w_a_out'].T,
                 b_out=full['w_b_out'], b_out_T=full['w_b_out'].T,
                 o=full['w_o'], o_T=full['w_o'].T, gate_up=gu, gate_T=gu[:, :F].T, up_T=gu[:, F:].T,
                 down=full['w_down'], down_T=full['w_down'].T, ple=full['w_ple'],
                 ple_gate=full['w_ple_gate'], ple_gate_T=full['w_ple_gate'].T)
        row = lambda v: v.reshape(1, -1)
        head_table = lambda v: jnp.broadcast_to(jnp.pad(v, (0, LANES - H))[:, None], (LANES, LANES))
        sm = dict(conv_a_w=jnp.pad(full['conv_a_w'], ((0, _ceil_to(KA, SUBLANES) - KA), (0, 0))),
                  ssm_conv_w=jnp.pad(full['ssm_conv_w'], ((0, _ceil_to(KB, SUBLANES) - KB), (0, 0))),
                  conv_a_b=row(conv_a_b[l]), ln_a_g=row(ln_a_g[l]), ln_a_b=row(ln_a_b[l]),
                  ssm_conv_b=row(ssm_conv_b[l]), ssm_norm_g=row(ssm_norm_g[l]),
                  ln1_g=row(ln1_g[l]), ln1_b=row(ln1_b[l]), ln2_g=row(ln2_g[l]), ln2_b=row(ln2_b[l]),
                  ple_norm_g=row(ple_norm_g[l]),
                  dtb_f=head_table(dt_bias[l, 0]), dtb_r=head_table(dt_bias[l, 1]),
                  alog_f=head_table(a_log[l, 0]), alog_r=head_table(a_log[l, 1]),
                  dskip_full=row(jnp.repeat(d_skip[l], HEAD_DIM)),
                  dskipT=jnp.broadcast_to(jnp.repeat(d_skip[l], HEAD_DIM)[:, None], (DI, LANES)))
        return W, sm

    def blocks(n, gl):
        g = gl[n]
        if n == 'conv_a_w':
            g = g.sum(axis=1)[:KA]
        elif n == 'ssm_conv_w':
            g = g.sum(axis=1)[:KB]
        if n in _COL_SHARDED:
            return g.reshape(g.shape[0], 4, g.shape[1] // 4).transpose(1, 0, 2)
        return g.reshape(4, g.shape[0] // 4, g.shape[1])

    def core_sums(gl):
        mine = [blocks(n, gl) for n in split_names]
        theirs = core_send_half("core_send_half", mine)
        return [core_sum("core_sum_" + n, core, b, t) for n, b, t in zip(split_names, mine, theirs)]

    def chip_sums(l, parts, acc):
        sums = [chip_sum_into("chip_sum_" + n, core, pr, l, L, into=acc.get(n)) for n, pr in zip(split_names, parts)]
        return dict(zip(split_names, core_fill("core_fill", sums, l, L)))

    def shards(l):
        return [wt[n][l].astype(BF16) for n in split_names]

    lw = [None] * L
    pending = None
    for l in range(L):
        if l < L - 1 or L == 1:
            lw[l] = layer_weights(l, gather_layer("gather_weights", shards(l), [wt[n][l] for n in _CONV]))
    xl = x[0]
    if L > 1:
        sh = shards(L - 1)
        send, recv, sh, lands, token = chip_legs_start(
            "gather_start", 'gather', sh, [lax.empty((4,) + a.shape, a.dtype) for a in sh])
        pending = (send, recv, sh, lands)
        xlb = (xl + token[0, 0]).astype(BF16)
    else:
        xlb = xl.astype(BF16)
    saved = []
    for l in range(L):
        if l == L - 1 and pending is not None:
            send, recv, sh, lands = pending
            landed = chip_legs_wait("gather_wait", 'gather', send, recv, sh, lands, xl)
            conv_got = chip_exchange("gather_conv", [[wt[n][l]] for n in _CONV], gather=True)
            lw[l] = layer_weights(l, list(gather_finish("gather_finish", sh, landed)) + list(conv_got))
        xl, xlb, sv = _layer_fwd(cf, xl, xlb, p[l, 0].astype(BF16), lw[l][0], lw[l][1])
        saved.append(sv)
    grads = [None] * L
    dxl = None
    gsum = {}
    pending = None
    for l in reversed(range(L)):
        sm_l = lw[l][1]
        if pending is not None:
            sm_l = dict(sm_l, ple_norm_g=sm_l['ple_norm_g'] + pending[4][0, 0])
        if l == L - 1:
            dxl, grads[l] = _layer_bwd(cf, saved[l], lw[l][0], sm_l, target=loss_target[0], xn=xl)
        else:
            dxl, grads[l] = _layer_bwd(cf, saved[l], lw[l][0], sm_l, dxn=dxl)
        both = core_sums(grads[l])
        if l == L - 1 and L > 1:
            send, recv, both, lands, token = chip_legs_start(
                "scatter_start", 'scatter', both, [lax.empty(a.shape, a.dtype) for a in both])
            pending = (send, recv, both, lands, token)
            continue
        if pending is not None:
            send, recv, sent, lands, _ = pending
            landed = chip_legs_wait("scatter_wait", 'scatter', send, recv, sent, lands, dxl)
            gsum = chip_sums(L - 1, place_own(sent, landed), gsum)
            pending = None
        parts = chip_exchange("scatter_grads", [[t] for t in both], gather=False)
        gsum = chip_sums(l, [pr.reshape(4, pr.shape[2], pr.shape[3]) for pr in parts], gsum)
    loss = lax.psum(0.5 / D * jnp.sum(grads[L - 1]['loss_sq']), ("x", "y", "c"))
    grad_x = dxl[None]

    res = {}
    for n in split_names:
        shp = wt[n].shape
        flat = lambda a: a.reshape(shp[0] * shp[1], shp[2])
        outs = adamw_full("adamw_" + n, gsum[n], flat(wt[n]), flat(mo[n]), flat(vo[n]))
        res[n] = [o.reshape(shp) for o in [gsum[n]] + list(outs)]
    parts = chip_exchange("scatter_conv", [[blocks(n, grads[l]) for l in range(L)] for n in _CONV], gather=False)
    chip_sums = [sum_chips("chip_sum_" + n, pr.reshape(4, L * pr.shape[2], pr.shape[3])) for n, pr in zip(_CONV, parts)]
    sib_sums = sibling_swap("core_swap", chip_sums)
    for n, mine, sib in zip(_CONV, chip_sums, sib_sums):
        shp = wt[n].shape
        flat = lambda a: a.reshape(shp[0] * shp[1], shp[2])
        outs = adamw_shard("adamw_" + n, mine, sib, flat(wt[n]), flat(mo[n]), flat(vo[n]))
        res[n] = [o.reshape(shp) for o in outs]

    def small_pieces(l):
        gl = grads[l]
        A = -jnp.exp(a_log[l])
        d = dict(gl)
        d_alog = jnp.concatenate([gl['dA_f'].sum(axis=1)[:H] * A[0], gl['dA_r'].sum(axis=1)[:H] * A[1]])
        d['a_log'] = jnp.pad(d_alog[None], ((0, SUBLANES - 1), (0, 0)))
        d['dt_bias'] = gl['dt_bias'][:, :2 * H]
        d['d_skip'] = gl['dskip_full'].reshape(SUBLANES, H, HEAD_DIM).sum(axis=-1)
        return [_pad_lanes(d[n], _ceil_to(d[n].shape[1], LANES)) for n in _SMALL]

    widths = [_ceil_to(math.prod(wt[n].shape[1:]), LANES) for n in _SMALL]
    packed = jnp.concatenate([pc for l in range(L) for pc in small_pieces(l)], axis=1)
    gathered = all8_gather("gather_small", packed)

    def pack_params(src):
        return jnp.concatenate([_pad_lanes(src[n][l].reshape(1, -1), wd) for l in range(L) for n, wd in zip(_SMALL, widths)],
                               axis=1)

    small_out = adamw_small("adamw_small", gathered, pack_params(wt), pack_params(mo), pack_params(vo))
    off = 0
    per = {n: [[] for _ in range(4)] for n in _SMALL}
    for l in range(L):
        for n, wd in zip(_SMALL, widths):
            size = math.prod(wt[n].shape[1:])
            for k in range(4):
                per[n][k].append(small_out[k][0, off:off + size].reshape(wt[n].shape[1:]))
            off += wd
    for n in _SMALL:
        res[n] = [jnp.stack(per[n][k]) for k in range(4)]

    return (loss, grad_x, *[res[n][0] for n in _WEIGHTS], *[res[n][1] for n in _WEIGHTS],
            *[res[n][2] for n in _WEIGHTS], *[res[n][3] for n in _WEIGHTS])
```

```python
import math

import jax
import jax.numpy as jnp
from jax import lax
from jax.experimental import pallas as pl
from jax.experimental.pallas import tpu as pltpu

F32 = jnp.float32
BF16 = jnp.bfloat16

VMEM_LIMIT_BYTES = 56 * 1024 * 1024
LANES = 128
SUBLANES = 8

CHUNK = 128
D_STATE = 128
HEAD_DIM = 64
LN_EPS = 1e-5
RMS_EPS = 1e-6
ADAM_LR = 0.001
ADAM_B1 = 0.9
ADAM_B2 = 0.999
ADAM_EPS = 1e-08
ADAM_WD = 0.01
ADAM_STEP = 10
HALO = 16
MESH = pl.DeviceIdType.MESH


def _params(**kw):
    return pltpu.CompilerParams(vmem_limit_bytes=VMEM_LIMIT_BYTES, **kw)


def _sig(x):
    return jax.nn.sigmoid(x)


def _dsilu(x, s):
    return s * (1.0 + x * (1.0 - s))


def _ln_stats(r):
    mu = jnp.mean(r, axis=-1, keepdims=True)
    xc = r - mu
    var = jnp.mean(xc * xc, axis=-1, keepdims=True)
    rstd = lax.rsqrt(var + LN_EPS)
    return xc * rstd, rstd


def _ln_bwd(dy, xhat, rstd, g):
    dxh = dy * g
    m1 = jnp.mean(dxh, axis=-1, keepdims=True)
    m2 = jnp.mean(dxh * xhat, axis=-1, keepdims=True)
    return rstd * (dxh - m1 - xhat * m2)


def _f32(v):
    return v if v.dtype == F32 else v.astype(F32)


def _rows8(v):
    tm, w = v.shape
    return v.reshape(tm // SUBLANES, SUBLANES, w).sum(axis=0)


def fused_mm(name, prods, extras, epi, row_outs, col_outs=(), *, M, tm, tn, nj=1, nk=1,
             passthrough=None, t_outs=()):
    np_ = len(prods)
    ne = len(extras)
    nro = len(row_outs)
    nco = len(col_outs)
    use_acc = nk > 1

    def body(*refs):
        a_refs = [refs[2 * p] for p in range(np_)]
        w_refs = [refs[2 * p + 1] for p in range(np_)]
        pos = 2 * np_
        e_refs = refs[pos:pos + ne]
        pos += ne
        if passthrough is not None:
            pos += 1
        ro_refs = refs[pos:pos + nro]
        pos += nro
        co_refs = refs[pos:pos + nco]
        pos += nco
        to_refs = refs[pos:pos + len(t_outs)]
        pos += len(t_outs)
        acc_ref = refs[pos] if use_acc else None
        i = pl.program_id(1)
        k = pl.program_id(2)

        def prod(p):
            a = a_refs[p][...]
            if a.dtype != BF16:
                a = a.astype(BF16)
            return jnp.dot(a, w_refs[p][...], preferred_element_type=F32)

        def finish(acc):
            res = epi(acc, [_f32(r[...]) for r in e_refs])
            rows, cols = res[0], res[1]
            for v, o in zip(rows, ro_refs):
                o[...] = v.astype(o.dtype)
            for v, o in zip(res[2] if len(res) > 2 else (), to_refs):
                o[...] = v.T.astype(o.dtype)
            for v, o in zip(cols, co_refs):
                v8 = _rows8(v)

                @pl.when(i == 0)
                def _():
                    o[...] = v8

                @pl.when(i > 0)
                def _():
                    o[...] += v8

        if not use_acc:
            acc = prod(0)
            for p in range(1, np_):
                acc = acc + prod(p)
            finish(acc)
        else:
            @pl.when(k == 0)
            def _():
                acc = None
                for p in range(np_):
                    acc = prod(p) if acc is None else acc + prod(p)
                acc_ref[...] = acc

            @pl.when(k > 0)
            def _():
                acc = None
                for p in range(np_):
                    if prods[p][3]:
                        acc = prod(p) if acc is None else acc + prod(p)
                acc_ref[...] += acc

            @pl.when(k == nk - 1)
            def _():
                finish(acc_ref[...])

    in_specs = []
    args = []
    for a, w, joff, ksplit in prods:
        K = a.shape[1]
        if ksplit:
            tk = K // nk
            in_specs.append(pl.BlockSpec((tm, tk), lambda j, i, k: (i, k)))
            in_specs.append(pl.BlockSpec((tk, tn), lambda j, i, k, joff=joff: (k, j + joff)))
        else:
            in_specs.append(pl.BlockSpec((tm, K), lambda j, i, k: (i, 0)))
            in_specs.append(pl.BlockSpec((K, tn), lambda j, i, k, joff=joff: (0, j + joff)))
        args += [a, w]
    for arr, kind, width, c0 in extras:
        if kind == 'row':
            in_specs.append(pl.BlockSpec((tm, width), lambda j, i, k, c0=c0: (i, c0 + j)))
        else:
            in_specs.append(pl.BlockSpec((arr.shape[0], width), lambda j, i, k, c0=c0: (0, c0 + j)))
        args.append(arr)
    aliases = {}
    if passthrough is not None:
        arr, oidx = passthrough
        in_specs.append(pl.BlockSpec(memory_space=pl.ANY))
        aliases = {len(args): oidx}
        args.append(arr)
    out_shape = []
    out_specs = []
    for n_total, dtype, width, c0 in row_outs:
        out_shape.append(jax.ShapeDtypeStruct((M, n_total), dtype))
        out_specs.append(pl.BlockSpec((tm, width), lambda j, i, k, c0=c0: (i, c0 + j)))
    for n_total, width, c0 in col_outs:
        out_shape.append(jax.ShapeDtypeStruct((SUBLANES, n_total), F32))
        out_specs.append(pl.BlockSpec((SUBLANES, width), lambda j, i, k, c0=c0: (0, c0 + j)))
    for n_total, dtype, width, c0 in t_outs:
        out_shape.append(jax.ShapeDtypeStruct((n_total, M), dtype))
        out_specs.append(pl.BlockSpec((width, tm), lambda j, i, k, c0=c0: (c0 + j, i)))
    scratch = [pltpu.VMEM((tm, tn), F32)] if use_acc else []
    return pl.pallas_call(
        body, name=name, grid=(nj, M // tm, nk), in_specs=in_specs, out_specs=out_specs,
        out_shape=out_shape, scratch_shapes=scratch, input_output_aliases=aliases,
        compiler_params=_params(dimension_semantics=("arbitrary", "arbitrary", "arbitrary")),
    )(*args)


def mm_tn(name, a, b, *, tm, tk, tn):
    M, K = a.shape
    N = b.shape[1]

    def body(a_ref, b_ref, o_ref):
        m = pl.program_id(2)
        p = lax.dot_general(a_ref[...], b_ref[...], (((0,), (0,)), ((), ())),
                            preferred_element_type=F32)

        @pl.when(m == 0)
        def _():
            o_ref[...] = p

        @pl.when(m > 0)
        def _():
            o_ref[...] += p

    return pl.pallas_call(
        body, name=name, grid=(K // tk, N // tn, M // tm),
        in_specs=[pl.BlockSpec((tm, tk), lambda kk, j, m: (m, kk)),
                  pl.BlockSpec((tm, tn), lambda kk, j, m: (m, j))],
        out_specs=pl.BlockSpec((tk, tn), lambda kk, j, m: (kk, j)),
        out_shape=jax.ShapeDtypeStruct((K, N), F32),
        compiler_params=_params(dimension_semantics=("arbitrary", "arbitrary", "arbitrary")),
    )(a, b)


def row_call(name, fn, ins, row_outs, col_outs=(), *, M, tm, nc=1):
    ni = len(ins)
    nro = len(row_outs)

    def body(*refs):
        i = pl.program_id(1)
        vals = [_f32(r[...]) for r in refs[:ni]]
        rows, cols = fn(*vals)
        for v, o in zip(rows, refs[ni:ni + nro]):
            o[...] = v.astype(o.dtype)
        for v, o in zip(cols, refs[ni + nro:]):
            v8 = _rows8(v)

            @pl.when(i == 0)
            def _():
                o[...] = v8

            @pl.when(i > 0)
            def _():
                o[...] += v8

    in_specs = []
    for arr, kind, width, c0, cmul in ins:
        if kind == 'row':
            in_specs.append(pl.BlockSpec((tm, width), lambda cj, i, c0=c0, cmul=cmul: (i, c0 + cmul * cj)))
        else:
            in_specs.append(pl.BlockSpec((arr.shape[0], width), lambda cj, i, c0=c0, cmul=cmul: (0, c0 + cmul * cj)))
    out_shape = []
    out_specs = []
    for n_total, dtype, width, c0, cmul in row_outs:
        out_shape.append(jax.ShapeDtypeStruct((M, n_total), dtype))
        out_specs.append(pl.BlockSpec((tm, width), lambda cj, i, c0=c0, cmul=cmul: (i, c0 + cmul * cj)))
    for n_total, width, c0, cmul in col_outs:
        out_shape.append(jax.ShapeDtypeStruct((SUBLANES, n_total), F32))
        out_specs.append(pl.BlockSpec((SUBLANES, width), lambda cj, i, c0=c0, cmul=cmul: (0, c0 + cmul * cj)))
    return pl.pallas_call(
        body, name=name, grid=(nc, M // tm), in_specs=in_specs, out_specs=out_specs,
        out_shape=out_shape,
        compiler_params=_params(dimension_semantics=("arbitrary", "arbitrary")),
    )(*[a[0] for a in ins])


def conv_call(name, src, src_c0, w, K, epi, extras, row_outs, col_outs=(), *, M, tm, cw, nc,
              reverse, xin=None, passthrough=None, t_outs=(), w_c0=0):
    pad = (K - 1) // 2
    assert pad <= HALO - 1
    R = tm // HALO
    nblk = M // HALO
    n_i = M // tm
    Kp = w.shape[0]
    ne = len(extras)
    nro = len(row_outs)
    nco = len(col_outs)
    rb = 64
    cbw = min(cw, 256)
    n_copies = SUBLANES if K > SUBLANES else 1

    def body(*refs):
        main_ref, prev_ref, next_ref, w_ref = refs[:4]
        pos = 4
        xin_ref = None
        if xin is not None:
            xin_ref = refs[pos]
            pos += 1
        e_refs = refs[pos:pos + ne]
        pos += ne
        if passthrough is not None:
            pos += 1
        ro_refs = refs[pos:pos + nro]
        pos += nro
        co_refs = refs[pos:pos + nco]
        pos += nco
        to_refs = refs[pos:pos + len(t_outs)]
        pos += len(t_outs)
        dw_ref = None
        if xin is not None:
            dw_ref = refs[pos]
            pos += 1
        ext_ref, conv_ref = refs[pos], refs[pos + 1]
        i = pl.program_id(1)

        ext_ref[0, 0:HALO, :] = jnp.where(i == 0, 0.0, prev_ref[...].astype(F32))
        ext_ref[0, HALO:HALO + tm, :] = main_ref[...].astype(F32)
        ext_ref[0, HALO + tm:, :] = jnp.where(i == n_i - 1, 0.0, next_ref[...].astype(F32))
        if dw_ref is not None:
            @pl.when(i == 0)
            def _():
                dw_ref[...] = jnp.zeros_like(dw_ref)

        n_sh = tm + 2 * HALO - SUBLANES
        for c0 in range(0, cw, cbw):
            for sft in range(1, n_copies):
                ext_ref[sft, 0:n_sh, c0:c0 + cbw] = ext_ref[0, sft:sft + n_sh, c0:c0 + cbw]

        for c0 in range(0, cw, cbw):
            for r0 in range(0, tm, rb):
                acc = jnp.zeros((rb, cbw), F32)
                if xin_ref is not None:
                    xblk = xin_ref[r0:r0 + rb, c0:c0 + cbw].astype(F32)
                for k in range(K):
                    off = HALO + r0 + ((pad - k) if reverse else (k - pad))
                    sft = off % SUBLANES if n_copies > 1 else 0
                    d = ext_ref[sft, off - sft:off - sft + rb, c0:c0 + cbw]
                    acc = acc + d * w_ref[k:k + 1, c0:c0 + cbw]
                    if xin_ref is not None:
                        dw_ref[k, :, c0:c0 + cbw] += _rows8(xblk * d)
                conv_ref[r0:r0 + rb, c0:c0 + cbw] = acc

        res = epi(conv_ref[...], [_f32(r[...]) for r in e_refs])
        rows, cols = res[0], res[1]
        for v, o in zip(rows, ro_refs):
            o[...] = v.astype(o.dtype)
        for v, o in zip(res[2] if len(res) > 2 else (), to_refs):
            o[...] = v.T.astype(o.dtype)
        for v, o in zip(cols, co_refs):
            v8 = _rows8(v)

            @pl.when(i == 0)
            def _():
                o[...] = v8

            @pl.when(i > 0)
            def _():
                o[...] += v8

    in_specs = [
        pl.BlockSpec((tm, cw), lambda cj, i: (i, src_c0 + cj)),
        pl.BlockSpec((HALO, cw), lambda cj, i: (jnp.maximum(i * R - 1, 0), src_c0 + cj)),
        pl.BlockSpec((HALO, cw), lambda cj, i: (jnp.minimum((i + 1) * R, nblk - 1), src_c0 + cj)),
        pl.BlockSpec((Kp, cw), lambda cj, i: (0, w_c0 + cj)),
    ]
    args = [src, src, src, w]
    if xin is not None:
        in_specs.append(pl.BlockSpec((tm, cw), lambda cj, i, c0=xin[1]: (i, c0 + cj)))
        args.append(xin[0])
    for arr, kind, width, c0, cmul in extras:
        if kind == 'row':
            in_specs.append(pl.BlockSpec((tm, width), lambda cj, i, c0=c0, cmul=cmul: (i, c0 + cmul * cj)))
        else:
            in_specs.append(pl.BlockSpec((arr.shape[0], width), lambda cj, i, c0=c0, cmul=cmul: (0, c0 + cmul * cj)))
        args.append(arr)
    aliases = {}
    if passthrough is not None:
        in_specs.append(pl.BlockSpec(memory_space=pl.ANY))
        aliases = {len(args): passthrough[1]}
        args.append(passthrough[0])
    out_shape = []
    out_specs = []
    for n_total, dtype, width, c0, cmul in row_outs:
        out_shape.append(jax.ShapeDtypeStruct((M, n_total), dtype))
        out_specs.append(pl.BlockSpec((tm, width), lambda cj, i, c0=c0, cmul=cmul: (i, c0 + cmul * cj)))
    for n_total, width, c0, cmul in col_outs:
        out_shape.append(jax.ShapeDtypeStruct((SUBLANES, n_total), F32))
        out_specs.append(pl.BlockSpec((SUBLANES, width), lambda cj, i, c0=c0, cmul=cmul: (0, c0 + cmul * cj)))
    for n_total, dtype, width, c0, cmul in t_outs:
        out_shape.append(jax.ShapeDtypeStruct((n_total, M), dtype))
        out_specs.append(pl.BlockSpec((width, tm), lambda cj, i, c0=c0, cmul=cmul: (c0 + cmul * cj, i)))
    if xin is not None:
        out_shape.append(jax.ShapeDtypeStruct((Kp, SUBLANES, cw * nc), F32))
        out_specs.append(pl.BlockSpec((Kp, SUBLANES, cw), lambda cj, i: (0, 0, cj)))
    return pl.pallas_call(
        body, name=name, grid=(nc, n_i), in_specs=in_specs, out_specs=out_specs,
        out_shape=out_shape, input_output_aliases=aliases,
        scratch_shapes=[pltpu.VMEM((n_copies, tm + 2 * HALO, cw), F32), pltpu.VMEM((tm, cw), F32)],
        compiler_params=_params(dimension_semantics=("arbitrary", "arbitrary")),
    )(*args)


def _split_dot(m_bf16, v, n_pass, dims=None):
    out = None
    rest = v
    for p in range(n_pass):
        piece = rest.astype(BF16)
        if p + 1 < n_pass:
            rest = rest - piece.astype(F32)
        if dims is None:
            t = jnp.dot(m_bf16, piece, preferred_element_type=F32)
        else:
            t = lax.dot_general(m_bf16, piece, dims, preferred_element_type=F32)
        out = t if out is None else out + t
    return out


def _split_dot_r(v, m_bf16, n_pass):
    out = None
    rest = v
    for p in range(n_pass):
        piece = rest.astype(BF16)
        if p + 1 < n_pass:
            rest = rest - piece.astype(F32)
        t = jnp.dot(piece, m_bf16, preferred_element_type=F32)
        out = t if out is None else out + t
    return out


def _softplus(x):
    return jnp.maximum(x, 0.0) + jnp.log1p(jnp.exp(-jnp.abs(x)))


NT_DIMS = (((1,), (1,)), ((), ()))
TN_DIMS = (((0,), (0,)), ((), ()))


def _ssd_common(dtraw, dtbT, alogT, rev, n_heads):
    L = CHUNK
    if rev:
        dtraw = pltpu.roll(dtraw, LANES - n_heads, 1)
    preT = dtraw.T + dtbT
    dtT = _softplus(preT)
    AT = -jnp.exp(alogT)
    aT = dtT * AT
    ri = lax.broadcasted_iota(jnp.int32, (L, L), 0)
    ci = lax.broadcasted_iota(jnp.int32, (L, L), 1)
    up = (ri >= ci) if rev else (ri <= ci)
    lo = (ri <= ci) if rev else (ri >= ci)
    csT = _split_dot_r(aT, up.astype(BF16), 3)
    last = 0 if rev else L - 1
    lastB = jnp.broadcast_to(csT[:, last:last + 1], (L, L))
    return dict(preT=preT, dtT=dtT, AT=AT, csT=csT, cs=csT.T, up=up, lo=lo, ci=ci, last=last,
                doutT=jnp.exp(csT), dstT=jnp.exp(lastB - csT), totB=jnp.exp(lastB))


def ssd_fwd(name, xsT, bc, dtraw, dtbT, alogT, *, S, DI, G, H, rev, tail=None):
    NC = S // CHUNK
    R = H // G
    GW = R * HEAD_DIM
    N = D_STATE
    P = HEAD_DIM

    def body(*refs):
        xsT_ref, bc_ref, dtraw_ref, dtb_ref, alog_ref = refs[:5]
        if tail is None:
            y_ref, st_ref, h_ref = refs[5:]
        else:
            yo_ref, z_ref, xs_ref, dsk_ref, ng_ref = refs[5:10]
            y_ref, st_ref, yn_ref, h_ref = refs[10:]
        c = pl.program_id(0)

        @pl.when(c == 0)
        def _():
            h_ref[...] = jnp.zeros_like(h_ref)

        q = _ssd_common(dtraw_ref[...], dtb_ref[...], alog_ref[...], rev, H)
        cs, csT, dtT, doutT, totB = q['cs'], q['csT'], q['dtT'], q['doutT'], q['totB']
        wstT = q['dstT'] * dtT
        GB = 2 if G % 2 == 0 else 1
        for g0 in range(0, G, GB):
            gs = list(range(g0, g0 + GB))
            Bgs = [bc_ref[:, g * N:(g + 1) * N].astype(BF16) for g in gs]
            Cgs = [bc_ref[:, G * N + g * N:G * N + (g + 1) * N].astype(BF16) for g in gs]
            CBTs = [lax.dot_general(b, c_, NT_DIMS, preferred_element_type=F32) for b, c_ in zip(Bgs, Cgs)]
            HTs = [h_ref[g] for g in gs]
            yoffTs = [lax.dot_general(HT.astype(BF16), c_, NT_DIMS, preferred_element_type=F32)
                      for HT, c_ in zip(HTs, Cgs)]
            xTs = [xsT_ref[g * GW:(g + 1) * GW, :] for g in gs]
            heads = [(k, r) for k in range(GB) for r in range(R)]
            hs = [gs[k] * R + r for k, r in heads]
            blks = [slice(r * P, (r + 1) * P) for _, r in heads]
            segs = [jnp.where(q['up'], csT[h:h + 1, :] - cs[:, h:h + 1], -1e30) for h in hs]
            GTs = [(CBTs[k] * jnp.exp(sg)).astype(BF16) for (k, _), sg in zip(heads, segs)]
            xThs = [xTs[k][b, :] for (k, _), b in zip(heads, blks)]
            XThs = [(xTh * dtT[h:h + 1, :]).astype(BF16) for xTh, h in zip(xThs, hs)]
            ydTs = [jnp.dot(a, GT, preferred_element_type=F32) for a, GT in zip(XThs, GTs)]
            ys = [ydT + yoffTs[k][b, :] * doutT[h:h + 1, :] for ydT, (k, _), b, h in zip(ydTs, heads, blks, hs)]
            xws = [xTh * wstT[h:h + 1, :] for xTh, h in zip(xThs, hs)]
            tots = [jnp.broadcast_to(totB[h:h + 1, :], (P, N)) for h in hs]
            for k, g in enumerate(gs):
                sel = slice(k * R, (k + 1) * R)
                y_ref[:, g * GW:(g + 1) * GW] = jnp.concatenate(ys[sel], axis=0).T
                xwT = jnp.concatenate(xws[sel], axis=0).astype(BF16)
                ST = jnp.dot(xwT, Bgs[k], preferred_element_type=F32)
                st_ref[0, g] = HTs[k]
                h_ref[g] = HTs[k] * jnp.concatenate(tots[sel], axis=0) + ST
        if tail is not None:
            y = y_ref[...] + yo_ref[...]
            y_ref[...] = y
            z = _f32(z_ref[...])
            yz = (y + xs_ref[...] * dsk_ref[...]) * (z * _sig(z))
            for g in range(G):
                t = yz[:, g * GW:(g + 1) * GW]
                tn = t * lax.rsqrt(jnp.mean(t * t, axis=-1, keepdims=True) + RMS_EPS)
                yn_ref[:, g * GW:(g + 1) * GW] = (tn * ng_ref[:, g * GW:(g + 1) * GW]).astype(BF16)

    cidx = (lambda c: NC - 1 - c) if rev else (lambda c: c)
    cmap = lambda c: (cidx(c), 0)
    smap = lambda c: (cidx(c), 0, 0, 0)
    const = lambda c: (0, 0)
    tmap = lambda c: (0, cidx(c))
    in_specs = [pl.BlockSpec((DI, CHUNK), tmap), pl.BlockSpec((CHUNK, 2 * G * N), cmap), pl.BlockSpec((CHUNK, LANES), cmap),
                pl.BlockSpec((LANES, LANES), const), pl.BlockSpec((LANES, LANES), const)]
    out_specs = [pl.BlockSpec((CHUNK, DI), cmap), pl.BlockSpec((1, G, GW, N), smap)]
    out_shape = [jax.ShapeDtypeStruct((S, DI), F32), jax.ShapeDtypeStruct((NC, G, GW, N), F32)]
    args = [xsT, bc, dtraw, dtbT, alogT]
    if tail is not None:
        y_other, (z_arr, z_blk), xs_row, dsk, ng = tail
        in_specs += [pl.BlockSpec((CHUNK, DI), cmap), pl.BlockSpec((CHUNK, DI), lambda c: (cidx(c), z_blk)),
                     pl.BlockSpec((CHUNK, DI), cmap), pl.BlockSpec((1, DI), const), pl.BlockSpec((1, DI), const)]
        out_specs.append(pl.BlockSpec((CHUNK, DI), cmap))
        out_shape.append(jax.ShapeDtypeStruct((S, DI), BF16))
        args += [y_other, z_arr, xs_row, dsk, ng]
    return pl.pallas_call(
        body, name=name, grid=(NC,), in_specs=in_specs, out_specs=out_specs, out_shape=out_shape,
        scratch_shapes=[pltpu.VMEM((G, GW, N), F32)],
        compiler_params=_params(dimension_semantics=("arbitrary",)),
    )(*args)


def ssd_bwd(name, xsT, bc, dtraw, dyT, st, dtbT, alogT, *, S, DI, G, H, rev, tail=None):
    NC = S // CHUNK
    R = H // G
    GW = R * HEAD_DIM
    N = D_STATE
    XBC = DI + 2 * G * N
    P = HEAD_DIM
    L = CHUNK

    def body(*refs):
        xsT_ref, bc_ref, dtraw_ref, dyT_ref, st_ref, dtb_ref, alog_ref = refs[:7]
        if tail is None:
            dxbc_ref, ddt_ref, da_ref, dh_ref, dcst_ref, p2t_ref, p3t_ref, e2t_ref = refs[7:]
        else:
            other_ref, cbx_ref, cbbc_ref, dskT_ref = refs[7:11]
            dxbc_ref, ddt_ref, da_ref, dcol_ref, dh_ref, dcst_ref, p2t_ref, p3t_ref, e2t_ref = refs[11:]
        c = pl.program_id(0)

        @pl.when(c == 0)
        def _():
            dh_ref[...] = jnp.zeros_like(dh_ref)
            da_ref[...] = jnp.zeros_like(da_ref)
            dcst_ref[...] = jnp.zeros_like(dcst_ref)
            p2t_ref[...] = jnp.zeros_like(p2t_ref)
            p3t_ref[...] = jnp.zeros_like(p3t_ref)
            e2t_ref[...] = jnp.zeros_like(e2t_ref)

        q = _ssd_common(dtraw_ref[...], dtb_ref[...], alog_ref[...], rev, H)
        cs, csT, dtT, doutT, dstT, totB = q['cs'], q['csT'], q['dtT'], q['doutT'], q['dstT'], q['totB']
        wstT = dstT * dtT
        lane = q['ci']
        GB = 2 if G % 2 == 0 else 1
        for g0 in range(0, G, GB):
            gs = list(range(g0, g0 + GB))
            Bgs = [bc_ref[:, g * N:(g + 1) * N].astype(BF16) for g in gs]
            Cgs = [bc_ref[:, G * N + g * N:G * N + (g + 1) * N].astype(BF16) for g in gs]
            CBs = [lax.dot_general(c_, b, NT_DIMS, preferred_element_type=F32) for b, c_ in zip(Bgs, Cgs)]
            HpTs = [st_ref[0, g] for g in gs]
            HpTbs = [v.astype(BF16) for v in HpTs]
            dHTs = [dh_ref[g] for g in gs]
            dHTbs = [v.astype(BF16) for v in dHTs]
            BdHTs = [lax.dot_general(d, b, NT_DIMS, preferred_element_type=F32) for d, b in zip(dHTbs, Bgs)]
            yoffTs = [lax.dot_general(hp, c_, NT_DIMS, preferred_element_type=F32) for hp, c_ in zip(HpTbs, Cgs)]
            xTs = [xsT_ref[g * GW:(g + 1) * GW, :] for g in gs]
            dyTs = [dyT_ref[g * GW:(g + 1) * GW, :] for g in gs]
            heads = [(k, r) for k in range(GB) for r in range(R)]
            ks = [k for k, _ in heads]
            hs = [gs[k] * R + r for k, r in heads]
            blks = [slice(r * P, (r + 1) * P) for _, r in heads]
            Lms = [jnp.exp(jnp.where(q['lo'], cs[:, h:h + 1] - csT[h:h + 1, :], -1e30)) for h in hs]
            xThs = [xTs[k][b, :] for k, b in zip(ks, blks)]
            dyThs = [dyTs[k][b, :] for k, b in zip(ks, blks)]
            xThbs = [v.astype(BF16) for v in xThs]
            dyThbs = [v.astype(BF16) for v in dyThs]
            dGxs = [lax.dot_general(a, b, TN_DIMS, preferred_element_type=F32) for a, b in zip(dyThbs, xThbs)]
            Gms = [(CBs[k] * Lm).astype(BF16) for k, Lm in zip(ks, Lms)]
            XThbs = [(xTh * dtT[h:h + 1, :]).astype(BF16) for xTh, h in zip(xThs, hs)]
            u1Ts = [jnp.dot(a, Gm, preferred_element_type=F32) for a, Gm in zip(dyThbs, Gms)]
            ydTs = [lax.dot_general(a, Gm, NT_DIMS, preferred_element_type=F32) for a, Gm in zip(XThbs, Gms)]
            Ts = [dGx * (Lm * dtT[h:h + 1, :]) for dGx, Lm, h in zip(dGxs, Lms, hs)]
            uTs = [u1T + BdHTs[k][b, :] * dstT[h:h + 1, :] for u1T, k, b, h in zip(u1Ts, ks, blks, hs)]
            dyds = [dyTh * doutT[h:h + 1, :] for dyTh, h in zip(dyThs, hs)]
            xws = [xTh * wstT[h:h + 1, :] for xTh, h in zip(xThs, hs)]
            for i, h in enumerate(hs):
                k, b = ks[i], blks[i]
                p3row = jnp.sum(xws[i] * BdHTs[k][b, :], axis=0, keepdims=True)
                seg_row = jnp.sum(_f32(dyThbs[i]) * ydTs[i], axis=0, keepdims=True)
                seg_col = jnp.sum(_f32(XThbs[i]) * u1Ts[i], axis=0, keepdims=True)
                dcst_ref[h:h + 1, :] = (jnp.sum(dyds[i] * yoffTs[k][b, :], axis=0, keepdims=True)
                                        + seg_row - seg_col - p3row)
                p2t_ref[h:h + 1, :] = jnp.sum(xThs[i] * uTs[i], axis=0, keepdims=True)
                p3t_ref[h:h + 1, :] = p3row
                e2t_ref[h:h + 1, :] = jnp.sum(HpTs[k][b, :] * dHTs[k][b, :], axis=0, keepdims=True)
            dxs = [uT * dtT[h:h + 1, :] for uT, h in zip(uTs, hs)]
            if tail is not None:
                dxs = [d + dyTh * dskT_ref[h * P:(h + 1) * P, :] for d, dyTh, h in zip(dxs, dyThs, hs)]
            tots = [jnp.broadcast_to(totB[h:h + 1, :], (P, N)) for h in hs]
            for k, g in enumerate(gs):
                sel = slice(k * R, (k + 1) * R)
                dCB = Ts[k * R]
                for T in Ts[k * R + 1:(k + 1) * R]:
                    dCB = dCB + T
                dxbc_ref[:, g * GW:(g + 1) * GW] = jnp.concatenate(dxs[sel], axis=0).T
                dydT = jnp.concatenate(dyds[sel], axis=0).astype(BF16)
                xwT = jnp.concatenate(xws[sel], axis=0).astype(BF16)
                dCBb = dCB.astype(BF16)
                dC = (jnp.dot(dCBb, Bgs[k], preferred_element_type=F32)
                      + lax.dot_general(dydT, HpTbs[k], TN_DIMS, preferred_element_type=F32))
                dB = (lax.dot_general(dCBb, Cgs[k], TN_DIMS, preferred_element_type=F32)
                      + lax.dot_general(xwT, dHTbs[k], TN_DIMS, preferred_element_type=F32))
                dxbc_ref[:, DI + g * N:DI + (g + 1) * N] = dB
                dxbc_ref[:, DI + G * N + g * N:DI + G * N + (g + 1) * N] = dC
                dh_ref[g] = (dHTs[k] * jnp.concatenate(tots[sel], axis=0)
                             + jnp.dot(dydT, Cgs[k], preferred_element_type=F32))
        e1 = jnp.sum(p3t_ref[...], axis=1, keepdims=True)
        e2 = jnp.sum(e2t_ref[...], axis=1, keepdims=True)
        dcsT = dcst_ref[...] + jnp.where(lane == q['last'], e1 + totB * e2, 0.0)
        daT = _split_dot_r(dcsT, q['lo'].astype(BF16), 3)
        ddtT = daT * q['AT'] + p2t_ref[...]
        da_ref[...] += daT * dtT
        ddraw = jnp.where(lane < H, (ddtT * _sig(q['preT'])).T, 0.0)
        if rev:
            ddraw = pltpu.roll(ddraw, H, 1)
        ddt_ref[...] = ddraw
        if tail is not None:
            for c0, cb_ref in ((0, cbx_ref), (DI, cbbc_ref)):
                d = dxbc_ref[:, c0:c0 + DI] + other_ref[:, c0:c0 + DI]
                cb = cb_ref[...]
                dcb = d * _dsilu(cb, _sig(cb))
                dxbc_ref[:, c0:c0 + DI] = dcb
                part = _rows8(dcb)

                @pl.when(c == 0)
                def _():
                    dcol_ref[:, c0:c0 + DI] = part

                @pl.when(c > 0)
                def _():
                    dcol_ref[:, c0:c0 + DI] += part

    cmap = (lambda c: (c, 0)) if rev else (lambda c: (NC - 1 - c, 0))
    smap = (lambda c: (c, 0, 0, 0)) if rev else (lambda c: (NC - 1 - c, 0, 0, 0))
    const = lambda c: (0, 0)
    sq = pltpu.VMEM((LANES, CHUNK), F32)
    cix = (lambda c: c) if rev else (lambda c: NC - 1 - c)
    tmap = lambda c: (0, cix(c))
    in_specs = [pl.BlockSpec((DI, CHUNK), tmap), pl.BlockSpec((CHUNK, 2 * G * N), cmap), pl.BlockSpec((CHUNK, LANES), cmap),
                pl.BlockSpec((DI, CHUNK), tmap),
                pl.BlockSpec((1, G, GW, N), smap),
                pl.BlockSpec((LANES, LANES), const), pl.BlockSpec((LANES, LANES), const)]
    out_specs = [pl.BlockSpec((CHUNK, XBC), cmap), pl.BlockSpec((CHUNK, LANES), cmap),
                 pl.BlockSpec((LANES, LANES), const)]
    out_shape = [jax.ShapeDtypeStruct((S, XBC), F32), jax.ShapeDtypeStruct((S, LANES), F32),
                 jax.ShapeDtypeStruct((LANES, LANES), F32)]
    args = [xsT, bc, dtraw, dyT, st, dtbT, alogT]
    if tail is not None:
        in_specs += [pl.BlockSpec((CHUNK, XBC), cmap), pl.BlockSpec((CHUNK, DI), cmap),
                     pl.BlockSpec((CHUNK, 2 * G * N), cmap), pl.BlockSpec((DI, LANES), const)]
        out_specs.append(pl.BlockSpec((SUBLANES, XBC), const))
        out_shape.append(jax.ShapeDtypeStruct((SUBLANES, XBC), F32))
        args += list(tail)
    return pl.pallas_call(
        body, name=name, grid=(NC,), in_specs=in_specs, out_specs=out_specs, out_shape=out_shape,
        scratch_shapes=[pltpu.VMEM((G, GW, N), F32), sq, sq, sq, sq],
        compiler_params=_params(dimension_semantics=("arbitrary",)),
    )(*args)


ANY = pl.BlockSpec(memory_space=pl.ANY)


def chip_exchange(name, groups, gather):
    flat = [arr for grp in groups for arr in grp]
    n_in = len(flat)
    n_out = len(groups)
    n_rc = 3 * n_in

    def body(*refs):
        in_refs = refs[:n_in]
        out_refs = refs[n_in:n_in + n_out]
        send, recv = refs[n_in + n_out:]
        x, y, c = lax.axis_index("x"), lax.axis_index("y"), lax.axis_index("c")
        me = 2 * x + y
        peers = [(1 - x, y), (x, 1 - y), (1 - x, 1 - y)]
        remote = []
        q = 0
        for a, grp in enumerate(groups):
            for l in range(len(grp)):
                src = in_refs[q]
                dst = out_refs[a].at[me] if gather else out_refs[a].at[me, l]
                for j, (px, py) in enumerate(peers):
                    blk = src if gather else src.at[2 * px + py]
                    rc = pltpu.make_async_remote_copy(
                        src_ref=blk, dst_ref=dst, send_sem=send.at[3 * q + j], recv_sem=recv.at[3 * q + j],
                        device_id=(px, py, c), device_id_type=MESH)
                    rc.start()
                    remote.append(rc)
                q += 1
        for rc in remote:
            rc.wait()

    out_shape = []
    for grp in groups:
        a0 = grp[0]
        if gather:
            out_shape.append(jax.ShapeDtypeStruct((4,) + a0.shape, a0.dtype))
        else:
            out_shape.append(jax.ShapeDtypeStruct((4, len(grp)) + a0.shape[1:], a0.dtype))
    outs = pl.pallas_call(
        body, name=name, in_specs=[ANY] * n_in, out_specs=[ANY] * n_out, out_shape=out_shape,
        scratch_shapes=[pltpu.SemaphoreType.DMA((n_rc,)), pltpu.SemaphoreType.DMA((n_rc,))],
    )(*flat)
    me = _chip_index()
    res = []
    for grp, o in zip(groups, outs):
        for l, src in enumerate(grp):
            o = _put_block(o, src, (me,)) if gather else _put_block(o, _take_block(src, me), (me, l))
        res.append(o)
    return res


def _chip_index():
    return 2 * lax.axis_index("x") + lax.axis_index("y")


def _take_block(arr, idx):
    return lax.dynamic_index_in_dim(arr, idx, 0, keepdims=False)


def _put_block(dst, blk, idx):
    lead = len(idx)
    return lax.dynamic_update_slice(dst, blk.reshape((1,) * lead + blk.shape), tuple(idx) + (0,) * (dst.ndim - lead))


def gather_layer(name, split, whole):
    ns, nw = len(split), len(whole)
    n = ns + nw
    n_rc = 3 * (n + ns)

    def body(*refs):
        in_refs = refs[:n]
        out_refs = refs[n:2 * n]
        send, recv = refs[2 * n:]
        x, y, c = lax.axis_index("x"), lax.axis_index("y"), lax.axis_index("c")
        me = 2 * x + y
        sibling = (x, y, 1 - c)
        peers = [(1 - x, y), (x, 1 - y), (1 - x, 1 - y)]

        def region(a, chip, half):
            if a >= ns:
                return out_refs[a].at[chip]
            hr = split[a].shape[0] // 2
            return out_refs[a].at[chip, pl.ds(half * hr, hr)]

        def mine(a):
            if a >= ns:
                return in_refs[a]
            hr = split[a].shape[0] // 2
            return in_refs[a].at[pl.ds(c * hr, hr)]

        sends = []
        for a in range(n):
            for j, (px, py) in enumerate(peers):
                rc = pltpu.make_async_remote_copy(
                    src_ref=mine(a), dst_ref=region(a, me, c), send_sem=send.at[3 * a + j],
                    recv_sem=recv.at[3 * a + j], device_id=(px, py, c), device_id_type=MESH)
                rc.start()
                sends.append(rc)
        for a in range(n):
            for j, (px, py) in enumerate(peers):
                chip = 2 * px + py
                landed = pltpu.make_async_remote_copy(
                    src_ref=mine(a), dst_ref=region(a, chip, c), send_sem=send.at[3 * a + j],
                    recv_sem=recv.at[3 * a + j], device_id=(px, py, c), device_id_type=MESH)
                landed.wait_recv()
                if a < ns:
                    fw = pltpu.make_async_remote_copy(
                        src_ref=region(a, chip, c), dst_ref=region(a, chip, c), send_sem=send.at[3 * n + 3 * a + j],
                        recv_sem=recv.at[3 * n + 3 * a + j], device_id=sibling, device_id_type=MESH)
                    fw.start()
                    sends.append(fw)
        for a in range(ns):
            for j, (px, py) in enumerate(peers):
                chip = 2 * px + py
                pltpu.make_async_remote_copy(
                    src_ref=region(a, chip, 1 - c), dst_ref=region(a, chip, 1 - c), send_sem=send.at[3 * n + 3 * a + j],
                    recv_sem=recv.at[3 * n + 3 * a + j], device_id=sibling, device_id_type=MESH).wait_recv()
        for rc in sends:
            rc.wait_send()

    arrs = list(split) + list(whole)
    outs = pl.pallas_call(
        body, name=name, in_specs=[ANY] * n, out_specs=[ANY] * n,
        out_shape=[jax.ShapeDtypeStruct((4,) + a.shape, a.dtype) for a in arrs],
        scratch_shapes=[pltpu.SemaphoreType.DMA((n_rc,)), pltpu.SemaphoreType.DMA((n_rc,))],
    )(*arrs)
    me = _chip_index()
    return [_put_block(o, a, (me,)) for o, a in zip(outs, arrs)]


HBM_SPEC = pl.BlockSpec(memory_space=pltpu.HBM)
SEM_SPEC = pl.BlockSpec(memory_space=pltpu.SEMAPHORE)
IN_FLIGHT = pltpu.SideEffectType.DATAFLOW_SIDE_EFFECTING


def _chip_leg(kind, a_ref, l_ref, shape, c, me, chip):
    if kind == 'gather':
        hr = shape[0] // 2
        rows = pl.ds(c * hr, hr)
        return a_ref.at[rows], l_ref.at[me, rows], l_ref.at[chip, rows]
    return a_ref.at[chip], l_ref.at[me], l_ref.at[chip]


def chip_legs_start(name, kind, arrs, lands):
    n = len(arrs)

    def body(*refs):
        a_refs = refs[:n]
        l_refs = refs[n:2 * n]
        send, recv = refs[2 * n], refs[2 * n + 1]
        token = refs[-1]
        x, y, c = lax.axis_index("x"), lax.axis_index("y"), lax.axis_index("c")
        me = 2 * x + y
        for a in range(n):
            for j, (px, py) in enumerate([(1 - x, y), (x, 1 - y), (1 - x, 1 - y)]):
                src, dst, _ = _chip_leg(kind, a_refs[a], l_refs[a], arrs[a].shape, c, me, 2 * px + py)
                pltpu.make_async_remote_copy(src_ref=src, dst_ref=dst, send_sem=send.at[3 * a + j],
                                             recv_sem=recv.at[3 * a + j], device_id=(px, py, c),
                                             device_id_type=MESH).start()
        token[...] = jnp.zeros_like(token)

    both = list(arrs) + list(lands)
    outs = pl.pallas_call(
        body, name=name,
        out_shape=(pltpu.SemaphoreType.DMA((3 * n,)), pltpu.SemaphoreType.DMA((3 * n,)),
                   *[pltpu.HBM(a.shape, a.dtype) for a in both], jax.ShapeDtypeStruct((SUBLANES, LANES), F32)),
        in_specs=[HBM_SPEC] * (2 * n),
        out_specs=(SEM_SPEC, SEM_SPEC, *[HBM_SPEC] * (2 * n), pl.BlockSpec(memory_space=pltpu.VMEM)),
        input_output_aliases={i: 2 + i for i in range(2 * n)},
        compiler_params=pltpu.CompilerParams(has_side_effects=IN_FLIGHT),
    )(*[pltpu.with_memory_space_constraint(a, pltpu.HBM) for a in both])
    return outs[0], outs[1], list(outs[2:2 + n]), list(outs[2 + n:2 + 2 * n]), outs[-1]


def chip_legs_wait(name, kind, send, recv, arrs, lands, after):
    n = len(arrs)

    def body(*refs):
        a_refs = refs[:n]
        l_refs = refs[n:2 * n]
        send_, recv_ = refs[2 * n], refs[2 * n + 1]
        x, y, c = lax.axis_index("x"), lax.axis_index("y"), lax.axis_index("c")
        me = 2 * x + y
        legs = []
        for a in range(n):
            for j, (px, py) in enumerate([(1 - x, y), (x, 1 - y), (1 - x, 1 - y)]):
                src, dst, landing = _chip_leg(kind, a_refs[a], l_refs[a], arrs[a].shape, c, me, 2 * px + py)
                legs.append(pltpu.make_async_remote_copy(src_ref=src, dst_ref=landing, send_sem=send_.at[3 * a + j],
                                                         recv_sem=recv_.at[3 * a + j], device_id=(px, py, c),
                                                         device_id_type=MESH))
        for leg in legs:
            leg.wait_send()
        for leg in legs:
            leg.wait_recv()

    both = list(arrs) + list(lands)
    outs = pl.pallas_call(
        body, name=name, out_shape=tuple(pltpu.HBM(a.shape, a.dtype) for a in both),
        in_specs=[HBM_SPEC] * (2 * n) + [SEM_SPEC, SEM_SPEC, ANY], out_specs=tuple([HBM_SPEC] * (2 * n)),
        input_output_aliases={i: i for i in range(2 * n)},
        compiler_params=pltpu.CompilerParams(has_side_effects=IN_FLIGHT),
    )(*both, send, recv, after)
    return list(outs[n:])


def gather_finish(name, split, landed):
    n = len(split)

    def body(*refs):
        out_refs = refs[n:2 * n]
        send, recv = refs[2 * n:]
        x, y, c = lax.axis_index("x"), lax.axis_index("y"), lax.axis_index("c")
        sibling = (x, y, 1 - c)
        chips = [2 * (1 - x) + y, 2 * x + (1 - y), 2 * (1 - x) + (1 - y)]

        def region(a, chip, half):
            hr = split[a].shape[0] // 2
            return out_refs[a].at[chip, pl.ds(half * hr, hr)]

        sends = []
        for a in range(n):
            for j, chip in enumerate(chips):
                fw = pltpu.make_async_remote_copy(
                    src_ref=region(a, chip, c), dst_ref=region(a, chip, c), send_sem=send.at[3 * a + j],
                    recv_sem=recv.at[3 * a + j], device_id=sibling, device_id_type=MESH)
                fw.start()
                sends.append(fw)
        for a in range(n):
            for j, chip in enumerate(chips):
                pltpu.make_async_remote_copy(
                    src_ref=region(a, chip, 1 - c), dst_ref=region(a, chip, 1 - c), send_sem=send.at[3 * a + j],
                    recv_sem=recv.at[3 * a + j], device_id=sibling, device_id_type=MESH).wait_recv()
        for fw in sends:
            fw.wait_send()

    outs = pl.pallas_call(
        body, name=name, in_specs=[ANY] * n, out_specs=[ANY] * n,
        out_shape=[jax.ShapeDtypeStruct(a.shape, a.dtype) for a in landed],
        input_output_aliases={a: a for a in range(n)},
        scratch_shapes=[pltpu.SemaphoreType.DMA((3 * n,)), pltpu.SemaphoreType.DMA((3 * n,))],
    )(*landed)
    me = _chip_index()
    return [_put_block(o, a, (me,)) for o, a in zip(outs, split)]


def place_own(arrs, landed):
    me = _chip_index()
    return [_put_block(l, _take_block(a, me), (me,)) for a, l in zip(arrs, landed)]


def core_send_half(name, arrs):
    n = len(arrs)

    def body(*refs):
        in_refs = refs[:n]
        out_refs = refs[n:2 * n]
        send, recv = refs[2 * n:]
        c = lax.axis_index("c")
        peer = (lax.axis_index("x"), lax.axis_index("y"), 1 - c)
        rcs = []
        for a in range(n):
            hr = arrs[a].shape[1] // 2
            rc = pltpu.make_async_remote_copy(
                src_ref=in_refs[a].at[:, pl.ds((1 - c) * hr, hr)], dst_ref=out_refs[a], send_sem=send.at[a],
                recv_sem=recv.at[a], device_id=peer, device_id_type=MESH)
            rc.start()
            rcs.append(rc)
        for rc in rcs:
            rc.wait()

    return pl.pallas_call(
        body, name=name, in_specs=[ANY] * n, out_specs=[ANY] * n,
        out_shape=[jax.ShapeDtypeStruct((4, a.shape[1] // 2, a.shape[2]), a.dtype) for a in arrs],
        scratch_shapes=[pltpu.SemaphoreType.DMA((n,)), pltpu.SemaphoreType.DMA((n,))],
    )(*arrs)


def core_fill(name, arrs, layer, n_layers):
    n = len(arrs)

    def body(*refs):
        out_refs = refs[n:2 * n]
        send, recv = refs[2 * n:]
        c = lax.axis_index("c")
        peer = (lax.axis_index("x"), lax.axis_index("y"), 1 - c)
        rcs = []
        for a in range(n):
            r = arrs[a].shape[0] // n_layers
            hr = r // 2
            rows = out_refs[a].at[pl.ds(layer * r + c * hr, hr)]
            rc = pltpu.make_async_remote_copy(src_ref=rows, dst_ref=rows, send_sem=send.at[a], recv_sem=recv.at[a],
                                              device_id=peer, device_id_type=MESH)
            rc.start()
            rcs.append(rc)
        for a in range(n):
            r = arrs[a].shape[0] // n_layers
            hr = r // 2
            theirs = out_refs[a].at[pl.ds(layer * r + (1 - c) * hr, hr)]
            pltpu.make_async_remote_copy(src_ref=theirs, dst_ref=theirs, send_sem=send.at[a], recv_sem=recv.at[a],
                                         device_id=peer, device_id_type=MESH).wait_recv()
        for rc in rcs:
            rc.wait_send()

    return pl.pallas_call(
        body, name=name, in_specs=[ANY] * n, out_specs=[ANY] * n,
        out_shape=[jax.ShapeDtypeStruct(a.shape, a.dtype) for a in arrs],
        input_output_aliases={a: a for a in range(n)},
        scratch_shapes=[pltpu.SemaphoreType.DMA((n,)), pltpu.SemaphoreType.DMA((n,))],
    )(*arrs)


def sibling_swap(name, arrs):
    n = len(arrs)

    def body(*refs):
        in_refs = refs[:n]
        out_refs = refs[n:2 * n]
        send, recv = refs[2 * n:]
        peer = (lax.axis_index("x"), lax.axis_index("y"), 1 - lax.axis_index("c"))
        rcs = []
        for a in range(n):
            rc = pltpu.make_async_remote_copy(src_ref=in_refs[a], dst_ref=out_refs[a], send_sem=send.at[a],
                                              recv_sem=recv.at[a], device_id=peer, device_id_type=MESH)
            rc.start()
            rcs.append(rc)
        for rc in rcs:
            rc.wait()

    return pl.pallas_call(
        body, name=name, in_specs=[ANY] * n, out_specs=[ANY] * n,
        out_shape=[jax.ShapeDtypeStruct(a.shape, a.dtype) for a in arrs],
        scratch_shapes=[pltpu.SemaphoreType.DMA((n,)), pltpu.SemaphoreType.DMA((n,))],
    )(*arrs)


def all8_gather(name, v):
    flips = [(fx, fy, fc) for fx in (0, 1) for fy in (0, 1) for fc in (0, 1) if (fx, fy, fc) != (0, 0, 0)]

    def body(v_ref, out_ref, send, recv, loc):
        x, y, c = lax.axis_index("x"), lax.axis_index("y"), lax.axis_index("c")
        me = 4 * x + 2 * y + c
        lc = pltpu.make_async_copy(v_ref, out_ref.at[me], loc)
        lc.start()
        rcs = []
        for k, (fx, fy, fc) in enumerate(flips):
            tgt = (x + fx - 2 * x * fx, y + fy - 2 * y * fy, c + fc - 2 * c * fc)
            rc = pltpu.make_async_remote_copy(src_ref=v_ref, dst_ref=out_ref.at[me], send_sem=send.at[k],
                                              recv_sem=recv.at[k], device_id=tgt, device_id_type=MESH)
            rc.start()
            rcs.append(rc)
        lc.wait()
        for rc in rcs:
            rc.wait()

    return pl.pallas_call(
        body, name=name, in_specs=[ANY], out_specs=ANY,
        out_shape=jax.ShapeDtypeStruct((8,) + v.shape, v.dtype),
        scratch_shapes=[pltpu.SemaphoreType.DMA((7,)), pltpu.SemaphoreType.DMA((7,)), pltpu.SemaphoreType.DMA],
    )(v)


def _pick_rows(rows, cols, target_elems=128 * 1024, mult=SUBLANES):
    if rows % mult != 0:
        return rows
    best = mult
    t = mult
    while t <= rows:
        if rows % t == 0 and t * cols <= target_elems:
            best = t
        t += mult
    return best


def sum_chips(name, parts):
    _, R, C = parts.shape
    tm = _pick_rows(R, C)

    def body(p_ref, o_ref):
        o_ref[...] = (p_ref[0] + p_ref[1]) + (p_ref[2] + p_ref[3])

    return pl.pallas_call(
        body, name=name, grid=(R // tm,),
        in_specs=[pl.BlockSpec((4, tm, C), lambda i: (0, i, 0))],
        out_specs=pl.BlockSpec((tm, C), lambda i: (i, 0)),
        out_shape=jax.ShapeDtypeStruct((R, C), F32),
        compiler_params=_params(dimension_semantics=("arbitrary",)),
    )(parts)


def _adamw(g, w, m, v):
    m = ADAM_B1 * m + (1.0 - ADAM_B1) * g
    v = ADAM_B2 * v + (1.0 - ADAM_B2) * (g * g)
    m_hat = m / (1.0 - ADAM_B1 ** ADAM_STEP)
    v_hat = v / (1.0 - ADAM_B2 ** ADAM_STEP)
    delta = -ADAM_LR * (m_hat / (jnp.sqrt(v_hat) + ADAM_EPS) + ADAM_WD * w)
    return delta, m, v


def adamw_shard(name, s_mine, s_sib, w, m, v):
    R, C = w.shape
    tm = _pick_rows(R, C)

    def body(a_ref, b_ref, w_ref, m_ref, v_ref, g_out, d_out, m_out, v_out):
        g = a_ref[...] + b_ref[...]
        d, mn, vn = _adamw(g, w_ref[...], m_ref[...], v_ref[...])
        g_out[...] = g
        d_out[...] = d
        m_out[...] = mn
        v_out[...] = vn

    spec = pl.BlockSpec((tm, C), lambda i: (i, 0))
    return pl.pallas_call(
        body, name=name, grid=(R // tm,), in_specs=[spec] * 5, out_specs=[spec] * 4,
        out_shape=[jax.ShapeDtypeStruct((R, C), F32)] * 4,
        compiler_params=_params(dimension_semantics=("arbitrary",)),
    )(s_mine, s_sib, w, m, v)


def core_sum(name, core, g, got):
    _, r, C = g.shape
    hr = r // 2
    tm = _pick_rows(hr, 4 * C, 256 * 1024, 2 * SUBLANES)
    nh = hr // tm

    def body(c_ref, g_ref, s_ref, o_ref):
        o_ref[...] = (g_ref[...] + s_ref[...]).astype(BF16)

    return pl.pallas_call(
        body, name=name,
        grid_spec=pltpu.PrefetchScalarGridSpec(
            num_scalar_prefetch=1, grid=(nh,),
            in_specs=[pl.BlockSpec((4, tm, C), lambda i, cr: (0, cr[0] * nh + i, 0)),
                      pl.BlockSpec((4, tm, C), lambda i, cr: (0, i, 0))],
            out_specs=pl.BlockSpec((4, tm, C), lambda i, cr: (0, i, 0))),
        out_shape=jax.ShapeDtypeStruct((4, hr, C), BF16),
        compiler_params=_params(dimension_semantics=("arbitrary",)),
    )(core, g, got)


def chip_sum_into(name, core, parts, layer, n_layers, into=None):
    _, hr, C = parts.shape
    r = 2 * hr
    tm = _pick_rows(hr, 4 * C, 256 * 1024, 2 * SUBLANES)
    nh = hr // tm

    def body(c_ref, p_ref, *rest):
        o_ref = rest[-1]
        o_ref[...] = (_f32(p_ref[0]) + _f32(p_ref[1])) + (_f32(p_ref[2]) + _f32(p_ref[3]))

    in_specs = [pl.BlockSpec((4, tm, C), lambda i, cr: (0, i, 0))]
    args = [core, parts]
    aliases = {}
    if into is not None:
        in_specs.append(pl.BlockSpec(memory_space=pl.ANY))
        args.append(into)
        aliases = {2: 0}
    return pl.pallas_call(
        body, name=name,
        grid_spec=pltpu.PrefetchScalarGridSpec(
            num_scalar_prefetch=1, grid=(nh,), in_specs=in_specs,
            out_specs=pl.BlockSpec((tm, C), lambda i, cr: ((layer * r) // tm + cr[0] * nh + i, 0))),
        out_shape=jax.ShapeDtypeStruct((n_layers * r, C), F32), input_output_aliases=aliases,
        compiler_params=_params(dimension_semantics=("arbitrary",)),
    )(*args)


def adamw_full(name, g, w, m, v):
    R, C = w.shape
    tm = _pick_rows(R, C)

    def body(g_ref, w_ref, m_ref, v_ref, d_out, m_out, v_out):
        d, mn, vn = _adamw(g_ref[...], w_ref[...], m_ref[...], v_ref[...])
        d_out[...] = d
        m_out[...] = mn
        v_out[...] = vn

    spec = pl.BlockSpec((tm, C), lambda i: (i, 0))
    return pl.pallas_call(
        body, name=name, grid=(R // tm,), in_specs=[spec] * 4, out_specs=[spec] * 3,
        out_shape=[jax.ShapeDtypeStruct((R, C), F32)] * 3,
        compiler_params=_params(dimension_semantics=("arbitrary",)),
    )(g, w, m, v)


def adamw_small(name, parts, w, m, v):
    W = w.shape[1]

    def body(p_ref, w_ref, m_ref, v_ref, g_out, d_out, m_out, v_out):
        acc = p_ref[0]
        for k in range(1, 8):
            acc = acc + p_ref[k]
        g = jnp.sum(acc, axis=0, keepdims=True)
        d, mn, vn = _adamw(g, w_ref[...], m_ref[...], v_ref[...])
        g_out[...] = g
        d_out[...] = d
        m_out[...] = mn
        v_out[...] = vn

    return pl.pallas_call(
        body, name=name, out_shape=[jax.ShapeDtypeStruct((1, W), F32)] * 4,
        compiler_params=_params(),
    )(parts, w, m, v)


def _pad_lanes(v, width=LANES):
    return jnp.pad(v, ((0, 0), (0, width - v.shape[1])))


def _layer_fwd(cf, x, xb, pb, W, sm, target=None):
    S, D, CD, DI, XBC, F, H, G = cf['S'], cf['D'], cf['CD'], cf['DI'], cf['XBC'], cf['F'], cf['H'], cf['G']
    NM = cf['NM']
    alpha = cf['alpha']
    tm = cf['tm']
    tmx = cf['tmx']
    tn_in = cf['tn_in']
    sv = {}

    ident = lambda acc, ex: ([acc], [])
    proj, = fused_mm("in_proj", [(xb, W['in_main'], 0, False)], [], ident, [(NM, BF16, tn_in, 0)],
                     M=S, tm=tmx, tn=tn_in, nj=NM // tn_in)
    dtraw, = fused_mm("dt_proj", [(xb, W['in_dt'], 0, False)], [], ident, [(LANES, F32, LANES, 0)],
                      M=S, tm=tmx, tn=LANES)

    u, = row_call("glu", lambda a, gt: ([a * _sig(gt)], []),
                  [(proj, 'row', CD, 0, 0), (proj, 'row', CD, 1, 0)], [(CD, F32, CD, 0, 0)], M=S, tm=tm)

    def conv_a_epi(conv, ex):
        cb_, g_, b_ = ex
        ca = conv + cb_
        xhat, _ = _ln_stats(ca)
        la = xhat * g_ + b_
        return [ca, la * _sig(la)], []

    ca, sa = conv_call("conv_a", u, 0, sm['conv_a_w'], cf['KA'], conv_a_epi,
                       [(sm['conv_a_b'], 'vec', CD, 0, 0), (sm['ln_a_g'], 'vec', CD, 0, 0), (sm['ln_a_b'], 'vec', CD, 0, 0)],
                       [(CD, F32, CD, 0, 0), (CD, BF16, CD, 0, 0)], M=S, tm=cf['tmc'], cw=CD, nc=1, reverse=False)
    y_a, = fused_mm("a_out", [(sa, W['a_out'], 0, False)], [], ident, [(D, F32, D, 0)], M=S, tm=tmx, tn=D)

    def conv_x_epi(conv, ex):
        cb = conv + ex[0]
        act = cb * _sig(cb)
        return [cb, act], [], [act]

    def conv_bc_epi(conv, ex):
        cb = conv + ex[0]
        return [cb, cb * _sig(cb)], []

    xoff = (2 * CD + 2 * D + DI) // DI
    cbv_x, xs, xsT = conv_call("conv_b_x", proj, xoff, sm['ssm_conv_w'], cf['KB'], conv_x_epi,
                               [(sm['ssm_conv_b'], 'vec', DI, 0, 0)],
                               [(DI, F32, DI, 0, 0), (DI, F32, DI, 0, 0)], M=S, tm=cf['tmc'], cw=DI, nc=1,
                               reverse=False, t_outs=[(DI, F32, DI, 0, 0)])
    cbv_bc, bc = conv_call("conv_b_bc", proj, xoff + 1, sm['ssm_conv_w'], cf['KB'], conv_bc_epi,
                           [(sm['ssm_conv_b'], 'vec', DI, 1, 0)],
                           [(DI, F32, DI, 0, 0), (DI, F32, DI, 0, 0)], M=S, tm=cf['tmc'], cw=DI, nc=1,
                           reverse=False, w_c0=1)
    y_f, st_f = ssd_fwd("ssd_fwd_f", xsT, bc, dtraw, sm['dtb_f'], sm['alog_f'], S=S, DI=DI, G=G, H=H, rev=False)
    zoff = (2 * CD + 2 * D) // DI
    ysum, st_r, yn = ssd_fwd("ssd_fwd_r", xsT, bc, dtraw, sm['dtb_r'], sm['alog_r'], S=S, DI=DI, G=G, H=H, rev=True,
                             tail=(y_f, (proj, zoff), xs, sm['dskip_full'], sm['ssm_norm_g']))
    goff = (2 * CD) // D

    def merge_epi(acc, ex):
        ga, gb, ya = ex
        return [acc, _sig(ga) * ya + _sig(gb) * acc], []

    y_b, merged = fused_mm("b_out", [(yn, W['b_out'], 0, False)],
                           [(proj, 'row', D, goff), (proj, 'row', D, goff + 1), (y_a, 'row', D, 0)],
                           merge_epi, [(D, F32, D, 0), (D, BF16, D, 0)], M=S, tm=tm, tn=D)

    def mix_epi(acc, ex):
        xin, g_, b_ = ex
        r1 = alpha * xin + acc
        xhat, _ = _ln_stats(r1)
        return [r1, xhat * g_ + b_], []

    r1, hb = fused_mm("o_mix", [(merged, W['o'], 0, False)],
                      [(x, 'row', D, 0), (sm['ln1_g'], 'vec', D, 0), (sm['ln1_b'], 'vec', D, 0)],
                      mix_epi, [(D, F32, D, 0), (D, BF16, D, 0)], M=S, tm=tm, tn=D)

    tnf = cf['tnf']

    g32, g_ = fused_mm("ffn_gate", [(hb, W['gate_up'], 0, False)], [], lambda acc, ex: ([acc, acc], []),
                       [(F, F32, tnf, 0), (F, BF16, tnf, 0)], M=S, tm=tmx, tn=tnf, nj=F // tnf)
    u_, f = fused_mm("ffn_up", [(hb, W['gate_up'], F // tnf, False)], [(g32, 'row', tnf, 0)],
                     lambda acc, ex: ([acc, ex[0] * _sig(ex[0]) * acc], []),
                     [(F, BF16, tnf, 0), (F, BF16, tnf, 0)], M=S, tm=tmx, tn=tnf, nj=F // tnf)

    def down_epi(acc, ex):
        r1_, g1, b1, g2, b2 = ex
        xh1, _ = _ln_stats(r1_)
        r2 = alpha * (xh1 * g1 + b1) + acc
        xh2, _ = _ln_stats(r2)
        return [r2, xh2 * g2 + b2], []

    r2, h2b = fused_mm("ffn_down", [(f, W['down'], 0, False)],
                       [(r1, 'row', D, 0), (sm['ln1_g'], 'vec', D, 0), (sm['ln1_b'], 'vec', D, 0),
                        (sm['ln2_g'], 'vec', D, 0), (sm['ln2_b'], 'vec', D, 0)],
                       down_epi, [(D, F32, D, 0), (D, BF16, D, 0)], M=S, tm=tm, tn=D)

    pe, = fused_mm("ple_proj", [(pb, W['ple'], 0, False)], [], ident, [(D, F32, D, 0)], M=S, tm=tmx, tn=D)

    def ple_out(acc, r2_, g2, b2, pe_, pg):
        xh2, _ = _ln_stats(r2_)
        h2 = xh2 * g2 + b2
        e = pe_ * lax.rsqrt(jnp.mean(pe_ * pe_, axis=-1, keepdims=True) + RMS_EPS) * pg
        return h2 + e * _sig(acc)

    ple_extras = [(r2, 'row', D, 0), (sm['ln2_g'], 'vec', D, 0), (sm['ln2_b'], 'vec', D, 0),
                  (pe, 'row', D, 0), (sm['ple_norm_g'], 'vec', D, 0)]
    if target is None:
        def ple_epi(acc, ex):
            xn_ = ple_out(acc, *ex)
            return [acc, xn_, xn_], []

        t_, xn, xnb = fused_mm("ple_gate", [(h2b, W['ple_gate'], 0, False)], ple_extras,
                               ple_epi, [(D, F32, D, 0), (D, F32, D, 0), (D, BF16, D, 0)], M=S, tm=tm, tn=D)
    else:
        def ple_loss_epi(acc, ex):
            err = ple_out(acc, *ex[:5]) - ex[5]
            dx_ = err * (1.0 / D)
            dtg, dpe, dpg = _ple_bwd(dx_, acc, ex[3], ex[4])
            return [acc, dx_, dtg, dpe], [dpg, err * err]

        t_, dxn, dtg, dpe, dpg, lsq = fused_mm(
            "ple_gate_loss", [(h2b, W['ple_gate'], 0, False)], ple_extras + [(target, 'row', D, 0)],
            ple_loss_epi, [(D, F32, D, 0), (D, F32, D, 0), (D, BF16, D, 0), (D, BF16, D, 0)],
            [(D, D, 0), (D, D, 0)], M=S, tm=tm, tn=D)
        xn = xnb = None
        sv['head'] = (dxn, dtg, dpe, dpg, lsq)
    sv.update(x=x, xb=xb, pb=pb, proj=proj, dtraw=dtraw, u=u, ca=ca, sa=sa, y_a=y_a, cbv_x=cbv_x, cbv_bc=cbv_bc,
              xs=xs, xsT=xsT, bc=bc,
              ysum=ysum, st_f=st_f, st_r=st_r, yn=yn, y_b=y_b, merged=merged, r1=r1, hb=hb,
              g_=g_, u_=u_, f=f, r2=r2, h2b=h2b, t_=t_, pe=pe)
    return xn, xnb, sv


def _ple_bwd(dx_, t, pe_, pg):
    s = _sig(t)
    rinv = lax.rsqrt(jnp.mean(pe_ * pe_, axis=-1, keepdims=True) + RMS_EPS)
    pn = pe_ * rinv
    e = pn * pg
    dtg = dx_ * e * (s * (1.0 - s))
    de = dx_ * s
    qv = de * pg
    dpe = rinv * (qv - pn * jnp.mean(qv * pn, axis=-1, keepdims=True))
    return dtg, dpe, de * pn


def _layer_bwd(cf, sv, W, sm, dxn=None):
    S, D, CD, DI, XBC, F, H, G = cf['S'], cf['D'], cf['CD'], cf['DI'], cf['XBC'], cf['F'], cf['H'], cf['G']
    NM = cf['NM']
    alpha = cf['alpha']
    tm = cf['tm']
    gw = cf['GW']
    out = {}

    if dxn is None:
        dxn, dtg, dpe, dpg, out['loss_sq'] = sv['head']
    else:
        def mid(dx_, t, pe_, pg):
            dtg, dpe, dpg = _ple_bwd(dx_, t, pe_, pg)
            return [dtg, dpe], [dpg]

        (dtg, dpe, dpg) = row_call(
            "ple_bwd", mid,
            [(dxn, 'row', D, 0, 0), (sv['t_'], 'row', D, 0, 0), (sv['pe'], 'row', D, 0, 0),
             (sm['ple_norm_g'], 'vec', D, 0, 0)],
            [(D, BF16, D, 0, 0), (D, BF16, D, 0, 0)], [(D, D, 0, 0)], M=S, tm=tm)
    out['ple_norm_g'] = dpg

    def ln_bwd_epi(scale):
        def epi(acc, ex):
            res, r_, g_ = ex
            dh = scale * res + acc
            xhat, rstd = _ln_stats(r_)
            dr = _ln_bwd(dh, xhat, rstd, g_)
            return [dr, dr], [dh * xhat, dh]
        return epi

    dr2, dr2b, dg2, db2 = fused_mm(
        "dh2", [(dtg, W['ple_gate_T'], 0, False)],
        [(dxn, 'row', D, 0), (sv['r2'], 'row', D, 0), (sm['ln2_g'], 'vec', D, 0)],
        ln_bwd_epi(1.0), [(D, F32, D, 0), (D, BF16, D, 0)], [(D, D, 0), (D, D, 0)], M=S, tm=tm, tn=D)
    out['ln2_g'], out['ln2_b'] = dg2, db2

    tnf = cf['tnf']

    def dswiglu_epi(acc, ex):
        gg, uu = ex
        s = _sig(gg)
        return [acc * uu * _dsilu(gg, s), acc * (gg * s)], []

    dg_b, du_b = fused_mm(
        "d_down", [(dr2b, W['down_T'], 0, False)],
        [(sv['g_'], 'row', tnf, 0), (sv['u_'], 'row', tnf, 0)], dswiglu_epi,
        [(F, BF16, tnf, 0), (F, BF16, tnf, 0)], M=S, tm=tm, tn=tnf, nj=F // tnf)

    dr1, dr1b, dg1, db1 = fused_mm(
        "dh1", [(dg_b, W['gate_T'], 0, True), (du_b, W['up_T'], 0, True)],
        [(dr2, 'row', D, 0), (sv['r1'], 'row', D, 0), (sm['ln1_g'], 'vec', D, 0)],
        ln_bwd_epi(alpha), [(D, F32, D, 0), (D, BF16, D, 0)], [(D, D, 0), (D, D, 0)],
        M=S, tm=tm, tn=D, nk=cf['nk_f'])
    out['ln1_g'], out['ln1_b'] = dg1, db1

    goff = (2 * CD) // D

    def dmerge_epi(acc, ex):
        ga, gb, ya, yb = ex
        sa_, sb_ = _sig(ga), _sig(gb)
        dga = acc * ya * (sa_ * (1.0 - sa_))
        dgb = acc * yb * (sb_ * (1.0 - sb_))
        return [jnp.concatenate([dga, dgb], axis=1), acc * sa_, acc * sb_], []

    dproj, dya_b, dyb_b = fused_mm(
        "d_merge", [(dr1b, W['o_T'], 0, False)],
        [(sv['proj'], 'row', D, goff), (sv['proj'], 'row', D, goff + 1), (sv['y_a'], 'row', D, 0), (sv['y_b'], 'row', D, 0)],
        dmerge_epi, [(NM, BF16, 2 * D, (2 * CD) // (2 * D)), (D, BF16, D, 0), (D, BF16, D, 0)], M=S, tm=tm, tn=D)

    def dsa_epi(acc, ex):
        ca_, g_, b_ = ex
        xhat, rstd = _ln_stats(ca_)
        la = xhat * g_ + b_
        dla = acc * _dsilu(la, _sig(la))
        dca = _ln_bwd(dla, xhat, rstd, g_)
        return [dca], [dla * xhat, dla, dca]

    dca, dlag, dlab, dcab = fused_mm(
        "d_a_out", [(dya_b, W['a_out_T'], 0, False)],
        [(sv['ca'], 'row', CD, 0), (sm['ln_a_g'], 'vec', CD, 0), (sm['ln_a_b'], 'vec', CD, 0)],
        dsa_epi, [(CD, F32, CD, 0)], [(CD, CD, 0), (CD, CD, 0), (CD, CD, 0)], M=S, tm=tm, tn=D)
    out['ln_a_g'], out['ln_a_b'], out['conv_a_b'] = dlag, dlab, dcab

    def dglu_epi(du, ex):
        a, gt = ex
        s = _sig(gt)
        return [jnp.concatenate([du * s, du * a * (s * (1.0 - s))], axis=1)], []

    dproj, dwa = conv_call(
        "d_conv_a", dca, 0, sm['conv_a_w'], cf['KA'], dglu_epi,
        [(sv['proj'], 'row', CD, 0, 0), (sv['proj'], 'row', CD, 1, 0)],
        [(NM, BF16, 2 * CD, 0, 0)], M=S, tm=cf['tmc'], cw=CD, nc=1, reverse=True, xin=(sv['u'], 0),
        passthrough=(dproj, 0))
    out['conv_a_w'] = dwa

    zoff = (2 * CD + 2 * D) // DI

    def dgate_norm_epi(acc, ex):
        ysum_, xs, z, dsk, ng = ex
        y = ysum_ + xs * dsk
        sz = _sig(z)
        siluz = z * sz
        yz = y * siluz
        dyzs, yhats = [], []
        for g in range(G):
            t = yz[:, g * gw:(g + 1) * gw]
            rinv = lax.rsqrt(jnp.mean(t * t, axis=-1, keepdims=True) + RMS_EPS)
            yh = t * rinv
            qv = acc[:, g * gw:(g + 1) * gw] * ng[:, g * gw:(g + 1) * gw]
            dyzs.append(rinv * (qv - yh * jnp.mean(qv * yh, axis=-1, keepdims=True)))
            yhats.append(yh)
        dyz = jnp.concatenate(dyzs, axis=1)
        yhat = jnp.concatenate(yhats, axis=1)
        dy = dyz * siluz
        dz = dyz * y * _dsilu(z, sz)
        return [dz], [acc * yhat, dy * xs], [dy]

    tmr = cf['tmr']
    dproj, dng, ddsk, dyT = fused_mm(
        "d_b_out", [(dyb_b, W['b_out_T'], 0, False)],
        [(sv['ysum'], 'row', DI, 0), (sv['xs'], 'row', DI, 0), (sv['proj'], 'row', DI, zoff),
         (sm['dskip_full'], 'vec', DI, 0), (sm['ssm_norm_g'], 'vec', DI, 0)],
        dgate_norm_epi, [(NM, BF16, DI, zoff)], [(DI, DI, 0), (DI, DI, 0)],
        M=S, tm=tmr, tn=DI, passthrough=(dproj, 0), t_outs=[(DI, F32, DI, 0)])
    out['ssm_norm_g'], out['dskip_full'] = dng, ddsk

    dxbc_f, ddt_f, dA_f = ssd_bwd("ssd_bwd_f", sv['xsT'], sv['bc'], sv['dtraw'], dyT, sv['st_f'], sm['dtb_f'],
                                  sm['alog_f'], S=S, DI=DI, G=G, H=H, rev=False)
    dcb, ddt_r, dA_r, dcbb = ssd_bwd("ssd_bwd_r", sv['xsT'], sv['bc'], sv['dtraw'], dyT, sv['st_r'], sm['dtb_r'],
                                     sm['alog_r'], S=S, DI=DI, G=G, H=H, rev=True,
                                     tail=(dxbc_f, sv['cbv_x'], sv['cbv_bc'], sm['dskipT']))
    out['dA_f'], out['dA_r'] = dA_f, dA_r
    out['ssm_conv_b'] = dcbb

    xoff = (2 * CD + 2 * D + DI) // DI
    dproj, dwb = conv_call(
        "d_conv_b", dcb, 0, sm['ssm_conv_w'], cf['KB'], lambda conv, ex: ([conv], []), [],
        [(NM, BF16, DI, xoff, 1)], M=S, tm=cf['tmc'], cw=DI, nc=XBC // DI, reverse=True, xin=(sv['proj'], xoff),
        passthrough=(dproj, 0))
    out['ssm_conv_w'] = dwb

    ddtb, ddt_bias = row_call("d_dt", lambda a, b: ([a + b], [a + b]),
                              [(ddt_f, 'row', LANES, 0, 0), (ddt_r, 'row', LANES, 0, 0)],
                              [(LANES, BF16, LANES, 0, 0)], [(LANES, LANES, 0, 0)], M=S, tm=tm)
    out['dt_bias'] = ddt_bias

    dx, = fused_mm("d_x", [(dproj, W['in_main_T'], 0, True), (ddtb, W['in_dt_T'], 0, False)],
                   [(dr1, 'row', D, 0)], lambda acc, ex: ([alpha * ex[0] + acc], []),
                   [(D, F32, D, 0)], M=S, tm=cf['tmx'], tn=D, nk=cf['nk_in'])

    tmw = cf['tmw']
    xb = sv['xb']
    out['w_in'] = jnp.concatenate(
        [mm_tn("dw_in", xb, dproj, tm=tmw, tk=D, tn=cf['tn_in']),
         mm_tn("dw_dt", xb, ddtb, tm=tmw, tk=D, tn=LANES)[:, :2 * H]], axis=1)
    out['w_a_out'] = mm_tn("dw_a_out", sv['sa'], dya_b, tm=tmw, tk=CD, tn=D)
    out['w_b_out'] = mm_tn("dw_b_out", sv['yn'], dyb_b, tm=tmw, tk=DI // 2, tn=D)
    out['w_o'] = mm_tn("dw_o", sv['merged'], dr1b, tm=tmw, tk=D, tn=D)
    out['w_gate_up'] = jnp.concatenate(
        [mm_tn("dw_gate", sv['hb'], dg_b, tm=tmw, tk=D, tn=tnf),
         mm_tn("dw_up", sv['hb'], du_b, tm=tmw, tk=D, tn=tnf)], axis=1)
    out['w_down'] = mm_tn("dw_down", sv['f'], dr2b, tm=tmw, tk=tnf, tn=D)
    out['w_ple'] = mm_tn("dw_ple", sv['pb'], dpe, tm=tmw, tk=sv['pb'].shape[1], tn=D)
    out['w_ple_gate'] = mm_tn("dw_ple_gate", sv['h2b'], dtg, tm=tmw, tk=D, tn=D)
    return dx, out


_WEIGHTS = ['w_in', 'conv_a_w', 'conv_a_b', 'ln_a_g', 'ln_a_b', 'w_a_out', 'ssm_conv_w', 'ssm_conv_b', 'a_log',
            'dt_bias', 'd_skip', 'ssm_norm_g', 'w_b_out', 'w_o', 'ln1_g', 'ln1_b', 'w_gate_up', 'w_down', 'ln2_g',
            'ln2_b', 'w_ple', 'ple_norm_g', 'w_ple_gate']
_COL_SHARDED = ['w_in', 'conv_a_w', 'ssm_conv_w', 'w_gate_up', 'w_ple']
_ROW_SHARDED = ['w_a_out', 'w_b_out', 'w_o', 'w_down', 'w_ple_gate']
_BIG = _COL_SHARDED + _ROW_SHARDED
_SMALL = [n for n in _WEIGHTS if n not in _BIG]
_CONV = ['conv_a_w', 'ssm_conv_w']


def _ceil_to(n, k):
    return -(-n // k) * k


def kernel(x, p, w_in, conv_a_w, conv_a_b, ln_a_g, ln_a_b, w_a_out, ssm_conv_w, ssm_conv_b, a_log, dt_bias, d_skip, ssm_norm_g, w_b_out, w_o, ln1_g, ln1_b, w_gate_up, w_down, ln2_g, ln2_b, w_ple, ple_norm_g, w_ple_gate, loss_target, m_w_in, m_conv_a_w, m_conv_a_b, m_ln_a_g, m_ln_a_b, m_w_a_out, m_ssm_conv_w, m_ssm_conv_b, m_a_log, m_dt_bias, m_d_skip, m_ssm_norm_g, m_w_b_out, m_w_o, m_ln1_g, m_ln1_b, m_w_gate_up, m_w_down, m_ln2_g, m_ln2_b, m_w_ple, m_ple_norm_g, m_w_ple_gate, v_w_in, v_conv_a_w, v_conv_a_b, v_ln_a_g, v_ln_a_b, v_w_a_out, v_ssm_conv_w, v_ssm_conv_b, v_a_log, v_dt_bias, v_d_skip, v_ssm_norm_g, v_w_b_out, v_w_o, v_ln1_g, v_ln1_b, v_w_gate_up, v_w_down, v_ln2_g, v_ln2_b, v_w_ple, v_ple_norm_g, v_w_ple_gate):
    wt = dict(w_in=w_in, conv_a_w=conv_a_w, conv_a_b=conv_a_b, ln_a_g=ln_a_g, ln_a_b=ln_a_b, w_a_out=w_a_out,
              ssm_conv_w=ssm_conv_w, ssm_conv_b=ssm_conv_b, a_log=a_log, dt_bias=dt_bias, d_skip=d_skip,
              ssm_norm_g=ssm_norm_g, w_b_out=w_b_out, w_o=w_o, ln1_g=ln1_g, ln1_b=ln1_b, w_gate_up=w_gate_up,
              w_down=w_down, ln2_g=ln2_g, ln2_b=ln2_b, w_ple=w_ple, ple_norm_g=ple_norm_g, w_ple_gate=w_ple_gate)
    mo = dict(w_in=m_w_in, conv_a_w=m_conv_a_w, conv_a_b=m_conv_a_b, ln_a_g=m_ln_a_g, ln_a_b=m_ln_a_b,
              w_a_out=m_w_a_out, ssm_conv_w=m_ssm_conv_w, ssm_conv_b=m_ssm_conv_b, a_log=m_a_log,
              dt_bias=m_dt_bias, d_skip=m_d_skip, ssm_norm_g=m_ssm_norm_g, w_b_out=m_w_b_out, w_o=m_w_o,
              ln1_g=m_ln1_g, ln1_b=m_ln1_b, w_gate_up=m_w_gate_up, w_down=m_w_down, ln2_g=m_ln2_g, ln2_b=m_ln2_b,
              w_ple=m_w_ple, ple_norm_g=m_ple_norm_g, w_ple_gate=m_w_ple_gate)
    vo = dict(w_in=v_w_in, conv_a_w=v_conv_a_w, conv_a_b=v_conv_a_b, ln_a_g=v_ln_a_g, ln_a_b=v_ln_a_b,
              w_a_out=v_w_a_out, ssm_conv_w=v_ssm_conv_w, ssm_conv_b=v_ssm_conv_b, a_log=v_a_log,
              dt_bias=v_dt_bias, d_skip=v_d_skip, ssm_norm_g=v_ssm_norm_g, w_b_out=v_w_b_out, w_o=v_w_o,
              ln1_g=v_ln1_g, ln1_b=v_ln1_b, w_gate_up=v_w_gate_up, w_down=v_w_down, ln2_g=v_ln2_g, ln2_b=v_ln2_b,
              w_ple=v_w_ple, ple_norm_g=v_ple_norm_g, w_ple_gate=v_w_ple_gate)

    L = w_in.shape[0]
    S, D = x.shape[1], x.shape[2]
    CD = conv_a_b.shape[1]
    DI = ssm_norm_g.shape[1]
    XBC = ssm_conv_b.shape[1]
    H = d_skip.shape[1]
    G = (XBC - DI) // (2 * D_STATE)
    F = w_down.shape[1] * 4
    N_IN = w_in.shape[2] * 4
    NM = N_IN - 2 * H
    KA, KB = conv_a_w.shape[1], ssm_conv_w.shape[1]
    assert DI == H * HEAD_DIM and CD == D and DI == 2 * D and XBC == 2 * DI and NM == 2 * CD + 2 * D + DI + XBC
    assert 2 * H <= LANES and S % CHUNK == 0
    tnf = F // 2
    cf = dict(S=S, D=D, CD=CD, DI=DI, XBC=XBC, F=F, H=H, G=G, NM=NM, KA=KA, KB=KB, GW=(H // G) * HEAD_DIM,
              alpha=float((2 * L) ** 0.25), tm=min(512, S), tmx=min(1024, S), tmc=min(256, S), tmr=min(256, S), tmw=min(1024, S),
              tn_in=D, tnf=tnf, nk_f=1, nk_in=NM // DI)

    core = lax.axis_index("c").astype(jnp.int32).reshape(1)
    split_names = [n for n in _BIG if n not in _CONV]

    def layer_weights(l, got):
        full = {}
        for n, g in zip(split_names + _CONV, got):
            if n in _COL_SHARDED:
                full[n] = g.transpose(1, 0, 2).reshape(g.shape[1], 4 * g.shape[2])
            else:
                full[n] = g.reshape(4 * g.shape[1], g.shape[2])
        win = full['w_in']
        in_main = win[:, :NM]
        in_dt = _pad_lanes(win[:, NM:])
        gu = full['w_gate_up']
        W = dict(in_main=in_main, in_dt=in_dt, in_main_T=in_main.T, in_dt_T=in_dt.T,
                 a_out=full['w_a_out'], a_out_T=full['w_a_out'].T,
                 b_out=full['w_b_out'], b_out_T=full['w_b_out'].T,
                 o=full['w_o'], o_T=full['w_o'].T, gate_up=gu, gate_T=gu[:, :F].T, up_T=gu[:, F:].T,
                 down=full['w_down'], down_T=full['w_down'].T, ple=full['w_ple'],
                 ple_gate=full['w_ple_gate'], ple_gate_T=full['w_ple_gate'].T)
        row = lambda v: v.reshape(1, -1)
        head_table = lambda v: jnp.broadcast_to(jnp.pad(v, (0, LANES - H))[:, None], (LANES, LANES))
        sm = dict(conv_a_w=jnp.pad(full['conv_a_w'], ((0, _ceil_to(KA, SUBLANES) - KA), (0, 0))),
                  ssm_conv_w=jnp.pad(full['ssm_conv_w'], ((0, _ceil_to(KB, SUBLANES) - KB), (0, 0))),
                  conv_a_b=row(conv_a_b[l]), ln_a_g=row(ln_a_g[l]), ln_a_b=row(ln_a_b[l]),
                  ssm_conv_b=row(ssm_conv_b[l]), ssm_norm_g=row(ssm_norm_g[l]),
                  ln1_g=row(ln1_g[l]), ln1_b=row(ln1_b[l]), ln2_g=row(ln2_g[l]), ln2_b=row(ln2_b[l]),
                  ple_norm_g=row(ple_norm_g[l]),
                  dtb_f=head_table(dt_bias[l, 0]), dtb_r=head_table(dt_bias[l, 1]),
                  alog_f=head_table(a_log[l, 0]), alog_r=head_table(a_log[l, 1]),
                  dskip_full=row(jnp.repeat(d_skip[l], HEAD_DIM)),
                  dskipT=jnp.broadcast_to(jnp.repeat(d_skip[l], HEAD_DIM)[:, None], (DI, LANES)))
        return W, sm

    def blocks(n, gl):
        g = gl[n]
        if n == 'conv_a_w':
            g = g.sum(axis=1)[:KA]
        elif n == 'ssm_conv_w':
            g = g.sum(axis=1)[:KB]
        if n in _COL_SHARDED:
            return g.reshape(g.shape[0], 4, g.shape[1] // 4).transpose(1, 0, 2)
        return g.reshape(4, g.shape[0] // 4, g.shape[1])

    def core_sums(gl):
        mine = [blocks(n, gl) for n in split_names]
        theirs = core_send_half("core_send_half", mine)
        return [core_sum("core_sum_" + n, core, b, t) for n, b, t in zip(split_names, mine, theirs)]

    def chip_sums(l, parts, acc):
        sums = [chip_sum_into("chip_sum_" + n, core, pr, l, L, into=acc.get(n)) for n, pr in zip(split_names, parts)]
        return dict(zip(split_names, core_fill("core_fill", sums, l, L)))

    def shards(l):
        return [wt[n][l].astype(BF16) for n in split_names]

    lw = [None] * L
    pending = None
    for l in range(L):
        if l < L - 1 or L == 1:
            lw[l] = layer_weights(l, gather_layer("gather_weights", shards(l), [wt[n][l] for n in _CONV]))
    xl = x[0]
    if L > 1:
        sh = shards(L - 1)
        send, recv, sh, lands, token = chip_legs_start(
            "gather_start", 'gather', sh, [lax.empty((4,) + a.shape, a.dtype) for a in sh])
        pending = (send, recv, sh, lands)
        xlb = (xl + token[0, 0]).astype(BF16)
    else:
        xlb = xl.astype(BF16)
    saved = []
    for l in range(L):
        if l == L - 1 and pending is not None:
            send, recv, sh, lands = pending
            landed = chip_legs_wait("gather_wait", 'gather', send, recv, sh, lands, xl)
            conv_got = chip_exchange("gather_conv", [[wt[n][l]] for n in _CONV], gather=True)
            lw[l] = layer_weights(l, list(gather_finish("gather_finish", sh, landed)) + list(conv_got))
        xl, xlb, sv = _layer_fwd(cf, xl, xlb, p[l, 0].astype(BF16), lw[l][0], lw[l][1],
                                 target=loss_target[0] if l == L - 1 else None)
        saved.append(sv)
    grads = [None] * L
    dxl = None
    gsum = {}
    pending = None
    for l in reversed(range(L)):
        sm_l = lw[l][1]
        if pending is not None:
            sm_l = dict(sm_l, ple_norm_g=sm_l['ple_norm_g'] + pending[4][0, 0])
        if l == L - 1:
            dxl, grads[l] = _layer_bwd(cf, saved[l], lw[l][0], sm_l)
        else:
            dxl, grads[l] = _layer_bwd(cf, saved[l], lw[l][0], sm_l, dxn=dxl)
        both = core_sums(grads[l])
        if l == L - 1 and L > 1:
            send, recv, both, lands, token = chip_legs_start(
                "scatter_start", 'scatter', both, [lax.empty(a.shape, a.dtype) for a in both])
            pending = (send, recv, both, lands, token)
            continue
        if pending is not None:
            send, recv, sent, lands, _ = pending
            landed = chip_legs_wait("scatter_wait", 'scatter', send, recv, sent, lands, dxl)
            gsum = chip_sums(L - 1, place_own(sent, landed), gsum)
            pending = None
        parts = chip_exchange("scatter_grads", [[t] for t in both], gather=False)
        gsum = chip_sums(l, [pr.reshape(4, pr.shape[2], pr.shape[3]) for pr in parts], gsum)
    loss = lax.psum(0.5 / D * jnp.sum(grads[L - 1]['loss_sq']), ("x", "y", "c"))
    grad_x = dxl[None]

    res = {}
    for n in split_names:
        shp = wt[n].shape
        flat = lambda a: a.reshape(shp[0] * shp[1], shp[2])
        outs = adamw_full("adamw_" + n, gsum[n], flat(wt[n]), flat(mo[n]), flat(vo[n]))
        res[n] = [o.reshape(shp) for o in [gsum[n]] + list(outs)]
    parts = chip_exchange("scatter_conv", [[blocks(n, grads[l]) for l in range(L)] for n in _CONV], gather=False)
    chip_sums = [sum_chips("chip_sum_" + n, pr.reshape(4, L * pr.shape[2], pr.shape[3])) for n, pr in zip(_CONV, parts)]
    sib_sums = sibling_swap("core_swap", chip_sums)
    for n, mine, sib in zip(_CONV, chip_sums, sib_sums):
        shp = wt[n].shape
        flat = lambda a: a.reshape(shp[0] * shp[1], shp[2])
        outs = adamw_shard("adamw_" + n, mine, sib, flat(wt[n]), flat(mo[n]), flat(vo[n]))
        res[n] = [o.reshape(shp) for o in outs]

    def small_pieces(l):
        gl = grads[l]
        A = -jnp.exp(a_log[l])
        d = dict(gl)
        d_alog = jnp.concatenate([gl['dA_f'].sum(axis=1)[:H] * A[0], gl['dA_r'].sum(axis=1)[:H] * A[1]])
        d['a_log'] = jnp.pad(d_alog[None], ((0, SUBLANES - 1), (0, 0)))
        d['dt_bias'] = gl['dt_bias'][:, :2 * H]
        d['d_skip'] = gl['dskip_full'].reshape(SUBLANES, H, HEAD_DIM).sum(axis=-1)
        return [_pad_lanes(d[n], _ceil_to(d[n].shape[1], LANES)) for n in _SMALL]

    widths = [_ceil_to(math.prod(wt[n].shape[1:]), LANES) for n in _SMALL]
    packed = jnp.concatenate([pc for l in range(L) for pc in small_pieces(l)], axis=1)
    gathered = all8_gather("gather_small", packed)

    def pack_params(src):
        return jnp.concatenate([_pad_lanes(src[n][l].reshape(1, -1), wd) for l in range(L) for n, wd in zip(_SMALL, widths)],
                               axis=1)

    small_out = adamw_small("adamw_small", gathered, pack_params(wt), pack_params(mo), pack_params(vo))
    off = 0
    per = {n: [[] for _ in range(4)] for n in _SMALL}
    for l in range(L):
        for n, wd in zip(_SMALL, widths):
            size = math.prod(wt[n].shape[1:])
            for k in range(4):
                per[n][k].append(small_out[k][0, off:off + size].reshape(wt[n].shape[1:]))
            off += wd
    for n in _SMALL:
        res[n] = [jnp.stack(per[n][k]) for k in range(4)]

    return (loss, grad_x, *[res[n][0] for n in _WEIGHTS], *[res[n][1] for n in _WEIGHTS],
            *[res[n][2] for n in _WEIGHTS], *[res[n][3] for n in _WEIGHTS])
```

```python
import math

import jax
import jax.numpy as jnp
from jax import lax
from jax.experimental import pallas as pl
from jax.experimental.pallas import tpu as pltpu

F32 = jnp.float32
BF16 = jnp.bfloat16

VMEM_LIMIT_BYTES = 56 * 1024 * 1024
LANES = 128
SUBLANES = 8

CHUNK = 128
D_STATE = 128
HEAD_DIM = 64
LN_EPS = 1e-5
RMS_EPS = 1e-6
ADAM_LR = 0.001
ADAM_B1 = 0.9
ADAM_B2 = 0.999
ADAM_EPS = 1e-08
ADAM_WD = 0.01
ADAM_STEP = 10
HALO = 16
MESH = pl.DeviceIdType.MESH


def _params(**kw):
    return pltpu.CompilerParams(vmem_limit_bytes=VMEM_LIMIT_BYTES, **kw)


def _sig(x):
    return jax.nn.sigmoid(x)


def _dsilu(x, s):
    return s * (1.0 + x * (1.0 - s))


def _ln_stats(r):
    mu = jnp.mean(r, axis=-1, keepdims=True)
    xc = r - mu
    var = jnp.mean(xc * xc, axis=-1, keepdims=True)
    rstd = lax.rsqrt(var + LN_EPS)
    return xc * rstd, rstd


def _ln_bwd(dy, xhat, rstd, g):
    dxh = dy * g
    m1 = jnp.mean(dxh, axis=-1, keepdims=True)
    m2 = jnp.mean(dxh * xhat, axis=-1, keepdims=True)
    return rstd * (dxh - m1 - xhat * m2)


def _f32(v):
    return v if v.dtype == F32 else v.astype(F32)


def _rows8(v):
    tm, w = v.shape
    return v.reshape(tm // SUBLANES, SUBLANES, w).sum(axis=0)


def fused_mm(name, prods, extras, epi, row_outs, col_outs=(), *, M, tm, tn, nj=1, nk=1,
             passthrough=None, t_outs=()):
    np_ = len(prods)
    ne = len(extras)
    nro = len(row_outs)
    nco = len(col_outs)
    use_acc = nk > 1

    def body(*refs):
        a_refs = [refs[2 * p] for p in range(np_)]
        w_refs = [refs[2 * p + 1] for p in range(np_)]
        pos = 2 * np_
        e_refs = refs[pos:pos + ne]
        pos += ne
        if passthrough is not None:
            pos += 1
        ro_refs = refs[pos:pos + nro]
        pos += nro
        co_refs = refs[pos:pos + nco]
        pos += nco
        to_refs = refs[pos:pos + len(t_outs)]
        pos += len(t_outs)
        acc_ref = refs[pos] if use_acc else None
        i = pl.program_id(1)
        k = pl.program_id(2)

        def prod(p):
            a = a_refs[p][...]
            if a.dtype != BF16:
                a = a.astype(BF16)
            return jnp.dot(a, w_refs[p][...], preferred_element_type=F32)

        def finish(acc):
            res = epi(acc, [_f32(r[...]) for r in e_refs])
            rows, cols = res[0], res[1]
            for v, o in zip(rows, ro_refs):
                o[...] = v.astype(o.dtype)
            for v, o in zip(res[2] if len(res) > 2 else (), to_refs):
                o[...] = v.T.astype(o.dtype)
            for v, o in zip(cols, co_refs):
                v8 = _rows8(v)

                @pl.when(i == 0)
                def _():
                    o[...] = v8

                @pl.when(i > 0)
                def _():
                    o[...] += v8

        if not use_acc:
            acc = prod(0)
            for p in range(1, np_):
                acc = acc + prod(p)
            finish(acc)
        else:
            @pl.when(k == 0)
            def _():
                acc = None
                for p in range(np_):
                    acc = prod(p) if acc is None else acc + prod(p)
                acc_ref[...] = acc

            @pl.when(k > 0)
            def _():
                acc = None
                for p in range(np_):
                    if prods[p][3]:
                        acc = prod(p) if acc is None else acc + prod(p)
                acc_ref[...] += acc

            @pl.when(k == nk - 1)
            def _():
                finish(acc_ref[...])

    in_specs = []
    args = []
    for a, w, joff, ksplit in prods:
        K = a.shape[1]
        if ksplit:
            tk = K // nk
            in_specs.append(pl.BlockSpec((tm, tk), lambda j, i, k: (i, k)))
            in_specs.append(pl.BlockSpec((tk, tn), lambda j, i, k, joff=joff: (k, j + joff)))
        else:
            in_specs.append(pl.BlockSpec((tm, K), lambda j, i, k: (i, 0)))
            in_specs.append(pl.BlockSpec((K, tn), lambda j, i, k, joff=joff: (0, j + joff)))
        args += [a, w]
    for arr, kind, width, c0 in extras:
        if kind == 'row':
            in_specs.append(pl.BlockSpec((tm, width), lambda j, i, k, c0=c0: (i, c0 + j)))
        else:
            in_specs.append(pl.BlockSpec((arr.shape[0], width), lambda j, i, k, c0=c0: (0, c0 + j)))
        args.append(arr)
    aliases = {}
    if passthrough is not None:
        arr, oidx = passthrough
        in_specs.append(pl.BlockSpec(memory_space=pl.ANY))
        aliases = {len(args): oidx}
        args.append(arr)
    out_shape = []
    out_specs = []
    for n_total, dtype, width, c0 in row_outs:
        out_shape.append(jax.ShapeDtypeStruct((M, n_total), dtype))
        out_specs.append(pl.BlockSpec((tm, width), lambda j, i, k, c0=c0: (i, c0 + j)))
    for n_total, width, c0 in col_outs:
        out_shape.append(jax.ShapeDtypeStruct((SUBLANES, n_total), F32))
        out_specs.append(pl.BlockSpec((SUBLANES, width), lambda j, i, k, c0=c0: (0, c0 + j)))
    for n_total, dtype, width, c0 in t_outs:
        out_shape.append(jax.ShapeDtypeStruct((n_total, M), dtype))
        out_specs.append(pl.BlockSpec((width, tm), lambda j, i, k, c0=c0: (c0 + j, i)))
    scratch = [pltpu.VMEM((tm, tn), F32)] if use_acc else []
    return pl.pallas_call(
        body, name=name, grid=(nj, M // tm, nk), in_specs=in_specs, out_specs=out_specs,
        out_shape=out_shape, scratch_shapes=scratch, input_output_aliases=aliases,
        compiler_params=_params(dimension_semantics=("arbitrary", "arbitrary", "arbitrary")),
    )(*args)


def mm_tn(name, a, b, *, tm, tk, tn):
    M, K = a.shape
    N = b.shape[1]

    def body(a_ref, b_ref, o_ref):
        m = pl.program_id(2)
        p = lax.dot_general(a_ref[...], b_ref[...], (((0,), (0,)), ((), ())),
                            preferred_element_type=F32)

        @pl.when(m == 0)
        def _():
            o_ref[...] = p

        @pl.when(m > 0)
        def _():
            o_ref[...] += p

    return pl.pallas_call(
        body, name=name, grid=(K // tk, N // tn, M // tm),
        in_specs=[pl.BlockSpec((tm, tk), lambda kk, j, m: (m, kk)),
                  pl.BlockSpec((tm, tn), lambda kk, j, m: (m, j))],
        out_specs=pl.BlockSpec((tk, tn), lambda kk, j, m: (kk, j)),
        out_shape=jax.ShapeDtypeStruct((K, N), F32),
        compiler_params=_params(dimension_semantics=("arbitrary", "arbitrary", "arbitrary")),
    )(a, b)


def row_call(name, fn, ins, row_outs, col_outs=(), *, M, tm, nc=1):
    ni = len(ins)
    nro = len(row_outs)

    def body(*refs):
        i = pl.program_id(1)
        vals = [_f32(r[...]) for r in refs[:ni]]
        rows, cols = fn(*vals)
        for v, o in zip(rows, refs[ni:ni + nro]):
            o[...] = v.astype(o.dtype)
        for v, o in zip(cols, refs[ni + nro:]):
            v8 = _rows8(v)

            @pl.when(i == 0)
            def _():
                o[...] = v8

            @pl.when(i > 0)
            def _():
                o[...] += v8

    in_specs = []
    for arr, kind, width, c0, cmul in ins:
        if kind == 'row':
            in_specs.append(pl.BlockSpec((tm, width), lambda cj, i, c0=c0, cmul=cmul: (i, c0 + cmul * cj)))
        else:
            in_specs.append(pl.BlockSpec((arr.shape[0], width), lambda cj, i, c0=c0, cmul=cmul: (0, c0 + cmul * cj)))
    out_shape = []
    out_specs = []
    for n_total, dtype, width, c0, cmul in row_outs:
        out_shape.append(jax.ShapeDtypeStruct((M, n_total), dtype))
        out_specs.append(pl.BlockSpec((tm, width), lambda cj, i, c0=c0, cmul=cmul: (i, c0 + cmul * cj)))
    for n_total, width, c0, cmul in col_outs:
        out_shape.append(jax.ShapeDtypeStruct((SUBLANES, n_total), F32))
        out_specs.append(pl.BlockSpec((SUBLANES, width), lambda cj, i, c0=c0, cmul=cmul: (0, c0 + cmul * cj)))
    return pl.pallas_call(
        body, name=name, grid=(nc, M // tm), in_specs=in_specs, out_specs=out_specs,
        out_shape=out_shape,
        compiler_params=_params(dimension_semantics=("arbitrary", "arbitrary")),
    )(*[a[0] for a in ins])


def conv_call(name, src, src_c0, w, K, epi, extras, row_outs, col_outs=(), *, M, tm, cw, nc,
              reverse, xin=None, passthrough=None, t_outs=(), w_c0=0):
    pad = (K - 1) // 2
    assert pad <= HALO - 1
    R = tm // HALO
    nblk = M // HALO
    n_i = M // tm
    Kp = w.shape[0]
    ne = len(extras)
    nro = len(row_outs)
    nco = len(col_outs)
    rb = 64
    cbw = min(cw, 256)
    n_copies = SUBLANES if K > SUBLANES else 1

    def body(*refs):
        main_ref, prev_ref, next_ref, w_ref = refs[:4]
        pos = 4
        xin_ref = None
        if xin is not None:
            xin_ref = refs[pos]
            pos += 1
        e_refs = refs[pos:pos + ne]
        pos += ne
        if passthrough is not None:
            pos += 1
        ro_refs = refs[pos:pos + nro]
        pos += nro
        co_refs = refs[pos:pos + nco]
        pos += nco
        to_refs = refs[pos:pos + len(t_outs)]
        pos += len(t_outs)
        dw_ref = None
        if xin is not None:
            dw_ref = refs[pos]
            pos += 1
        ext_ref, conv_ref = refs[pos], refs[pos + 1]
        i = pl.program_id(1)

        ext_ref[0, 0:HALO, :] = jnp.where(i == 0, 0.0, prev_ref[...].astype(F32))
        ext_ref[0, HALO:HALO + tm, :] = main_ref[...].astype(F32)
        ext_ref[0, HALO + tm:, :] = jnp.where(i == n_i - 1, 0.0, next_ref[...].astype(F32))
        if dw_ref is not None:
            @pl.when(i == 0)
            def _():
                dw_ref[...] = jnp.zeros_like(dw_ref)

        n_sh = tm + 2 * HALO - SUBLANES
        for c0 in range(0, cw, cbw):
            for sft in range(1, n_copies):
                ext_ref[sft, 0:n_sh, c0:c0 + cbw] = ext_ref[0, sft:sft + n_sh, c0:c0 + cbw]

        for c0 in range(0, cw, cbw):
            for r0 in range(0, tm, rb):
                acc = jnp.zeros((rb, cbw), F32)
                if xin_ref is not None:
                    xblk = xin_ref[r0:r0 + rb, c0:c0 + cbw].astype(F32)
                for k in range(K):
                    off = HALO + r0 + ((pad - k) if reverse else (k - pad))
                    sft = off % SUBLANES if n_copies > 1 else 0
                    d = ext_ref[sft, off - sft:off - sft + rb, c0:c0 + cbw]
                    acc = acc + d * w_ref[k:k + 1, c0:c0 + cbw]
                    if xin_ref is not None:
                        dw_ref[k, :, c0:c0 + cbw] += _rows8(xblk * d)
                conv_ref[r0:r0 + rb, c0:c0 + cbw] = acc

        res = epi(conv_ref[...], [_f32(r[...]) for r in e_refs])
        rows, cols = res[0], res[1]
        for v, o in zip(rows, ro_refs):
            o[...] = v.astype(o.dtype)
        for v, o in zip(res[2] if len(res) > 2 else (), to_refs):
            o[...] = v.T.astype(o.dtype)
        for v, o in zip(cols, co_refs):
            v8 = _rows8(v)

            @pl.when(i == 0)
            def _():
                o[...] = v8

            @pl.when(i > 0)
            def _():
                o[...] += v8

    in_specs = [
        pl.BlockSpec((tm, cw), lambda cj, i: (i, src_c0 + cj)),
        pl.BlockSpec((HALO, cw), lambda cj, i: (jnp.maximum(i * R - 1, 0), src_c0 + cj)),
        pl.BlockSpec((HALO, cw), lambda cj, i: (jnp.minimum((i + 1) * R, nblk - 1), src_c0 + cj)),
        pl.BlockSpec((Kp, cw), lambda cj, i: (0, w_c0 + cj)),
    ]
    args = [src, src, src, w]
    if xin is not None:
        in_specs.append(pl.BlockSpec((tm, cw), lambda cj, i, c0=xin[1]: (i, c0 + cj)))
        args.append(xin[0])
    for arr, kind, width, c0, cmul in extras:
        if kind == 'row':
            in_specs.append(pl.BlockSpec((tm, width), lambda cj, i, c0=c0, cmul=cmul: (i, c0 + cmul * cj)))
        else:
            in_specs.append(pl.BlockSpec((arr.shape[0], width), lambda cj, i, c0=c0, cmul=cmul: (0, c0 + cmul * cj)))
        args.append(arr)
    aliases = {}
    if passthrough is not None:
        in_specs.append(pl.BlockSpec(memory_space=pl.ANY))
        aliases = {len(args): passthrough[1]}
        args.append(passthrough[0])
    out_shape = []
    out_specs = []
    for n_total, dtype, width, c0, cmul in row_outs:
        out_shape.append(jax.ShapeDtypeStruct((M, n_total), dtype))
        out_specs.append(pl.BlockSpec((tm, width), lambda cj, i, c0=c0, cmul=cmul: (i, c0 + cmul * cj)))
    for n_total, width, c0, cmul in col_outs:
        out_shape.append(jax.ShapeDtypeStruct((SUBLANES, n_total), F32))
        out_specs.append(pl.BlockSpec((SUBLANES, width), lambda cj, i, c0=c0, cmul=cmul: (0, c0 + cmul * cj)))
    for n_total, dtype, width, c0, cmul in t_outs:
        out_shape.append(jax.ShapeDtypeStruct((n_total, M), dtype))
        out_specs.append(pl.BlockSpec((width, tm), lambda cj, i, c0=c0, cmul=cmul: (c0 + cmul * cj, i)))
    if xin is not None:
        out_shape.append(jax.ShapeDtypeStruct((Kp, SUBLANES, cw * nc), F32))
        out_specs.append(pl.BlockSpec((Kp, SUBLANES, cw), lambda cj, i: (0, 0, cj)))
    return pl.pallas_call(
        body, name=name, grid=(nc, n_i), in_specs=in_specs, out_specs=out_specs,
        out_shape=out_shape, input_output_aliases=aliases,
        scratch_shapes=[pltpu.VMEM((n_copies, tm + 2 * HALO, cw), F32), pltpu.VMEM((tm, cw), F32)],
        compiler_params=_params(dimension_semantics=("arbitrary", "arbitrary")),
    )(*args)


def _split_dot(m_bf16, v, n_pass, dims=None):
    out = None
    rest = v
    for p in range(n_pass):
        piece = rest.astype(BF16)
        if p + 1 < n_pass:
            rest = rest - piece.astype(F32)
        if dims is None:
            t = jnp.dot(m_bf16, piece, preferred_element_type=F32)
        else:
            t = lax.dot_general(m_bf16, piece, dims, preferred_element_type=F32)
        out = t if out is None else out + t
    return out


def _split_dot_r(v, m_bf16, n_pass):
    out = None
    rest = v
    for p in range(n_pass):
        piece = rest.astype(BF16)
        if p + 1 < n_pass:
            rest = rest - piece.astype(F32)
        t = jnp.dot(piece, m_bf16, preferred_element_type=F32)
        out = t if out is None else out + t
    return out


def _softplus(x):
    return jnp.maximum(x, 0.0) + jnp.log1p(jnp.exp(-jnp.abs(x)))


NT_DIMS = (((1,), (1,)), ((), ()))
TN_DIMS = (((0,), (0,)), ((), ()))


def _ssd_common(dtraw, dtbT, alogT, rev, n_heads):
    L = CHUNK
    if rev:
        dtraw = pltpu.roll(dtraw, LANES - n_heads, 1)
    preT = dtraw.T + dtbT
    dtT = _softplus(preT)
    AT = -jnp.exp(alogT)
    aT = dtT * AT
    ri = lax.broadcasted_iota(jnp.int32, (L, L), 0)
    ci = lax.broadcasted_iota(jnp.int32, (L, L), 1)
    up = (ri >= ci) if rev else (ri <= ci)
    lo = (ri <= ci) if rev else (ri >= ci)
    csT = _split_dot_r(aT, up.astype(BF16), 3)
    last = 0 if rev else L - 1
    lastB = jnp.broadcast_to(csT[:, last:last + 1], (L, L))
    return dict(preT=preT, dtT=dtT, AT=AT, csT=csT, cs=csT.T, up=up, lo=lo, ci=ci, last=last,
                doutT=jnp.exp(csT), dstT=jnp.exp(lastB - csT), totB=jnp.exp(lastB))


def ssd_fwd(name, xsT, bc, dtraw, dtbT, alogT, *, S, DI, G, H, rev, tail=None):
    NC = S // CHUNK
    R = H // G
    GW = R * HEAD_DIM
    N = D_STATE
    P = HEAD_DIM

    def body(*refs):
        xsT_ref, bc_ref, dtraw_ref, dtb_ref, alog_ref = refs[:5]
        if tail is None:
            y_ref, st_ref, h_ref = refs[5:]
        else:
            yo_ref, z_ref, xs_ref, dsk_ref, ng_ref = refs[5:10]
            y_ref, st_ref, yn_ref, h_ref = refs[10:]
        c = pl.program_id(0)

        @pl.when(c == 0)
        def _():
            h_ref[...] = jnp.zeros_like(h_ref)

        q = _ssd_common(dtraw_ref[...], dtb_ref[...], alog_ref[...], rev, H)
        cs, csT, dtT, doutT, totB = q['cs'], q['csT'], q['dtT'], q['doutT'], q['totB']
        wstT = q['dstT'] * dtT
        GB = 2 if G % 2 == 0 else 1
        for g0 in range(0, G, GB):
            gs = list(range(g0, g0 + GB))
            Bgs = [bc_ref[:, g * N:(g + 1) * N].astype(BF16) for g in gs]
            Cgs = [bc_ref[:, G * N + g * N:G * N + (g + 1) * N].astype(BF16) for g in gs]
            CBTs = [lax.dot_general(b, c_, NT_DIMS, preferred_element_type=F32) for b, c_ in zip(Bgs, Cgs)]
            HTs = [h_ref[g] for g in gs]
            yoffTs = [lax.dot_general(HT.astype(BF16), c_, NT_DIMS, preferred_element_type=F32)
                      for HT, c_ in zip(HTs, Cgs)]
            xTs = [xsT_ref[g * GW:(g + 1) * GW, :] for g in gs]
            heads = [(k, r) for k in range(GB) for r in range(R)]
            hs = [gs[k] * R + r for k, r in heads]
            blks = [slice(r * P, (r + 1) * P) for _, r in heads]
            segs = [jnp.where(q['up'], csT[h:h + 1, :] - cs[:, h:h + 1], -1e30) for h in hs]
            GTs = [(CBTs[k] * jnp.exp(sg)).astype(BF16) for (k, _), sg in zip(heads, segs)]
            xThs = [xTs[k][b, :] for (k, _), b in zip(heads, blks)]
            XThs = [(xTh * dtT[h:h + 1, :]).astype(BF16) for xTh, h in zip(xThs, hs)]
            ydTs = [jnp.dot(a, GT, preferred_element_type=F32) for a, GT in zip(XThs, GTs)]
            ys = [ydT + yoffTs[k][b, :] * doutT[h:h + 1, :] for ydT, (k, _), b, h in zip(ydTs, heads, blks, hs)]
            xws = [xTh * wstT[h:h + 1, :] for xTh, h in zip(xThs, hs)]
            tots = [jnp.broadcast_to(totB[h:h + 1, :], (P, N)) for h in hs]
            for k, g in enumerate(gs):
                sel = slice(k * R, (k + 1) * R)
                y_ref[:, g * GW:(g + 1) * GW] = jnp.concatenate(ys[sel], axis=0).T
                xwT = jnp.concatenate(xws[sel], axis=0).astype(BF16)
                ST = jnp.dot(xwT, Bgs[k], preferred_element_type=F32)
                st_ref[0, g] = HTs[k]
                h_ref[g] = HTs[k] * jnp.concatenate(tots[sel], axis=0) + ST
        if tail is not None:
            y = y_ref[...] + yo_ref[...]
            y_ref[...] = y
            z = _f32(z_ref[...])
            yz = (y + xs_ref[...] * dsk_ref[...]) * (z * _sig(z))
            for g in range(G):
                t = yz[:, g * GW:(g + 1) * GW]
                tn = t * lax.rsqrt(jnp.mean(t * t, axis=-1, keepdims=True) + RMS_EPS)
                yn_ref[:, g * GW:(g + 1) * GW] = (tn * ng_ref[:, g * GW:(g + 1) * GW]).astype(BF16)

    cidx = (lambda c: NC - 1 - c) if rev else (lambda c: c)
    cmap = lambda c: (cidx(c), 0)
    smap = lambda c: (cidx(c), 0, 0, 0)
    const = lambda c: (0, 0)
    tmap = lambda c: (0, cidx(c))
    in_specs = [pl.BlockSpec((DI, CHUNK), tmap), pl.BlockSpec((CHUNK, 2 * G * N), cmap), pl.BlockSpec((CHUNK, LANES), cmap),
                pl.BlockSpec((LANES, LANES), const), pl.BlockSpec((LANES, LANES), const)]
    out_specs = [pl.BlockSpec((CHUNK, DI), cmap), pl.BlockSpec((1, G, GW, N), smap)]
    out_shape = [jax.ShapeDtypeStruct((S, DI), F32), jax.ShapeDtypeStruct((NC, G, GW, N), F32)]
    args = [xsT, bc, dtraw, dtbT, alogT]
    if tail is not None:
        y_other, (z_arr, z_blk), xs_row, dsk, ng = tail
        in_specs += [pl.BlockSpec((CHUNK, DI), cmap), pl.BlockSpec((CHUNK, DI), lambda c: (cidx(c), z_blk)),
                     pl.BlockSpec((CHUNK, DI), cmap), pl.BlockSpec((1, DI), const), pl.BlockSpec((1, DI), const)]
        out_specs.append(pl.BlockSpec((CHUNK, DI), cmap))
        out_shape.append(jax.ShapeDtypeStruct((S, DI), BF16))
        args += [y_other, z_arr, xs_row, dsk, ng]
    return pl.pallas_call(
        body, name=name, grid=(NC,), in_specs=in_specs, out_specs=out_specs, out_shape=out_shape,
        scratch_shapes=[pltpu.VMEM((G, GW, N), F32)],
        compiler_params=_params(dimension_semantics=("arbitrary",)),
    )(*args)


def ssd_bwd(name, xsT, bc, dtraw, dyT, st, dtbT, alogT, *, S, DI, G, H, rev, tail=None):
    NC = S // CHUNK
    R = H // G
    GW = R * HEAD_DIM
    N = D_STATE
    XBC = DI + 2 * G * N
    P = HEAD_DIM
    L = CHUNK

    def body(*refs):
        xsT_ref, bc_ref, dtraw_ref, dyT_ref, st_ref, dtb_ref, alog_ref = refs[:7]
        if tail is None:
            dxbc_ref, ddt_ref, da_ref, dh_ref, dcst_ref, p2t_ref, p3t_ref, e2t_ref = refs[7:]
        else:
            other_ref, cbx_ref, cbbc_ref, dskT_ref = refs[7:11]
            dxbc_ref, ddt_ref, da_ref, dcol_ref, dh_ref, dcst_ref, p2t_ref, p3t_ref, e2t_ref = refs[11:]
        c = pl.program_id(0)

        @pl.when(c == 0)
        def _():
            dh_ref[...] = jnp.zeros_like(dh_ref)
            da_ref[...] = jnp.zeros_like(da_ref)
            dcst_ref[...] = jnp.zeros_like(dcst_ref)
            p2t_ref[...] = jnp.zeros_like(p2t_ref)
            p3t_ref[...] = jnp.zeros_like(p3t_ref)
            e2t_ref[...] = jnp.zeros_like(e2t_ref)

        q = _ssd_common(dtraw_ref[...], dtb_ref[...], alog_ref[...], rev, H)
        cs, csT, dtT, doutT, dstT, totB = q['cs'], q['csT'], q['dtT'], q['doutT'], q['dstT'], q['totB']
        wstT = dstT * dtT
        lane = q['ci']
        GB = 2 if G % 2 == 0 else 1
        for g0 in range(0, G, GB):
            gs = list(range(g0, g0 + GB))
            Bgs = [bc_ref[:, g * N:(g + 1) * N].astype(BF16) for g in gs]
            Cgs = [bc_ref[:, G * N + g * N:G * N + (g + 1) * N].astype(BF16) for g in gs]
            CBs = [lax.dot_general(c_, b, NT_DIMS, preferred_element_type=F32) for b, c_ in zip(Bgs, Cgs)]
            HpTs = [st_ref[0, g] for g in gs]
            HpTbs = [v.astype(BF16) for v in HpTs]
            dHTs = [dh_ref[g] for g in gs]
            dHTbs = [v.astype(BF16) for v in dHTs]
            BdHTs = [lax.dot_general(d, b, NT_DIMS, preferred_element_type=F32) for d, b in zip(dHTbs, Bgs)]
            yoffTs = [lax.dot_general(hp, c_, NT_DIMS, preferred_element_type=F32) for hp, c_ in zip(HpTbs, Cgs)]
            xTs = [xsT_ref[g * GW:(g + 1) * GW, :] for g in gs]
            dyTs = [dyT_ref[g * GW:(g + 1) * GW, :] for g in gs]
            heads = [(k, r) for k in range(GB) for r in range(R)]
            ks = [k for k, _ in heads]
            hs = [gs[k] * R + r for k, r in heads]
            blks = [slice(r * P, (r + 1) * P) for _, r in heads]
            Lms = [jnp.exp(jnp.where(q['lo'], cs[:, h:h + 1] - csT[h:h + 1, :], -1e30)) for h in hs]
            xThs = [xTs[k][b, :] for k, b in zip(ks, blks)]
            dyThs = [dyTs[k][b, :] for k, b in zip(ks, blks)]
            xThbs = [v.astype(BF16) for v in xThs]
            dyThbs = [v.astype(BF16) for v in dyThs]
            dGxs = [lax.dot_general(a, b, TN_DIMS, preferred_element_type=F32) for a, b in zip(dyThbs, xThbs)]
            Gms = [(CBs[k] * Lm).astype(BF16) for k, Lm in zip(ks, Lms)]
            XThbs = [(xTh * dtT[h:h + 1, :]).astype(BF16) for xTh, h in zip(xThs, hs)]
            u1Ts = [jnp.dot(a, Gm, preferred_element_type=F32) for a, Gm in zip(dyThbs, Gms)]
            ydTs = [lax.dot_general(a, Gm, NT_DIMS, preferred_element_type=F32) for a, Gm in zip(XThbs, Gms)]
            Ts = [dGx * (Lm * dtT[h:h + 1, :]) for dGx, Lm, h in zip(dGxs, Lms, hs)]
            uTs = [u1T + BdHTs[k][b, :] * dstT[h:h + 1, :] for u1T, k, b, h in zip(u1Ts, ks, blks, hs)]
            dyds = [dyTh * doutT[h:h + 1, :] for dyTh, h in zip(dyThs, hs)]
            xws = [xTh * wstT[h:h + 1, :] for xTh, h in zip(xThs, hs)]
            for i, h in enumerate(hs):
                k, b = ks[i], blks[i]
                p3row = jnp.sum(xws[i] * BdHTs[k][b, :], axis=0, keepdims=True)
                seg_row = jnp.sum(_f32(dyThbs[i]) * ydTs[i], axis=0, keepdims=True)
                seg_col = jnp.sum(_f32(XThbs[i]) * u1Ts[i], axis=0, keepdims=True)
                dcst_ref[h:h + 1, :] = (jnp.sum(dyds[i] * yoffTs[k][b, :], axis=0, keepdims=True)
                                        + seg_row - seg_col - p3row)
                p2t_ref[h:h + 1, :] = jnp.sum(xThs[i] * uTs[i], axis=0, keepdims=True)
                p3t_ref[h:h + 1, :] = p3row
                e2t_ref[h:h + 1, :] = jnp.sum(HpTs[k][b, :] * dHTs[k][b, :], axis=0, keepdims=True)
            dxs = [uT * dtT[h:h + 1, :] for uT, h in zip(uTs, hs)]
            if tail is not None:
                dxs = [d + dyTh * dskT_ref[h * P:(h + 1) * P, :] for d, dyTh, h in zip(dxs, dyThs, hs)]
            tots = [jnp.broadcast_to(totB[h:h + 1, :], (P, N)) for h in hs]
            for k, g in enumerate(gs):
                sel = slice(k * R, (k + 1) * R)
                dCB = Ts[k * R]
                for T in Ts[k * R + 1:(k + 1) * R]:
                    dCB = dCB + T
                dxbc_ref[:, g * GW:(g + 1) * GW] = jnp.concatenate(dxs[sel], axis=0).T
                dydT = jnp.concatenate(dyds[sel], axis=0).astype(BF16)
                xwT = jnp.concatenate(xws[sel], axis=0).astype(BF16)
                dCBb = dCB.astype(BF16)
                dC = (jnp.dot(dCBb, Bgs[k], preferred_element_type=F32)
                      + lax.dot_general(dydT, HpTbs[k], TN_DIMS, preferred_element_type=F32))
                dB = (lax.dot_general(dCBb, Cgs[k], TN_DIMS, preferred_element_type=F32)
                      + lax.dot_general(xwT, dHTbs[k], TN_DIMS, preferred_element_type=F32))
                dxbc_ref[:, DI + g * N:DI + (g + 1) * N] = dB
                dxbc_ref[:, DI + G * N + g * N:DI + G * N + (g + 1) * N] = dC
                dh_ref[g] = (dHTs[k] * jnp.concatenate(tots[sel], axis=0)
                             + jnp.dot(dydT, Cgs[k], preferred_element_type=F32))
        e1 = jnp.sum(p3t_ref[...], axis=1, keepdims=True)
        e2 = jnp.sum(e2t_ref[...], axis=1, keepdims=True)
        dcsT = dcst_ref[...] + jnp.where(lane == q['last'], e1 + totB * e2, 0.0)
        daT = _split_dot_r(dcsT, q['lo'].astype(BF16), 3)
        ddtT = daT * q['AT'] + p2t_ref[...]
        da_ref[...] += daT * dtT
        ddraw = jnp.where(lane < H, (ddtT * _sig(q['preT'])).T, 0.0)
        if rev:
            ddraw = pltpu.roll(ddraw, H, 1)
        ddt_ref[...] = ddraw
        if tail is not None:
            for c0, cb_ref in ((0, cbx_ref), (DI, cbbc_ref)):
                d = dxbc_ref[:, c0:c0 + DI] + other_ref[:, c0:c0 + DI]
                cb = cb_ref[...]
                dcb = d * _dsilu(cb, _sig(cb))
                dxbc_ref[:, c0:c0 + DI] = dcb
                part = _rows8(dcb)

                @pl.when(c == 0)
                def _():
                    dcol_ref[:, c0:c0 + DI] = part

                @pl.when(c > 0)
                def _():
                    dcol_ref[:, c0:c0 + DI] += part

    cmap = (lambda c: (c, 0)) if rev else (lambda c: (NC - 1 - c, 0))
    smap = (lambda c: (c, 0, 0, 0)) if rev else (lambda c: (NC - 1 - c, 0, 0, 0))
    const = lambda c: (0, 0)
    sq = pltpu.VMEM((LANES, CHUNK), F32)
    cix = (lambda c: c) if rev else (lambda c: NC - 1 - c)
    tmap = lambda c: (0, cix(c))
    in_specs = [pl.BlockSpec((DI, CHUNK), tmap), pl.BlockSpec((CHUNK, 2 * G * N), cmap), pl.BlockSpec((CHUNK, LANES), cmap),
                pl.BlockSpec((DI, CHUNK), tmap),
                pl.BlockSpec((1, G, GW, N), smap),
                pl.BlockSpec((LANES, LANES), const), pl.BlockSpec((LANES, LANES), const)]
    out_specs = [pl.BlockSpec((CHUNK, XBC), cmap), pl.BlockSpec((CHUNK, LANES), cmap),
                 pl.BlockSpec((LANES, LANES), const)]
    out_shape = [jax.ShapeDtypeStruct((S, XBC), F32), jax.ShapeDtypeStruct((S, LANES), F32),
                 jax.ShapeDtypeStruct((LANES, LANES), F32)]
    args = [xsT, bc, dtraw, dyT, st, dtbT, alogT]
    if tail is not None:
        in_specs += [pl.BlockSpec((CHUNK, XBC), cmap), pl.BlockSpec((CHUNK, DI), cmap),
                     pl.BlockSpec((CHUNK, 2 * G * N), cmap), pl.BlockSpec((DI, LANES), const)]
        out_specs.append(pl.BlockSpec((SUBLANES, XBC), const))
        out_shape.append(jax.ShapeDtypeStruct((SUBLANES, XBC), F32))
        args += list(tail)
    return pl.pallas_call(
        body, name=name, grid=(NC,), in_specs=in_specs, out_specs=out_specs, out_shape=out_shape,
        scratch_shapes=[pltpu.VMEM((G, GW, N), F32), sq, sq, sq, sq],
        compiler_params=_params(dimension_semantics=("arbitrary",)),
    )(*args)


ANY = pl.BlockSpec(memory_space=pl.ANY)


def chip_exchange(name, groups, gather):
    flat = [arr for grp in groups for arr in grp]
    n_in = len(flat)
    n_out = len(groups)
    n_rc = 3 * n_in

    def body(*refs):
        in_refs = refs[:n_in]
        out_refs = refs[n_in:n_in + n_out]
        send, recv = refs[n_in + n_out:]
        x, y, c = lax.axis_index("x"), lax.axis_index("y"), lax.axis_index("c")
        me = 2 * x + y
        peers = [(1 - x, y), (x, 1 - y), (1 - x, 1 - y)]
        remote = []
        q = 0
        for a, grp in enumerate(groups):
            for l in range(len(grp)):
                src = in_refs[q]
                dst = out_refs[a].at[me] if gather else out_refs[a].at[me, l]
                for j, (px, py) in enumerate(peers):
                    blk = src if gather else src.at[2 * px + py]
                    rc = pltpu.make_async_remote_copy(
                        src_ref=blk, dst_ref=dst, send_sem=send.at[3 * q + j], recv_sem=recv.at[3 * q + j],
                        device_id=(px, py, c), device_id_type=MESH)
                    rc.start()
                    remote.append(rc)
                q += 1
        for rc in remote:
            rc.wait()

    out_shape = []
    for grp in groups:
        a0 = grp[0]
        if gather:
            out_shape.append(jax.ShapeDtypeStruct((4,) + a0.shape, a0.dtype))
        else:
            out_shape.append(jax.ShapeDtypeStruct((4, len(grp)) + a0.shape[1:], a0.dtype))
    outs = pl.pallas_call(
        body, name=name, in_specs=[ANY] * n_in, out_specs=[ANY] * n_out, out_shape=out_shape,
        scratch_shapes=[pltpu.SemaphoreType.DMA((n_rc,)), pltpu.SemaphoreType.DMA((n_rc,))],
    )(*flat)
    me = _chip_index()
    res = []
    for grp, o in zip(groups, outs):
        for l, src in enumerate(grp):
            o = _put_block(o, src, (me,)) if gather else _put_block(o, _take_block(src, me), (me, l))
        res.append(o)
    return res


def _chip_index():
    return 2 * lax.axis_index("x") + lax.axis_index("y")


def _take_block(arr, idx):
    return lax.dynamic_index_in_dim(arr, idx, 0, keepdims=False)


def _put_block(dst, blk, idx):
    lead = len(idx)
    return lax.dynamic_update_slice(dst, blk.reshape((1,) * lead + blk.shape), tuple(idx) + (0,) * (dst.ndim - lead))


def gather_layer(name, split, whole):
    ns, nw = len(split), len(whole)
    n = ns + nw
    n_rc = 3 * (n + ns)

    def body(*refs):
        in_refs = refs[:n]
        out_refs = refs[n:2 * n]
        send, recv = refs[2 * n:]
        x, y, c = lax.axis_index("x"), lax.axis_index("y"), lax.axis_index("c")
        me = 2 * x + y
        sibling = (x, y, 1 - c)
        peers = [(1 - x, y), (x, 1 - y), (1 - x, 1 - y)]

        def region(a, chip, half):
            if a >= ns:
                return out_refs[a].at[chip]
            hr = split[a].shape[0] // 2
            return out_refs[a].at[chip, pl.ds(half * hr, hr)]

        def mine(a):
            if a >= ns:
                return in_refs[a]
            hr = split[a].shape[0] // 2
            return in_refs[a].at[pl.ds(c * hr, hr)]

        sends = []
        for a in range(n):
            for j, (px, py) in enumerate(peers):
                rc = pltpu.make_async_remote_copy(
                    src_ref=mine(a), dst_ref=region(a, me, c), send_sem=send.at[3 * a + j],
                    recv_sem=recv.at[3 * a + j], device_id=(px, py, c), device_id_type=MESH)
                rc.start()
                sends.append(rc)
        for a in range(n):
            for j, (px, py) in enumerate(peers):
                chip = 2 * px + py
                landed = pltpu.make_async_remote_copy(
                    src_ref=mine(a), dst_ref=region(a, chip, c), send_sem=send.at[3 * a + j],
                    recv_sem=recv.at[3 * a + j], device_id=(px, py, c), device_id_type=MESH)
                landed.wait_recv()
                if a < ns:
                    fw = pltpu.make_async_remote_copy(
                        src_ref=region(a, chip, c), dst_ref=region(a, chip, c), send_sem=send.at[3 * n + 3 * a + j],
                        recv_sem=recv.at[3 * n + 3 * a + j], device_id=sibling, device_id_type=MESH)
                    fw.start()
                    sends.append(fw)
        for a in range(ns):
            for j, (px, py) in enumerate(peers):
                chip = 2 * px + py
                pltpu.make_async_remote_copy(
                    src_ref=region(a, chip, 1 - c), dst_ref=region(a, chip, 1 - c), send_sem=send.at[3 * n + 3 * a + j],
                    recv_sem=recv.at[3 * n + 3 * a + j], device_id=sibling, device_id_type=MESH).wait_recv()
        for rc in sends:
            rc.wait_send()

    arrs = list(split) + list(whole)
    outs = pl.pallas_call(
        body, name=name, in_specs=[ANY] * n, out_specs=[ANY] * n,
        out_shape=[jax.ShapeDtypeStruct((4,) + a.shape, a.dtype) for a in arrs],
        scratch_shapes=[pltpu.SemaphoreType.DMA((n_rc,)), pltpu.SemaphoreType.DMA((n_rc,))],
    )(*arrs)
    me = _chip_index()
    return [_put_block(o, a, (me,)) for o, a in zip(outs, arrs)]


HBM_SPEC = pl.BlockSpec(memory_space=pltpu.HBM)
SEM_SPEC = pl.BlockSpec(memory_space=pltpu.SEMAPHORE)
IN_FLIGHT = pltpu.SideEffectType.DATAFLOW_SIDE_EFFECTING


def _chip_leg(kind, a_ref, l_ref, shape, c, me, chip):
    if kind == 'gather':
        hr = shape[0] // 2
        rows = pl.ds(c * hr, hr)
        return a_ref.at[rows], l_ref.at[me, rows], l_ref.at[chip, rows]
    return a_ref.at[chip], l_ref.at[me], l_ref.at[chip]


def chip_legs_start(name, kind, arrs, lands):
    n = len(arrs)

    def body(*refs):
        a_refs = refs[:n]
        l_refs = refs[n:2 * n]
        send, recv = refs[2 * n], refs[2 * n + 1]
        token = refs[-1]
        x, y, c = lax.axis_index("x"), lax.axis_index("y"), lax.axis_index("c")
        me = 2 * x + y
        for a in range(n):
            for j, (px, py) in enumerate([(1 - x, y), (x, 1 - y), (1 - x, 1 - y)]):
                src, dst, _ = _chip_leg(kind, a_refs[a], l_refs[a], arrs[a].shape, c, me, 2 * px + py)
                pltpu.make_async_remote_copy(src_ref=src, dst_ref=dst, send_sem=send.at[3 * a + j],
                                             recv_sem=recv.at[3 * a + j], device_id=(px, py, c),
                                             device_id_type=MESH).start()
        token[...] = jnp.zeros_like(token)

    both = list(arrs) + list(lands)
    outs = pl.pallas_call(
        body, name=name,
        out_shape=(pltpu.SemaphoreType.DMA((3 * n,)), pltpu.SemaphoreType.DMA((3 * n,)),
                   *[pltpu.HBM(a.shape, a.dtype) for a in both], jax.ShapeDtypeStruct((SUBLANES, LANES), F32)),
        in_specs=[HBM_SPEC] * (2 * n),
        out_specs=(SEM_SPEC, SEM_SPEC, *[HBM_SPEC] * (2 * n), pl.BlockSpec(memory_space=pltpu.VMEM)),
        input_output_aliases={i: 2 + i for i in range(2 * n)},
        compiler_params=pltpu.CompilerParams(has_side_effects=IN_FLIGHT),
    )(*[pltpu.with_memory_space_constraint(a, pltpu.HBM) for a in both])
    return outs[0], outs[1], list(outs[2:2 + n]), list(outs[2 + n:2 + 2 * n]), outs[-1]


def chip_legs_wait(name, kind, send, recv, arrs, lands, after):
    n = len(arrs)

    def body(*refs):
        a_refs = refs[:n]
        l_refs = refs[n:2 * n]
        send_, recv_ = refs[2 * n], refs[2 * n + 1]
        x, y, c = lax.axis_index("x"), lax.axis_index("y"), lax.axis_index("c")
        me = 2 * x + y
        legs = []
        for a in range(n):
            for j, (px, py) in enumerate([(1 - x, y), (x, 1 - y), (1 - x, 1 - y)]):
                src, dst, landing = _chip_leg(kind, a_refs[a], l_refs[a], arrs[a].shape, c, me, 2 * px + py)
                legs.append(pltpu.make_async_remote_copy(src_ref=src, dst_ref=landing, send_sem=send_.at[3 * a + j],
                                                         recv_sem=recv_.at[3 * a + j], device_id=(px, py, c),
                                                         device_id_type=MESH))
        for leg in legs:
            leg.wait_send()
        for leg in legs:
            leg.wait_recv()

    both = list(arrs) + list(lands)
    outs = pl.pallas_call(
        body, name=name, out_shape=tuple(pltpu.HBM(a.shape, a.dtype) for a in both),
        in_specs=[HBM_SPEC] * (2 * n) + [SEM_SPEC, SEM_SPEC, ANY], out_specs=tuple([HBM_SPEC] * (2 * n)),
        input_output_aliases={i: i for i in range(2 * n)},
        compiler_params=pltpu.CompilerParams(has_side_effects=IN_FLIGHT),
    )(*both, send, recv, after)
    return list(outs[n:])


def gather_finish(name, split, landed):
    n = len(split)

    def body(*refs):
        out_refs = refs[n:2 * n]
        send, recv = refs[2 * n:]
        x, y, c = lax.axis_index("x"), lax.axis_index("y"), lax.axis_index("c")
        sibling = (x, y, 1 - c)
        chips = [2 * (1 - x) + y, 2 * x + (1 - y), 2 * (1 - x) + (1 - y)]

        def region(a, chip, half):
            hr = split[a].shape[0] // 2
            return out_refs[a].at[chip, pl.ds(half * hr, hr)]

        sends = []
        for a in range(n):
            for j, chip in enumerate(chips):
                fw = pltpu.make_async_remote_copy(
                    src_ref=region(a, chip, c), dst_ref=region(a, chip, c), send_sem=send.at[3 * a + j],
                    recv_sem=recv.at[3 * a + j], device_id=sibling, device_id_type=MESH)
                fw.start()
                sends.append(fw)
        for a in range(n):
            for j, chip in enumerate(chips):
                pltpu.make_async_remote_copy(
                    src_ref=region(a, chip, 1 - c), dst_ref=region(a, chip, 1 - c), send_sem=send.at[3 * a + j],
                    recv_sem=recv.at[3 * a + j], device_id=sibling, device_id_type=MESH).wait_recv()
        for fw in sends:
            fw.wait_send()

    outs = pl.pallas_call(
        body, name=name, in_specs=[ANY] * n, out_specs=[ANY] * n,
        out_shape=[jax.ShapeDtypeStruct(a.shape, a.dtype) for a in landed],
        input_output_aliases={a: a for a in range(n)},
        scratch_shapes=[pltpu.SemaphoreType.DMA((3 * n,)), pltpu.SemaphoreType.DMA((3 * n,))],
    )(*landed)
    me = _chip_index()
    return [_put_block(o, a, (me,)) for o, a in zip(outs, split)]


def place_own(arrs, landed):
    me = _chip_index()
    return [_put_block(l, _take_block(a, me), (me,)) for a, l in zip(arrs, landed)]


def core_send_half(name, arrs):
    n = len(arrs)

    def body(*refs):
        in_refs = refs[:n]
        out_refs = refs[n:2 * n]
        send, recv = refs[2 * n:]
        c = lax.axis_index("c")
        peer = (lax.axis_index("x"), lax.axis_index("y"), 1 - c)
        rcs = []
        for a in range(n):
            hr = arrs[a].shape[1] // 2
            rc = pltpu.make_async_remote_copy(
                src_ref=in_refs[a].at[:, pl.ds((1 - c) * hr, hr)], dst_ref=out_refs[a], send_sem=send.at[a],
                recv_sem=recv.at[a], device_id=peer, device_id_type=MESH)
            rc.start()
            rcs.append(rc)
        for rc in rcs:
            rc.wait()

    return pl.pallas_call(
        body, name=name, in_specs=[ANY] * n, out_specs=[ANY] * n,
        out_shape=[jax.ShapeDtypeStruct((4, a.shape[1] // 2, a.shape[2]), a.dtype) for a in arrs],
        scratch_shapes=[pltpu.SemaphoreType.DMA((n,)), pltpu.SemaphoreType.DMA((n,))],
    )(*arrs)


def core_fill(name, arrs, layer, n_layers):
    n = len(arrs)

    def body(*refs):
        out_refs = refs[n:2 * n]
        send, recv = refs[2 * n:]
        c = lax.axis_index("c")
        peer = (lax.axis_index("x"), lax.axis_index("y"), 1 - c)
        rcs = []
        for a in range(n):
            r = arrs[a].shape[0] // n_layers
            hr = r // 2
            rows = out_refs[a].at[pl.ds(layer * r + c * hr, hr)]
            rc = pltpu.make_async_remote_copy(src_ref=rows, dst_ref=rows, send_sem=send.at[a], recv_sem=recv.at[a],
                                              device_id=peer, device_id_type=MESH)
            rc.start()
            rcs.append(rc)
        for a in range(n):
            r = arrs[a].shape[0] // n_layers
            hr = r // 2
            theirs = out_refs[a].at[pl.ds(layer * r + (1 - c) * hr, hr)]
            pltpu.make_async_remote_copy(src_ref=theirs, dst_ref=theirs, send_sem=send.at[a], recv_sem=recv.at[a],
                                         device_id=peer, device_id_type=MESH).wait_recv()
        for rc in rcs:
            rc.wait_send()

    return pl.pallas_call(
        body, name=name, in_specs=[ANY] * n, out_specs=[ANY] * n,
        out_shape=[jax.ShapeDtypeStruct(a.shape, a.dtype) for a in arrs],
        input_output_aliases={a: a for a in range(n)},
        scratch_shapes=[pltpu.SemaphoreType.DMA((n,)), pltpu.SemaphoreType.DMA((n,))],
    )(*arrs)


def sibling_swap(name, arrs):
    n = len(arrs)

    def body(*refs):
        in_refs = refs[:n]
        out_refs = refs[n:2 * n]
        send, recv = refs[2 * n:]
        peer = (lax.axis_index("x"), lax.axis_index("y"), 1 - lax.axis_index("c"))
        rcs = []
        for a in range(n):
            rc = pltpu.make_async_remote_copy(src_ref=in_refs[a], dst_ref=out_refs[a], send_sem=send.at[a],
                                              recv_sem=recv.at[a], device_id=peer, device_id_type=MESH)
            rc.start()
            rcs.append(rc)
        for rc in rcs:
            rc.wait()

    return pl.pallas_call(
        body, name=name, in_specs=[ANY] * n, out_specs=[ANY] * n,
        out_shape=[jax.ShapeDtypeStruct(a.shape, a.dtype) for a in arrs],
        scratch_shapes=[pltpu.SemaphoreType.DMA((n,)), pltpu.SemaphoreType.DMA((n,))],
    )(*arrs)


def all8_gather(name, v):
    flips = [(fx, fy, fc) for fx in (0, 1) for fy in (0, 1) for fc in (0, 1) if (fx, fy, fc) != (0, 0, 0)]

    def body(v_ref, out_ref, send, recv, loc):
        x, y, c = lax.axis_index("x"), lax.axis_index("y"), lax.axis_index("c")
        me = 4 * x + 2 * y + c
        lc = pltpu.make_async_copy(v_ref, out_ref.at[me], loc)
        lc.start()
        rcs = []
        for k, (fx, fy, fc) in enumerate(flips):
            tgt = (x + fx - 2 * x * fx, y + fy - 2 * y * fy, c + fc - 2 * c * fc)
            rc = pltpu.make_async_remote_copy(src_ref=v_ref, dst_ref=out_ref.at[me], send_sem=send.at[k],
                                              recv_sem=recv.at[k], device_id=tgt, device_id_type=MESH)
            rc.start()
            rcs.append(rc)
        lc.wait()
        for rc in rcs:
            rc.wait()

    return pl.pallas_call(
        body, name=name, in_specs=[ANY], out_specs=ANY,
        out_shape=jax.ShapeDtypeStruct((8,) + v.shape, v.dtype),
        scratch_shapes=[pltpu.SemaphoreType.DMA((7,)), pltpu.SemaphoreType.DMA((7,)), pltpu.SemaphoreType.DMA],
    )(v)


def _pick_rows(rows, cols, target_elems=128 * 1024, mult=SUBLANES):
    if rows % mult != 0:
        return rows
    best = mult
    t = mult
    while t <= rows:
        if rows % t == 0 and t * cols <= target_elems:
            best = t
        t += mult
    return best


def sum_chips(name, parts):
    _, R, C = parts.shape
    tm = _pick_rows(R, C)

    def body(p_ref, o_ref):
        o_ref[...] = (p_ref[0] + p_ref[1]) + (p_ref[2] + p_ref[3])

    return pl.pallas_call(
        body, name=name, grid=(R // tm,),
        in_specs=[pl.BlockSpec((4, tm, C), lambda i: (0, i, 0))],
        out_specs=pl.BlockSpec((tm, C), lambda i: (i, 0)),
        out_shape=jax.ShapeDtypeStruct((R, C), F32),
        compiler_params=_params(dimension_semantics=("arbitrary",)),
    )(parts)


def _adamw(g, w, m, v):
    m = ADAM_B1 * m + (1.0 - ADAM_B1) * g
    v = ADAM_B2 * v + (1.0 - ADAM_B2) * (g * g)
    m_hat = m / (1.0 - ADAM_B1 ** ADAM_STEP)
    v_hat = v / (1.0 - ADAM_B2 ** ADAM_STEP)
    delta = -ADAM_LR * (m_hat / (jnp.sqrt(v_hat) + ADAM_EPS) + ADAM_WD * w)
    return delta, m, v


def adamw_shard(name, s_mine, s_sib, w, m, v):
    R, C = w.shape
    tm = _pick_rows(R, C)

    def body(a_ref, b_ref, w_ref, m_ref, v_ref, g_out, d_out, m_out, v_out):
        g = a_ref[...] + b_ref[...]
        d, mn, vn = _adamw(g, w_ref[...], m_ref[...], v_ref[...])
        g_out[...] = g
        d_out[...] = d
        m_out[...] = mn
        v_out[...] = vn

    spec = pl.BlockSpec((tm, C), lambda i: (i, 0))
    return pl.pallas_call(
        body, name=name, grid=(R // tm,), in_specs=[spec] * 5, out_specs=[spec] * 4,
        out_shape=[jax.ShapeDtypeStruct((R, C), F32)] * 4,
        compiler_params=_params(dimension_semantics=("arbitrary",)),
    )(s_mine, s_sib, w, m, v)


def core_sum(name, core, g, got):
    _, r, C = g.shape
    hr = r // 2
    tm = _pick_rows(hr, 4 * C, 256 * 1024, 2 * SUBLANES)
    nh = hr // tm

    def body(c_ref, g_ref, s_ref, o_ref):
        o_ref[...] = (g_ref[...] + s_ref[...]).astype(BF16)

    return pl.pallas_call(
        body, name=name,
        grid_spec=pltpu.PrefetchScalarGridSpec(
            num_scalar_prefetch=1, grid=(nh,),
            in_specs=[pl.BlockSpec((4, tm, C), lambda i, cr: (0, cr[0] * nh + i, 0)),
                      pl.BlockSpec((4, tm, C), lambda i, cr: (0, i, 0))],
            out_specs=pl.BlockSpec((4, tm, C), lambda i, cr: (0, i, 0))),
        out_shape=jax.ShapeDtypeStruct((4, hr, C), BF16),
        compiler_params=_params(dimension_semantics=("arbitrary",)),
    )(core, g, got)


def chip_sum_into(name, core, parts, layer, n_layers, into=None):
    _, hr, C = parts.shape
    r = 2 * hr
    tm = _pick_rows(hr, 4 * C, 256 * 1024, 2 * SUBLANES)
    nh = hr // tm

    def body(c_ref, p_ref, *rest):
        o_ref = rest[-1]
        o_ref[...] = (_f32(p_ref[0]) + _f32(p_ref[1])) + (_f32(p_ref[2]) + _f32(p_ref[3]))

    in_specs = [pl.BlockSpec((4, tm, C), lambda i, cr: (0, i, 0))]
    args = [core, parts]
    aliases = {}
    if into is not None:
        in_specs.append(pl.BlockSpec(memory_space=pl.ANY))
        args.append(into)
        aliases = {2: 0}
    return pl.pallas_call(
        body, name=name,
        grid_spec=pltpu.PrefetchScalarGridSpec(
            num_scalar_prefetch=1, grid=(nh,), in_specs=in_specs,
            out_specs=pl.BlockSpec((tm, C), lambda i, cr: ((layer * r) // tm + cr[0] * nh + i, 0))),
        out_shape=jax.ShapeDtypeStruct((n_layers * r, C), F32), input_output_aliases=aliases,
        compiler_params=_params(dimension_semantics=("arbitrary",)),
    )(*args)


def adamw_full(name, g, w, m, v):
    R, C = w.shape
    tm = _pick_rows(R, C)

    def body(g_ref, w_ref, m_ref, v_ref, d_out, m_out, v_out):
        d, mn, vn = _adamw(g_ref[...], w_ref[...], m_ref[...], v_ref[...])
        d_out[...] = d
        m_out[...] = mn
        v_out[...] = vn

    spec = pl.BlockSpec((tm, C), lambda i: (i, 0))
    return pl.pallas_call(
        body, name=name, grid=(R // tm,), in_specs=[spec] * 4, out_specs=[spec] * 3,
        out_shape=[jax.ShapeDtypeStruct((R, C), F32)] * 3,
        compiler_params=_params(dimension_semantics=("arbitrary",)),
    )(g, w, m, v)


def fold_rows(name, v):
    def body(v_ref, o_ref):
        o_ref[...] = jnp.sum(v_ref[...], axis=0, keepdims=True)

    return pl.pallas_call(body, name=name, out_shape=jax.ShapeDtypeStruct((1, v.shape[1]), F32),
                          compiler_params=_params())(v)


def adamw_small(name, parts, w, m, v):
    W = w.shape[1]

    def body(p_ref, w_ref, m_ref, v_ref, g_out, d_out, m_out, v_out):
        acc = p_ref[0]
        for k in range(1, 8):
            acc = acc + p_ref[k]
        g = jnp.sum(acc, axis=0, keepdims=True)
        d, mn, vn = _adamw(g, w_ref[...], m_ref[...], v_ref[...])
        g_out[...] = g
        d_out[...] = d
        m_out[...] = mn
        v_out[...] = vn

    return pl.pallas_call(
        body, name=name, out_shape=[jax.ShapeDtypeStruct((1, W), F32)] * 4,
        compiler_params=_params(),
    )(parts, w, m, v)


def _pad_lanes(v, width=LANES):
    return jnp.pad(v, ((0, 0), (0, width - v.shape[1])))


def _layer_fwd(cf, x, xb, pb, W, sm, target=None):
    S, D, CD, DI, XBC, F, H, G = cf['S'], cf['D'], cf['CD'], cf['DI'], cf['XBC'], cf['F'], cf['H'], cf['G']
    NM = cf['NM']
    alpha = cf['alpha']
    tm = cf['tm']
    tmx = cf['tmx']
    tn_in = cf['tn_in']
    sv = {}

    ident = lambda acc, ex: ([acc], [])
    proj, = fused_mm("in_proj", [(xb, W['in_main'], 0, False)], [], ident, [(NM, BF16, tn_in, 0)],
                     M=S, tm=tmx, tn=tn_in, nj=NM // tn_in)
    dtraw, = fused_mm("dt_proj", [(xb, W['in_dt'], 0, False)], [], ident, [(LANES, F32, LANES, 0)],
                      M=S, tm=tmx, tn=LANES)

    u, = row_call("glu", lambda a, gt: ([a * _sig(gt)], []),
                  [(proj, 'row', CD, 0, 0), (proj, 'row', CD, 1, 0)], [(CD, F32, CD, 0, 0)], M=S, tm=tm)

    def conv_a_epi(conv, ex):
        cb_, g_, b_ = ex
        ca = conv + cb_
        xhat, _ = _ln_stats(ca)
        la = xhat * g_ + b_
        return [ca, la * _sig(la)], []

    ca, sa = conv_call("conv_a", u, 0, sm['conv_a_w'], cf['KA'], conv_a_epi,
                       [(sm['conv_a_b'], 'vec', CD, 0, 0), (sm['ln_a_g'], 'vec', CD, 0, 0), (sm['ln_a_b'], 'vec', CD, 0, 0)],
                       [(CD, F32, CD, 0, 0), (CD, BF16, CD, 0, 0)], M=S, tm=cf['tmc'], cw=CD, nc=1, reverse=False)
    y_a, = fused_mm("a_out", [(sa, W['a_out'], 0, False)], [], ident, [(D, F32, D, 0)], M=S, tm=tmx, tn=D)

    def conv_x_epi(conv, ex):
        cb = conv + ex[0]
        act = cb * _sig(cb)
        return [cb, act], [], [act]

    def conv_bc_epi(conv, ex):
        cb = conv + ex[0]
        return [cb, cb * _sig(cb)], []

    xoff = (2 * CD + 2 * D + DI) // DI
    cbv_x, xs, xsT = conv_call("conv_b_x", proj, xoff, sm['ssm_conv_w'], cf['KB'], conv_x_epi,
                               [(sm['ssm_conv_b'], 'vec', DI, 0, 0)],
                               [(DI, F32, DI, 0, 0), (DI, F32, DI, 0, 0)], M=S, tm=cf['tmc'], cw=DI, nc=1,
                               reverse=False, t_outs=[(DI, F32, DI, 0, 0)])
    cbv_bc, bc = conv_call("conv_b_bc", proj, xoff + 1, sm['ssm_conv_w'], cf['KB'], conv_bc_epi,
                           [(sm['ssm_conv_b'], 'vec', DI, 1, 0)],
                           [(DI, F32, DI, 0, 0), (DI, F32, DI, 0, 0)], M=S, tm=cf['tmc'], cw=DI, nc=1,
                           reverse=False, w_c0=1)
    y_f, st_f = ssd_fwd("ssd_fwd_f", xsT, bc, dtraw, sm['dtb_f'], sm['alog_f'], S=S, DI=DI, G=G, H=H, rev=False)
    zoff = (2 * CD + 2 * D) // DI
    ysum, st_r, yn = ssd_fwd("ssd_fwd_r", xsT, bc, dtraw, sm['dtb_r'], sm['alog_r'], S=S, DI=DI, G=G, H=H, rev=True,
                             tail=(y_f, (proj, zoff), xs, sm['dskip_full'], sm['ssm_norm_g']))
    goff = (2 * CD) // D

    def merge_epi(acc, ex):
        ga, gb, ya = ex
        return [acc, _sig(ga) * ya + _sig(gb) * acc], []

    y_b, merged = fused_mm("b_out", [(yn, W['b_out'], 0, False)],
                           [(proj, 'row', D, goff), (proj, 'row', D, goff + 1), (y_a, 'row', D, 0)],
                           merge_epi, [(D, F32, D, 0), (D, BF16, D, 0)], M=S, tm=tm, tn=D)

    def mix_epi(acc, ex):
        xin, g_, b_ = ex
        r1 = alpha * xin + acc
        xhat, _ = _ln_stats(r1)
        return [r1, xhat * g_ + b_], []

    r1, hb = fused_mm("o_mix", [(merged, W['o'], 0, False)],
                      [(x, 'row', D, 0), (sm['ln1_g'], 'vec', D, 0), (sm['ln1_b'], 'vec', D, 0)],
                      mix_epi, [(D, F32, D, 0), (D, BF16, D, 0)], M=S, tm=tm, tn=D)

    tnf = cf['tnf']

    g32, g_ = fused_mm("ffn_gate", [(hb, W['gate_up'], 0, False)], [], lambda acc, ex: ([acc, acc], []),
                       [(F, F32, tnf, 0), (F, BF16, tnf, 0)], M=S, tm=tmx, tn=tnf, nj=F // tnf)
    u_, f = fused_mm("ffn_up", [(hb, W['gate_up'], F // tnf, False)], [(g32, 'row', tnf, 0)],
                     lambda acc, ex: ([acc, ex[0] * _sig(ex[0]) * acc], []),
                     [(F, BF16, tnf, 0), (F, BF16, tnf, 0)], M=S, tm=tmx, tn=tnf, nj=F // tnf)

    def down_epi(acc, ex):
        r1_, g1, b1, g2, b2 = ex
        xh1, _ = _ln_stats(r1_)
        r2 = alpha * (xh1 * g1 + b1) + acc
        xh2, _ = _ln_stats(r2)
        return [r2, xh2 * g2 + b2], []

    r2, h2b = fused_mm("ffn_down", [(f, W['down'], 0, False)],
                       [(r1, 'row', D, 0), (sm['ln1_g'], 'vec', D, 0), (sm['ln1_b'], 'vec', D, 0),
                        (sm['ln2_g'], 'vec', D, 0), (sm['ln2_b'], 'vec', D, 0)],
                       down_epi, [(D, F32, D, 0), (D, BF16, D, 0)], M=S, tm=tm, tn=D)

    pe, = fused_mm("ple_proj", [(pb, W['ple'], 0, False)], [], ident, [(D, F32, D, 0)], M=S, tm=tmx, tn=D)

    def ple_out(acc, r2_, g2, b2, pe_, pg):
        xh2, _ = _ln_stats(r2_)
        h2 = xh2 * g2 + b2
        e = pe_ * lax.rsqrt(jnp.mean(pe_ * pe_, axis=-1, keepdims=True) + RMS_EPS) * pg
        return h2 + e * _sig(acc)

    ple_extras = [(r2, 'row', D, 0), (sm['ln2_g'], 'vec', D, 0), (sm['ln2_b'], 'vec', D, 0),
                  (pe, 'row', D, 0), (sm['ple_norm_g'], 'vec', D, 0)]
    if target is None:
        def ple_epi(acc, ex):
            xn_ = ple_out(acc, *ex)
            return [acc, xn_, xn_], []

        t_, xn, xnb = fused_mm("ple_gate", [(h2b, W['ple_gate'], 0, False)], ple_extras,
                               ple_epi, [(D, F32, D, 0), (D, F32, D, 0), (D, BF16, D, 0)], M=S, tm=tm, tn=D)
    else:
        def ple_loss_epi(acc, ex):
            err = ple_out(acc, *ex[:5]) - ex[5]
            dx_ = err * (1.0 / D)
            dtg, dpe, dpg = _ple_bwd(dx_, acc, ex[3], ex[4])
            return [acc, dx_, dtg, dpe], [dpg, err * err]

        t_, dxn, dtg, dpe, dpg, lsq = fused_mm(
            "ple_gate_loss", [(h2b, W['ple_gate'], 0, False)], ple_extras + [(target, 'row', D, 0)],
            ple_loss_epi, [(D, F32, D, 0), (D, F32, D, 0), (D, BF16, D, 0), (D, BF16, D, 0)],
            [(D, D, 0), (D, D, 0)], M=S, tm=tm, tn=D)
        xn = xnb = None
        sv['head'] = (dxn, dtg, dpe, dpg, lsq)
    sv.update(x=x, xb=xb, pb=pb, proj=proj, dtraw=dtraw, u=u, ca=ca, sa=sa, y_a=y_a, cbv_x=cbv_x, cbv_bc=cbv_bc,
              xs=xs, xsT=xsT, bc=bc,
              ysum=ysum, st_f=st_f, st_r=st_r, yn=yn, y_b=y_b, merged=merged, r1=r1, hb=hb,
              g_=g_, u_=u_, f=f, r2=r2, h2b=h2b, t_=t_, pe=pe)
    return xn, xnb, sv


def _ple_bwd(dx_, t, pe_, pg):
    s = _sig(t)
    rinv = lax.rsqrt(jnp.mean(pe_ * pe_, axis=-1, keepdims=True) + RMS_EPS)
    pn = pe_ * rinv
    e = pn * pg
    dtg = dx_ * e * (s * (1.0 - s))
    de = dx_ * s
    qv = de * pg
    dpe = rinv * (qv - pn * jnp.mean(qv * pn, axis=-1, keepdims=True))
    return dtg, dpe, de * pn


def _layer_bwd(cf, sv, W, sm, dxn=None):
    S, D, CD, DI, XBC, F, H, G = cf['S'], cf['D'], cf['CD'], cf['DI'], cf['XBC'], cf['F'], cf['H'], cf['G']
    NM = cf['NM']
    alpha = cf['alpha']
    tm = cf['tm']
    gw = cf['GW']
    out = {}

    if dxn is None:
        dxn, dtg, dpe, dpg, out['loss_sq'] = sv['head']
    else:
        def mid(dx_, t, pe_, pg):
            dtg, dpe, dpg = _ple_bwd(dx_, t, pe_, pg)
            return [dtg, dpe], [dpg]

        (dtg, dpe, dpg) = row_call(
            "ple_bwd", mid,
            [(dxn, 'row', D, 0, 0), (sv['t_'], 'row', D, 0, 0), (sv['pe'], 'row', D, 0, 0),
             (sm['ple_norm_g'], 'vec', D, 0, 0)],
            [(D, BF16, D, 0, 0), (D, BF16, D, 0, 0)], [(D, D, 0, 0)], M=S, tm=tm)
    out['ple_norm_g'] = dpg

    def ln_bwd_epi(scale):
        def epi(acc, ex):
            res, r_, g_ = ex
            dh = scale * res + acc
            xhat, rstd = _ln_stats(r_)
            dr = _ln_bwd(dh, xhat, rstd, g_)
            return [dr, dr], [dh * xhat, dh]
        return epi

    dr2, dr2b, dg2, db2 = fused_mm(
        "dh2", [(dtg, W['ple_gate_T'], 0, False)],
        [(dxn, 'row', D, 0), (sv['r2'], 'row', D, 0), (sm['ln2_g'], 'vec', D, 0)],
        ln_bwd_epi(1.0), [(D, F32, D, 0), (D, BF16, D, 0)], [(D, D, 0), (D, D, 0)], M=S, tm=tm, tn=D)
    out['ln2_g'], out['ln2_b'] = dg2, db2

    tnf = cf['tnf']

    def dswiglu_epi(acc, ex):
        gg, uu = ex
        s = _sig(gg)
        return [acc * uu * _dsilu(gg, s), acc * (gg * s)], []

    dg_b, du_b = fused_mm(
        "d_down", [(dr2b, W['down_T'], 0, False)],
        [(sv['g_'], 'row', tnf, 0), (sv['u_'], 'row', tnf, 0)], dswiglu_epi,
        [(F, BF16, tnf, 0), (F, BF16, tnf, 0)], M=S, tm=tm, tn=tnf, nj=F // tnf)

    dr1, dr1b, dg1, db1 = fused_mm(
        "dh1", [(dg_b, W['gate_T'], 0, True), (du_b, W['up_T'], 0, True)],
        [(dr2, 'row', D, 0), (sv['r1'], 'row', D, 0), (sm['ln1_g'], 'vec', D, 0)],
        ln_bwd_epi(alpha), [(D, F32, D, 0), (D, BF16, D, 0)], [(D, D, 0), (D, D, 0)],
        M=S, tm=tm, tn=D, nk=cf['nk_f'])
    out['ln1_g'], out['ln1_b'] = dg1, db1

    goff = (2 * CD) // D

    def dmerge_epi(acc, ex):
        ga, gb, ya, yb = ex
        sa_, sb_ = _sig(ga), _sig(gb)
        dga = acc * ya * (sa_ * (1.0 - sa_))
        dgb = acc * yb * (sb_ * (1.0 - sb_))
        return [jnp.concatenate([dga, dgb], axis=1), acc * sa_, acc * sb_], []

    dproj, dya_b, dyb_b = fused_mm(
        "d_merge", [(dr1b, W['o_T'], 0, False)],
        [(sv['proj'], 'row', D, goff), (sv['proj'], 'row', D, goff + 1), (sv['y_a'], 'row', D, 0), (sv['y_b'], 'row', D, 0)],
        dmerge_epi, [(NM, BF16, 2 * D, (2 * CD) // (2 * D)), (D, BF16, D, 0), (D, BF16, D, 0)], M=S, tm=tm, tn=D)

    def dsa_epi(acc, ex):
        ca_, g_, b_ = ex
        xhat, rstd = _ln_stats(ca_)
        la = xhat * g_ + b_
        dla = acc * _dsilu(la, _sig(la))
        dca = _ln_bwd(dla, xhat, rstd, g_)
        return [dca], [dla * xhat, dla, dca]

    dca, dlag, dlab, dcab = fused_mm(
        "d_a_out", [(dya_b, W['a_out_T'], 0, False)],
        [(sv['ca'], 'row', CD, 0), (sm['ln_a_g'], 'vec', CD, 0), (sm['ln_a_b'], 'vec', CD, 0)],
        dsa_epi, [(CD, F32, CD, 0)], [(CD, CD, 0), (CD, CD, 0), (CD, CD, 0)], M=S, tm=tm, tn=D)
    out['ln_a_g'], out['ln_a_b'], out['conv_a_b'] = dlag, dlab, dcab

    def dglu_epi(du, ex):
        a, gt = ex
        s = _sig(gt)
        return [jnp.concatenate([du * s, du * a * (s * (1.0 - s))], axis=1)], []

    dproj, dwa = conv_call(
        "d_conv_a", dca, 0, sm['conv_a_w'], cf['KA'], dglu_epi,
        [(sv['proj'], 'row', CD, 0, 0), (sv['proj'], 'row', CD, 1, 0)],
        [(NM, BF16, 2 * CD, 0, 0)], M=S, tm=cf['tmc'], cw=CD, nc=1, reverse=True, xin=(sv['u'], 0),
        passthrough=(dproj, 0))
    out['conv_a_w'] = dwa

    zoff = (2 * CD + 2 * D) // DI

    def dgate_norm_epi(acc, ex):
        ysum_, xs, z, dsk, ng = ex
        y = ysum_ + xs * dsk
        sz = _sig(z)
        siluz = z * sz
        yz = y * siluz
        dyzs, yhats = [], []
        for g in range(G):
            t = yz[:, g * gw:(g + 1) * gw]
            rinv = lax.rsqrt(jnp.mean(t * t, axis=-1, keepdims=True) + RMS_EPS)
            yh = t * rinv
            qv = acc[:, g * gw:(g + 1) * gw] * ng[:, g * gw:(g + 1) * gw]
            dyzs.append(rinv * (qv - yh * jnp.mean(qv * yh, axis=-1, keepdims=True)))
            yhats.append(yh)
        dyz = jnp.concatenate(dyzs, axis=1)
        yhat = jnp.concatenate(yhats, axis=1)
        dy = dyz * siluz
        dz = dyz * y * _dsilu(z, sz)
        return [dz], [acc * yhat, dy * xs], [dy]

    tmr = cf['tmr']
    dproj, dng, ddsk, dyT = fused_mm(
        "d_b_out", [(dyb_b, W['b_out_T'], 0, False)],
        [(sv['ysum'], 'row', DI, 0), (sv['xs'], 'row', DI, 0), (sv['proj'], 'row', DI, zoff),
         (sm['dskip_full'], 'vec', DI, 0), (sm['ssm_norm_g'], 'vec', DI, 0)],
        dgate_norm_epi, [(NM, BF16, DI, zoff)], [(DI, DI, 0), (DI, DI, 0)],
        M=S, tm=tmr, tn=DI, passthrough=(dproj, 0), t_outs=[(DI, F32, DI, 0)])
    out['ssm_norm_g'], out['dskip_full'] = dng, ddsk

    dxbc_f, ddt_f, dA_f = ssd_bwd("ssd_bwd_f", sv['xsT'], sv['bc'], sv['dtraw'], dyT, sv['st_f'], sm['dtb_f'],
                                  sm['alog_f'], S=S, DI=DI, G=G, H=H, rev=False)
    dcb, ddt_r, dA_r, dcbb = ssd_bwd("ssd_bwd_r", sv['xsT'], sv['bc'], sv['dtraw'], dyT, sv['st_r'], sm['dtb_r'],
                                     sm['alog_r'], S=S, DI=DI, G=G, H=H, rev=True,
                                     tail=(dxbc_f, sv['cbv_x'], sv['cbv_bc'], sm['dskipT']))
    out['dA_f'], out['dA_r'] = dA_f, dA_r
    out['ssm_conv_b'] = dcbb

    xoff = (2 * CD + 2 * D + DI) // DI
    dproj, dwb = conv_call(
        "d_conv_b", dcb, 0, sm['ssm_conv_w'], cf['KB'], lambda conv, ex: ([conv], []), [],
        [(NM, BF16, DI, xoff, 1)], M=S, tm=cf['tmc'], cw=DI, nc=XBC // DI, reverse=True, xin=(sv['proj'], xoff),
        passthrough=(dproj, 0))
    out['ssm_conv_w'] = dwb

    ddtb, ddt_bias = row_call("d_dt", lambda a, b: ([a + b], [a + b]),
                              [(ddt_f, 'row', LANES, 0, 0), (ddt_r, 'row', LANES, 0, 0)],
                              [(LANES, BF16, LANES, 0, 0)], [(LANES, LANES, 0, 0)], M=S, tm=tm)
    out['dt_bias'] = ddt_bias

    dx, = fused_mm("d_x", [(dproj, W['in_main_T'], 0, True), (ddtb, W['in_dt_T'], 0, False)],
                   [(dr1, 'row', D, 0)], lambda acc, ex: ([alpha * ex[0] + acc], []),
                   [(D, F32, D, 0)], M=S, tm=cf['tmx'], tn=D, nk=cf['nk_in'])

    tmw = cf['tmw']
    xb = sv['xb']
    out['w_in'] = jnp.concatenate(
        [mm_tn("dw_in", xb, dproj, tm=tmw, tk=D, tn=cf['tn_in']),
         mm_tn("dw_dt", xb, ddtb, tm=tmw, tk=D, tn=LANES)[:, :2 * H]], axis=1)
    out['w_a_out'] = mm_tn("dw_a_out", sv['sa'], dya_b, tm=tmw, tk=CD, tn=D)
    out['w_b_out'] = mm_tn("dw_b_out", sv['yn'], dyb_b, tm=tmw, tk=DI // 2, tn=D)
    out['w_o'] = mm_tn("dw_o", sv['merged'], dr1b, tm=tmw, tk=D, tn=D)
    out['w_gate_up'] = jnp.concatenate(
        [mm_tn("dw_gate", sv['hb'], dg_b, tm=tmw, tk=D, tn=tnf),
         mm_tn("dw_up", sv['hb'], du_b, tm=tmw, tk=D, tn=tnf)], axis=1)
    out['w_down'] = mm_tn("dw_down", sv['f'], dr2b, tm=tmw, tk=tnf, tn=D)
    out['w_ple'] = mm_tn("dw_ple", sv['pb'], dpe, tm=tmw, tk=sv['pb'].shape[1], tn=D)
    out['w_ple_gate'] = mm_tn("dw_ple_gate", sv['h2b'], dtg, tm=tmw, tk=D, tn=D)
    return dx, out


_WEIGHTS = ['w_in', 'conv_a_w', 'conv_a_b', 'ln_a_g', 'ln_a_b', 'w_a_out', 'ssm_conv_w', 'ssm_conv_b', 'a_log',
            'dt_bias', 'd_skip', 'ssm_norm_g', 'w_b_out', 'w_o', 'ln1_g', 'ln1_b', 'w_gate_up', 'w_down', 'ln2_g',
            'ln2_b', 'w_ple', 'ple_norm_g', 'w_ple_gate']
_COL_SHARDED = ['w_in', 'conv_a_w', 'ssm_conv_w', 'w_gate_up', 'w_ple']
_ROW_SHARDED = ['w_a_out', 'w_b_out', 'w_o', 'w_down', 'w_ple_gate']
_BIG = _COL_SHARDED + _ROW_SHARDED
_SMALL = [n for n in _WEIGHTS if n not in _BIG]
_CONV = ['conv_a_w', 'ssm_conv_w']


def _ceil_to(n, k):
    return -(-n // k) * k


def kernel(x, p, w_in, conv_a_w, conv_a_b, ln_a_g, ln_a_b, w_a_out, ssm_conv_w, ssm_conv_b, a_log, dt_bias, d_skip, ssm_norm_g, w_b_out, w_o, ln1_g, ln1_b, w_gate_up, w_down, ln2_g, ln2_b, w_ple, ple_norm_g, w_ple_gate, loss_target, m_w_in, m_conv_a_w, m_conv_a_b, m_ln_a_g, m_ln_a_b, m_w_a_out, m_ssm_conv_w, m_ssm_conv_b, m_a_log, m_dt_bias, m_d_skip, m_ssm_norm_g, m_w_b_out, m_w_o, m_ln1_g, m_ln1_b, m_w_gate_up, m_w_down, m_ln2_g, m_ln2_b, m_w_ple, m_ple_norm_g, m_w_ple_gate, v_w_in, v_conv_a_w, v_conv_a_b, v_ln_a_g, v_ln_a_b, v_w_a_out, v_ssm_conv_w, v_ssm_conv_b, v_a_log, v_dt_bias, v_d_skip, v_ssm_norm_g, v_w_b_out, v_w_o, v_ln1_g, v_ln1_b, v_w_gate_up, v_w_down, v_ln2_g, v_ln2_b, v_w_ple, v_ple_norm_g, v_w_ple_gate):
    wt = dict(w_in=w_in, conv_a_w=conv_a_w, conv_a_b=conv_a_b, ln_a_g=ln_a_g, ln_a_b=ln_a_b, w_a_out=w_a_out,
              ssm_conv_w=ssm_conv_w, ssm_conv_b=ssm_conv_b, a_log=a_log, dt_bias=dt_bias, d_skip=d_skip,
              ssm_norm_g=ssm_norm_g, w_b_out=w_b_out, w_o=w_o, ln1_g=ln1_g, ln1_b=ln1_b, w_gate_up=w_gate_up,
              w_down=w_down, ln2_g=ln2_g, ln2_b=ln2_b, w_ple=w_ple, ple_norm_g=ple_norm_g, w_ple_gate=w_ple_gate)
    mo = dict(w_in=m_w_in, conv_a_w=m_conv_a_w, conv_a_b=m_conv_a_b, ln_a_g=m_ln_a_g, ln_a_b=m_ln_a_b,
              w_a_out=m_w_a_out, ssm_conv_w=m_ssm_conv_w, ssm_conv_b=m_ssm_conv_b, a_log=m_a_log,
              dt_bias=m_dt_bias, d_skip=m_d_skip, ssm_norm_g=m_ssm_norm_g, w_b_out=m_w_b_out, w_o=m_w_o,
              ln1_g=m_ln1_g, ln1_b=m_ln1_b, w_gate_up=m_w_gate_up, w_down=m_w_down, ln2_g=m_ln2_g, ln2_b=m_ln2_b,
              w_ple=m_w_ple, ple_norm_g=m_ple_norm_g, w_ple_gate=m_w_ple_gate)
    vo = dict(w_in=v_w_in, conv_a_w=v_conv_a_w, conv_a_b=v_conv_a_b, ln_a_g=v_ln_a_g, ln_a_b=v_ln_a_b,
              w_a_out=v_w_a_out, ssm_conv_w=v_ssm_conv_w, ssm_conv_b=v_ssm_conv_b, a_log=v_a_log,
              dt_bias=v_dt_bias, d_skip=v_d_skip, ssm_norm_g=v_ssm_norm_g, w_b_out=v_w_b_out, w_o=v_w_o,
              ln1_g=v_ln1_g, ln1_b=v_ln1_b, w_gate_up=v_w_gate_up, w_down=v_w_down, ln2_g=v_ln2_g, ln2_b=v_ln2_b,
              w_ple=v_w_ple, ple_norm_g=v_ple_norm_g, w_ple_gate=v_w_ple_gate)

    L = w_in.shape[0]
    S, D = x.shape[1], x.shape[2]
    CD = conv_a_b.shape[1]
    DI = ssm_norm_g.shape[1]
    XBC = ssm_conv_b.shape[1]
    H = d_skip.shape[1]
    G = (XBC - DI) // (2 * D_STATE)
    F = w_down.shape[1] * 4
    N_IN = w_in.shape[2] * 4
    NM = N_IN - 2 * H
    KA, KB = conv_a_w.shape[1], ssm_conv_w.shape[1]
    assert DI == H * HEAD_DIM and CD == D and DI == 2 * D and XBC == 2 * DI and NM == 2 * CD + 2 * D + DI + XBC
    assert 2 * H <= LANES and S % CHUNK == 0
    tnf = F // 2
    cf = dict(S=S, D=D, CD=CD, DI=DI, XBC=XBC, F=F, H=H, G=G, NM=NM, KA=KA, KB=KB, GW=(H // G) * HEAD_DIM,
              alpha=float((2 * L) ** 0.25), tm=min(512, S), tmx=min(1024, S), tmc=min(256, S), tmr=min(256, S), tmw=min(1024, S),
              tn_in=D, tnf=tnf, nk_f=1, nk_in=NM // DI)

    core = lax.axis_index("c").astype(jnp.int32).reshape(1)
    split_names = [n for n in _BIG if n not in _CONV]

    def layer_weights(l, got):
        full = {}
        for n, g in zip(split_names + _CONV, got):
            if n in _COL_SHARDED:
                full[n] = g.transpose(1, 0, 2).reshape(g.shape[1], 4 * g.shape[2])
            else:
                full[n] = g.reshape(4 * g.shape[1], g.shape[2])
        win = full['w_in']
        in_main = win[:, :NM]
        in_dt = _pad_lanes(win[:, NM:])
        gu = full['w_gate_up']
        W = dict(in_main=in_main, in_dt=in_dt, in_main_T=in_main.T, in_dt_T=in_dt.T,
                 a_out=full['w_a_out'], a_out_T=full['w_a_out'].T,
                 b_out=full['w_b_out'], b_out_T=full['w_b_out'].T,
                 o=full['w_o'], o_T=full['w_o'].T, gate_up=gu, gate_T=gu[:, :F].T, up_T=gu[:, F:].T,
                 down=full['w_down'], down_T=full['w_down'].T, ple=full['w_ple'],
                 ple_gate=full['w_ple_gate'], ple_gate_T=full['w_ple_gate'].T)
        row = lambda v: v.reshape(1, -1)
        head_table = lambda v: jnp.broadcast_to(jnp.pad(v, (0, LANES - H))[:, None], (LANES, LANES))
        sm = dict(conv_a_w=jnp.pad(full['conv_a_w'], ((0, _ceil_to(KA, SUBLANES) - KA), (0, 0))),
                  ssm_conv_w=jnp.pad(full['ssm_conv_w'], ((0, _ceil_to(KB, SUBLANES) - KB), (0, 0))),
                  conv_a_b=row(conv_a_b[l]), ln_a_g=row(ln_a_g[l]), ln_a_b=row(ln_a_b[l]),
                  ssm_conv_b=row(ssm_conv_b[l]), ssm_norm_g=row(ssm_norm_g[l]),
                  ln1_g=row(ln1_g[l]), ln1_b=row(ln1_b[l]), ln2_g=row(ln2_g[l]), ln2_b=row(ln2_b[l]),
                  ple_norm_g=row(ple_norm_g[l]),
                  dtb_f=head_table(dt_bias[l, 0]), dtb_r=head_table(dt_bias[l, 1]),
                  alog_f=head_table(a_log[l, 0]), alog_r=head_table(a_log[l, 1]),
                  dskip_full=row(jnp.repeat(d_skip[l], HEAD_DIM)),
                  dskipT=jnp.broadcast_to(jnp.repeat(d_skip[l], HEAD_DIM)[:, None], (DI, LANES)))
        return W, sm

    def blocks(n, gl):
        g = gl[n]
        if n == 'conv_a_w':
            g = g.sum(axis=1)[:KA]
        elif n == 'ssm_conv_w':
            g = g.sum(axis=1)[:KB]
        if n in _COL_SHARDED:
            return g.reshape(g.shape[0], 4, g.shape[1] // 4).transpose(1, 0, 2)
        return g.reshape(4, g.shape[0] // 4, g.shape[1])

    def core_sums(gl):
        mine = [blocks(n, gl) for n in split_names]
        theirs = core_send_half("core_send_half", mine)
        return [core_sum("core_sum_" + n, core, b, t) for n, b, t in zip(split_names, mine, theirs)]

    def chip_sums(l, parts, acc):
        sums = [chip_sum_into("chip_sum_" + n, core, pr, l, L, into=acc.get(n)) for n, pr in zip(split_names, parts)]
        return dict(zip(split_names, core_fill("core_fill", sums, l, L)))

    def shards(l):
        return [wt[n][l].astype(BF16) for n in split_names]

    lw = [None] * L
    pending = None
    for l in range(L):
        if l < L - 1 or L == 1:
            lw[l] = layer_weights(l, gather_layer("gather_weights", shards(l), [wt[n][l] for n in _CONV]))
    xl = x[0]
    if L > 1:
        sh = shards(L - 1)
        send, recv, sh, lands, token = chip_legs_start(
            "gather_start", 'gather', sh, [lax.empty((4,) + a.shape, a.dtype) for a in sh])
        pending = (send, recv, sh, lands)
        xlb = (xl + token[0, 0]).astype(BF16)
    else:
        xlb = xl.astype(BF16)
    saved = []
    for l in range(L):
        if l == L - 1 and pending is not None:
            send, recv, sh, lands = pending
            landed = chip_legs_wait("gather_wait", 'gather', send, recv, sh, lands, xl)
            conv_got = chip_exchange("gather_conv", [[wt[n][l]] for n in _CONV], gather=True)
            lw[l] = layer_weights(l, list(gather_finish("gather_finish", sh, landed)) + list(conv_got))
        xl, xlb, sv = _layer_fwd(cf, xl, xlb, p[l, 0].astype(BF16), lw[l][0], lw[l][1],
                                 target=loss_target[0] if l == L - 1 else None)
        saved.append(sv)
    grads = [None] * L
    dxl = None
    gsum = {}
    pending = None
    for l in reversed(range(L)):
        sm_l = lw[l][1]
        if pending is not None:
            sm_l = dict(sm_l, ple_norm_g=sm_l['ple_norm_g'] + pending[4][0, 0])
        if l == L - 1:
            dxl, grads[l] = _layer_bwd(cf, saved[l], lw[l][0], sm_l)
        else:
            dxl, grads[l] = _layer_bwd(cf, saved[l], lw[l][0], sm_l, dxn=dxl)
        both = core_sums(grads[l])
        if l == L - 1 and L > 1:
            send, recv, both, lands, token = chip_legs_start(
                "scatter_start", 'scatter', both, [lax.empty(a.shape, a.dtype) for a in both])
            pending = (send, recv, both, lands, token)
            continue
        if pending is not None:
            send, recv, sent, lands, _ = pending
            landed = chip_legs_wait("scatter_wait", 'scatter', send, recv, sent, lands, dxl)
            gsum = chip_sums(L - 1, place_own(sent, landed), gsum)
            pending = None
        parts = chip_exchange("scatter_grads", [[t] for t in both], gather=False)
        gsum = chip_sums(l, [pr.reshape(4, pr.shape[2], pr.shape[3]) for pr in parts], gsum)
    loss = lax.psum(0.5 / D * jnp.sum(grads[L - 1]['loss_sq']), ("x", "y", "c"))
    grad_x = dxl[None]

    res = {}
    for n in split_names:
        shp = wt[n].shape
        flat = lambda a: a.reshape(shp[0] * shp[1], shp[2])
        outs = adamw_full("adamw_" + n, gsum[n], flat(wt[n]), flat(mo[n]), flat(vo[n]))
        res[n] = [o.reshape(shp) for o in [gsum[n]] + list(outs)]
    parts = chip_exchange("scatter_conv", [[blocks(n, grads[l]) for l in range(L)] for n in _CONV], gather=False)
    chip_sums = [sum_chips("chip_sum_" + n, pr.reshape(4, L * pr.shape[2], pr.shape[3])) for n, pr in zip(_CONV, parts)]
    sib_sums = sibling_swap("core_swap", chip_sums)
    for n, mine, sib in zip(_CONV, chip_sums, sib_sums):
        shp = wt[n].shape
        flat = lambda a: a.reshape(shp[0] * shp[1], shp[2])
        outs = adamw_shard("adamw_" + n, mine, sib, flat(wt[n]), flat(mo[n]), flat(vo[n]))
        res[n] = [o.reshape(shp) for o in outs]

    def small_pieces(l):
        gl = grads[l]
        A = -jnp.exp(a_log[l])
        d = dict(gl)
        d_alog = jnp.concatenate([gl['dA_f'].sum(axis=1)[:H] * A[0], gl['dA_r'].sum(axis=1)[:H] * A[1]])
        d['a_log'] = jnp.pad(d_alog[None], ((0, SUBLANES - 1), (0, 0)))
        d['dt_bias'] = gl['dt_bias'][:, :2 * H]
        d['d_skip'] = gl['dskip_full'].reshape(SUBLANES, H, HEAD_DIM).sum(axis=-1)
        return [_pad_lanes(d[n], _ceil_to(d[n].shape[1], LANES)) for n in _SMALL]

    widths = [_ceil_to(math.prod(wt[n].shape[1:]), LANES) for n in _SMALL]
    packed = jnp.concatenate([pc for l in range(L) for pc in small_pieces(l)], axis=1)
    gathered = all8_gather("gather_small", fold_rows("fold_small", packed))

    def pack_params(src):
        return jnp.concatenate([_pad_lanes(src[n][l].reshape(1, -1), wd) for l in range(L) for n, wd in zip(_SMALL, widths)],
                               axis=1)

    small_out = adamw_small("adamw_small", gathered, pack_params(wt), pack_params(mo), pack_params(vo))
    off = 0
    per = {n: [[] for _ in range(4)] for n in _SMALL}
    for l in range(L):
        for n, wd in zip(_SMALL, widths):
            size = math.prod(wt[n].shape[1:])
            for k in range(4):
                per[n][k].append(small_out[k][0, off:off + size].reshape(wt[n].shape[1:]))
            off += wd
    for n in _SMALL:
        res[n] = [jnp.stack(per[n][k]) for k in range(4)]

    return (loss, grad_x, *[res[n][0] for n in _WEIGHTS], *[res[n][1] for n in _WEIGHTS],
            *[res[n][2] for n in _WEIGHTS], *[res[n][3] for n in _WEIGHTS])
```

```python
import math

import jax
import jax.numpy as jnp
from jax import lax
from jax.experimental import pallas as pl
from jax.experimental.pallas import tpu as pltpu

F32 = jnp.float32
BF16 = jnp.bfloat16

VMEM_LIMIT_BYTES = 56 * 1024 * 1024
LANES = 128
SUBLANES = 8

CHUNK = 128
D_STATE = 128
HEAD_DIM = 64
LN_EPS = 1e-5
RMS_EPS = 1e-6
ADAM_LR = 0.001
ADAM_B1 = 0.9
ADAM_B2 = 0.999
ADAM_EPS = 1e-08
ADAM_WD = 0.01
ADAM_STEP = 10
HALO = 16
MESH = pl.DeviceIdType.MESH


def _params(**kw):
    return pltpu.CompilerParams(vmem_limit_bytes=VMEM_LIMIT_BYTES, **kw)


def _sig(x):
    return jax.nn.sigmoid(x)


def _dsilu(x, s):
    return s * (1.0 + x * (1.0 - s))


def _ln_stats(r):
    mu = jnp.mean(r, axis=-1, keepdims=True)
    xc = r - mu
    var = jnp.mean(xc * xc, axis=-1, keepdims=True)
    rstd = lax.rsqrt(var + LN_EPS)
    return xc * rstd, rstd


def _ln_bwd(dy, xhat, rstd, g):
    dxh = dy * g
    m1 = jnp.mean(dxh, axis=-1, keepdims=True)
    m2 = jnp.mean(dxh * xhat, axis=-1, keepdims=True)
    return rstd * (dxh - m1 - xhat * m2)


def _f32(v):
    return v if v.dtype == F32 else v.astype(F32)


def _rows8(v):
    tm, w = v.shape
    return v.reshape(tm // SUBLANES, SUBLANES, w).sum(axis=0)


def fused_mm(name, prods, extras, epi, row_outs, col_outs=(), *, M, tm, tn, nj=1, nk=1,
             passthrough=None, t_outs=()):
    np_ = len(prods)
    ne = len(extras)
    nro = len(row_outs)
    nco = len(col_outs)
    use_acc = nk > 1

    def body(*refs):
        a_refs = [refs[2 * p] for p in range(np_)]
        w_refs = [refs[2 * p + 1] for p in range(np_)]
        pos = 2 * np_
        e_refs = refs[pos:pos + ne]
        pos += ne
        if passthrough is not None:
            pos += 1
        ro_refs = refs[pos:pos + nro]
        pos += nro
        co_refs = refs[pos:pos + nco]
        pos += nco
        to_refs = refs[pos:pos + len(t_outs)]
        pos += len(t_outs)
        acc_ref = refs[pos] if use_acc else None
        i = pl.program_id(1)
        k = pl.program_id(2)

        def prod(p):
            a = a_refs[p][...]
            if a.dtype != BF16:
                a = a.astype(BF16)
            return jnp.dot(a, w_refs[p][...], preferred_element_type=F32)

        def finish(acc):
            res = epi(acc, [_f32(r[...]) for r in e_refs])
            rows, cols = res[0], res[1]
            for v, o in zip(rows, ro_refs):
                o[...] = v.astype(o.dtype)
            for v, o in zip(res[2] if len(res) > 2 else (), to_refs):
                o[...] = v.T.astype(o.dtype)
            for v, o in zip(cols, co_refs):
                v8 = _rows8(v)

                @pl.when(i == 0)
                def _():
                    o[...] = v8

                @pl.when(i > 0)
                def _():
                    o[...] += v8

        if not use_acc:
            acc = prod(0)
            for p in range(1, np_):
                acc = acc + prod(p)
            finish(acc)
        else:
            @pl.when(k == 0)
            def _():
                acc = None
                for p in range(np_):
                    acc = prod(p) if acc is None else acc + prod(p)
                acc_ref[...] = acc

            @pl.when(k > 0)
            def _():
                acc = None
                for p in range(np_):
                    if prods[p][3]:
                        acc = prod(p) if acc is None else acc + prod(p)
                acc_ref[...] += acc

            @pl.when(k == nk - 1)
            def _():
                finish(acc_ref[...])

    in_specs = []
    args = []
    for a, w, joff, ksplit in prods:
        K = a.shape[1]
        if ksplit:
            tk = K // nk
            in_specs.append(pl.BlockSpec((tm, tk), lambda j, i, k: (i, k)))
            in_specs.append(pl.BlockSpec((tk, tn), lambda j, i, k, joff=joff: (k, j + joff)))
        else:
            in_specs.append(pl.BlockSpec((tm, K), lambda j, i, k: (i, 0)))
            in_specs.append(pl.BlockSpec((K, tn), lambda j, i, k, joff=joff: (0, j + joff)))
        args += [a, w]
    for arr, kind, width, c0 in extras:
        if kind == 'row':
            in_specs.append(pl.BlockSpec((tm, width), lambda j, i, k, c0=c0: (i, c0 + j)))
        else:
            in_specs.append(pl.BlockSpec((arr.shape[0], width), lambda j, i, k, c0=c0: (0, c0 + j)))
        args.append(arr)
    aliases = {}
    if passthrough is not None:
        arr, oidx = passthrough
        in_specs.append(pl.BlockSpec(memory_space=pl.ANY))
        aliases = {len(args): oidx}
        args.append(arr)
    out_shape = []
    out_specs = []
    for n_total, dtype, width, c0 in row_outs:
        out_shape.append(jax.ShapeDtypeStruct((M, n_total), dtype))
        out_specs.append(pl.BlockSpec((tm, width), lambda j, i, k, c0=c0: (i, c0 + j)))
    for n_total, width, c0 in col_outs:
        out_shape.append(jax.ShapeDtypeStruct((SUBLANES, n_total), F32))
        out_specs.append(pl.BlockSpec((SUBLANES, width), lambda j, i, k, c0=c0: (0, c0 + j)))
    for n_total, dtype, width, c0 in t_outs:
        out_shape.append(jax.ShapeDtypeStruct((n_total, M), dtype))
        out_specs.append(pl.BlockSpec((width, tm), lambda j, i, k, c0=c0: (c0 + j, i)))
    scratch = [pltpu.VMEM((tm, tn), F32)] if use_acc else []
    return pl.pallas_call(
        body, name=name, grid=(nj, M // tm, nk), in_specs=in_specs, out_specs=out_specs,
        out_shape=out_shape, scratch_shapes=scratch, input_output_aliases=aliases,
        compiler_params=_params(dimension_semantics=("arbitrary", "arbitrary", "arbitrary")),
    )(*args)


def mm_tn(name, a, b, *, tm, tk, tn):
    M, K = a.shape
    N = b.shape[1]

    def body(a_ref, b_ref, o_ref):
        m = pl.program_id(2)
        p = lax.dot_general(a_ref[...], b_ref[...], (((0,), (0,)), ((), ())),
                            preferred_element_type=F32)

        @pl.when(m == 0)
        def _():
            o_ref[...] = p

        @pl.when(m > 0)
        def _():
            o_ref[...] += p

    return pl.pallas_call(
        body, name=name, grid=(K // tk, N // tn, M // tm),
        in_specs=[pl.BlockSpec((tm, tk), lambda kk, j, m: (m, kk)),
                  pl.BlockSpec((tm, tn), lambda kk, j, m: (m, j))],
        out_specs=pl.BlockSpec((tk, tn), lambda kk, j, m: (kk, j)),
        out_shape=jax.ShapeDtypeStruct((K, N), F32),
        compiler_params=_params(dimension_semantics=("arbitrary", "arbitrary", "arbitrary")),
    )(a, b)


def row_call(name, fn, ins, row_outs, col_outs=(), *, M, tm, nc=1):
    ni = len(ins)
    nro = len(row_outs)

    def body(*refs):
        i = pl.program_id(1)
        vals = [_f32(r[...]) for r in refs[:ni]]
        rows, cols = fn(*vals)
        for v, o in zip(rows, refs[ni:ni + nro]):
            o[...] = v.astype(o.dtype)
        for v, o in zip(cols, refs[ni + nro:]):
            v8 = _rows8(v)

            @pl.when(i == 0)
            def _():
                o[...] = v8

            @pl.when(i > 0)
            def _():
                o[...] += v8

    in_specs = []
    for arr, kind, width, c0, cmul in ins:
        if kind == 'row':
            in_specs.append(pl.BlockSpec((tm, width), lambda cj, i, c0=c0, cmul=cmul: (i, c0 + cmul * cj)))
        else:
            in_specs.append(pl.BlockSpec((arr.shape[0], width), lambda cj, i, c0=c0, cmul=cmul: (0, c0 + cmul * cj)))
    out_shape = []
    out_specs = []
    for n_total, dtype, width, c0, cmul in row_outs:
        out_shape.append(jax.ShapeDtypeStruct((M, n_total), dtype))
        out_specs.append(pl.BlockSpec((tm, width), lambda cj, i, c0=c0, cmul=cmul: (i, c0 + cmul * cj)))
    for n_total, width, c0, cmul in col_outs:
        out_shape.append(jax.ShapeDtypeStruct((SUBLANES, n_total), F32))
        out_specs.append(pl.BlockSpec((SUBLANES, width), lambda cj, i, c0=c0, cmul=cmul: (0, c0 + cmul * cj)))
    return pl.pallas_call(
        body, name=name, grid=(nc, M // tm), in_specs=in_specs, out_specs=out_specs,
        out_shape=out_shape,
        compiler_params=_params(dimension_semantics=("arbitrary", "arbitrary")),
    )(*[a[0] for a in ins])


def conv_call(name, src, src_c0, w, K, epi, extras, row_outs, col_outs=(), *, M, tm, cw, nc,
              reverse, xin=None, passthrough=None, t_outs=(), w_c0=0):
    pad = (K - 1) // 2
    assert pad <= HALO - 1
    R = tm // HALO
    nblk = M // HALO
    n_i = M // tm
    Kp = w.shape[0]
    ne = len(extras)
    nro = len(row_outs)
    nco = len(col_outs)
    rb = 64
    cbw = min(cw, 256)
    n_copies = SUBLANES if K > SUBLANES else 1

    def body(*refs):
        main_ref, prev_ref, next_ref, w_ref = refs[:4]
        pos = 4
        xin_ref = None
        if xin is not None:
            xin_ref = refs[pos]
            pos += 1
        e_refs = refs[pos:pos + ne]
        pos += ne
        if passthrough is not None:
            pos += 1
        ro_refs = refs[pos:pos + nro]
        pos += nro
        co_refs = refs[pos:pos + nco]
        pos += nco
        to_refs = refs[pos:pos + len(t_outs)]
        pos += len(t_outs)
        dw_ref = None
        if xin is not None:
            dw_ref = refs[pos]
            pos += 1
        ext_ref, conv_ref = refs[pos], refs[pos + 1]
        i = pl.program_id(1)

        ext_ref[0, 0:HALO, :] = jnp.where(i == 0, 0.0, prev_ref[...].astype(F32))
        ext_ref[0, HALO:HALO + tm, :] = main_ref[...].astype(F32)
        ext_ref[0, HALO + tm:, :] = jnp.where(i == n_i - 1, 0.0, next_ref[...].astype(F32))
        if dw_ref is not None:
            @pl.when(i == 0)
            def _():
                dw_ref[...] = jnp.zeros_like(dw_ref)

        n_sh = tm + 2 * HALO - SUBLANES
        for c0 in range(0, cw, cbw):
            for sft in range(1, n_copies):
                ext_ref[sft, 0:n_sh, c0:c0 + cbw] = ext_ref[0, sft:sft + n_sh, c0:c0 + cbw]

        for c0 in range(0, cw, cbw):
            for r0 in range(0, tm, rb):
                acc = jnp.zeros((rb, cbw), F32)
                if xin_ref is not None:
                    xblk = xin_ref[r0:r0 + rb, c0:c0 + cbw].astype(F32)
                for k in range(K):
                    off = HALO + r0 + ((pad - k) if reverse else (k - pad))
                    sft = off % SUBLANES if n_copies > 1 else 0
                    d = ext_ref[sft, off - sft:off - sft + rb, c0:c0 + cbw]
                    acc = acc + d * w_ref[k:k + 1, c0:c0 + cbw]
                    if xin_ref is not None:
                        dw_ref[k, :, c0:c0 + cbw] += _rows8(xblk * d)
                conv_ref[r0:r0 + rb, c0:c0 + cbw] = acc

        res = epi(conv_ref[...], [_f32(r[...]) for r in e_refs])
        rows, cols = res[0], res[1]
        for v, o in zip(rows, ro_refs):
            o[...] = v.astype(o.dtype)
        for v, o in zip(res[2] if len(res) > 2 else (), to_refs):
            o[...] = v.T.astype(o.dtype)
        for v, o in zip(cols, co_refs):
            v8 = _rows8(v)

            @pl.when(i == 0)
            def _():
                o[...] = v8

            @pl.when(i > 0)
            def _():
                o[...] += v8

    in_specs = [
        pl.BlockSpec((tm, cw), lambda cj, i: (i, src_c0 + cj)),
        pl.BlockSpec((HALO, cw), lambda cj, i: (jnp.maximum(i * R - 1, 0), src_c0 + cj)),
        pl.BlockSpec((HALO, cw), lambda cj, i: (jnp.minimum((i + 1) * R, nblk - 1), src_c0 + cj)),
        pl.BlockSpec((Kp, cw), lambda cj, i: (0, w_c0 + cj)),
    ]
    args = [src, src, src, w]
    if xin is not None:
        in_specs.append(pl.BlockSpec((tm, cw), lambda cj, i, c0=xin[1]: (i, c0 + cj)))
        args.append(xin[0])
    for arr, kind, width, c0, cmul in extras:
        if kind == 'row':
            in_specs.append(pl.BlockSpec((tm, width), lambda cj, i, c0=c0, cmul=cmul: (i, c0 + cmul * cj)))
        else:
            in_specs.append(pl.BlockSpec((arr.shape[0], width), lambda cj, i, c0=c0, cmul=cmul: (0, c0 + cmul * cj)))
        args.append(arr)
    aliases = {}
    if passthrough is not None:
        in_specs.append(pl.BlockSpec(memory_space=pl.ANY))
        aliases = {len(args): passthrough[1]}
        args.append(passthrough[0])
    out_shape = []
    out_specs = []
    for n_total, dtype, width, c0, cmul in row_outs:
        out_shape.append(jax.ShapeDtypeStruct((M, n_total), dtype))
        out_specs.append(pl.BlockSpec((tm, width), lambda cj, i, c0=c0, cmul=cmul: (i, c0 + cmul * cj)))
    for n_total, width, c0, cmul in col_outs:
        out_shape.append(jax.ShapeDtypeStruct((SUBLANES, n_total), F32))
        out_specs.append(pl.BlockSpec((SUBLANES, width), lambda cj, i, c0=c0, cmul=cmul: (0, c0 + cmul * cj)))
    for n_total, dtype, width, c0, cmul in t_outs:
        out_shape.append(jax.ShapeDtypeStruct((n_total, M), dtype))
        out_specs.append(pl.BlockSpec((width, tm), lambda cj, i, c0=c0, cmul=cmul: (c0 + cmul * cj, i)))
    if xin is not None:
        out_shape.append(jax.ShapeDtypeStruct((Kp, SUBLANES, cw * nc), F32))
        out_specs.append(pl.BlockSpec((Kp, SUBLANES, cw), lambda cj, i: (0, 0, cj)))
    return pl.pallas_call(
        body, name=name, grid=(nc, n_i), in_specs=in_specs, out_specs=out_specs,
        out_shape=out_shape, input_output_aliases=aliases,
        scratch_shapes=[pltpu.VMEM((n_copies, tm + 2 * HALO, cw), F32), pltpu.VMEM((tm, cw), F32)],
        compiler_params=_params(dimension_semantics=("arbitrary", "arbitrary")),
    )(*args)


def _split_dot(m_bf16, v, n_pass, dims=None):
    out = None
    rest = v
    for p in range(n_pass):
        piece = rest.astype(BF16)
        if p + 1 < n_pass:
            rest = rest - piece.astype(F32)
        if dims is None:
            t = jnp.dot(m_bf16, piece, preferred_element_type=F32)
        else:
            t = lax.dot_general(m_bf16, piece, dims, preferred_element_type=F32)
        out = t if out is None else out + t
    return out


def _split_dot_r(v, m_bf16, n_pass):
    out = None
    rest = v
    for p in range(n_pass):
        piece = rest.astype(BF16)
        if p + 1 < n_pass:
            rest = rest - piece.astype(F32)
        t = jnp.dot(piece, m_bf16, preferred_element_type=F32)
        out = t if out is None else out + t
    return out


def _softplus(x):
    return jnp.maximum(x, 0.0) + jnp.log1p(jnp.exp(-jnp.abs(x)))


NT_DIMS = (((1,), (1,)), ((), ()))
TN_DIMS = (((0,), (0,)), ((), ()))


def _ssd_common(dtraw, dtbT, alogT, rev, n_heads):
    L = CHUNK
    if rev:
        dtraw = pltpu.roll(dtraw, LANES - n_heads, 1)
    preT = dtraw.T + dtbT
    dtT = _softplus(preT)
    AT = -jnp.exp(alogT)
    aT = dtT * AT
    ri = lax.broadcasted_iota(jnp.int32, (L, L), 0)
    ci = lax.broadcasted_iota(jnp.int32, (L, L), 1)
    up = (ri >= ci) if rev else (ri <= ci)
    lo = (ri <= ci) if rev else (ri >= ci)
    csT = _split_dot_r(aT, up.astype(BF16), 3)
    last = 0 if rev else L - 1
    lastB = jnp.broadcast_to(csT[:, last:last + 1], (L, L))
    return dict(preT=preT, dtT=dtT, AT=AT, csT=csT, cs=csT.T, up=up, lo=lo, ci=ci, last=last,
                doutT=jnp.exp(csT), dstT=jnp.exp(lastB - csT), totB=jnp.exp(lastB))


def ssd_fwd(name, xsT, bc, dtraw, dtbT, alogT, *, S, DI, G, H, rev, tail=None):
    NC = S // CHUNK
    R = H // G
    GW = R * HEAD_DIM
    N = D_STATE
    P = HEAD_DIM

    def body(*refs):
        xsT_ref, bc_ref, dtraw_ref, dtb_ref, alog_ref = refs[:5]
        if tail is None:
            y_ref, st_ref, h_ref = refs[5:]
        else:
            yo_ref, z_ref, xs_ref, dsk_ref, ng_ref = refs[5:10]
            y_ref, st_ref, yn_ref, h_ref = refs[10:]
        c = pl.program_id(0)

        @pl.when(c == 0)
        def _():
            h_ref[...] = jnp.zeros_like(h_ref)

        q = _ssd_common(dtraw_ref[...], dtb_ref[...], alog_ref[...], rev, H)
        cs, csT, dtT, doutT, totB = q['cs'], q['csT'], q['dtT'], q['doutT'], q['totB']
        wstT = q['dstT'] * dtT
        GB = 2 if G % 2 == 0 else 1
        for g0 in range(0, G, GB):
            gs = list(range(g0, g0 + GB))
            Bgs = [bc_ref[:, g * N:(g + 1) * N].astype(BF16) for g in gs]
            Cgs = [bc_ref[:, G * N + g * N:G * N + (g + 1) * N].astype(BF16) for g in gs]
            CBTs = [lax.dot_general(b, c_, NT_DIMS, preferred_element_type=F32) for b, c_ in zip(Bgs, Cgs)]
            HTs = [h_ref[g] for g in gs]
            yoffTs = [lax.dot_general(HT.astype(BF16), c_, NT_DIMS, preferred_element_type=F32)
                      for HT, c_ in zip(HTs, Cgs)]
            xTs = [xsT_ref[g * GW:(g + 1) * GW, :] for g in gs]
            heads = [(k, r) for k in range(GB) for r in range(R)]
            hs = [gs[k] * R + r for k, r in heads]
            blks = [slice(r * P, (r + 1) * P) for _, r in heads]
            segs = [jnp.where(q['up'], csT[h:h + 1, :] - cs[:, h:h + 1], -1e30) for h in hs]
            GTs = [(CBTs[k] * jnp.exp(sg)).astype(BF16) for (k, _), sg in zip(heads, segs)]
            xThs = [xTs[k][b, :] for (k, _), b in zip(heads, blks)]
            XThs = [(xTh * dtT[h:h + 1, :]).astype(BF16) for xTh, h in zip(xThs, hs)]
            ydTs = [jnp.dot(a, GT, preferred_element_type=F32) for a, GT in zip(XThs, GTs)]
            ys = [ydT + yoffTs[k][b, :] * doutT[h:h + 1, :] for ydT, (k, _), b, h in zip(ydTs, heads, blks, hs)]
            xws = [xTh * wstT[h:h + 1, :] for xTh, h in zip(xThs, hs)]
            tots = [jnp.broadcast_to(totB[h:h + 1, :], (P, N)) for h in hs]
            for k, g in enumerate(gs):
                sel = slice(k * R, (k + 1) * R)
                y_ref[:, g * GW:(g + 1) * GW] = jnp.concatenate(ys[sel], axis=0).T
                xwT = jnp.concatenate(xws[sel], axis=0).astype(BF16)
                ST = jnp.dot(xwT, Bgs[k], preferred_element_type=F32)
                st_ref[0, g] = HTs[k]
                h_ref[g] = HTs[k] * jnp.concatenate(tots[sel], axis=0) + ST
        if tail is not None:
            y = y_ref[...] + yo_ref[...]
            y_ref[...] = y
            z = _f32(z_ref[...])
            yz = (y + xs_ref[...] * dsk_ref[...]) * (z * _sig(z))
            for g in range(G):
                t = yz[:, g * GW:(g + 1) * GW]
                tn = t * lax.rsqrt(jnp.mean(t * t, axis=-1, keepdims=True) + RMS_EPS)
                yn_ref[:, g * GW:(g + 1) * GW] = (tn * ng_ref[:, g * GW:(g + 1) * GW]).astype(BF16)

    cidx = (lambda c: NC - 1 - c) if rev else (lambda c: c)
    cmap = lambda c: (cidx(c), 0)
    smap = lambda c: (cidx(c), 0, 0, 0)
    const = lambda c: (0, 0)
    tmap = lambda c: (0, cidx(c))
    in_specs = [pl.BlockSpec((DI, CHUNK), tmap), pl.BlockSpec((CHUNK, 2 * G * N), cmap), pl.BlockSpec((CHUNK, LANES), cmap),
                pl.BlockSpec((LANES, LANES), const), pl.BlockSpec((LANES, LANES), const)]
    out_specs = [pl.BlockSpec((CHUNK, DI), cmap), pl.BlockSpec((1, G, GW, N), smap)]
    out_shape = [jax.ShapeDtypeStruct((S, DI), F32), jax.ShapeDtypeStruct((NC, G, GW, N), F32)]
    args = [xsT, bc, dtraw, dtbT, alogT]
    if tail is not None:
        y_other, (z_arr, z_blk), xs_row, dsk, ng = tail
        in_specs += [pl.BlockSpec((CHUNK, DI), cmap), pl.BlockSpec((CHUNK, DI), lambda c: (cidx(c), z_blk)),
                     pl.BlockSpec((CHUNK, DI), cmap), pl.BlockSpec((1, DI), const), pl.BlockSpec((1, DI), const)]
        out_specs.append(pl.BlockSpec((CHUNK, DI), cmap))
        out_shape.append(jax.ShapeDtypeStruct((S, DI), BF16))
        args += [y_other, z_arr, xs_row, dsk, ng]
    return pl.pallas_call(
        body, name=name, grid=(NC,), in_specs=in_specs, out_specs=out_specs, out_shape=out_shape,
        scratch_shapes=[pltpu.VMEM((G, GW, N), F32)],
        compiler_params=_params(dimension_semantics=("arbitrary",)),
    )(*args)


def ssd_bwd(name, xsT, bc, dtraw, dyT, st, dtbT, alogT, *, S, DI, G, H, rev, tail=None):
    NC = S // CHUNK
    R = H // G
    GW = R * HEAD_DIM
    N = D_STATE
    XBC = DI + 2 * G * N
    P = HEAD_DIM
    L = CHUNK

    def body(*refs):
        xsT_ref, bc_ref, dtraw_ref, dyT_ref, st_ref, dtb_ref, alog_ref = refs[:7]
        if tail is None:
            dxbc_ref, ddt_ref, da_ref, dh_ref, dcst_ref, p2t_ref, p3t_ref, e2t_ref = refs[7:]
        else:
            other_ref, cbx_ref, cbbc_ref, dskT_ref = refs[7:11]
            dxbc_ref, ddt_ref, da_ref, dcol_ref, dh_ref, dcst_ref, p2t_ref, p3t_ref, e2t_ref = refs[11:]
        c = pl.program_id(0)

        @pl.when(c == 0)
        def _():
            dh_ref[...] = jnp.zeros_like(dh_ref)
            da_ref[...] = jnp.zeros_like(da_ref)
            dcst_ref[...] = jnp.zeros_like(dcst_ref)
            p2t_ref[...] = jnp.zeros_like(p2t_ref)
            p3t_ref[...] = jnp.zeros_like(p3t_ref)
            e2t_ref[...] = jnp.zeros_like(e2t_ref)

        q = _ssd_common(dtraw_ref[...], dtb_ref[...], alog_ref[...], rev, H)
        cs, csT, dtT, doutT, dstT, totB = q['cs'], q['csT'], q['dtT'], q['doutT'], q['dstT'], q['totB']
        wstT = dstT * dtT
        lane = q['ci']
        GB = 2 if G % 2 == 0 else 1
        for g0 in range(0, G, GB):
            gs = list(range(g0, g0 + GB))
            Bgs = [bc_ref[:, g * N:(g + 1) * N].astype(BF16) for g in gs]
            Cgs = [bc_ref[:, G * N + g * N:G * N + (g + 1) * N].astype(BF16) for g in gs]
            CBs = [lax.dot_general(c_, b, NT_DIMS, preferred_element_type=F32) for b, c_ in zip(Bgs, Cgs)]
            HpTs = [st_ref[0, g] for g in gs]
            HpTbs = [v.astype(BF16) for v in HpTs]
            dHTs = [dh_ref[g] for g in gs]
            dHTbs = [v.astype(BF16) for v in dHTs]
            BdHTs = [lax.dot_general(d, b, NT_DIMS, preferred_element_type=F32) for d, b in zip(dHTbs, Bgs)]
            yoffTs = [lax.dot_general(hp, c_, NT_DIMS, preferred_element_type=F32) for hp, c_ in zip(HpTbs, Cgs)]
            xTs = [xsT_ref[g * GW:(g + 1) * GW, :] for g in gs]
            dyTs = [dyT_ref[g * GW:(g + 1) * GW, :] for g in gs]
            heads = [(k, r) for k in range(GB) for r in range(R)]
            ks = [k for k, _ in heads]
            hs = [gs[k] * R + r for k, r in heads]
            blks = [slice(r * P, (r + 1) * P) for _, r in heads]
            Lms = [jnp.exp(jnp.where(q['lo'], cs[:, h:h + 1] - csT[h:h + 1, :], -1e30)) for h in hs]
            xThs = [xTs[k][b, :] for k, b in zip(ks, blks)]
            dyThs = [dyTs[k][b, :] for k, b in zip(ks, blks)]
            xThbs = [v.astype(BF16) for v in xThs]
            dyThbs = [v.astype(BF16) for v in dyThs]
            dGxs = [lax.dot_general(a, b, TN_DIMS, preferred_element_type=F32) for a, b in zip(dyThbs, xThbs)]
            Gms = [(CBs[k] * Lm).astype(BF16) for k, Lm in zip(ks, Lms)]
            XThbs = [(xTh * dtT[h:h + 1, :]).astype(BF16) for xTh, h in zip(xThs, hs)]
            u1Ts = [jnp.dot(a, Gm, preferred_element_type=F32) for a, Gm in zip(dyThbs, Gms)]
            ydTs = [lax.dot_general(a, Gm, NT_DIMS, preferred_element_type=F32) for a, Gm in zip(XThbs, Gms)]
            Ts = [dGx * (Lm * dtT[h:h + 1, :]) for dGx, Lm, h in zip(dGxs, Lms, hs)]
            uTs = [u1T + BdHTs[k][b, :] * dstT[h:h + 1, :] for u1T, k, b, h in zip(u1Ts, ks, blks, hs)]
            dyds = [dyTh * doutT[h:h + 1, :] for dyTh, h in zip(dyThs, hs)]
            xws = [xTh * wstT[h:h + 1, :] for xTh, h in zip(xThs, hs)]
            for i, h in enumerate(hs):
                k, b = ks[i], blks[i]
                p3row = jnp.sum(xws[i] * BdHTs[k][b, :], axis=0, keepdims=True)
                seg_row = jnp.sum(_f32(dyThbs[i]) * ydTs[i], axis=0, keepdims=True)
                seg_col = jnp.sum(_f32(XThbs[i]) * u1Ts[i], axis=0, keepdims=True)
                dcst_ref[h:h + 1, :] = (jnp.sum(dyds[i] * yoffTs[k][b, :], axis=0, keepdims=True)
                                        + seg_row - seg_col - p3row)
                p2t_ref[h:h + 1, :] = jnp.sum(xThs[i] * uTs[i], axis=0, keepdims=True)
                p3t_ref[h:h + 1, :] = p3row
                e2t_ref[h:h + 1, :] = jnp.sum(HpTs[k][b, :] * dHTs[k][b, :], axis=0, keepdims=True)
            dxs = [uT * dtT[h:h + 1, :] for uT, h in zip(uTs, hs)]
            if tail is not None:
                dxs = [d + dyTh * dskT_ref[h * P:(h + 1) * P, :] for d, dyTh, h in zip(dxs, dyThs, hs)]
            tots = [jnp.broadcast_to(totB[h:h + 1, :], (P, N)) for h in hs]
            for k, g in enumerate(gs):
                sel = slice(k * R, (k + 1) * R)
                dCB = Ts[k * R]
                for T in Ts[k * R + 1:(k + 1) * R]:
                    dCB = dCB + T
                dxbc_ref[:, g * GW:(g + 1) * GW] = jnp.concatenate(dxs[sel], axis=0).T
                dydT = jnp.concatenate(dyds[sel], axis=0).astype(BF16)
                xwT = jnp.concatenate(xws[sel], axis=0).astype(BF16)
                dCBb = dCB.astype(BF16)
                dC = (jnp.dot(dCBb, Bgs[k], preferred_element_type=F32)
                      + lax.dot_general(dydT, HpTbs[k], TN_DIMS, preferred_element_type=F32))
                dB = (lax.dot_general(dCBb, Cgs[k], TN_DIMS, preferred_element_type=F32)
                      + lax.dot_general(xwT, dHTbs[k], TN_DIMS, preferred_element_type=F32))
                dxbc_ref[:, DI + g * N:DI + (g + 1) * N] = dB
                dxbc_ref[:, DI + G * N + g * N:DI + G * N + (g + 1) * N] = dC
                dh_ref[g] = (dHTs[k] * jnp.concatenate(tots[sel], axis=0)
                             + jnp.dot(dydT, Cgs[k], preferred_element_type=F32))
        e1 = jnp.sum(p3t_ref[...], axis=1, keepdims=True)
        e2 = jnp.sum(e2t_ref[...], axis=1, keepdims=True)
        dcsT = dcst_ref[...] + jnp.where(lane == q['last'], e1 + totB * e2, 0.0)
        daT = _split_dot_r(dcsT, q['lo'].astype(BF16), 3)
        ddtT = daT * q['AT'] + p2t_ref[...]
        da_ref[...] += daT * dtT
        ddraw = jnp.where(lane < H, (ddtT * _sig(q['preT'])).T, 0.0)
        if rev:
            ddraw = pltpu.roll(ddraw, H, 1)
        ddt_ref[...] = ddraw
        if tail is not None:
            for c0, cb_ref in ((0, cbx_ref), (DI, cbbc_ref)):
                d = dxbc_ref[:, c0:c0 + DI] + other_ref[:, c0:c0 + DI]
                cb = cb_ref[...]
                dcb = d * _dsilu(cb, _sig(cb))
                dxbc_ref[:, c0:c0 + DI] = dcb
                part = _rows8(dcb)

                @pl.when(c == 0)
                def _():
                    dcol_ref[:, c0:c0 + DI] = part

                @pl.when(c > 0)
                def _():
                    dcol_ref[:, c0:c0 + DI] += part

    cmap = (lambda c: (c, 0)) if rev else (lambda c: (NC - 1 - c, 0))
    smap = (lambda c: (c, 0, 0, 0)) if rev else (lambda c: (NC - 1 - c, 0, 0, 0))
    const = lambda c: (0, 0)
    sq = pltpu.VMEM((LANES, CHUNK), F32)
    cix = (lambda c: c) if rev else (lambda c: NC - 1 - c)
    tmap = lambda c: (0, cix(c))
    in_specs = [pl.BlockSpec((DI, CHUNK), tmap), pl.BlockSpec((CHUNK, 2 * G * N), cmap), pl.BlockSpec((CHUNK, LANES), cmap),
                pl.BlockSpec((DI, CHUNK), tmap),
                pl.BlockSpec((1, G, GW, N), smap),
                pl.BlockSpec((LANES, LANES), const), pl.BlockSpec((LANES, LANES), const)]
    out_specs = [pl.BlockSpec((CHUNK, XBC), cmap), pl.BlockSpec((CHUNK, LANES), cmap),
                 pl.BlockSpec((LANES, LANES), const)]
    out_shape = [jax.ShapeDtypeStruct((S, XBC), F32), jax.ShapeDtypeStruct((S, LANES), F32),
                 jax.ShapeDtypeStruct((LANES, LANES), F32)]
    args = [xsT, bc, dtraw, dyT, st, dtbT, alogT]
    if tail is not None:
        in_specs += [pl.BlockSpec((CHUNK, XBC), cmap), pl.BlockSpec((CHUNK, DI), cmap),
                     pl.BlockSpec((CHUNK, 2 * G * N), cmap), pl.BlockSpec((DI, LANES), const)]
        out_specs.append(pl.BlockSpec((SUBLANES, XBC), const))
        out_shape.append(jax.ShapeDtypeStruct((SUBLANES, XBC), F32))
        args += list(tail)
    return pl.pallas_call(
        body, name=name, grid=(NC,), in_specs=in_specs, out_specs=out_specs, out_shape=out_shape,
        scratch_shapes=[pltpu.VMEM((G, GW, N), F32), sq, sq, sq, sq],
        compiler_params=_params(dimension_semantics=("arbitrary",)),
    )(*args)


ANY = pl.BlockSpec(memory_space=pl.ANY)


def chip_exchange(name, groups, gather):
    flat = [arr for grp in groups for arr in grp]
    n_in = len(flat)
    n_out = len(groups)
    n_rc = 3 * n_in

    def body(*refs):
        in_refs = refs[:n_in]
        out_refs = refs[n_in:n_in + n_out]
        send, recv = refs[n_in + n_out:]
        x, y, c = lax.axis_index("x"), lax.axis_index("y"), lax.axis_index("c")
        me = 2 * x + y
        peers = [(1 - x, y), (x, 1 - y), (1 - x, 1 - y)]
        remote = []
        q = 0
        for a, grp in enumerate(groups):
            for l in range(len(grp)):
                src = in_refs[q]
                dst = out_refs[a].at[me] if gather else out_refs[a].at[me, l]
                for j, (px, py) in enumerate(peers):
                    blk = src if gather else src.at[2 * px + py]
                    rc = pltpu.make_async_remote_copy(
                        src_ref=blk, dst_ref=dst, send_sem=send.at[3 * q + j], recv_sem=recv.at[3 * q + j],
                        device_id=(px, py, c), device_id_type=MESH)
                    rc.start()
                    remote.append(rc)
                q += 1
        for rc in remote:
            rc.wait()

    out_shape = []
    for grp in groups:
        a0 = grp[0]
        if gather:
            out_shape.append(jax.ShapeDtypeStruct((4,) + a0.shape, a0.dtype))
        else:
            out_shape.append(jax.ShapeDtypeStruct((4, len(grp)) + a0.shape[1:], a0.dtype))
    outs = pl.pallas_call(
        body, name=name, in_specs=[ANY] * n_in, out_specs=[ANY] * n_out, out_shape=out_shape,
        scratch_shapes=[pltpu.SemaphoreType.DMA((n_rc,)), pltpu.SemaphoreType.DMA((n_rc,))],
    )(*flat)
    me = _chip_index()
    res = []
    for grp, o in zip(groups, outs):
        for l, src in enumerate(grp):
            o = _put_block(o, src, (me,)) if gather else _put_block(o, _take_block(src, me), (me, l))
        res.append(o)
    return res


def _chip_index():
    return 2 * lax.axis_index("x") + lax.axis_index("y")


def _take_block(arr, idx):
    return lax.dynamic_index_in_dim(arr, idx, 0, keepdims=False)


def _put_block(dst, blk, idx):
    lead = len(idx)
    return lax.dynamic_update_slice(dst, blk.reshape((1,) * lead + blk.shape), tuple(idx) + (0,) * (dst.ndim - lead))


def gather_layer(name, split, whole):
    ns, nw = len(split), len(whole)
    n = ns + nw
    n_rc = 3 * (n + ns)

    def body(*refs):
        in_refs = refs[:n]
        out_refs = refs[n:2 * n]
        send, recv = refs[2 * n:]
        x, y, c = lax.axis_index("x"), lax.axis_index("y"), lax.axis_index("c")
        me = 2 * x + y
        sibling = (x, y, 1 - c)
        peers = [(1 - x, y), (x, 1 - y), (1 - x, 1 - y)]

        def region(a, chip, half):
            if a >= ns:
                return out_refs[a].at[chip]
            hr = split[a].shape[0] // 2
            return out_refs[a].at[chip, pl.ds(half * hr, hr)]

        def mine(a):
            if a >= ns:
                return in_refs[a]
            hr = split[a].shape[0] // 2
            return in_refs[a].at[pl.ds(c * hr, hr)]

        sends = []
        for a in range(n):
            for j, (px, py) in enumerate(peers):
                rc = pltpu.make_async_remote_copy(
                    src_ref=mine(a), dst_ref=region(a, me, c), send_sem=send.at[3 * a + j],
                    recv_sem=recv.at[3 * a + j], device_id=(px, py, c), device_id_type=MESH)
                rc.start()
                sends.append(rc)
        for a in range(n):
            for j, (px, py) in enumerate(peers):
                chip = 2 * px + py
                landed = pltpu.make_async_remote_copy(
                    src_ref=mine(a), dst_ref=region(a, chip, c), send_sem=send.at[3 * a + j],
                    recv_sem=recv.at[3 * a + j], device_id=(px, py, c), device_id_type=MESH)
                landed.wait_recv()
                if a < ns:
                    fw = pltpu.make_async_remote_copy(
                        src_ref=region(a, chip, c), dst_ref=region(a, chip, c), send_sem=send.at[3 * n + 3 * a + j],
                        recv_sem=recv.at[3 * n + 3 * a + j], device_id=sibling, device_id_type=MESH)
                    fw.start()
                    sends.append(fw)
        for a in range(ns):
            for j, (px, py) in enumerate(peers):
                chip = 2 * px + py
                pltpu.make_async_remote_copy(
                    src_ref=region(a, chip, 1 - c), dst_ref=region(a, chip, 1 - c), send_sem=send.at[3 * n + 3 * a + j],
                    recv_sem=recv.at[3 * n + 3 * a + j], device_id=sibling, device_id_type=MESH).wait_recv()
        for rc in sends:
            rc.wait_send()

    arrs = list(split) + list(whole)
    outs = pl.pallas_call(
        body, name=name, in_specs=[ANY] * n, out_specs=[ANY] * n,
        out_shape=[jax.ShapeDtypeStruct((4,) + a.shape, a.dtype) for a in arrs],
        scratch_shapes=[pltpu.SemaphoreType.DMA((n_rc,)), pltpu.SemaphoreType.DMA((n_rc,))],
    )(*arrs)
    me = _chip_index()
    return [_put_block(o, a, (me,)) for o, a in zip(outs, arrs)]


HBM_SPEC = pl.BlockSpec(memory_space=pltpu.HBM)
SEM_SPEC = pl.BlockSpec(memory_space=pltpu.SEMAPHORE)
IN_FLIGHT = pltpu.SideEffectType.DATAFLOW_SIDE_EFFECTING


def _chip_leg(kind, a_ref, l_ref, shape, c, me, chip):
    if kind == 'gather':
        hr = shape[0] // 2
        rows = pl.ds(c * hr, hr)
        return a_ref.at[rows], l_ref.at[me, rows], l_ref.at[chip, rows]
    return a_ref.at[chip], l_ref.at[me], l_ref.at[chip]


def chip_legs_start(name, kind, arrs, lands):
    n = len(arrs)

    def body(*refs):
        a_refs = refs[:n]
        l_refs = refs[n:2 * n]
        send, recv = refs[2 * n], refs[2 * n + 1]
        token = refs[-1]
        x, y, c = lax.axis_index("x"), lax.axis_index("y"), lax.axis_index("c")
        me = 2 * x + y
        for a in range(n):
            for j, (px, py) in enumerate([(1 - x, y), (x, 1 - y), (1 - x, 1 - y)]):
                src, dst, _ = _chip_leg(kind, a_refs[a], l_refs[a], arrs[a].shape, c, me, 2 * px + py)
                pltpu.make_async_remote_copy(src_ref=src, dst_ref=dst, send_sem=send.at[3 * a + j],
                                             recv_sem=recv.at[3 * a + j], device_id=(px, py, c),
                                             device_id_type=MESH).start()
        token[...] = jnp.zeros_like(token)

    both = list(arrs) + list(lands)
    outs = pl.pallas_call(
        body, name=name,
        out_shape=(pltpu.SemaphoreType.DMA((3 * n,)), pltpu.SemaphoreType.DMA((3 * n,)),
                   *[pltpu.HBM(a.shape, a.dtype) for a in both], jax.ShapeDtypeStruct((SUBLANES, LANES), F32)),
        in_specs=[HBM_SPEC] * (2 * n),
        out_specs=(SEM_SPEC, SEM_SPEC, *[HBM_SPEC] * (2 * n), pl.BlockSpec(memory_space=pltpu.VMEM)),
        input_output_aliases={i: 2 + i for i in range(2 * n)},
        compiler_params=pltpu.CompilerParams(has_side_effects=IN_FLIGHT),
    )(*[pltpu.with_memory_space_constraint(a, pltpu.HBM) for a in both])
    return outs[0], outs[1], list(outs[2:2 + n]), list(outs[2 + n:2 + 2 * n]), outs[-1]


def chip_legs_wait(name, kind, send, recv, arrs, lands, after):
    n = len(arrs)

    def body(*refs):
        a_refs = refs[:n]
        l_refs = refs[n:2 * n]
        send_, recv_ = refs[2 * n], refs[2 * n + 1]
        x, y, c = lax.axis_index("x"), lax.axis_index("y"), lax.axis_index("c")
        me = 2 * x + y
        legs = []
        for a in range(n):
            for j, (px, py) in enumerate([(1 - x, y), (x, 1 - y), (1 - x, 1 - y)]):
                src, dst, landing = _chip_leg(kind, a_refs[a], l_refs[a], arrs[a].shape, c, me, 2 * px + py)
                legs.append(pltpu.make_async_remote_copy(src_ref=src, dst_ref=landing, send_sem=send_.at[3 * a + j],
                                                         recv_sem=recv_.at[3 * a + j], device_id=(px, py, c),
                                                         device_id_type=MESH))
        for leg in legs:
            leg.wait_send()
        for leg in legs:
            leg.wait_recv()

    both = list(arrs) + list(lands)
    outs = pl.pallas_call(
        body, name=name, out_shape=tuple(pltpu.HBM(a.shape, a.dtype) for a in both),
        in_specs=[HBM_SPEC] * (2 * n) + [SEM_SPEC, SEM_SPEC, ANY], out_specs=tuple([HBM_SPEC] * (2 * n)),
        input_output_aliases={i: i for i in range(2 * n)},
        compiler_params=pltpu.CompilerParams(has_side_effects=IN_FLIGHT),
    )(*both, send, recv, after)
    return list(outs[n:])


def gather_finish(name, split, landed):
    n = len(split)

    def body(*refs):
        out_refs = refs[n:2 * n]
        send, recv = refs[2 * n:]
        x, y, c = lax.axis_index("x"), lax.axis_index("y"), lax.axis_index("c")
        sibling = (x, y, 1 - c)
        chips = [2 * (1 - x) + y, 2 * x + (1 - y), 2 * (1 - x) + (1 - y)]

        def region(a, chip, half):
            hr = split[a].shape[0] // 2
            return out_refs[a].at[chip, pl.ds(half * hr, hr)]

        sends = []
        for a in range(n):
            for j, chip in enumerate(chips):
                fw = pltpu.make_async_remote_copy(
                    src_ref=region(a, chip, c), dst_ref=region(a, chip, c), send_sem=send.at[3 * a + j],
                    recv_sem=recv.at[3 * a + j], device_id=sibling, device_id_type=MESH)
                fw.start()
                sends.append(fw)
        for a in range(n):
            for j, chip in enumerate(chips):
                pltpu.make_async_remote_copy(
                    src_ref=region(a, chip, 1 - c), dst_ref=region(a, chip, 1 - c), send_sem=send.at[3 * a + j],
                    recv_sem=recv.at[3 * a + j], device_id=sibling, device_id_type=MESH).wait_recv()
        for fw in sends:
            fw.wait_send()

    outs = pl.pallas_call(
        body, name=name, in_specs=[ANY] * n, out_specs=[ANY] * n,
        out_shape=[jax.ShapeDtypeStruct(a.shape, a.dtype) for a in landed],
        input_output_aliases={a: a for a in range(n)},
        scratch_shapes=[pltpu.SemaphoreType.DMA((3 * n,)), pltpu.SemaphoreType.DMA((3 * n,))],
    )(*landed)
    me = _chip_index()
    return [_put_block(o, a, (me,)) for o, a in zip(outs, split)]


def place_own(arrs, landed):
    me = _chip_index()
    return [_put_block(l, _take_block(a, me), (me,)) for a, l in zip(arrs, landed)]


def core_send_half(name, arrs):
    n = len(arrs)

    def body(*refs):
        in_refs = refs[:n]
        out_refs = refs[n:2 * n]
        send, recv = refs[2 * n:]
        c = lax.axis_index("c")
        peer = (lax.axis_index("x"), lax.axis_index("y"), 1 - c)
        rcs = []
        for a in range(n):
            hr = arrs[a].shape[1] // 2
            rc = pltpu.make_async_remote_copy(
                src_ref=in_refs[a].at[:, pl.ds((1 - c) * hr, hr)], dst_ref=out_refs[a], send_sem=send.at[a],
                recv_sem=recv.at[a], device_id=peer, device_id_type=MESH)
            rc.start()
            rcs.append(rc)
        for rc in rcs:
            rc.wait()

    return pl.pallas_call(
        body, name=name, in_specs=[ANY] * n, out_specs=[ANY] * n,
        out_shape=[jax.ShapeDtypeStruct((4, a.shape[1] // 2, a.shape[2]), a.dtype) for a in arrs],
        scratch_shapes=[pltpu.SemaphoreType.DMA((n,)), pltpu.SemaphoreType.DMA((n,))],
    )(*arrs)


def core_fill(name, arrs, layer, n_layers):
    n = len(arrs)

    def body(*refs):
        out_refs = refs[n:2 * n]
        send, recv = refs[2 * n:]
        c = lax.axis_index("c")
        peer = (lax.axis_index("x"), lax.axis_index("y"), 1 - c)
        rcs = []
        for a in range(n):
            r = arrs[a].shape[0] // n_layers
            hr = r // 2
            rows = out_refs[a].at[pl.ds(layer * r + c * hr, hr)]
            rc = pltpu.make_async_remote_copy(src_ref=rows, dst_ref=rows, send_sem=send.at[a], recv_sem=recv.at[a],
                                              device_id=peer, device_id_type=MESH)
            rc.start()
            rcs.append(rc)
        for a in range(n):
            r = arrs[a].shape[0] // n_layers
            hr = r // 2
            theirs = out_refs[a].at[pl.ds(layer * r + (1 - c) * hr, hr)]
            pltpu.make_async_remote_copy(src_ref=theirs, dst_ref=theirs, send_sem=send.at[a], recv_sem=recv.at[a],
                                         device_id=peer, device_id_type=MESH).wait_recv()
        for rc in rcs:
            rc.wait_send()

    return pl.pallas_call(
        body, name=name, in_specs=[ANY] * n, out_specs=[ANY] * n,
        out_shape=[jax.ShapeDtypeStruct(a.shape, a.dtype) for a in arrs],
        input_output_aliases={a: a for a in range(n)},
        scratch_shapes=[pltpu.SemaphoreType.DMA((n,)), pltpu.SemaphoreType.DMA((n,))],
    )(*arrs)


def sibling_swap(name, arrs):
    n = len(arrs)

    def body(*refs):
        in_refs = refs[:n]
        out_refs = refs[n:2 * n]
        send, recv = refs[2 * n:]
        peer = (lax.axis_index("x"), lax.axis_index("y"), 1 - lax.axis_index("c"))
        rcs = []
        for a in range(n):
            rc = pltpu.make_async_remote_copy(src_ref=in_refs[a], dst_ref=out_refs[a], send_sem=send.at[a],
                                              recv_sem=recv.at[a], device_id=peer, device_id_type=MESH)
            rc.start()
            rcs.append(rc)
        for rc in rcs:
            rc.wait()

    return pl.pallas_call(
        body, name=name, in_specs=[ANY] * n, out_specs=[ANY] * n,
        out_shape=[jax.ShapeDtypeStruct(a.shape, a.dtype) for a in arrs],
        scratch_shapes=[pltpu.SemaphoreType.DMA((n,)), pltpu.SemaphoreType.DMA((n,))],
    )(*arrs)


def all8_gather(name, v):
    flips = [(fx, fy, fc) for fx in (0, 1) for fy in (0, 1) for fc in (0, 1) if (fx, fy, fc) != (0, 0, 0)]

    def body(v_ref, out_ref, send, recv, loc):
        x, y, c = lax.axis_index("x"), lax.axis_index("y"), lax.axis_index("c")
        me = 4 * x + 2 * y + c
        lc = pltpu.make_async_copy(v_ref, out_ref.at[me], loc)
        lc.start()
        rcs = []
        for k, (fx, fy, fc) in enumerate(flips):
            tgt = (x + fx - 2 * x * fx, y + fy - 2 * y * fy, c + fc - 2 * c * fc)
            rc = pltpu.make_async_remote_copy(src_ref=v_ref, dst_ref=out_ref.at[me], send_sem=send.at[k],
                                              recv_sem=recv.at[k], device_id=tgt, device_id_type=MESH)
            rc.start()
            rcs.append(rc)
        lc.wait()
        for rc in rcs:
            rc.wait()

    return pl.pallas_call(
        body, name=name, in_specs=[ANY], out_specs=ANY,
        out_shape=jax.ShapeDtypeStruct((8,) + v.shape, v.dtype),
        scratch_shapes=[pltpu.SemaphoreType.DMA((7,)), pltpu.SemaphoreType.DMA((7,)), pltpu.SemaphoreType.DMA],
    )(v)


def _pick_rows(rows, cols, target_elems=128 * 1024, mult=SUBLANES):
    if rows % mult != 0:
        return rows
    best = mult
    t = mult
    while t <= rows:
        if rows % t == 0 and t * cols <= target_elems:
            best = t
        t += mult
    return best


def sum_chips(name, parts):
    _, R, C = parts.shape
    tm = _pick_rows(R, C)

    def body(p_ref, o_ref):
        o_ref[...] = (p_ref[0] + p_ref[1]) + (p_ref[2] + p_ref[3])

    return pl.pallas_call(
        body, name=name, grid=(R // tm,),
        in_specs=[pl.BlockSpec((4, tm, C), lambda i: (0, i, 0))],
        out_specs=pl.BlockSpec((tm, C), lambda i: (i, 0)),
        out_shape=jax.ShapeDtypeStruct((R, C), F32),
        compiler_params=_params(dimension_semantics=("arbitrary",)),
    )(parts)


def _adamw(g, w, m, v):
    m = ADAM_B1 * m + (1.0 - ADAM_B1) * g
    v = ADAM_B2 * v + (1.0 - ADAM_B2) * (g * g)
    m_hat = m / (1.0 - ADAM_B1 ** ADAM_STEP)
    v_hat = v / (1.0 - ADAM_B2 ** ADAM_STEP)
    delta = -ADAM_LR * (m_hat / (jnp.sqrt(v_hat) + ADAM_EPS) + ADAM_WD * w)
    return delta, m, v


def adamw_shard(name, s_mine, s_sib, w, m, v):
    R, C = w.shape
    tm = _pick_rows(R, C)

    def body(a_ref, b_ref, w_ref, m_ref, v_ref, g_out, d_out, m_out, v_out):
        g = a_ref[...] + b_ref[...]
        d, mn, vn = _adamw(g, w_ref[...], m_ref[...], v_ref[...])
        g_out[...] = g
        d_out[...] = d
        m_out[...] = mn
        v_out[...] = vn

    spec = pl.BlockSpec((tm, C), lambda i: (i, 0))
    return pl.pallas_call(
        body, name=name, grid=(R // tm,), in_specs=[spec] * 5, out_specs=[spec] * 4,
        out_shape=[jax.ShapeDtypeStruct((R, C), F32)] * 4,
        compiler_params=_params(dimension_semantics=("arbitrary",)),
    )(s_mine, s_sib, w, m, v)


def core_sum(name, core, g, got):
    _, r, C = g.shape
    hr = r // 2
    tm = _pick_rows(hr, 4 * C, 256 * 1024, 2 * SUBLANES)
    nh = hr // tm

    def body(c_ref, g_ref, s_ref, o_ref):
        o_ref[...] = (g_ref[...] + s_ref[...]).astype(BF16)

    return pl.pallas_call(
        body, name=name,
        grid_spec=pltpu.PrefetchScalarGridSpec(
            num_scalar_prefetch=1, grid=(nh,),
            in_specs=[pl.BlockSpec((4, tm, C), lambda i, cr: (0, cr[0] * nh + i, 0)),
                      pl.BlockSpec((4, tm, C), lambda i, cr: (0, i, 0))],
            out_specs=pl.BlockSpec((4, tm, C), lambda i, cr: (0, i, 0))),
        out_shape=jax.ShapeDtypeStruct((4, hr, C), BF16),
        compiler_params=_params(dimension_semantics=("arbitrary",)),
    )(core, g, got)


def chip_sum_into(name, core, parts, layer, n_layers, into=None):
    _, hr, C = parts.shape
    r = 2 * hr
    tm = _pick_rows(hr, 4 * C, 256 * 1024, 2 * SUBLANES)
    nh = hr // tm

    def body(c_ref, p_ref, *rest):
        o_ref = rest[-1]
        o_ref[...] = (_f32(p_ref[0]) + _f32(p_ref[1])) + (_f32(p_ref[2]) + _f32(p_ref[3]))

    in_specs = [pl.BlockSpec((4, tm, C), lambda i, cr: (0, i, 0))]
    args = [core, parts]
    aliases = {}
    if into is not None:
        in_specs.append(pl.BlockSpec(memory_space=pl.ANY))
        args.append(into)
        aliases = {2: 0}
    return pl.pallas_call(
        body, name=name,
        grid_spec=pltpu.PrefetchScalarGridSpec(
            num_scalar_prefetch=1, grid=(nh,), in_specs=in_specs,
            out_specs=pl.BlockSpec((tm, C), lambda i, cr: ((layer * r) // tm + cr[0] * nh + i, 0))),
        out_shape=jax.ShapeDtypeStruct((n_layers * r, C), F32), input_output_aliases=aliases,
        compiler_params=_params(dimension_semantics=("arbitrary",)),
    )(*args)


def adamw_full(name, g, w, m, v):
    R, C = w.shape
    tm = _pick_rows(R, C)

    def body(g_ref, w_ref, m_ref, v_ref, d_out, m_out, v_out):
        d, mn, vn = _adamw(g_ref[...], w_ref[...], m_ref[...], v_ref[...])
        d_out[...] = d
        m_out[...] = mn
        v_out[...] = vn

    spec = pl.BlockSpec((tm, C), lambda i: (i, 0))
    return pl.pallas_call(
        body, name=name, grid=(R // tm,), in_specs=[spec] * 4, out_specs=[spec] * 3,
        out_shape=[jax.ShapeDtypeStruct((R, C), F32)] * 3,
        compiler_params=_params(dimension_semantics=("arbitrary",)),
    )(g, w, m, v)


def fold_rows(name, v):
    def body(v_ref, o_ref):
        o_ref[...] = jnp.sum(v_ref[...], axis=0, keepdims=True)

    return pl.pallas_call(body, name=name, out_shape=jax.ShapeDtypeStruct((1, v.shape[1]), F32),
                          compiler_params=_params())(v)


def adamw_small(name, parts, w, m, v):
    W = w.shape[1]

    def body(p_ref, w_ref, m_ref, v_ref, g_out, d_out, m_out, v_out):
        acc = p_ref[0]
        for k in range(1, 8):
            acc = acc + p_ref[k]
        g = jnp.sum(acc, axis=0, keepdims=True)
        d, mn, vn = _adamw(g, w_ref[...], m_ref[...], v_ref[...])
        g_out[...] = g
        d_out[...] = d
        m_out[...] = mn
        v_out[...] = vn

    return pl.pallas_call(
        body, name=name, out_shape=[jax.ShapeDtypeStruct((1, W), F32)] * 4,
        compiler_params=_params(),
    )(parts, w, m, v)


def _pad_lanes(v, width=LANES):
    return jnp.pad(v, ((0, 0), (0, width - v.shape[1])))


def _layer_fwd(cf, x, xb, pb, W, sm, target=None):
    S, D, CD, DI, XBC, F, H, G = cf['S'], cf['D'], cf['CD'], cf['DI'], cf['XBC'], cf['F'], cf['H'], cf['G']
    NM = cf['NM']
    alpha = cf['alpha']
    tm = cf['tm']
    tmx = cf['tmx']
    tn_in = cf['tn_in']
    sv = {}

    ident = lambda acc, ex: ([acc], [])
    proj, = fused_mm("in_proj", [(xb, W['in_main'], 0, False)], [], ident, [(NM, BF16, tn_in, 0)],
                     M=S, tm=tmx, tn=tn_in, nj=NM // tn_in)
    dtraw, = fused_mm("dt_proj", [(xb, W['in_dt'], 0, False)], [], ident, [(LANES, F32, LANES, 0)],
                      M=S, tm=tmx, tn=LANES)

    u, = row_call("glu", lambda a, gt: ([a * _sig(gt)], []),
                  [(proj, 'row', CD, 0, 0), (proj, 'row', CD, 1, 0)], [(CD, F32, CD, 0, 0)], M=S, tm=tm)

    def conv_a_epi(conv, ex):
        cb_, g_, b_ = ex
        ca = conv + cb_
        xhat, _ = _ln_stats(ca)
        la = xhat * g_ + b_
        return [ca, la * _sig(la)], []

    ca, sa = conv_call("conv_a", u, 0, sm['conv_a_w'], cf['KA'], conv_a_epi,
                       [(sm['conv_a_b'], 'vec', CD, 0, 0), (sm['ln_a_g'], 'vec', CD, 0, 0), (sm['ln_a_b'], 'vec', CD, 0, 0)],
                       [(CD, F32, CD, 0, 0), (CD, BF16, CD, 0, 0)], M=S, tm=cf['tmc'], cw=CD, nc=1, reverse=False)
    y_a, = fused_mm("a_out", [(sa, W['a_out'], 0, False)], [], ident, [(D, F32, D, 0)], M=S, tm=tmx, tn=D)

    def conv_x_epi(conv, ex):
        cb = conv + ex[0]
        act = cb * _sig(cb)
        return [cb, act], [], [act]

    def conv_bc_epi(conv, ex):
        cb = conv + ex[0]
        return [cb, cb * _sig(cb)], []

    xoff = (2 * CD + 2 * D + DI) // DI
    cbv_x, xs, xsT = conv_call("conv_b_x", proj, xoff, sm['ssm_conv_w'], cf['KB'], conv_x_epi,
                               [(sm['ssm_conv_b'], 'vec', DI, 0, 0)],
                               [(DI, F32, DI, 0, 0), (DI, F32, DI, 0, 0)], M=S, tm=cf['tmc'], cw=DI, nc=1,
                               reverse=False, t_outs=[(DI, F32, DI, 0, 0)])
    cbv_bc, bc = conv_call("conv_b_bc", proj, xoff + 1, sm['ssm_conv_w'], cf['KB'], conv_bc_epi,
                           [(sm['ssm_conv_b'], 'vec', DI, 1, 0)],
                           [(DI, F32, DI, 0, 0), (DI, F32, DI, 0, 0)], M=S, tm=cf['tmc'], cw=DI, nc=1,
                           reverse=False, w_c0=1)
    y_f, st_f = ssd_fwd("ssd_fwd_f", xsT, bc, dtraw, sm['dtb_f'], sm['alog_f'], S=S, DI=DI, G=G, H=H, rev=False)
    zoff = (2 * CD + 2 * D) // DI
    ysum, st_r, yn = ssd_fwd("ssd_fwd_r", xsT, bc, dtraw, sm['dtb_r'], sm['alog_r'], S=S, DI=DI, G=G, H=H, rev=True,
                             tail=(y_f, (proj, zoff), xs, sm['dskip_full'], sm['ssm_norm_g']))
    goff = (2 * CD) // D

    def merge_epi(acc, ex):
        ga, gb, ya = ex
        return [acc, _sig(ga) * ya + _sig(gb) * acc], []

    y_b, merged = fused_mm("b_out", [(yn, W['b_out'], 0, False)],
                           [(proj, 'row', D, goff), (proj, 'row', D, goff + 1), (y_a, 'row', D, 0)],
                           merge_epi, [(D, F32, D, 0), (D, BF16, D, 0)], M=S, tm=tm, tn=D)

    def mix_epi(acc, ex):
        xin, g_, b_ = ex
        r1 = alpha * xin + acc
        xhat, _ = _ln_stats(r1)
        return [r1, xhat * g_ + b_], []

    r1, hb = fused_mm("o_mix", [(merged, W['o'], 0, False)],
                      [(x, 'row', D, 0), (sm['ln1_g'], 'vec', D, 0), (sm['ln1_b'], 'vec', D, 0)],
                      mix_epi, [(D, F32, D, 0), (D, BF16, D, 0)], M=S, tm=tm, tn=D)

    tnf = cf['tnf']

    g32, g_ = fused_mm("ffn_gate", [(hb, W['gate_up'], 0, False)], [], lambda acc, ex: ([acc, acc], []),
                       [(F, F32, tnf, 0), (F, BF16, tnf, 0)], M=S, tm=tmx, tn=tnf, nj=F // tnf)
    u_, f = fused_mm("ffn_up", [(hb, W['gate_up'], F // tnf, False)], [(g32, 'row', tnf, 0)],
                     lambda acc, ex: ([acc, ex[0] * _sig(ex[0]) * acc], []),
                     [(F, BF16, tnf, 0), (F, BF16, tnf, 0)], M=S, tm=tmx, tn=tnf, nj=F // tnf)

    def down_epi(acc, ex):
        r1_, g1, b1, g2, b2 = ex
        xh1, _ = _ln_stats(r1_)
        r2 = alpha * (xh1 * g1 + b1) + acc
        xh2, _ = _ln_stats(r2)
        return [r2, xh2 * g2 + b2], []

    r2, h2b = fused_mm("ffn_down", [(f, W['down'], 0, False)],
                       [(r1, 'row', D, 0), (sm['ln1_g'], 'vec', D, 0), (sm['ln1_b'], 'vec', D, 0),
                        (sm['ln2_g'], 'vec', D, 0), (sm['ln2_b'], 'vec', D, 0)],
                       down_epi, [(D, F32, D, 0), (D, BF16, D, 0)], M=S, tm=tm, tn=D)

    pe, = fused_mm("ple_proj", [(pb, W['ple'], 0, False)], [], ident, [(D, F32, D, 0)], M=S, tm=tmx, tn=D)

    def ple_out(acc, r2_, g2, b2, pe_, pg):
        xh2, _ = _ln_stats(r2_)
        h2 = xh2 * g2 + b2
        e = pe_ * lax.rsqrt(jnp.mean(pe_ * pe_, axis=-1, keepdims=True) + RMS_EPS) * pg
        return h2 + e * _sig(acc)

    ple_extras = [(r2, 'row', D, 0), (sm['ln2_g'], 'vec', D, 0), (sm['ln2_b'], 'vec', D, 0),
                  (pe, 'row', D, 0), (sm['ple_norm_g'], 'vec', D, 0)]
    if target is None:
        def ple_epi(acc, ex):
            xn_ = ple_out(acc, *ex)
            return [acc, xn_, xn_], []

        t_, xn, xnb = fused_mm("ple_gate", [(h2b, W['ple_gate'], 0, False)], ple_extras,
                               ple_epi, [(D, F32, D, 0), (D, F32, D, 0), (D, BF16, D, 0)], M=S, tm=tm, tn=D)
    else:
        def ple_loss_epi(acc, ex):
            err = ple_out(acc, *ex[:5]) - ex[5]
            dx_ = err * (1.0 / D)
            dtg, dpe, dpg = _ple_bwd(dx_, acc, ex[3], ex[4])
            return [acc, dx_, dtg, dpe], [dpg, err * err]

        t_, dxn, dtg, dpe, dpg, lsq = fused_mm(
            "ple_gate_loss", [(h2b, W['ple_gate'], 0, False)], ple_extras + [(target, 'row', D, 0)],
            ple_loss_epi, [(D, F32, D, 0), (D, F32, D, 0), (D, BF16, D, 0), (D, BF16, D, 0)],
            [(D, D, 0), (D, D, 0)], M=S, tm=tm, tn=D)
        xn = xnb = None
        sv['head'] = (dxn, dtg, dpe, dpg, lsq)
    sv.update(x=x, xb=xb, pb=pb, proj=proj, dtraw=dtraw, u=u, ca=ca, sa=sa, y_a=y_a, cbv_x=cbv_x, cbv_bc=cbv_bc,
              xs=xs, xsT=xsT, bc=bc,
              ysum=ysum, st_f=st_f, st_r=st_r, yn=yn, y_b=y_b, merged=merged, r1=r1, hb=hb,
              g_=g_, u_=u_, f=f, r2=r2, h2b=h2b, t_=t_, pe=pe)
    return xn, xnb, sv


def _ple_bwd(dx_, t, pe_, pg):
    s = _sig(t)
    rinv = lax.rsqrt(jnp.mean(pe_ * pe_, axis=-1, keepdims=True) + RMS_EPS)
    pn = pe_ * rinv
    e = pn * pg
    dtg = dx_ * e * (s * (1.0 - s))
    de = dx_ * s
    qv = de * pg
    dpe = rinv * (qv - pn * jnp.mean(qv * pn, axis=-1, keepdims=True))
    return dtg, dpe, de * pn


def _layer_bwd(cf, sv, W, sm, dxn=None):
    S, D, CD, DI, XBC, F, H, G = cf['S'], cf['D'], cf['CD'], cf['DI'], cf['XBC'], cf['F'], cf['H'], cf['G']
    NM = cf['NM']
    alpha = cf['alpha']
    tm = cf['tm']
    gw = cf['GW']
    out = {}

    if dxn is None:
        dxn, dtg, dpe, dpg, out['loss_sq'] = sv['head']
    else:
        def mid(dx_, t, pe_, pg):
            dtg, dpe, dpg = _ple_bwd(dx_, t, pe_, pg)
            return [dtg, dpe], [dpg]

        (dtg, dpe, dpg) = row_call(
            "ple_bwd", mid,
            [(dxn, 'row', D, 0, 0), (sv['t_'], 'row', D, 0, 0), (sv['pe'], 'row', D, 0, 0),
             (sm['ple_norm_g'], 'vec', D, 0, 0)],
            [(D, BF16, D, 0, 0), (D, BF16, D, 0, 0)], [(D, D, 0, 0)], M=S, tm=tm)
    out['ple_norm_g'] = dpg

    def ln_bwd_epi(scale):
        def epi(acc, ex):
            res, r_, g_ = ex
            dh = scale * res + acc
            xhat, rstd = _ln_stats(r_)
            dr = _ln_bwd(dh, xhat, rstd, g_)
            return [dr, dr], [dh * xhat, dh]
        return epi

    dr2, dr2b, dg2, db2 = fused_mm(
        "dh2", [(dtg, W['ple_gate_T'], 0, False)],
        [(dxn, 'row', D, 0), (sv['r2'], 'row', D, 0), (sm['ln2_g'], 'vec', D, 0)],
        ln_bwd_epi(1.0), [(D, F32, D, 0), (D, BF16, D, 0)], [(D, D, 0), (D, D, 0)], M=S, tm=tm, tn=D)
    out['ln2_g'], out['ln2_b'] = dg2, db2

    tnf = cf['tnf']

    def dswiglu_epi(acc, ex):
        gg, uu = ex
        s = _sig(gg)
        return [acc * uu * _dsilu(gg, s), acc * (gg * s)], []

    dg_b, du_b = fused_mm(
        "d_down", [(dr2b, W['down_T'], 0, False)],
        [(sv['g_'], 'row', tnf, 0), (sv['u_'], 'row', tnf, 0)], dswiglu_epi,
        [(F, BF16, tnf, 0), (F, BF16, tnf, 0)], M=S, tm=tm, tn=tnf, nj=F // tnf)

    dr1, dr1b, dg1, db1 = fused_mm(
        "dh1", [(dg_b, W['gate_T'], 0, True), (du_b, W['up_T'], 0, True)],
        [(dr2, 'row', D, 0), (sv['r1'], 'row', D, 0), (sm['ln1_g'], 'vec', D, 0)],
        ln_bwd_epi(alpha), [(D, F32, D, 0), (D, BF16, D, 0)], [(D, D, 0), (D, D, 0)],
        M=S, tm=tm, tn=D, nk=cf['nk_f'])
    out['ln1_g'], out['ln1_b'] = dg1, db1

    goff = (2 * CD) // D

    def dmerge_epi(acc, ex):
        ga, gb, ya, yb = ex
        sa_, sb_ = _sig(ga), _sig(gb)
        dga = acc * ya * (sa_ * (1.0 - sa_))
        dgb = acc * yb * (sb_ * (1.0 - sb_))
        return [jnp.concatenate([dga, dgb], axis=1), acc * sa_, acc * sb_], []

    dproj, dya_b, dyb_b = fused_mm(
        "d_merge", [(dr1b, W['o_T'], 0, False)],
        [(sv['proj'], 'row', D, goff), (sv['proj'], 'row', D, goff + 1), (sv['y_a'], 'row', D, 0), (sv['y_b'], 'row', D, 0)],
        dmerge_epi, [(NM, BF16, 2 * D, (2 * CD) // (2 * D)), (D, BF16, D, 0), (D, BF16, D, 0)], M=S, tm=tm, tn=D)

    def dsa_epi(acc, ex):
        ca_, g_, b_ = ex
        xhat, rstd = _ln_stats(ca_)
        la = xhat * g_ + b_
        dla = acc * _dsilu(la, _sig(la))
        dca = _ln_bwd(dla, xhat, rstd, g_)
        return [dca], [dla * xhat, dla, dca]

    dca, dlag, dlab, dcab = fused_mm(
        "d_a_out", [(dya_b, W['a_out_T'], 0, False)],
        [(sv['ca'], 'row', CD, 0), (sm['ln_a_g'], 'vec', CD, 0), (sm['ln_a_b'], 'vec', CD, 0)],
        dsa_epi, [(CD, F32, CD, 0)], [(CD, CD, 0), (CD, CD, 0), (CD, CD, 0)], M=S, tm=tm, tn=D)
    out['ln_a_g'], out['ln_a_b'], out['conv_a_b'] = dlag, dlab, dcab

    def dglu_epi(du, ex):
        a, gt = ex
        s = _sig(gt)
        return [jnp.concatenate([du * s, du * a * (s * (1.0 - s))], axis=1)], []

    dproj, dwa = conv_call(
        "d_conv_a", dca, 0, sm['conv_a_w'], cf['KA'], dglu_epi,
        [(sv['proj'], 'row', CD, 0, 0), (sv['proj'], 'row', CD, 1, 0)],
        [(NM, BF16, 2 * CD, 0, 0)], M=S, tm=cf['tmc'], cw=CD, nc=1, reverse=True, xin=(sv['u'], 0),
        passthrough=(dproj, 0))
    out['conv_a_w'] = dwa

    zoff = (2 * CD + 2 * D) // DI

    def dgate_norm_epi(acc, ex):
        ysum_, xs, z, dsk, ng = ex
        y = ysum_ + xs * dsk
        sz = _sig(z)
        siluz = z * sz
        yz = y * siluz
        dyzs, yhats = [], []
        for g in range(G):
            t = yz[:, g * gw:(g + 1) * gw]
            rinv = lax.rsqrt(jnp.mean(t * t, axis=-1, keepdims=True) + RMS_EPS)
            yh = t * rinv
            qv = acc[:, g * gw:(g + 1) * gw] * ng[:, g * gw:(g + 1) * gw]
            dyzs.append(rinv * (qv - yh * jnp.mean(qv * yh, axis=-1, keepdims=True)))
            yhats.append(yh)
        dyz = jnp.concatenate(dyzs, axis=1)
        yhat = jnp.concatenate(yhats, axis=1)
        dy = dyz * siluz
        dz = dyz * y * _dsilu(z, sz)
        return [dz], [acc * yhat, dy * xs], [dy]

    tmr = cf['tmr']
    dproj, dng, ddsk, dyT = fused_mm(
        "d_b_out", [(dyb_b, W['b_out_T'], 0, False)],
        [(sv['ysum'], 'row', DI, 0), (sv['xs'], 'row', DI, 0), (sv['proj'], 'row', DI, zoff),
         (sm['dskip_full'], 'vec', DI, 0), (sm['ssm_norm_g'], 'vec', DI, 0)],
        dgate_norm_epi, [(NM, BF16, DI, zoff)], [(DI, DI, 0), (DI, DI, 0)],
        M=S, tm=tmr, tn=DI, passthrough=(dproj, 0), t_outs=[(DI, F32, DI, 0)])
    out['ssm_norm_g'], out['dskip_full'] = dng, ddsk

    dxbc_f, ddt_f, dA_f = ssd_bwd("ssd_bwd_f", sv['xsT'], sv['bc'], sv['dtraw'], dyT, sv['st_f'], sm['dtb_f'],
                                  sm['alog_f'], S=S, DI=DI, G=G, H=H, rev=False)
    dcb, ddt_r, dA_r, dcbb = ssd_bwd("ssd_bwd_r", sv['xsT'], sv['bc'], sv['dtraw'], dyT, sv['st_r'], sm['dtb_r'],
                                     sm['alog_r'], S=S, DI=DI, G=G, H=H, rev=True,
                                     tail=(dxbc_f, sv['cbv_x'], sv['cbv_bc'], sm['dskipT']))
    out['dA_f'], out['dA_r'] = dA_f, dA_r
    out['ssm_conv_b'] = dcbb

    xoff = (2 * CD + 2 * D + DI) // DI
    dproj, dwb = conv_call(
        "d_conv_b", dcb, 0, sm['ssm_conv_w'], cf['KB'], lambda conv, ex: ([conv], []), [],
        [(NM, BF16, DI, xoff, 1)], M=S, tm=cf['tmc'], cw=DI, nc=XBC // DI, reverse=True, xin=(sv['proj'], xoff),
        passthrough=(dproj, 0))
    out['ssm_conv_w'] = dwb

    ddtb, ddt_bias = row_call("d_dt", lambda a, b: ([a + b], [a + b]),
                              [(ddt_f, 'row', LANES, 0, 0), (ddt_r, 'row', LANES, 0, 0)],
                              [(LANES, BF16, LANES, 0, 0)], [(LANES, LANES, 0, 0)], M=S, tm=tm)
    out['dt_bias'] = ddt_bias

    dx, = fused_mm("d_x", [(dproj, W['in_main_T'], 0, True), (ddtb, W['in_dt_T'], 0, False)],
                   [(dr1, 'row', D, 0)], lambda acc, ex: ([alpha * ex[0] + acc], []),
                   [(D, F32, D, 0)], M=S, tm=cf['tmx'], tn=D, nk=cf['nk_in'])

    tmw = cf['tmw']
    xb = sv['xb']
    out['w_in'] = jnp.concatenate(
        [mm_tn("dw_in", xb, dproj, tm=tmw, tk=D, tn=2 * cf['tn_in'] if NM % (2 * cf['tn_in']) == 0 else cf['tn_in']),
         mm_tn("dw_dt", xb, ddtb, tm=tmw, tk=D, tn=LANES)[:, :2 * H]], axis=1)
    out['w_a_out'] = mm_tn("dw_a_out", sv['sa'], dya_b, tm=tmw, tk=CD, tn=D)
    out['w_b_out'] = mm_tn("dw_b_out", sv['yn'], dyb_b, tm=tmw, tk=DI // 2, tn=D)
    out['w_o'] = mm_tn("dw_o", sv['merged'], dr1b, tm=tmw, tk=D, tn=D)
    out['w_gate_up'] = jnp.concatenate(
        [mm_tn("dw_gate", sv['hb'], dg_b, tm=tmw, tk=D, tn=tnf),
         mm_tn("dw_up", sv['hb'], du_b, tm=tmw, tk=D, tn=tnf)], axis=1)
    out['w_down'] = mm_tn("dw_down", sv['f'], dr2b, tm=tmw, tk=tnf, tn=D)
    out['w_ple'] = mm_tn("dw_ple", sv['pb'], dpe, tm=tmw, tk=sv['pb'].shape[1], tn=D)
    out['w_ple_gate'] = mm_tn("dw_ple_gate", sv['h2b'], dtg, tm=tmw, tk=D, tn=D)
    return dx, out


_WEIGHTS = ['w_in', 'conv_a_w', 'conv_a_b', 'ln_a_g', 'ln_a_b', 'w_a_out', 'ssm_conv_w', 'ssm_conv_b', 'a_log',
            'dt_bias', 'd_skip', 'ssm_norm_g', 'w_b_out', 'w_o', 'ln1_g', 'ln1_b', 'w_gate_up', 'w_down', 'ln2_g',
            'ln2_b', 'w_ple', 'ple_norm_g', 'w_ple_gate']
_COL_SHARDED = ['w_in', 'conv_a_w', 'ssm_conv_w', 'w_gate_up', 'w_ple']
_ROW_SHARDED = ['w_a_out', 'w_b_out', 'w_o', 'w_down', 'w_ple_gate']
_BIG = _COL_SHARDED + _ROW_SHARDED
_SMALL = [n for n in _WEIGHTS if n not in _BIG]
_CONV = ['conv_a_w', 'ssm_conv_w']


def _ceil_to(n, k):
    return -(-n // k) * k


def kernel(x, p, w_in, conv_a_w, conv_a_b, ln_a_g, ln_a_b, w_a_out, ssm_conv_w, ssm_conv_b, a_log, dt_bias, d_skip, ssm_norm_g, w_b_out, w_o, ln1_g, ln1_b, w_gate_up, w_down, ln2_g, ln2_b, w_ple, ple_norm_g, w_ple_gate, loss_target, m_w_in, m_conv_a_w, m_conv_a_b, m_ln_a_g, m_ln_a_b, m_w_a_out, m_ssm_conv_w, m_ssm_conv_b, m_a_log, m_dt_bias, m_d_skip, m_ssm_norm_g, m_w_b_out, m_w_o, m_ln1_g, m_ln1_b, m_w_gate_up, m_w_down, m_ln2_g, m_ln2_b, m_w_ple, m_ple_norm_g, m_w_ple_gate, v_w_in, v_conv_a_w, v_conv_a_b, v_ln_a_g, v_ln_a_b, v_w_a_out, v_ssm_conv_w, v_ssm_conv_b, v_a_log, v_dt_bias, v_d_skip, v_ssm_norm_g, v_w_b_out, v_w_o, v_ln1_g, v_ln1_b, v_w_gate_up, v_w_down, v_ln2_g, v_ln2_b, v_w_ple, v_ple_norm_g, v_w_ple_gate):
    wt = dict(w_in=w_in, conv_a_w=conv_a_w, conv_a_b=conv_a_b, ln_a_g=ln_a_g, ln_a_b=ln_a_b, w_a_out=w_a_out,
              ssm_conv_w=ssm_conv_w, ssm_conv_b=ssm_conv_b, a_log=a_log, dt_bias=dt_bias, d_skip=d_skip,
              ssm_norm_g=ssm_norm_g, w_b_out=w_b_out, w_o=w_o, ln1_g=ln1_g, ln1_b=ln1_b, w_gate_up=w_gate_up,
              w_down=w_down, ln2_g=ln2_g, ln2_b=ln2_b, w_ple=w_ple, ple_norm_g=ple_norm_g, w_ple_gate=w_ple_gate)
    mo = dict(w_in=m_w_in, conv_a_w=m_conv_a_w, conv_a_b=m_conv_a_b, ln_a_g=m_ln_a_g, ln_a_b=m_ln_a_b,
              w_a_out=m_w_a_out, ssm_conv_w=m_ssm_conv_w, ssm_conv_b=m_ssm_conv_b, a_log=m_a_log,
              dt_bias=m_dt_bias, d_skip=m_d_skip, ssm_norm_g=m_ssm_norm_g, w_b_out=m_w_b_out, w_o=m_w_o,
              ln1_g=m_ln1_g, ln1_b=m_ln1_b, w_gate_up=m_w_gate_up, w_down=m_w_down, ln2_g=m_ln2_g, ln2_b=m_ln2_b,
              w_ple=m_w_ple, ple_norm_g=m_ple_norm_g, w_ple_gate=m_w_ple_gate)
    vo = dict(w_in=v_w_in, conv_a_w=v_conv_a_w, conv_a_b=v_conv_a_b, ln_a_g=v_ln_a_g, ln_a_b=v_ln_a_b,
              w_a_out=v_w_a_out, ssm_conv_w=v_ssm_conv_w, ssm_conv_b=v_ssm_conv_b, a_log=v_a_log,
              dt_bias=v_dt_bias, d_skip=v_d_skip, ssm_norm_g=v_ssm_norm_g, w_b_out=v_w_b_out, w_o=v_w_o,
              ln1_g=v_ln1_g, ln1_b=v_ln1_b, w_gate_up=v_w_gate_up, w_down=v_w_down, ln2_g=v_ln2_g, ln2_b=v_ln2_b,
              w_ple=v_w_ple, ple_norm_g=v_ple_norm_g, w_ple_gate=v_w_ple_gate)

    L = w_in.shape[0]
    S, D = x.shape[1], x.shape[2]
    CD = conv_a_b.shape[1]
    DI = ssm_norm_g.shape[1]
    XBC = ssm_conv_b.shape[1]
    H = d_skip.shape[1]
    G = (XBC - DI) // (2 * D_STATE)
    F = w_down.shape[1] * 4
    N_IN = w_in.shape[2] * 4
    NM = N_IN - 2 * H
    KA, KB = conv_a_w.shape[1], ssm_conv_w.shape[1]
    assert DI == H * HEAD_DIM and CD == D and DI == 2 * D and XBC == 2 * DI and NM == 2 * CD + 2 * D + DI + XBC
    assert 2 * H <= LANES and S % CHUNK == 0
    tnf = F // 2
    cf = dict(S=S, D=D, CD=CD, DI=DI, XBC=XBC, F=F, H=H, G=G, NM=NM, KA=KA, KB=KB, GW=(H // G) * HEAD_DIM,
              alpha=float((2 * L) ** 0.25), tm=min(512, S), tmx=min(1024, S), tmc=min(256, S), tmr=min(256, S), tmw=min(1024, S),
              tn_in=D, tnf=tnf, nk_f=1, nk_in=NM // DI)

    core = lax.axis_index("c").astype(jnp.int32).reshape(1)
    split_names = [n for n in _BIG if n not in _CONV]

    def layer_weights(l, got):
        full = {}
        for n, g in zip(split_names + _CONV, got):
            if n in _COL_SHARDED:
                full[n] = g.transpose(1, 0, 2).reshape(g.shape[1], 4 * g.shape[2])
            else:
                full[n] = g.reshape(4 * g.shape[1], g.shape[2])
        win = full['w_in']
        in_main = win[:, :NM]
        in_dt = _pad_lanes(win[:, NM:])
        gu = full['w_gate_up']
        W = dict(in_main=in_main, in_dt=in_dt, in_main_T=in_main.T, in_dt_T=in_dt.T,
                 a_out=full['w_a_out'], a_out_T=full['w_a_out'].T,
                 b_out=full['w_b_out'], b_out_T=full['w_b_out'].T,
                 o=full['w_o'], o_T=full['w_o'].T, gate_up=gu, gate_T=gu[:, :F].T, up_T=gu[:, F:].T,
                 down=full['w_down'], down_T=full['w_down'].T, ple=full['w_ple'],
                 ple_gate=full['w_ple_gate'], ple_gate_T=full['w_ple_gate'].T)
        row = lambda v: v.reshape(1, -1)
        head_table = lambda v: jnp.broadcast_to(jnp.pad(v, (0, LANES - H))[:, None], (LANES, LANES))
        sm = dict(conv_a_w=jnp.pad(full['conv_a_w'], ((0, _ceil_to(KA, SUBLANES) - KA), (0, 0))),
                  ssm_conv_w=jnp.pad(full['ssm_conv_w'], ((0, _ceil_to(KB, SUBLANES) - KB), (0, 0))),
                  conv_a_b=row(conv_a_b[l]), ln_a_g=row(ln_a_g[l]), ln_a_b=row(ln_a_b[l]),
                  ssm_conv_b=row(ssm_conv_b[l]), ssm_norm_g=row(ssm_norm_g[l]),
                  ln1_g=row(ln1_g[l]), ln1_b=row(ln1_b[l]), ln2_g=row(ln2_g[l]), ln2_b=row(ln2_b[l]),
                  ple_norm_g=row(ple_norm_g[l]),
                  dtb_f=head_table(dt_bias[l, 0]), dtb_r=head_table(dt_bias[l, 1]),
                  alog_f=head_table(a_log[l, 0]), alog_r=head_table(a_log[l, 1]),
                  dskip_full=row(jnp.repeat(d_skip[l], HEAD_DIM)),
                  dskipT=jnp.broadcast_to(jnp.repeat(d_skip[l], HEAD_DIM)[:, None], (DI, LANES)))
        return W, sm

    def blocks(n, gl):
        g = gl[n]
        if n == 'conv_a_w':
            g = g.sum(axis=1)[:KA]
        elif n == 'ssm_conv_w':
            g = g.sum(axis=1)[:KB]
        if n in _COL_SHARDED:
            return g.reshape(g.shape[0], 4, g.shape[1] // 4).transpose(1, 0, 2)
        return g.reshape(4, g.shape[0] // 4, g.shape[1])

    def core_sums(gl):
        mine = [blocks(n, gl) for n in split_names]
        theirs = core_send_half("core_send_half", mine)
        return [core_sum("core_sum_" + n, core, b, t) for n, b, t in zip(split_names, mine, theirs)]

    def chip_sums(l, parts, acc):
        sums = [chip_sum_into("chip_sum_" + n, core, pr, l, L, into=acc.get(n)) for n, pr in zip(split_names, parts)]
        return dict(zip(split_names, core_fill("core_fill", sums, l, L)))

    def shards(l):
        return [wt[n][l].astype(BF16) for n in split_names]

    lw = [None] * L
    pending = None
    for l in range(L):
        if l < L - 1 or L == 1:
            lw[l] = layer_weights(l, gather_layer("gather_weights", shards(l), [wt[n][l] for n in _CONV]))
    xl = x[0]
    if L > 1:
        sh = shards(L - 1)
        send, recv, sh, lands, token = chip_legs_start(
            "gather_start", 'gather', sh, [lax.empty((4,) + a.shape, a.dtype) for a in sh])
        pending = (send, recv, sh, lands)
        xlb = (xl + token[0, 0]).astype(BF16)
    else:
        xlb = xl.astype(BF16)
    saved = []
    for l in range(L):
        if l == L - 1 and pending is not None:
            send, recv, sh, lands = pending
            landed = chip_legs_wait("gather_wait", 'gather', send, recv, sh, lands, xl)
            conv_got = chip_exchange("gather_conv", [[wt[n][l]] for n in _CONV], gather=True)
            lw[l] = layer_weights(l, list(gather_finish("gather_finish", sh, landed)) + list(conv_got))
        xl, xlb, sv = _layer_fwd(cf, xl, xlb, p[l, 0].astype(BF16), lw[l][0], lw[l][1],
                                 target=loss_target[0] if l == L - 1 else None)
        saved.append(sv)
    grads = [None] * L
    dxl = None
    gsum = {}
    pending = None
    for l in reversed(range(L)):
        sm_l = lw[l][1]
        if pending is not None:
            sm_l = dict(sm_l, ple_norm_g=sm_l['ple_norm_g'] + pending[4][0, 0])
        if l == L - 1:
            dxl, grads[l] = _layer_bwd(cf, saved[l], lw[l][0], sm_l)
        else:
            dxl, grads[l] = _layer_bwd(cf, saved[l], lw[l][0], sm_l, dxn=dxl)
        both = core_sums(grads[l])
        if l == L - 1 and L > 1:
            send, recv, both, lands, token = chip_legs_start(
                "scatter_start", 'scatter', both, [lax.empty(a.shape, a.dtype) for a in both])
            pending = (send, recv, both, lands, token)
            continue
        if pending is not None:
            send, recv, sent, lands, _ = pending
            landed = chip_legs_wait("scatter_wait", 'scatter', send, recv, sent, lands, dxl)
            gsum = chip_sums(L - 1, place_own(sent, landed), gsum)
            pending = None
        parts = chip_exchange("scatter_grads", [[t] for t in both], gather=False)
        gsum = chip_sums(l, [pr.reshape(4, pr.shape[2], pr.shape[3]) for pr in parts], gsum)
    loss = lax.psum(0.5 / D * jnp.sum(grads[L - 1]['loss_sq']), ("x", "y", "c"))
    grad_x = dxl[None]

    res = {}
    for n in split_names:
        shp = wt[n].shape
        flat = lambda a: a.reshape(shp[0] * shp[1], shp[2])
        outs = adamw_full("adamw_" + n, gsum[n], flat(wt[n]), flat(mo[n]), flat(vo[n]))
        res[n] = [o.reshape(shp) for o in [gsum[n]] + list(outs)]
    parts = chip_exchange("scatter_conv", [[blocks(n, grads[l]) for l in range(L)] for n in _CONV], gather=False)
    chip_sums = [sum_chips("chip_sum_" + n, pr.reshape(4, L * pr.shape[2], pr.shape[3])) for n, pr in zip(_CONV, parts)]
    sib_sums = sibling_swap("core_swap", chip_sums)
    for n, mine, sib in zip(_CONV, chip_sums, sib_sums):
        shp = wt[n].shape
        flat = lambda a: a.reshape(shp[0] * shp[1], shp[2])
        outs = adamw_shard("adamw_" + n, mine, sib, flat(wt[n]), flat(mo[n]), flat(vo[n]))
        res[n] = [o.reshape(shp) for o in outs]

    def small_pieces(l):
        gl = grads[l]
        A = -jnp.exp(a_log[l])
        d = dict(gl)
        d_alog = jnp.concatenate([gl['dA_f'].sum(axis=1)[:H] * A[0], gl['dA_r'].sum(axis=1)[:H] * A[1]])
        d['a_log'] = jnp.pad(d_alog[None], ((0, SUBLANES - 1), (0, 0)))
        d['dt_bias'] = gl['dt_bias'][:, :2 * H]
        d['d_skip'] = gl['dskip_full'].reshape(SUBLANES, H, HEAD_DIM).sum(axis=-1)
        return [_pad_lanes(d[n], _ceil_to(d[n].shape[1], LANES)) for n in _SMALL]

    widths = [_ceil_to(math.prod(wt[n].shape[1:]), LANES) for n in _SMALL]
    packed = jnp.concatenate([pc for l in range(L) for pc in small_pieces(l)], axis=1)
    gathered = all8_gather("gather_small", fold_rows("fold_small", packed))

    def pack_params(src):
        return jnp.concatenate([_pad_lanes(src[n][l].reshape(1, -1), wd) for l in range(L) for n, wd in zip(_SMALL, widths)],
                               axis=1)

    small_out = adamw_small("adamw_small", gathered, pack_params(wt), pack_params(mo), pack_params(vo))
    off = 0
    per = {n: [[] for _ in range(4)] for n in _SMALL}
    for l in range(L):
        for n, wd in zip(_SMALL, widths):
            size = math.prod(wt[n].shape[1:])
            for k in range(4):
                per[n][k].append(small_out[k][0, off:off + size].reshape(wt[n].shape[1:]))
            off += wd
    for n in _SMALL:
        res[n] = [jnp.stack(per[n][k]) for k in range(4)]

    return (loss, grad_x, *[res[n][0] for n in _WEIGHTS], *[res[n][1] for n in _WEIGHTS],
            *[res[n][2] for n in _WEIGHTS], *[res[n][3] for n in _WEIGHTS])
```

```python
import math

import jax
import jax.numpy as jnp
from jax import lax
from jax.experimental import pallas as pl
from jax.experimental.pallas import tpu as pltpu

F32 = jnp.float32
BF16 = jnp.bfloat16

VMEM_LIMIT_BYTES = 56 * 1024 * 1024
LANES = 128
SUBLANES = 8

CHUNK = 128
D_STATE = 128
HEAD_DIM = 64
LN_EPS = 1e-5
RMS_EPS = 1e-6
ADAM_LR = 0.001
ADAM_B1 = 0.9
ADAM_B2 = 0.999
ADAM_EPS = 1e-08
ADAM_WD = 0.01
ADAM_STEP = 10
HALO = 16
MESH = pl.DeviceIdType.MESH


def _params(**kw):
    return pltpu.CompilerParams(vmem_limit_bytes=VMEM_LIMIT_BYTES, **kw)


def _sig(x):
    return jax.nn.sigmoid(x)


def _dsilu(x, s):
    return s * (1.0 + x * (1.0 - s))


def _ln_stats(r):
    mu = jnp.mean(r, axis=-1, keepdims=True)
    xc = r - mu
    var = jnp.mean(xc * xc, axis=-1, keepdims=True)
    rstd = lax.rsqrt(var + LN_EPS)
    return xc * rstd, rstd


def _ln_bwd(dy, xhat, rstd, g):
    dxh = dy * g
    m1 = jnp.mean(dxh, axis=-1, keepdims=True)
    m2 = jnp.mean(dxh * xhat, axis=-1, keepdims=True)
    return rstd * (dxh - m1 - xhat * m2)


def _f32(v):
    return v if v.dtype == F32 else v.astype(F32)


def _rows8(v):
    tm, w = v.shape
    return v.reshape(tm // SUBLANES, SUBLANES, w).sum(axis=0)


def fused_mm(name, prods, extras, epi, row_outs, col_outs=(), *, M, tm, tn, nj=1, nk=1,
             passthrough=None, t_outs=()):
    np_ = len(prods)
    ne = len(extras)
    nro = len(row_outs)
    nco = len(col_outs)
    use_acc = nk > 1

    def body(*refs):
        a_refs = [refs[2 * p] for p in range(np_)]
        w_refs = [refs[2 * p + 1] for p in range(np_)]
        pos = 2 * np_
        e_refs = refs[pos:pos + ne]
        pos += ne
        if passthrough is not None:
            pos += 1
        ro_refs = refs[pos:pos + nro]
        pos += nro
        co_refs = refs[pos:pos + nco]
        pos += nco
        to_refs = refs[pos:pos + len(t_outs)]
        pos += len(t_outs)
        acc_ref = refs[pos] if use_acc else None
        i = pl.program_id(1)
        k = pl.program_id(2)

        def prod(p):
            a = a_refs[p][...]
            if a.dtype != BF16:
                a = a.astype(BF16)
            return jnp.dot(a, w_refs[p][...], preferred_element_type=F32)

        def finish(acc):
            res = epi(acc, [_f32(r[...]) for r in e_refs])
            rows, cols = res[0], res[1]
            for v, o in zip(rows, ro_refs):
                o[...] = v.astype(o.dtype)
            for v, o in zip(res[2] if len(res) > 2 else (), to_refs):
                o[...] = v.T.astype(o.dtype)
            for v, o in zip(cols, co_refs):
                v8 = _rows8(v)

                @pl.when(i == 0)
                def _():
                    o[...] = v8

                @pl.when(i > 0)
                def _():
                    o[...] += v8

        if not use_acc:
            acc = prod(0)
            for p in range(1, np_):
                acc = acc + prod(p)
            finish(acc)
        else:
            @pl.when(k == 0)
            def _():
                acc = None
                for p in range(np_):
                    acc = prod(p) if acc is None else acc + prod(p)
                acc_ref[...] = acc

            @pl.when(k > 0)
            def _():
                acc = None
                for p in range(np_):
                    if prods[p][3]:
                        acc = prod(p) if acc is None else acc + prod(p)
                acc_ref[...] += acc

            @pl.when(k == nk - 1)
            def _():
                finish(acc_ref[...])

    in_specs = []
    args = []
    for a, w, joff, ksplit in prods:
        K = a.shape[1]
        if ksplit:
            tk = K // nk
            in_specs.append(pl.BlockSpec((tm, tk), lambda j, i, k: (i, k)))
            in_specs.append(pl.BlockSpec((tk, tn), lambda j, i, k, joff=joff: (k, j + joff)))
        else:
            in_specs.append(pl.BlockSpec((tm, K), lambda j, i, k: (i, 0)))
            in_specs.append(pl.BlockSpec((K, tn), lambda j, i, k, joff=joff: (0, j + joff)))
        args += [a, w]
    for arr, kind, width, c0 in extras:
        if kind == 'row':
            in_specs.append(pl.BlockSpec((tm, width), lambda j, i, k, c0=c0: (i, c0 + j)))
        else:
            in_specs.append(pl.BlockSpec((arr.shape[0], width), lambda j, i, k, c0=c0: (0, c0 + j)))
        args.append(arr)
    aliases = {}
    if passthrough is not None:
        arr, oidx = passthrough
        in_specs.append(pl.BlockSpec(memory_space=pl.ANY))
        aliases = {len(args): oidx}
        args.append(arr)
    out_shape = []
    out_specs = []
    for n_total, dtype, width, c0 in row_outs:
        out_shape.append(jax.ShapeDtypeStruct((M, n_total), dtype))
        out_specs.append(pl.BlockSpec((tm, width), lambda j, i, k, c0=c0: (i, c0 + j)))
    for n_total, width, c0 in col_outs:
        out_shape.append(jax.ShapeDtypeStruct((SUBLANES, n_total), F32))
        out_specs.append(pl.BlockSpec((SUBLANES, width), lambda j, i, k, c0=c0: (0, c0 + j)))
    for n_total, dtype, width, c0 in t_outs:
        out_shape.append(jax.ShapeDtypeStruct((n_total, M), dtype))
        out_specs.append(pl.BlockSpec((width, tm), lambda j, i, k, c0=c0: (c0 + j, i)))
    scratch = [pltpu.VMEM((tm, tn), F32)] if use_acc else []
    return pl.pallas_call(
        body, name=name, grid=(nj, M // tm, nk), in_specs=in_specs, out_specs=out_specs,
        out_shape=out_shape, scratch_shapes=scratch, input_output_aliases=aliases,
        compiler_params=_params(dimension_semantics=("arbitrary", "arbitrary", "arbitrary")),
    )(*args)


def mm_tn(name, a, b, *, tm, tk, tn):
    M, K = a.shape
    N = b.shape[1]

    def body(a_ref, b_ref, o_ref):
        m = pl.program_id(2)
        p = lax.dot_general(a_ref[...], b_ref[...], (((0,), (0,)), ((), ())),
                            preferred_element_type=F32)

        @pl.when(m == 0)
        def _():
            o_ref[...] = p

        @pl.when(m > 0)
        def _():
            o_ref[...] += p

    return pl.pallas_call(
        body, name=name, grid=(K // tk, N // tn, M // tm),
        in_specs=[pl.BlockSpec((tm, tk), lambda kk, j, m: (m, kk)),
                  pl.BlockSpec((tm, tn), lambda kk, j, m: (m, j))],
        out_specs=pl.BlockSpec((tk, tn), lambda kk, j, m: (kk, j)),
        out_shape=jax.ShapeDtypeStruct((K, N), F32),
        compiler_params=_params(dimension_semantics=("arbitrary", "arbitrary", "arbitrary")),
    )(a, b)


def row_call(name, fn, ins, row_outs, col_outs=(), *, M, tm, nc=1):
    ni = len(ins)
    nro = len(row_outs)

    def body(*refs):
        i = pl.program_id(1)
        vals = [_f32(r[...]) for r in refs[:ni]]
        rows, cols = fn(*vals)
        for v, o in zip(rows, refs[ni:ni + nro]):
            o[...] = v.astype(o.dtype)
        for v, o in zip(cols, refs[ni + nro:]):
            v8 = _rows8(v)

            @pl.when(i == 0)
            def _():
                o[...] = v8

            @pl.when(i > 0)
            def _():
                o[...] += v8

    in_specs = []
    for arr, kind, width, c0, cmul in ins:
        if kind == 'row':
            in_specs.append(pl.BlockSpec((tm, width), lambda cj, i, c0=c0, cmul=cmul: (i, c0 + cmul * cj)))
        else:
            in_specs.append(pl.BlockSpec((arr.shape[0], width), lambda cj, i, c0=c0, cmul=cmul: (0, c0 + cmul * cj)))
    out_shape = []
    out_specs = []
    for n_total, dtype, width, c0, cmul in row_outs:
        out_shape.append(jax.ShapeDtypeStruct((M, n_total), dtype))
        out_specs.append(pl.BlockSpec((tm, width), lambda cj, i, c0=c0, cmul=cmul: (i, c0 + cmul * cj)))
    for n_total, width, c0, cmul in col_outs:
        out_shape.append(jax.ShapeDtypeStruct((SUBLANES, n_total), F32))
        out_specs.append(pl.BlockSpec((SUBLANES, width), lambda cj, i, c0=c0, cmul=cmul: (0, c0 + cmul * cj)))
    return pl.pallas_call(
        body, name=name, grid=(nc, M // tm), in_specs=in_specs, out_specs=out_specs,
        out_shape=out_shape,
        compiler_params=_params(dimension_semantics=("arbitrary", "arbitrary")),
    )(*[a[0] for a in ins])


def conv_call(name, src, src_c0, w, K, epi, extras, row_outs, col_outs=(), *, M, tm, cw, nc,
              reverse, xin=None, passthrough=None, t_outs=(), w_c0=0):
    pad = (K - 1) // 2
    assert pad <= HALO - 1
    R = tm // HALO
    nblk = M // HALO
    n_i = M // tm
    Kp = w.shape[0]
    ne = len(extras)
    nro = len(row_outs)
    nco = len(col_outs)
    rb = 64
    cbw = min(cw, 256)
    n_copies = SUBLANES if K > SUBLANES else 1

    def body(*refs):
        main_ref, prev_ref, next_ref, w_ref = refs[:4]
        pos = 4
        xin_ref = None
        if xin is not None:
            xin_ref = refs[pos]
            pos += 1
        e_refs = refs[pos:pos + ne]
        pos += ne
        if passthrough is not None:
            pos += 1
        ro_refs = refs[pos:pos + nro]
        pos += nro
        co_refs = refs[pos:pos + nco]
        pos += nco
        to_refs = refs[pos:pos + len(t_outs)]
        pos += len(t_outs)
        dw_ref = None
        if xin is not None:
            dw_ref = refs[pos]
            pos += 1
        ext_ref, conv_ref = refs[pos], refs[pos + 1]
        i = pl.program_id(1)

        ext_ref[0, 0:HALO, :] = jnp.where(i == 0, 0.0, prev_ref[...].astype(F32))
        ext_ref[0, HALO:HALO + tm, :] = main_ref[...].astype(F32)
        ext_ref[0, HALO + tm:, :] = jnp.where(i == n_i - 1, 0.0, next_ref[...].astype(F32))
        if dw_ref is not None:
            @pl.when(i == 0)
            def _():
                dw_ref[...] = jnp.zeros_like(dw_ref)

        n_sh = tm + 2 * HALO - SUBLANES
        for c0 in range(0, cw, cbw):
            for sft in range(1, n_copies):
                ext_ref[sft, 0:n_sh, c0:c0 + cbw] = ext_ref[0, sft:sft + n_sh, c0:c0 + cbw]

        for c0 in range(0, cw, cbw):
            for r0 in range(0, tm, rb):
                acc = jnp.zeros((rb, cbw), F32)
                if xin_ref is not None:
                    xblk = xin_ref[r0:r0 + rb, c0:c0 + cbw].astype(F32)
                for k in range(K):
                    off = HALO + r0 + ((pad - k) if reverse else (k - pad))
                    sft = off % SUBLANES if n_copies > 1 else 0
                    d = ext_ref[sft, off - sft:off - sft + rb, c0:c0 + cbw]
                    acc = acc + d * w_ref[k:k + 1, c0:c0 + cbw]
                    if xin_ref is not None:
                        dw_ref[k, :, c0:c0 + cbw] += _rows8(xblk * d)
                conv_ref[r0:r0 + rb, c0:c0 + cbw] = acc

        res = epi(conv_ref[...], [_f32(r[...]) for r in e_refs])
        rows, cols = res[0], res[1]
        for v, o in zip(rows, ro_refs):
            o[...] = v.astype(o.dtype)
        for v, o in zip(res[2] if len(res) > 2 else (), to_refs):
            o[...] = v.T.astype(o.dtype)
        for v, o in zip(cols, co_refs):
            v8 = _rows8(v)

            @pl.when(i == 0)
            def _():
                o[...] = v8

            @pl.when(i > 0)
            def _():
                o[...] += v8

    in_specs = [
        pl.BlockSpec((tm, cw), lambda cj, i: (i, src_c0 + cj)),
        pl.BlockSpec((HALO, cw), lambda cj, i: (jnp.maximum(i * R - 1, 0), src_c0 + cj)),
        pl.BlockSpec((HALO, cw), lambda cj, i: (jnp.minimum((i + 1) * R, nblk - 1), src_c0 + cj)),
        pl.BlockSpec((Kp, cw), lambda cj, i: (0, w_c0 + cj)),
    ]
    args = [src, src, src, w]
    if xin is not None:
        in_specs.append(pl.BlockSpec((tm, cw), lambda cj, i, c0=xin[1]: (i, c0 + cj)))
        args.append(xin[0])
    for arr, kind, width, c0, cmul in extras:
        if kind == 'row':
            in_specs.append(pl.BlockSpec((tm, width), lambda cj, i, c0=c0, cmul=cmul: (i, c0 + cmul * cj)))
        else:
            in_specs.append(pl.BlockSpec((arr.shape[0], width), lambda cj, i, c0=c0, cmul=cmul: (0, c0 + cmul * cj)))
        args.append(arr)
    aliases = {}
    if passthrough is not None:
        in_specs.append(pl.BlockSpec(memory_space=pl.ANY))
        aliases = {len(args): passthrough[1]}
        args.append(passthrough[0])
    out_shape = []
    out_specs = []
    for n_total, dtype, width, c0, cmul in row_outs:
        out_shape.append(jax.ShapeDtypeStruct((M, n_total), dtype))
        out_specs.append(pl.BlockSpec((tm, width), lambda cj, i, c0=c0, cmul=cmul: (i, c0 + cmul * cj)))
    for n_total, width, c0, cmul in col_outs:
        out_shape.append(jax.ShapeDtypeStruct((SUBLANES, n_total), F32))
        out_specs.append(pl.BlockSpec((SUBLANES, width), lambda cj, i, c0=c0, cmul=cmul: (0, c0 + cmul * cj)))
    for n_total, dtype, width, c0, cmul in t_outs:
        out_shape.append(jax.ShapeDtypeStruct((n_total, M), dtype))
        out_specs.append(pl.BlockSpec((width, tm), lambda cj, i, c0=c0, cmul=cmul: (c0 + cmul * cj, i)))
    if xin is not None:
        out_shape.append(jax.ShapeDtypeStruct((Kp, SUBLANES, cw * nc), F32))
        out_specs.append(pl.BlockSpec((Kp, SUBLANES, cw), lambda cj, i: (0, 0, cj)))
    return pl.pallas_call(
        body, name=name, grid=(nc, n_i), in_specs=in_specs, out_specs=out_specs,
        out_shape=out_shape, input_output_aliases=aliases,
        scratch_shapes=[pltpu.VMEM((n_copies, tm + 2 * HALO, cw), F32), pltpu.VMEM((tm, cw), F32)],
        compiler_params=_params(dimension_semantics=("arbitrary", "arbitrary")),
    )(*args)


def _split_dot(m_bf16, v, n_pass, dims=None):
    out = None
    rest = v
    for p in range(n_pass):
        piece = rest.astype(BF16)
        if p + 1 < n_pass:
            rest = rest - piece.astype(F32)
        if dims is None:
            t = jnp.dot(m_bf16, piece, preferred_element_type=F32)
        else:
            t = lax.dot_general(m_bf16, piece, dims, preferred_element_type=F32)
        out = t if out is None else out + t
    return out


def _split_dot_r(v, m_bf16, n_pass):
    out = None
    rest = v
    for p in range(n_pass):
        piece = rest.astype(BF16)
        if p + 1 < n_pass:
            rest = rest - piece.astype(F32)
        t = jnp.dot(piece, m_bf16, preferred_element_type=F32)
        out = t if out is None else out + t
    return out


def _softplus(x):
    return jnp.maximum(x, 0.0) + jnp.log1p(jnp.exp(-jnp.abs(x)))


NT_DIMS = (((1,), (1,)), ((), ()))
TN_DIMS = (((0,), (0,)), ((), ()))


def _ssd_common(dtraw, dtbT, alogT, rev, n_heads):
    L = CHUNK
    if rev:
        dtraw = pltpu.roll(dtraw, LANES - n_heads, 1)
    preT = dtraw.T + dtbT
    dtT = _softplus(preT)
    AT = -jnp.exp(alogT)
    aT = dtT * AT
    ri = lax.broadcasted_iota(jnp.int32, (L, L), 0)
    ci = lax.broadcasted_iota(jnp.int32, (L, L), 1)
    up = (ri >= ci) if rev else (ri <= ci)
    lo = (ri <= ci) if rev else (ri >= ci)
    csT = _split_dot_r(aT, up.astype(BF16), 3)
    last = 0 if rev else L - 1
    lastB = jnp.broadcast_to(csT[:, last:last + 1], (L, L))
    return dict(preT=preT, dtT=dtT, AT=AT, csT=csT, cs=csT.T, up=up, lo=lo, ci=ci, last=last,
                doutT=jnp.exp(csT), dstT=jnp.exp(lastB - csT), totB=jnp.exp(lastB))


def ssd_fwd(name, xsT, bc, dtraw, dtbT, alogT, *, S, DI, G, H, rev, tail=None):
    NC = S // CHUNK
    R = H // G
    GW = R * HEAD_DIM
    N = D_STATE
    P = HEAD_DIM

    def body(*refs):
        xsT_ref, bc_ref, dtraw_ref, dtb_ref, alog_ref = refs[:5]
        if tail is None:
            y_ref, st_ref, h_ref = refs[5:]
        else:
            yo_ref, z_ref, xs_ref, dsk_ref, ng_ref = refs[5:10]
            y_ref, st_ref, yn_ref, h_ref = refs[10:]
        c = pl.program_id(0)

        @pl.when(c == 0)
        def _():
            h_ref[...] = jnp.zeros_like(h_ref)

        q = _ssd_common(dtraw_ref[...], dtb_ref[...], alog_ref[...], rev, H)
        cs, csT, dtT, doutT, totB = q['cs'], q['csT'], q['dtT'], q['doutT'], q['totB']
        wstT = q['dstT'] * dtT
        GB = 2 if G % 2 == 0 else 1
        for g0 in range(0, G, GB):
            gs = list(range(g0, g0 + GB))
            Bgs = [bc_ref[:, g * N:(g + 1) * N].astype(BF16) for g in gs]
            Cgs = [bc_ref[:, G * N + g * N:G * N + (g + 1) * N].astype(BF16) for g in gs]
            CBTs = [lax.dot_general(b, c_, NT_DIMS, preferred_element_type=F32) for b, c_ in zip(Bgs, Cgs)]
            HTs = [h_ref[g] for g in gs]
            yoffTs = [lax.dot_general(HT.astype(BF16), c_, NT_DIMS, preferred_element_type=F32)
                      for HT, c_ in zip(HTs, Cgs)]
            xTs = [xsT_ref[g * GW:(g + 1) * GW, :] for g in gs]
            heads = [(k, r) for k in range(GB) for r in range(R)]
            hs = [gs[k] * R + r for k, r in heads]
            blks = [slice(r * P, (r + 1) * P) for _, r in heads]
            segs = [jnp.where(q['up'], csT[h:h + 1, :] - cs[:, h:h + 1], -1e30) for h in hs]
            GTs = [(CBTs[k] * jnp.exp(sg)).astype(BF16) for (k, _), sg in zip(heads, segs)]
            xThs = [xTs[k][b, :] for (k, _), b in zip(heads, blks)]
            XThs = [(xTh * dtT[h:h + 1, :]).astype(BF16) for xTh, h in zip(xThs, hs)]
            ydTs = [jnp.dot(a, GT, preferred_element_type=F32) for a, GT in zip(XThs, GTs)]
            ys = [ydT + yoffTs[k][b, :] * doutT[h:h + 1, :] for ydT, (k, _), b, h in zip(ydTs, heads, blks, hs)]
            xws = [xTh * wstT[h:h + 1, :] for xTh, h in zip(xThs, hs)]
            tots = [jnp.broadcast_to(totB[h:h + 1, :], (P, N)) for h in hs]
            for k, g in enumerate(gs):
                sel = slice(k * R, (k + 1) * R)
                y_ref[:, g * GW:(g + 1) * GW] = jnp.concatenate(ys[sel], axis=0).T
                xwT = jnp.concatenate(xws[sel], axis=0).astype(BF16)
                ST = jnp.dot(xwT, Bgs[k], preferred_element_type=F32)
                st_ref[0, g] = HTs[k]
                h_ref[g] = HTs[k] * jnp.concatenate(tots[sel], axis=0) + ST
        if tail is not None:
            y = y_ref[...] + yo_ref[...]
            y_ref[...] = y
            z = _f32(z_ref[...])
            yz = (y + xs_ref[...] * dsk_ref[...]) * (z * _sig(z))
            for g in range(G):
                t = yz[:, g * GW:(g + 1) * GW]
                tn = t * lax.rsqrt(jnp.mean(t * t, axis=-1, keepdims=True) + RMS_EPS)
                yn_ref[:, g * GW:(g + 1) * GW] = (tn * ng_ref[:, g * GW:(g + 1) * GW]).astype(BF16)

    cidx = (lambda c: NC - 1 - c) if rev else (lambda c: c)
    cmap = lambda c: (cidx(c), 0)
    smap = lambda c: (cidx(c), 0, 0, 0)
    const = lambda c: (0, 0)
    tmap = lambda c: (0, cidx(c))
    in_specs = [pl.BlockSpec((DI, CHUNK), tmap), pl.BlockSpec((CHUNK, 2 * G * N), cmap), pl.BlockSpec((CHUNK, LANES), cmap),
                pl.BlockSpec((LANES, LANES), const), pl.BlockSpec((LANES, LANES), const)]
    out_specs = [pl.BlockSpec((CHUNK, DI), cmap), pl.BlockSpec((1, G, GW, N), smap)]
    out_shape = [jax.ShapeDtypeStruct((S, DI), F32), jax.ShapeDtypeStruct((NC, G, GW, N), F32)]
    args = [xsT, bc, dtraw, dtbT, alogT]
    if tail is not None:
        y_other, (z_arr, z_blk), xs_row, dsk, ng = tail
        in_specs += [pl.BlockSpec((CHUNK, DI), cmap), pl.BlockSpec((CHUNK, DI), lambda c: (cidx(c), z_blk)),
                     pl.BlockSpec((CHUNK, DI), cmap), pl.BlockSpec((1, DI), const), pl.BlockSpec((1, DI), const)]
        out_specs.append(pl.BlockSpec((CHUNK, DI), cmap))
        out_shape.append(jax.ShapeDtypeStruct((S, DI), BF16))
        args += [y_other, z_arr, xs_row, dsk, ng]
    return pl.pallas_call(
        body, name=name, grid=(NC,), in_specs=in_specs, out_specs=out_specs, out_shape=out_shape,
        scratch_shapes=[pltpu.VMEM((G, GW, N), F32)],
        compiler_params=_params(dimension_semantics=("arbitrary",)),
    )(*args)


def ssd_bwd(name, xsT, bc, dtraw, dyT, st, dtbT, alogT, *, S, DI, G, H, rev, tail=None):
    NC = S // CHUNK
    R = H // G
    GW = R * HEAD_DIM
    N = D_STATE
    XBC = DI + 2 * G * N
    P = HEAD_DIM
    L = CHUNK

    def body(*refs):
        xsT_ref, bc_ref, dtraw_ref, dyT_ref, st_ref, dtb_ref, alog_ref = refs[:7]
        if tail is None:
            dxbc_ref, ddt_ref, da_ref, dh_ref, dcst_ref, p2t_ref, p3t_ref, e2t_ref = refs[7:]
        else:
            other_ref, cbx_ref, cbbc_ref, dskT_ref = refs[7:11]
            dxbc_ref, ddt_ref, da_ref, dcol_ref, dh_ref, dcst_ref, p2t_ref, p3t_ref, e2t_ref = refs[11:]
        c = pl.program_id(0)

        @pl.when(c == 0)
        def _():
            dh_ref[...] = jnp.zeros_like(dh_ref)
            da_ref[...] = jnp.zeros_like(da_ref)
            dcst_ref[...] = jnp.zeros_like(dcst_ref)
            p2t_ref[...] = jnp.zeros_like(p2t_ref)
            p3t_ref[...] = jnp.zeros_like(p3t_ref)
            e2t_ref[...] = jnp.zeros_like(e2t_ref)

        q = _ssd_common(dtraw_ref[...], dtb_ref[...], alog_ref[...], rev, H)
        cs, csT, dtT, doutT, dstT, totB = q['cs'], q['csT'], q['dtT'], q['doutT'], q['dstT'], q['totB']
        wstT = dstT * dtT
        lane = q['ci']
        GB = 2 if G % 2 == 0 else 1
        for g0 in range(0, G, GB):
            gs = list(range(g0, g0 + GB))
            Bgs = [bc_ref[:, g * N:(g + 1) * N].astype(BF16) for g in gs]
            Cgs = [bc_ref[:, G * N + g * N:G * N + (g + 1) * N].astype(BF16) for g in gs]
            CBs = [lax.dot_general(c_, b, NT_DIMS, preferred_element_type=F32) for b, c_ in zip(Bgs, Cgs)]
            HpTs = [st_ref[0, g] for g in gs]
            HpTbs = [v.astype(BF16) for v in HpTs]
            dHTs = [dh_ref[g] for g in gs]
            dHTbs = [v.astype(BF16) for v in dHTs]
            BdHTs = [lax.dot_general(d, b, NT_DIMS, preferred_element_type=F32) for d, b in zip(dHTbs, Bgs)]
            yoffTs = [lax.dot_general(hp, c_, NT_DIMS, preferred_element_type=F32) for hp, c_ in zip(HpTbs, Cgs)]
            xTs = [xsT_ref[g * GW:(g + 1) * GW, :] for g in gs]
            dyTs = [dyT_ref[g * GW:(g + 1) * GW, :] for g in gs]
            heads = [(k, r) for k in range(GB) for r in range(R)]
            ks = [k for k, _ in heads]
            hs = [gs[k] * R + r for k, r in heads]
            blks = [slice(r * P, (r + 1) * P) for _, r in heads]
            Lms = [jnp.exp(jnp.where(q['lo'], cs[:, h:h + 1] - csT[h:h + 1, :], -1e30)) for h in hs]
            xThs = [xTs[k][b, :] for k, b in zip(ks, blks)]
            dyThs = [dyTs[k][b, :] for k, b in zip(ks, blks)]
            xThbs = [v.astype(BF16) for v in xThs]
            dyThbs = [v.astype(BF16) for v in dyThs]
            dGxs = [lax.dot_general(a, b, TN_DIMS, preferred_element_type=F32) for a, b in zip(dyThbs, xThbs)]
            Gms = [(CBs[k] * Lm).astype(BF16) for k, Lm in zip(ks, Lms)]
            XThbs = [(xTh * dtT[h:h + 1, :]).astype(BF16) for xTh, h in zip(xThs, hs)]
            u1Ts = [jnp.dot(a, Gm, preferred_element_type=F32) for a, Gm in zip(dyThbs, Gms)]
            ydTs = [lax.dot_general(a, Gm, NT_DIMS, preferred_element_type=F32) for a, Gm in zip(XThbs, Gms)]
            Ts = [dGx * (Lm * dtT[h:h + 1, :]) for dGx, Lm, h in zip(dGxs, Lms, hs)]
            uTs = [u1T + BdHTs[k][b, :] * dstT[h:h + 1, :] for u1T, k, b, h in zip(u1Ts, ks, blks, hs)]
            dyds = [dyTh * doutT[h:h + 1, :] for dyTh, h in zip(dyThs, hs)]
            xws = [xTh * wstT[h:h + 1, :] for xTh, h in zip(xThs, hs)]
            for i, h in enumerate(hs):
                k, b = ks[i], blks[i]
                p3row = jnp.sum(xws[i] * BdHTs[k][b, :], axis=0, keepdims=True)
                seg_row = jnp.sum(_f32(dyThbs[i]) * ydTs[i], axis=0, keepdims=True)
                seg_col = jnp.sum(_f32(XThbs[i]) * u1Ts[i], axis=0, keepdims=True)
                dcst_ref[h:h + 1, :] = (jnp.sum(dyds[i] * yoffTs[k][b, :], axis=0, keepdims=True)
                                        + seg_row - seg_col - p3row)
                p2t_ref[h:h + 1, :] = jnp.sum(xThs[i] * uTs[i], axis=0, keepdims=True)
                p3t_ref[h:h + 1, :] = p3row
                e2t_ref[h:h + 1, :] = jnp.sum(HpTs[k][b, :] * dHTs[k][b, :], axis=0, keepdims=True)
            dxs = [uT * dtT[h:h + 1, :] for uT, h in zip(uTs, hs)]
            if tail is not None:
                dxs = [d + dyTh * dskT_ref[h * P:(h + 1) * P, :] for d, dyTh, h in zip(dxs, dyThs, hs)]
            tots = [jnp.broadcast_to(totB[h:h + 1, :], (P, N)) for h in hs]
            for k, g in enumerate(gs):
                sel = slice(k * R, (k + 1) * R)
                dCB = Ts[k * R]
                for T in Ts[k * R + 1:(k + 1) * R]:
                    dCB = dCB + T
                dxbc_ref[:, g * GW:(g + 1) * GW] = jnp.concatenate(dxs[sel], axis=0).T
                dydT = jnp.concatenate(dyds[sel], axis=0).astype(BF16)
                xwT = jnp.concatenate(xws[sel], axis=0).astype(BF16)
                dCBb = dCB.astype(BF16)
                dC = (jnp.dot(dCBb, Bgs[k], preferred_element_type=F32)
                      + lax.dot_general(dydT, HpTbs[k], TN_DIMS, preferred_element_type=F32))
                dB = (lax.dot_general(dCBb, Cgs[k], TN_DIMS, preferred_element_type=F32)
                      + lax.dot_general(xwT, dHTbs[k], TN_DIMS, preferred_element_type=F32))
                dxbc_ref[:, DI + g * N:DI + (g + 1) * N] = dB
                dxbc_ref[:, DI + G * N + g * N:DI + G * N + (g + 1) * N] = dC
                dh_ref[g] = (dHTs[k] * jnp.concatenate(tots[sel], axis=0)
                             + jnp.dot(dydT, Cgs[k], preferred_element_type=F32))
        e1 = jnp.sum(p3t_ref[...], axis=1, keepdims=True)
        e2 = jnp.sum(e2t_ref[...], axis=1, keepdims=True)
        dcsT = dcst_ref[...] + jnp.where(lane == q['last'], e1 + totB * e2, 0.0)
        daT = _split_dot_r(dcsT, q['lo'].astype(BF16), 3)
        ddtT = daT * q['AT'] + p2t_ref[...]
        da_ref[...] += daT * dtT
        ddraw = jnp.where(lane < H, (ddtT * _sig(q['preT'])).T, 0.0)
        if rev:
            ddraw = pltpu.roll(ddraw, H, 1)
        ddt_ref[...] = ddraw
        if tail is not None:
            for c0, cb_ref in ((0, cbx_ref), (DI, cbbc_ref)):
                d = dxbc_ref[:, c0:c0 + DI] + other_ref[:, c0:c0 + DI]
                cb = cb_ref[...]
                dcb = d * _dsilu(cb, _sig(cb))
                dxbc_ref[:, c0:c0 + DI] = dcb
                part = _rows8(dcb)

                @pl.when(c == 0)
                def _():
                    dcol_ref[:, c0:c0 + DI] = part

                @pl.when(c > 0)
                def _():
                    dcol_ref[:, c0:c0 + DI] += part

    cmap = (lambda c: (c, 0)) if rev else (lambda c: (NC - 1 - c, 0))
    smap = (lambda c: (c, 0, 0, 0)) if rev else (lambda c: (NC - 1 - c, 0, 0, 0))
    const = lambda c: (0, 0)
    sq = pltpu.VMEM((LANES, CHUNK), F32)
    cix = (lambda c: c) if rev else (lambda c: NC - 1 - c)
    tmap = lambda c: (0, cix(c))
    in_specs = [pl.BlockSpec((DI, CHUNK), tmap), pl.BlockSpec((CHUNK, 2 * G * N), cmap), pl.BlockSpec((CHUNK, LANES), cmap),
                pl.BlockSpec((DI, CHUNK), tmap),
                pl.BlockSpec((1, G, GW, N), smap),
                pl.BlockSpec((LANES, LANES), const), pl.BlockSpec((LANES, LANES), const)]
    out_specs = [pl.BlockSpec((CHUNK, XBC), cmap), pl.BlockSpec((CHUNK, LANES), cmap),
                 pl.BlockSpec((LANES, LANES), const)]
    out_shape = [jax.ShapeDtypeStruct((S, XBC), F32), jax.ShapeDtypeStruct((S, LANES), F32),
                 jax.ShapeDtypeStruct((LANES, LANES), F32)]
    args = [xsT, bc, dtraw, dyT, st, dtbT, alogT]
    if tail is not None:
        in_specs += [pl.BlockSpec((CHUNK, XBC), cmap), pl.BlockSpec((CHUNK, DI), cmap),
                     pl.BlockSpec((CHUNK, 2 * G * N), cmap), pl.BlockSpec((DI, LANES), const)]
        out_specs.append(pl.BlockSpec((SUBLANES, XBC), const))
        out_shape.append(jax.ShapeDtypeStruct((SUBLANES, XBC), F32))
        args += list(tail)
    return pl.pallas_call(
        body, name=name, grid=(NC,), in_specs=in_specs, out_specs=out_specs, out_shape=out_shape,
        scratch_shapes=[pltpu.VMEM((G, GW, N), F32), sq, sq, sq, sq],
        compiler_params=_params(dimension_semantics=("arbitrary",)),
    )(*args)


ANY = pl.BlockSpec(memory_space=pl.ANY)


def chip_exchange(name, groups, gather):
    flat = [arr for grp in groups for arr in grp]
    n_in = len(flat)
    n_out = len(groups)
    n_rc = 3 * n_in

    def body(*refs):
        in_refs = refs[:n_in]
        out_refs = refs[n_in:n_in + n_out]
        send, recv = refs[n_in + n_out:]
        x, y, c = lax.axis_index("x"), lax.axis_index("y"), lax.axis_index("c")
        me = 2 * x + y
        peers = [(1 - x, y), (x, 1 - y), (1 - x, 1 - y)]
        remote = []
        q = 0
        for a, grp in enumerate(groups):
            for l in range(len(grp)):
                src = in_refs[q]
                dst = out_refs[a].at[me] if gather else out_refs[a].at[me, l]
                for j, (px, py) in enumerate(peers):
                    blk = src if gather else src.at[2 * px + py]
                    rc = pltpu.make_async_remote_copy(
                        src_ref=blk, dst_ref=dst, send_sem=send.at[3 * q + j], recv_sem=recv.at[3 * q + j],
                        device_id=(px, py, c), device_id_type=MESH)
                    rc.start()
                    remote.append(rc)
                q += 1
        for rc in remote:
            rc.wait()

    out_shape = []
    for grp in groups:
        a0 = grp[0]
        if gather:
            out_shape.append(jax.ShapeDtypeStruct((4,) + a0.shape, a0.dtype))
        else:
            out_shape.append(jax.ShapeDtypeStruct((4, len(grp)) + a0.shape[1:], a0.dtype))
    outs = pl.pallas_call(
        body, name=name, in_specs=[ANY] * n_in, out_specs=[ANY] * n_out, out_shape=out_shape,
        scratch_shapes=[pltpu.SemaphoreType.DMA((n_rc,)), pltpu.SemaphoreType.DMA((n_rc,))],
    )(*flat)
    me = _chip_index()
    res = []
    for grp, o in zip(groups, outs):
        for l, src in enumerate(grp):
            o = _put_block(o, src, (me,)) if gather else _put_block(o, _take_block(src, me), (me, l))
        res.append(o)
    return res


def _chip_index():
    return 2 * lax.axis_index("x") + lax.axis_index("y")


def _take_block(arr, idx):
    return lax.dynamic_index_in_dim(arr, idx, 0, keepdims=False)


def _put_block(dst, blk, idx):
    lead = len(idx)
    return lax.dynamic_update_slice(dst, blk.reshape((1,) * lead + blk.shape), tuple(idx) + (0,) * (dst.ndim - lead))


def gather_layer(name, split, whole):
    ns, nw = len(split), len(whole)
    n = ns + nw
    n_rc = 3 * (n + ns)

    def body(*refs):
        in_refs = refs[:n]
        out_refs = refs[n:2 * n]
        send, recv = refs[2 * n:]
        x, y, c = lax.axis_index("x"), lax.axis_index("y"), lax.axis_index("c")
        me = 2 * x + y
        sibling = (x, y, 1 - c)
        peers = [(1 - x, y), (x, 1 - y), (1 - x, 1 - y)]

        def region(a, chip, half):
            if a >= ns:
                return out_refs[a].at[chip]
            hr = split[a].shape[0] // 2
            return out_refs[a].at[chip, pl.ds(half * hr, hr)]

        def mine(a):
            if a >= ns:
                return in_refs[a]
            hr = split[a].shape[0] // 2
            return in_refs[a].at[pl.ds(c * hr, hr)]

        sends = []
        for a in range(n):
            for j, (px, py) in enumerate(peers):
                rc = pltpu.make_async_remote_copy(
                    src_ref=mine(a), dst_ref=region(a, me, c), send_sem=send.at[3 * a + j],
                    recv_sem=recv.at[3 * a + j], device_id=(px, py, c), device_id_type=MESH)
                rc.start()
                sends.append(rc)
        for a in range(n):
            for j, (px, py) in enumerate(peers):
                chip = 2 * px + py
                landed = pltpu.make_async_remote_copy(
                    src_ref=mine(a), dst_ref=region(a, chip, c), send_sem=send.at[3 * a + j],
                    recv_sem=recv.at[3 * a + j], device_id=(px, py, c), device_id_type=MESH)
                landed.wait_recv()
                if a < ns:
                    fw = pltpu.make_async_remote_copy(
                        src_ref=region(a, chip, c), dst_ref=region(a, chip, c), send_sem=send.at[3 * n + 3 * a + j],
                        recv_sem=recv.at[3 * n + 3 * a + j], device_id=sibling, device_id_type=MESH)
                    fw.start()
                    sends.append(fw)
        for a in range(ns):
            for j, (px, py) in enumerate(peers):
                chip = 2 * px + py
                pltpu.make_async_remote_copy(
                    src_ref=region(a, chip, 1 - c), dst_ref=region(a, chip, 1 - c), send_sem=send.at[3 * n + 3 * a + j],
                    recv_sem=recv.at[3 * n + 3 * a + j], device_id=sibling, device_id_type=MESH).wait_recv()
        for rc in sends:
            rc.wait_send()

    arrs = list(split) + list(whole)
    outs = pl.pallas_call(
        body, name=name, in_specs=[ANY] * n, out_specs=[ANY] * n,
        out_shape=[jax.ShapeDtypeStruct((4,) + a.shape, a.dtype) for a in arrs],
        scratch_shapes=[pltpu.SemaphoreType.DMA((n_rc,)), pltpu.SemaphoreType.DMA((n_rc,))],
    )(*arrs)
    me = _chip_index()
    return [_put_block(o, a, (me,)) for o, a in zip(outs, arrs)]


HBM_SPEC = pl.BlockSpec(memory_space=pltpu.HBM)
SEM_SPEC = pl.BlockSpec(memory_space=pltpu.SEMAPHORE)
IN_FLIGHT = pltpu.SideEffectType.DATAFLOW_SIDE_EFFECTING


def _chip_leg(kind, a_ref, l_ref, shape, c, me, chip):
    if kind == 'gather':
        hr = shape[0] // 2
        rows = pl.ds(c * hr, hr)
        return a_ref.at[rows], l_ref.at[me, rows], l_ref.at[chip, rows]
    return a_ref.at[chip], l_ref.at[me], l_ref.at[chip]


def chip_legs_start(name, kind, arrs, lands):
    n = len(arrs)

    def body(*refs):
        a_refs = refs[:n]
        l_refs = refs[n:2 * n]
        send, recv = refs[2 * n], refs[2 * n + 1]
        token = refs[-1]
        x, y, c = lax.axis_index("x"), lax.axis_index("y"), lax.axis_index("c")
        me = 2 * x + y
        for a in range(n):
            for j, (px, py) in enumerate([(1 - x, y), (x, 1 - y), (1 - x, 1 - y)]):
                src, dst, _ = _chip_leg(kind, a_refs[a], l_refs[a], arrs[a].shape, c, me, 2 * px + py)
                pltpu.make_async_remote_copy(src_ref=src, dst_ref=dst, send_sem=send.at[3 * a + j],
                                             recv_sem=recv.at[3 * a + j], device_id=(px, py, c),
                                             device_id_type=MESH).start()
        token[...] = jnp.zeros_like(token)

    both = list(arrs) + list(lands)
    outs = pl.pallas_call(
        body, name=name,
        out_shape=(pltpu.SemaphoreType.DMA((3 * n,)), pltpu.SemaphoreType.DMA((3 * n,)),
                   *[pltpu.HBM(a.shape, a.dtype) for a in both], jax.ShapeDtypeStruct((SUBLANES, LANES), F32)),
        in_specs=[HBM_SPEC] * (2 * n),
        out_specs=(SEM_SPEC, SEM_SPEC, *[HBM_SPEC] * (2 * n), pl.BlockSpec(memory_space=pltpu.VMEM)),
        input_output_aliases={i: 2 + i for i in range(2 * n)},
        compiler_params=pltpu.CompilerParams(has_side_effects=IN_FLIGHT),
    )(*[pltpu.with_memory_space_constraint(a, pltpu.HBM) for a in both])
    return outs[0], outs[1], list(outs[2:2 + n]), list(outs[2 + n:2 + 2 * n]), outs[-1]


def chip_legs_wait(name, kind, send, recv, arrs, lands, after):
    n = len(arrs)

    def body(*refs):
        a_refs = refs[:n]
        l_refs = refs[n:2 * n]
        send_, recv_ = refs[2 * n], refs[2 * n + 1]
        x, y, c = lax.axis_index("x"), lax.axis_index("y"), lax.axis_index("c")
        me = 2 * x + y
        legs = []
        for a in range(n):
            for j, (px, py) in enumerate([(1 - x, y), (x, 1 - y), (1 - x, 1 - y)]):
                src, dst, landing = _chip_leg(kind, a_refs[a], l_refs[a], arrs[a].shape, c, me, 2 * px + py)
                legs.append(pltpu.make_async_remote_copy(src_ref=src, dst_ref=landing, send_sem=send_.at[3 * a + j],
                                                         recv_sem=recv_.at[3 * a + j], device_id=(px, py, c),
                                                         device_id_type=MESH))
        for leg in legs:
            leg.wait_send()
        for leg in legs:
            leg.wait_recv()

    both = list(arrs) + list(lands)
    outs = pl.pallas_call(
        body, name=name, out_shape=tuple(pltpu.HBM(a.shape, a.dtype) for a in both),
        in_specs=[HBM_SPEC] * (2 * n) + [SEM_SPEC, SEM_SPEC, ANY], out_specs=tuple([HBM_SPEC] * (2 * n)),
        input_output_aliases={i: i for i in range(2 * n)},
        compiler_params=pltpu.CompilerParams(has_side_effects=IN_FLIGHT),
    )(*both, send, recv, after)
    return list(outs[n:])


def gather_finish(name, split, landed):
    n = len(split)

    def body(*refs):
        out_refs = refs[n:2 * n]
        send, recv = refs[2 * n:]
        x, y, c = lax.axis_index("x"), lax.axis_index("y"), lax.axis_index("c")
        sibling = (x, y, 1 - c)
        chips = [2 * (1 - x) + y, 2 * x + (1 - y), 2 * (1 - x) + (1 - y)]

        def region(a, chip, half):
            hr = split[a].shape[0] // 2
            return out_refs[a].at[chip, pl.ds(half * hr, hr)]

        sends = []
        for a in range(n):
            for j, chip in enumerate(chips):
                fw = pltpu.make_async_remote_copy(
                    src_ref=region(a, chip, c), dst_ref=region(a, chip, c), send_sem=send.at[3 * a + j],
                    recv_sem=recv.at[3 * a + j], device_id=sibling, device_id_type=MESH)
                fw.start()
                sends.append(fw)
        for a in range(n):
            for j, chip in enumerate(chips):
                pltpu.make_async_remote_copy(
                    src_ref=region(a, chip, 1 - c), dst_ref=region(a, chip, 1 - c), send_sem=send.at[3 * a + j],
                    recv_sem=recv.at[3 * a + j], device_id=sibling, device_id_type=MESH).wait_recv()
        for fw in sends:
            fw.wait_send()

    outs = pl.pallas_call(
        body, name=name, in_specs=[ANY] * n, out_specs=[ANY] * n,
        out_shape=[jax.ShapeDtypeStruct(a.shape, a.dtype) for a in landed],
        input_output_aliases={a: a for a in range(n)},
        scratch_shapes=[pltpu.SemaphoreType.DMA((3 * n,)), pltpu.SemaphoreType.DMA((3 * n,))],
    )(*landed)
    me = _chip_index()
    return [_put_block(o, a, (me,)) for o, a in zip(outs, split)]


def place_own(arrs, landed):
    me = _chip_index()
    return [_put_block(l, _take_block(a, me), (me,)) for a, l in zip(arrs, landed)]


def core_send_half(name, arrs):
    n = len(arrs)

    def body(*refs):
        in_refs = refs[:n]
        out_refs = refs[n:2 * n]
        send, recv = refs[2 * n:]
        c = lax.axis_index("c")
        peer = (lax.axis_index("x"), lax.axis_index("y"), 1 - c)
        rcs = []
        for a in range(n):
            hr = arrs[a].shape[1] // 2
            rc = pltpu.make_async_remote_copy(
                src_ref=in_refs[a].at[:, pl.ds((1 - c) * hr, hr)], dst_ref=out_refs[a], send_sem=send.at[a],
                recv_sem=recv.at[a], device_id=peer, device_id_type=MESH)
            rc.start()
            rcs.append(rc)
        for rc in rcs:
            rc.wait()

    return pl.pallas_call(
        body, name=name, in_specs=[ANY] * n, out_specs=[ANY] * n,
        out_shape=[jax.ShapeDtypeStruct((4, a.shape[1] // 2, a.shape[2]), a.dtype) for a in arrs],
        scratch_shapes=[pltpu.SemaphoreType.DMA((n,)), pltpu.SemaphoreType.DMA((n,))],
    )(*arrs)


def core_fill(name, arrs, layer, n_layers):
    n = len(arrs)

    def body(*refs):
        out_refs = refs[n:2 * n]
        send, recv = refs[2 * n:]
        c = lax.axis_index("c")
        peer = (lax.axis_index("x"), lax.axis_index("y"), 1 - c)
        rcs = []
        for a in range(n):
            r = arrs[a].shape[0] // n_layers
            hr = r // 2
            rows = out_refs[a].at[pl.ds(layer * r + c * hr, hr)]
            rc = pltpu.make_async_remote_copy(src_ref=rows, dst_ref=rows, send_sem=send.at[a], recv_sem=recv.at[a],
                                              device_id=peer, device_id_type=MESH)
            rc.start()
            rcs.append(rc)
        for a in range(n):
            r = arrs[a].shape[0] // n_layers
            hr = r // 2
            theirs = out_refs[a].at[pl.ds(layer * r + (1 - c) * hr, hr)]
            pltpu.make_async_remote_copy(src_ref=theirs, dst_ref=theirs, send_sem=send.at[a], recv_sem=recv.at[a],
                                         device_id=peer, device_id_type=MESH).wait_recv()
        for rc in rcs:
            rc.wait_send()

    return pl.pallas_call(
        body, name=name, in_specs=[ANY] * n, out_specs=[ANY] * n,
        out_shape=[jax.ShapeDtypeStruct(a.shape, a.dtype) for a in arrs],
        input_output_aliases={a: a for a in range(n)},
        scratch_shapes=[pltpu.SemaphoreType.DMA((n,)), pltpu.SemaphoreType.DMA((n,))],
    )(*arrs)


def sibling_swap(name, arrs):
    n = len(arrs)

    def body(*refs):
        in_refs = refs[:n]
        out_refs = refs[n:2 * n]
        send, recv = refs[2 * n:]
        peer = (lax.axis_index("x"), lax.axis_index("y"), 1 - lax.axis_index("c"))
        rcs = []
        for a in range(n):
            rc = pltpu.make_async_remote_copy(src_ref=in_refs[a], dst_ref=out_refs[a], send_sem=send.at[a],
                                              recv_sem=recv.at[a], device_id=peer, device_id_type=MESH)
            rc.start()
            rcs.append(rc)
        for rc in rcs:
            rc.wait()

    return pl.pallas_call(
        body, name=name, in_specs=[ANY] * n, out_specs=[ANY] * n,
        out_shape=[jax.ShapeDtypeStruct(a.shape, a.dtype) for a in arrs],
        scratch_shapes=[pltpu.SemaphoreType.DMA((n,)), pltpu.SemaphoreType.DMA((n,))],
    )(*arrs)


def all8_gather(name, v):
    flips = [(fx, fy, fc) for fx in (0, 1) for fy in (0, 1) for fc in (0, 1) if (fx, fy, fc) != (0, 0, 0)]

    def body(v_ref, out_ref, send, recv, loc):
        x, y, c = lax.axis_index("x"), lax.axis_index("y"), lax.axis_index("c")
        me = 4 * x + 2 * y + c
        lc = pltpu.make_async_copy(v_ref, out_ref.at[me], loc)
        lc.start()
        rcs = []
        for k, (fx, fy, fc) in enumerate(flips):
            tgt = (x + fx - 2 * x * fx, y + fy - 2 * y * fy, c + fc - 2 * c * fc)
            rc = pltpu.make_async_remote_copy(src_ref=v_ref, dst_ref=out_ref.at[me], send_sem=send.at[k],
                                              recv_sem=recv.at[k], device_id=tgt, device_id_type=MESH)
            rc.start()
            rcs.append(rc)
        lc.wait()
        for rc in rcs:
            rc.wait()

    return pl.pallas_call(
        body, name=name, in_specs=[ANY], out_specs=ANY,
        out_shape=jax.ShapeDtypeStruct((8,) + v.shape, v.dtype),
        scratch_shapes=[pltpu.SemaphoreType.DMA((7,)), pltpu.SemaphoreType.DMA((7,)), pltpu.SemaphoreType.DMA],
    )(v)


def _pick_rows(rows, cols, target_elems=128 * 1024, mult=SUBLANES):
    if rows % mult != 0:
        return rows
    best = mult
    t = mult
    while t <= rows:
        if rows % t == 0 and t * cols <= target_elems:
            best = t
        t += mult
    return best


def sum_chips(name, parts):
    _, R, C = parts.shape
    tm = _pick_rows(R, C)

    def body(p_ref, o_ref):
        o_ref[...] = (p_ref[0] + p_ref[1]) + (p_ref[2] + p_ref[3])

    return pl.pallas_call(
        body, name=name, grid=(R // tm,),
        in_specs=[pl.BlockSpec((4, tm, C), lambda i: (0, i, 0))],
        out_specs=pl.BlockSpec((tm, C), lambda i: (i, 0)),
        out_shape=jax.ShapeDtypeStruct((R, C), F32),
        compiler_params=_params(dimension_semantics=("arbitrary",)),
    )(parts)


def _adamw(g, w, m, v):
    m = ADAM_B1 * m + (1.0 - ADAM_B1) * g
    v = ADAM_B2 * v + (1.0 - ADAM_B2) * (g * g)
    m_hat = m / (1.0 - ADAM_B1 ** ADAM_STEP)
    v_hat = v / (1.0 - ADAM_B2 ** ADAM_STEP)
    delta = -ADAM_LR * (m_hat / (jnp.sqrt(v_hat) + ADAM_EPS) + ADAM_WD * w)
    return delta, m, v


def adamw_shard(name, s_mine, s_sib, w, m, v):
    R, C = w.shape
    tm = _pick_rows(R, C)

    def body(a_ref, b_ref, w_ref, m_ref, v_ref, g_out, d_out, m_out, v_out):
        g = a_ref[...] + b_ref[...]
        d, mn, vn = _adamw(g, w_ref[...], m_ref[...], v_ref[...])
        g_out[...] = g
        d_out[...] = d
        m_out[...] = mn
        v_out[...] = vn

    spec = pl.BlockSpec((tm, C), lambda i: (i, 0))
    return pl.pallas_call(
        body, name=name, grid=(R // tm,), in_specs=[spec] * 5, out_specs=[spec] * 4,
        out_shape=[jax.ShapeDtypeStruct((R, C), F32)] * 4,
        compiler_params=_params(dimension_semantics=("arbitrary",)),
    )(s_mine, s_sib, w, m, v)


def core_sum(name, core, g, got):
    _, r, C = g.shape
    hr = r // 2
    tm = _pick_rows(hr, 4 * C, 256 * 1024, 2 * SUBLANES)
    nh = hr // tm

    def body(c_ref, g_ref, s_ref, o_ref):
        o_ref[...] = (g_ref[...] + s_ref[...]).astype(BF16)

    return pl.pallas_call(
        body, name=name,
        grid_spec=pltpu.PrefetchScalarGridSpec(
            num_scalar_prefetch=1, grid=(nh,),
            in_specs=[pl.BlockSpec((4, tm, C), lambda i, cr: (0, cr[0] * nh + i, 0)),
                      pl.BlockSpec((4, tm, C), lambda i, cr: (0, i, 0))],
            out_specs=pl.BlockSpec((4, tm, C), lambda i, cr: (0, i, 0))),
        out_shape=jax.ShapeDtypeStruct((4, hr, C), BF16),
        compiler_params=_params(dimension_semantics=("arbitrary",)),
    )(core, g, got)


def chip_sum_into(name, core, parts, layer, n_layers, into=None):
    _, hr, C = parts.shape
    r = 2 * hr
    tm = _pick_rows(hr, 4 * C, 256 * 1024, 2 * SUBLANES)
    nh = hr // tm

    def body(c_ref, p_ref, *rest):
        o_ref = rest[-1]
        o_ref[...] = (_f32(p_ref[0]) + _f32(p_ref[1])) + (_f32(p_ref[2]) + _f32(p_ref[3]))

    in_specs = [pl.BlockSpec((4, tm, C), lambda i, cr: (0, i, 0))]
    args = [core, parts]
    aliases = {}
    if into is not None:
        in_specs.append(pl.BlockSpec(memory_space=pl.ANY))
        args.append(into)
        aliases = {2: 0}
    return pl.pallas_call(
        body, name=name,
        grid_spec=pltpu.PrefetchScalarGridSpec(
            num_scalar_prefetch=1, grid=(nh,), in_specs=in_specs,
            out_specs=pl.BlockSpec((tm, C), lambda i, cr: ((layer * r) // tm + cr[0] * nh + i, 0))),
        out_shape=jax.ShapeDtypeStruct((n_layers * r, C), F32), input_output_aliases=aliases,
        compiler_params=_params(dimension_semantics=("arbitrary",)),
    )(*args)


def adamw_full(name, g, w, m, v):
    R, C = w.shape
    tm = _pick_rows(R, C)

    def body(g_ref, w_ref, m_ref, v_ref, d_out, m_out, v_out):
        d, mn, vn = _adamw(g_ref[...], w_ref[...], m_ref[...], v_ref[...])
        d_out[...] = d
        m_out[...] = mn
        v_out[...] = vn

    spec = pl.BlockSpec((tm, C), lambda i: (i, 0))
    return pl.pallas_call(
        body, name=name, grid=(R // tm,), in_specs=[spec] * 4, out_specs=[spec] * 3,
        out_shape=[jax.ShapeDtypeStruct((R, C), F32)] * 3,
        compiler_params=_params(dimension_semantics=("arbitrary",)),
    )(g, w, m, v)


def fold_rows(name, v):
    def body(v_ref, o_ref):
        o_ref[...] = jnp.sum(v_ref[...], axis=0, keepdims=True)

    return pl.pallas_call(body, name=name, out_shape=jax.ShapeDtypeStruct((1, v.shape[1]), F32),
                          compiler_params=_params())(v)


def adamw_small(name, parts, w, m, v):
    W = w.shape[1]

    def body(p_ref, w_ref, m_ref, v_ref, g_out, d_out, m_out, v_out):
        acc = p_ref[0]
        for k in range(1, 8):
            acc = acc + p_ref[k]
        g = jnp.sum(acc, axis=0, keepdims=True)
        d, mn, vn = _adamw(g, w_ref[...], m_ref[...], v_ref[...])
        g_out[...] = g
        d_out[...] = d
        m_out[...] = mn
        v_out[...] = vn

    return pl.pallas_call(
        body, name=name, out_shape=[jax.ShapeDtypeStruct((1, W), F32)] * 4,
        compiler_params=_params(),
    )(parts, w, m, v)


def _pad_lanes(v, width=LANES):
    return jnp.pad(v, ((0, 0), (0, width - v.shape[1])))


def _layer_fwd(cf, x, xb, pb, W, sm, target=None):
    S, D, CD, DI, XBC, F, H, G = cf['S'], cf['D'], cf['CD'], cf['DI'], cf['XBC'], cf['F'], cf['H'], cf['G']
    NM = cf['NM']
    alpha = cf['alpha']
    tm = cf['tm']
    tmx = cf['tmx']
    tn_in = cf['tn_in']
    sv = {}

    ident = lambda acc, ex: ([acc], [])
    proj, = fused_mm("in_proj", [(xb, W['in_main'], 0, False)], [], ident, [(NM, BF16, tn_in, 0)],
                     M=S, tm=tmx, tn=tn_in, nj=NM // tn_in)
    dtraw, = fused_mm("dt_proj", [(xb, W['in_dt'], 0, False)], [], ident, [(LANES, F32, LANES, 0)],
                      M=S, tm=tmx, tn=LANES)

    u, = row_call("glu", lambda a, gt: ([a * _sig(gt)], []),
                  [(proj, 'row', CD, 0, 0), (proj, 'row', CD, 1, 0)], [(CD, F32, CD, 0, 0)], M=S, tm=tm)

    def conv_a_epi(conv, ex):
        cb_, g_, b_ = ex
        ca = conv + cb_
        xhat, _ = _ln_stats(ca)
        la = xhat * g_ + b_
        return [ca, la * _sig(la)], []

    ca, sa = conv_call("conv_a", u, 0, sm['conv_a_w'], cf['KA'], conv_a_epi,
                       [(sm['conv_a_b'], 'vec', CD, 0, 0), (sm['ln_a_g'], 'vec', CD, 0, 0), (sm['ln_a_b'], 'vec', CD, 0, 0)],
                       [(CD, F32, CD, 0, 0), (CD, BF16, CD, 0, 0)], M=S, tm=cf['tmc'], cw=CD, nc=1, reverse=False)
    y_a, = fused_mm("a_out", [(sa, W['a_out'], 0, False)], [], ident, [(D, F32, D, 0)], M=S, tm=tmx, tn=D)

    def conv_x_epi(conv, ex):
        cb = conv + ex[0]
        act = cb * _sig(cb)
        return [cb, act], [], [act]

    def conv_bc_epi(conv, ex):
        cb = conv + ex[0]
        return [cb, cb * _sig(cb)], []

    xoff = (2 * CD + 2 * D + DI) // DI
    cbv_x, xs, xsT = conv_call("conv_b_x", proj, xoff, sm['ssm_conv_w'], cf['KB'], conv_x_epi,
                               [(sm['ssm_conv_b'], 'vec', DI, 0, 0)],
                               [(DI, F32, DI, 0, 0), (DI, F32, DI, 0, 0)], M=S, tm=cf['tmc'], cw=DI, nc=1,
                               reverse=False, t_outs=[(DI, F32, DI, 0, 0)])
    cbv_bc, bc = conv_call("conv_b_bc", proj, xoff + 1, sm['ssm_conv_w'], cf['KB'], conv_bc_epi,
                           [(sm['ssm_conv_b'], 'vec', DI, 1, 0)],
                           [(DI, F32, DI, 0, 0), (DI, F32, DI, 0, 0)], M=S, tm=cf['tmc'], cw=DI, nc=1,
                           reverse=False, w_c0=1)
    y_f, st_f = ssd_fwd("ssd_fwd_f", xsT, bc, dtraw, sm['dtb_f'], sm['alog_f'], S=S, DI=DI, G=G, H=H, rev=False)
    zoff = (2 * CD + 2 * D) // DI
    ysum, st_r, yn = ssd_fwd("ssd_fwd_r", xsT, bc, dtraw, sm['dtb_r'], sm['alog_r'], S=S, DI=DI, G=G, H=H, rev=True,
                             tail=(y_f, (proj, zoff), xs, sm['dskip_full'], sm['ssm_norm_g']))
    goff = (2 * CD) // D

    def merge_epi(acc, ex):
        ga, gb, ya = ex
        return [acc, _sig(ga) * ya + _sig(gb) * acc], []

    y_b, merged = fused_mm("b_out", [(yn, W['b_out'], 0, False)],
                           [(proj, 'row', D, goff), (proj, 'row', D, goff + 1), (y_a, 'row', D, 0)],
                           merge_epi, [(D, F32, D, 0), (D, BF16, D, 0)], M=S, tm=tm, tn=D)

    def mix_epi(acc, ex):
        xin, g_, b_ = ex
        r1 = alpha * xin + acc
        xhat, _ = _ln_stats(r1)
        return [r1, xhat * g_ + b_], []

    r1, hb = fused_mm("o_mix", [(merged, W['o'], 0, False)],
                      [(x, 'row', D, 0), (sm['ln1_g'], 'vec', D, 0), (sm['ln1_b'], 'vec', D, 0)],
                      mix_epi, [(D, F32, D, 0), (D, BF16, D, 0)], M=S, tm=tm, tn=D)

    tnf = cf['tnf']

    g32, g_ = fused_mm("ffn_gate", [(hb, W['gate_up'], 0, False)], [], lambda acc, ex: ([acc, acc], []),
                       [(F, F32, tnf, 0), (F, BF16, tnf, 0)], M=S, tm=tmx, tn=tnf, nj=F // tnf)
    u_, f = fused_mm("ffn_up", [(hb, W['gate_up'], F // tnf, False)], [(g32, 'row', tnf, 0)],
                     lambda acc, ex: ([acc, ex[0] * _sig(ex[0]) * acc], []),
                     [(F, BF16, tnf, 0), (F, BF16, tnf, 0)], M=S, tm=tmx, tn=tnf, nj=F // tnf)

    def down_epi(acc, ex):
        r1_, g1, b1, g2, b2 = ex
        xh1, _ = _ln_stats(r1_)
        r2 = alpha * (xh1 * g1 + b1) + acc
        xh2, _ = _ln_stats(r2)
        return [r2, xh2 * g2 + b2], []

    r2, h2b = fused_mm("ffn_down", [(f, W['down'], 0, False)],
                       [(r1, 'row', D, 0), (sm['ln1_g'], 'vec', D, 0), (sm['ln1_b'], 'vec', D, 0),
                        (sm['ln2_g'], 'vec', D, 0), (sm['ln2_b'], 'vec', D, 0)],
                       down_epi, [(D, F32, D, 0), (D, BF16, D, 0)], M=S, tm=tm, tn=D)

    pe, = fused_mm("ple_proj", [(pb, W['ple'], 0, False)], [], ident, [(D, F32, D, 0)], M=S, tm=tmx, tn=D)

    def ple_out(acc, r2_, g2, b2, pe_, pg):
        xh2, _ = _ln_stats(r2_)
        h2 = xh2 * g2 + b2
        e = pe_ * lax.rsqrt(jnp.mean(pe_ * pe_, axis=-1, keepdims=True) + RMS_EPS) * pg
        return h2 + e * _sig(acc)

    ple_extras = [(r2, 'row', D, 0), (sm['ln2_g'], 'vec', D, 0), (sm['ln2_b'], 'vec', D, 0),
                  (pe, 'row', D, 0), (sm['ple_norm_g'], 'vec', D, 0)]
    if target is None:
        def ple_epi(acc, ex):
            xn_ = ple_out(acc, *ex)
            return [acc, xn_, xn_], []

        t_, xn, xnb = fused_mm("ple_gate", [(h2b, W['ple_gate'], 0, False)], ple_extras,
                               ple_epi, [(D, F32, D, 0), (D, F32, D, 0), (D, BF16, D, 0)], M=S, tm=tm, tn=D)
    else:
        def ple_loss_epi(acc, ex):
            err = ple_out(acc, *ex[:5]) - ex[5]
            dx_ = err * (1.0 / D)
            dtg, dpe, dpg = _ple_bwd(dx_, acc, ex[3], ex[4])
            return [acc, dx_, dtg, dpe], [dpg, err * err]

        t_, dxn, dtg, dpe, dpg, lsq = fused_mm(
            "ple_gate_loss", [(h2b, W['ple_gate'], 0, False)], ple_extras + [(target, 'row', D, 0)],
            ple_loss_epi, [(D, F32, D, 0), (D, F32, D, 0), (D, BF16, D, 0), (D, BF16, D, 0)],
            [(D, D, 0), (D, D, 0)], M=S, tm=tm, tn=D)
        xn = xnb = None
        sv['head'] = (dxn, dtg, dpe, dpg, lsq)
    sv.update(x=x, xb=xb, pb=pb, proj=proj, dtraw=dtraw, u=u, ca=ca, sa=sa, y_a=y_a, cbv_x=cbv_x, cbv_bc=cbv_bc,
              xs=xs, xsT=xsT, bc=bc,
              ysum=ysum, st_f=st_f, st_r=st_r, yn=yn, y_b=y_b, merged=merged, r1=r1, hb=hb,
              g_=g_, u_=u_, f=f, r2=r2, h2b=h2b, t_=t_, pe=pe)
    return xn, xnb, sv


def _ple_bwd(dx_, t, pe_, pg):
    s = _sig(t)
    rinv = lax.rsqrt(jnp.mean(pe_ * pe_, axis=-1, keepdims=True) + RMS_EPS)
    pn = pe_ * rinv
    e = pn * pg
    dtg = dx_ * e * (s * (1.0 - s))
    de = dx_ * s
    qv = de * pg
    dpe = rinv * (qv - pn * jnp.mean(qv * pn, axis=-1, keepdims=True))
    return dtg, dpe, de * pn


def _layer_bwd(cf, sv, W, sm, dxn=None):
    S, D, CD, DI, XBC, F, H, G = cf['S'], cf['D'], cf['CD'], cf['DI'], cf['XBC'], cf['F'], cf['H'], cf['G']
    NM = cf['NM']
    alpha = cf['alpha']
    tm = cf['tm']
    gw = cf['GW']
    out = {}

    if dxn is None:
        dxn, dtg, dpe, dpg, out['loss_sq'] = sv['head']
    else:
        def mid(dx_, t, pe_, pg):
            dtg, dpe, dpg = _ple_bwd(dx_, t, pe_, pg)
            return [dtg, dpe], [dpg]

        (dtg, dpe, dpg) = row_call(
            "ple_bwd", mid,
            [(dxn, 'row', D, 0, 0), (sv['t_'], 'row', D, 0, 0), (sv['pe'], 'row', D, 0, 0),
             (sm['ple_norm_g'], 'vec', D, 0, 0)],
            [(D, BF16, D, 0, 0), (D, BF16, D, 0, 0)], [(D, D, 0, 0)], M=S, tm=tm)
    out['ple_norm_g'] = dpg

    def ln_bwd_epi(scale):
        def epi(acc, ex):
            res, r_, g_ = ex
            dh = scale * res + acc
            xhat, rstd = _ln_stats(r_)
            dr = _ln_bwd(dh, xhat, rstd, g_)
            return [dr, dr], [dh * xhat, dh]
        return epi

    dr2, dr2b, dg2, db2 = fused_mm(
        "dh2", [(dtg, W['ple_gate_T'], 0, False)],
        [(dxn, 'row', D, 0), (sv['r2'], 'row', D, 0), (sm['ln2_g'], 'vec', D, 0)],
        ln_bwd_epi(1.0), [(D, F32, D, 0), (D, BF16, D, 0)], [(D, D, 0), (D, D, 0)], M=S, tm=tm, tn=D)
    out['ln2_g'], out['ln2_b'] = dg2, db2

    tnf = cf['tnf']

    def dswiglu_epi(acc, ex):
        gg, uu = ex
        s = _sig(gg)
        return [acc * uu * _dsilu(gg, s), acc * (gg * s)], []

    dg_b, du_b = fused_mm(
        "d_down", [(dr2b, W['down_T'], 0, False)],
        [(sv['g_'], 'row', tnf, 0), (sv['u_'], 'row', tnf, 0)], dswiglu_epi,
        [(F, BF16, tnf, 0), (F, BF16, tnf, 0)], M=S, tm=tm, tn=tnf, nj=F // tnf)

    dr1, dr1b, dg1, db1 = fused_mm(
        "dh1", [(dg_b, W['gate_T'], 0, True), (du_b, W['up_T'], 0, True)],
        [(dr2, 'row', D, 0), (sv['r1'], 'row', D, 0), (sm['ln1_g'], 'vec', D, 0)],
        ln_bwd_epi(alpha), [(D, F32, D, 0), (D, BF16, D, 0)], [(D, D, 0), (D, D, 0)],
        M=S, tm=tm, tn=D, nk=cf['nk_f'])
    out['ln1_g'], out['ln1_b'] = dg1, db1

    goff = (2 * CD) // D

    def dmerge_epi(acc, ex):
        ga, gb, ya, yb = ex
        sa_, sb_ = _sig(ga), _sig(gb)
        dga = acc * ya * (sa_ * (1.0 - sa_))
        dgb = acc * yb * (sb_ * (1.0 - sb_))
        return [jnp.concatenate([dga, dgb], axis=1), acc * sa_, acc * sb_], []

    dproj, dya_b, dyb_b = fused_mm(
        "d_merge", [(dr1b, W['o_T'], 0, False)],
        [(sv['proj'], 'row', D, goff), (sv['proj'], 'row', D, goff + 1), (sv['y_a'], 'row', D, 0), (sv['y_b'], 'row', D, 0)],
        dmerge_epi, [(NM, BF16, 2 * D, (2 * CD) // (2 * D)), (D, BF16, D, 0), (D, BF16, D, 0)], M=S, tm=tm, tn=D)

    def dsa_epi(acc, ex):
        ca_, g_, b_ = ex
        xhat, rstd = _ln_stats(ca_)
        la = xhat * g_ + b_
        dla = acc * _dsilu(la, _sig(la))
        dca = _ln_bwd(dla, xhat, rstd, g_)
        return [dca], [dla * xhat, dla, dca]

    dca, dlag, dlab, dcab = fused_mm(
        "d_a_out", [(dya_b, W['a_out_T'], 0, False)],
        [(sv['ca'], 'row', CD, 0), (sm['ln_a_g'], 'vec', CD, 0), (sm['ln_a_b'], 'vec', CD, 0)],
        dsa_epi, [(CD, F32, CD, 0)], [(CD, CD, 0), (CD, CD, 0), (CD, CD, 0)], M=S, tm=tm, tn=D)
    out['ln_a_g'], out['ln_a_b'], out['conv_a_b'] = dlag, dlab, dcab

    def dglu_epi(du, ex):
        a, gt = ex
        s = _sig(gt)
        return [jnp.concatenate([du * s, du * a * (s * (1.0 - s))], axis=1)], []

    dproj, dwa = conv_call(
        "d_conv_a", dca, 0, sm['conv_a_w'], cf['KA'], dglu_epi,
        [(sv['proj'], 'row', CD, 0, 0), (sv['proj'], 'row', CD, 1, 0)],
        [(NM, BF16, 2 * CD, 0, 0)], M=S, tm=cf['tmc'], cw=CD, nc=1, reverse=True, xin=(sv['u'], 0),
        passthrough=(dproj, 0))
    out['conv_a_w'] = dwa

    zoff = (2 * CD + 2 * D) // DI

    def dgate_norm_epi(acc, ex):
        ysum_, xs, z, dsk, ng = ex
        y = ysum_ + xs * dsk
        sz = _sig(z)
        siluz = z * sz
        yz = y * siluz
        dyzs, yhats = [], []
        for g in range(G):
            t = yz[:, g * gw:(g + 1) * gw]
            rinv = lax.rsqrt(jnp.mean(t * t, axis=-1, keepdims=True) + RMS_EPS)
            yh = t * rinv
            qv = acc[:, g * gw:(g + 1) * gw] * ng[:, g * gw:(g + 1) * gw]
            dyzs.append(rinv * (qv - yh * jnp.mean(qv * yh, axis=-1, keepdims=True)))
            yhats.append(yh)
        dyz = jnp.concatenate(dyzs, axis=1)
        yhat = jnp.concatenate(yhats, axis=1)
        dy = dyz * siluz
        dz = dyz * y * _dsilu(z, sz)
        return [dz], [acc * yhat, dy * xs], [dy]

    tmr = cf['tmr']
    dproj, dng, ddsk, dyT = fused_mm(
        "d_b_out", [(dyb_b, W['b_out_T'], 0, False)],
        [(sv['ysum'], 'row', DI, 0), (sv['xs'], 'row', DI, 0), (sv['proj'], 'row', DI, zoff),
         (sm['dskip_full'], 'vec', DI, 0), (sm['ssm_norm_g'], 'vec', DI, 0)],
        dgate_norm_epi, [(NM, BF16, DI, zoff)], [(DI, DI, 0), (DI, DI, 0)],
        M=S, tm=tmr, tn=DI, passthrough=(dproj, 0), t_outs=[(DI, F32, DI, 0)])
    out['ssm_norm_g'], out['dskip_full'] = dng, ddsk

    dxbc_f, ddt_f, dA_f = ssd_bwd("ssd_bwd_f", sv['xsT'], sv['bc'], sv['dtraw'], dyT, sv['st_f'], sm['dtb_f'],
                                  sm['alog_f'], S=S, DI=DI, G=G, H=H, rev=False)
    dcb, ddt_r, dA_r, dcbb = ssd_bwd("ssd_bwd_r", sv['xsT'], sv['bc'], sv['dtraw'], dyT, sv['st_r'], sm['dtb_r'],
                                     sm['alog_r'], S=S, DI=DI, G=G, H=H, rev=True,
                                     tail=(dxbc_f, sv['cbv_x'], sv['cbv_bc'], sm['dskipT']))
    out['dA_f'], out['dA_r'] = dA_f, dA_r
    out['ssm_conv_b'] = dcbb

    xoff = (2 * CD + 2 * D + DI) // DI
    dproj, dwb = conv_call(
        "d_conv_b", dcb, 0, sm['ssm_conv_w'], cf['KB'], lambda conv, ex: ([conv], []), [],
        [(NM, BF16, DI, xoff, 1)], M=S, tm=cf['tmc'], cw=DI, nc=XBC // DI, reverse=True, xin=(sv['proj'], xoff),
        passthrough=(dproj, 0))
    out['ssm_conv_w'] = dwb

    ddtb, ddt_bias = row_call("d_dt", lambda a, b: ([a + b], [a + b]),
                              [(ddt_f, 'row', LANES, 0, 0), (ddt_r, 'row', LANES, 0, 0)],
                              [(LANES, BF16, LANES, 0, 0)], [(LANES, LANES, 0, 0)], M=S, tm=tm)
    out['dt_bias'] = ddt_bias

    dx, = fused_mm("d_x", [(dproj, W['in_main_T'], 0, True), (ddtb, W['in_dt_T'], 0, False)],
                   [(dr1, 'row', D, 0)], lambda acc, ex: ([alpha * ex[0] + acc], []),
                   [(D, F32, D, 0)], M=S, tm=cf['tmx'], tn=D, nk=cf['nk_in'])

    tmw = cf['tmw']
    xb = sv['xb']
    out['w_in'] = jnp.concatenate(
        [mm_tn("dw_in", xb, dproj, tm=tmw, tk=D, tn=2 * cf['tn_in'] if NM % (2 * cf['tn_in']) == 0 else cf['tn_in']),
         mm_tn("dw_dt", xb, ddtb, tm=tmw, tk=D, tn=LANES)[:, :2 * H]], axis=1)
    out['w_a_out'] = mm_tn("dw_a_out", sv['sa'], dya_b, tm=tmw, tk=CD, tn=D)
    out['w_b_out'] = mm_tn("dw_b_out", sv['yn'], dyb_b, tm=tmw, tk=DI // 2, tn=D)
    out['w_o'] = mm_tn("dw_o", sv['merged'], dr1b, tm=tmw, tk=D, tn=D)
    out['w_gate_up'] = jnp.concatenate(
        [mm_tn("dw_gate", sv['hb'], dg_b, tm=tmw, tk=D, tn=F),
         mm_tn("dw_up", sv['hb'], du_b, tm=tmw, tk=D, tn=F)], axis=1)
    out['w_down'] = mm_tn("dw_down", sv['f'], dr2b, tm=tmw, tk=tnf, tn=D)
    out['w_ple'] = mm_tn("dw_ple", sv['pb'], dpe, tm=tmw, tk=sv['pb'].shape[1], tn=D)
    out['w_ple_gate'] = mm_tn("dw_ple_gate", sv['h2b'], dtg, tm=tmw, tk=D, tn=D)
    return dx, out


_WEIGHTS = ['w_in', 'conv_a_w', 'conv_a_b', 'ln_a_g', 'ln_a_b', 'w_a_out', 'ssm_conv_w', 'ssm_conv_b', 'a_log',
            'dt_bias', 'd_skip', 'ssm_norm_g', 'w_b_out', 'w_o', 'ln1_g', 'ln1_b', 'w_gate_up', 'w_down', 'ln2_g',
            'ln2_b', 'w_ple', 'ple_norm_g', 'w_ple_gate']
_COL_SHARDED = ['w_in', 'conv_a_w', 'ssm_conv_w', 'w_gate_up', 'w_ple']
_ROW_SHARDED = ['w_a_out', 'w_b_out', 'w_o', 'w_down', 'w_ple_gate']
_BIG = _COL_SHARDED + _ROW_SHARDED
_SMALL = [n for n in _WEIGHTS if n not in _BIG]
_CONV = ['conv_a_w', 'ssm_conv_w']


def _ceil_to(n, k):
    return -(-n // k) * k


def kernel(x, p, w_in, conv_a_w, conv_a_b, ln_a_g, ln_a_b, w_a_out, ssm_conv_w, ssm_conv_b, a_log, dt_bias, d_skip, ssm_norm_g, w_b_out, w_o, ln1_g, ln1_b, w_gate_up, w_down, ln2_g, ln2_b, w_ple, ple_norm_g, w_ple_gate, loss_target, m_w_in, m_conv_a_w, m_conv_a_b, m_ln_a_g, m_ln_a_b, m_w_a_out, m_ssm_conv_w, m_ssm_conv_b, m_a_log, m_dt_bias, m_d_skip, m_ssm_norm_g, m_w_b_out, m_w_o, m_ln1_g, m_ln1_b, m_w_gate_up, m_w_down, m_ln2_g, m_ln2_b, m_w_ple, m_ple_norm_g, m_w_ple_gate, v_w_in, v_conv_a_w, v_conv_a_b, v_ln_a_g, v_ln_a_b, v_w_a_out, v_ssm_conv_w, v_ssm_conv_b, v_a_log, v_dt_bias, v_d_skip, v_ssm_norm_g, v_w_b_out, v_w_o, v_ln1_g, v_ln1_b, v_w_gate_up, v_w_down, v_ln2_g, v_ln2_b, v_w_ple, v_ple_norm_g, v_w_ple_gate):
    wt = dict(w_in=w_in, conv_a_w=conv_a_w, conv_a_b=conv_a_b, ln_a_g=ln_a_g, ln_a_b=ln_a_b, w_a_out=w_a_out,
              ssm_conv_w=ssm_conv_w, ssm_conv_b=ssm_conv_b, a_log=a_log, dt_bias=dt_bias, d_skip=d_skip,
              ssm_norm_g=ssm_norm_g, w_b_out=w_b_out, w_o=w_o, ln1_g=ln1_g, ln1_b=ln1_b, w_gate_up=w_gate_up,
              w_down=w_down, ln2_g=ln2_g, ln2_b=ln2_b, w_ple=w_ple, ple_norm_g=ple_norm_g, w_ple_gate=w_ple_gate)
    mo = dict(w_in=m_w_in, conv_a_w=m_conv_a_w, conv_a_b=m_conv_a_b, ln_a_g=m_ln_a_g, ln_a_b=m_ln_a_b,
              w_a_out=m_w_a_out, ssm_conv_w=m_ssm_conv_w, ssm_conv_b=m_ssm_conv_b, a_log=m_a_log,
              dt_bias=m_dt_bias, d_skip=m_d_skip, ssm_norm_g=m_ssm_norm_g, w_b_out=m_w_b_out, w_o=m_w_o,
              ln1_g=m_ln1_g, ln1_b=m_ln1_b, w_gate_up=m_w_gate_up, w_down=m_w_down, ln2_g=m_ln2_g, ln2_b=m_ln2_b,
              w_ple=m_w_ple, ple_norm_g=m_ple_norm_g, w_ple_gate=m_w_ple_gate)
    vo = dict(w_in=v_w_in, conv_a_w=v_conv_a_w, conv_a_b=v_conv_a_b, ln_a_g=v_ln_a_g, ln_a_b=v_ln_a_b,
              w_a_out=v_w_a_out, ssm_conv_w=v_ssm_conv_w, ssm_conv_b=v_ssm_conv_b, a_log=v_a_log,
              dt_bias=v_dt_bias, d_skip=v_d_skip, ssm_norm_g=v_ssm_norm_g, w_b_out=v_w_b_out, w_o=v_w_o,
              ln1_g=v_ln1_g, ln1_b=v_ln1_b, w_gate_up=v_w_gate_up, w_down=v_w_down, ln2_g=v_ln2_g, ln2_b=v_ln2_b,
              w_ple=v_w_ple, ple_norm_g=v_ple_norm_g, w_ple_gate=v_w_ple_gate)

    L = w_in.shape[0]
    S, D = x.shape[1], x.shape[2]
    CD = conv_a_b.shape[1]
    DI = ssm_norm_g.shape[1]
    XBC = ssm_conv_b.shape[1]
    H = d_skip.shape[1]
    G = (XBC - DI) // (2 * D_STATE)
    F = w_down.shape[1] * 4
    N_IN = w_in.shape[2] * 4
    NM = N_IN - 2 * H
    KA, KB = conv_a_w.shape[1], ssm_conv_w.shape[1]
    assert DI == H * HEAD_DIM and CD == D and DI == 2 * D and XBC == 2 * DI and NM == 2 * CD + 2 * D + DI + XBC
    assert 2 * H <= LANES and S % CHUNK == 0
    tnf = F // 2
    cf = dict(S=S, D=D, CD=CD, DI=DI, XBC=XBC, F=F, H=H, G=G, NM=NM, KA=KA, KB=KB, GW=(H // G) * HEAD_DIM,
              alpha=float((2 * L) ** 0.25), tm=min(512, S), tmx=min(1024, S), tmc=min(256, S), tmr=min(256, S), tmw=min(1024, S),
              tn_in=D, tnf=tnf, nk_f=1, nk_in=NM // DI)

    core = lax.axis_index("c").astype(jnp.int32).reshape(1)
    split_names = [n for n in _BIG if n not in _CONV]

    def layer_weights(l, got):
        full = {}
        for n, g in zip(split_names + _CONV, got):
            if n in _COL_SHARDED:
                full[n] = g.transpose(1, 0, 2).reshape(g.shape[1], 4 * g.shape[2])
            else:
                full[n] = g.reshape(4 * g.shape[1], g.shape[2])
        win = full['w_in']
        in_main = win[:, :NM]
        in_dt = _pad_lanes(win[:, NM:])
        gu = full['w_gate_up']
        W = dict(in_main=in_main, in_dt=in_dt, in_main_T=in_main.T, in_dt_T=in_dt.T,
                 a_out=full['w_a_out'], a_out_T=full['w_a_out'].T,
                 b_out=full['w_b_out'], b_out_T=full['w_b_out'].T,
                 o=full['w_o'], o_T=full['w_o'].T, gate_up=gu, gate_T=gu[:, :F].T, up_T=gu[:, F:].T,
                 down=full['w_down'], down_T=full['w_down'].T, ple=full['w_ple'],
                 ple_gate=full['w_ple_gate'], ple_gate_T=full['w_ple_gate'].T)
        row = lambda v: v.reshape(1, -1)
        head_table = lambda v: jnp.broadcast_to(jnp.pad(v, (0, LANES - H))[:, None], (LANES, LANES))
        sm = dict(conv_a_w=jnp.pad(full['conv_a_w'], ((0, _ceil_to(KA, SUBLANES) - KA), (0, 0))),
                  ssm_conv_w=jnp.pad(full['ssm_conv_w'], ((0, _ceil_to(KB, SUBLANES) - KB), (0, 0))),
                  conv_a_b=row(conv_a_b[l]), ln_a_g=row(ln_a_g[l]), ln_a_b=row(ln_a_b[l]),
                  ssm_conv_b=row(ssm_conv_b[l]), ssm_norm_g=row(ssm_norm_g[l]),
                  ln1_g=row(ln1_g[l]), ln1_b=row(ln1_b[l]), ln2_g=row(ln2_g[l]), ln2_b=row(ln2_b[l]),
                  ple_norm_g=row(ple_norm_g[l]),
                  dtb_f=head_table(dt_bias[l, 0]), dtb_r=head_table(dt_bias[l, 1]),
                  alog_f=head_table(a_log[l, 0]), alog_r=head_table(a_log[l, 1]),
                  dskip_full=row(jnp.repeat(d_skip[l], HEAD_DIM)),
                  dskipT=jnp.broadcast_to(jnp.repeat(d_skip[l], HEAD_DIM)[:, None], (DI, LANES)))
        return W, sm

    def blocks(n, gl):
        g = gl[n]
        if n == 'conv_a_w':
            g = g.sum(axis=1)[:KA]
        elif n == 'ssm_conv_w':
            g = g.sum(axis=1)[:KB]
        if n in _COL_SHARDED:
            return g.reshape(g.shape[0], 4, g.shape[1] // 4).transpose(1, 0, 2)
        return g.reshape(4, g.shape[0] // 4, g.shape[1])

    def core_sums(gl):
        mine = [blocks(n, gl) for n in split_names]
        theirs = core_send_half("core_send_half", mine)
        return [core_sum("core_sum_" + n, core, b, t) for n, b, t in zip(split_names, mine, theirs)]

    def chip_sums(l, parts, acc):
        sums = [chip_sum_into("chip_sum_" + n, core, pr, l, L, into=acc.get(n)) for n, pr in zip(split_names, parts)]
        return dict(zip(split_names, core_fill("core_fill", sums, l, L)))

    def shards(l):
        return [wt[n][l].astype(BF16) for n in split_names]

    lw = [None] * L
    pending = None
    for l in range(L):
        if l < L - 1 or L == 1:
            lw[l] = layer_weights(l, gather_layer("gather_weights", shards(l), [wt[n][l] for n in _CONV]))
    xl = x[0]
    if L > 1:
        sh = shards(L - 1)
        send, recv, sh, lands, token = chip_legs_start(
            "gather_start", 'gather', sh, [lax.empty((4,) + a.shape, a.dtype) for a in sh])
        pending = (send, recv, sh, lands)
        xlb = (xl + token[0, 0]).astype(BF16)
    else:
        xlb = xl.astype(BF16)
    saved = []
    for l in range(L):
        if l == L - 1 and pending is not None:
            send, recv, sh, lands = pending
            landed = chip_legs_wait("gather_wait", 'gather', send, recv, sh, lands, xl)
            conv_got = chip_exchange("gather_conv", [[wt[n][l]] for n in _CONV], gather=True)
            lw[l] = layer_weights(l, list(gather_finish("gather_finish", sh, landed)) + list(conv_got))
        xl, xlb, sv = _layer_fwd(cf, xl, xlb, p[l, 0].astype(BF16), lw[l][0], lw[l][1],
                                 target=loss_target[0] if l == L - 1 else None)
        saved.append(sv)
    grads = [None] * L
    dxl = None
    gsum = {}
    pending = None
    for l in reversed(range(L)):
        sm_l = lw[l][1]
        if pending is not None:
            sm_l = dict(sm_l, ple_norm_g=sm_l['ple_norm_g'] + pending[4][0, 0])
        if l == L - 1:
            dxl, grads[l] = _layer_bwd(cf, saved[l], lw[l][0], sm_l)
        else:
            dxl, grads[l] = _layer_bwd(cf, saved[l], lw[l][0], sm_l, dxn=dxl)
        both = core_sums(grads[l])
        if l == L - 1 and L > 1:
            send, recv, both, lands, token = chip_legs_start(
                "scatter_start", 'scatter', both, [lax.empty(a.shape, a.dtype) for a in both])
            pending = (send, recv, both, lands, token)
            continue
        if pending is not None:
            send, recv, sent, lands, _ = pending
            landed = chip_legs_wait("scatter_wait", 'scatter', send, recv, sent, lands, dxl)
            gsum = chip_sums(L - 1, place_own(sent, landed), gsum)
            pending = None
        parts = chip_exchange("scatter_grads", [[t] for t in both], gather=False)
        gsum = chip_sums(l, [pr.reshape(4, pr.shape[2], pr.shape[3]) for pr in parts], gsum)
    loss = lax.psum(0.5 / D * jnp.sum(grads[L - 1]['loss_sq']), ("x", "y", "c"))
    grad_x = dxl[None]

    res = {}
    for n in split_names:
        shp = wt[n].shape
        flat = lambda a: a.reshape(shp[0] * shp[1], shp[2])
        outs = adamw_full("adamw_" + n, gsum[n], flat(wt[n]), flat(mo[n]), flat(vo[n]))
        res[n] = [o.reshape(shp) for o in [gsum[n]] + list(outs)]
    parts = chip_exchange("scatter_conv", [[blocks(n, grads[l]) for l in range(L)] for n in _CONV], gather=False)
    chip_sums = [sum_chips("chip_sum_" + n, pr.reshape(4, L * pr.shape[2], pr.shape[3])) for n, pr in zip(_CONV, parts)]
    sib_sums = sibling_swap("core_swap", chip_sums)
    for n, mine, sib in zip(_CONV, chip_sums, sib_sums):
        shp = wt[n].shape
        flat = lambda a: a.reshape(shp[0] * shp[1], shp[2])
        outs = adamw_shard("adamw_" + n, mine, sib, flat(wt[n]), flat(mo[n]), flat(vo[n]))
        res[n] = [o.reshape(shp) for o in outs]

    def small_pieces(l):
        gl = grads[l]
        A = -jnp.exp(a_log[l])
        d = dict(gl)
        d_alog = jnp.concatenate([gl['dA_f'].sum(axis=1)[:H] * A[0], gl['dA_r'].sum(axis=1)[:H] * A[1]])
        d['a_log'] = jnp.pad(d_alog[None], ((0, SUBLANES - 1), (0, 0)))
        d['dt_bias'] = gl['dt_bias'][:, :2 * H]
        d['d_skip'] = gl['dskip_full'].reshape(SUBLANES, H, HEAD_DIM).sum(axis=-1)
        return [_pad_lanes(d[n], _ceil_to(d[n].shape[1], LANES)) for n in _SMALL]

    widths = [_ceil_to(math.prod(wt[n].shape[1:]), LANES) for n in _SMALL]
    packed = jnp.concatenate([pc for l in range(L) for pc in small_pieces(l)], axis=1)
    gathered = all8_gather("gather_small", fold_rows("fold_small", packed))

    def pack_params(src):
        return jnp.concatenate([_pad_lanes(src[n][l].reshape(1, -1), wd) for l in range(L) for n, wd in zip(_SMALL, widths)],
                               axis=1)

    small_out = adamw_small("adamw_small", gathered, pack_params(wt), pack_params(mo), pack_params(vo))
    off = 0
    per = {n: [[] for _ in range(4)] for n in _SMALL}
    for l in range(L):
        for n, wd in zip(_SMALL, widths):
            size = math.prod(wt[n].shape[1:])
            for k in range(4):
                per[n][k].append(small_out[k][0, off:off + size].reshape(wt[n].shape[1:]))
            off += wd
    for n in _SMALL:
        res[n] = [jnp.stack(per[n][k]) for k in range(4)]

    return (loss, grad_x, *[res[n][0] for n in _WEIGHTS], *[res[n][1] for n in _WEIGHTS],
            *[res[n][2] for n in _WEIGHTS], *[res[n][3] for n in _WEIGHTS])
```
